```python
import jax, jax.numpy as jnp
from jax import lax
import numpy as np

D_MODEL = 2048
BATCH = 8
SEQ = 4096
DEPTH = 4

D_MIX = D_MODEL
HEAD_DIM = 128
D_ATTN = D_MIX // 2
D_GMLP = D_MIX - D_ATTN
N_HEADS = D_ATTN // HEAD_DIM
GMLP_GROUP = 128
N_GMLP_GROUPS = D_GMLP // GMLP_GROUP
CHUNK = 128
DILATED_PATTERNS = ((128, 1), (512, 4), (2048, 16))
BLK = 64
ROPE_THETA = 10000.0
D_FF = -(-(8 * D_MODEL) // (3 * 256)) * 256
D_IN = 3 * D_ATTN + 2 * D_GMLP
EPS = 1e-6
NEG = -1e30

kernel_name = "hybrid_dilated_attn_gmlp_encoder"


def rmsnorm(x, g):
    xf = x.astype(jnp.float32)
    y = xf * lax.rsqrt(jnp.mean(xf * xf, axis=-1, keepdims=True) + EPS)
    return (y * g.astype(jnp.float32)).astype(x.dtype)


def layernorm(x, g):
    xf = x.astype(jnp.float32)
    mu = jnp.mean(xf, axis=-1, keepdims=True)
    xc = xf - mu
    y = xc * lax.rsqrt(jnp.mean(xc * xc, axis=-1, keepdims=True) + EPS)
    return (y * g.astype(jnp.float32)).astype(x.dtype)


def rope_tables(seq):
    pos = jnp.arange(seq, dtype=jnp.float32)
    inv = ROPE_THETA ** (-jnp.arange(0, HEAD_DIM, 2, dtype=jnp.float32) / HEAD_DIM)
    ang = pos[:, None] * inv[None, :]
    return jnp.cos(ang), jnp.sin(ang)


def apply_rope(t, cos, sin):
    tf = t.astype(jnp.float32)
    t1, t2 = jnp.split(tf, 2, axis=-1)
    c = cos[None, :, None, :]
    s = sin[None, :, None, :]
    out = jnp.concatenate([t1 * c - t2 * s, t1 * s + t2 * c], axis=-1)
    return out.astype(t.dtype)


def _band_blocks(t, nb):
    return jnp.concatenate([t[:, :, 0:nb], t[:, :, 1:nb + 1], t[:, :, 2:nb + 2]], axis=3)


def dilated_window_branch(q, k, v, window, dilation):
    B, S, H, Dh = q.shape
    n_side = window // (2 * dilation)
    span = dilation * BLK
    s_pad = -(-S // span) * span
    L = s_pad // dilation
    nb = L // BLK

    def to_sub(t):
        t = jnp.pad(t, ((0, 0), (0, s_pad - S), (0, 0), (0, 0)))
        return t.reshape(B, L, dilation, H, Dh).transpose(0, 2, 1, 3, 4)

    qs, ks, vs = to_sub(q), to_sub(k), to_sub(v)
    valid = (jnp.arange(s_pad) < S).reshape(L, dilation).T

    qb = qs.reshape(B, dilation, nb, BLK, H, Dh)
    pad_kv = ((0, 0), (0, 0), (BLK, BLK), (0, 0), (0, 0))
    kb = _band_blocks(jnp.pad(ks, pad_kv).reshape(B, dilation, nb + 2, BLK, H, Dh), nb)
    vb = _band_blocks(jnp.pad(vs, pad_kv).reshape(B, dilation, nb + 2, BLK, H, Dh), nb)
    vk = jnp.pad(valid, ((0, 0), (BLK, BLK))).reshape(dilation, nb + 2, BLK)
    vk = jnp.concatenate([vk[:, 0:nb], vk[:, 1:nb + 1], vk[:, 2:nb + 2]], axis=2)

    scale = HEAD_DIM ** -0.5
    s = jnp.einsum('brnqhd,brnkhd->brnhqk', qb, kb,
                   preferred_element_type=jnp.float32) * scale
    rel = jnp.arange(3 * BLK)[None, :] - BLK - jnp.arange(BLK)[:, None]
    band = jnp.abs(rel) <= n_side
    mask = band[None, None, None, None] & vk[None, :, :, None, None, :]
    s = jnp.where(mask, s, NEG)
    m = jnp.max(s, axis=-1, keepdims=True)
    p = jnp.exp(s - m)
    denom = jnp.sum(p, axis=-1)
    o = jnp.einsum('brnhqk,brnkhd->brnqhd', p, vb.astype(jnp.float32))
    o = o / jnp.moveaxis(denom, -1, -2)[..., None]
    lse = jnp.moveaxis(m[..., 0] + jnp.log(denom), -1, -2)

    o = o.reshape(B, dilation, L, H, Dh).transpose(0, 2, 1, 3, 4).reshape(B, s_pad, H, Dh)[:, :S]
    lse = lse.reshape(B, dilation, L, H).transpose(0, 2, 1, 3).reshape(B, s_pad, H)[:, :S]
    return o, lse


def dilated_attention(q, k, v):
    outs, lses = [], []
    for window, dilation in DILATED_PATTERNS:
        o, l = dilated_window_branch(q, k, v, window, dilation)
        outs.append(o)
        lses.append(l)
    w = jax.nn.softmax(jnp.stack(lses, axis=0), axis=0)
    o = jnp.sum(w[..., None] * jnp.stack(outs, axis=0), axis=0)
    B, S = q.shape[0], q.shape[1]
    return o.reshape(B, S, D_ATTN).astype(q.dtype)


def chunked_spatial_gating(uv, ln_g, w_s, b_s):
    uv = jax.nn.gelu(uv, approximate=False)
    u, v = jnp.split(uv, 2, axis=-1)
    v = layernorm(v, ln_g)
    B, S, _ = v.shape
    vg = v.reshape(B, S // CHUNK, CHUNK, N_GMLP_GROUPS, GMLP_GROUP)
    mixed = jnp.einsum('gpq,bnqgc->bnpgc', w_s, vg) + b_s.T[None, None, :, :, None]
    return u * mixed.reshape(B, S, D_GMLP)


def _fwd_setup_inputs(seed: int = 0) -> dict:
    key = jax.random.key(seed)
    ks = jax.random.split(key, 14)
    f32 = jnp.float32
    nrm = lambda k, shape, scale: jax.random.normal(k, shape, f32) * scale
    gain = lambda k, shape: 1.0 + 0.02 * jax.random.normal(k, shape, f32)
    return {
        "x": jax.random.normal(ks[0], (BATCH, SEQ, D_MODEL), f32),
        "norm1_g": gain(ks[1], (DEPTH, D_MODEL)),
        "w_in": nrm(ks[2], (DEPTH, D_MODEL, D_IN), D_MODEL ** -0.5),
        "gmlp_ln_g": gain(ks[3], (DEPTH, D_GMLP)),
        "w_spatial": nrm(ks[4], (DEPTH, N_GMLP_GROUPS, CHUNK, CHUNK), CHUNK ** -0.5),
        "b_spatial": gain(ks[5], (DEPTH, N_GMLP_GROUPS, CHUNK)),
        "mix_norm_attn_g": gain(ks[6], (DEPTH, D_ATTN)),
        "mix_norm_gmlp_g": gain(ks[7], (DEPTH, D_GMLP)),
        "w_out": nrm(ks[8], (DEPTH, D_MIX, D_MODEL), D_MIX ** -0.5),
        "norm2_g": gain(ks[9], (DEPTH, D_MODEL)),
        "w_gate": nrm(ks[10], (DEPTH, D_MODEL, D_FF), D_MODEL ** -0.5),
        "w_up": nrm(ks[11], (DEPTH, D_MODEL, D_FF), D_MODEL ** -0.5),
        "w_down": nrm(ks[12], (DEPTH, D_FF, D_MODEL), D_FF ** -0.5),
        "final_g": gain(ks[13], (D_MODEL,)),
    }


def _fwd_reference(x, norm1_g, w_in, gmlp_ln_g, w_spatial, b_spatial, mix_norm_attn_g,
              mix_norm_gmlp_g, w_out, norm2_g, w_gate, w_up, w_down, final_g):
    B, S, _ = x.shape
    cos, sin = rope_tables(S)
    for l in range(DEPTH):
        h = rmsnorm(x, norm1_g[l])
        proj = jnp.einsum('bsd,de->bse', h, w_in[l])
        q = proj[..., 0:D_ATTN].reshape(B, S, N_HEADS, HEAD_DIM)
        k = proj[..., D_ATTN:2 * D_ATTN].reshape(B, S, N_HEADS, HEAD_DIM)
        v = proj[..., 2 * D_ATTN:3 * D_ATTN].reshape(B, S, N_HEADS, HEAD_DIM)
        uv = proj[..., 3 * D_ATTN:]
        q = apply_rope(q, cos, sin)
        k = apply_rope(k, cos, sin)
        a = dilated_attention(q, k, v)
        g = chunked_spatial_gating(uv, gmlp_ln_g[l], w_spatial[l], b_spatial[l])
        mix = jnp.concatenate([rmsnorm(a, mix_norm_attn_g[l]),
                               rmsnorm(g, mix_norm_gmlp_g[l])], axis=-1)
        x = x + jnp.einsum('bse,ed->bsd', mix, w_out[l])
        h = rmsnorm(x, norm2_g[l])
        ff = jax.nn.silu(jnp.einsum('bsd,df->bsf', h, w_gate[l])) * jnp.einsum('bsd,df->bsf', h, w_up[l])
        x = x + jnp.einsum('bsf,fd->bsd', ff, w_down[l])
    return rmsnorm(x, final_g)


import jax as _jax
import jax.numpy as _jnp

TWIN_FORMAT = 'train_step'
FWD_PARAMS = ['x', 'norm1_g', 'w_in', 'gmlp_ln_g', 'w_spatial', 'b_spatial', 'mix_norm_attn_g', 'mix_norm_gmlp_g', 'w_out', 'norm2_g', 'w_gate', 'w_up', 'w_down', 'final_g']
TWIN_WEIGHTS = ['norm1_g', 'w_in', 'gmlp_ln_g', 'w_spatial', 'b_spatial', 'mix_norm_attn_g', 'mix_norm_gmlp_g', 'w_out', 'norm2_g', 'w_gate', 'w_up', 'w_down', 'final_g']
TWIN_DIFF_INPUT = 'x'
TWIN_INPUTS = ['x', 'norm1_g', 'w_in', 'gmlp_ln_g', 'w_spatial', 'b_spatial', 'mix_norm_attn_g', 'mix_norm_gmlp_g', 'w_out', 'norm2_g', 'w_gate', 'w_up', 'w_down', 'final_g', 'loss_target', 'm_norm1_g', 'm_w_in', 'm_gmlp_ln_g', 'm_w_spatial', 'm_b_spatial', 'm_mix_norm_attn_g', 'm_mix_norm_gmlp_g', 'm_w_out', 'm_norm2_g', 'm_w_gate', 'm_w_up', 'm_w_down', 'm_final_g', 'v_norm1_g', 'v_w_in', 'v_gmlp_ln_g', 'v_w_spatial', 'v_b_spatial', 'v_mix_norm_attn_g', 'v_mix_norm_gmlp_g', 'v_w_out', 'v_norm2_g', 'v_w_gate', 'v_w_up', 'v_w_down', 'v_final_g']
TWIN_OUTPUTS = ['loss', 'grad_x', 'grad_norm1_g', 'grad_w_in', 'grad_gmlp_ln_g', 'grad_w_spatial', 'grad_b_spatial', 'grad_mix_norm_attn_g', 'grad_mix_norm_gmlp_g', 'grad_w_out', 'grad_norm2_g', 'grad_w_gate', 'grad_w_up', 'grad_w_down', 'grad_final_g', 'delta_norm1_g', 'delta_w_in', 'delta_gmlp_ln_g', 'delta_w_spatial', 'delta_b_spatial', 'delta_mix_norm_attn_g', 'delta_mix_norm_gmlp_g', 'delta_w_out', 'delta_norm2_g', 'delta_w_gate', 'delta_w_up', 'delta_w_down', 'delta_final_g', 'new_m_norm1_g', 'new_m_w_in', 'new_m_gmlp_ln_g', 'new_m_w_spatial', 'new_m_b_spatial', 'new_m_mix_norm_attn_g', 'new_m_mix_norm_gmlp_g', 'new_m_w_out', 'new_m_norm2_g', 'new_m_w_gate', 'new_m_w_up', 'new_m_w_down', 'new_m_final_g', 'new_v_norm1_g', 'new_v_w_in', 'new_v_gmlp_ln_g', 'new_v_w_spatial', 'new_v_b_spatial', 'new_v_mix_norm_attn_g', 'new_v_mix_norm_gmlp_g', 'new_v_w_out', 'new_v_norm2_g', 'new_v_w_gate', 'new_v_w_up', 'new_v_w_down', 'new_v_final_g']
TWIN_LEAF_KINDS = {'loss': 'loss', 'grad_x': 'grad_x', 'grad_norm1_g': 'grad_w', 'grad_w_in': 'grad_w', 'grad_gmlp_ln_g': 'grad_w', 'grad_w_spatial': 'grad_w', 'grad_b_spatial': 'grad_w', 'grad_mix_norm_attn_g': 'grad_w', 'grad_mix_norm_gmlp_g': 'grad_w', 'grad_w_out': 'grad_w', 'grad_norm2_g': 'grad_w', 'grad_w_gate': 'grad_w', 'grad_w_up': 'grad_w', 'grad_w_down': 'grad_w', 'grad_final_g': 'grad_w', 'delta_norm1_g': 'delta_w', 'delta_w_in': 'delta_w', 'delta_gmlp_ln_g': 'delta_w', 'delta_w_spatial': 'delta_w', 'delta_b_spatial': 'delta_w', 'delta_mix_norm_attn_g': 'delta_w', 'delta_mix_norm_gmlp_g': 'delta_w', 'delta_w_out': 'delta_w', 'delta_norm2_g': 'delta_w', 'delta_w_gate': 'delta_w', 'delta_w_up': 'delta_w', 'delta_w_down': 'delta_w', 'delta_final_g': 'delta_w', 'new_m_norm1_g': 'new_m', 'new_m_w_in': 'new_m', 'new_m_gmlp_ln_g': 'new_m', 'new_m_w_spatial': 'new_m', 'new_m_b_spatial': 'new_m', 'new_m_mix_norm_attn_g': 'new_m', 'new_m_mix_norm_gmlp_g': 'new_m', 'new_m_w_out': 'new_m', 'new_m_norm2_g': 'new_m', 'new_m_w_gate': 'new_m', 'new_m_w_up': 'new_m', 'new_m_w_down': 'new_m', 'new_m_final_g': 'new_m', 'new_v_norm1_g': 'new_v', 'new_v_w_in': 'new_v', 'new_v_gmlp_ln_g': 'new_v', 'new_v_w_spatial': 'new_v', 'new_v_b_spatial': 'new_v', 'new_v_mix_norm_attn_g': 'new_v', 'new_v_mix_norm_gmlp_g': 'new_v', 'new_v_w_out': 'new_v', 'new_v_norm2_g': 'new_v', 'new_v_w_gate': 'new_v', 'new_v_w_up': 'new_v', 'new_v_w_down': 'new_v', 'new_v_final_g': 'new_v'}


def _forward(args):
    return _fwd_reference(*[args[k] for k in FWD_PARAMS])


def _output_shape():
    out = _jax.eval_shape(lambda: _forward(_fwd_setup_inputs(0)))
    return out.shape, out.dtype

N_MICROBATCH = 1
ADAM_LR = 0.001
ADAM_B1 = 0.9
ADAM_B2 = 0.999
ADAM_EPS = 1e-08
ADAM_WD = 0.01
ADAM_STEP = 10
PER_EXAMPLE_BATCH_AXIS = {'x': 0, 'loss_target': 0}
SHARED_INPUTS = []
_WEIGHT_DTYPES = {'norm1_g': _jnp.float32, 'w_in': _jnp.float32, 'gmlp_ln_g': _jnp.float32, 'w_spatial': _jnp.float32, 'b_spatial': _jnp.float32, 'mix_norm_attn_g': _jnp.float32, 'mix_norm_gmlp_g': _jnp.float32, 'w_out': _jnp.float32, 'norm2_g': _jnp.float32, 'w_gate': _jnp.float32, 'w_up': _jnp.float32, 'w_down': _jnp.float32, 'final_g': _jnp.float32}
MOMENT_SCALE = {'norm1_g': 8.781980e-02, 'w_in': 5.379775e-02, 'gmlp_ln_g': 3.898702e-02, 'w_spatial': 3.929669e-02, 'b_spatial': 3.943011e-02, 'mix_norm_attn_g': 7.960396e-02, 'mix_norm_gmlp_g': 7.115145e-02, 'w_out': 7.546829e-02, 'norm2_g': 4.293502e-02, 'w_gate': 1.845511e-02, 'w_up': 1.818966e-02, 'w_down': 3.016481e-02, 'final_g': 1.641727e+01}


def _to_microbatches(a, axis):
    t = _jnp.moveaxis(a, axis, 0)
    t = t.reshape((N_MICROBATCH, t.shape[0] // N_MICROBATCH) + t.shape[1:])
    return _jnp.moveaxis(t, 1, axis + 1)


def setup_inputs(seed: int = 0) -> dict:
    inp = _fwd_setup_inputs(seed)
    key = _jax.random.fold_in(_jax.random.key(seed), 7919)
    shape, _ = _output_shape()
    out = dict(inp)
    out["loss_target"] = _jax.random.normal(_jax.random.fold_in(key, 0), shape, _jnp.float32)
    for i, name in enumerate(TWIN_WEIGHTS):
        w = inp[name].astype(_jnp.float32)
        if MOMENT_SCALE is None:
            s = _jnp.sqrt(_jnp.mean(_jnp.square(w)) + 1e-30)
        else:
            s = MOMENT_SCALE[name]
        km, kv = _jax.random.split(_jax.random.fold_in(key, i + 1))
        out[name] = w
        out["m_" + name] = s * _jax.random.normal(km, w.shape, _jnp.float32)
        out["v_" + name] = (s * s) * _jax.random.uniform(kv, w.shape, _jnp.float32, 0.5, 1.5)
    if N_MICROBATCH > 1:
        for name, axis in PER_EXAMPLE_BATCH_AXIS.items():
            out[name] = _to_microbatches(out[name], axis)
    return {'x': out['x'], 'norm1_g': out['norm1_g'], 'w_in': out['w_in'], 'gmlp_ln_g': out['gmlp_ln_g'], 'w_spatial': out['w_spatial'], 'b_spatial': out['b_spatial'], 'mix_norm_attn_g': out['mix_norm_attn_g'], 'mix_norm_gmlp_g': out['mix_norm_gmlp_g'], 'w_out': out['w_out'], 'norm2_g': out['norm2_g'], 'w_gate': out['w_gate'], 'w_up': out['w_up'], 'w_down': out['w_down'], 'final_g': out['final_g'], 'loss_target': out['loss_target'], 'm_norm1_g': out['m_norm1_g'], 'm_w_in': out['m_w_in'], 'm_gmlp_ln_g': out['m_gmlp_ln_g'], 'm_w_spatial': out['m_w_spatial'], 'm_b_spatial': out['m_b_spatial'], 'm_mix_norm_attn_g': out['m_mix_norm_attn_g'], 'm_mix_norm_gmlp_g': out['m_mix_norm_gmlp_g'], 'm_w_out': out['m_w_out'], 'm_norm2_g': out['m_norm2_g'], 'm_w_gate': out['m_w_gate'], 'm_w_up': out['m_w_up'], 'm_w_down': out['m_w_down'], 'm_final_g': out['m_final_g'], 'v_norm1_g': out['v_norm1_g'], 'v_w_in': out['v_w_in'], 'v_gmlp_ln_g': out['v_gmlp_ln_g'], 'v_w_spatial': out['v_w_spatial'], 'v_b_spatial': out['v_b_spatial'], 'v_mix_norm_attn_g': out['v_mix_norm_attn_g'], 'v_mix_norm_gmlp_g': out['v_mix_norm_gmlp_g'], 'v_w_out': out['v_w_out'], 'v_norm2_g': out['v_norm2_g'], 'v_w_gate': out['v_w_gate'], 'v_w_up': out['v_w_up'], 'v_w_down': out['v_w_down'], 'v_final_g': out['v_final_g']}


def _loss(weights, diff, rest, loss_target):
    with _jax.named_scope("forward"):
        args = {**rest, TWIN_DIFF_INPUT: diff, **{k: w.astype(_WEIGHT_DTYPES[k]) for k, w in weights.items()}}
        y = _forward(args)
    with _jax.named_scope("loss_head"):
        err = _jnp.square(y.astype(_jnp.float32) - loss_target)
        return 0.5 * _jnp.sum(_jnp.mean(err, axis=-1)) if err.ndim else 0.5 * err


def _adamw(w, g, m, v):
    m = ADAM_B1 * m + (1.0 - ADAM_B1) * g
    v = ADAM_B2 * v + (1.0 - ADAM_B2) * _jnp.square(g)
    m_hat = m / (1.0 - ADAM_B1 ** ADAM_STEP)
    v_hat = v / (1.0 - ADAM_B2 ** ADAM_STEP)
    delta = -ADAM_LR * (m_hat / (_jnp.sqrt(v_hat) + ADAM_EPS) + ADAM_WD * w)
    return delta, m, v


def reference(x, norm1_g, w_in, gmlp_ln_g, w_spatial, b_spatial, mix_norm_attn_g, mix_norm_gmlp_g, w_out, norm2_g, w_gate, w_up, w_down, final_g, loss_target, m_norm1_g, m_w_in, m_gmlp_ln_g, m_w_spatial, m_b_spatial, m_mix_norm_attn_g, m_mix_norm_gmlp_g, m_w_out, m_norm2_g, m_w_gate, m_w_up, m_w_down, m_final_g, v_norm1_g, v_w_in, v_gmlp_ln_g, v_w_spatial, v_b_spatial, v_mix_norm_attn_g, v_mix_norm_gmlp_g, v_w_out, v_norm2_g, v_w_gate, v_w_up, v_w_down, v_final_g):
    given = dict(x=x, norm1_g=norm1_g, w_in=w_in, gmlp_ln_g=gmlp_ln_g, w_spatial=w_spatial, b_spatial=b_spatial, mix_norm_attn_g=mix_norm_attn_g, mix_norm_gmlp_g=mix_norm_gmlp_g, w_out=w_out, norm2_g=norm2_g, w_gate=w_gate, w_up=w_up, w_down=w_down, final_g=final_g, loss_target=loss_target, m_norm1_g=m_norm1_g, m_w_in=m_w_in, m_gmlp_ln_g=m_gmlp_ln_g, m_w_spatial=m_w_spatial, m_b_spatial=m_b_spatial, m_mix_norm_attn_g=m_mix_norm_attn_g, m_mix_norm_gmlp_g=m_mix_norm_gmlp_g, m_w_out=m_w_out, m_norm2_g=m_norm2_g, m_w_gate=m_w_gate, m_w_up=m_w_up, m_w_down=m_w_down, m_final_g=m_final_g, v_norm1_g=v_norm1_g, v_w_in=v_w_in, v_gmlp_ln_g=v_gmlp_ln_g, v_w_spatial=v_w_spatial, v_b_spatial=v_b_spatial, v_mix_norm_attn_g=v_mix_norm_attn_g, v_mix_norm_gmlp_g=v_mix_norm_gmlp_g, v_w_out=v_w_out, v_norm2_g=v_norm2_g, v_w_gate=v_w_gate, v_w_up=v_w_up, v_w_down=v_w_down, v_final_g=v_final_g)
    weights = {n: given[n] for n in TWIN_WEIGHTS}
    shared = {n: given[n] for n in SHARED_INPUTS}
    per_example = {n: given[n] for n in ['x']}
    grad_fn = _jax.value_and_grad(_loss, argnums=(0, 1))

    def one_microbatch(ex, loss_target):
        ex = dict(ex)
        diff = ex.pop(TWIN_DIFF_INPUT)
        return grad_fn(weights, diff, {**shared, **ex}, loss_target)

    if N_MICROBATCH == 1:
        loss, (grad_w, grad_x) = one_microbatch(per_example, given["loss_target"])
    else:
        def body(carry, xs):
            loss_sum, grad_sum = carry
            l_k, (gw_k, gx_k) = one_microbatch(xs[0], xs[1])
            with _jax.named_scope("update"):
                return (loss_sum + l_k, _jax.tree.map(_jnp.add, grad_sum, gw_k)), gx_k

        init = (_jnp.zeros((), _jnp.float32), _jax.tree.map(_jnp.zeros_like, weights))
        (loss, grad_w), grad_x = _jax.lax.scan(body, init, (per_example, given["loss_target"]))
    with _jax.named_scope("update"):
        delta_w, new_m, new_v = {}, {}, {}
        for n in TWIN_WEIGHTS:
            delta_w[n], new_m[n], new_v[n] = _adamw(weights[n], grad_w[n], given["m_" + n], given["v_" + n])
    return (loss, grad_x, *[grad_w[n] for n in TWIN_WEIGHTS], *[delta_w[n] for n in TWIN_WEIGHTS],
            *[new_m[n] for n in TWIN_WEIGHTS], *[new_v[n] for n in TWIN_WEIGHTS])
```

```python
import functools

import jax
import jax.numpy as jnp
from jax import lax
from jax.experimental import pallas as pl
from jax.experimental.pallas import tpu as pltpu

F32 = jnp.float32
BF16 = jnp.bfloat16

D = 2048
T = 4096
NL = 4
HD = 128
DA = D // 2
DG = D - DA
NH = DA // HD
NG = DG // 128
CH = 128
DIN = 3 * DA + 2 * DG
FF = 5632
DILATIONS = (1, 4, 16)
NSIDE = 64
ROPE_THETA = 10000.0
EPS = 1e-6
NEG = -1e30
SCALE = HD ** -0.5
NCHIP = 4

LR, B1, B2, AEPS, WD, STEP = 0.001, 0.9, 0.999, 1e-08, 0.01, 10

LANES = 128
VMEM_LIMIT = 56 * 1024 * 1024
MESH = pl.DeviceIdType.MESH


def _call(body, *, name, out_shape, grid=(), in_specs=None, out_specs=None, scratch=(), sem=None, aliases=None):
    params = dict(vmem_limit_bytes=VMEM_LIMIT)
    if sem is not None:
        params["dimension_semantics"] = sem
    kw = {}
    if grid:
        kw["grid"] = grid
    return pl.pallas_call(
        body, name=name, out_shape=out_shape, in_specs=in_specs, out_specs=out_specs,
        scratch_shapes=list(scratch), input_output_aliases=aliases or {},
        compiler_params=pltpu.CompilerParams(**params), **kw)


def _sds(shape, dtype):
    return jax.ShapeDtypeStruct(tuple(shape), dtype)


def _dot(a, b):
    return jnp.dot(a, b, preferred_element_type=F32)


def _dot_nt(a, b):
    return lax.dot_general(a, b, (((1,), (1,)), ((), ())), preferred_element_type=F32)


def _dot_tn(a, b):
    return lax.dot_general(a, b, (((0,), (0,)), ((), ())), preferred_element_type=F32)


def _sigmoid(x):
    return 1.0 / (1.0 + jnp.exp(-x))


def rms_fwd(x, g):
    n, d = x.shape
    tm = 256

    def body(x_ref, g_ref, h_ref):
        xv = x_ref[...]
        r = lax.rsqrt(jnp.mean(xv * xv, axis=-1, keepdims=True) + EPS)
        h_ref[...] = (xv * r * g_ref[...]).astype(BF16)

    return _call(body, name="rms_fwd", out_shape=_sds((n, d), BF16), grid=(n // tm,),
                 in_specs=[pl.BlockSpec((tm, d), lambda i: (i, 0)), pl.BlockSpec((1, d), lambda i: (0, 0))],
                 out_specs=pl.BlockSpec((tm, d), lambda i: (i, 0)), sem=("parallel",))(x, g)


def rms_bwd(dh, x, g, dres):
    n, d = x.shape
    tm = 256

    def body(dh_ref, x_ref, g_ref, dres_ref, dx_ref, dxb_ref, dg_ref):
        @pl.when(pl.program_id(0) == 0)
        def _():
            dg_ref[...] = jnp.zeros_like(dg_ref)

        xv = x_ref[...]
        r = lax.rsqrt(jnp.mean(xv * xv, axis=-1, keepdims=True) + EPS)
        xhat = xv * r
        dhv = dh_ref[...].astype(F32)
        dg_ref[...] += jnp.sum(dhv * xhat, axis=0, keepdims=True)
        dxn = dhv * g_ref[...]
        dx = dres_ref[...] + r * (dxn - xhat * jnp.mean(dxn * xhat, axis=-1, keepdims=True))
        dx_ref[...] = dx
        dxb_ref[...] = dx.astype(BF16)

    row = pl.BlockSpec((tm, d), lambda i: (i, 0))
    vec = pl.BlockSpec((1, d), lambda i: (0, 0))
    return _call(body, name="rms_bwd", out_shape=(_sds((n, d), F32), _sds((n, d), BF16), _sds((1, d), F32)),
                 grid=(n // tm,), in_specs=[row, row, vec, row], out_specs=(row, row, vec), sem=("arbitrary",))(dh, x, g, dres)


def loss_head(x, g, target):
    n, d = x.shape
    tm = 256

    def body(x_ref, g_ref, t_ref, loss_ref, dx_ref, dxb_ref, dg_ref):
        @pl.when(pl.program_id(0) == 0)
        def _():
            dg_ref[...] = jnp.zeros_like(dg_ref)
            loss_ref[...] = jnp.zeros_like(loss_ref)

        xv = x_ref[...]
        r = lax.rsqrt(jnp.mean(xv * xv, axis=-1, keepdims=True) + EPS)
        xhat = xv * r
        gv = g_ref[...]
        err = xhat * gv - t_ref[...]
        loss_ref[...] += 0.5 * jnp.sum(jnp.mean(err * err, axis=-1, keepdims=True))
        dy = err * (1.0 / d)
        dg_ref[...] += jnp.sum(dy * xhat, axis=0, keepdims=True)
        dxn = dy * gv
        dx = r * (dxn - xhat * jnp.mean(dxn * xhat, axis=-1, keepdims=True))
        dx_ref[...] = dx
        dxb_ref[...] = dx.astype(BF16)

    row = pl.BlockSpec((tm, d), lambda i: (i, 0))
    vec = pl.BlockSpec((1, d), lambda i: (0, 0))
    return _call(body, name="loss_head",
                 out_shape=(_sds((8, LANES), F32), _sds((n, d), F32), _sds((n, d), BF16), _sds((1, d), F32)),
                 grid=(n // tm,), in_specs=[row, vec, row],
                 out_specs=(pl.BlockSpec((8, LANES), lambda i: (0, 0)), row, row, vec), sem=("arbitrary",))(x, g, target)


def mm_cols(a, wc, tm=512):
    n, k = a.shape
    s, _, nq = wc.shape

    def body(a_ref, w_ref, o_ref):
        o_ref[...] = _dot(a_ref[...], w_ref[...]).astype(BF16)

    return _call(body, name="mm_cols", out_shape=_sds((n, s * nq), BF16), grid=(s, n // tm),
                 in_specs=[pl.BlockSpec((tm, k), lambda j, i: (i, 0)), pl.BlockSpec((None, k, nq), lambda j, i: (j, 0, 0))],
                 out_specs=pl.BlockSpec((tm, nq), lambda j, i: (i, j)), sem=("parallel", "parallel"))(a, wc)


def mm_gateup(h, wg, wu, tm=512):
    n, k = h.shape
    s, _, nq = wg.shape

    def body(h_ref, wg_ref, wu_ref, gate_ref, up_ref, ff_ref):
        hv = h_ref[...]
        gate = _dot(hv, wg_ref[...])
        up = _dot(hv, wu_ref[...])
        gate_ref[...] = gate.astype(BF16)
        up_ref[...] = up.astype(BF16)
        ff_ref[...] = (gate * _sigmoid(gate) * up).astype(BF16)

    wspec = pl.BlockSpec((None, k, nq), lambda j, i: (j, 0, 0))
    ospec = pl.BlockSpec((tm, nq), lambda j, i: (i, j))
    o = _sds((n, s * nq), BF16)
    return _call(body, name="mm_gateup", out_shape=(o, o, o), grid=(s, n // tm),
                 in_specs=[pl.BlockSpec((tm, k), lambda j, i: (i, 0)), wspec, wspec],
                 out_specs=(ospec, ospec, ospec), sem=("parallel", "parallel"))(h, wg, wu)


def mm_rows_res(a, wr, res, tk, tm=512, tn=1024):
    n, k = a.shape
    _, nn = wr.shape
    nk = k // tk

    def body(a_ref, w_ref, r_ref, o_ref, acc):
        kk = pl.program_id(2)

        @pl.when(kk == 0)
        def _():
            acc[...] = r_ref[...]

        acc[...] += _dot(a_ref[...], w_ref[...])

        @pl.when(kk == nk - 1)
        def _():
            o_ref[...] = acc[...]

    return _call(body, name="mm_rows_res", out_shape=_sds((n, nn), F32), grid=(nn // tn, n // tm, nk),
                 in_specs=[pl.BlockSpec((tm, tk), lambda j, i, kk: (i, kk)), pl.BlockSpec((tk, tn), lambda j, i, kk: (kk, j)),
                           pl.BlockSpec((tm, tn), lambda j, i, kk: (i, j))],
                 out_specs=pl.BlockSpec((tm, tn), lambda j, i, kk: (i, j)),
                 scratch=[pltpu.VMEM((tm, tn), F32)], sem=("parallel", "parallel", "arbitrary"))(a, wr, res)


def mm_nt_rows(dy, wr, tk, tm=512):
    n, nn = dy.shape
    k, _ = wr.shape

    def body(dy_ref, w_ref, o_ref):
        o_ref[...] = _dot_nt(dy_ref[...], w_ref[...]).astype(BF16)

    return _call(body, name="mm_nt_rows", out_shape=_sds((n, k), BF16), grid=(k // tk, n // tm),
                 in_specs=[pl.BlockSpec((tm, nn), lambda j, i: (i, 0)), pl.BlockSpec((tk, nn), lambda j, i: (j, 0))],
                 out_specs=pl.BlockSpec((tm, tk), lambda j, i: (i, j)), sem=("parallel", "parallel"))(dy, wr)


def mm_dff(dy, wd, gate, up, tk, tm=512):
    n, nn = dy.shape
    k, _ = wd.shape

    def body(dy_ref, w_ref, gate_ref, up_ref, dgate_ref, dup_ref):
        dff = _dot_nt(dy_ref[...], w_ref[...])
        gate = gate_ref[...].astype(F32)
        up = up_ref[...].astype(F32)
        sig = _sigmoid(gate)
        dgate_ref[...] = (dff * up * (sig * (1.0 + gate * (1.0 - sig)))).astype(BF16)
        dup_ref[...] = (dff * (gate * sig)).astype(BF16)

    tile = pl.BlockSpec((tm, tk), lambda j, i: (i, j))
    o = _sds((n, k), BF16)
    return _call(body, name="mm_dff", out_shape=(o, o), grid=(k // tk, n // tm),
                 in_specs=[pl.BlockSpec((tm, nn), lambda j, i: (i, 0)), pl.BlockSpec((tk, nn), lambda j, i: (j, 0)), tile, tile],
                 out_specs=(tile, tile), sem=("parallel", "parallel"))(dy, wd, gate, up)


def mm_nt_cols(das, wcs, tm=512):
    n = das[0].shape[0]
    s, k, nq = wcs[0].shape
    npair = len(das)

    def body(*refs):
        da_refs, w_refs, o_ref, acc = refs[:npair], refs[npair:2 * npair], refs[2 * npair], refs[2 * npair + 1]
        ss = pl.program_id(1)

        @pl.when(ss == 0)
        def _():
            acc[...] = jnp.zeros_like(acc)

        for da_ref, w_ref in zip(da_refs, w_refs):
            acc[...] += _dot_nt(da_ref[...], w_ref[...])

        @pl.when(ss == s - 1)
        def _():
            o_ref[...] = acc[...].astype(BF16)

    return _call(body, name="mm_nt_cols%d" % npair, out_shape=_sds((n, k), BF16), grid=(n // tm, s),
                 in_specs=[pl.BlockSpec((tm, nq), lambda i, ss: (i, ss))] * npair
                 + [pl.BlockSpec((None, k, nq), lambda i, ss: (ss, 0, 0))] * npair,
                 out_specs=pl.BlockSpec((tm, k), lambda i, ss: (i, 0)),
                 scratch=[pltpu.VMEM((tm, k), F32)], sem=("parallel", "arbitrary"))(*das, *wcs)


def mm_tn(a, b, s, tka, tt=512):
    n, ka = a.shape
    nq = b.shape[1] // s
    nt = n // tt

    def body(a_ref, b_ref, o_ref, acc):
        t = pl.program_id(2)

        @pl.when(t == 0)
        def _():
            acc[...] = jnp.zeros_like(acc)

        acc[...] += _dot_tn(a_ref[...], b_ref[...])

        @pl.when(t == nt - 1)
        def _():
            o_ref[...] = acc[...].astype(BF16)

    return _call(body, name="mm_tn", out_shape=_sds((s, ka, nq), BF16), grid=(ka // tka, s, nt),
                 in_specs=[pl.BlockSpec((tt, tka), lambda i, j, t: (t, i)), pl.BlockSpec((tt, nq), lambda i, j, t: (t, j))],
                 out_specs=pl.BlockSpec((None, tka, nq), lambda i, j, t: (j, i, 0)),
                 scratch=[pltpu.VMEM((tka, nq), F32)], sem=("parallel", "parallel", "arbitrary"))(a, b)


def rope_tables(n):
    pos = jnp.arange(n, dtype=F32)
    inv = ROPE_THETA ** (-jnp.arange(0, HD, 2, dtype=F32) / HD)
    ang = pos[:, None] * inv[None, :]
    cos, sin = jnp.cos(ang), jnp.sin(ang)
    return jnp.concatenate([cos, cos], axis=-1), jnp.concatenate([-sin, sin], axis=-1)


def rope_fwd(proj, cos2, sin2):
    n = proj.shape[0]
    tm = 256
    nheads = 2 * DA // HD

    def body(p_ref, c_ref, s_ref, o_ref):
        c, s = c_ref[...], s_ref[...]
        for h in range(nheads):
            t = p_ref[:, h * HD:(h + 1) * HD].astype(F32)
            o_ref[:, h * HD:(h + 1) * HD] = (t * c + pltpu.roll(t, HD // 2, 1) * s).astype(BF16)

    tab = pl.BlockSpec((tm, HD), lambda i: (i, 0))
    return _call(body, name="rope_fwd", out_shape=_sds((n, 2 * DA), BF16), grid=(n // tm,),
                 in_specs=[pl.BlockSpec((tm, 2 * DA), lambda i: (i, 0)), tab, tab],
                 out_specs=pl.BlockSpec((tm, 2 * DA), lambda i: (i, 0)), sem=("parallel",))(proj, cos2, sin2)


def dproj_assemble(dqs, dks, dvs, duv, cos2, sin2):
    n = duv.shape[0]
    tm = 256

    def body(*refs):
        dq_refs, dk_refs, dv_refs = refs[0:3], refs[3:6], refs[6:9]
        duv_ref, c_ref, s_ref, o_ref = refs[9:]
        c, s = c_ref[...], s_ref[...]
        for part, part_refs in enumerate((dq_refs, dk_refs)):
            for h in range(NH):
                cols = slice(h * HD, (h + 1) * HD)
                t = part_refs[0][:, cols].astype(F32) + part_refs[1][:, cols].astype(F32) + part_refs[2][:, cols].astype(F32)
                out = t * c - pltpu.roll(t, HD // 2, 1) * s
                o_ref[:, part * DA + h * HD:part * DA + (h + 1) * HD] = out.astype(BF16)
        dv = dv_refs[0][...].astype(F32) + dv_refs[1][...].astype(F32) + dv_refs[2][...].astype(F32)
        o_ref[:, 2 * DA:3 * DA] = dv.astype(BF16)
        o_ref[:, 3 * DA:] = duv_ref[...]

    blk = pl.BlockSpec((tm, DA), lambda i: (i, 0))
    tab = pl.BlockSpec((tm, HD), lambda i: (i, 0))
    return _call(body, name="dproj_assemble", out_shape=_sds((n, DIN), BF16), grid=(n // tm,),
                 in_specs=[blk] * 9 + [pl.BlockSpec((tm, 2 * DG), lambda i: (i, 0)), tab, tab],
                 out_specs=pl.BlockSpec((tm, DIN), lambda i: (i, 0)), sem=("parallel",))(
        *dqs, *dks, *dvs, duv, cos2, sin2)


def _attn_tiles(ls):
    tq = min(128, ls)
    w = min(tq + 2 * NSIDE, ls)
    return tq, w, ls // tq


def _band(t, tq, w, ls):
    q0 = pl.multiple_of(t * tq, tq)
    ks = pl.multiple_of(jnp.clip(t * tq - NSIDE, 0, ls - w), NSIDE)
    qpos = q0 + lax.broadcasted_iota(jnp.int32, (tq, w), 0)
    kpos = ks + lax.broadcasted_iota(jnp.int32, (tq, w), 1)
    return q0, ks, jnp.abs(kpos - qpos) <= NSIDE


def _attn_specs(ls, dil):
    nqk = 2 * DA // HD
    npj = DIN // HD
    voff = 2 * DA // HD
    q = pl.BlockSpec((ls, HD), lambda g: (0, (g // NH) * nqk + g % NH))
    k = pl.BlockSpec((ls, HD), lambda g: (0, (g // NH) * nqk + NH + g % NH))
    v = pl.BlockSpec((ls, HD), lambda g: (0, (g // NH) * npj + voff + g % NH))
    o = pl.BlockSpec((ls, HD), lambda g: (0, g))
    return q, k, v, o


def attn_fwd(qk, proj, dil):
    n = qk.shape[0]
    ls = n // dil
    tq, w, nt = _attn_tiles(ls)

    def body(q_ref, k_ref, v_ref, o_ref, l_ref):
        def tile(t, carry):
            q0, ks, valid = _band(t, tq, w, ls)
            s = _dot_nt(q_ref[pl.ds(q0, tq), :], k_ref[pl.ds(ks, w), :]) * SCALE
            s = jnp.where(valid, s, NEG)
            m = jnp.max(s, axis=1, keepdims=True)
            p = jnp.exp(s - m)
            l = jnp.sum(p, axis=1, keepdims=True)
            o = _dot(p.astype(BF16), v_ref[pl.ds(ks, w), :]) / l
            o_ref[pl.ds(q0, tq), :] = o.astype(BF16)
            l_ref[pl.ds(q0, tq), :] = jnp.broadcast_to(m + jnp.log(l), (tq, HD))
            return carry

        lax.fori_loop(0, nt, tile, 0)

    q, k, v, o = _attn_specs(ls, dil)
    op, lp = _call(body, name="attn_fwd_d%d" % dil, out_shape=(_sds((ls, dil * DA), BF16), _sds((ls, dil * DA), F32)),
                   grid=(dil * NH,), in_specs=[q, k, v], out_specs=(o, o), sem=("parallel",))(
        qk.reshape(ls, dil * 2 * DA), qk.reshape(ls, dil * 2 * DA), proj.reshape(ls, dil * DIN))
    return op.reshape(n, DA), lp.reshape(n, DA)


def attn_merge(os_, ls_, ga):
    n = os_[0].shape[0]
    tm = 256

    def body(o1, o2, o3, l1, l2, l3, g_ref, a_ref, lse_ref, mix_ref):
        lv = [l1[...], l2[...], l3[...]]
        lmax = jnp.maximum(jnp.maximum(lv[0], lv[1]), lv[2])
        e = [jnp.exp(x - lmax) for x in lv]
        den = e[0] + e[1] + e[2]
        a = (e[0] * o1[...].astype(F32) + e[1] * o2[...].astype(F32) + e[2] * o3[...].astype(F32)) / den
        a_ref[...] = a.astype(BF16)
        lse_ref[...] = lmax + jnp.log(den)
        r = lax.rsqrt(jnp.mean(a * a, axis=-1, keepdims=True) + EPS)
        mix_ref[...] = (a * r * g_ref[...]).astype(BF16)

    blk = pl.BlockSpec((tm, DA), lambda i: (i, 0))
    return _call(body, name="attn_merge", out_shape=(_sds((n, DA), BF16), _sds((n, DA), F32), _sds((n, D), BF16)),
                 grid=(n // tm,), in_specs=[blk] * 6 + [pl.BlockSpec((1, DA), lambda i: (0, 0))],
                 out_specs=(blk, blk, blk), sem=("parallel",))(*os_, *ls_, ga)


def attn_norm_bwd(dmix, a, ga):
    n = a.shape[0]
    tm = 256

    def body(dm_ref, a_ref, g_ref, da_ref, dg_ref):
        @pl.when(pl.program_id(0) == 0)
        def _():
            dg_ref[...] = jnp.zeros_like(dg_ref)

        av = a_ref[...].astype(F32)
        r = lax.rsqrt(jnp.mean(av * av, axis=-1, keepdims=True) + EPS)
        ahat = av * r
        dm = dm_ref[...].astype(F32)
        dg_ref[...] += jnp.sum(dm * ahat, axis=0, keepdims=True)
        dn = dm * g_ref[...]
        da_ref[...] = (r * (dn - ahat * jnp.mean(dn * ahat, axis=-1, keepdims=True))).astype(BF16)

    blk = pl.BlockSpec((tm, DA), lambda i: (i, 0))
    vec = pl.BlockSpec((1, DA), lambda i: (0, 0))
    return _call(body, name="attn_norm_bwd", out_shape=(_sds((n, DA), BF16), _sds((1, DA), F32)), grid=(n // tm,),
                 in_specs=[blk, blk, vec], out_specs=(blk, vec), sem=("arbitrary",))(dmix, a, ga)


def attn_bwd(qk, proj, a, da, lse, dil):
    n = qk.shape[0]
    ls = n // dil
    tq, w, nt = _attn_tiles(ls)

    def body(q_ref, k_ref, v_ref, a_ref, da_ref, lse_ref, dq_ref, dk_ref, dv_ref, dk_acc, dv_acc):
        dk_acc[...] = jnp.zeros_like(dk_acc)
        dv_acc[...] = jnp.zeros_like(dv_acc)

        def tile(t, carry):
            q0, ks, valid = _band(t, tq, w, ls)
            q = q_ref[pl.ds(q0, tq), :]
            k = k_ref[pl.ds(ks, w), :]
            v = v_ref[pl.ds(ks, w), :]
            do = da_ref[pl.ds(q0, tq), :]
            s = jnp.where(valid, _dot_nt(q, k) * SCALE, NEG)
            p = jnp.exp(s - lse_ref[pl.ds(q0, tq), 0:1])
            drow = jnp.sum(do.astype(F32) * a_ref[pl.ds(q0, tq), :].astype(F32), axis=1, keepdims=True)
            ds = (p * (_dot_nt(do, v) - drow) * SCALE).astype(BF16)
            dv_acc[pl.ds(ks, w), :] += _dot_tn(p.astype(BF16), do)
            dk_acc[pl.ds(ks, w), :] += _dot_tn(ds, q)
            dq_ref[pl.ds(q0, tq), :] = _dot(ds, k).astype(BF16)
            return carry

        lax.fori_loop(0, nt, tile, 0)
        dk_ref[...] = dk_acc[...].astype(BF16)
        dv_ref[...] = dv_acc[...].astype(BF16)

    q, k, v, o = _attn_specs(ls, dil)
    out = _sds((ls, dil * DA), BF16)
    dq, dk, dv = _call(body, name="attn_bwd_d%d" % dil, out_shape=(out, out, out), grid=(dil * NH,),
                       in_specs=[q, k, v, o, o, o], out_specs=(o, o, o),
                       scratch=[pltpu.VMEM((ls, HD), F32), pltpu.VMEM((ls, HD), F32)], sem=("parallel",))(
        qk.reshape(ls, dil * 2 * DA), qk.reshape(ls, dil * 2 * DA), proj.reshape(ls, dil * DIN),
        a.reshape(ls, dil * DA), da.reshape(ls, dil * DA), lse.reshape(ls, dil * DA))
    return dq.reshape(n, DA), dk.reshape(n, DA), dv.reshape(n, DA)


GM_TM = 256
INV_SQRT2 = 0.7071067811865476
INV_SQRT2PI = 0.3989422804014327


def _gelu(x):
    return 0.5 * x * (1.0 + lax.erf(x * INV_SQRT2))


def _gelu_grad(x):
    return 0.5 * (1.0 + lax.erf(x * INV_SQRT2)) + x * (INV_SQRT2PI * jnp.exp(-0.5 * x * x))


def _gmlp_forward(up, vp, ln_g, ws_ref, bias):
    u = _gelu(up)
    v = _gelu(vp)
    vc = v - jnp.mean(v, axis=-1, keepdims=True)
    rs = lax.rsqrt(jnp.mean(vc * vc, axis=-1, keepdims=True) + EPS)
    vhat = vc * rs
    vln = (vhat * ln_g).astype(BF16)
    rows = []
    for c in range(GM_TM // CH):
        cols = [_dot(ws_ref[g], vln[c * CH:(c + 1) * CH, g * 128:(g + 1) * 128]) for g in range(NG)]
        rows.append(jnp.concatenate(cols, axis=1) + bias)
    return u, vhat, rs, vln, jnp.concatenate(rows, axis=0)


def _gmlp_specs():
    ublk = pl.BlockSpec((GM_TM, DG), lambda i: (i, 3 * DA // DG))
    vblk = pl.BlockSpec((GM_TM, DG), lambda i: (i, 3 * DA // DG + 1))
    vec = pl.BlockSpec((1, DG), lambda i: (0, 0))
    wsp = pl.BlockSpec((NG, CH, CH), lambda i: (0, 0, 0))
    bsp = pl.BlockSpec((CH, DG), lambda i: (0, 0))
    return ublk, vblk, vec, wsp, bsp


def gmlp_fwd(proj, ln_g, ws, bias, gg, mix):
    n = proj.shape[0]

    def body(up_ref, vp_ref, ln_ref, ws_ref, b_ref, gg_ref, mix_in, mix_ref):
        del mix_in
        u, _, _, _, mixed = _gmlp_forward(up_ref[...].astype(F32), vp_ref[...].astype(F32), ln_ref[...], ws_ref, b_ref[...])
        gout = u * mixed
        r = lax.rsqrt(jnp.mean(gout * gout, axis=-1, keepdims=True) + EPS)
        mix_ref[...] = (gout * r * gg_ref[...]).astype(BF16)

    ublk, vblk, vec, wsp, bsp = _gmlp_specs()
    return _call(body, name="gmlp_fwd", out_shape=_sds((n, D), BF16), grid=(n // GM_TM,),
                 in_specs=[ublk, vblk, vec, wsp, bsp, vec, pl.BlockSpec(memory_space=pl.ANY)],
                 out_specs=pl.BlockSpec((GM_TM, DG), lambda i: (i, DA // DG)), sem=("parallel",), aliases={6: 0})(
        proj, proj, ln_g, ws, bias, gg, mix)


def gmlp_bwd(proj, dmix, ln_g, ws, wst, bias, gg):
    n = proj.shape[0]

    def body(up_ref, vp_ref, dm_ref, ln_ref, ws_ref, wst_ref, b_ref, gg_ref, duv_ref, dln_ref, dws_ref, dbs_ref, dgg_ref,
             db_ref):
        @pl.when(pl.program_id(0) == 0)
        def _():
            dln_ref[...] = jnp.zeros_like(dln_ref)
            dws_ref[...] = jnp.zeros_like(dws_ref)
            db_ref[...] = jnp.zeros_like(db_ref)
            dgg_ref[...] = jnp.zeros_like(dgg_ref)

        up = up_ref[...].astype(F32)
        vp = vp_ref[...].astype(F32)
        ln_g = ln_ref[...]
        u, vhat, rs, vln, mixed = _gmlp_forward(up, vp, ln_g, ws_ref, b_ref[...])
        gout = u * mixed
        r = lax.rsqrt(jnp.mean(gout * gout, axis=-1, keepdims=True) + EPS)
        ghat = gout * r
        dm = dm_ref[...].astype(F32)
        dgg_ref[...] += jnp.sum(dm * ghat, axis=0, keepdims=True)
        dn = dm * gg_ref[...]
        dgout = r * (dn - ghat * jnp.mean(dn * ghat, axis=-1, keepdims=True))
        du = dgout * mixed
        dmixed = dgout * u
        dmb = dmixed.astype(BF16)
        rows = []
        for c in range(GM_TM // CH):
            rsl = slice(c * CH, (c + 1) * CH)
            db_ref[...] += dmixed[rsl, :]
            cols = []
            for g in range(NG):
                csl = slice(g * 128, (g + 1) * 128)
                dws_ref[g] += _dot_nt(dmb[rsl, csl], vln[rsl, csl])
                cols.append(_dot(wst_ref[g], dmb[rsl, csl]))
            rows.append(jnp.concatenate(cols, axis=1))
        dvln = jnp.concatenate(rows, axis=0)
        dln_ref[...] += jnp.sum(dvln * vhat, axis=0, keepdims=True)
        dvh = dvln * ln_g
        dv = rs * (dvh - jnp.mean(dvh, axis=-1, keepdims=True) - vhat * jnp.mean(dvh * vhat, axis=-1, keepdims=True))
        duv_ref[:, 0:DG] = (du * _gelu_grad(up)).astype(BF16)
        duv_ref[:, DG:] = (dv * _gelu_grad(vp)).astype(BF16)

        @pl.when(pl.program_id(0) == pl.num_programs(0) - 1)
        def _():
            for g in range(NG):
                dbs_ref[g:g + 1, :] = jnp.sum(jnp.transpose(db_ref[:, g * 128:(g + 1) * 128]), axis=0, keepdims=True)

    ublk, vblk, vec, wsp, bsp = _gmlp_specs()
    return _call(body, name="gmlp_bwd",
                 out_shape=(_sds((n, 2 * DG), BF16), _sds((1, DG), F32), _sds((NG, CH, CH), F32), _sds((NG, CH), F32),
                            _sds((1, DG), F32)),
                 grid=(n // GM_TM,),
                 in_specs=[ublk, vblk, pl.BlockSpec((GM_TM, DG), lambda i: (i, DA // DG)), vec, wsp, wsp, bsp, vec],
                 out_specs=(pl.BlockSpec((GM_TM, 2 * DG), lambda i: (i, 0)), vec, wsp,
                            pl.BlockSpec((NG, CH), lambda i: (0, 0)), vec),
                 scratch=[pltpu.VMEM((CH, DG), F32)], sem=("arbitrary",))(
        proj, proj, dmix, ln_g, ws, wst, bias, gg)


def _row(v):
    return v.reshape(1, -1)


def local_step(x, target, small, big):
    cos2, sin2 = rope_tables(T)
    tk_d = FF // NCHIP
    saved = []
    for l in range(NL):
        w = big[l]
        ws_b = small["w_spatial"][l].astype(BF16)
        bias = jnp.repeat(small["b_spatial"][l].T, 128, axis=1)
        h = rms_fwd(x, _row(small["norm1_g"][l]))
        proj = mm_cols(h, w["win"])
        qk = rope_fwd(proj, cos2, sin2)
        outs = [attn_fwd(qk, proj, dil) for dil in DILATIONS]
        a, lse, mix = attn_merge([o for o, _ in outs], [s for _, s in outs], _row(small["mix_norm_attn_g"][l]))
        mix = gmlp_fwd(proj, _row(small["gmlp_ln_g"][l]), ws_b, bias, _row(small["mix_norm_gmlp_g"][l]), mix)
        x_mid = mm_rows_res(mix, w["wout"], x, tk=D)
        h2 = rms_fwd(x_mid, _row(small["norm2_g"][l]))
        gate, up, ff = mm_gateup(h2, w["wg"], w["wu"])
        x_out = mm_rows_res(ff, w["wd"], x_mid, tk=tk_d)
        saved.append(dict(x=x, h=h, proj=proj, qk=qk, a=a, lse=lse, mix=mix, x_mid=x_mid, h2=h2, gate=gate, up=up, ff=ff,
                          ws_b=ws_b, bias=bias))
        x = x_out

    loss8, dx, dxb, dfinal = loss_head(x, _row(small["final_g"]), target)
    gs = {k: [None] * NL for k in ("norm1_g", "gmlp_ln_g", "w_spatial", "b_spatial", "mix_norm_attn_g", "mix_norm_gmlp_g",
                                   "norm2_g")}
    gbig = [None] * NL
    for l in reversed(range(NL)):
        w, sv = big[l], saved[l]
        dgate, dup = mm_dff(dxb, w["wd"], sv["gate"], sv["up"], tk=tk_d)
        g_wd = mm_tn(sv["ff"], dxb, 1, tka=tk_d)
        dh2 = mm_nt_cols([dgate, dup], [w["wg"], w["wu"]])
        g_wg = mm_tn(sv["h2"], dgate, NCHIP, tka=D)
        g_wu = mm_tn(sv["h2"], dup, NCHIP, tka=D)
        dx, dxb, gs["norm2_g"][l] = rms_bwd(dh2, sv["x_mid"], _row(small["norm2_g"][l]), dx)
        dmix = mm_nt_rows(dxb, w["wout"], tk=D // 2)
        g_wout = mm_tn(sv["mix"], dxb, 1, tka=D // 2)
        da, gs["mix_norm_attn_g"][l] = attn_norm_bwd(dmix, sv["a"], _row(small["mix_norm_attn_g"][l]))
        parts = [attn_bwd(sv["qk"], sv["proj"], sv["a"], da, sv["lse"], dil) for dil in DILATIONS]
        wst = jnp.swapaxes(small["w_spatial"][l], 1, 2).astype(BF16)
        duv, gs["gmlp_ln_g"][l], gs["w_spatial"][l], dbias, gs["mix_norm_gmlp_g"][l] = gmlp_bwd(
            sv["proj"], dmix, _row(small["gmlp_ln_g"][l]), sv["ws_b"], wst, sv["bias"], _row(small["mix_norm_gmlp_g"][l]))
        gs["b_spatial"][l] = dbias
        dproj = dproj_assemble([p[0] for p in parts], [p[1] for p in parts], [p[2] for p in parts], duv, cos2, sin2)
        dh = mm_nt_cols([dproj], [w["win"]])
        g_win = mm_tn(sv["h"], dproj, NCHIP, tka=D)
        dx, dxb, gs["norm1_g"][l] = rms_bwd(dh, sv["x"], _row(small["norm1_g"][l]), dx)
        gbig[l] = dict(win=g_win, wout=g_wout.reshape(NCHIP, D // NCHIP, D), wg=g_wg, wu=g_wu,
                       wd=g_wd.reshape(NCHIP, FF // NCHIP, D))
    return loss8, dx, gs, dfinal, gbig


KINDS = ("win", "wout", "wg", "wu", "wd")
ANY = pl.BlockSpec(memory_space=pl.ANY)


def _place():
    return lax.axis_index("x"), lax.axis_index("y"), lax.axis_index("c")


def _other_chips(x, y):
    return [(1 - x, y), (x, 1 - y), (1 - x, 1 - y)]


def _remote(src, dst, send_sem, recv_sem, to):
    return pltpu.make_async_remote_copy(src_ref=src, dst_ref=dst, send_sem=send_sem, recv_sem=recv_sem,
                                        device_id=to, device_id_type=MESH)


def _hbm_call(body, *, name, n_in, out_shape, scratch):
    return pl.pallas_call(body, name=name, out_shape=tuple(out_shape), in_specs=[ANY] * n_in,
                          out_specs=tuple(ANY for _ in out_shape), scratch_shapes=list(scratch))


def gather_weights(shards):
    nk = len(shards)

    def body(*refs):
        ins, outs = refs[:nk], refs[nk:2 * nk]
        send, recv, loc = refs[2 * nk:]
        x, y, c = _place()
        me, sib = 2 * x + y, (x, y, 1 - c)
        chips = _other_chips(x, y)

        def half(k, chip, which):
            hr = ins[k].shape[0] // 2
            return outs[k].at[chip, pl.ds(which * hr, hr), :]

        local = [pltpu.make_async_copy(ins[k], outs[k].at[me], loc.at[k]) for k in range(nk)]
        for cp in local:
            cp.start()
        first = []
        for k in range(nk):
            hr = ins[k].shape[0] // 2
            for j, (cx, cy) in enumerate(chips):
                first.append(_remote(ins[k].at[pl.ds(c * hr, hr), :], half(k, me, c), send.at[k, j], recv.at[k, j], (cx, cy, c)))
        for cp in first:
            cp.start()
        passed = []
        for j, (cx, cy) in enumerate(chips):
            for k in range(nk):
                blk = half(k, 2 * cx + cy, c)
                _remote(blk, blk, send.at[k, j], recv.at[k, j], (cx, cy, c)).wait_recv()
                fw = _remote(blk, blk, send.at[k, 3 + j], recv.at[k, 3 + j], sib)
                fw.start()
                passed.append(fw)
        for j, (cx, cy) in enumerate(chips):
            for k in range(nk):
                blk = half(k, 2 * cx + cy, 1 - c)
                _remote(blk, blk, send.at[k, 3 + j], recv.at[k, 3 + j], sib).wait_recv()
        for cp in first + passed:
            cp.wait_send()
        for cp in local:
            cp.wait()

    return _hbm_call(body, name="gather_weights", n_in=nk,
                     out_shape=[_sds((NCHIP,) + s.shape, s.dtype) for s in shards],
                     scratch=[pltpu.SemaphoreType.DMA((nk, 6)), pltpu.SemaphoreType.DMA((nk, 6)),
                              pltpu.SemaphoreType.DMA((nk,))])(*shards)


def rs_pair_exchange(gs):
    nk = len(gs)

    def body(*refs):
        ins, own, got = refs[:nk], refs[nk:2 * nk], refs[2 * nk:3 * nk]
        send, recv, loc = refs[3 * nk:]
        x, y, c = _place()
        cps = []
        for k in range(nk):
            hr = ins[k].shape[1] // 2
            cps.append(pltpu.make_async_copy(ins[k].at[:, pl.ds(c * hr, hr), :], own[k], loc.at[k]))
            cps.append(_remote(ins[k].at[:, pl.ds((1 - c) * hr, hr), :], got[k], send.at[k], recv.at[k], (x, y, 1 - c)))
        for cp in cps:
            cp.start()
        for cp in cps:
            cp.wait()

    half = [_sds((g.shape[0], g.shape[1] // 2, g.shape[2]), g.dtype) for g in gs]
    res = _hbm_call(body, name="rs_pair_exchange", n_in=nk, out_shape=half + half,
                    scratch=[pltpu.SemaphoreType.DMA((nk,))] * 3)(*gs)
    return res[:nk], res[nk:]


def rs_chip_exchange(ps):
    nk = len(ps)

    def body(*refs):
        ins, own, got = refs[:nk], refs[nk:2 * nk], refs[2 * nk:3 * nk]
        send, recv, loc = refs[3 * nk:]
        x, y, c = _place()
        cps = []
        for k in range(nk):
            cps.append(pltpu.make_async_copy(ins[k].at[2 * x + y], own[k], loc.at[k]))
            for j, (cx, cy) in enumerate(_other_chips(x, y)):
                cps.append(_remote(ins[k].at[2 * cx + cy], got[k].at[j], send.at[k, j], recv.at[k, j], (cx, cy, c)))
        for cp in cps:
            cp.start()
        for cp in cps:
            cp.wait()

    res = _hbm_call(body, name="rs_chip_exchange", n_in=nk,
                    out_shape=[_sds(p.shape[1:], p.dtype) for p in ps] + [_sds((3,) + p.shape[1:], p.dtype) for p in ps],
                    scratch=[pltpu.SemaphoreType.DMA((nk, 3)), pltpu.SemaphoreType.DMA((nk, 3)),
                             pltpu.SemaphoreType.DMA((nk,))])(*ps)
    return res[:nk], res[nk:]


def rs_pair_share(ghs):
    nk = len(ghs)

    def body(*refs):
        ins, outs = refs[:nk], refs[nk:2 * nk]
        send, recv, loc = refs[2 * nk:]
        x, y, c = _place()
        cps = []
        for k in range(nk):
            cps.append(pltpu.make_async_copy(ins[k], outs[k].at[c], loc.at[k]))
            cps.append(_remote(ins[k], outs[k].at[c], send.at[k], recv.at[k], (x, y, 1 - c)))
        for cp in cps:
            cp.start()
        for cp in cps:
            cp.wait()

    return _hbm_call(body, name="rs_pair_share", n_in=nk, out_shape=[_sds((2,) + g.shape, g.dtype) for g in ghs],
                     scratch=[pltpu.SemaphoreType.DMA((nk,))] * 3)(*ghs)


N_DEV = 8


def allreduce_small(buf):
    rows = buf.shape[0]
    rp = rows // N_DEV

    def body(in_ref, out_ref, land_ref, send, recv):
        x, y, c = _place()
        mine = pl.ds(pl.multiple_of((4 * x + 2 * y + c) * rp, 8), rp)
        peers = [(x ^ fx, y ^ fy, c ^ fc) for fx in (0, 1) for fy in (0, 1) for fc in (0, 1)][1:]
        block = [pl.ds(pl.multiple_of((4 * px + 2 * py + pc) * rp, 8), rp) for px, py, pc in peers]
        scatter = [_remote(in_ref.at[block[r], :], land_ref.at[r], send.at[r], recv.at[r], peers[r]) for r in range(7)]
        for cp in scatter:
            cp.start()
        for cp in scatter:
            cp.wait()
        acc = in_ref[mine, :]
        for r in range(7):
            acc = acc + land_ref[r]
        out_ref[mine, :] = acc
        spread = [_remote(out_ref.at[mine, :], out_ref.at[mine, :], send.at[7 + r], recv.at[7 + r], peers[r]) for r in range(7)]
        for cp in spread:
            cp.start()
        for r in range(7):
            _remote(out_ref.at[block[r], :], out_ref.at[block[r], :], send.at[7 + r], recv.at[7 + r], peers[r]).wait_recv()
        for cp in spread:
            cp.wait_send()

    vm = pl.BlockSpec(memory_space=pltpu.VMEM)
    return pl.pallas_call(body, name="allreduce_small", out_shape=_sds(buf.shape, F32), in_specs=[vm], out_specs=vm,
                          scratch_shapes=[pltpu.VMEM((7, rp, LANES), F32), pltpu.SemaphoreType.DMA((14,)),
                                          pltpu.SemaphoreType.DMA((14,))])(buf)


def _row_tile(rows, cap=512):
    return max(t for t in range(16, cap + 1, 16) if rows % t == 0)


def add_pair(a, b):
    shape = a.shape
    cols = shape[-1]
    rows = a.size // cols
    tr = _row_tile(rows)

    def body(a_ref, b_ref, o_ref):
        o_ref[...] = (a_ref[...].astype(F32) + b_ref[...].astype(F32)).astype(BF16)

    blk = pl.BlockSpec((tr, cols), lambda i: (i, 0))
    return _call(body, name="add_pair", out_shape=_sds((rows, cols), BF16), grid=(rows // tr,),
                 in_specs=[blk, blk], out_specs=blk, sem=("parallel",))(a.reshape(rows, cols), b.reshape(rows, cols)).reshape(shape)


def add_slots(own, got):
    rows, cols = own.shape
    tr = _row_tile(rows)

    def body(o_ref, g0, g1, g2, out_ref):
        out_ref[...] = ((o_ref[...].astype(F32) + g0[...].astype(F32)) + g1[...].astype(F32)) + g2[...].astype(F32)

    blk = pl.BlockSpec((tr, cols), lambda i: (i, 0))
    slots = [pl.BlockSpec((None, tr, cols), functools.partial(lambda j, i: (j, i, 0), j)) for j in range(3)]
    return _call(body, name="add_slots", out_shape=_sds((rows, cols), F32), grid=(rows // tr,),
                 in_specs=[blk] + slots, out_specs=blk, sem=("parallel",))(own, got, got, got)


def adamw(w, g, m, v):
    shape = w.shape
    cols = shape[-1]
    rows = w.size // cols
    tr = _row_tile(rows, 256)

    def body(w_ref, g_ref, m_ref, v_ref, d_ref, nm_ref, nv_ref):
        gv = g_ref[...]
        nm = B1 * m_ref[...] + (1.0 - B1) * gv
        nv = B2 * v_ref[...] + (1.0 - B2) * (gv * gv)
        m_hat = nm / (1.0 - B1 ** STEP)
        v_hat = nv / (1.0 - B2 ** STEP)
        d_ref[...] = -LR * (m_hat / (jnp.sqrt(v_hat) + AEPS) + WD * w_ref[...])
        nm_ref[...] = nm
        nv_ref[...] = nv

    blk = pl.BlockSpec((tr, cols), lambda i: (i, 0))
    o = _sds((rows, cols), F32)
    outs = _call(body, name="adamw", out_shape=(o, o, o), grid=(rows // tr,), in_specs=[blk] * 4, out_specs=(blk, blk, blk),
                 sem=("parallel",))(*[t.reshape(rows, cols) for t in (w, g, m, v)])
    return [t.reshape(shape) for t in outs]


SMALL = ("norm1_g", "gmlp_ln_g", "w_spatial", "b_spatial", "mix_norm_attn_g", "mix_norm_gmlp_g", "norm2_g", "final_g")
WEIGHTS = ("norm1_g", "w_in", "gmlp_ln_g", "w_spatial", "b_spatial", "mix_norm_attn_g", "mix_norm_gmlp_g", "w_out",
           "norm2_g", "w_gate", "w_up", "w_down", "final_g")
BIG = dict(win="w_in", wout="w_out", wg="w_gate", wu="w_up", wd="w_down")


def _pack(parts):
    flat = jnp.concatenate([parts[n].reshape(-1) for n in SMALL])
    rows = -(-flat.shape[0] // (LANES * 8 * N_DEV)) * 8 * N_DEV
    return jnp.pad(flat, (0, rows * LANES - flat.shape[0])).reshape(rows, LANES)


def _unpack(buf, like):
    flat = buf.reshape(-1)
    out, at = {}, 0
    for n in SMALL:
        out[n] = flat[at:at + like[n].size].reshape(like[n].shape)
        at += like[n].size
    return out


def kernel(x, norm1_g, w_in, gmlp_ln_g, w_spatial, b_spatial, mix_norm_attn_g, mix_norm_gmlp_g, w_out, norm2_g, w_gate, w_up, w_down, final_g, loss_target, m_norm1_g, m_w_in, m_gmlp_ln_g, m_w_spatial, m_b_spatial, m_mix_norm_attn_g, m_mix_norm_gmlp_g, m_w_out, m_norm2_g, m_w_gate, m_w_up, m_w_down, m_final_g, v_norm1_g, v_w_in, v_gmlp_ln_g, v_w_spatial, v_b_spatial, v_mix_norm_attn_g, v_mix_norm_gmlp_g, v_w_out, v_norm2_g, v_w_gate, v_w_up, v_w_down, v_final_g):
    w = dict(norm1_g=norm1_g, w_in=w_in, gmlp_ln_g=gmlp_ln_g, w_spatial=w_spatial, b_spatial=b_spatial,
             mix_norm_attn_g=mix_norm_attn_g, mix_norm_gmlp_g=mix_norm_gmlp_g, w_out=w_out, norm2_g=norm2_g,
             w_gate=w_gate, w_up=w_up, w_down=w_down, final_g=final_g)
    m = dict(norm1_g=m_norm1_g, w_in=m_w_in, gmlp_ln_g=m_gmlp_ln_g, w_spatial=m_w_spatial, b_spatial=m_b_spatial,
             mix_norm_attn_g=m_mix_norm_attn_g, mix_norm_gmlp_g=m_mix_norm_gmlp_g, w_out=m_w_out, norm2_g=m_norm2_g,
             w_gate=m_w_gate, w_up=m_w_up, w_down=m_w_down, final_g=m_final_g)
    v = dict(norm1_g=v_norm1_g, w_in=v_w_in, gmlp_ln_g=v_gmlp_ln_g, w_spatial=v_w_spatial, b_spatial=v_b_spatial,
             mix_norm_attn_g=v_mix_norm_attn_g, mix_norm_gmlp_g=v_mix_norm_gmlp_g, w_out=v_w_out, norm2_g=v_norm2_g,
             w_gate=v_w_gate, w_up=v_w_up, w_down=v_w_down, final_g=v_final_g)

    big = []
    for l in range(NL):
        full = gather_weights([w[BIG[k]][l].astype(BF16) for k in KINDS])
        big.append(dict(zip(KINDS, full)))
        big[l]["wout"] = big[l]["wout"].reshape(D, D)
        big[l]["wd"] = big[l]["wd"].reshape(FF, D)

    small = {n: w[n] for n in SMALL}
    loss8, dx, gs, dfinal, gbig = local_step(x[0], loss_target[0], small, big)
    loss = lax.psum(loss8[0, 0], ("x", "y", "c"))

    grads = {}
    reduced = {k: [] for k in KINDS}
    for l in range(NL):
        own, got = rs_pair_exchange([gbig[l][k] for k in KINDS])
        pair = [add_pair(a, b) for a, b in zip(own, got)]
        own, got = rs_chip_exchange(pair)
        halves = [add_slots(a, b) for a, b in zip(own, got)]
        for k, full in zip(KINDS, rs_pair_share(halves)):
            reduced[k].append(full.reshape(w[BIG[k]].shape[1:]))
    for k in KINDS:
        grads[BIG[k]] = jnp.stack(reduced[k])
    part = {n: jnp.stack(gs[n]).reshape(w[n].shape) for n in SMALL[:-1]}
    part["final_g"] = dfinal.reshape(w["final_g"].shape)
    grads.update(_unpack(allreduce_small(_pack(part)), part))

    delta, new_m, new_v = {}, {}, {}
    for n in BIG.values():
        delta[n], new_m[n], new_v[n] = adamw(w[n], grads[n], m[n], v[n])
    sm = adamw(_pack({n: w[n] for n in SMALL}), _pack({n: grads[n] for n in SMALL}), _pack({n: m[n] for n in SMALL}),
               _pack({n: v[n] for n in SMALL}))
    for res, packed in zip((delta, new_m, new_v), sm):
        res.update(_unpack(packed, part))

    return (loss, dx.reshape(x.shape), *[grads[n] for n in WEIGHTS], *[delta[n] for n in WEIGHTS],
            *[new_m[n] for n in WEIGHTS], *[new_v[n] for n in WEIGHTS])
```

```python
import functools

import jax
import jax.numpy as jnp
from jax import lax
from jax.experimental import pallas as pl
from jax.experimental.pallas import tpu as pltpu

F32 = jnp.float32
BF16 = jnp.bfloat16

D = 2048
T = 4096
NL = 4
HD = 128
DA = D // 2
DG = D - DA
NH = DA // HD
NG = DG // 128
CH = 128
DIN = 3 * DA + 2 * DG
FF = 5632
DILATIONS = (1, 4, 16)
NSIDE = 64
ROPE_THETA = 10000.0
EPS = 1e-6
NEG = -1e30
SCALE = HD ** -0.5
NCHIP = 4

LR, B1, B2, AEPS, WD, STEP = 0.001, 0.9, 0.999, 1e-08, 0.01, 10

LANES = 128
VMEM_LIMIT = 56 * 1024 * 1024
MESH = pl.DeviceIdType.MESH


def _call(body, *, name, out_shape, grid=(), in_specs=None, out_specs=None, scratch=(), sem=None, aliases=None,
          prefetch=0):
    params = dict(vmem_limit_bytes=VMEM_LIMIT)
    if sem is not None:
        params["dimension_semantics"] = sem
    if prefetch:
        spec = pltpu.PrefetchScalarGridSpec(num_scalar_prefetch=prefetch, grid=grid, in_specs=in_specs,
                                            out_specs=out_specs, scratch_shapes=list(scratch))
        return pl.pallas_call(body, name=name, out_shape=out_shape, grid_spec=spec,
                              input_output_aliases=aliases or {}, compiler_params=pltpu.CompilerParams(**params))
    kw = {}
    if grid:
        kw["grid"] = grid
    return pl.pallas_call(
        body, name=name, out_shape=out_shape, in_specs=in_specs, out_specs=out_specs,
        scratch_shapes=list(scratch), input_output_aliases=aliases or {},
        compiler_params=pltpu.CompilerParams(**params), **kw)


def _sds(shape, dtype):
    return jax.ShapeDtypeStruct(tuple(shape), dtype)


def _dot(a, b):
    return jnp.dot(a, b, preferred_element_type=F32)


def _dot_nt(a, b):
    return lax.dot_general(a, b, (((1,), (1,)), ((), ())), preferred_element_type=F32)


def _dot_tn(a, b):
    return lax.dot_general(a, b, (((0,), (0,)), ((), ())), preferred_element_type=F32)


def _sigmoid(x):
    return 1.0 / (1.0 + jnp.exp(-x))


def rms_fwd(x, g):
    n, d = x.shape
    tm = 256

    def body(x_ref, g_ref, h_ref):
        xv = x_ref[...]
        r = lax.rsqrt(jnp.mean(xv * xv, axis=-1, keepdims=True) + EPS)
        h_ref[...] = (xv * r * g_ref[...]).astype(BF16)

    return _call(body, name="rms_fwd", out_shape=_sds((n, d), BF16), grid=(n // tm,),
                 in_specs=[pl.BlockSpec((tm, d), lambda i: (i, 0)), pl.BlockSpec((1, d), lambda i: (0, 0))],
                 out_specs=pl.BlockSpec((tm, d), lambda i: (i, 0)), sem=("parallel",))(x, g)


def rms_bwd(dh, x, g, dres):
    n, d = x.shape
    tm = 256

    def body(dh_ref, x_ref, g_ref, dres_ref, dx_ref, dxb_ref, dg_ref):
        @pl.when(pl.program_id(0) == 0)
        def _():
            dg_ref[...] = jnp.zeros_like(dg_ref)

        xv = x_ref[...]
        r = lax.rsqrt(jnp.mean(xv * xv, axis=-1, keepdims=True) + EPS)
        xhat = xv * r
        dhv = dh_ref[...].astype(F32)
        dg_ref[...] += jnp.sum(dhv * xhat, axis=0, keepdims=True)
        dxn = dhv * g_ref[...]
        dx = dres_ref[...] + r * (dxn - xhat * jnp.mean(dxn * xhat, axis=-1, keepdims=True))
        dx_ref[...] = dx
        dxb_ref[...] = dx.astype(BF16)

    row = pl.BlockSpec((tm, d), lambda i: (i, 0))
    vec = pl.BlockSpec((1, d), lambda i: (0, 0))
    return _call(body, name="rms_bwd", out_shape=(_sds((n, d), F32), _sds((n, d), BF16), _sds((1, d), F32)),
                 grid=(n // tm,), in_specs=[row, row, vec, row], out_specs=(row, row, vec), sem=("arbitrary",))(dh, x, g, dres)


def loss_head(x, g, target):
    n, d = x.shape
    tm = 256

    def body(x_ref, g_ref, t_ref, loss_ref, dx_ref, dxb_ref, dg_ref):
        @pl.when(pl.program_id(0) == 0)
        def _():
            dg_ref[...] = jnp.zeros_like(dg_ref)
            loss_ref[...] = jnp.zeros_like(loss_ref)

        xv = x_ref[...]
        r = lax.rsqrt(jnp.mean(xv * xv, axis=-1, keepdims=True) + EPS)
        xhat = xv * r
        gv = g_ref[...]
        err = xhat * gv - t_ref[...]
        loss_ref[...] += 0.5 * jnp.sum(jnp.mean(err * err, axis=-1, keepdims=True))
        dy = err * (1.0 / d)
        dg_ref[...] += jnp.sum(dy * xhat, axis=0, keepdims=True)
        dxn = dy * gv
        dx = r * (dxn - xhat * jnp.mean(dxn * xhat, axis=-1, keepdims=True))
        dx_ref[...] = dx
        dxb_ref[...] = dx.astype(BF16)

    row = pl.BlockSpec((tm, d), lambda i: (i, 0))
    vec = pl.BlockSpec((1, d), lambda i: (0, 0))
    return _call(body, name="loss_head",
                 out_shape=(_sds((8, LANES), F32), _sds((n, d), F32), _sds((n, d), BF16), _sds((1, d), F32)),
                 grid=(n // tm,), in_specs=[row, vec, row],
                 out_specs=(pl.BlockSpec((8, LANES), lambda i: (0, 0)), row, row, vec), sem=("arbitrary",))(x, g, target)


def mm_cols(a, wc, tm=512):
    n, k = a.shape
    s, _, nq = wc.shape

    def body(a_ref, w_ref, o_ref):
        o_ref[...] = _dot(a_ref[...], w_ref[...]).astype(BF16)

    return _call(body, name="mm_cols", out_shape=_sds((n, s * nq), BF16), grid=(s, n // tm),
                 in_specs=[pl.BlockSpec((tm, k), lambda j, i: (i, 0)), pl.BlockSpec((None, k, nq), lambda j, i: (j, 0, 0))],
                 out_specs=pl.BlockSpec((tm, nq), lambda j, i: (i, j)), sem=("parallel", "parallel"))(a, wc)


def mm_gateup(h, wg, wu, tm=512):
    n, k = h.shape
    s, _, nq = wg.shape

    def body(h_ref, wg_ref, wu_ref, gate_ref, up_ref, ff_ref):
        hv = h_ref[...]
        gate = _dot(hv, wg_ref[...])
        up = _dot(hv, wu_ref[...])
        gate_ref[...] = gate.astype(BF16)
        up_ref[...] = up.astype(BF16)
        ff_ref[...] = (gate * _sigmoid(gate) * up).astype(BF16)

    wspec = pl.BlockSpec((None, k, nq), lambda j, i: (j, 0, 0))
    ospec = pl.BlockSpec((tm, nq), lambda j, i: (i, j))
    o = _sds((n, s * nq), BF16)
    return _call(body, name="mm_gateup", out_shape=(o, o, o), grid=(s, n // tm),
                 in_specs=[pl.BlockSpec((tm, k), lambda j, i: (i, 0)), wspec, wspec],
                 out_specs=(ospec, ospec, ospec), sem=("parallel", "parallel"))(h, wg, wu)


def mm_rows_res(a, wr, res, tk, tm=512, tn=1024):
    n, k = a.shape
    _, nn = wr.shape
    nk = k // tk

    def body(a_ref, w_ref, r_ref, o_ref, acc):
        kk = pl.program_id(2)

        @pl.when(kk == 0)
        def _():
            acc[...] = r_ref[...]

        acc[...] += _dot(a_ref[...], w_ref[...])

        @pl.when(kk == nk - 1)
        def _():
            o_ref[...] = acc[...]

    return _call(body, name="mm_rows_res", out_shape=_sds((n, nn), F32), grid=(nn // tn, n // tm, nk),
                 in_specs=[pl.BlockSpec((tm, tk), lambda j, i, kk: (i, kk)), pl.BlockSpec((tk, tn), lambda j, i, kk: (kk, j)),
                           pl.BlockSpec((tm, tn), lambda j, i, kk: (i, j))],
                 out_specs=pl.BlockSpec((tm, tn), lambda j, i, kk: (i, j)),
                 scratch=[pltpu.VMEM((tm, tn), F32)], sem=("parallel", "parallel", "arbitrary"))(a, wr, res)


def mm_nt_rows(dy, wr, tk, tm=512):
    n, nn = dy.shape
    k, _ = wr.shape

    def body(dy_ref, w_ref, o_ref):
        o_ref[...] = _dot_nt(dy_ref[...], w_ref[...]).astype(BF16)

    return _call(body, name="mm_nt_rows", out_shape=_sds((n, k), BF16), grid=(k // tk, n // tm),
                 in_specs=[pl.BlockSpec((tm, nn), lambda j, i: (i, 0)), pl.BlockSpec((tk, nn), lambda j, i: (j, 0))],
                 out_specs=pl.BlockSpec((tm, tk), lambda j, i: (i, j)), sem=("parallel", "parallel"))(dy, wr)


def mm_dff(dy, wd, gate, up, tk, tm=512):
    n, nn = dy.shape
    k, _ = wd.shape

    def body(dy_ref, w_ref, gate_ref, up_ref, dgate_ref, dup_ref):
        dff = _dot_nt(dy_ref[...], w_ref[...])
        gate = gate_ref[...].astype(F32)
        up = up_ref[...].astype(F32)
        sig = _sigmoid(gate)
        dgate_ref[...] = (dff * up * (sig * (1.0 + gate * (1.0 - sig)))).astype(BF16)
        dup_ref[...] = (dff * (gate * sig)).astype(BF16)

    tile = pl.BlockSpec((tm, tk), lambda j, i: (i, j))
    o = _sds((n, k), BF16)
    return _call(body, name="mm_dff", out_shape=(o, o), grid=(k // tk, n // tm),
                 in_specs=[pl.BlockSpec((tm, nn), lambda j, i: (i, 0)), pl.BlockSpec((tk, nn), lambda j, i: (j, 0)), tile, tile],
                 out_specs=(tile, tile), sem=("parallel", "parallel"))(dy, wd, gate, up)


def mm_nt_cols(das, wcs, tm=512):
    n = das[0].shape[0]
    s, k, nq = wcs[0].shape
    npair = len(das)

    def body(*refs):
        da_refs, w_refs, o_ref, acc = refs[:npair], refs[npair:2 * npair], refs[2 * npair], refs[2 * npair + 1]
        ss = pl.program_id(1)

        @pl.when(ss == 0)
        def _():
            acc[...] = jnp.zeros_like(acc)

        for da_ref, w_ref in zip(da_refs, w_refs):
            acc[...] += _dot_nt(da_ref[...], w_ref[...])

        @pl.when(ss == s - 1)
        def _():
            o_ref[...] = acc[...].astype(BF16)

    return _call(body, name="mm_nt_cols%d" % npair, out_shape=_sds((n, k), BF16), grid=(n // tm, s),
                 in_specs=[pl.BlockSpec((tm, nq), lambda i, ss: (i, ss))] * npair
                 + [pl.BlockSpec((None, k, nq), lambda i, ss: (ss, 0, 0))] * npair,
                 out_specs=pl.BlockSpec((tm, k), lambda i, ss: (i, 0)),
                 scratch=[pltpu.VMEM((tm, k), F32)], sem=("parallel", "arbitrary"))(*das, *wcs)


def mm_tn(a, b, s, tka, tt=512):
    n, ka = a.shape
    nq = b.shape[1] // s
    nt = n // tt

    def body(a_ref, b_ref, o_ref, acc):
        t = pl.program_id(2)

        @pl.when(t == 0)
        def _():
            acc[...] = jnp.zeros_like(acc)

        acc[...] += _dot_tn(a_ref[...], b_ref[...])

        @pl.when(t == nt - 1)
        def _():
            o_ref[...] = acc[...].astype(BF16)

    return _call(body, name="mm_tn", out_shape=_sds((s, ka, nq), BF16), grid=(ka // tka, s, nt),
                 in_specs=[pl.BlockSpec((tt, tka), lambda i, j, t: (t, i)), pl.BlockSpec((tt, nq), lambda i, j, t: (t, j))],
                 out_specs=pl.BlockSpec((None, tka, nq), lambda i, j, t: (j, i, 0)),
                 scratch=[pltpu.VMEM((tka, nq), F32)], sem=("parallel", "parallel", "arbitrary"))(a, b)


def rope_tables(n):
    pos = jnp.arange(n, dtype=F32)
    inv = ROPE_THETA ** (-jnp.arange(0, HD, 2, dtype=F32) / HD)
    ang = pos[:, None] * inv[None, :]
    cos, sin = jnp.cos(ang), jnp.sin(ang)
    return jnp.concatenate([cos, cos], axis=-1), jnp.concatenate([-sin, sin], axis=-1)


def rope_fwd(proj, cos2, sin2):
    n = proj.shape[0]
    tm = 256
    nheads = 2 * DA // HD

    def body(p_ref, c_ref, s_ref, o_ref):
        c, s = c_ref[...], s_ref[...]
        for h in range(nheads):
            t = p_ref[:, h * HD:(h + 1) * HD].astype(F32)
            o_ref[:, h * HD:(h + 1) * HD] = (t * c + pltpu.roll(t, HD // 2, 1) * s).astype(BF16)

    tab = pl.BlockSpec((tm, HD), lambda i: (i, 0))
    return _call(body, name="rope_fwd", out_shape=_sds((n, 2 * DA), BF16), grid=(n // tm,),
                 in_specs=[pl.BlockSpec((tm, 2 * DA), lambda i: (i, 0)), tab, tab],
                 out_specs=pl.BlockSpec((tm, 2 * DA), lambda i: (i, 0)), sem=("parallel",))(proj, cos2, sin2)


def dproj_assemble(dqs, dks, dvs, duv, cos2, sin2):
    n = duv.shape[0]
    tm = 256

    def body(*refs):
        dq_refs, dk_refs, dv_refs = refs[0:3], refs[3:6], refs[6:9]
        duv_ref, c_ref, s_ref, o_ref = refs[9:]
        c, s = c_ref[...], s_ref[...]
        for part, part_refs in enumerate((dq_refs, dk_refs)):
            for h in range(NH):
                cols = slice(h * HD, (h + 1) * HD)
                t = part_refs[0][:, cols].astype(F32) + part_refs[1][:, cols].astype(F32) + part_refs[2][:, cols].astype(F32)
                out = t * c - pltpu.roll(t, HD // 2, 1) * s
                o_ref[:, part * DA + h * HD:part * DA + (h + 1) * HD] = out.astype(BF16)
        dv = dv_refs[0][...].astype(F32) + dv_refs[1][...].astype(F32) + dv_refs[2][...].astype(F32)
        o_ref[:, 2 * DA:3 * DA] = dv.astype(BF16)
        o_ref[:, 3 * DA:] = duv_ref[...]

    blk = pl.BlockSpec((tm, DA), lambda i: (i, 0))
    tab = pl.BlockSpec((tm, HD), lambda i: (i, 0))
    return _call(body, name="dproj_assemble", out_shape=_sds((n, DIN), BF16), grid=(n // tm,),
                 in_specs=[blk] * 9 + [pl.BlockSpec((tm, 2 * DG), lambda i: (i, 0)), tab, tab],
                 out_specs=pl.BlockSpec((tm, DIN), lambda i: (i, 0)), sem=("parallel",))(
        *dqs, *dks, *dvs, duv, cos2, sin2)


def _attn_tiles(ls):
    tq = min(128, ls)
    w = min(tq + 2 * NSIDE, ls)
    return tq, w, ls // tq


def _band(t, tq, w, ls):
    q0 = pl.multiple_of(t * tq, tq)
    ks = pl.multiple_of(jnp.clip(t * tq - NSIDE, 0, ls - w), NSIDE)
    qpos = q0 + lax.broadcasted_iota(jnp.int32, (tq, w), 0)
    kpos = ks + lax.broadcasted_iota(jnp.int32, (tq, w), 1)
    return q0, ks, jnp.abs(kpos - qpos) <= NSIDE


def _attn_specs(ls, dil):
    nqk = 2 * DA // HD
    npj = DIN // HD
    voff = 2 * DA // HD
    q = pl.BlockSpec((ls, HD), lambda g: (0, (g // NH) * nqk + g % NH))
    k = pl.BlockSpec((ls, HD), lambda g: (0, (g // NH) * nqk + NH + g % NH))
    v = pl.BlockSpec((ls, HD), lambda g: (0, (g // NH) * npj + voff + g % NH))
    o = pl.BlockSpec((ls, HD), lambda g: (0, g))
    return q, k, v, o


def attn_fwd(qk, proj, dil):
    n = qk.shape[0]
    ls = n // dil
    tq, w, nt = _attn_tiles(ls)

    def body(q_ref, k_ref, v_ref, o_ref, l_ref):
        def tile(t, carry):
            q0, ks, valid = _band(t, tq, w, ls)
            s = _dot_nt(q_ref[pl.ds(q0, tq), :], k_ref[pl.ds(ks, w), :]) * SCALE
            s = jnp.where(valid, s, NEG)
            m = jnp.max(s, axis=1, keepdims=True)
            p = jnp.exp(s - m)
            l = jnp.sum(p, axis=1, keepdims=True)
            o = _dot(p.astype(BF16), v_ref[pl.ds(ks, w), :]) / l
            o_ref[pl.ds(q0, tq), :] = o.astype(BF16)
            l_ref[pl.ds(q0, tq), :] = jnp.broadcast_to(m + jnp.log(l), (tq, HD))
            return carry

        lax.fori_loop(0, nt, tile, 0)

    q, k, v, o = _attn_specs(ls, dil)
    op, lp = _call(body, name="attn_fwd_d%d" % dil, out_shape=(_sds((ls, dil * DA), BF16), _sds((ls, dil * DA), F32)),
                   grid=(dil * NH,), in_specs=[q, k, v], out_specs=(o, o), sem=("parallel",))(
        qk.reshape(ls, dil * 2 * DA), qk.reshape(ls, dil * 2 * DA), proj.reshape(ls, dil * DIN))
    return op.reshape(n, DA), lp.reshape(n, DA)


def attn_merge(os_, ls_, ga):
    n = os_[0].shape[0]
    tm = 256

    def body(o1, o2, o3, l1, l2, l3, g_ref, a_ref, lse_ref, mix_ref):
        lv = [l1[...], l2[...], l3[...]]
        lmax = jnp.maximum(jnp.maximum(lv[0], lv[1]), lv[2])
        e = [jnp.exp(x - lmax) for x in lv]
        den = e[0] + e[1] + e[2]
        a = (e[0] * o1[...].astype(F32) + e[1] * o2[...].astype(F32) + e[2] * o3[...].astype(F32)) / den
        a_ref[...] = a.astype(BF16)
        lse_ref[...] = lmax + jnp.log(den)
        r = lax.rsqrt(jnp.mean(a * a, axis=-1, keepdims=True) + EPS)
        mix_ref[...] = (a * r * g_ref[...]).astype(BF16)

    blk = pl.BlockSpec((tm, DA), lambda i: (i, 0))
    return _call(body, name="attn_merge", out_shape=(_sds((n, DA), BF16), _sds((n, DA), F32), _sds((n, D), BF16)),
                 grid=(n // tm,), in_specs=[blk] * 6 + [pl.BlockSpec((1, DA), lambda i: (0, 0))],
                 out_specs=(blk, blk, blk), sem=("parallel",))(*os_, *ls_, ga)


def attn_norm_bwd(dmix, a, ga):
    n = a.shape[0]
    tm = 256

    def body(dm_ref, a_ref, g_ref, da_ref, dg_ref):
        @pl.when(pl.program_id(0) == 0)
        def _():
            dg_ref[...] = jnp.zeros_like(dg_ref)

        av = a_ref[...].astype(F32)
        r = lax.rsqrt(jnp.mean(av * av, axis=-1, keepdims=True) + EPS)
        ahat = av * r
        dm = dm_ref[...].astype(F32)
        dg_ref[...] += jnp.sum(dm * ahat, axis=0, keepdims=True)
        dn = dm * g_ref[...]
        da_ref[...] = (r * (dn - ahat * jnp.mean(dn * ahat, axis=-1, keepdims=True))).astype(BF16)

    blk = pl.BlockSpec((tm, DA), lambda i: (i, 0))
    vec = pl.BlockSpec((1, DA), lambda i: (0, 0))
    return _call(body, name="attn_norm_bwd", out_shape=(_sds((n, DA), BF16), _sds((1, DA), F32)), grid=(n // tm,),
                 in_specs=[blk, blk, vec], out_specs=(blk, vec), sem=("arbitrary",))(dmix, a, ga)


def attn_bwd(qk, proj, a, da, lse, dil):
    n = qk.shape[0]
    ls = n // dil
    tq, w, nt = _attn_tiles(ls)

    def body(q_ref, k_ref, v_ref, a_ref, da_ref, lse_ref, dq_ref, dk_ref, dv_ref, dk_acc, dv_acc):
        dk_acc[...] = jnp.zeros_like(dk_acc)
        dv_acc[...] = jnp.zeros_like(dv_acc)

        def tile(t, carry):
            q0, ks, valid = _band(t, tq, w, ls)
            q = q_ref[pl.ds(q0, tq), :]
            k = k_ref[pl.ds(ks, w), :]
            v = v_ref[pl.ds(ks, w), :]
            do = da_ref[pl.ds(q0, tq), :]
            s = jnp.where(valid, _dot_nt(q, k) * SCALE, NEG)
            p = jnp.exp(s - lse_ref[pl.ds(q0, tq), 0:1])
            drow = jnp.sum(do.astype(F32) * a_ref[pl.ds(q0, tq), :].astype(F32), axis=1, keepdims=True)
            ds = (p * (_dot_nt(do, v) - drow) * SCALE).astype(BF16)
            dv_acc[pl.ds(ks, w), :] += _dot_tn(p.astype(BF16), do)
            dk_acc[pl.ds(ks, w), :] += _dot_tn(ds, q)
            dq_ref[pl.ds(q0, tq), :] = _dot(ds, k).astype(BF16)
            return carry

        lax.fori_loop(0, nt, tile, 0)
        dk_ref[...] = dk_acc[...].astype(BF16)
        dv_ref[...] = dv_acc[...].astype(BF16)

    q, k, v, o = _attn_specs(ls, dil)
    out = _sds((ls, dil * DA), BF16)
    dq, dk, dv = _call(body, name="attn_bwd_d%d" % dil, out_shape=(out, out, out), grid=(dil * NH,),
                       in_specs=[q, k, v, o, o, o], out_specs=(o, o, o),
                       scratch=[pltpu.VMEM((ls, HD), F32), pltpu.VMEM((ls, HD), F32)], sem=("parallel",))(
        qk.reshape(ls, dil * 2 * DA), qk.reshape(ls, dil * 2 * DA), proj.reshape(ls, dil * DIN),
        a.reshape(ls, dil * DA), da.reshape(ls, dil * DA), lse.reshape(ls, dil * DA))
    return dq.reshape(n, DA), dk.reshape(n, DA), dv.reshape(n, DA)


GM_TM = 256
INV_SQRT2 = 0.7071067811865476
INV_SQRT2PI = 0.3989422804014327


def _gelu(x):
    return 0.5 * x * (1.0 + lax.erf(x * INV_SQRT2))


def _gelu_grad(x):
    return 0.5 * (1.0 + lax.erf(x * INV_SQRT2)) + x * (INV_SQRT2PI * jnp.exp(-0.5 * x * x))


def _gmlp_forward(up, vp, ln_g, ws_ref, bias):
    u = _gelu(up)
    v = _gelu(vp)
    vc = v - jnp.mean(v, axis=-1, keepdims=True)
    rs = lax.rsqrt(jnp.mean(vc * vc, axis=-1, keepdims=True) + EPS)
    vhat = vc * rs
    vln = (vhat * ln_g).astype(BF16)
    rows = []
    for c in range(GM_TM // CH):
        cols = [_dot(ws_ref[g], vln[c * CH:(c + 1) * CH, g * 128:(g + 1) * 128]) for g in range(NG)]
        rows.append(jnp.concatenate(cols, axis=1) + bias)
    return u, vhat, rs, vln, jnp.concatenate(rows, axis=0)


def _gmlp_specs():
    ublk = pl.BlockSpec((GM_TM, DG), lambda i: (i, 3 * DA // DG))
    vblk = pl.BlockSpec((GM_TM, DG), lambda i: (i, 3 * DA // DG + 1))
    vec = pl.BlockSpec((1, DG), lambda i: (0, 0))
    wsp = pl.BlockSpec((NG, CH, CH), lambda i: (0, 0, 0))
    bsp = pl.BlockSpec((CH, DG), lambda i: (0, 0))
    return ublk, vblk, vec, wsp, bsp


def gmlp_fwd(proj, ln_g, ws, bias, gg, mix):
    n = proj.shape[0]

    def body(up_ref, vp_ref, ln_ref, ws_ref, b_ref, gg_ref, mix_in, mix_ref):
        del mix_in
        u, _, _, _, mixed = _gmlp_forward(up_ref[...].astype(F32), vp_ref[...].astype(F32), ln_ref[...], ws_ref, b_ref[...])
        gout = u * mixed
        r = lax.rsqrt(jnp.mean(gout * gout, axis=-1, keepdims=True) + EPS)
        mix_ref[...] = (gout * r * gg_ref[...]).astype(BF16)

    ublk, vblk, vec, wsp, bsp = _gmlp_specs()
    return _call(body, name="gmlp_fwd", out_shape=_sds((n, D), BF16), grid=(n // GM_TM,),
                 in_specs=[ublk, vblk, vec, wsp, bsp, vec, pl.BlockSpec(memory_space=pl.ANY)],
                 out_specs=pl.BlockSpec((GM_TM, DG), lambda i: (i, DA // DG)), sem=("parallel",), aliases={6: 0})(
        proj, proj, ln_g, ws, bias, gg, mix)


def gmlp_bwd(proj, dmix, ln_g, ws, wst, bias, gg):
    n = proj.shape[0]

    def body(up_ref, vp_ref, dm_ref, ln_ref, ws_ref, wst_ref, b_ref, gg_ref, duv_ref, dln_ref, dws_ref, dbs_ref, dgg_ref,
             db_ref):
        @pl.when(pl.program_id(0) == 0)
        def _():
            dln_ref[...] = jnp.zeros_like(dln_ref)
            dws_ref[...] = jnp.zeros_like(dws_ref)
            db_ref[...] = jnp.zeros_like(db_ref)
            dgg_ref[...] = jnp.zeros_like(dgg_ref)

        up = up_ref[...].astype(F32)
        vp = vp_ref[...].astype(F32)
        ln_g = ln_ref[...]
        u, vhat, rs, vln, mixed = _gmlp_forward(up, vp, ln_g, ws_ref, b_ref[...])
        gout = u * mixed
        r = lax.rsqrt(jnp.mean(gout * gout, axis=-1, keepdims=True) + EPS)
        ghat = gout * r
        dm = dm_ref[...].astype(F32)
        dgg_ref[...] += jnp.sum(dm * ghat, axis=0, keepdims=True)
        dn = dm * gg_ref[...]
        dgout = r * (dn - ghat * jnp.mean(dn * ghat, axis=-1, keepdims=True))
        du = dgout * mixed
        dmixed = dgout * u
        dmb = dmixed.astype(BF16)
        rows = []
        for c in range(GM_TM // CH):
            rsl = slice(c * CH, (c + 1) * CH)
            db_ref[...] += dmixed[rsl, :]
            cols = []
            for g in range(NG):
                csl = slice(g * 128, (g + 1) * 128)
                dws_ref[g] += _dot_nt(dmb[rsl, csl], vln[rsl, csl])
                cols.append(_dot(wst_ref[g], dmb[rsl, csl]))
            rows.append(jnp.concatenate(cols, axis=1))
        dvln = jnp.concatenate(rows, axis=0)
        dln_ref[...] += jnp.sum(dvln * vhat, axis=0, keepdims=True)
        dvh = dvln * ln_g
        dv = rs * (dvh - jnp.mean(dvh, axis=-1, keepdims=True) - vhat * jnp.mean(dvh * vhat, axis=-1, keepdims=True))
        duv_ref[:, 0:DG] = (du * _gelu_grad(up)).astype(BF16)
        duv_ref[:, DG:] = (dv * _gelu_grad(vp)).astype(BF16)

        @pl.when(pl.program_id(0) == pl.num_programs(0) - 1)
        def _():
            for g in range(NG):
                dbs_ref[g:g + 1, :] = jnp.sum(jnp.transpose(db_ref[:, g * 128:(g + 1) * 128]), axis=0, keepdims=True)

    ublk, vblk, vec, wsp, bsp = _gmlp_specs()
    return _call(body, name="gmlp_bwd",
                 out_shape=(_sds((n, 2 * DG), BF16), _sds((1, DG), F32), _sds((NG, CH, CH), F32), _sds((NG, CH), F32),
                            _sds((1, DG), F32)),
                 grid=(n // GM_TM,),
                 in_specs=[ublk, vblk, pl.BlockSpec((GM_TM, DG), lambda i: (i, DA // DG)), vec, wsp, wsp, bsp, vec],
                 out_specs=(pl.BlockSpec((GM_TM, 2 * DG), lambda i: (i, 0)), vec, wsp,
                            pl.BlockSpec((NG, CH), lambda i: (0, 0)), vec),
                 scratch=[pltpu.VMEM((CH, DG), F32)], sem=("arbitrary",))(
        proj, proj, dmix, ln_g, ws, wst, bias, gg)


def _row(v):
    return v.reshape(1, -1)


def local_step(x, target, small, big):
    cos2, sin2 = rope_tables(T)
    tk_d = FF // NCHIP
    saved = []
    for l in range(NL):
        w = big[l]
        ws_b = small["w_spatial"][l].astype(BF16)
        bias = jnp.repeat(small["b_spatial"][l].T, 128, axis=1)
        h = rms_fwd(x, _row(small["norm1_g"][l]))
        proj = mm_cols(h, w["win"])
        qk = rope_fwd(proj, cos2, sin2)
        outs = [attn_fwd(qk, proj, dil) for dil in DILATIONS]
        a, lse, mix = attn_merge([o for o, _ in outs], [s for _, s in outs], _row(small["mix_norm_attn_g"][l]))
        mix = gmlp_fwd(proj, _row(small["gmlp_ln_g"][l]), ws_b, bias, _row(small["mix_norm_gmlp_g"][l]), mix)
        x_mid = mm_rows_res(mix, w["wout"], x, tk=D)
        h2 = rms_fwd(x_mid, _row(small["norm2_g"][l]))
        gate, up, ff = mm_gateup(h2, w["wg"], w["wu"])
        x_out = mm_rows_res(ff, w["wd"], x_mid, tk=tk_d)
        saved.append(dict(x=x, h=h, proj=proj, qk=qk, a=a, lse=lse, mix=mix, x_mid=x_mid, h2=h2, gate=gate, up=up, ff=ff,
                          ws_b=ws_b, bias=bias))
        x = x_out

    loss8, dx, dxb, dfinal = loss_head(x, _row(small["final_g"]), target)
    gs = {k: [None] * NL for k in ("norm1_g", "gmlp_ln_g", "w_spatial", "b_spatial", "mix_norm_attn_g", "mix_norm_gmlp_g",
                                   "norm2_g")}
    gbig = [None] * NL
    for l in reversed(range(NL)):
        w, sv = big[l], saved[l]
        dgate, dup = mm_dff(dxb, w["wd"], sv["gate"], sv["up"], tk=tk_d)
        g_wd = mm_tn(sv["ff"], dxb, 1, tka=tk_d)
        dh2 = mm_nt_cols([dgate, dup], [w["wg"], w["wu"]])
        g_wg = mm_tn(sv["h2"], dgate, NCHIP, tka=D)
        g_wu = mm_tn(sv["h2"], dup, NCHIP, tka=D)
        dx, dxb, gs["norm2_g"][l] = rms_bwd(dh2, sv["x_mid"], _row(small["norm2_g"][l]), dx)
        dmix = mm_nt_rows(dxb, w["wout"], tk=D // 2)
        g_wout = mm_tn(sv["mix"], dxb, 1, tka=D // 2)
        da, gs["mix_norm_attn_g"][l] = attn_norm_bwd(dmix, sv["a"], _row(small["mix_norm_attn_g"][l]))
        parts = [attn_bwd(sv["qk"], sv["proj"], sv["a"], da, sv["lse"], dil) for dil in DILATIONS]
        wst = jnp.swapaxes(small["w_spatial"][l], 1, 2).astype(BF16)
        duv, gs["gmlp_ln_g"][l], gs["w_spatial"][l], dbias, gs["mix_norm_gmlp_g"][l] = gmlp_bwd(
            sv["proj"], dmix, _row(small["gmlp_ln_g"][l]), sv["ws_b"], wst, sv["bias"], _row(small["mix_norm_gmlp_g"][l]))
        gs["b_spatial"][l] = dbias
        dproj = dproj_assemble([p[0] for p in parts], [p[1] for p in parts], [p[2] for p in parts], duv, cos2, sin2)
        dh = mm_nt_cols([dproj], [w["win"]])
        g_win = mm_tn(sv["h"], dproj, NCHIP, tka=D)
        dx, dxb, gs["norm1_g"][l] = rms_bwd(dh, sv["x"], _row(small["norm1_g"][l]), dx)
        gbig[l] = dict(win=g_win, wout=g_wout.reshape(NCHIP, D // NCHIP, D), wg=g_wg, wu=g_wu,
                       wd=g_wd.reshape(NCHIP, FF // NCHIP, D))
    return loss8, dx, gs, dfinal, gbig


KINDS = ("win", "wout", "wg", "wu", "wd")
ANY = pl.BlockSpec(memory_space=pl.ANY)


def _place():
    return lax.axis_index("x"), lax.axis_index("y"), lax.axis_index("c")


def _other_chips(x, y):
    return [(1 - x, y), (x, 1 - y), (1 - x, 1 - y)]


def _remote(src, dst, send_sem, recv_sem, to):
    return pltpu.make_async_remote_copy(src_ref=src, dst_ref=dst, send_sem=send_sem, recv_sem=recv_sem,
                                        device_id=to, device_id_type=MESH)


def _hbm_call(body, *, name, n_in, out_shape, scratch, in_place=False):
    return pl.pallas_call(body, name=name, out_shape=tuple(out_shape), in_specs=[ANY] * n_in,
                          out_specs=tuple(ANY for _ in out_shape), scratch_shapes=list(scratch),
                          input_output_aliases={k: k for k in range(n_in)} if in_place else {},
                          compiler_params=pltpu.CompilerParams(vmem_limit_bytes=VMEM_LIMIT))


def place_vector():
    x, y, c = _place()
    return jnp.stack([c, 2 * x + y] + [2 * cx + cy for cx, cy in _other_chips(x, y)]).astype(jnp.int32)


PUSH_ROWS = 256


def _push_tiles(src, dst, buf, load_sem, send_sem, recv_sem, to, src_at, dst_at, nt):
    tr = buf.shape[1]
    n = len(src_at) * nt

    def pick(vals, seg):
        out = vals[0]
        for q in range(1, len(vals)):
            out = jnp.where(seg == q, vals[q], out)
        return out

    def tile(ref, at, t):
        seg = t // nt
        row = pl.multiple_of(pick([a[1] for a in at], seg) + (t - seg * nt) * tr, 16)
        return ref.at[pick([a[0] for a in at], seg), pl.ds(row, tr), :]

    def load(t, slot):
        return pltpu.make_async_copy(tile(src, src_at, t), buf.at[slot], load_sem.at[slot])

    def send(t, slot):
        return _remote(buf.at[slot], tile(dst, dst_at, t), send_sem, recv_sem, to)

    load(0, 0).start()

    def step(t, carry):
        slot = t % 2
        load(t, slot).wait()

        @pl.when(t > 0)
        def _():
            send(t - 1, 1 - slot).wait_send()

        @pl.when(t + 1 < n)
        def _():
            load(t + 1, 1 - slot).start()

        send(t, slot).start()
        return carry

    lax.fori_loop(0, n, step, 0)
    send(n - 1, (n - 1) % 2).wait_send()


def _await_tiles(dst, send_sem, recv_sem, to, nseg, rows):
    whole = dst.at[pl.ds(0, nseg), pl.ds(0, rows), :]
    _remote(whole, whole, send_sem, recv_sem, to).wait_recv()


def _push_rows(seg_rows):
    return _row_tile(seg_rows, PUSH_ROWS)


def _push_scratch(arrs, seg_rows):
    return ([pltpu.VMEM((2, _push_rows(r), a.shape[-1]), a.dtype) for a, r in zip(arrs, seg_rows)]
            + [pltpu.SemaphoreType.DMA((len(arrs), 2)), pltpu.SemaphoreType.DMA((len(arrs),)),
               pltpu.SemaphoreType.DMA((len(arrs),))])


def cast_into(w, l, pos):
    _, r, cols = w.shape
    tr = _row_tile(r)

    def body(pos_ref, w_ref, o_ref):
        del pos_ref
        o_ref[...] = w_ref[...].astype(BF16)

    return _call(body, name="cast_into", out_shape=_sds((NCHIP, r, cols), BF16), grid=(r // tr,),
                 in_specs=[pl.BlockSpec((None, tr, cols), lambda i, pos: (l, i, 0))],
                 out_specs=pl.BlockSpec((None, tr, cols), lambda i, pos: (pos[1], i, 0)),
                 sem=("parallel",), prefetch=1)(pos, w)


def gather_halves(fulls):
    nk = len(fulls)

    def body(*refs):
        bufs = refs[nk:2 * nk]
        send, recv = refs[2 * nk:]
        x, y, c = _place()
        chips = _other_chips(x, y)

        def half(k, chip):
            hr = bufs[k].shape[1] // 2
            return bufs[k].at[chip, pl.ds(c * hr, hr), :]

        cps = [_remote(half(k, 2 * x + y), half(k, 2 * x + y), send.at[k, j], recv.at[k, j], (cx, cy, c))
               for k in range(nk) for j, (cx, cy) in enumerate(chips)]
        for cp in cps:
            cp.start()
        for k in range(nk):
            for j, (cx, cy) in enumerate(chips):
                _remote(half(k, 2 * cx + cy), half(k, 2 * cx + cy), send.at[k, j], recv.at[k, j], (cx, cy, c)).wait_recv()
        for cp in cps:
            cp.wait_send()

    return _hbm_call(body, name="gather_halves", n_in=nk, out_shape=[_sds(f.shape, f.dtype) for f in fulls], in_place=True,
                     scratch=[pltpu.SemaphoreType.DMA((nk, 3)), pltpu.SemaphoreType.DMA((nk, 3))])(*fulls)


def pair_forward(fulls):
    nk = len(fulls)

    def body(*refs):
        bufs = refs[nk:2 * nk]
        stage = refs[2 * nk:3 * nk]
        load_sem, send_sem, recv_sem = refs[3 * nk:]
        x, y, c = _place()
        sib = (x, y, 1 - c)
        chips = [2 * cx + cy for cx, cy in _other_chips(x, y)]
        for k in range(nk):
            hr = bufs[k].shape[1] // 2
            at = [(chip, c * hr) for chip in chips]
            _push_tiles(bufs[k], bufs[k], stage[k], load_sem.at[k], send_sem.at[k], recv_sem.at[k], sib, at, at,
                        hr // _push_rows(hr))
        for k in range(nk):
            _await_tiles(bufs[k], send_sem.at[k], recv_sem.at[k], sib, 3, bufs[k].shape[1] // 2)

    return _hbm_call(body, name="pair_forward", n_in=nk, out_shape=[_sds(f.shape, f.dtype) for f in fulls], in_place=True,
                     scratch=_push_scratch(fulls, [f.shape[1] // 2 for f in fulls]))(*fulls)


def rs_pair_send(gs):
    nk = len(gs)

    def body(*refs):
        ins, got = refs[:nk], refs[nk:2 * nk]
        stage = refs[2 * nk:3 * nk]
        load_sem, send_sem, recv_sem = refs[3 * nk:]
        x, y, c = _place()
        sib = (x, y, 1 - c)
        for k in range(nk):
            hr = ins[k].shape[1] // 2
            _push_tiles(ins[k], got[k], stage[k], load_sem.at[k], send_sem.at[k], recv_sem.at[k], sib,
                        [(s, (1 - c) * hr) for s in range(NCHIP)], [(s, 0) for s in range(NCHIP)], hr // _push_rows(hr))
        for k in range(nk):
            _await_tiles(got[k], send_sem.at[k], recv_sem.at[k], sib, NCHIP, ins[k].shape[1] // 2)

    half = [_sds((g.shape[0], g.shape[1] // 2, g.shape[2]), g.dtype) for g in gs]
    return _hbm_call(body, name="rs_pair_send", n_in=nk, out_shape=half,
                     scratch=_push_scratch(gs, [g.shape[1] // 2 for g in gs]))(*gs)


def rs_chip_exchange(ps):
    nk = len(ps)

    def body(*refs):
        ins, got = refs[:nk], refs[nk:2 * nk]
        send, recv = refs[2 * nk:]
        x, y, c = _place()
        cps = [_remote(ins[k].at[2 * cx + cy], got[k].at[j], send.at[k, j], recv.at[k, j], (cx, cy, c))
               for k in range(nk) for j, (cx, cy) in enumerate(_other_chips(x, y))]
        for cp in cps:
            cp.start()
        for cp in cps:
            cp.wait()

    out = [_sds((3,) + p.shape[1:], p.dtype) for p in ps]
    return _hbm_call(body, name="rs_chip_exchange", n_in=nk, out_shape=out,
                     scratch=[pltpu.SemaphoreType.DMA((nk, 3)), pltpu.SemaphoreType.DMA((nk, 3))])(*ps)


def rs_pair_share(fulls):
    nk = len(fulls)

    def body(*refs):
        bufs = refs[nk:2 * nk]
        stage = refs[2 * nk:3 * nk]
        load_sem, send_sem, recv_sem = refs[3 * nk:]
        x, y, c = _place()
        sib = (x, y, 1 - c)
        for k in range(nk):
            hr = bufs[k].shape[1]
            _push_tiles(bufs[k], bufs[k], stage[k], load_sem.at[k], send_sem.at[k], recv_sem.at[k], sib,
                        [(c, 0)], [(c, 0)], hr // _push_rows(hr))
        for k in range(nk):
            _await_tiles(bufs[k], send_sem.at[k], recv_sem.at[k], sib, 1, bufs[k].shape[1])

    return _hbm_call(body, name="rs_pair_share", n_in=nk, out_shape=[_sds(f.shape, f.dtype) for f in fulls], in_place=True,
                     scratch=_push_scratch(fulls, [f.shape[1] for f in fulls]))(*fulls)


N_DEV = 8


def allreduce_small(buf):
    rows = buf.shape[0]
    rp = rows // N_DEV

    def body(in_ref, out_ref, land_ref, send, recv):
        x, y, c = _place()
        mine = pl.ds(pl.multiple_of((4 * x + 2 * y + c) * rp, 8), rp)
        peers = [(x ^ fx, y ^ fy, c ^ fc) for fx in (0, 1) for fy in (0, 1) for fc in (0, 1)][1:]
        block = [pl.ds(pl.multiple_of((4 * px + 2 * py + pc) * rp, 8), rp) for px, py, pc in peers]
        scatter = [_remote(in_ref.at[block[r], :], land_ref.at[r], send.at[r], recv.at[r], peers[r]) for r in range(7)]
        for cp in scatter:
            cp.start()
        for cp in scatter:
            cp.wait()
        acc = in_ref[mine, :]
        for r in range(7):
            acc = acc + land_ref[r]
        out_ref[mine, :] = acc
        spread = [_remote(out_ref.at[mine, :], out_ref.at[mine, :], send.at[7 + r], recv.at[7 + r], peers[r]) for r in range(7)]
        for cp in spread:
            cp.start()
        for r in range(7):
            _remote(out_ref.at[block[r], :], out_ref.at[block[r], :], send.at[7 + r], recv.at[7 + r], peers[r]).wait_recv()
        for cp in spread:
            cp.wait_send()

    vm = pl.BlockSpec(memory_space=pltpu.VMEM)
    return pl.pallas_call(body, name="allreduce_small", out_shape=_sds(buf.shape, F32), in_specs=[vm], out_specs=vm,
                          scratch_shapes=[pltpu.VMEM((7, rp, LANES), F32), pltpu.SemaphoreType.DMA((14,)),
                                          pltpu.SemaphoreType.DMA((14,))])(buf)


def _row_tile(rows, cap=512):
    return max(t for t in range(16, cap + 1, 16) if rows % t == 0)


def add_halves(g, got, pos):
    s, r, cols = g.shape
    hr = r // 2
    tr = _row_tile(hr)
    nt = hr // tr

    def body(pos_ref, a_ref, b_ref, o_ref):
        del pos_ref
        o_ref[...] = (a_ref[...].astype(F32) + b_ref[...].astype(F32)).astype(BF16)

    blk = pl.BlockSpec((None, tr, cols), lambda q, i, pos: (q, i, 0))
    return _call(body, name="add_halves", out_shape=_sds((s, hr, cols), BF16), grid=(s, nt),
                 in_specs=[pl.BlockSpec((None, tr, cols), lambda q, i, pos: (q, pos[0] * nt + i, 0)), blk],
                 out_specs=blk, sem=("parallel", "parallel"), prefetch=1)(pos, g, got)


def add_slots(p, got, pos):
    _, hr, cols = p.shape
    tr = _row_tile(hr)

    def body(pos_ref, o_ref, g0, g1, g2, out_ref):
        del pos_ref
        out_ref[...] = ((o_ref[...].astype(F32) + g0[...].astype(F32)) + g1[...].astype(F32)) + g2[...].astype(F32)

    slots = [pl.BlockSpec((None, tr, cols), functools.partial(lambda j, i, pos: (j, i, 0), j)) for j in range(3)]
    return _call(body, name="add_slots", out_shape=_sds((2, hr, cols), F32), grid=(hr // tr,),
                 in_specs=[pl.BlockSpec((None, tr, cols), lambda i, pos: (pos[1], i, 0))] + slots,
                 out_specs=pl.BlockSpec((None, tr, cols), lambda i, pos: (pos[0], i, 0)),
                 sem=("parallel",), prefetch=1)(pos, p, got, got, got)


def _adamw_math(w, g, m, v):
    nm = B1 * m + (1.0 - B1) * g
    nv = B2 * v + (1.0 - B2) * (g * g)
    m_hat = nm / (1.0 - B1 ** STEP)
    v_hat = nv / (1.0 - B2 ** STEP)
    return -LR * (m_hat / (jnp.sqrt(v_hat) + AEPS) + WD * w), nm, nv


def adamw_layer(w, g, m, v, l, prev):
    nl, r, cols = w.shape
    tr = _row_tile(r, 256)

    def body(w_ref, g_ref, m_ref, v_ref, *rest):
        go_ref, d_ref, nm_ref, nv_ref = rest[-4:]
        gv = g_ref[...]
        d, nm, nv = _adamw_math(w_ref[...], gv, m_ref[...], v_ref[...])
        go_ref[...] = gv
        d_ref[...] = d
        nm_ref[...] = nm
        nv_ref[...] = nv

    lay = pl.BlockSpec((None, tr, cols), lambda i: (l, i, 0))
    o = _sds((nl, r, cols), F32)
    extra = [] if prev is None else list(prev)
    return _call(body, name="adamw_layer", out_shape=(o, o, o, o), grid=(r // tr,),
                 in_specs=[lay, pl.BlockSpec((tr, cols), lambda i: (i, 0)), lay, lay] + [ANY] * len(extra),
                 out_specs=(lay, lay, lay, lay), sem=("parallel",),
                 aliases={4 + j: j for j in range(len(extra))})(w, g, m, v, *extra)


def adamw(w, g, m, v):
    shape = w.shape
    cols = shape[-1]
    rows = w.size // cols
    tr = _row_tile(rows, 256)

    def body(w_ref, g_ref, m_ref, v_ref, d_ref, nm_ref, nv_ref):
        d_ref[...], nm_ref[...], nv_ref[...] = _adamw_math(w_ref[...], g_ref[...], m_ref[...], v_ref[...])

    blk = pl.BlockSpec((tr, cols), lambda i: (i, 0))
    o = _sds((rows, cols), F32)
    outs = _call(body, name="adamw", out_shape=(o, o, o), grid=(rows // tr,), in_specs=[blk] * 4, out_specs=(blk, blk, blk),
                 sem=("parallel",))(*[t.reshape(rows, cols) for t in (w, g, m, v)])
    return [t.reshape(shape) for t in outs]


SMALL = ("norm1_g", "gmlp_ln_g", "w_spatial", "b_spatial", "mix_norm_attn_g", "mix_norm_gmlp_g", "norm2_g", "final_g")
WEIGHTS = ("norm1_g", "w_in", "gmlp_ln_g", "w_spatial", "b_spatial", "mix_norm_attn_g", "mix_norm_gmlp_g", "w_out",
           "norm2_g", "w_gate", "w_up", "w_down", "final_g")
BIG = dict(win="w_in", wout="w_out", wg="w_gate", wu="w_up", wd="w_down")


def _pack(parts):
    flat = jnp.concatenate([parts[n].reshape(-1) for n in SMALL])
    rows = -(-flat.shape[0] // (LANES * 8 * N_DEV)) * 8 * N_DEV
    return jnp.pad(flat, (0, rows * LANES - flat.shape[0])).reshape(rows, LANES)


def _unpack(buf, like):
    flat = buf.reshape(-1)
    out, at = {}, 0
    for n in SMALL:
        out[n] = flat[at:at + like[n].size].reshape(like[n].shape)
        at += like[n].size
    return out


def kernel(x, norm1_g, w_in, gmlp_ln_g, w_spatial, b_spatial, mix_norm_attn_g, mix_norm_gmlp_g, w_out, norm2_g, w_gate, w_up, w_down, final_g, loss_target, m_norm1_g, m_w_in, m_gmlp_ln_g, m_w_spatial, m_b_spatial, m_mix_norm_attn_g, m_mix_norm_gmlp_g, m_w_out, m_norm2_g, m_w_gate, m_w_up, m_w_down, m_final_g, v_norm1_g, v_w_in, v_gmlp_ln_g, v_w_spatial, v_b_spatial, v_mix_norm_attn_g, v_mix_norm_gmlp_g, v_w_out, v_norm2_g, v_w_gate, v_w_up, v_w_down, v_final_g):
    w = dict(norm1_g=norm1_g, w_in=w_in, gmlp_ln_g=gmlp_ln_g, w_spatial=w_spatial, b_spatial=b_spatial,
             mix_norm_attn_g=mix_norm_attn_g, mix_norm_gmlp_g=mix_norm_gmlp_g, w_out=w_out, norm2_g=norm2_g,
             w_gate=w_gate, w_up=w_up, w_down=w_down, final_g=final_g)
    m = dict(norm1_g=m_norm1_g, w_in=m_w_in, gmlp_ln_g=m_gmlp_ln_g, w_spatial=m_w_spatial, b_spatial=m_b_spatial,
             mix_norm_attn_g=m_mix_norm_attn_g, mix_norm_gmlp_g=m_mix_norm_gmlp_g, w_out=m_w_out, norm2_g=m_norm2_g,
             w_gate=m_w_gate, w_up=m_w_up, w_down=m_w_down, final_g=m_final_g)
    v = dict(norm1_g=v_norm1_g, w_in=v_w_in, gmlp_ln_g=v_gmlp_ln_g, w_spatial=v_w_spatial, b_spatial=v_b_spatial,
             mix_norm_attn_g=v_mix_norm_attn_g, mix_norm_gmlp_g=v_mix_norm_gmlp_g, w_out=v_w_out, norm2_g=v_norm2_g,
             w_gate=v_w_gate, w_up=v_w_up, w_down=v_w_down, final_g=v_final_g)

    pos = place_vector()
    big = []
    for l in range(NL):
        full = pair_forward(gather_halves([cast_into(w[BIG[k]], l, pos) for k in KINDS]))
        big.append(dict(zip(KINDS, full)))
        big[l]["wout"] = big[l]["wout"].reshape(D, D)
        big[l]["wd"] = big[l]["wd"].reshape(FF, D)

    small = {n: w[n] for n in SMALL}
    loss8, dx, gs, dfinal, gbig = local_step(x[0], loss_target[0], small, big)
    loss = lax.psum(loss8[0, 0], ("x", "y", "c"))

    grads, delta, new_m, new_v = {}, {}, {}, {}
    stacked = {k: None for k in KINDS}
    for l in reversed(range(NL)):
        partial_grads = [gbig[l][k] for k in KINDS]
        pair = [add_halves(g, got, pos) for g, got in zip(partial_grads, rs_pair_send(partial_grads))]
        halves = [add_slots(p, got, pos) for p, got in zip(pair, rs_chip_exchange(pair))]
        for k, full in zip(KINDS, rs_pair_share(halves)):
            n = BIG[k]
            stacked[k] = adamw_layer(w[n], full.reshape(w[n].shape[1:]), m[n], v[n], l, stacked[k])
    for k in KINDS:
        grads[BIG[k]], delta[BIG[k]], new_m[BIG[k]], new_v[BIG[k]] = stacked[k]
    part = {n: jnp.stack(gs[n]).reshape(w[n].shape) for n in SMALL[:-1]}
    part["final_g"] = dfinal.reshape(w["final_g"].shape)
    grads.update(_unpack(allreduce_small(_pack(part)), part))

    sm = adamw(_pack({n: w[n] for n in SMALL}), _pack({n: grads[n] for n in SMALL}), _pack({n: m[n] for n in SMALL}),
               _pack({n: v[n] for n in SMALL}))
    for res, packed in zip((delta, new_m, new_v), sm):
        res.update(_unpack(packed, part))

    return (loss, dx.reshape(x.shape), *[grads[n] for n in WEIGHTS], *[delta[n] for n in WEIGHTS],
            *[new_m[n] for n in WEIGHTS], *[new_v[n] for n in WEIGHTS])
```

```python
import functools

import jax
import jax.numpy as jnp
from jax import lax
from jax.experimental import pallas as pl
from jax.experimental.pallas import tpu as pltpu

F32 = jnp.float32
BF16 = jnp.bfloat16

D = 2048
T = 4096
NL = 4
HD = 128
DA = D // 2
DG = D - DA
NH = DA // HD
NG = DG // 128
CH = 128
DIN = 3 * DA + 2 * DG
FF = 5632
DILATIONS = (1, 4, 16)
NSIDE = 64
ROPE_THETA = 10000.0
EPS = 1e-6
NEG = -1e30
SCALE = HD ** -0.5
NCHIP = 4

LR, B1, B2, AEPS, WD, STEP = 0.001, 0.9, 0.999, 1e-08, 0.01, 10

LANES = 128
VMEM_LIMIT = 56 * 1024 * 1024
MESH = pl.DeviceIdType.MESH


def _call(body, *, name, out_shape, grid=(), in_specs=None, out_specs=None, scratch=(), sem=None, aliases=None,
          prefetch=0):
    params = dict(vmem_limit_bytes=VMEM_LIMIT)
    if sem is not None:
        params["dimension_semantics"] = sem
    if prefetch:
        spec = pltpu.PrefetchScalarGridSpec(num_scalar_prefetch=prefetch, grid=grid, in_specs=in_specs,
                                            out_specs=out_specs, scratch_shapes=list(scratch))
        return pl.pallas_call(body, name=name, out_shape=out_shape, grid_spec=spec,
                              input_output_aliases=aliases or {}, compiler_params=pltpu.CompilerParams(**params))
    kw = {}
    if grid:
        kw["grid"] = grid
    return pl.pallas_call(
        body, name=name, out_shape=out_shape, in_specs=in_specs, out_specs=out_specs,
        scratch_shapes=list(scratch), input_output_aliases=aliases or {},
        compiler_params=pltpu.CompilerParams(**params), **kw)


def _sds(shape, dtype):
    return jax.ShapeDtypeStruct(tuple(shape), dtype)


def _dot(a, b):
    return jnp.dot(a, b, preferred_element_type=F32)


def _dot_nt(a, b):
    return lax.dot_general(a, b, (((1,), (1,)), ((), ())), preferred_element_type=F32)


def _dot_tn(a, b):
    return lax.dot_general(a, b, (((0,), (0,)), ((), ())), preferred_element_type=F32)


def _sigmoid(x):
    return 1.0 / (1.0 + jnp.exp(-x))


def _behind(after):
    return ([], []) if after is None else ([after], [pl.BlockSpec(memory_space=pl.ANY)])


def rms_fwd(x, g, after=None):
    n, d = x.shape
    tm = 256
    extra, extra_specs = _behind(after)

    def body(x_ref, g_ref, *rest):
        h_ref = rest[-1]
        xv = x_ref[...]
        r = lax.rsqrt(jnp.mean(xv * xv, axis=-1, keepdims=True) + EPS)
        h_ref[...] = (xv * r * g_ref[...]).astype(BF16)

    return _call(body, name="rms_fwd", out_shape=_sds((n, d), BF16), grid=(n // tm,),
                 in_specs=[pl.BlockSpec((tm, d), lambda i: (i, 0)), pl.BlockSpec((1, d), lambda i: (0, 0))] + extra_specs,
                 out_specs=pl.BlockSpec((tm, d), lambda i: (i, 0)), sem=("parallel",))(x, g, *extra)


def rms_bwd(dh, x, g, dres):
    n, d = x.shape
    tm = 256

    def body(dh_ref, x_ref, g_ref, dres_ref, dx_ref, dxb_ref, dg_ref):
        @pl.when(pl.program_id(0) == 0)
        def _():
            dg_ref[...] = jnp.zeros_like(dg_ref)

        xv = x_ref[...]
        r = lax.rsqrt(jnp.mean(xv * xv, axis=-1, keepdims=True) + EPS)
        xhat = xv * r
        dhv = dh_ref[...].astype(F32)
        dg_ref[...] += jnp.sum(dhv * xhat, axis=0, keepdims=True)
        dxn = dhv * g_ref[...]
        dx = dres_ref[...] + r * (dxn - xhat * jnp.mean(dxn * xhat, axis=-1, keepdims=True))
        dx_ref[...] = dx
        dxb_ref[...] = dx.astype(BF16)

    row = pl.BlockSpec((tm, d), lambda i: (i, 0))
    vec = pl.BlockSpec((1, d), lambda i: (0, 0))
    return _call(body, name="rms_bwd", out_shape=(_sds((n, d), F32), _sds((n, d), BF16), _sds((1, d), F32)),
                 grid=(n // tm,), in_specs=[row, row, vec, row], out_specs=(row, row, vec), sem=("arbitrary",))(dh, x, g, dres)


def loss_head(x, g, target):
    n, d = x.shape
    tm = 256

    def body(x_ref, g_ref, t_ref, loss_ref, dx_ref, dxb_ref, dg_ref):
        @pl.when(pl.program_id(0) == 0)
        def _():
            dg_ref[...] = jnp.zeros_like(dg_ref)
            loss_ref[...] = jnp.zeros_like(loss_ref)

        xv = x_ref[...]
        r = lax.rsqrt(jnp.mean(xv * xv, axis=-1, keepdims=True) + EPS)
        xhat = xv * r
        gv = g_ref[...]
        err = xhat * gv - t_ref[...]
        loss_ref[...] += 0.5 * jnp.sum(jnp.mean(err * err, axis=-1, keepdims=True))
        dy = err * (1.0 / d)
        dg_ref[...] += jnp.sum(dy * xhat, axis=0, keepdims=True)
        dxn = dy * gv
        dx = r * (dxn - xhat * jnp.mean(dxn * xhat, axis=-1, keepdims=True))
        dx_ref[...] = dx
        dxb_ref[...] = dx.astype(BF16)

    row = pl.BlockSpec((tm, d), lambda i: (i, 0))
    vec = pl.BlockSpec((1, d), lambda i: (0, 0))
    return _call(body, name="loss_head",
                 out_shape=(_sds((8, LANES), F32), _sds((n, d), F32), _sds((n, d), BF16), _sds((1, d), F32)),
                 grid=(n // tm,), in_specs=[row, vec, row],
                 out_specs=(pl.BlockSpec((8, LANES), lambda i: (0, 0)), row, row, vec), sem=("arbitrary",))(x, g, target)


def mm_cols(a, wc, tm=512):
    n, k = a.shape
    s, _, nq = wc.shape

    def body(a_ref, w_ref, o_ref):
        o_ref[...] = _dot(a_ref[...], w_ref[...]).astype(BF16)

    return _call(body, name="mm_cols", out_shape=_sds((n, s * nq), BF16), grid=(s, n // tm),
                 in_specs=[pl.BlockSpec((tm, k), lambda j, i: (i, 0)), pl.BlockSpec((None, k, nq), lambda j, i: (j, 0, 0))],
                 out_specs=pl.BlockSpec((tm, nq), lambda j, i: (i, j)), sem=("parallel", "parallel"))(a, wc)


def mm_gateup(h, wg, wu, tm=512):
    n, k = h.shape
    s, _, nq = wg.shape

    def body(h_ref, wg_ref, wu_ref, gate_ref, up_ref, ff_ref):
        hv = h_ref[...]
        gate = _dot(hv, wg_ref[...])
        up = _dot(hv, wu_ref[...])
        gate_ref[...] = gate.astype(BF16)
        up_ref[...] = up.astype(BF16)
        ff_ref[...] = (gate * _sigmoid(gate) * up).astype(BF16)

    wspec = pl.BlockSpec((None, k, nq), lambda j, i: (j, 0, 0))
    ospec = pl.BlockSpec((tm, nq), lambda j, i: (i, j))
    o = _sds((n, s * nq), BF16)
    return _call(body, name="mm_gateup", out_shape=(o, o, o), grid=(s, n // tm),
                 in_specs=[pl.BlockSpec((tm, k), lambda j, i: (i, 0)), wspec, wspec],
                 out_specs=(ospec, ospec, ospec), sem=("parallel", "parallel"))(h, wg, wu)


def mm_rows_res(a, wr, res, tk, tm=512, tn=1024):
    n, k = a.shape
    _, nn = wr.shape
    nk = k // tk

    def body(a_ref, w_ref, r_ref, o_ref, acc):
        kk = pl.program_id(2)

        @pl.when(kk == 0)
        def _():
            acc[...] = r_ref[...]

        acc[...] += _dot(a_ref[...], w_ref[...])

        @pl.when(kk == nk - 1)
        def _():
            o_ref[...] = acc[...]

    return _call(body, name="mm_rows_res", out_shape=_sds((n, nn), F32), grid=(nn // tn, n // tm, nk),
                 in_specs=[pl.BlockSpec((tm, tk), lambda j, i, kk: (i, kk)), pl.BlockSpec((tk, tn), lambda j, i, kk: (kk, j)),
                           pl.BlockSpec((tm, tn), lambda j, i, kk: (i, j))],
                 out_specs=pl.BlockSpec((tm, tn), lambda j, i, kk: (i, j)),
                 scratch=[pltpu.VMEM((tm, tn), F32)], sem=("parallel", "parallel", "arbitrary"))(a, wr, res)


def mm_nt_rows(dy, wr, tk, tm=512):
    n, nn = dy.shape
    k, _ = wr.shape

    def body(dy_ref, w_ref, o_ref):
        o_ref[...] = _dot_nt(dy_ref[...], w_ref[...]).astype(BF16)

    return _call(body, name="mm_nt_rows", out_shape=_sds((n, k), BF16), grid=(k // tk, n // tm),
                 in_specs=[pl.BlockSpec((tm, nn), lambda j, i: (i, 0)), pl.BlockSpec((tk, nn), lambda j, i: (j, 0))],
                 out_specs=pl.BlockSpec((tm, tk), lambda j, i: (i, j)), sem=("parallel", "parallel"))(dy, wr)


def mm_dff(dy, wd, gate, up, tk, tm=512, after=None):
    n, nn = dy.shape
    k, _ = wd.shape
    extra, extra_specs = _behind(after)

    def body(dy_ref, w_ref, gate_ref, up_ref, *rest):
        dgate_ref, dup_ref = rest[-2:]
        dff = _dot_nt(dy_ref[...], w_ref[...])
        gate = gate_ref[...].astype(F32)
        up = up_ref[...].astype(F32)
        sig = _sigmoid(gate)
        dgate_ref[...] = (dff * up * (sig * (1.0 + gate * (1.0 - sig)))).astype(BF16)
        dup_ref[...] = (dff * (gate * sig)).astype(BF16)

    tile = pl.BlockSpec((tm, tk), lambda j, i: (i, j))
    o = _sds((n, k), BF16)
    return _call(body, name="mm_dff", out_shape=(o, o), grid=(k // tk, n // tm),
                 in_specs=[pl.BlockSpec((tm, nn), lambda j, i: (i, 0)), pl.BlockSpec((tk, nn), lambda j, i: (j, 0)), tile, tile]
                 + extra_specs,
                 out_specs=(tile, tile), sem=("parallel", "parallel"))(dy, wd, gate, up, *extra)


def mm_nt_cols(das, wcs, tm=512):
    n = das[0].shape[0]
    s, k, nq = wcs[0].shape
    npair = len(das)

    def body(*refs):
        da_refs, w_refs, o_ref, acc = refs[:npair], refs[npair:2 * npair], refs[2 * npair], refs[2 * npair + 1]
        ss = pl.program_id(1)

        @pl.when(ss == 0)
        def _():
            acc[...] = jnp.zeros_like(acc)

        for da_ref, w_ref in zip(da_refs, w_refs):
            acc[...] += _dot_nt(da_ref[...], w_ref[...])

        @pl.when(ss == s - 1)
        def _():
            o_ref[...] = acc[...].astype(BF16)

    return _call(body, name="mm_nt_cols%d" % npair, out_shape=_sds((n, k), BF16), grid=(n // tm, s),
                 in_specs=[pl.BlockSpec((tm, nq), lambda i, ss: (i, ss))] * npair
                 + [pl.BlockSpec((None, k, nq), lambda i, ss: (ss, 0, 0))] * npair,
                 out_specs=pl.BlockSpec((tm, k), lambda i, ss: (i, 0)),
                 scratch=[pltpu.VMEM((tm, k), F32)], sem=("parallel", "arbitrary"))(*das, *wcs)


def mm_tn(a, b, s, tka, tt=512):
    n, ka = a.shape
    nq = b.shape[1] // s
    nt = n // tt

    def body(a_ref, b_ref, o_ref, acc):
        t = pl.program_id(2)

        @pl.when(t == 0)
        def _():
            acc[...] = jnp.zeros_like(acc)

        acc[...] += _dot_tn(a_ref[...], b_ref[...])

        @pl.when(t == nt - 1)
        def _():
            o_ref[...] = acc[...].astype(BF16)

    return _call(body, name="mm_tn", out_shape=_sds((s, ka, nq), BF16), grid=(ka // tka, s, nt),
                 in_specs=[pl.BlockSpec((tt, tka), lambda i, j, t: (t, i)), pl.BlockSpec((tt, nq), lambda i, j, t: (t, j))],
                 out_specs=pl.BlockSpec((None, tka, nq), lambda i, j, t: (j, i, 0)),
                 scratch=[pltpu.VMEM((tka, nq), F32)], sem=("parallel", "parallel", "arbitrary"))(a, b)


def rope_tables(n):
    pos = jnp.arange(n, dtype=F32)
    inv = ROPE_THETA ** (-jnp.arange(0, HD, 2, dtype=F32) / HD)
    ang = pos[:, None] * inv[None, :]
    cos, sin = jnp.cos(ang), jnp.sin(ang)
    return jnp.concatenate([cos, cos], axis=-1), jnp.concatenate([-sin, sin], axis=-1)


def rope_fwd(proj, cos2, sin2):
    n = proj.shape[0]
    tm = 256
    nheads = 2 * DA // HD

    def body(p_ref, c_ref, s_ref, o_ref):
        c, s = c_ref[...], s_ref[...]
        for h in range(nheads):
            t = p_ref[:, h * HD:(h + 1) * HD].astype(F32)
            o_ref[:, h * HD:(h + 1) * HD] = (t * c + pltpu.roll(t, HD // 2, 1) * s).astype(BF16)

    tab = pl.BlockSpec((tm, HD), lambda i: (i, 0))
    return _call(body, name="rope_fwd", out_shape=_sds((n, 2 * DA), BF16), grid=(n // tm,),
                 in_specs=[pl.BlockSpec((tm, 2 * DA), lambda i: (i, 0)), tab, tab],
                 out_specs=pl.BlockSpec((tm, 2 * DA), lambda i: (i, 0)), sem=("parallel",))(proj, cos2, sin2)


def dproj_assemble(dqs, dks, dvs, duv, cos2, sin2):
    n = duv.shape[0]
    tm = 256

    def body(*refs):
        dq_refs, dk_refs, dv_refs = refs[0:3], refs[3:6], refs[6:9]
        duv_ref, c_ref, s_ref, o_ref = refs[9:]
        c, s = c_ref[...], s_ref[...]
        for part, part_refs in enumerate((dq_refs, dk_refs)):
            for h in range(NH):
                cols = slice(h * HD, (h + 1) * HD)
                t = part_refs[0][:, cols].astype(F32) + part_refs[1][:, cols].astype(F32) + part_refs[2][:, cols].astype(F32)
                out = t * c - pltpu.roll(t, HD // 2, 1) * s
                o_ref[:, part * DA + h * HD:part * DA + (h + 1) * HD] = out.astype(BF16)
        dv = dv_refs[0][...].astype(F32) + dv_refs[1][...].astype(F32) + dv_refs[2][...].astype(F32)
        o_ref[:, 2 * DA:3 * DA] = dv.astype(BF16)
        o_ref[:, 3 * DA:] = duv_ref[...]

    blk = pl.BlockSpec((tm, DA), lambda i: (i, 0))
    tab = pl.BlockSpec((tm, HD), lambda i: (i, 0))
    return _call(body, name="dproj_assemble", out_shape=_sds((n, DIN), BF16), grid=(n // tm,),
                 in_specs=[blk] * 9 + [pl.BlockSpec((tm, 2 * DG), lambda i: (i, 0)), tab, tab],
                 out_specs=pl.BlockSpec((tm, DIN), lambda i: (i, 0)), sem=("parallel",))(
        *dqs, *dks, *dvs, duv, cos2, sin2)


def _attn_tiles(ls):
    tq = min(128, ls)
    w = min(tq + 2 * NSIDE, ls)
    return tq, w, ls // tq


def _band(t, tq, w, ls):
    q0 = pl.multiple_of(t * tq, tq)
    ks = pl.multiple_of(jnp.clip(t * tq - NSIDE, 0, ls - w), NSIDE)
    qpos = q0 + lax.broadcasted_iota(jnp.int32, (tq, w), 0)
    kpos = ks + lax.broadcasted_iota(jnp.int32, (tq, w), 1)
    return q0, ks, jnp.abs(kpos - qpos) <= NSIDE


def _attn_specs(ls, dil):
    nqk = 2 * DA // HD
    npj = DIN // HD
    voff = 2 * DA // HD
    q = pl.BlockSpec((ls, HD), lambda g: (0, (g // NH) * nqk + g % NH))
    k = pl.BlockSpec((ls, HD), lambda g: (0, (g // NH) * nqk + NH + g % NH))
    v = pl.BlockSpec((ls, HD), lambda g: (0, (g // NH) * npj + voff + g % NH))
    o = pl.BlockSpec((ls, HD), lambda g: (0, g))
    return q, k, v, o


def attn_fwd(qk, proj, dil):
    n = qk.shape[0]
    ls = n // dil
    tq, w, nt = _attn_tiles(ls)

    def body(q_ref, k_ref, v_ref, o_ref, l_ref):
        def tile(t, carry):
            q0, ks, valid = _band(t, tq, w, ls)
            s = _dot_nt(q_ref[pl.ds(q0, tq), :], k_ref[pl.ds(ks, w), :]) * SCALE
            s = jnp.where(valid, s, NEG)
            m = jnp.max(s, axis=1, keepdims=True)
            p = jnp.exp(s - m)
            l = jnp.sum(p, axis=1, keepdims=True)
            o = _dot(p.astype(BF16), v_ref[pl.ds(ks, w), :]) / l
            o_ref[pl.ds(q0, tq), :] = o.astype(BF16)
            l_ref[pl.ds(q0, tq), :] = jnp.broadcast_to(m + jnp.log(l), (tq, HD))
            return carry

        lax.fori_loop(0, nt, tile, 0)

    q, k, v, o = _attn_specs(ls, dil)
    op, lp = _call(body, name="attn_fwd_d%d" % dil, out_shape=(_sds((ls, dil * DA), BF16), _sds((ls, dil * DA), F32)),
                   grid=(dil * NH,), in_specs=[q, k, v], out_specs=(o, o), sem=("parallel",))(
        qk.reshape(ls, dil * 2 * DA), qk.reshape(ls, dil * 2 * DA), proj.reshape(ls, dil * DIN))
    return op.reshape(n, DA), lp.reshape(n, DA)


def attn_merge(os_, ls_, ga):
    n = os_[0].shape[0]
    tm = 256

    def body(o1, o2, o3, l1, l2, l3, g_ref, a_ref, lse_ref, mix_ref):
        lv = [l1[...], l2[...], l3[...]]
        lmax = jnp.maximum(jnp.maximum(lv[0], lv[1]), lv[2])
        e = [jnp.exp(x - lmax) for x in lv]
        den = e[0] + e[1] + e[2]
        a = (e[0] * o1[...].astype(F32) + e[1] * o2[...].astype(F32) + e[2] * o3[...].astype(F32)) / den
        a_ref[...] = a.astype(BF16)
        lse_ref[...] = lmax + jnp.log(den)
        r = lax.rsqrt(jnp.mean(a * a, axis=-1, keepdims=True) + EPS)
        mix_ref[...] = (a * r * g_ref[...]).astype(BF16)

    blk = pl.BlockSpec((tm, DA), lambda i: (i, 0))
    return _call(body, name="attn_merge", out_shape=(_sds((n, DA), BF16), _sds((n, DA), F32), _sds((n, D), BF16)),
                 grid=(n // tm,), in_specs=[blk] * 6 + [pl.BlockSpec((1, DA), lambda i: (0, 0))],
                 out_specs=(blk, blk, blk), sem=("parallel",))(*os_, *ls_, ga)


def attn_norm_bwd(dmix, a, ga):
    n = a.shape[0]
    tm = 256

    def body(dm_ref, a_ref, g_ref, da_ref, dg_ref):
        @pl.when(pl.program_id(0) == 0)
        def _():
            dg_ref[...] = jnp.zeros_like(dg_ref)

        av = a_ref[...].astype(F32)
        r = lax.rsqrt(jnp.mean(av * av, axis=-1, keepdims=True) + EPS)
        ahat = av * r
        dm = dm_ref[...].astype(F32)
        dg_ref[...] += jnp.sum(dm * ahat, axis=0, keepdims=True)
        dn = dm * g_ref[...]
        da_ref[...] = (r * (dn - ahat * jnp.mean(dn * ahat, axis=-1, keepdims=True))).astype(BF16)

    blk = pl.BlockSpec((tm, DA), lambda i: (i, 0))
    vec = pl.BlockSpec((1, DA), lambda i: (0, 0))
    return _call(body, name="attn_norm_bwd", out_shape=(_sds((n, DA), BF16), _sds((1, DA), F32)), grid=(n // tm,),
                 in_specs=[blk, blk, vec], out_specs=(blk, vec), sem=("arbitrary",))(dmix, a, ga)


def attn_bwd(qk, proj, a, da, lse, dil):
    n = qk.shape[0]
    ls = n // dil
    tq, w, nt = _attn_tiles(ls)

    def body(q_ref, k_ref, v_ref, a_ref, da_ref, lse_ref, dq_ref, dk_ref, dv_ref, dk_acc, dv_acc):
        dk_acc[...] = jnp.zeros_like(dk_acc)
        dv_acc[...] = jnp.zeros_like(dv_acc)

        def tile(t, carry):
            q0, ks, valid = _band(t, tq, w, ls)
            q = q_ref[pl.ds(q0, tq), :]
            k = k_ref[pl.ds(ks, w), :]
            v = v_ref[pl.ds(ks, w), :]
            do = da_ref[pl.ds(q0, tq), :]
            s = jnp.where(valid, _dot_nt(q, k) * SCALE, NEG)
            p = jnp.exp(s - lse_ref[pl.ds(q0, tq), 0:1])
            drow = jnp.sum(do.astype(F32) * a_ref[pl.ds(q0, tq), :].astype(F32), axis=1, keepdims=True)
            ds = (p * (_dot_nt(do, v) - drow) * SCALE).astype(BF16)
            dv_acc[pl.ds(ks, w), :] += _dot_tn(p.astype(BF16), do)
            dk_acc[pl.ds(ks, w), :] += _dot_tn(ds, q)
            dq_ref[pl.ds(q0, tq), :] = _dot(ds, k).astype(BF16)
            return carry

        lax.fori_loop(0, nt, tile, 0)
        dk_ref[...] = dk_acc[...].astype(BF16)
        dv_ref[...] = dv_acc[...].astype(BF16)

    q, k, v, o = _attn_specs(ls, dil)
    out = _sds((ls, dil * DA), BF16)
    dq, dk, dv = _call(body, name="attn_bwd_d%d" % dil, out_shape=(out, out, out), grid=(dil * NH,),
                       in_specs=[q, k, v, o, o, o], out_specs=(o, o, o),
                       scratch=[pltpu.VMEM((ls, HD), F32), pltpu.VMEM((ls, HD), F32)], sem=("parallel",))(
        qk.reshape(ls, dil * 2 * DA), qk.reshape(ls, dil * 2 * DA), proj.reshape(ls, dil * DIN),
        a.reshape(ls, dil * DA), da.reshape(ls, dil * DA), lse.reshape(ls, dil * DA))
    return dq.reshape(n, DA), dk.reshape(n, DA), dv.reshape(n, DA)


GM_TM = 256
INV_SQRT2 = 0.7071067811865476
INV_SQRT2PI = 0.3989422804014327


def _gelu(x):
    return 0.5 * x * (1.0 + lax.erf(x * INV_SQRT2))


def _gelu_grad(x):
    return 0.5 * (1.0 + lax.erf(x * INV_SQRT2)) + x * (INV_SQRT2PI * jnp.exp(-0.5 * x * x))


def _gmlp_forward(up, vp, ln_g, ws_ref, bias):
    u = _gelu(up)
    v = _gelu(vp)
    vc = v - jnp.mean(v, axis=-1, keepdims=True)
    rs = lax.rsqrt(jnp.mean(vc * vc, axis=-1, keepdims=True) + EPS)
    vhat = vc * rs
    vln = (vhat * ln_g).astype(BF16)
    rows = []
    for c in range(GM_TM // CH):
        cols = [_dot(ws_ref[g], vln[c * CH:(c + 1) * CH, g * 128:(g + 1) * 128]) for g in range(NG)]
        rows.append(jnp.concatenate(cols, axis=1) + bias)
    return u, vhat, rs, vln, jnp.concatenate(rows, axis=0)


def _gmlp_specs():
    ublk = pl.BlockSpec((GM_TM, DG), lambda i: (i, 3 * DA // DG))
    vblk = pl.BlockSpec((GM_TM, DG), lambda i: (i, 3 * DA // DG + 1))
    vec = pl.BlockSpec((1, DG), lambda i: (0, 0))
    wsp = pl.BlockSpec((NG, CH, CH), lambda i: (0, 0, 0))
    bsp = pl.BlockSpec((CH, DG), lambda i: (0, 0))
    return ublk, vblk, vec, wsp, bsp


def gmlp_fwd(proj, ln_g, ws, bias, gg, mix):
    n = proj.shape[0]

    def body(up_ref, vp_ref, ln_ref, ws_ref, b_ref, gg_ref, mix_in, mix_ref):
        del mix_in
        u, _, _, _, mixed = _gmlp_forward(up_ref[...].astype(F32), vp_ref[...].astype(F32), ln_ref[...], ws_ref, b_ref[...])
        gout = u * mixed
        r = lax.rsqrt(jnp.mean(gout * gout, axis=-1, keepdims=True) + EPS)
        mix_ref[...] = (gout * r * gg_ref[...]).astype(BF16)

    ublk, vblk, vec, wsp, bsp = _gmlp_specs()
    return _call(body, name="gmlp_fwd", out_shape=_sds((n, D), BF16), grid=(n // GM_TM,),
                 in_specs=[ublk, vblk, vec, wsp, bsp, vec, pl.BlockSpec(memory_space=pl.ANY)],
                 out_specs=pl.BlockSpec((GM_TM, DG), lambda i: (i, DA // DG)), sem=("parallel",), aliases={6: 0})(
        proj, proj, ln_g, ws, bias, gg, mix)


def gmlp_bwd(proj, dmix, ln_g, ws, wst, bias, gg):
    n = proj.shape[0]

    def body(up_ref, vp_ref, dm_ref, ln_ref, ws_ref, wst_ref, b_ref, gg_ref, duv_ref, dln_ref, dws_ref, dbs_ref, dgg_ref,
             db_ref):
        @pl.when(pl.program_id(0) == 0)
        def _():
            dln_ref[...] = jnp.zeros_like(dln_ref)
            dws_ref[...] = jnp.zeros_like(dws_ref)
            db_ref[...] = jnp.zeros_like(db_ref)
            dgg_ref[...] = jnp.zeros_like(dgg_ref)

        up = up_ref[...].astype(F32)
        vp = vp_ref[...].astype(F32)
        ln_g = ln_ref[...]
        u, vhat, rs, vln, mixed = _gmlp_forward(up, vp, ln_g, ws_ref, b_ref[...])
        gout = u * mixed
        r = lax.rsqrt(jnp.mean(gout * gout, axis=-1, keepdims=True) + EPS)
        ghat = gout * r
        dm = dm_ref[...].astype(F32)
        dgg_ref[...] += jnp.sum(dm * ghat, axis=0, keepdims=True)
        dn = dm * gg_ref[...]
        dgout = r * (dn - ghat * jnp.mean(dn * ghat, axis=-1, keepdims=True))
        du = dgout * mixed
        dmixed = dgout * u
        dmb = dmixed.astype(BF16)
        rows = []
        for c in range(GM_TM // CH):
            rsl = slice(c * CH, (c + 1) * CH)
            db_ref[...] += dmixed[rsl, :]
            cols = []
            for g in range(NG):
                csl = slice(g * 128, (g + 1) * 128)
                dws_ref[g] += _dot_nt(dmb[rsl, csl], vln[rsl, csl])
                cols.append(_dot(wst_ref[g], dmb[rsl, csl]))
            rows.append(jnp.concatenate(cols, axis=1))
        dvln = jnp.concatenate(rows, axis=0)
        dln_ref[...] += jnp.sum(dvln * vhat, axis=0, keepdims=True)
        dvh = dvln * ln_g
        dv = rs * (dvh - jnp.mean(dvh, axis=-1, keepdims=True) - vhat * jnp.mean(dvh * vhat, axis=-1, keepdims=True))
        duv_ref[:, 0:DG] = (du * _gelu_grad(up)).astype(BF16)
        duv_ref[:, DG:] = (dv * _gelu_grad(vp)).astype(BF16)

        @pl.when(pl.program_id(0) == pl.num_programs(0) - 1)
        def _():
            for g in range(NG):
                dbs_ref[g:g + 1, :] = jnp.sum(jnp.transpose(db_ref[:, g * 128:(g + 1) * 128]), axis=0, keepdims=True)

    ublk, vblk, vec, wsp, bsp = _gmlp_specs()
    return _call(body, name="gmlp_bwd",
                 out_shape=(_sds((n, 2 * DG), BF16), _sds((1, DG), F32), _sds((NG, CH, CH), F32), _sds((NG, CH), F32),
                            _sds((1, DG), F32)),
                 grid=(n // GM_TM,),
                 in_specs=[ublk, vblk, pl.BlockSpec((GM_TM, DG), lambda i: (i, DA // DG)), vec, wsp, wsp, bsp, vec],
                 out_specs=(pl.BlockSpec((GM_TM, 2 * DG), lambda i: (i, 0)), vec, wsp,
                            pl.BlockSpec((NG, CH), lambda i: (0, 0)), vec),
                 scratch=[pltpu.VMEM((CH, DG), F32)], sem=("arbitrary",))(
        proj, proj, dmix, ln_g, ws, wst, bias, gg)


def _row(v):
    return v.reshape(1, -1)


def local_step(x, target, small, big):
    tables = rope_tables(T)
    saved = []
    for l in range(NL):
        x, sv = layer_fwd(x, small, l, big[l], tables)
        saved.append(sv)
    loss8, dx, dxb, dfinal = loss_head(x, _row(small["final_g"]), target)
    gs = {k: [None] * NL for k in SMALL[:-1]}
    gbig = [None] * NL
    for l in reversed(range(NL)):
        dx, dxb, gsl, gbig[l] = layer_bwd(dx, dxb, small, l, big[l], saved[l], tables)
        for k in gsl:
            gs[k][l] = gsl[k]
    return loss8, dx, gs, dfinal, gbig


def layer_fwd(x, small, l, w, tables, after=None):
    cos2, sin2 = tables
    ws_b = small["w_spatial"][l].astype(BF16)
    bias = jnp.repeat(small["b_spatial"][l].T, 128, axis=1)
    h = rms_fwd(x, _row(small["norm1_g"][l]), after)
    proj = mm_cols(h, w["win"])
    qk = rope_fwd(proj, cos2, sin2)
    outs = [attn_fwd(qk, proj, dil) for dil in DILATIONS]
    a, lse, mix = attn_merge([o for o, _ in outs], [s for _, s in outs], _row(small["mix_norm_attn_g"][l]))
    mix = gmlp_fwd(proj, _row(small["gmlp_ln_g"][l]), ws_b, bias, _row(small["mix_norm_gmlp_g"][l]), mix)
    x_mid = mm_rows_res(mix, w["wout"], x, tk=D)
    h2 = rms_fwd(x_mid, _row(small["norm2_g"][l]))
    gate, up, ff = mm_gateup(h2, w["wg"], w["wu"])
    x_out = mm_rows_res(ff, w["wd"], x_mid, tk=FF // NCHIP)
    return x_out, dict(x=x, h=h, proj=proj, qk=qk, a=a, lse=lse, mix=mix, x_mid=x_mid, h2=h2, gate=gate, up=up, ff=ff,
                       ws_b=ws_b, bias=bias)


def layer_bwd(dx, dxb, small, l, w, sv, tables, after=None):
    cos2, sin2 = tables
    gs = {}
    dgate, dup = mm_dff(dxb, w["wd"], sv["gate"], sv["up"], tk=FF // NCHIP, after=after)
    g_wd = mm_tn(sv["ff"], dxb, 1, tka=FF // NCHIP)
    dh2 = mm_nt_cols([dgate, dup], [w["wg"], w["wu"]])
    g_wg = mm_tn(sv["h2"], dgate, NCHIP, tka=D)
    g_wu = mm_tn(sv["h2"], dup, NCHIP, tka=D)
    dx, dxb, gs["norm2_g"] = rms_bwd(dh2, sv["x_mid"], _row(small["norm2_g"][l]), dx)
    dmix = mm_nt_rows(dxb, w["wout"], tk=D // 2)
    g_wout = mm_tn(sv["mix"], dxb, 1, tka=D // 2)
    da, gs["mix_norm_attn_g"] = attn_norm_bwd(dmix, sv["a"], _row(small["mix_norm_attn_g"][l]))
    parts = [attn_bwd(sv["qk"], sv["proj"], sv["a"], da, sv["lse"], dil) for dil in DILATIONS]
    wst = jnp.swapaxes(small["w_spatial"][l], 1, 2).astype(BF16)
    duv, gs["gmlp_ln_g"], gs["w_spatial"], gs["b_spatial"], gs["mix_norm_gmlp_g"] = gmlp_bwd(
        sv["proj"], dmix, _row(small["gmlp_ln_g"][l]), sv["ws_b"], wst, sv["bias"], _row(small["mix_norm_gmlp_g"][l]))
    dproj = dproj_assemble([p[0] for p in parts], [p[1] for p in parts], [p[2] for p in parts], duv, cos2, sin2)
    dh = mm_nt_cols([dproj], [w["win"]])
    g_win = mm_tn(sv["h"], dproj, NCHIP, tka=D)
    dx, dxb, gs["norm1_g"] = rms_bwd(dh, sv["x"], _row(small["norm1_g"][l]), dx)
    gbig = dict(win=g_win, wout=g_wout.reshape(NCHIP, D // NCHIP, D), wg=g_wg, wu=g_wu,
                wd=g_wd.reshape(NCHIP, FF // NCHIP, D))
    return dx, dxb, gs, gbig


KINDS = ("win", "wout", "wg", "wu", "wd")
ANY = pl.BlockSpec(memory_space=pl.ANY)


def _place():
    return lax.axis_index("x"), lax.axis_index("y"), lax.axis_index("c")


def _other_chips(x, y):
    return [(1 - x, y), (x, 1 - y), (1 - x, 1 - y)]


def _remote(src, dst, send_sem, recv_sem, to):
    return pltpu.make_async_remote_copy(src_ref=src, dst_ref=dst, send_sem=send_sem, recv_sem=recv_sem,
                                        device_id=to, device_id_type=MESH)


def _hbm_call(body, *, name, n_in, out_shape, scratch, in_place=False):
    return pl.pallas_call(body, name=name, out_shape=tuple(out_shape), in_specs=[ANY] * n_in,
                          out_specs=tuple(ANY for _ in out_shape), scratch_shapes=list(scratch),
                          input_output_aliases={k: k for k in range(n_in)} if in_place else {},
                          compiler_params=pltpu.CompilerParams(vmem_limit_bytes=VMEM_LIMIT))


def place_vector():
    x, y, c = _place()
    return jnp.stack([c, 2 * x + y] + [2 * cx + cy for cx, cy in _other_chips(x, y)]).astype(jnp.int32)


PUSH_ROWS = 256


def _push_tiles(src, dst, buf, load_sem, send_sem, recv_sem, to, src_at, dst_at, nt):
    tr = buf.shape[1]
    n = len(src_at) * nt

    def pick(vals, seg):
        out = vals[0]
        for q in range(1, len(vals)):
            out = jnp.where(seg == q, vals[q], out)
        return out

    def tile(ref, at, t):
        seg = t // nt
        row = pl.multiple_of(pick([a[1] for a in at], seg) + (t - seg * nt) * tr, 16)
        return ref.at[pick([a[0] for a in at], seg), pl.ds(row, tr), :]

    def load(t, slot):
        return pltpu.make_async_copy(tile(src, src_at, t), buf.at[slot], load_sem.at[slot])

    def send(t, slot):
        return _remote(buf.at[slot], tile(dst, dst_at, t), send_sem, recv_sem, to)

    load(0, 0).start()

    def step(t, carry):
        slot = t % 2
        load(t, slot).wait()

        @pl.when(t > 0)
        def _():
            send(t - 1, 1 - slot).wait_send()

        @pl.when(t + 1 < n)
        def _():
            load(t + 1, 1 - slot).start()

        send(t, slot).start()
        return carry

    lax.fori_loop(0, n, step, 0)
    send(n - 1, (n - 1) % 2).wait_send()


def _await_tiles(dst, send_sem, recv_sem, to, nseg, rows):
    whole = dst.at[pl.ds(0, nseg), pl.ds(0, rows), :]
    _remote(whole, whole, send_sem, recv_sem, to).wait_recv()


def _push_rows(seg_rows):
    return _row_tile(seg_rows, PUSH_ROWS)


def _push_scratch(arrs, seg_rows):
    return ([pltpu.VMEM((2, _push_rows(r), a.shape[-1]), a.dtype) for a, r in zip(arrs, seg_rows)]
            + [pltpu.SemaphoreType.DMA((len(arrs), 2)), pltpu.SemaphoreType.DMA((len(arrs),)),
               pltpu.SemaphoreType.DMA((len(arrs),))])


def cast_into(w, l, pos, after=None):
    _, r, cols = w.shape
    tr = _row_tile(r)
    extra, extra_specs = _behind(after)

    def body(pos_ref, w_ref, *rest):
        del pos_ref
        rest[-1][...] = w_ref[...].astype(BF16)

    return _call(body, name="cast_into", out_shape=_sds((NCHIP, r, cols), BF16), grid=(r // tr,),
                 in_specs=[pl.BlockSpec((None, tr, cols), lambda i, pos: (l, i, 0))] + extra_specs,
                 out_specs=pl.BlockSpec((None, tr, cols), lambda i, pos: (pos[1], i, 0)),
                 sem=("parallel",), prefetch=1)(pos, w, *extra)


HBM = pl.BlockSpec(memory_space=pltpu.HBM)
SEM = pl.BlockSpec(memory_space=pltpu.SEMAPHORE)
EFFECT = pltpu.SideEffectType.DATAFLOW_SIDE_EFFECTING


def _in_hbm(a):
    return pltpu.with_memory_space_constraint(a, pltpu.HBM)


def _gather_copies(bufs, send, recv):
    x, y, c = _place()
    chips = _other_chips(x, y)

    def half(k, chip):
        hr = bufs[k].shape[1] // 2
        return bufs[k].at[chip, pl.ds(c * hr, hr), :]

    out, back = [], []
    for k in range(len(bufs)):
        for j, (cx, cy) in enumerate(chips):
            i = 3 * k + j
            out.append(_remote(half(k, 2 * x + y), half(k, 2 * x + y), send.at[i], recv.at[i], (cx, cy, c)))
            back.append(_remote(half(k, 2 * x + y), half(k, 2 * cx + cy), send.at[i], recv.at[i], (cx, cy, c)))
    return out, back


def gather_start(fulls):
    nk = len(fulls)

    def body(*refs):
        send, recv = refs[nk], refs[nk + 1]
        bufs, token = refs[nk + 2:2 * nk + 2], refs[2 * nk + 2]
        for cp in _gather_copies(bufs, send, recv)[0]:
            cp.start()
        token[...] = jnp.zeros_like(token)

    res = pl.pallas_call(
        body, name="gather_start",
        out_shape=(pltpu.SemaphoreType.DMA((3 * nk,)), pltpu.SemaphoreType.DMA((3 * nk,)),
                   *[pltpu.HBM(f.shape, f.dtype) for f in fulls], _sds((8, LANES), F32)),
        in_specs=[HBM] * nk, out_specs=(SEM, SEM, *[HBM] * nk, pl.BlockSpec(memory_space=pltpu.VMEM)),
        input_output_aliases={k: 2 + k for k in range(nk)},
        compiler_params=pltpu.CompilerParams(has_side_effects=EFFECT))(*[_in_hbm(f) for f in fulls])
    return res[0], res[1], list(res[2:2 + nk]), res[2 + nk]


def gather_wait(send, recv, fulls, after):
    nk = len(fulls)

    def body(*refs):
        bufs, send_ref, recv_ref = refs[:nk], refs[nk], refs[nk + 1]
        for cp in _gather_copies(bufs, send_ref, recv_ref)[1]:
            cp.wait_send()
            cp.wait_recv()

    return list(pl.pallas_call(
        body, name="gather_wait", out_shape=tuple(pltpu.HBM(f.shape, f.dtype) for f in fulls),
        in_specs=[HBM] * nk + [SEM, SEM, ANY], out_specs=tuple([HBM] * nk),
        input_output_aliases={k: k for k in range(nk)},
        compiler_params=pltpu.CompilerParams(has_side_effects=EFFECT))(*fulls, send, recv, after))


def pair_forward(fulls):
    nk = len(fulls)

    def body(*refs):
        bufs = refs[nk:2 * nk]
        stage = refs[2 * nk:3 * nk]
        load_sem, send_sem, recv_sem = refs[3 * nk:]
        x, y, c = _place()
        sib = (x, y, 1 - c)
        chips = [2 * cx + cy for cx, cy in _other_chips(x, y)]
        for k in range(nk):
            hr = bufs[k].shape[1] // 2
            at = [(chip, c * hr) for chip in chips]
            _push_tiles(bufs[k], bufs[k], stage[k], load_sem.at[k], send_sem.at[k], recv_sem.at[k], sib, at, at,
                        hr // _push_rows(hr))
        for k in range(nk):
            _await_tiles(bufs[k], send_sem.at[k], recv_sem.at[k], sib, 3, bufs[k].shape[1] // 2)

    return _hbm_call(body, name="pair_forward", n_in=nk, out_shape=[_sds(f.shape, f.dtype) for f in fulls], in_place=True,
                     scratch=_push_scratch(fulls, [f.shape[1] // 2 for f in fulls]))(*fulls)


def rs_pair_send(gs):
    nk = len(gs)

    def body(*refs):
        ins, got = refs[:nk], refs[nk:2 * nk]
        stage = refs[2 * nk:3 * nk]
        load_sem, send_sem, recv_sem = refs[3 * nk:]
        x, y, c = _place()
        sib = (x, y, 1 - c)
        for k in range(nk):
            hr = ins[k].shape[1] // 2
            _push_tiles(ins[k], got[k], stage[k], load_sem.at[k], send_sem.at[k], recv_sem.at[k], sib,
                        [(s, (1 - c) * hr) for s in range(NCHIP)], [(s, 0) for s in range(NCHIP)], hr // _push_rows(hr))
        for k in range(nk):
            _await_tiles(got[k], send_sem.at[k], recv_sem.at[k], sib, NCHIP, ins[k].shape[1] // 2)

    half = [_sds((g.shape[0], g.shape[1] // 2, g.shape[2]), g.dtype) for g in gs]
    return _hbm_call(body, name="rs_pair_send", n_in=nk, out_shape=half,
                     scratch=_push_scratch(gs, [g.shape[1] // 2 for g in gs]))(*gs)


def _chip_copies(ps, lands, send, recv):
    x, y, c = _place()
    return [_remote(ps[k].at[2 * cx + cy], lands[k].at[j], send.at[3 * k + j], recv.at[3 * k + j], (cx, cy, c))
            for k in range(len(ps)) for j, (cx, cy) in enumerate(_other_chips(x, y))]


def chip_exchange_start(ps):
    nk = len(ps)
    lands = [lax.empty((3,) + p.shape[1:], p.dtype) for p in ps]

    def body(*refs):
        send, recv = refs[2 * nk], refs[2 * nk + 1]
        srcs, dsts, token = refs[2 * nk + 2:3 * nk + 2], refs[3 * nk + 2:4 * nk + 2], refs[4 * nk + 2]
        for cp in _chip_copies(srcs, dsts, send, recv):
            cp.start()
        token[...] = jnp.zeros_like(token)

    res = pl.pallas_call(
        body, name="chip_exchange_start",
        out_shape=(pltpu.SemaphoreType.DMA((3 * nk,)), pltpu.SemaphoreType.DMA((3 * nk,)),
                   *[pltpu.HBM(a.shape, a.dtype) for a in ps + lands], _sds((8, LANES), F32)),
        in_specs=[HBM] * (2 * nk), out_specs=(SEM, SEM, *[HBM] * (2 * nk), pl.BlockSpec(memory_space=pltpu.VMEM)),
        input_output_aliases={k: 2 + k for k in range(2 * nk)},
        compiler_params=pltpu.CompilerParams(has_side_effects=EFFECT))(*[_in_hbm(a) for a in ps + lands])
    return res[0], res[1], list(res[2:2 + nk]), list(res[2 + nk:2 + 2 * nk]), res[2 + 2 * nk]


def chip_exchange_wait(send, recv, ps, lands, after):
    nk = len(ps)

    def body(*refs):
        srcs, dsts, send_ref, recv_ref = refs[:nk], refs[nk:2 * nk], refs[2 * nk], refs[2 * nk + 1]
        for cp in _chip_copies(srcs, dsts, send_ref, recv_ref):
            cp.wait_send()
            cp.wait_recv()

    res = pl.pallas_call(
        body, name="chip_exchange_wait", out_shape=tuple(pltpu.HBM(a.shape, a.dtype) for a in ps + lands),
        in_specs=[HBM] * (2 * nk) + [SEM, SEM, ANY], out_specs=tuple([HBM] * (2 * nk)),
        input_output_aliases={k: k for k in range(2 * nk)},
        compiler_params=pltpu.CompilerParams(has_side_effects=EFFECT))(*ps, *lands, send, recv, after)
    return list(res[:nk]), list(res[nk:])


def rs_pair_share(fulls):
    nk = len(fulls)

    def body(*refs):
        bufs = refs[nk:2 * nk]
        stage = refs[2 * nk:3 * nk]
        load_sem, send_sem, recv_sem = refs[3 * nk:]
        x, y, c = _place()
        sib = (x, y, 1 - c)
        for k in range(nk):
            hr = bufs[k].shape[1]
            _push_tiles(bufs[k], bufs[k], stage[k], load_sem.at[k], send_sem.at[k], recv_sem.at[k], sib,
                        [(c, 0)], [(c, 0)], hr // _push_rows(hr))
        for k in range(nk):
            _await_tiles(bufs[k], send_sem.at[k], recv_sem.at[k], sib, 1, bufs[k].shape[1])

    return _hbm_call(body, name="rs_pair_share", n_in=nk, out_shape=[_sds(f.shape, f.dtype) for f in fulls], in_place=True,
                     scratch=_push_scratch(fulls, [f.shape[1] for f in fulls]))(*fulls)


N_DEV = 8


def allreduce_small(buf):
    rows = buf.shape[0]
    rp = rows // N_DEV

    def body(in_ref, out_ref, land_ref, send, recv):
        x, y, c = _place()
        mine = pl.ds(pl.multiple_of((4 * x + 2 * y + c) * rp, 8), rp)
        peers = [(x ^ fx, y ^ fy, c ^ fc) for fx in (0, 1) for fy in (0, 1) for fc in (0, 1)][1:]
        block = [pl.ds(pl.multiple_of((4 * px + 2 * py + pc) * rp, 8), rp) for px, py, pc in peers]
        scatter = [_remote(in_ref.at[block[r], :], land_ref.at[r], send.at[r], recv.at[r], peers[r]) for r in range(7)]
        for cp in scatter:
            cp.start()
        for cp in scatter:
            cp.wait()
        acc = in_ref[mine, :]
        for r in range(7):
            acc = acc + land_ref[r]
        out_ref[mine, :] = acc
        spread = [_remote(out_ref.at[mine, :], out_ref.at[mine, :], send.at[7 + r], recv.at[7 + r], peers[r]) for r in range(7)]
        for cp in spread:
            cp.start()
        for r in range(7):
            _remote(out_ref.at[block[r], :], out_ref.at[block[r], :], send.at[7 + r], recv.at[7 + r], peers[r]).wait_recv()
        for cp in spread:
            cp.wait_send()

    vm = pl.BlockSpec(memory_space=pltpu.VMEM)
    return pl.pallas_call(body, name="allreduce_small", out_shape=_sds(buf.shape, F32), in_specs=[vm], out_specs=vm,
                          scratch_shapes=[pltpu.VMEM((7, rp, LANES), F32), pltpu.SemaphoreType.DMA((14,)),
                                          pltpu.SemaphoreType.DMA((14,))])(buf)


def _row_tile(rows, cap=512):
    return max(t for t in range(16, cap + 1, 16) if rows % t == 0)


def add_halves(g, got, pos):
    s, r, cols = g.shape
    hr = r // 2
    tr = _row_tile(hr)
    nt = hr // tr

    def body(pos_ref, a_ref, b_ref, o_ref):
        del pos_ref
        o_ref[...] = (a_ref[...].astype(F32) + b_ref[...].astype(F32)).astype(BF16)

    blk = pl.BlockSpec((None, tr, cols), lambda q, i, pos: (q, i, 0))
    return _call(body, name="add_halves", out_shape=_sds((s, hr, cols), BF16), grid=(s, nt),
                 in_specs=[pl.BlockSpec((None, tr, cols), lambda q, i, pos: (q, pos[0] * nt + i, 0)), blk],
                 out_specs=blk, sem=("parallel", "parallel"), prefetch=1)(pos, g, got)


def add_slots(p, got, pos):
    _, hr, cols = p.shape
    tr = _row_tile(hr)

    def body(pos_ref, o_ref, g0, g1, g2, out_ref):
        del pos_ref
        out_ref[...] = ((o_ref[...].astype(F32) + g0[...].astype(F32)) + g1[...].astype(F32)) + g2[...].astype(F32)

    slots = [pl.BlockSpec((None, tr, cols), functools.partial(lambda j, i, pos: (j, i, 0), j)) for j in range(3)]
    return _call(body, name="add_slots", out_shape=_sds((2, hr, cols), F32), grid=(hr // tr,),
                 in_specs=[pl.BlockSpec((None, tr, cols), lambda i, pos: (pos[1], i, 0))] + slots,
                 out_specs=pl.BlockSpec((None, tr, cols), lambda i, pos: (pos[0], i, 0)),
                 sem=("parallel",), prefetch=1)(pos, p, got, got, got)


def _adamw_math(w, g, m, v):
    nm = B1 * m + (1.0 - B1) * g
    nv = B2 * v + (1.0 - B2) * (g * g)
    m_hat = nm / (1.0 - B1 ** STEP)
    v_hat = nv / (1.0 - B2 ** STEP)
    return -LR * (m_hat / (jnp.sqrt(v_hat) + AEPS) + WD * w), nm, nv


def adamw_layer(w, g, m, v, l, prev):
    nl, r, cols = w.shape
    tr = _row_tile(r, 256)

    def body(w_ref, g_ref, m_ref, v_ref, *rest):
        go_ref, d_ref, nm_ref, nv_ref = rest[-4:]
        gv = g_ref[...]
        d, nm, nv = _adamw_math(w_ref[...], gv, m_ref[...], v_ref[...])
        go_ref[...] = gv
        d_ref[...] = d
        nm_ref[...] = nm
        nv_ref[...] = nv

    lay = pl.BlockSpec((None, tr, cols), lambda i: (l, i, 0))
    o = _sds((nl, r, cols), F32)
    extra = [] if prev is None else list(prev)
    return _call(body, name="adamw_layer", out_shape=(o, o, o, o), grid=(r // tr,),
                 in_specs=[lay, pl.BlockSpec((tr, cols), lambda i: (i, 0)), lay, lay] + [ANY] * len(extra),
                 out_specs=(lay, lay, lay, lay), sem=("parallel",),
                 aliases={4 + j: j for j in range(len(extra))})(w, g, m, v, *extra)


def adamw(w, g, m, v):
    shape = w.shape
    cols = shape[-1]
    rows = w.size // cols
    tr = _row_tile(rows, 256)

    def body(w_ref, g_ref, m_ref, v_ref, d_ref, nm_ref, nv_ref):
        d_ref[...], nm_ref[...], nv_ref[...] = _adamw_math(w_ref[...], g_ref[...], m_ref[...], v_ref[...])

    blk = pl.BlockSpec((tr, cols), lambda i: (i, 0))
    o = _sds((rows, cols), F32)
    outs = _call(body, name="adamw", out_shape=(o, o, o), grid=(rows // tr,), in_specs=[blk] * 4, out_specs=(blk, blk, blk),
                 sem=("parallel",))(*[t.reshape(rows, cols) for t in (w, g, m, v)])
    return [t.reshape(shape) for t in outs]


SMALL = ("norm1_g", "gmlp_ln_g", "w_spatial", "b_spatial", "mix_norm_attn_g", "mix_norm_gmlp_g", "norm2_g", "final_g")
WEIGHTS = ("norm1_g", "w_in", "gmlp_ln_g", "w_spatial", "b_spatial", "mix_norm_attn_g", "mix_norm_gmlp_g", "w_out",
           "norm2_g", "w_gate", "w_up", "w_down", "final_g")
BIG = dict(win="w_in", wout="w_out", wg="w_gate", wu="w_up", wd="w_down")


def _pack(parts):
    flat = jnp.concatenate([parts[n].reshape(-1) for n in SMALL])
    rows = -(-flat.shape[0] // (LANES * 8 * N_DEV)) * 8 * N_DEV
    return jnp.pad(flat, (0, rows * LANES - flat.shape[0])).reshape(rows, LANES)


def _unpack(buf, like):
    flat = buf.reshape(-1)
    out, at = {}, 0
    for n in SMALL:
        out[n] = flat[at:at + like[n].size].reshape(like[n].shape)
        at += like[n].size
    return out


def kernel(x, norm1_g, w_in, gmlp_ln_g, w_spatial, b_spatial, mix_norm_attn_g, mix_norm_gmlp_g, w_out, norm2_g, w_gate, w_up, w_down, final_g, loss_target, m_norm1_g, m_w_in, m_gmlp_ln_g, m_w_spatial, m_b_spatial, m_mix_norm_attn_g, m_mix_norm_gmlp_g, m_w_out, m_norm2_g, m_w_gate, m_w_up, m_w_down, m_final_g, v_norm1_g, v_w_in, v_gmlp_ln_g, v_w_spatial, v_b_spatial, v_mix_norm_attn_g, v_mix_norm_gmlp_g, v_w_out, v_norm2_g, v_w_gate, v_w_up, v_w_down, v_final_g):
    w = dict(norm1_g=norm1_g, w_in=w_in, gmlp_ln_g=gmlp_ln_g, w_spatial=w_spatial, b_spatial=b_spatial,
             mix_norm_attn_g=mix_norm_attn_g, mix_norm_gmlp_g=mix_norm_gmlp_g, w_out=w_out, norm2_g=norm2_g,
             w_gate=w_gate, w_up=w_up, w_down=w_down, final_g=final_g)
    m = dict(norm1_g=m_norm1_g, w_in=m_w_in, gmlp_ln_g=m_gmlp_ln_g, w_spatial=m_w_spatial, b_spatial=m_b_spatial,
             mix_norm_attn_g=m_mix_norm_attn_g, mix_norm_gmlp_g=m_mix_norm_gmlp_g, w_out=m_w_out, norm2_g=m_norm2_g,
             w_gate=m_w_gate, w_up=m_w_up, w_down=m_w_down, final_g=m_final_g)
    v = dict(norm1_g=v_norm1_g, w_in=v_w_in, gmlp_ln_g=v_gmlp_ln_g, w_spatial=v_w_spatial, b_spatial=v_b_spatial,
             mix_norm_attn_g=v_mix_norm_attn_g, mix_norm_gmlp_g=v_mix_norm_gmlp_g, w_out=v_w_out, norm2_g=v_norm2_g,
             w_gate=v_w_gate, w_up=v_w_up, w_down=v_w_down, final_g=v_final_g)

    pos = place_vector()
    small = {n: w[n] for n in SMALL}
    tables = rope_tables(T)

    def start_gather(l, after):
        return gather_start([cast_into(w[BIG[k]], l, pos, after) for k in KINDS])

    def whole(fulls):
        big = dict(zip(KINDS, fulls))
        big["wout"] = big["wout"].reshape(D, D)
        big["wd"] = big["wd"].reshape(FF, D)
        return big

    act = x[0]
    flight = start_gather(0, None)
    fulls = gather_wait(flight[0], flight[1], flight[2], flight[3])
    big, saved = [], []
    for l in range(NL):
        token = None
        if l + 1 < NL:
            flight = start_gather(l + 1, act)
            token = flight[3]
        big.append(whole(pair_forward(fulls)))
        act, sv = layer_fwd(act, small, l, big[l], tables, token)
        saved.append(sv)
        if l + 1 < NL:
            fulls = gather_wait(flight[0], flight[1], flight[2], act)
    loss8, dx, dxb, dfinal = loss_head(act, _row(small["final_g"]), loss_target[0])
    loss = lax.psum(loss8[0, 0], ("x", "y", "c"))

    grads, delta, new_m, new_v = {}, {}, {}, {}
    stacked = {k: None for k in KINDS}
    gs = {k: [None] * NL for k in SMALL[:-1]}

    def finish(flight, after):
        pair, lands = chip_exchange_wait(*flight[:4], after)
        halves = [add_slots(p, got, pos) for p, got in zip(pair, lands)]
        for k, full in zip(KINDS, rs_pair_share(halves)):
            n = BIG[k]
            stacked[k] = adamw_layer(w[n], full.reshape(w[n].shape[1:]), m[n], v[n], flight[5], stacked[k])

    flight = None
    for l in reversed(range(NL)):
        dx, dxb, gsl, gbig = layer_bwd(dx, dxb, small, l, big[l], saved[l], tables, None if flight is None else flight[4])
        for k in gsl:
            gs[k][l] = gsl[k]
        if flight is not None:
            finish(flight, dx)
        partial_grads = [gbig[k] for k in KINDS]
        pair = [add_halves(g, got, pos) for g, got in zip(partial_grads, rs_pair_send(partial_grads))]
        flight = chip_exchange_start(pair) + (l,)
    finish(flight, flight[4])
    for k in KINDS:
        grads[BIG[k]], delta[BIG[k]], new_m[BIG[k]], new_v[BIG[k]] = stacked[k]
    part = {n: jnp.stack(gs[n]).reshape(w[n].shape) for n in SMALL[:-1]}
    part["final_g"] = dfinal.reshape(w["final_g"].shape)
    grads.update(_unpack(allreduce_small(_pack(part)), part))

    sm = adamw(_pack({n: w[n] for n in SMALL}), _pack({n: grads[n] for n in SMALL}), _pack({n: m[n] for n in SMALL}),
               _pack({n: v[n] for n in SMALL}))
    for res, packed in zip((delta, new_m, new_v), sm):
        res.update(_unpack(packed, part))

    return (loss, dx.reshape(x.shape), *[grads[n] for n in WEIGHTS], *[delta[n] for n in WEIGHTS],
            *[new_m[n] for n in WEIGHTS], *[new_v[n] for n in WEIGHTS])
```

```python
import functools

import jax
import jax.numpy as jnp
from jax import lax
from jax.experimental import pallas as pl
from jax.experimental.pallas import tpu as pltpu

F32 = jnp.float32
BF16 = jnp.bfloat16

D = 2048
T = 4096
NL = 4
HD = 128
DA = D // 2
DG = D - DA
NH = DA // HD
NG = DG // 128
CH = 128
DIN = 3 * DA + 2 * DG
FF = 5632
DILATIONS = (1, 4, 16)
NSIDE = 64
ROPE_THETA = 10000.0
EPS = 1e-6
NEG = -1e30
SCALE = HD ** -0.5
NCHIP = 4

LR, B1, B2, AEPS, WD, STEP = 0.001, 0.9, 0.999, 1e-08, 0.01, 10

LANES = 128
VMEM_LIMIT = 56 * 1024 * 1024
MESH = pl.DeviceIdType.MESH


def _call(body, *, name, out_shape, grid=(), in_specs=None, out_specs=None, scratch=(), sem=None, aliases=None,
          prefetch=0):
    params = dict(vmem_limit_bytes=VMEM_LIMIT)
    if sem is not None:
        params["dimension_semantics"] = sem
    if prefetch:
        spec = pltpu.PrefetchScalarGridSpec(num_scalar_prefetch=prefetch, grid=grid, in_specs=in_specs,
                                            out_specs=out_specs, scratch_shapes=list(scratch))
        return pl.pallas_call(body, name=name, out_shape=out_shape, grid_spec=spec,
                              input_output_aliases=aliases or {}, compiler_params=pltpu.CompilerParams(**params))
    kw = {}
    if grid:
        kw["grid"] = grid
    return pl.pallas_call(
        body, name=name, out_shape=out_shape, in_specs=in_specs, out_specs=out_specs,
        scratch_shapes=list(scratch), input_output_aliases=aliases or {},
        compiler_params=pltpu.CompilerParams(**params), **kw)


def _sds(shape, dtype):
    return jax.ShapeDtypeStruct(tuple(shape), dtype)


def _dot(a, b):
    return jnp.dot(a, b, preferred_element_type=F32)


def _dot_nt(a, b):
    return lax.dot_general(a, b, (((1,), (1,)), ((), ())), preferred_element_type=F32)


def _dot_tn(a, b):
    return lax.dot_general(a, b, (((0,), (0,)), ((), ())), preferred_element_type=F32)


def _sigmoid(x):
    return 1.0 / (1.0 + jnp.exp(-x))


def _behind(after):
    return ([], []) if after is None else ([after], [pl.BlockSpec(memory_space=pl.ANY)])


def rms_fwd(x, g, after=None):
    n, d = x.shape
    tm = 256
    extra, extra_specs = _behind(after)

    def body(x_ref, g_ref, *rest):
        h_ref = rest[-1]
        xv = x_ref[...]
        r = lax.rsqrt(jnp.mean(xv * xv, axis=-1, keepdims=True) + EPS)
        h_ref[...] = (xv * r * g_ref[...]).astype(BF16)

    return _call(body, name="rms_fwd", out_shape=_sds((n, d), BF16), grid=(n // tm,),
                 in_specs=[pl.BlockSpec((tm, d), lambda i: (i, 0)), pl.BlockSpec((1, d), lambda i: (0, 0))] + extra_specs,
                 out_specs=pl.BlockSpec((tm, d), lambda i: (i, 0)), sem=("parallel",))(x, g, *extra)


def rms_bwd(dh, x, g, dres):
    n, d = x.shape
    tm = 256

    def body(dh_ref, x_ref, g_ref, dres_ref, dx_ref, dxb_ref, dg_ref):
        @pl.when(pl.program_id(0) == 0)
        def _():
            dg_ref[...] = jnp.zeros_like(dg_ref)

        xv = x_ref[...]
        r = lax.rsqrt(jnp.mean(xv * xv, axis=-1, keepdims=True) + EPS)
        xhat = xv * r
        dhv = dh_ref[...].astype(F32)
        dg_ref[...] += jnp.sum(dhv * xhat, axis=0, keepdims=True)
        dxn = dhv * g_ref[...]
        dx = dres_ref[...] + r * (dxn - xhat * jnp.mean(dxn * xhat, axis=-1, keepdims=True))
        dx_ref[...] = dx
        dxb_ref[...] = dx.astype(BF16)

    row = pl.BlockSpec((tm, d), lambda i: (i, 0))
    vec = pl.BlockSpec((1, d), lambda i: (0, 0))
    return _call(body, name="rms_bwd", out_shape=(_sds((n, d), F32), _sds((n, d), BF16), _sds((1, d), F32)),
                 grid=(n // tm,), in_specs=[row, row, vec, row], out_specs=(row, row, vec), sem=("arbitrary",))(dh, x, g, dres)


def loss_head(x, g, target):
    n, d = x.shape
    tm = 256

    def body(x_ref, g_ref, t_ref, loss_ref, dx_ref, dxb_ref, dg_ref):
        @pl.when(pl.program_id(0) == 0)
        def _():
            dg_ref[...] = jnp.zeros_like(dg_ref)
            loss_ref[...] = jnp.zeros_like(loss_ref)

        xv = x_ref[...]
        r = lax.rsqrt(jnp.mean(xv * xv, axis=-1, keepdims=True) + EPS)
        xhat = xv * r
        gv = g_ref[...]
        err = xhat * gv - t_ref[...]
        loss_ref[...] += 0.5 * jnp.sum(jnp.mean(err * err, axis=-1, keepdims=True))
        dy = err * (1.0 / d)
        dg_ref[...] += jnp.sum(dy * xhat, axis=0, keepdims=True)
        dxn = dy * gv
        dx = r * (dxn - xhat * jnp.mean(dxn * xhat, axis=-1, keepdims=True))
        dx_ref[...] = dx
        dxb_ref[...] = dx.astype(BF16)

    row = pl.BlockSpec((tm, d), lambda i: (i, 0))
    vec = pl.BlockSpec((1, d), lambda i: (0, 0))
    return _call(body, name="loss_head",
                 out_shape=(_sds((8, LANES), F32), _sds((n, d), F32), _sds((n, d), BF16), _sds((1, d), F32)),
                 grid=(n // tm,), in_specs=[row, vec, row],
                 out_specs=(pl.BlockSpec((8, LANES), lambda i: (0, 0)), row, row, vec), sem=("arbitrary",))(x, g, target)


def mm_cols(a, wc, tm=512):
    n, k = a.shape
    s, _, nq = wc.shape

    def body(a_ref, w_ref, o_ref):
        o_ref[...] = _dot(a_ref[...], w_ref[...]).astype(BF16)

    return _call(body, name="mm_cols", out_shape=_sds((n, s * nq), BF16), grid=(s, n // tm),
                 in_specs=[pl.BlockSpec((tm, k), lambda j, i: (i, 0)), pl.BlockSpec((None, k, nq), lambda j, i: (j, 0, 0))],
                 out_specs=pl.BlockSpec((tm, nq), lambda j, i: (i, j)), sem=("parallel", "parallel"))(a, wc)


def mm_gateup(h, wg, wu, tm=512):
    n, k = h.shape
    s, _, nq = wg.shape

    def body(h_ref, wg_ref, wu_ref, gate_ref, up_ref, ff_ref):
        hv = h_ref[...]
        gate = _dot(hv, wg_ref[...])
        up = _dot(hv, wu_ref[...])
        gate_ref[...] = gate.astype(BF16)
        up_ref[...] = up.astype(BF16)
        ff_ref[...] = (gate * _sigmoid(gate) * up).astype(BF16)

    wspec = pl.BlockSpec((None, k, nq), lambda j, i: (j, 0, 0))
    ospec = pl.BlockSpec((tm, nq), lambda j, i: (i, j))
    o = _sds((n, s * nq), BF16)
    return _call(body, name="mm_gateup", out_shape=(o, o, o), grid=(s, n // tm),
                 in_specs=[pl.BlockSpec((tm, k), lambda j, i: (i, 0)), wspec, wspec],
                 out_specs=(ospec, ospec, ospec), sem=("parallel", "parallel"))(h, wg, wu)


def mm_rows_res(a, wr, res, tk, tm=512, tn=1024):
    n, k = a.shape
    _, nn = wr.shape
    nk = k // tk

    def body(a_ref, w_ref, r_ref, o_ref, acc):
        kk = pl.program_id(2)

        @pl.when(kk == 0)
        def _():
            acc[...] = r_ref[...]

        acc[...] += _dot(a_ref[...], w_ref[...])

        @pl.when(kk == nk - 1)
        def _():
            o_ref[...] = acc[...]

    return _call(body, name="mm_rows_res", out_shape=_sds((n, nn), F32), grid=(nn // tn, n // tm, nk),
                 in_specs=[pl.BlockSpec((tm, tk), lambda j, i, kk: (i, kk)), pl.BlockSpec((tk, tn), lambda j, i, kk: (kk, j)),
                           pl.BlockSpec((tm, tn), lambda j, i, kk: (i, j))],
                 out_specs=pl.BlockSpec((tm, tn), lambda j, i, kk: (i, j)),
                 scratch=[pltpu.VMEM((tm, tn), F32)], sem=("parallel", "parallel", "arbitrary"))(a, wr, res)


def mm_nt_rows(dy, wr, tk, tm=512):
    n, nn = dy.shape
    k, _ = wr.shape

    def body(dy_ref, w_ref, o_ref):
        o_ref[...] = _dot_nt(dy_ref[...], w_ref[...]).astype(BF16)

    return _call(body, name="mm_nt_rows", out_shape=_sds((n, k), BF16), grid=(k // tk, n // tm),
                 in_specs=[pl.BlockSpec((tm, nn), lambda j, i: (i, 0)), pl.BlockSpec((tk, nn), lambda j, i: (j, 0))],
                 out_specs=pl.BlockSpec((tm, tk), lambda j, i: (i, j)), sem=("parallel", "parallel"))(dy, wr)


def mm_dff(dy, wd, gate, up, tk, tm=512, after=None):
    n, nn = dy.shape
    k, _ = wd.shape
    extra, extra_specs = _behind(after)

    def body(dy_ref, w_ref, gate_ref, up_ref, *rest):
        dgate_ref, dup_ref = rest[-2:]
        dff = _dot_nt(dy_ref[...], w_ref[...])
        gate = gate_ref[...].astype(F32)
        up = up_ref[...].astype(F32)
        sig = _sigmoid(gate)
        dgate_ref[...] = (dff * up * (sig * (1.0 + gate * (1.0 - sig)))).astype(BF16)
        dup_ref[...] = (dff * (gate * sig)).astype(BF16)

    tile = pl.BlockSpec((tm, tk), lambda j, i: (i, j))
    o = _sds((n, k), BF16)
    return _call(body, name="mm_dff", out_shape=(o, o), grid=(k // tk, n // tm),
                 in_specs=[pl.BlockSpec((tm, nn), lambda j, i: (i, 0)), pl.BlockSpec((tk, nn), lambda j, i: (j, 0)), tile, tile]
                 + extra_specs,
                 out_specs=(tile, tile), sem=("parallel", "parallel"))(dy, wd, gate, up, *extra)


def mm_nt_cols(das, wcs, tm=512):
    n = das[0].shape[0]
    s, k, nq = wcs[0].shape
    npair = len(das)

    def body(*refs):
        da_refs, w_refs, o_ref, acc = refs[:npair], refs[npair:2 * npair], refs[2 * npair], refs[2 * npair + 1]
        ss = pl.program_id(1)

        @pl.when(ss == 0)
        def _():
            acc[...] = jnp.zeros_like(acc)

        for da_ref, w_ref in zip(da_refs, w_refs):
            acc[...] += _dot_nt(da_ref[...], w_ref[...])

        @pl.when(ss == s - 1)
        def _():
            o_ref[...] = acc[...].astype(BF16)

    return _call(body, name="mm_nt_cols%d" % npair, out_shape=_sds((n, k), BF16), grid=(n // tm, s),
                 in_specs=[pl.BlockSpec((tm, nq), lambda i, ss: (i, ss))] * npair
                 + [pl.BlockSpec((None, k, nq), lambda i, ss: (ss, 0, 0))] * npair,
                 out_specs=pl.BlockSpec((tm, k), lambda i, ss: (i, 0)),
                 scratch=[pltpu.VMEM((tm, k), F32)], sem=("parallel", "arbitrary"))(*das, *wcs)


def mm_tn(a, b, s, tka, tt=512):
    n, ka = a.shape
    nq = b.shape[1] // s
    nt = n // tt

    def body(a_ref, b_ref, o_ref, acc):
        t = pl.program_id(2)

        @pl.when(t == 0)
        def _():
            acc[...] = jnp.zeros_like(acc)

        acc[...] += _dot_tn(a_ref[...], b_ref[...])

        @pl.when(t == nt - 1)
        def _():
            o_ref[...] = acc[...].astype(BF16)

    return _call(body, name="mm_tn", out_shape=_sds((s, ka, nq), BF16), grid=(ka // tka, s, nt),
                 in_specs=[pl.BlockSpec((tt, tka), lambda i, j, t: (t, i)), pl.BlockSpec((tt, nq), lambda i, j, t: (t, j))],
                 out_specs=pl.BlockSpec((None, tka, nq), lambda i, j, t: (j, i, 0)),
                 scratch=[pltpu.VMEM((tka, nq), F32)], sem=("parallel", "parallel", "arbitrary"))(a, b)


def rope_tables(n):
    pos = jnp.arange(n, dtype=F32)
    inv = ROPE_THETA ** (-jnp.arange(0, HD, 2, dtype=F32) / HD)
    ang = pos[:, None] * inv[None, :]
    cos, sin = jnp.cos(ang), jnp.sin(ang)
    return jnp.concatenate([cos, cos], axis=-1), jnp.concatenate([-sin, sin], axis=-1)


PERM_ROWS = 256


def perm_matrix(d):
    n = PERM_ROWS // d
    i = jnp.arange(PERM_ROWS)
    src = (i % n) * d + i // n
    return (src[:, None] == jnp.arange(PERM_ROWS)[None, :]).astype(BF16)


def _perm_f32(p, x):
    hi = x.astype(BF16)
    rest = x - hi.astype(F32)
    mid = rest.astype(BF16)
    lo = (rest - mid.astype(F32)).astype(BF16)
    return (_dot(p, hi) + _dot(p, mid)) + _dot(p, lo)


def _res_spec(d, cols, col_block=0):
    return pl.BlockSpec((d, PERM_ROWS // d, cols), lambda i: (0, i, col_block))


def _perm_specs():
    return [pl.BlockSpec((PERM_ROWS, PERM_ROWS), lambda i: (0, 0))] * (len(DILATIONS) - 1)


def qkv_split(proj, cos2, sin2, perms):
    n = proj.shape[0]
    tm = PERM_ROWS

    def body(p_ref, c_ref, s_ref, *rest):
        perm_refs, outs = rest[:len(DILATIONS) - 1], rest[len(DILATIONS) - 1:]
        c, s = c_ref[...], s_ref[...]
        for h in range(2 * NH):
            t = p_ref[:, h * HD:(h + 1) * HD].astype(F32)
            outs[0][0, :, h * HD:(h + 1) * HD] = (t * c + pltpu.roll(t, HD // 2, 1) * s).astype(BF16)
        outs[0][0, :, 2 * DA:] = p_ref[:, 2 * DA:]
        nat = outs[0][0]
        for o_ref, p_ref_d, d in zip(outs[1:], perm_refs, DILATIONS[1:]):
            o_ref[...] = _dot(p_ref_d[...], nat).astype(BF16).reshape(d, tm // d, 3 * DA)

    tab = pl.BlockSpec((tm, HD), lambda i: (i, 0))
    return _call(body, name="qkv_split", out_shape=tuple(_sds((d, n // d, 3 * DA), BF16) for d in DILATIONS),
                 grid=(n // tm,), in_specs=[pl.BlockSpec((tm, 3 * DA), lambda i: (i, 0)), tab, tab] + _perm_specs(),
                 out_specs=tuple(_res_spec(d, 3 * DA) for d in DILATIONS), sem=("parallel",))(proj, cos2, sin2, *perms)


def dproj_assemble(dqs, dks, dvs, duv, cos2, sin2, perms_t):
    n = duv.shape[0]
    tm = PERM_ROWS
    npat = len(DILATIONS)

    def body(*refs):
        parts = refs[0:npat], refs[npat:2 * npat], refs[2 * npat:3 * npat]
        duv_ref, c_ref, s_ref = refs[3 * npat:3 * npat + 3]
        pt_refs, o_ref = refs[3 * npat + 3:-1], refs[-1]
        c, s = c_ref[...], s_ref[...]
        for part, part_refs in enumerate(parts):
            t = part_refs[0][0].astype(F32)
            for ref, pt_ref in zip(part_refs[1:], pt_refs):
                t = t + _dot(pt_ref[...], ref[...].reshape(tm, DA))
            if part == 2:
                o_ref[:, 2 * DA:3 * DA] = t.astype(BF16)
                continue
            for h in range(NH):
                th = t[:, h * HD:(h + 1) * HD]
                o_ref[:, part * DA + h * HD:part * DA + (h + 1) * HD] = (th * c - pltpu.roll(th, HD // 2, 1) * s).astype(BF16)
        o_ref[:, 3 * DA:] = duv_ref[...]

    blks = [_res_spec(d, DA) for d in DILATIONS]
    tab = pl.BlockSpec((tm, HD), lambda i: (i, 0))
    return _call(body, name="dproj_assemble", out_shape=_sds((n, DIN), BF16), grid=(n // tm,),
                 in_specs=blks * 3 + [pl.BlockSpec((tm, 2 * DG), lambda i: (i, 0)), tab, tab] + _perm_specs(),
                 out_specs=pl.BlockSpec((tm, DIN), lambda i: (i, 0)), sem=("parallel",))(
        *dqs, *dks, *dvs, duv, cos2, sin2, *perms_t)


ATTN_BLOCK_BYTES = 2 << 20


def _attn_tiles(ls):
    tq = min(128, ls)
    w = min(tq + 2 * NSIDE, ls)
    return tq, w, ls // tq


def _attn_heads(ls):
    return NH if ls * DA * 2 <= ATTN_BLOCK_BYTES else 1


ATTN_UNROLL = 8


def _attn_unroll(nt):
    return min(nt, ATTN_UNROLL)


def _band(t, tq, w, ls):
    q0 = pl.multiple_of(t * tq, tq)
    ks = pl.multiple_of(jnp.clip(t * tq - NSIDE, 0, ls - w), NSIDE)
    qpos = q0 + lax.broadcasted_iota(jnp.int32, (tq, w), 0)
    kpos = ks + lax.broadcasted_iota(jnp.int32, (tq, w), 1)
    return q0, ks, jnp.abs(kpos - qpos) <= NSIDE


def _attn_specs(ls, hb):
    nhb = NH // hb
    q = pl.BlockSpec((None, ls, hb * HD), lambda g: (g // nhb, 0, g % nhb))
    k = pl.BlockSpec((None, ls, hb * HD), lambda g: (g // nhb, 0, nhb + g % nhb))
    v = pl.BlockSpec((None, ls, hb * HD), lambda g: (g // nhb, 0, 2 * nhb + g % nhb))
    return q, k, v, q


def attn_fwd(qkv):
    d, ls, _ = qkv.shape
    tq, w, nt = _attn_tiles(ls)
    hb = _attn_heads(ls)

    def body(q_ref, k_ref, v_ref, o_ref, l_ref):
        for h in range(hb):
            cols = slice(h * HD, (h + 1) * HD)

            def tile(t, carry, cols=cols):
                q0, ks, valid = _band(t, tq, w, ls)
                s = _dot_nt(q_ref[pl.ds(q0, tq), cols], k_ref[pl.ds(ks, w), cols]) * SCALE
                s = jnp.where(valid, s, NEG)
                m = jnp.max(s, axis=1, keepdims=True)
                p = jnp.exp(s - m)
                l = jnp.sum(p, axis=1, keepdims=True)
                o = _dot(p.astype(BF16), v_ref[pl.ds(ks, w), cols]) / l
                o_ref[pl.ds(q0, tq), cols] = o.astype(BF16)
                l_ref[pl.ds(q0, tq), cols] = jnp.broadcast_to(m + jnp.log(l), (tq, HD))
                return carry

            lax.fori_loop(0, nt, tile, 0, unroll=_attn_unroll(nt))

    q, k, v, o = _attn_specs(ls, hb)
    return _call(body, name="attn_fwd_d%d" % d, out_shape=(_sds((d, ls, DA), BF16), _sds((d, ls, DA), F32)),
                 grid=(d * NH // hb,), in_specs=[q, k, v], out_specs=(o, o), sem=("parallel",))(qkv, qkv, qkv)


def attn_merge(os_, ls_, ga, perms_t):
    n = os_[0].shape[1]
    tm = PERM_ROWS
    npat = len(DILATIONS)

    def body(*refs):
        o_refs, l_refs = refs[:npat], refs[npat:2 * npat]
        g_ref, pt_refs = refs[2 * npat], refs[2 * npat + 1:3 * npat]
        a_ref, lse_ref, mix_ref = refs[3 * npat:]
        ov = [o_refs[0][0].astype(F32)] + [_dot(pt[...], o[...].reshape(tm, DA)) for o, pt in zip(o_refs[1:], pt_refs)]
        lv = [l_refs[0][0]] + [_perm_f32(pt[...], l[...].reshape(tm, DA)) for l, pt in zip(l_refs[1:], pt_refs)]
        lmax = functools.reduce(jnp.maximum, lv)
        e = [jnp.exp(x - lmax) for x in lv]
        den = functools.reduce(lambda p, q: p + q, e)
        a = functools.reduce(lambda p, q: p + q, [ei * oi for ei, oi in zip(e, ov)]) / den
        a_ref[...] = a.astype(BF16)
        lse_ref[...] = lmax + jnp.log(den)
        r = lax.rsqrt(jnp.mean(a * a, axis=-1, keepdims=True) + EPS)
        mix_ref[...] = (a * r * g_ref[...]).astype(BF16)

    blk = pl.BlockSpec((tm, DA), lambda i: (i, 0))
    res = [_res_spec(d, DA) for d in DILATIONS]
    return _call(body, name="attn_merge", out_shape=(_sds((n, DA), BF16), _sds((n, DA), F32), _sds((n, D), BF16)),
                 grid=(n // tm,), in_specs=res * 2 + [pl.BlockSpec((1, DA), lambda i: (0, 0))] + _perm_specs(),
                 out_specs=(blk, blk, blk), sem=("parallel",))(*os_, *ls_, ga, *perms_t)


def attn_norm_bwd(dmix, a, lse, ga, perms):
    n = a.shape[0]
    tm = PERM_ROWS
    npat = len(DILATIONS)

    def body(dm_ref, a_ref, lse_ref, g_ref, *rest):
        p_refs = rest[:npat - 1]
        da_refs, a_refs, lse_refs = rest[npat - 1:2 * npat - 1], rest[2 * npat - 1:3 * npat - 2], rest[3 * npat - 2:4 * npat - 3]
        dg_ref = rest[-1]

        @pl.when(pl.program_id(0) == 0)
        def _():
            dg_ref[...] = jnp.zeros_like(dg_ref)

        ab = a_ref[...]
        av = ab.astype(F32)
        r = lax.rsqrt(jnp.mean(av * av, axis=-1, keepdims=True) + EPS)
        ahat = av * r
        dm = dm_ref[...].astype(F32)
        dg_ref[...] += jnp.sum(dm * ahat, axis=0, keepdims=True)
        dn = dm * g_ref[...]
        da = (r * (dn - ahat * jnp.mean(dn * ahat, axis=-1, keepdims=True))).astype(BF16)
        da_refs[0][0] = da
        lv = lse_ref[...]
        for i, d in enumerate(DILATIONS[1:]):
            p = p_refs[i][...]
            da_refs[i + 1][...] = _dot(p, da).astype(BF16).reshape(d, tm // d, DA)
            a_refs[i][...] = _dot(p, ab).astype(BF16).reshape(d, tm // d, DA)
            lse_refs[i][...] = _perm_f32(p, lv).reshape(d, tm // d, DA)

    blk = pl.BlockSpec((tm, DA), lambda i: (i, 0))
    vec = pl.BlockSpec((1, DA), lambda i: (0, 0))
    res = [_res_spec(d, DA) for d in DILATIONS]
    outs = _call(body, name="attn_norm_bwd",
                 out_shape=tuple([_sds((d, n // d, DA), BF16) for d in DILATIONS]
                                 + [_sds((d, n // d, DA), BF16) for d in DILATIONS[1:]]
                                 + [_sds((d, n // d, DA), F32) for d in DILATIONS[1:]] + [_sds((1, DA), F32)]),
                 grid=(n // tm,), in_specs=[blk, blk, blk, vec] + _perm_specs(),
                 out_specs=tuple(res + res[1:] + res[1:] + [vec]), sem=("arbitrary",))(dmix, a, lse, ga, *perms)
    return outs[:npat], outs[npat:2 * npat - 1], outs[2 * npat - 1:3 * npat - 2], outs[-1]


def attn_bwd(qkv, a, da, lse):
    d, ls, _ = qkv.shape
    tq, w, nt = _attn_tiles(ls)
    hb = _attn_heads(ls)

    def body(q_ref, k_ref, v_ref, a_ref, da_ref, lse_ref, dq_ref, dk_ref, dv_ref, dk_acc, dv_acc):
        dk_acc[...] = jnp.zeros_like(dk_acc)
        dv_acc[...] = jnp.zeros_like(dv_acc)
        for h in range(hb):
            cols = slice(h * HD, (h + 1) * HD)

            def tile(t, carry, cols=cols, lse_col=slice(h * HD, h * HD + 1)):
                q0, ks, valid = _band(t, tq, w, ls)
                q = q_ref[pl.ds(q0, tq), cols]
                k = k_ref[pl.ds(ks, w), cols]
                v = v_ref[pl.ds(ks, w), cols]
                do = da_ref[pl.ds(q0, tq), cols]
                s = jnp.where(valid, _dot_nt(q, k) * SCALE, NEG)
                p = jnp.exp(s - lse_ref[pl.ds(q0, tq), lse_col])
                drow = jnp.sum(do.astype(F32) * a_ref[pl.ds(q0, tq), cols].astype(F32), axis=1, keepdims=True)
                ds = (p * (_dot_nt(do, v) - drow) * SCALE).astype(BF16)
                dv_acc[pl.ds(ks, w), cols] += _dot_tn(p.astype(BF16), do)
                dk_acc[pl.ds(ks, w), cols] += _dot_tn(ds, q)
                dq_ref[pl.ds(q0, tq), cols] = _dot(ds, k).astype(BF16)
                return carry

            lax.fori_loop(0, nt, tile, 0, unroll=_attn_unroll(nt))
        dk_ref[...] = dk_acc[...].astype(BF16)
        dv_ref[...] = dv_acc[...].astype(BF16)

    q, k, v, o = _attn_specs(ls, hb)
    out = _sds((d, ls, DA), BF16)
    return _call(body, name="attn_bwd_d%d" % d, out_shape=(out, out, out), grid=(d * NH // hb,),
                 in_specs=[q, k, v, o, o, o], out_specs=(o, o, o),
                 scratch=[pltpu.VMEM((ls, hb * HD), F32), pltpu.VMEM((ls, hb * HD), F32)], sem=("parallel",))(
        qkv, qkv, qkv, a, da, lse)


GM_TM = 256
INV_SQRT2 = 0.7071067811865476
INV_SQRT2PI = 0.3989422804014327


def _gelu(x):
    return 0.5 * x * (1.0 + lax.erf(x * INV_SQRT2))


def _gelu_grad(x):
    return 0.5 * (1.0 + lax.erf(x * INV_SQRT2)) + x * (INV_SQRT2PI * jnp.exp(-0.5 * x * x))


def _gmlp_forward(up, vp, ln_g, ws_ref, bias):
    u = _gelu(up)
    v = _gelu(vp)
    vc = v - jnp.mean(v, axis=-1, keepdims=True)
    rs = lax.rsqrt(jnp.mean(vc * vc, axis=-1, keepdims=True) + EPS)
    vhat = vc * rs
    vln = (vhat * ln_g).astype(BF16)
    rows = []
    for c in range(GM_TM // CH):
        cols = [_dot(ws_ref[g], vln[c * CH:(c + 1) * CH, g * 128:(g + 1) * 128]) for g in range(NG)]
        rows.append(jnp.concatenate(cols, axis=1) + bias)
    return u, vhat, rs, vln, jnp.concatenate(rows, axis=0)


def _gmlp_specs():
    ublk = pl.BlockSpec((GM_TM, DG), lambda i: (i, 3 * DA // DG))
    vblk = pl.BlockSpec((GM_TM, DG), lambda i: (i, 3 * DA // DG + 1))
    vec = pl.BlockSpec((1, DG), lambda i: (0, 0))
    wsp = pl.BlockSpec((NG, CH, CH), lambda i: (0, 0, 0))
    bsp = pl.BlockSpec((CH, DG), lambda i: (0, 0))
    return ublk, vblk, vec, wsp, bsp


def gmlp_fwd(proj, ln_g, ws, bias, gg, mix):
    n = proj.shape[0]

    def body(up_ref, vp_ref, ln_ref, ws_ref, b_ref, gg_ref, mix_in, mix_ref):
        del mix_in
        u, _, _, _, mixed = _gmlp_forward(up_ref[...].astype(F32), vp_ref[...].astype(F32), ln_ref[...], ws_ref, b_ref[...])
        gout = u * mixed
        r = lax.rsqrt(jnp.mean(gout * gout, axis=-1, keepdims=True) + EPS)
        mix_ref[...] = (gout * r * gg_ref[...]).astype(BF16)

    ublk, vblk, vec, wsp, bsp = _gmlp_specs()
    return _call(body, name="gmlp_fwd", out_shape=_sds((n, D), BF16), grid=(n // GM_TM,),
                 in_specs=[ublk, vblk, vec, wsp, bsp, vec, pl.BlockSpec(memory_space=pl.ANY)],
                 out_specs=pl.BlockSpec((GM_TM, DG), lambda i: (i, DA // DG)), sem=("parallel",), aliases={6: 0})(
        proj, proj, ln_g, ws, bias, gg, mix)


def gmlp_bwd(proj, dmix, ln_g, ws, wst, bias, gg):
    n = proj.shape[0]

    def body(up_ref, vp_ref, dm_ref, ln_ref, ws_ref, wst_ref, b_ref, gg_ref, duv_ref, dln_ref, dws_ref, dbs_ref, dgg_ref,
             db_ref):
        @pl.when(pl.program_id(0) == 0)
        def _():
            dln_ref[...] = jnp.zeros_like(dln_ref)
            dws_ref[...] = jnp.zeros_like(dws_ref)
            db_ref[...] = jnp.zeros_like(db_ref)
            dgg_ref[...] = jnp.zeros_like(dgg_ref)

        up = up_ref[...].astype(F32)
        vp = vp_ref[...].astype(F32)
        ln_g = ln_ref[...]
        u, vhat, rs, vln, mixed = _gmlp_forward(up, vp, ln_g, ws_ref, b_ref[...])
        gout = u * mixed
        r = lax.rsqrt(jnp.mean(gout * gout, axis=-1, keepdims=True) + EPS)
        ghat = gout * r
        dm = dm_ref[...].astype(F32)
        dgg_ref[...] += jnp.sum(dm * ghat, axis=0, keepdims=True)
        dn = dm * gg_ref[...]
        dgout = r * (dn - ghat * jnp.mean(dn * ghat, axis=-1, keepdims=True))
        du = dgout * mixed
        dmixed = dgout * u
        dmb = dmixed.astype(BF16)
        rows = []
        for c in range(GM_TM // CH):
            rsl = slice(c * CH, (c + 1) * CH)
            db_ref[...] += dmixed[rsl, :]
            cols = []
            for g in range(NG):
                csl = slice(g * 128, (g + 1) * 128)
                dws_ref[g] += _dot_nt(dmb[rsl, csl], vln[rsl, csl])
                cols.append(_dot(wst_ref[g], dmb[rsl, csl]))
            rows.append(jnp.concatenate(cols, axis=1))
        dvln = jnp.concatenate(rows, axis=0)
        dln_ref[...] += jnp.sum(dvln * vhat, axis=0, keepdims=True)
        dvh = dvln * ln_g
        dv = rs * (dvh - jnp.mean(dvh, axis=-1, keepdims=True) - vhat * jnp.mean(dvh * vhat, axis=-1, keepdims=True))
        duv_ref[:, 0:DG] = (du * _gelu_grad(up)).astype(BF16)
        duv_ref[:, DG:] = (dv * _gelu_grad(vp)).astype(BF16)

        @pl.when(pl.program_id(0) == pl.num_programs(0) - 1)
        def _():
            for g in range(NG):
                dbs_ref[g:g + 1, :] = jnp.sum(jnp.transpose(db_ref[:, g * 128:(g + 1) * 128]), axis=0, keepdims=True)

    ublk, vblk, vec, wsp, bsp = _gmlp_specs()
    return _call(body, name="gmlp_bwd",
                 out_shape=(_sds((n, 2 * DG), BF16), _sds((1, DG), F32), _sds((NG, CH, CH), F32), _sds((NG, CH), F32),
                            _sds((1, DG), F32)),
                 grid=(n // GM_TM,),
                 in_specs=[ublk, vblk, pl.BlockSpec((GM_TM, DG), lambda i: (i, DA // DG)), vec, wsp, wsp, bsp, vec],
                 out_specs=(pl.BlockSpec((GM_TM, 2 * DG), lambda i: (i, 0)), vec, wsp,
                            pl.BlockSpec((NG, CH), lambda i: (0, 0)), vec),
                 scratch=[pltpu.VMEM((CH, DG), F32)], sem=("arbitrary",))(
        proj, proj, dmix, ln_g, ws, wst, bias, gg)


def _row(v):
    return v.reshape(1, -1)


def local_step(x, target, small, big):
    tables = step_tables()
    saved = []
    for l in range(NL):
        x, sv = layer_fwd(x, small, l, big[l], tables)
        saved.append(sv)
    loss8, dx, dxb, dfinal = loss_head(x, _row(small["final_g"]), target)
    gs = {k: [None] * NL for k in SMALL[:-1]}
    gbig = [None] * NL
    for l in reversed(range(NL)):
        dx, dxb, gsl, gbig[l] = layer_bwd(dx, dxb, small, l, big[l], saved[l], tables)
        for k in gsl:
            gs[k][l] = gsl[k]
    return loss8, dx, gs, dfinal, gbig


def step_tables():
    perms = [perm_matrix(d) for d in DILATIONS[1:]]
    return rope_tables(T) + (perms, [p.T for p in perms])


def layer_fwd(x, small, l, w, tables, after=None):
    cos2, sin2, perms, perms_t = tables
    ws_b = small["w_spatial"][l].astype(BF16)
    bias = jnp.repeat(small["b_spatial"][l].T, 128, axis=1)
    h = rms_fwd(x, _row(small["norm1_g"][l]), after)
    proj = mm_cols(h, w["win"])
    qkvs = qkv_split(proj, cos2, sin2, perms)
    outs = [attn_fwd(qkv) for qkv in qkvs]
    a, lse, mix = attn_merge([o for o, _ in outs], [s for _, s in outs], _row(small["mix_norm_attn_g"][l]), perms_t)
    mix = gmlp_fwd(proj, _row(small["gmlp_ln_g"][l]), ws_b, bias, _row(small["mix_norm_gmlp_g"][l]), mix)
    x_mid = mm_rows_res(mix, w["wout"], x, tk=D)
    h2 = rms_fwd(x_mid, _row(small["norm2_g"][l]))
    gate, up, ff = mm_gateup(h2, w["wg"], w["wu"])
    x_out = mm_rows_res(ff, w["wd"], x_mid, tk=FF // NCHIP)
    return x_out, dict(x=x, h=h, proj=proj, qkvs=qkvs, a=a, lse=lse, mix=mix, x_mid=x_mid, h2=h2, gate=gate, up=up, ff=ff,
                       ws_b=ws_b, bias=bias)


def layer_bwd(dx, dxb, small, l, w, sv, tables, after=None):
    cos2, sin2, perms, perms_t = tables
    gs = {}
    dgate, dup = mm_dff(dxb, w["wd"], sv["gate"], sv["up"], tk=FF // NCHIP, after=after)
    g_wd = mm_tn(sv["ff"], dxb, 1, tka=FF // NCHIP)
    dh2 = mm_nt_cols([dgate, dup], [w["wg"], w["wu"]])
    g_wg = mm_tn(sv["h2"], dgate, NCHIP, tka=D)
    g_wu = mm_tn(sv["h2"], dup, NCHIP, tka=D)
    dx, dxb, gs["norm2_g"] = rms_bwd(dh2, sv["x_mid"], _row(small["norm2_g"][l]), dx)
    dmix = mm_nt_rows(dxb, w["wout"], tk=D // 2)
    g_wout = mm_tn(sv["mix"], dxb, 1, tka=D // 2)
    das, a_res, lse_res, gs["mix_norm_attn_g"] = attn_norm_bwd(dmix, sv["a"], sv["lse"],
                                                                _row(small["mix_norm_attn_g"][l]), perms)
    a_all = [sv["a"].reshape((1,) + sv["a"].shape)] + list(a_res)
    lse_all = [sv["lse"].reshape((1,) + sv["lse"].shape)] + list(lse_res)
    parts = [attn_bwd(*operands) for operands in zip(sv["qkvs"], a_all, das, lse_all)]
    wst = jnp.swapaxes(small["w_spatial"][l], 1, 2).astype(BF16)
    duv, gs["gmlp_ln_g"], gs["w_spatial"], gs["b_spatial"], gs["mix_norm_gmlp_g"] = gmlp_bwd(
        sv["proj"], dmix, _row(small["gmlp_ln_g"][l]), sv["ws_b"], wst, sv["bias"], _row(small["mix_norm_gmlp_g"][l]))
    dproj = dproj_assemble([p[0] for p in parts], [p[1] for p in parts], [p[2] for p in parts], duv, cos2, sin2,
                           perms_t)
    dh = mm_nt_cols([dproj], [w["win"]])
    g_win = mm_tn(sv["h"], dproj, NCHIP, tka=D)
    dx, dxb, gs["norm1_g"] = rms_bwd(dh, sv["x"], _row(small["norm1_g"][l]), dx)
    gbig = dict(win=g_win, wout=g_wout.reshape(NCHIP, D // NCHIP, D), wg=g_wg, wu=g_wu,
                wd=g_wd.reshape(NCHIP, FF // NCHIP, D))
    return dx, dxb, gs, gbig


KINDS = ("win", "wout", "wg", "wu", "wd")
ANY = pl.BlockSpec(memory_space=pl.ANY)


def _place():
    return lax.axis_index("x"), lax.axis_index("y"), lax.axis_index("c")


def _other_chips(x, y):
    return [(1 - x, y), (x, 1 - y), (1 - x, 1 - y)]


def _remote(src, dst, send_sem, recv_sem, to):
    return pltpu.make_async_remote_copy(src_ref=src, dst_ref=dst, send_sem=send_sem, recv_sem=recv_sem,
                                        device_id=to, device_id_type=MESH)


def _hbm_call(body, *, name, n_in, out_shape, scratch, in_place=False):
    return pl.pallas_call(body, name=name, out_shape=tuple(out_shape), in_specs=[ANY] * n_in,
                          out_specs=tuple(ANY for _ in out_shape), scratch_shapes=list(scratch),
                          input_output_aliases={k: k for k in range(n_in)} if in_place else {},
                          compiler_params=pltpu.CompilerParams(vmem_limit_bytes=VMEM_LIMIT))


def place_vector():
    x, y, c = _place()
    return jnp.stack([c, 2 * x + y] + [2 * cx + cy for cx, cy in _other_chips(x, y)]).astype(jnp.int32)


PUSH_ROWS = 256


def _push_tiles(src, dst, buf, load_sem, send_sem, recv_sem, to, src_at, dst_at, nt):
    tr = buf.shape[1]
    n = len(src_at) * nt

    def pick(vals, seg):
        out = vals[0]
        for q in range(1, len(vals)):
            out = jnp.where(seg == q, vals[q], out)
        return out

    def tile(ref, at, t):
        seg = t // nt
        row = pl.multiple_of(pick([a[1] for a in at], seg) + (t - seg * nt) * tr, 16)
        return ref.at[pick([a[0] for a in at], seg), pl.ds(row, tr), :]

    def load(t, slot):
        return pltpu.make_async_copy(tile(src, src_at, t), buf.at[slot], load_sem.at[slot])

    def send(t, slot):
        return _remote(buf.at[slot], tile(dst, dst_at, t), send_sem, recv_sem, to)

    load(0, 0).start()

    def step(t, carry):
        slot = t % 2
        load(t, slot).wait()

        @pl.when(t > 0)
        def _():
            send(t - 1, 1 - slot).wait_send()

        @pl.when(t + 1 < n)
        def _():
            load(t + 1, 1 - slot).start()

        send(t, slot).start()
        return carry

    lax.fori_loop(0, n, step, 0)
    send(n - 1, (n - 1) % 2).wait_send()


def _await_tiles(dst, send_sem, recv_sem, to, nseg, rows):
    whole = dst.at[pl.ds(0, nseg), pl.ds(0, rows), :]
    _remote(whole, whole, send_sem, recv_sem, to).wait_recv()


def _push_rows(seg_rows):
    return _row_tile(seg_rows, PUSH_ROWS)


def _push_scratch(arrs, seg_rows):
    return ([pltpu.VMEM((2, _push_rows(r), a.shape[-1]), a.dtype) for a, r in zip(arrs, seg_rows)]
            + [pltpu.SemaphoreType.DMA((len(arrs), 2)), pltpu.SemaphoreType.DMA((len(arrs),)),
               pltpu.SemaphoreType.DMA((len(arrs),))])


def cast_into(w, l, pos, after=None):
    _, r, cols = w.shape
    tr = _row_tile(r)
    extra, extra_specs = _behind(after)

    def body(pos_ref, w_ref, *rest):
        del pos_ref
        rest[-1][...] = w_ref[...].astype(BF16)

    return _call(body, name="cast_into", out_shape=_sds((NCHIP, r, cols), BF16), grid=(r // tr,),
                 in_specs=[pl.BlockSpec((None, tr, cols), lambda i, pos: (l, i, 0))] + extra_specs,
                 out_specs=pl.BlockSpec((None, tr, cols), lambda i, pos: (pos[1], i, 0)),
                 sem=("parallel",), prefetch=1)(pos, w, *extra)


HBM = pl.BlockSpec(memory_space=pltpu.HBM)
SEM = pl.BlockSpec(memory_space=pltpu.SEMAPHORE)
EFFECT = pltpu.SideEffectType.DATAFLOW_SIDE_EFFECTING


def _in_hbm(a):
    return pltpu.with_memory_space_constraint(a, pltpu.HBM)


def _gather_copies(bufs, send, recv):
    x, y, c = _place()
    chips = _other_chips(x, y)

    def half(k, chip):
        hr = bufs[k].shape[1] // 2
        return bufs[k].at[chip, pl.ds(c * hr, hr), :]

    out, back = [], []
    for k in range(len(bufs)):
        for j, (cx, cy) in enumerate(chips):
            i = 3 * k + j
            out.append(_remote(half(k, 2 * x + y), half(k, 2 * x + y), send.at[i], recv.at[i], (cx, cy, c)))
            back.append(_remote(half(k, 2 * x + y), half(k, 2 * cx + cy), send.at[i], recv.at[i], (cx, cy, c)))
    return out, back


def gather_start(fulls):
    nk = len(fulls)

    def body(*refs):
        send, recv = refs[nk], refs[nk + 1]
        bufs, token = refs[nk + 2:2 * nk + 2], refs[2 * nk + 2]
        for cp in _gather_copies(bufs, send, recv)[0]:
            cp.start()
        token[...] = jnp.zeros_like(token)

    res = pl.pallas_call(
        body, name="gather_start",
        out_shape=(pltpu.SemaphoreType.DMA((3 * nk,)), pltpu.SemaphoreType.DMA((3 * nk,)),
                   *[pltpu.HBM(f.shape, f.dtype) for f in fulls], _sds((8, LANES), F32)),
        in_specs=[HBM] * nk, out_specs=(SEM, SEM, *[HBM] * nk, pl.BlockSpec(memory_space=pltpu.VMEM)),
        input_output_aliases={k: 2 + k for k in range(nk)},
        compiler_params=pltpu.CompilerParams(has_side_effects=EFFECT))(*[_in_hbm(f) for f in fulls])
    return res[0], res[1], list(res[2:2 + nk]), res[2 + nk]


def gather_wait(send, recv, fulls, after):
    nk = len(fulls)

    def body(*refs):
        bufs, send_ref, recv_ref = refs[:nk], refs[nk], refs[nk + 1]
        for cp in _gather_copies(bufs, send_ref, recv_ref)[1]:
            cp.wait_send()
            cp.wait_recv()

    return list(pl.pallas_call(
        body, name="gather_wait", out_shape=tuple(pltpu.HBM(f.shape, f.dtype) for f in fulls),
        in_specs=[HBM] * nk + [SEM, SEM, ANY], out_specs=tuple([HBM] * nk),
        input_output_aliases={k: k for k in range(nk)},
        compiler_params=pltpu.CompilerParams(has_side_effects=EFFECT))(*fulls, send, recv, after))


def pair_forward(fulls):
    nk = len(fulls)

    def body(*refs):
        bufs = refs[nk:2 * nk]
        stage = refs[2 * nk:3 * nk]
        load_sem, send_sem, recv_sem = refs[3 * nk:]
        x, y, c = _place()
        sib = (x, y, 1 - c)
        chips = [2 * cx + cy for cx, cy in _other_chips(x, y)]
        for k in range(nk):
            hr = bufs[k].shape[1] // 2
            at = [(chip, c * hr) for chip in chips]
            _push_tiles(bufs[k], bufs[k], stage[k], load_sem.at[k], send_sem.at[k], recv_sem.at[k], sib, at, at,
                        hr // _push_rows(hr))
        for k in range(nk):
            _await_tiles(bufs[k], send_sem.at[k], recv_sem.at[k], sib, 3, bufs[k].shape[1] // 2)

    return _hbm_call(body, name="pair_forward", n_in=nk, out_shape=[_sds(f.shape, f.dtype) for f in fulls], in_place=True,
                     scratch=_push_scratch(fulls, [f.shape[1] // 2 for f in fulls]))(*fulls)


def rs_pair_send(gs):
    nk = len(gs)

    def body(*refs):
        ins, got = refs[:nk], refs[nk:2 * nk]
        stage = refs[2 * nk:3 * nk]
        load_sem, send_sem, recv_sem = refs[3 * nk:]
        x, y, c = _place()
        sib = (x, y, 1 - c)
        for k in range(nk):
            hr = ins[k].shape[1] // 2
            _push_tiles(ins[k], got[k], stage[k], load_sem.at[k], send_sem.at[k], recv_sem.at[k], sib,
                        [(s, (1 - c) * hr) for s in range(NCHIP)], [(s, 0) for s in range(NCHIP)], hr // _push_rows(hr))
        for k in range(nk):
            _await_tiles(got[k], send_sem.at[k], recv_sem.at[k], sib, NCHIP, ins[k].shape[1] // 2)

    half = [_sds((g.shape[0], g.shape[1] // 2, g.shape[2]), g.dtype) for g in gs]
    return _hbm_call(body, name="rs_pair_send", n_in=nk, out_shape=half,
                     scratch=_push_scratch(gs, [g.shape[1] // 2 for g in gs]))(*gs)


def _chip_copies(ps, lands, send, recv):
    x, y, c = _place()
    return [_remote(ps[k].at[2 * cx + cy], lands[k].at[j], send.at[3 * k + j], recv.at[3 * k + j], (cx, cy, c))
            for k in range(len(ps)) for j, (cx, cy) in enumerate(_other_chips(x, y))]


def chip_exchange_start(ps):
    nk = len(ps)
    lands = [lax.empty((3,) + p.shape[1:], p.dtype) for p in ps]

    def body(*refs):
        send, recv = refs[2 * nk], refs[2 * nk + 1]
        srcs, dsts, token = refs[2 * nk + 2:3 * nk + 2], refs[3 * nk + 2:4 * nk + 2], refs[4 * nk + 2]
        for cp in _chip_copies(srcs, dsts, send, recv):
            cp.start()
        token[...] = jnp.zeros_like(token)

    res = pl.pallas_call(
        body, name="chip_exchange_start",
        out_shape=(pltpu.SemaphoreType.DMA((3 * nk,)), pltpu.SemaphoreType.DMA((3 * nk,)),
                   *[pltpu.HBM(a.shape, a.dtype) for a in ps + lands], _sds((8, LANES), F32)),
        in_specs=[HBM] * (2 * nk), out_specs=(SEM, SEM, *[HBM] * (2 * nk), pl.BlockSpec(memory_space=pltpu.VMEM)),
        input_output_aliases={k: 2 + k for k in range(2 * nk)},
        compiler_params=pltpu.CompilerParams(has_side_effects=EFFECT))(*[_in_hbm(a) for a in ps + lands])
    return res[0], res[1], list(res[2:2 + nk]), list(res[2 + nk:2 + 2 * nk]), res[2 + 2 * nk]


def chip_exchange_wait(send, recv, ps, lands, after):
    nk = len(ps)

    def body(*refs):
        srcs, dsts, send_ref, recv_ref = refs[:nk], refs[nk:2 * nk], refs[2 * nk], refs[2 * nk + 1]
        for cp in _chip_copies(srcs, dsts, send_ref, recv_ref):
            cp.wait_send()
            cp.wait_recv()

    res = pl.pallas_call(
        body, name="chip_exchange_wait", out_shape=tuple(pltpu.HBM(a.shape, a.dtype) for a in ps + lands),
        in_specs=[HBM] * (2 * nk) + [SEM, SEM, ANY], out_specs=tuple([HBM] * (2 * nk)),
        input_output_aliases={k: k for k in range(2 * nk)},
        compiler_params=pltpu.CompilerParams(has_side_effects=EFFECT))(*ps, *lands, send, recv, after)
    return list(res[:nk]), list(res[nk:])


def rs_pair_share(fulls):
    nk = len(fulls)

    def body(*refs):
        bufs = refs[nk:2 * nk]
        stage = refs[2 * nk:3 * nk]
        load_sem, send_sem, recv_sem = refs[3 * nk:]
        x, y, c = _place()
        sib = (x, y, 1 - c)
        for k in range(nk):
            hr = bufs[k].shape[1]
            _push_tiles(bufs[k], bufs[k], stage[k], load_sem.at[k], send_sem.at[k], recv_sem.at[k], sib,
                        [(c, 0)], [(c, 0)], hr // _push_rows(hr))
        for k in range(nk):
            _await_tiles(bufs[k], send_sem.at[k], recv_sem.at[k], sib, 1, bufs[k].shape[1])

    return _hbm_call(body, name="rs_pair_share", n_in=nk, out_shape=[_sds(f.shape, f.dtype) for f in fulls], in_place=True,
                     scratch=_push_scratch(fulls, [f.shape[1] for f in fulls]))(*fulls)


N_DEV = 8


def allreduce_small(buf):
    rows = buf.shape[0]
    rp = rows // N_DEV

    def body(in_ref, out_ref, land_ref, send, recv):
        x, y, c = _place()
        mine = pl.ds(pl.multiple_of((4 * x + 2 * y + c) * rp, 8), rp)
        peers = [(x ^ fx, y ^ fy, c ^ fc) for fx in (0, 1) for fy in (0, 1) for fc in (0, 1)][1:]
        block = [pl.ds(pl.multiple_of((4 * px + 2 * py + pc) * rp, 8), rp) for px, py, pc in peers]
        scatter = [_remote(in_ref.at[block[r], :], land_ref.at[r], send.at[r], recv.at[r], peers[r]) for r in range(7)]
        for cp in scatter:
            cp.start()
        for cp in scatter:
            cp.wait()
        acc = in_ref[mine, :]
        for r in range(7):
            acc = acc + land_ref[r]
        out_ref[mine, :] = acc
        spread = [_remote(out_ref.at[mine, :], out_ref.at[mine, :], send.at[7 + r], recv.at[7 + r], peers[r]) for r in range(7)]
        for cp in spread:
            cp.start()
        for r in range(7):
            _remote(out_ref.at[block[r], :], out_ref.at[block[r], :], send.at[7 + r], recv.at[7 + r], peers[r]).wait_recv()
        for cp in spread:
            cp.wait_send()

    vm = pl.BlockSpec(memory_space=pltpu.VMEM)
    return pl.pallas_call(body, name="allreduce_small", out_shape=_sds(buf.shape, F32), in_specs=[vm], out_specs=vm,
                          scratch_shapes=[pltpu.VMEM((7, rp, LANES), F32), pltpu.SemaphoreType.DMA((14,)),
                                          pltpu.SemaphoreType.DMA((14,))])(buf)


def _row_tile(rows, cap=512):
    return max(t for t in range(16, cap + 1, 16) if rows % t == 0)


def add_halves(g, got, pos):
    s, r, cols = g.shape
    hr = r // 2
    tr = _row_tile(hr)
    nt = hr // tr

    def body(pos_ref, a_ref, b_ref, o_ref):
        del pos_ref
        o_ref[...] = (a_ref[...].astype(F32) + b_ref[...].astype(F32)).astype(BF16)

    blk = pl.BlockSpec((None, tr, cols), lambda q, i, pos: (q, i, 0))
    return _call(body, name="add_halves", out_shape=_sds((s, hr, cols), BF16), grid=(s, nt),
                 in_specs=[pl.BlockSpec((None, tr, cols), lambda q, i, pos: (q, pos[0] * nt + i, 0)), blk],
                 out_specs=blk, sem=("parallel", "parallel"), prefetch=1)(pos, g, got)


def add_slots(p, got, pos):
    _, hr, cols = p.shape
    tr = _row_tile(hr)

    def body(pos_ref, o_ref, g0, g1, g2, out_ref):
        del pos_ref
        out_ref[...] = ((o_ref[...].astype(F32) + g0[...].astype(F32)) + g1[...].astype(F32)) + g2[...].astype(F32)

    slots = [pl.BlockSpec((None, tr, cols), functools.partial(lambda j, i, pos: (j, i, 0), j)) for j in range(3)]
    return _call(body, name="add_slots", out_shape=_sds((2, hr, cols), F32), grid=(hr // tr,),
                 in_specs=[pl.BlockSpec((None, tr, cols), lambda i, pos: (pos[1], i, 0))] + slots,
                 out_specs=pl.BlockSpec((None, tr, cols), lambda i, pos: (pos[0], i, 0)),
                 sem=("parallel",), prefetch=1)(pos, p, got, got, got)


def _adamw_math(w, g, m, v):
    nm = B1 * m + (1.0 - B1) * g
    nv = B2 * v + (1.0 - B2) * (g * g)
    m_hat = nm / (1.0 - B1 ** STEP)
    v_hat = nv / (1.0 - B2 ** STEP)
    return -LR * (m_hat / (jnp.sqrt(v_hat) + AEPS) + WD * w), nm, nv


def adamw_layer(w, g, m, v, l, prev):
    nl, r, cols = w.shape
    tr = _row_tile(r, 256)

    def body(w_ref, g_ref, m_ref, v_ref, *rest):
        go_ref, d_ref, nm_ref, nv_ref = rest[-4:]
        gv = g_ref[...]
        d, nm, nv = _adamw_math(w_ref[...], gv, m_ref[...], v_ref[...])
        go_ref[...] = gv
        d_ref[...] = d
        nm_ref[...] = nm
        nv_ref[...] = nv

    lay = pl.BlockSpec((None, tr, cols), lambda i: (l, i, 0))
    o = _sds((nl, r, cols), F32)
    extra = [] if prev is None else list(prev)
    return _call(body, name="adamw_layer", out_shape=(o, o, o, o), grid=(r // tr,),
                 in_specs=[lay, pl.BlockSpec((tr, cols), lambda i: (i, 0)), lay, lay] + [ANY] * len(extra),
                 out_specs=(lay, lay, lay, lay), sem=("parallel",),
                 aliases={4 + j: j for j in range(len(extra))})(w, g, m, v, *extra)


def adamw(w, g, m, v):
    shape = w.shape
    cols = shape[-1]
    rows = w.size // cols
    tr = _row_tile(rows, 256)

    def body(w_ref, g_ref, m_ref, v_ref, d_ref, nm_ref, nv_ref):
        d_ref[...], nm_ref[...], nv_ref[...] = _adamw_math(w_ref[...], g_ref[...], m_ref[...], v_ref[...])

    blk = pl.BlockSpec((tr, cols), lambda i: (i, 0))
    o = _sds((rows, cols), F32)
    outs = _call(body, name="adamw", out_shape=(o, o, o), grid=(rows // tr,), in_specs=[blk] * 4, out_specs=(blk, blk, blk),
                 sem=("parallel",))(*[t.reshape(rows, cols) for t in (w, g, m, v)])
    return [t.reshape(shape) for t in outs]


SMALL = ("norm1_g", "gmlp_ln_g", "w_spatial", "b_spatial", "mix_norm_attn_g", "mix_norm_gmlp_g", "norm2_g", "final_g")
WEIGHTS = ("norm1_g", "w_in", "gmlp_ln_g", "w_spatial", "b_spatial", "mix_norm_attn_g", "mix_norm_gmlp_g", "w_out",
           "norm2_g", "w_gate", "w_up", "w_down", "final_g")
BIG = dict(win="w_in", wout="w_out", wg="w_gate", wu="w_up", wd="w_down")


def _pack(parts):
    flat = jnp.concatenate([parts[n].reshape(-1) for n in SMALL])
    rows = -(-flat.shape[0] // (LANES * 8 * N_DEV)) * 8 * N_DEV
    return jnp.pad(flat, (0, rows * LANES - flat.shape[0])).reshape(rows, LANES)


def _unpack(buf, like):
    flat = buf.reshape(-1)
    out, at = {}, 0
    for n in SMALL:
        out[n] = flat[at:at + like[n].size].reshape(like[n].shape)
        at += like[n].size
    return out


def kernel(x, norm1_g, w_in, gmlp_ln_g, w_spatial, b_spatial, mix_norm_attn_g, mix_norm_gmlp_g, w_out, norm2_g, w_gate, w_up, w_down, final_g, loss_target, m_norm1_g, m_w_in, m_gmlp_ln_g, m_w_spatial, m_b_spatial, m_mix_norm_attn_g, m_mix_norm_gmlp_g, m_w_out, m_norm2_g, m_w_gate, m_w_up, m_w_down, m_final_g, v_norm1_g, v_w_in, v_gmlp_ln_g, v_w_spatial, v_b_spatial, v_mix_norm_attn_g, v_mix_norm_gmlp_g, v_w_out, v_norm2_g, v_w_gate, v_w_up, v_w_down, v_final_g):
    w = dict(norm1_g=norm1_g, w_in=w_in, gmlp_ln_g=gmlp_ln_g, w_spatial=w_spatial, b_spatial=b_spatial,
             mix_norm_attn_g=mix_norm_attn_g, mix_norm_gmlp_g=mix_norm_gmlp_g, w_out=w_out, norm2_g=norm2_g,
             w_gate=w_gate, w_up=w_up, w_down=w_down, final_g=final_g)
    m = dict(norm1_g=m_norm1_g, w_in=m_w_in, gmlp_ln_g=m_gmlp_ln_g, w_spatial=m_w_spatial, b_spatial=m_b_spatial,
             mix_norm_attn_g=m_mix_norm_attn_g, mix_norm_gmlp_g=m_mix_norm_gmlp_g, w_out=m_w_out, norm2_g=m_norm2_g,
             w_gate=m_w_gate, w_up=m_w_up, w_down=m_w_down, final_g=m_final_g)
    v = dict(norm1_g=v_norm1_g, w_in=v_w_in, gmlp_ln_g=v_gmlp_ln_g, w_spatial=v_w_spatial, b_spatial=v_b_spatial,
             mix_norm_attn_g=v_mix_norm_attn_g, mix_norm_gmlp_g=v_mix_norm_gmlp_g, w_out=v_w_out, norm2_g=v_norm2_g,
             w_gate=v_w_gate, w_up=v_w_up, w_down=v_w_down, final_g=v_final_g)

    pos = place_vector()
    small = {n: w[n] for n in SMALL}
    tables = step_tables()

    def start_gather(l, after):
        return gather_start([cast_into(w[BIG[k]], l, pos, after) for k in KINDS])

    def whole(fulls):
        big = dict(zip(KINDS, fulls))
        big["wout"] = big["wout"].reshape(D, D)
        big["wd"] = big["wd"].reshape(FF, D)
        return big

    act = x[0]
    flight = start_gather(0, None)
    fulls = gather_wait(flight[0], flight[1], flight[2], flight[3])
    big, saved = [], []
    for l in range(NL):
        token = None
        if l + 1 < NL:
            flight = start_gather(l + 1, act)
            token = flight[3]
        big.append(whole(pair_forward(fulls)))
        act, sv = layer_fwd(act, small, l, big[l], tables, token)
        saved.append(sv)
        if l + 1 < NL:
            fulls = gather_wait(flight[0], flight[1], flight[2], act)
    loss8, dx, dxb, dfinal = loss_head(act, _row(small["final_g"]), loss_target[0])
    loss = lax.psum(loss8[0, 0], ("x", "y", "c"))

    grads, delta, new_m, new_v = {}, {}, {}, {}
    stacked = {k: None for k in KINDS}
    gs = {k: [None] * NL for k in SMALL[:-1]}

    def finish(flight, after):
        pair, lands = chip_exchange_wait(*flight[:4], after)
        halves = [add_slots(p, got, pos) for p, got in zip(pair, lands)]
        for k, full in zip(KINDS, rs_pair_share(halves)):
            n = BIG[k]
            stacked[k] = adamw_layer(w[n], full.reshape(w[n].shape[1:]), m[n], v[n], flight[5], stacked[k])

    flight = None
    for l in reversed(range(NL)):
        dx, dxb, gsl, gbig = layer_bwd(dx, dxb, small, l, big[l], saved[l], tables, None if flight is None else flight[4])
        for k in gsl:
            gs[k][l] = gsl[k]
        if flight is not None:
            finish(flight, dx)
        partial_grads = [gbig[k] for k in KINDS]
        pair = [add_halves(g, got, pos) for g, got in zip(partial_grads, rs_pair_send(partial_grads))]
        flight = chip_exchange_start(pair) + (l,)
    finish(flight, flight[4])
    for k in KINDS:
        grads[BIG[k]], delta[BIG[k]], new_m[BIG[k]], new_v[BIG[k]] = stacked[k]
    part = {n: jnp.stack(gs[n]).reshape(w[n].shape) for n in SMALL[:-1]}
    part["final_g"] = dfinal.reshape(w["final_g"].shape)
    grads.update(_unpack(allreduce_small(_pack(part)), part))

    sm = adamw(_pack({n: w[n] for n in SMALL}), _pack({n: grads[n] for n in SMALL}), _pack({n: m[n] for n in SMALL}),
               _pack({n: v[n] for n in SMALL}))
    for res, packed in zip((delta, new_m, new_v), sm):
        res.update(_unpack(packed, part))

    return (loss, dx.reshape(x.shape), *[grads[n] for n in WEIGHTS], *[delta[n] for n in WEIGHTS],
            *[new_m[n] for n in WEIGHTS], *[new_v[n] for n in WEIGHTS])
```

```python
import functools

import jax
import jax.numpy as jnp
from jax import lax
from jax.experimental import pallas as pl
from jax.experimental.pallas import tpu as pltpu

F32 = jnp.float32
BF16 = jnp.bfloat16

D = 2048
T = 4096
NL = 4
HD = 128
DA = D // 2
DG = D - DA
NH = DA // HD
NG = DG // 128
CH = 128
DIN = 3 * DA + 2 * DG
FF = 5632
DILATIONS = (1, 4, 16)
NSIDE = 64
ROPE_THETA = 10000.0
EPS = 1e-6
NEG = -1e30
SCALE = HD ** -0.5
NCHIP = 4

LR, B1, B2, AEPS, WD, STEP = 0.001, 0.9, 0.999, 1e-08, 0.01, 10

LANES = 128
VMEM_LIMIT = 56 * 1024 * 1024
MESH = pl.DeviceIdType.MESH


def _call(body, *, name, out_shape, grid=(), in_specs=None, out_specs=None, scratch=(), sem=None, aliases=None,
          prefetch=0):
    params = dict(vmem_limit_bytes=VMEM_LIMIT)
    if sem is not None:
        params["dimension_semantics"] = sem
    if prefetch:
        spec = pltpu.PrefetchScalarGridSpec(num_scalar_prefetch=prefetch, grid=grid, in_specs=in_specs,
                                            out_specs=out_specs, scratch_shapes=list(scratch))
        return pl.pallas_call(body, name=name, out_shape=out_shape, grid_spec=spec,
                              input_output_aliases=aliases or {}, compiler_params=pltpu.CompilerParams(**params))
    kw = {}
    if grid:
        kw["grid"] = grid
    return pl.pallas_call(
        body, name=name, out_shape=out_shape, in_specs=in_specs, out_specs=out_specs,
        scratch_shapes=list(scratch), input_output_aliases=aliases or {},
        compiler_params=pltpu.CompilerParams(**params), **kw)


def _sds(shape, dtype):
    return jax.ShapeDtypeStruct(tuple(shape), dtype)


def _dot(a, b):
    return jnp.dot(a, b, preferred_element_type=F32)


def _dot_nt(a, b):
    return lax.dot_general(a, b, (((1,), (1,)), ((), ())), preferred_element_type=F32)


def _dot_tn(a, b):
    return lax.dot_general(a, b, (((0,), (0,)), ((), ())), preferred_element_type=F32)


def _sigmoid(x):
    return 1.0 / (1.0 + jnp.exp(-x))


def _behind(after):
    return ([], []) if after is None else ([after], [pl.BlockSpec(memory_space=pl.ANY)])


def rms_fwd(x, g, after=None):
    n, d = x.shape
    tm = 256
    extra, extra_specs = _behind(after)

    def body(x_ref, g_ref, *rest):
        h_ref = rest[-1]
        xv = x_ref[...]
        r = lax.rsqrt(jnp.mean(xv * xv, axis=-1, keepdims=True) + EPS)
        h_ref[...] = (xv * r * g_ref[...]).astype(BF16)

    return _call(body, name="rms_fwd", out_shape=_sds((n, d), BF16), grid=(n // tm,),
                 in_specs=[pl.BlockSpec((tm, d), lambda i: (i, 0)), pl.BlockSpec((1, d), lambda i: (0, 0))] + extra_specs,
                 out_specs=pl.BlockSpec((tm, d), lambda i: (i, 0)), sem=("parallel",))(x, g, *extra)


def rms_bwd(dh, x, g, dres):
    n, d = x.shape
    tm = 256

    def body(dh_ref, x_ref, g_ref, dres_ref, dx_ref, dxb_ref, dg_ref):
        @pl.when(pl.program_id(0) == 0)
        def _():
            dg_ref[...] = jnp.zeros_like(dg_ref)

        xv = x_ref[...]
        r = lax.rsqrt(jnp.mean(xv * xv, axis=-1, keepdims=True) + EPS)
        xhat = xv * r
        dhv = dh_ref[...].astype(F32)
        dg_ref[...] += jnp.sum(dhv * xhat, axis=0, keepdims=True)
        dxn = dhv * g_ref[...]
        dx = dres_ref[...] + r * (dxn - xhat * jnp.mean(dxn * xhat, axis=-1, keepdims=True))
        dx_ref[...] = dx
        dxb_ref[...] = dx.astype(BF16)

    row = pl.BlockSpec((tm, d), lambda i: (i, 0))
    vec = pl.BlockSpec((1, d), lambda i: (0, 0))
    return _call(body, name="rms_bwd", out_shape=(_sds((n, d), F32), _sds((n, d), BF16), _sds((1, d), F32)),
                 grid=(n // tm,), in_specs=[row, row, vec, row], out_specs=(row, row, vec), sem=("arbitrary",))(dh, x, g, dres)


def loss_head(x, g, target):
    n, d = x.shape
    tm = 256

    def body(x_ref, g_ref, t_ref, loss_ref, dx_ref, dxb_ref, dg_ref):
        @pl.when(pl.program_id(0) == 0)
        def _():
            dg_ref[...] = jnp.zeros_like(dg_ref)
            loss_ref[...] = jnp.zeros_like(loss_ref)

        xv = x_ref[...]
        r = lax.rsqrt(jnp.mean(xv * xv, axis=-1, keepdims=True) + EPS)
        xhat = xv * r
        gv = g_ref[...]
        err = xhat * gv - t_ref[...]
        loss_ref[...] += 0.5 * jnp.sum(jnp.mean(err * err, axis=-1, keepdims=True))
        dy = err * (1.0 / d)
        dg_ref[...] += jnp.sum(dy * xhat, axis=0, keepdims=True)
        dxn = dy * gv
        dx = r * (dxn - xhat * jnp.mean(dxn * xhat, axis=-1, keepdims=True))
        dx_ref[...] = dx
        dxb_ref[...] = dx.astype(BF16)

    row = pl.BlockSpec((tm, d), lambda i: (i, 0))
    vec = pl.BlockSpec((1, d), lambda i: (0, 0))
    return _call(body, name="loss_head",
                 out_shape=(_sds((8, LANES), F32), _sds((n, d), F32), _sds((n, d), BF16), _sds((1, d), F32)),
                 grid=(n // tm,), in_specs=[row, vec, row],
                 out_specs=(pl.BlockSpec((8, LANES), lambda i: (0, 0)), row, row, vec), sem=("arbitrary",))(x, g, target)


def mm_cols(a, wc, tm=512):
    n, k = a.shape
    s, _, nq = wc.shape

    def body(a_ref, w_ref, o_ref):
        o_ref[...] = _dot(a_ref[...], w_ref[...]).astype(BF16)

    return _call(body, name="mm_cols", out_shape=_sds((n, s * nq), BF16), grid=(s, n // tm),
                 in_specs=[pl.BlockSpec((tm, k), lambda j, i: (i, 0)), pl.BlockSpec((None, k, nq), lambda j, i: (j, 0, 0))],
                 out_specs=pl.BlockSpec((tm, nq), lambda j, i: (i, j)), sem=("parallel", "parallel"))(a, wc)


def mm_gateup(h, wg, wu, tm=512):
    n, k = h.shape
    s, _, nq = wg.shape

    def body(h_ref, wg_ref, wu_ref, gate_ref, up_ref, ff_ref):
        hv = h_ref[...]
        gate = _dot(hv, wg_ref[...])
        up = _dot(hv, wu_ref[...])
        gate_ref[...] = gate.astype(BF16)
        up_ref[...] = up.astype(BF16)
        ff_ref[...] = (gate * _sigmoid(gate) * up).astype(BF16)

    wspec = pl.BlockSpec((None, k, nq), lambda j, i: (j, 0, 0))
    ospec = pl.BlockSpec((tm, nq), lambda j, i: (i, j))
    o = _sds((n, s * nq), BF16)
    return _call(body, name="mm_gateup", out_shape=(o, o, o), grid=(s, n // tm),
                 in_specs=[pl.BlockSpec((tm, k), lambda j, i: (i, 0)), wspec, wspec],
                 out_specs=(ospec, ospec, ospec), sem=("parallel", "parallel"))(h, wg, wu)


def mm_rows_res(a, wr, res, tk, tm=512, tn=1024):
    n, k = a.shape
    _, nn = wr.shape
    nk = k // tk

    def body(a_ref, w_ref, r_ref, o_ref, acc):
        kk = pl.program_id(2)

        @pl.when(kk == 0)
        def _():
            acc[...] = r_ref[...]

        acc[...] += _dot(a_ref[...], w_ref[...])

        @pl.when(kk == nk - 1)
        def _():
            o_ref[...] = acc[...]

    return _call(body, name="mm_rows_res", out_shape=_sds((n, nn), F32), grid=(nn // tn, n // tm, nk),
                 in_specs=[pl.BlockSpec((tm, tk), lambda j, i, kk: (i, kk)), pl.BlockSpec((tk, tn), lambda j, i, kk: (kk, j)),
                           pl.BlockSpec((tm, tn), lambda j, i, kk: (i, j))],
                 out_specs=pl.BlockSpec((tm, tn), lambda j, i, kk: (i, j)),
                 scratch=[pltpu.VMEM((tm, tn), F32)], sem=("parallel", "parallel", "arbitrary"))(a, wr, res)


def mm_nt_rows(dy, wr, tk, tm=512, after=None):
    n, nn = dy.shape
    k, _ = wr.shape
    extra, extra_specs = _behind(after)

    def body(dy_ref, w_ref, *rest):
        rest[-1][...] = _dot_nt(dy_ref[...], w_ref[...]).astype(BF16)

    return _call(body, name="mm_nt_rows", out_shape=_sds((n, k), BF16), grid=(k // tk, n // tm),
                 in_specs=[pl.BlockSpec((tm, nn), lambda j, i: (i, 0)), pl.BlockSpec((tk, nn), lambda j, i: (j, 0))]
                 + extra_specs,
                 out_specs=pl.BlockSpec((tm, tk), lambda j, i: (i, j)), sem=("parallel", "parallel"))(dy, wr, *extra)


def mm_dff(dy, wd, gate, up, tk, tm=512, after=None):
    n, nn = dy.shape
    k, _ = wd.shape
    extra, extra_specs = _behind(after)

    def body(dy_ref, w_ref, gate_ref, up_ref, *rest):
        dgate_ref, dup_ref = rest[-2:]
        dff = _dot_nt(dy_ref[...], w_ref[...])
        gate = gate_ref[...].astype(F32)
        up = up_ref[...].astype(F32)
        sig = _sigmoid(gate)
        dgate_ref[...] = (dff * up * (sig * (1.0 + gate * (1.0 - sig)))).astype(BF16)
        dup_ref[...] = (dff * (gate * sig)).astype(BF16)

    tile = pl.BlockSpec((tm, tk), lambda j, i: (i, j))
    o = _sds((n, k), BF16)
    return _call(body, name="mm_dff", out_shape=(o, o), grid=(k // tk, n // tm),
                 in_specs=[pl.BlockSpec((tm, nn), lambda j, i: (i, 0)), pl.BlockSpec((tk, nn), lambda j, i: (j, 0)), tile, tile]
                 + extra_specs,
                 out_specs=(tile, tile), sem=("parallel", "parallel"))(dy, wd, gate, up, *extra)


def mm_nt_cols(das, wcs, tm=512):
    n = das[0].shape[0]
    s, k, nq = wcs[0].shape
    npair = len(das)

    def body(*refs):
        da_refs, w_refs, o_ref, acc = refs[:npair], refs[npair:2 * npair], refs[2 * npair], refs[2 * npair + 1]
        ss = pl.program_id(1)

        @pl.when(ss == 0)
        def _():
            acc[...] = jnp.zeros_like(acc)

        for da_ref, w_ref in zip(da_refs, w_refs):
            acc[...] += _dot_nt(da_ref[...], w_ref[...])

        @pl.when(ss == s - 1)
        def _():
            o_ref[...] = acc[...].astype(BF16)

    return _call(body, name="mm_nt_cols%d" % npair, out_shape=_sds((n, k), BF16), grid=(n // tm, s),
                 in_specs=[pl.BlockSpec((tm, nq), lambda i, ss: (i, ss))] * npair
                 + [pl.BlockSpec((None, k, nq), lambda i, ss: (ss, 0, 0))] * npair,
                 out_specs=pl.BlockSpec((tm, k), lambda i, ss: (i, 0)),
                 scratch=[pltpu.VMEM((tm, k), F32)], sem=("parallel", "arbitrary"))(*das, *wcs)


def mm_tn(a, b, s, tka, tt=512):
    n, ka = a.shape
    nq = b.shape[1] // s
    nt = n // tt

    def body(a_ref, b_ref, o_ref, acc):
        t = pl.program_id(2)

        @pl.when(t == 0)
        def _():
            acc[...] = jnp.zeros_like(acc)

        acc[...] += _dot_tn(a_ref[...], b_ref[...])

        @pl.when(t == nt - 1)
        def _():
            o_ref[...] = acc[...].astype(BF16)

    return _call(body, name="mm_tn", out_shape=_sds((s, ka, nq), BF16), grid=(ka // tka, s, nt),
                 in_specs=[pl.BlockSpec((tt, tka), lambda i, j, t: (t, i)), pl.BlockSpec((tt, nq), lambda i, j, t: (t, j))],
                 out_specs=pl.BlockSpec((None, tka, nq), lambda i, j, t: (j, i, 0)),
                 scratch=[pltpu.VMEM((tka, nq), F32)], sem=("parallel", "parallel", "arbitrary"))(a, b)


def rope_tables(n):
    pos = jnp.arange(n, dtype=F32)
    inv = ROPE_THETA ** (-jnp.arange(0, HD, 2, dtype=F32) / HD)
    ang = pos[:, None] * inv[None, :]
    cos, sin = jnp.cos(ang), jnp.sin(ang)
    return jnp.concatenate([cos, cos], axis=-1), jnp.concatenate([-sin, sin], axis=-1)


PERM_ROWS = 256


def perm_matrix(d):
    n = PERM_ROWS // d
    i = jnp.arange(PERM_ROWS)
    src = (i % n) * d + i // n
    return (src[:, None] == jnp.arange(PERM_ROWS)[None, :]).astype(BF16)


def _perm_f32(p, x):
    hi = x.astype(BF16)
    rest = x - hi.astype(F32)
    mid = rest.astype(BF16)
    lo = (rest - mid.astype(F32)).astype(BF16)
    return (_dot(p, hi) + _dot(p, mid)) + _dot(p, lo)


def _res_spec(d, cols, col_block=0):
    return pl.BlockSpec((d, PERM_ROWS // d, cols), lambda i: (0, i, col_block))


def _perm_specs():
    return [pl.BlockSpec((PERM_ROWS, PERM_ROWS), lambda i: (0, 0))] * (len(DILATIONS) - 1)


def qkv_split(proj, cos2, sin2, perms):
    n = proj.shape[0]
    tm = PERM_ROWS

    def body(p_ref, c_ref, s_ref, *rest):
        perm_refs, outs = rest[:len(DILATIONS) - 1], rest[len(DILATIONS) - 1:]
        c, s = c_ref[...], s_ref[...]
        for h in range(2 * NH):
            t = p_ref[:, h * HD:(h + 1) * HD].astype(F32)
            outs[0][0, :, h * HD:(h + 1) * HD] = (t * c + pltpu.roll(t, HD // 2, 1) * s).astype(BF16)
        outs[0][0, :, 2 * DA:] = p_ref[:, 2 * DA:]
        nat = outs[0][0]
        for o_ref, p_ref_d, d in zip(outs[1:], perm_refs, DILATIONS[1:]):
            o_ref[...] = _dot(p_ref_d[...], nat).astype(BF16).reshape(d, tm // d, 3 * DA)

    tab = pl.BlockSpec((tm, HD), lambda i: (i, 0))
    return _call(body, name="qkv_split", out_shape=tuple(_sds((d, n // d, 3 * DA), BF16) for d in DILATIONS),
                 grid=(n // tm,), in_specs=[pl.BlockSpec((tm, 3 * DA), lambda i: (i, 0)), tab, tab] + _perm_specs(),
                 out_specs=tuple(_res_spec(d, 3 * DA) for d in DILATIONS), sem=("parallel",))(proj, cos2, sin2, *perms)


def dproj_assemble(dqs, dks, dvs, duv, cos2, sin2, perms_t):
    n = duv.shape[0]
    tm = PERM_ROWS
    npat = len(DILATIONS)

    def body(*refs):
        parts = refs[0:npat], refs[npat:2 * npat], refs[2 * npat:3 * npat]
        duv_ref, c_ref, s_ref = refs[3 * npat:3 * npat + 3]
        pt_refs, o_ref = refs[3 * npat + 3:-1], refs[-1]
        c, s = c_ref[...], s_ref[...]
        for part, part_refs in enumerate(parts):
            t = part_refs[0][0].astype(F32)
            for ref, pt_ref in zip(part_refs[1:], pt_refs):
                t = t + _dot(pt_ref[...], ref[...].reshape(tm, DA))
            if part == 2:
                o_ref[:, 2 * DA:3 * DA] = t.astype(BF16)
                continue
            for h in range(NH):
                th = t[:, h * HD:(h + 1) * HD]
                o_ref[:, part * DA + h * HD:part * DA + (h + 1) * HD] = (th * c - pltpu.roll(th, HD // 2, 1) * s).astype(BF16)
        o_ref[:, 3 * DA:] = duv_ref[...]

    blks = [_res_spec(d, DA) for d in DILATIONS]
    tab = pl.BlockSpec((tm, HD), lambda i: (i, 0))
    return _call(body, name="dproj_assemble", out_shape=_sds((n, DIN), BF16), grid=(n // tm,),
                 in_specs=blks * 3 + [pl.BlockSpec((tm, 2 * DG), lambda i: (i, 0)), tab, tab] + _perm_specs(),
                 out_specs=pl.BlockSpec((tm, DIN), lambda i: (i, 0)), sem=("parallel",))(
        *dqs, *dks, *dvs, duv, cos2, sin2, *perms_t)


ATTN_BLOCK_BYTES = 2 << 20


def _attn_tiles(ls):
    tq = min(128, ls)
    w = min(tq + 2 * NSIDE, ls)
    return tq, w, ls // tq


def _attn_heads(ls):
    return NH if ls * DA * 2 <= ATTN_BLOCK_BYTES else 1


ATTN_UNROLL = 8


def _attn_unroll(nt):
    return min(nt, ATTN_UNROLL)


def _band(t, tq, w, ls):
    q0 = pl.multiple_of(t * tq, tq)
    ks = pl.multiple_of(jnp.clip(t * tq - NSIDE, 0, ls - w), NSIDE)
    qpos = q0 + lax.broadcasted_iota(jnp.int32, (tq, w), 0)
    kpos = ks + lax.broadcasted_iota(jnp.int32, (tq, w), 1)
    return q0, ks, jnp.abs(kpos - qpos) <= NSIDE


def _attn_specs(ls, hb):
    nhb = NH // hb
    q = pl.BlockSpec((None, ls, hb * HD), lambda g: (g // nhb, 0, g % nhb))
    k = pl.BlockSpec((None, ls, hb * HD), lambda g: (g // nhb, 0, nhb + g % nhb))
    v = pl.BlockSpec((None, ls, hb * HD), lambda g: (g // nhb, 0, 2 * nhb + g % nhb))
    return q, k, v, q


def attn_fwd(qkv):
    d, ls, _ = qkv.shape
    tq, w, nt = _attn_tiles(ls)
    hb = _attn_heads(ls)

    def body(q_ref, k_ref, v_ref, o_ref, l_ref):
        for h in range(hb):
            cols = slice(h * HD, (h + 1) * HD)

            def tile(t, carry, cols=cols):
                q0, ks, valid = _band(t, tq, w, ls)
                s = _dot_nt(q_ref[pl.ds(q0, tq), cols], k_ref[pl.ds(ks, w), cols]) * SCALE
                s = jnp.where(valid, s, NEG)
                m = jnp.max(s, axis=1, keepdims=True)
                p = jnp.exp(s - m)
                l = jnp.sum(p, axis=1, keepdims=True)
                o = _dot(p.astype(BF16), v_ref[pl.ds(ks, w), cols]) / l
                o_ref[pl.ds(q0, tq), cols] = o.astype(BF16)
                l_ref[pl.ds(q0, tq), cols] = jnp.broadcast_to(m + jnp.log(l), (tq, HD))
                return carry

            lax.fori_loop(0, nt, tile, 0, unroll=_attn_unroll(nt))

    q, k, v, o = _attn_specs(ls, hb)
    return _call(body, name="attn_fwd_d%d" % d, out_shape=(_sds((d, ls, DA), BF16), _sds((d, ls, DA), F32)),
                 grid=(d * NH // hb,), in_specs=[q, k, v], out_specs=(o, o), sem=("parallel",))(qkv, qkv, qkv)


def attn_merge(os_, ls_, ga, perms_t):
    n = os_[0].shape[1]
    tm = PERM_ROWS
    npat = len(DILATIONS)

    def body(*refs):
        o_refs, l_refs = refs[:npat], refs[npat:2 * npat]
        g_ref, pt_refs = refs[2 * npat], refs[2 * npat + 1:3 * npat]
        a_ref, lse_ref, mix_ref = refs[3 * npat:]
        ov = [o_refs[0][0].astype(F32)] + [_dot(pt[...], o[...].reshape(tm, DA)) for o, pt in zip(o_refs[1:], pt_refs)]
        lv = [l_refs[0][0]] + [_perm_f32(pt[...], l[...].reshape(tm, DA)) for l, pt in zip(l_refs[1:], pt_refs)]
        lmax = functools.reduce(jnp.maximum, lv)
        e = [jnp.exp(x - lmax) for x in lv]
        den = functools.reduce(lambda p, q: p + q, e)
        a = functools.reduce(lambda p, q: p + q, [ei * oi for ei, oi in zip(e, ov)]) / den
        a_ref[...] = a.astype(BF16)
        lse_ref[...] = lmax + jnp.log(den)
        r = lax.rsqrt(jnp.mean(a * a, axis=-1, keepdims=True) + EPS)
        mix_ref[...] = (a * r * g_ref[...]).astype(BF16)

    blk = pl.BlockSpec((tm, DA), lambda i: (i, 0))
    res = [_res_spec(d, DA) for d in DILATIONS]
    return _call(body, name="attn_merge", out_shape=(_sds((n, DA), BF16), _sds((n, DA), F32), _sds((n, D), BF16)),
                 grid=(n // tm,), in_specs=res * 2 + [pl.BlockSpec((1, DA), lambda i: (0, 0))] + _perm_specs(),
                 out_specs=(blk, blk, blk), sem=("parallel",))(*os_, *ls_, ga, *perms_t)


def attn_norm_bwd(dmix, a, lse, ga, perms):
    n = a.shape[0]
    tm = PERM_ROWS
    npat = len(DILATIONS)

    def body(dm_ref, a_ref, lse_ref, g_ref, *rest):
        p_refs = rest[:npat - 1]
        da_refs, a_refs, lse_refs = rest[npat - 1:2 * npat - 1], rest[2 * npat - 1:3 * npat - 2], rest[3 * npat - 2:4 * npat - 3]
        dg_ref = rest[-1]

        @pl.when(pl.program_id(0) == 0)
        def _():
            dg_ref[...] = jnp.zeros_like(dg_ref)

        ab = a_ref[...]
        av = ab.astype(F32)
        r = lax.rsqrt(jnp.mean(av * av, axis=-1, keepdims=True) + EPS)
        ahat = av * r
        dm = dm_ref[...].astype(F32)
        dg_ref[...] += jnp.sum(dm * ahat, axis=0, keepdims=True)
        dn = dm * g_ref[...]
        da = (r * (dn - ahat * jnp.mean(dn * ahat, axis=-1, keepdims=True))).astype(BF16)
        da_refs[0][0] = da
        lv = lse_ref[...]
        for i, d in enumerate(DILATIONS[1:]):
            p = p_refs[i][...]
            da_refs[i + 1][...] = _dot(p, da).astype(BF16).reshape(d, tm // d, DA)
            a_refs[i][...] = _dot(p, ab).astype(BF16).reshape(d, tm // d, DA)
            lse_refs[i][...] = _perm_f32(p, lv).reshape(d, tm // d, DA)

    blk = pl.BlockSpec((tm, DA), lambda i: (i, 0))
    vec = pl.BlockSpec((1, DA), lambda i: (0, 0))
    res = [_res_spec(d, DA) for d in DILATIONS]
    outs = _call(body, name="attn_norm_bwd",
                 out_shape=tuple([_sds((d, n // d, DA), BF16) for d in DILATIONS]
                                 + [_sds((d, n // d, DA), BF16) for d in DILATIONS[1:]]
                                 + [_sds((d, n // d, DA), F32) for d in DILATIONS[1:]] + [_sds((1, DA), F32)]),
                 grid=(n // tm,), in_specs=[blk, blk, blk, vec] + _perm_specs(),
                 out_specs=tuple(res + res[1:] + res[1:] + [vec]), sem=("arbitrary",))(dmix, a, lse, ga, *perms)
    return outs[:npat], outs[npat:2 * npat - 1], outs[2 * npat - 1:3 * npat - 2], outs[-1]


def attn_bwd(qkv, a, da, lse):
    d, ls, _ = qkv.shape
    tq, w, nt = _attn_tiles(ls)
    hb = _attn_heads(ls)

    def body(q_ref, k_ref, v_ref, a_ref, da_ref, lse_ref, dq_ref, dk_ref, dv_ref, dk_acc, dv_acc):
        dk_acc[...] = jnp.zeros_like(dk_acc)
        dv_acc[...] = jnp.zeros_like(dv_acc)
        for h in range(hb):
            cols = slice(h * HD, (h + 1) * HD)

            def tile(t, carry, cols=cols, lse_col=slice(h * HD, h * HD + 1)):
                q0, ks, valid = _band(t, tq, w, ls)
                q = q_ref[pl.ds(q0, tq), cols]
                k = k_ref[pl.ds(ks, w), cols]
                v = v_ref[pl.ds(ks, w), cols]
                do = da_ref[pl.ds(q0, tq), cols]
                s = jnp.where(valid, _dot_nt(q, k) * SCALE, NEG)
                p = jnp.exp(s - lse_ref[pl.ds(q0, tq), lse_col])
                drow = jnp.sum(do.astype(F32) * a_ref[pl.ds(q0, tq), cols].astype(F32), axis=1, keepdims=True)
                ds = (p * (_dot_nt(do, v) - drow) * SCALE).astype(BF16)
                dv_acc[pl.ds(ks, w), cols] += _dot_tn(p.astype(BF16), do)
                dk_acc[pl.ds(ks, w), cols] += _dot_tn(ds, q)
                dq_ref[pl.ds(q0, tq), cols] = _dot(ds, k).astype(BF16)
                return carry

            lax.fori_loop(0, nt, tile, 0, unroll=_attn_unroll(nt))
        dk_ref[...] = dk_acc[...].astype(BF16)
        dv_ref[...] = dv_acc[...].astype(BF16)

    q, k, v, o = _attn_specs(ls, hb)
    out = _sds((d, ls, DA), BF16)
    return _call(body, name="attn_bwd_d%d" % d, out_shape=(out, out, out), grid=(d * NH // hb,),
                 in_specs=[q, k, v, o, o, o], out_specs=(o, o, o),
                 scratch=[pltpu.VMEM((ls, hb * HD), F32), pltpu.VMEM((ls, hb * HD), F32)], sem=("parallel",))(
        qkv, qkv, qkv, a, da, lse)


GM_TM = 256
INV_SQRT2 = 0.7071067811865476
INV_SQRT2PI = 0.3989422804014327


def _gelu(x):
    return 0.5 * x * (1.0 + lax.erf(x * INV_SQRT2))


def _gelu_grad(x):
    return 0.5 * (1.0 + lax.erf(x * INV_SQRT2)) + x * (INV_SQRT2PI * jnp.exp(-0.5 * x * x))


def _gmlp_forward(up, vp, ln_g, ws_ref, bias):
    u = _gelu(up)
    v = _gelu(vp)
    vc = v - jnp.mean(v, axis=-1, keepdims=True)
    rs = lax.rsqrt(jnp.mean(vc * vc, axis=-1, keepdims=True) + EPS)
    vhat = vc * rs
    vln = (vhat * ln_g).astype(BF16)
    rows = []
    for c in range(GM_TM // CH):
        cols = [_dot(ws_ref[g], vln[c * CH:(c + 1) * CH, g * 128:(g + 1) * 128]) for g in range(NG)]
        rows.append(jnp.concatenate(cols, axis=1) + bias)
    return u, vhat, rs, vln, jnp.concatenate(rows, axis=0)


def _gmlp_specs():
    ublk = pl.BlockSpec((GM_TM, DG), lambda i: (i, 3 * DA // DG))
    vblk = pl.BlockSpec((GM_TM, DG), lambda i: (i, 3 * DA // DG + 1))
    vec = pl.BlockSpec((1, DG), lambda i: (0, 0))
    wsp = pl.BlockSpec((NG, CH, CH), lambda i: (0, 0, 0))
    bsp = pl.BlockSpec((CH, DG), lambda i: (0, 0))
    return ublk, vblk, vec, wsp, bsp


def gmlp_fwd(proj, ln_g, ws, bias, gg, mix):
    n = proj.shape[0]

    def body(up_ref, vp_ref, ln_ref, ws_ref, b_ref, gg_ref, mix_in, mix_ref):
        del mix_in
        u, _, _, _, mixed = _gmlp_forward(up_ref[...].astype(F32), vp_ref[...].astype(F32), ln_ref[...], ws_ref, b_ref[...])
        gout = u * mixed
        r = lax.rsqrt(jnp.mean(gout * gout, axis=-1, keepdims=True) + EPS)
        mix_ref[...] = (gout * r * gg_ref[...]).astype(BF16)

    ublk, vblk, vec, wsp, bsp = _gmlp_specs()
    return _call(body, name="gmlp_fwd", out_shape=_sds((n, D), BF16), grid=(n // GM_TM,),
                 in_specs=[ublk, vblk, vec, wsp, bsp, vec, pl.BlockSpec(memory_space=pl.ANY)],
                 out_specs=pl.BlockSpec((GM_TM, DG), lambda i: (i, DA // DG)), sem=("parallel",), aliases={6: 0})(
        proj, proj, ln_g, ws, bias, gg, mix)


def gmlp_bwd(proj, dmix, ln_g, ws, wst, bias, gg):
    n = proj.shape[0]

    def body(up_ref, vp_ref, dm_ref, ln_ref, ws_ref, wst_ref, b_ref, gg_ref, duv_ref, dln_ref, dws_ref, dbs_ref, dgg_ref,
             db_ref):
        @pl.when(pl.program_id(0) == 0)
        def _():
            dln_ref[...] = jnp.zeros_like(dln_ref)
            dws_ref[...] = jnp.zeros_like(dws_ref)
            db_ref[...] = jnp.zeros_like(db_ref)
            dgg_ref[...] = jnp.zeros_like(dgg_ref)

        up = up_ref[...].astype(F32)
        vp = vp_ref[...].astype(F32)
        ln_g = ln_ref[...]
        u, vhat, rs, vln, mixed = _gmlp_forward(up, vp, ln_g, ws_ref, b_ref[...])
        gout = u * mixed
        r = lax.rsqrt(jnp.mean(gout * gout, axis=-1, keepdims=True) + EPS)
        ghat = gout * r
        dm = dm_ref[...].astype(F32)
        dgg_ref[...] += jnp.sum(dm * ghat, axis=0, keepdims=True)
        dn = dm * gg_ref[...]
        dgout = r * (dn - ghat * jnp.mean(dn * ghat, axis=-1, keepdims=True))
        du = dgout * mixed
        dmixed = dgout * u
        dmb = dmixed.astype(BF16)
        rows = []
        for c in range(GM_TM // CH):
            rsl = slice(c * CH, (c + 1) * CH)
            db_ref[...] += dmixed[rsl, :]
            cols = []
            for g in range(NG):
                csl = slice(g * 128, (g + 1) * 128)
                dws_ref[g] += _dot_nt(dmb[rsl, csl], vln[rsl, csl])
                cols.append(_dot(wst_ref[g], dmb[rsl, csl]))
            rows.append(jnp.concatenate(cols, axis=1))
        dvln = jnp.concatenate(rows, axis=0)
        dln_ref[...] += jnp.sum(dvln * vhat, axis=0, keepdims=True)
        dvh = dvln * ln_g
        dv = rs * (dvh - jnp.mean(dvh, axis=-1, keepdims=True) - vhat * jnp.mean(dvh * vhat, axis=-1, keepdims=True))
        duv_ref[:, 0:DG] = (du * _gelu_grad(up)).astype(BF16)
        duv_ref[:, DG:] = (dv * _gelu_grad(vp)).astype(BF16)

        @pl.when(pl.program_id(0) == pl.num_programs(0) - 1)
        def _():
            for g in range(NG):
                dbs_ref[g:g + 1, :] = jnp.sum(jnp.transpose(db_ref[:, g * 128:(g + 1) * 128]), axis=0, keepdims=True)

    ublk, vblk, vec, wsp, bsp = _gmlp_specs()
    return _call(body, name="gmlp_bwd",
                 out_shape=(_sds((n, 2 * DG), BF16), _sds((1, DG), F32), _sds((NG, CH, CH), F32), _sds((NG, CH), F32),
                            _sds((1, DG), F32)),
                 grid=(n // GM_TM,),
                 in_specs=[ublk, vblk, pl.BlockSpec((GM_TM, DG), lambda i: (i, DA // DG)), vec, wsp, wsp, bsp, vec],
                 out_specs=(pl.BlockSpec((GM_TM, 2 * DG), lambda i: (i, 0)), vec, wsp,
                            pl.BlockSpec((NG, CH), lambda i: (0, 0)), vec),
                 scratch=[pltpu.VMEM((CH, DG), F32)], sem=("arbitrary",))(
        proj, proj, dmix, ln_g, ws, wst, bias, gg)


def _row(v):
    return v.reshape(1, -1)


def local_step(x, target, small, big):
    tables = step_tables()
    saved = []
    for l in range(NL):
        x_mid, sv = mixer_fwd(x, small, l, big[l], tables)
        x, sv2 = ffn_fwd(x_mid, small, l, big[l])
        saved.append({**sv, **sv2})
    loss8, dx, dxb, dfinal = loss_head(x, _row(small["final_g"]), target)
    gs = {k: [None] * NL for k in SMALL[:-1]}
    gbig = [None] * NL
    for l in reversed(range(NL)):
        dx, dxb, gs1, gb1 = ffn_bwd(dx, dxb, small, l, big[l], saved[l])
        dx, dxb, gs2, gb2 = mixer_bwd(dx, dxb, small, l, big[l], saved[l], tables)
        gbig[l] = {**gb1, **gb2}
        for k, g in {**gs1, **gs2}.items():
            gs[k][l] = g
    return loss8, dx, gs, dfinal, gbig


def step_tables():
    perms = [perm_matrix(d) for d in DILATIONS[1:]]
    return rope_tables(T) + (perms, [p.T for p in perms])


def mixer_fwd(x, small, l, w, tables, after=None):
    cos2, sin2, perms, perms_t = tables
    ws_b = small["w_spatial"][l].astype(BF16)
    bias = jnp.repeat(small["b_spatial"][l].T, 128, axis=1)
    h = rms_fwd(x, _row(small["norm1_g"][l]), after)
    proj = mm_cols(h, w["win"])
    qkvs = qkv_split(proj, cos2, sin2, perms)
    outs = [attn_fwd(qkv) for qkv in qkvs]
    a, lse, mix = attn_merge([o for o, _ in outs], [s for _, s in outs], _row(small["mix_norm_attn_g"][l]), perms_t)
    mix = gmlp_fwd(proj, _row(small["gmlp_ln_g"][l]), ws_b, bias, _row(small["mix_norm_gmlp_g"][l]), mix)
    x_mid = mm_rows_res(mix, w["wout"], x, tk=D)
    return x_mid, dict(x=x, h=h, proj=proj, qkvs=qkvs, a=a, lse=lse, mix=mix, ws_b=ws_b, bias=bias)


def ffn_fwd(x_mid, small, l, w, after=None):
    h2 = rms_fwd(x_mid, _row(small["norm2_g"][l]), after)
    gate, up, ff = mm_gateup(h2, w["wg"], w["wu"])
    x_out = mm_rows_res(ff, w["wd"], x_mid, tk=FF // NCHIP)
    return x_out, dict(x_mid=x_mid, h2=h2, gate=gate, up=up, ff=ff)


def ffn_bwd(dx, dxb, small, l, w, sv, after=None):
    gs = {}
    dgate, dup = mm_dff(dxb, w["wd"], sv["gate"], sv["up"], tk=FF // NCHIP, after=after)
    g_wd = mm_tn(sv["ff"], dxb, 1, tka=FF // NCHIP)
    dh2 = mm_nt_cols([dgate, dup], [w["wg"], w["wu"]])
    g_wg = mm_tn(sv["h2"], dgate, NCHIP, tka=D)
    g_wu = mm_tn(sv["h2"], dup, NCHIP, tka=D)
    dx, dxb, gs["norm2_g"] = rms_bwd(dh2, sv["x_mid"], _row(small["norm2_g"][l]), dx)
    return dx, dxb, gs, dict(wg=g_wg, wu=g_wu, wd=g_wd.reshape(NCHIP, FF // NCHIP, D))


def mixer_bwd(dx, dxb, small, l, w, sv, tables, after=None):
    cos2, sin2, perms, perms_t = tables
    gs = {}
    dmix = mm_nt_rows(dxb, w["wout"], tk=D // 2, after=after)
    g_wout = mm_tn(sv["mix"], dxb, 1, tka=D // 2)
    das, a_res, lse_res, gs["mix_norm_attn_g"] = attn_norm_bwd(dmix, sv["a"], sv["lse"],
                                                                _row(small["mix_norm_attn_g"][l]), perms)
    a_all = [sv["a"].reshape((1,) + sv["a"].shape)] + list(a_res)
    lse_all = [sv["lse"].reshape((1,) + sv["lse"].shape)] + list(lse_res)
    parts = [attn_bwd(*operands) for operands in zip(sv["qkvs"], a_all, das, lse_all)]
    wst = jnp.swapaxes(small["w_spatial"][l], 1, 2).astype(BF16)
    duv, gs["gmlp_ln_g"], gs["w_spatial"], gs["b_spatial"], gs["mix_norm_gmlp_g"] = gmlp_bwd(
        sv["proj"], dmix, _row(small["gmlp_ln_g"][l]), sv["ws_b"], wst, sv["bias"], _row(small["mix_norm_gmlp_g"][l]))
    dproj = dproj_assemble([p[0] for p in parts], [p[1] for p in parts], [p[2] for p in parts], duv, cos2, sin2,
                           perms_t)
    dh = mm_nt_cols([dproj], [w["win"]])
    g_win = mm_tn(sv["h"], dproj, NCHIP, tka=D)
    dx, dxb, gs["norm1_g"] = rms_bwd(dh, sv["x"], _row(small["norm1_g"][l]), dx)
    return dx, dxb, gs, dict(win=g_win, wout=g_wout.reshape(NCHIP, D // NCHIP, D))


KINDS = ("win", "wout", "wg", "wu", "wd")
GROUPS = (("win", "wout"), ("wg", "wu", "wd"))
ANY = pl.BlockSpec(memory_space=pl.ANY)


def _place():
    return lax.axis_index("x"), lax.axis_index("y"), lax.axis_index("c")


def _other_chips(x, y):
    return [(1 - x, y), (x, 1 - y), (1 - x, 1 - y)]


def _remote(src, dst, send_sem, recv_sem, to):
    return pltpu.make_async_remote_copy(src_ref=src, dst_ref=dst, send_sem=send_sem, recv_sem=recv_sem,
                                        device_id=to, device_id_type=MESH)


def _hbm_call(body, *, name, n_in, out_shape, scratch, in_place=False):
    return pl.pallas_call(body, name=name, out_shape=tuple(out_shape), in_specs=[ANY] * n_in,
                          out_specs=tuple(ANY for _ in out_shape), scratch_shapes=list(scratch),
                          input_output_aliases={k: k for k in range(n_in)} if in_place else {},
                          compiler_params=pltpu.CompilerParams(vmem_limit_bytes=VMEM_LIMIT))


def place_vector():
    x, y, c = _place()
    return jnp.stack([c, 2 * x + y] + [2 * cx + cy for cx, cy in _other_chips(x, y)]).astype(jnp.int32)


PUSH_ROWS = 256
PUSH_SLOTS = 3


def _push_tiles(src, dst, buf, load_sem, send_sem, recv_sem, to, src_at, dst_at, nt):
    tr = buf.shape[1]
    n = len(src_at) * nt

    def pick(vals, seg):
        out = vals[0]
        for q in range(1, len(vals)):
            out = jnp.where(seg == q, vals[q], out)
        return out

    def tile(ref, at, t):
        seg = t // nt
        row = pl.multiple_of(pick([a[1] for a in at], seg) + (t - seg * nt) * tr, 16)
        return ref.at[pick([a[0] for a in at], seg), pl.ds(row, tr), :]

    def load(t):
        slot = t % PUSH_SLOTS
        return pltpu.make_async_copy(tile(src, src_at, t), buf.at[slot], load_sem.at[slot])

    def send(t):
        slot = t % PUSH_SLOTS
        return _remote(buf.at[slot], tile(dst, dst_at, t), send_sem.at[slot], recv_sem, to)

    load(0).start()

    def step(t, carry):
        load(t).wait()
        send(t).start()

        @pl.when(t >= PUSH_SLOTS - 1)
        def _():
            send(t - (PUSH_SLOTS - 1)).wait_send()

        @pl.when(t + 1 < n)
        def _():
            load(t + 1).start()

        return carry

    lax.fori_loop(0, n, step, 0)
    for t in range(max(n - (PUSH_SLOTS - 1), 0), n):
        send(t).wait_send()


def _await_tiles(dst, send_sem, recv_sem, to, nseg, rows):
    whole = dst.at[pl.ds(0, nseg), pl.ds(0, rows), :]
    _remote(whole, whole, send_sem.at[0], recv_sem, to).wait_recv()


def _push_rows(seg_rows):
    return _row_tile(seg_rows, PUSH_ROWS)


def _push_scratch(arrs, seg_rows):
    return ([pltpu.VMEM((PUSH_SLOTS, _push_rows(r), a.shape[-1]), a.dtype) for a, r in zip(arrs, seg_rows)]
            + [pltpu.SemaphoreType.DMA((len(arrs), PUSH_SLOTS)), pltpu.SemaphoreType.DMA((len(arrs), PUSH_SLOTS)),
               pltpu.SemaphoreType.DMA((len(arrs),))])


def cast_into(w, l, pos, after=None):
    _, r, cols = w.shape
    tr = _row_tile(r)
    extra, extra_specs = _behind(after)

    def body(pos_ref, w_ref, *rest):
        del pos_ref
        rest[-1][...] = w_ref[...].astype(BF16)

    return _call(body, name="cast_into", out_shape=_sds((NCHIP, r, cols), BF16), grid=(r // tr,),
                 in_specs=[pl.BlockSpec((None, tr, cols), lambda i, pos: (l, i, 0))] + extra_specs,
                 out_specs=pl.BlockSpec((None, tr, cols), lambda i, pos: (pos[1], i, 0)),
                 sem=("parallel",), prefetch=1)(pos, w, *extra)


HBM = pl.BlockSpec(memory_space=pltpu.HBM)
SEM = pl.BlockSpec(memory_space=pltpu.SEMAPHORE)
EFFECT = pltpu.SideEffectType.DATAFLOW_SIDE_EFFECTING


def _in_hbm(a):
    return pltpu.with_memory_space_constraint(a, pltpu.HBM)


def _gather_copies(bufs, send, recv):
    x, y, c = _place()
    chips = _other_chips(x, y)

    def half(k, chip):
        hr = bufs[k].shape[1] // 2
        return bufs[k].at[chip, pl.ds(c * hr, hr), :]

    out, back = [], []
    for k in range(len(bufs)):
        for j, (cx, cy) in enumerate(chips):
            i = 3 * k + j
            out.append(_remote(half(k, 2 * x + y), half(k, 2 * x + y), send.at[i], recv.at[i], (cx, cy, c)))
            back.append(_remote(half(k, 2 * x + y), half(k, 2 * cx + cy), send.at[i], recv.at[i], (cx, cy, c)))
    return out, back


def gather_start(fulls):
    nk = len(fulls)

    def body(*refs):
        send, recv = refs[nk], refs[nk + 1]
        bufs, token = refs[nk + 2:2 * nk + 2], refs[2 * nk + 2]
        for cp in _gather_copies(bufs, send, recv)[0]:
            cp.start()
        token[...] = jnp.zeros_like(token)

    res = pl.pallas_call(
        body, name="gather_start",
        out_shape=(pltpu.SemaphoreType.DMA((3 * nk,)), pltpu.SemaphoreType.DMA((3 * nk,)),
                   *[pltpu.HBM(f.shape, f.dtype) for f in fulls], _sds((8, LANES), F32)),
        in_specs=[HBM] * nk, out_specs=(SEM, SEM, *[HBM] * nk, pl.BlockSpec(memory_space=pltpu.VMEM)),
        input_output_aliases={k: 2 + k for k in range(nk)},
        compiler_params=pltpu.CompilerParams(has_side_effects=EFFECT))(*[_in_hbm(f) for f in fulls])
    return res[0], res[1], list(res[2:2 + nk]), res[2 + nk]


def gather_wait(send, recv, fulls, after):
    nk = len(fulls)

    def body(*refs):
        bufs, send_ref, recv_ref = refs[:nk], refs[nk], refs[nk + 1]
        for cp in _gather_copies(bufs, send_ref, recv_ref)[1]:
            cp.wait_send()
            cp.wait_recv()

    return list(pl.pallas_call(
        body, name="gather_wait", out_shape=tuple(pltpu.HBM(f.shape, f.dtype) for f in fulls),
        in_specs=[HBM] * nk + [SEM, SEM, ANY], out_specs=tuple([HBM] * nk),
        input_output_aliases={k: k for k in range(nk)},
        compiler_params=pltpu.CompilerParams(has_side_effects=EFFECT))(*fulls, send, recv, after))


def pair_forward(fulls):
    nk = len(fulls)

    def body(*refs):
        bufs = refs[nk:2 * nk]
        stage = refs[2 * nk:3 * nk]
        load_sem, send_sem, recv_sem = refs[3 * nk:]
        x, y, c = _place()
        sib = (x, y, 1 - c)
        chips = [2 * cx + cy for cx, cy in _other_chips(x, y)]
        for k in range(nk):
            hr = bufs[k].shape[1] // 2
            at = [(chip, c * hr) for chip in chips]
            _push_tiles(bufs[k], bufs[k], stage[k], load_sem.at[k], send_sem.at[k], recv_sem.at[k], sib, at, at,
                        hr // _push_rows(hr))
        for k in range(nk):
            _await_tiles(bufs[k], send_sem.at[k], recv_sem.at[k], sib, 3, bufs[k].shape[1] // 2)

    return _hbm_call(body, name="pair_forward", n_in=nk, out_shape=[_sds(f.shape, f.dtype) for f in fulls], in_place=True,
                     scratch=_push_scratch(fulls, [f.shape[1] // 2 for f in fulls]))(*fulls)


def rs_pair_send(gs):
    nk = len(gs)

    def body(*refs):
        ins, got = refs[:nk], refs[nk:2 * nk]
        stage = refs[2 * nk:3 * nk]
        load_sem, send_sem, recv_sem = refs[3 * nk:]
        x, y, c = _place()
        sib = (x, y, 1 - c)
        for k in range(nk):
            hr = ins[k].shape[1] // 2
            _push_tiles(ins[k], got[k], stage[k], load_sem.at[k], send_sem.at[k], recv_sem.at[k], sib,
                        [(s, (1 - c) * hr) for s in range(NCHIP)], [(s, 0) for s in range(NCHIP)], hr // _push_rows(hr))
        for k in range(nk):
            _await_tiles(got[k], send_sem.at[k], recv_sem.at[k], sib, NCHIP, ins[k].shape[1] // 2)

    half = [_sds((g.shape[0], g.shape[1] // 2, g.shape[2]), g.dtype) for g in gs]
    return _hbm_call(body, name="rs_pair_send", n_in=nk, out_shape=half,
                     scratch=_push_scratch(gs, [g.shape[1] // 2 for g in gs]))(*gs)


def _chip_copies(ps, lands, send, recv):
    x, y, c = _place()
    return [_remote(ps[k].at[2 * cx + cy], lands[k].at[j], send.at[3 * k + j], recv.at[3 * k + j], (cx, cy, c))
            for k in range(len(ps)) for j, (cx, cy) in enumerate(_other_chips(x, y))]


def chip_exchange_start(ps):
    nk = len(ps)
    lands = [lax.empty((3,) + p.shape[1:], p.dtype) for p in ps]

    def body(*refs):
        send, recv = refs[2 * nk], refs[2 * nk + 1]
        srcs, dsts, token = refs[2 * nk + 2:3 * nk + 2], refs[3 * nk + 2:4 * nk + 2], refs[4 * nk + 2]
        for cp in _chip_copies(srcs, dsts, send, recv):
            cp.start()
        token[...] = jnp.zeros_like(token)

    res = pl.pallas_call(
        body, name="chip_exchange_start",
        out_shape=(pltpu.SemaphoreType.DMA((3 * nk,)), pltpu.SemaphoreType.DMA((3 * nk,)),
                   *[pltpu.HBM(a.shape, a.dtype) for a in ps + lands], _sds((8, LANES), F32)),
        in_specs=[HBM] * (2 * nk), out_specs=(SEM, SEM, *[HBM] * (2 * nk), pl.BlockSpec(memory_space=pltpu.VMEM)),
        input_output_aliases={k: 2 + k for k in range(2 * nk)},
        compiler_params=pltpu.CompilerParams(has_side_effects=EFFECT))(*[_in_hbm(a) for a in ps + lands])
    return res[0], res[1], list(res[2:2 + nk]), list(res[2 + nk:2 + 2 * nk]), res[2 + 2 * nk]


def chip_exchange_wait(send, recv, ps, lands, after):
    nk = len(ps)

    def body(*refs):
        srcs, dsts, send_ref, recv_ref = refs[:nk], refs[nk:2 * nk], refs[2 * nk], refs[2 * nk + 1]
        for cp in _chip_copies(srcs, dsts, send_ref, recv_ref):
            cp.wait_send()
            cp.wait_recv()

    res = pl.pallas_call(
        body, name="chip_exchange_wait", out_shape=tuple(pltpu.HBM(a.shape, a.dtype) for a in ps + lands),
        in_specs=[HBM] * (2 * nk) + [SEM, SEM, ANY], out_specs=tuple([HBM] * (2 * nk)),
        input_output_aliases={k: k for k in range(2 * nk)},
        compiler_params=pltpu.CompilerParams(has_side_effects=EFFECT))(*ps, *lands, send, recv, after)
    return list(res[:nk]), list(res[nk:])


def rs_pair_share(fulls):
    nk = len(fulls)

    def body(*refs):
        bufs = refs[nk:2 * nk]
        stage = refs[2 * nk:3 * nk]
        load_sem, send_sem, recv_sem = refs[3 * nk:]
        x, y, c = _place()
        sib = (x, y, 1 - c)
        for k in range(nk):
            hr = bufs[k].shape[1]
            _push_tiles(bufs[k], bufs[k], stage[k], load_sem.at[k], send_sem.at[k], recv_sem.at[k], sib,
                        [(c, 0)], [(c, 0)], hr // _push_rows(hr))
        for k in range(nk):
            _await_tiles(bufs[k], send_sem.at[k], recv_sem.at[k], sib, 1, bufs[k].shape[1])

    return _hbm_call(body, name="rs_pair_share", n_in=nk, out_shape=[_sds(f.shape, f.dtype) for f in fulls], in_place=True,
                     scratch=_push_scratch(fulls, [f.shape[1] for f in fulls]))(*fulls)


N_DEV = 8


def allreduce_small(buf):
    rows = buf.shape[0]
    rp = rows // N_DEV

    def body(in_ref, out_ref, land_ref, send, recv):
        x, y, c = _place()
        mine = pl.ds(pl.multiple_of((4 * x + 2 * y + c) * rp, 8), rp)
        peers = [(x ^ fx, y ^ fy, c ^ fc) for fx in (0, 1) for fy in (0, 1) for fc in (0, 1)][1:]
        block = [pl.ds(pl.multiple_of((4 * px + 2 * py + pc) * rp, 8), rp) for px, py, pc in peers]
        scatter = [_remote(in_ref.at[block[r], :], land_ref.at[r], send.at[r], recv.at[r], peers[r]) for r in range(7)]
        for cp in scatter:
            cp.start()
        for cp in scatter:
            cp.wait()
        acc = in_ref[mine, :]
        for r in range(7):
            acc = acc + land_ref[r]
        out_ref[mine, :] = acc
        spread = [_remote(out_ref.at[mine, :], out_ref.at[mine, :], send.at[7 + r], recv.at[7 + r], peers[r]) for r in range(7)]
        for cp in spread:
            cp.start()
        for r in range(7):
            _remote(out_ref.at[block[r], :], out_ref.at[block[r], :], send.at[7 + r], recv.at[7 + r], peers[r]).wait_recv()
        for cp in spread:
            cp.wait_send()

    vm = pl.BlockSpec(memory_space=pltpu.VMEM)
    return pl.pallas_call(body, name="allreduce_small", out_shape=_sds(buf.shape, F32), in_specs=[vm], out_specs=vm,
                          scratch_shapes=[pltpu.VMEM((7, rp, LANES), F32), pltpu.SemaphoreType.DMA((14,)),
                                          pltpu.SemaphoreType.DMA((14,))])(buf)


def _row_tile(rows, cap=512):
    return max(t for t in range(16, cap + 1, 16) if rows % t == 0)


def add_halves(g, got, pos):
    s, r, cols = g.shape
    hr = r // 2
    tr = _row_tile(hr)
    nt = hr // tr

    def body(pos_ref, a_ref, b_ref, o_ref):
        del pos_ref
        o_ref[...] = (a_ref[...].astype(F32) + b_ref[...].astype(F32)).astype(BF16)

    blk = pl.BlockSpec((None, tr, cols), lambda q, i, pos: (q, i, 0))
    return _call(body, name="add_halves", out_shape=_sds((s, hr, cols), BF16), grid=(s, nt),
                 in_specs=[pl.BlockSpec((None, tr, cols), lambda q, i, pos: (q, pos[0] * nt + i, 0)), blk],
                 out_specs=blk, sem=("parallel", "parallel"), prefetch=1)(pos, g, got)


def add_slots(p, got, pos):
    _, hr, cols = p.shape
    tr = _row_tile(hr)

    def body(pos_ref, o_ref, g0, g1, g2, out_ref):
        del pos_ref
        out_ref[...] = ((o_ref[...].astype(F32) + g0[...].astype(F32)) + g1[...].astype(F32)) + g2[...].astype(F32)

    slots = [pl.BlockSpec((None, tr, cols), functools.partial(lambda j, i, pos: (j, i, 0), j)) for j in range(3)]
    return _call(body, name="add_slots", out_shape=_sds((2, hr, cols), F32), grid=(hr // tr,),
                 in_specs=[pl.BlockSpec((None, tr, cols), lambda i, pos: (pos[1], i, 0))] + slots,
                 out_specs=pl.BlockSpec((None, tr, cols), lambda i, pos: (pos[0], i, 0)),
                 sem=("parallel",), prefetch=1)(pos, p, got, got, got)


def _adamw_math(w, g, m, v):
    nm = B1 * m + (1.0 - B1) * g
    nv = B2 * v + (1.0 - B2) * (g * g)
    m_hat = nm / (1.0 - B1 ** STEP)
    v_hat = nv / (1.0 - B2 ** STEP)
    return -LR * (m_hat / (jnp.sqrt(v_hat) + AEPS) + WD * w), nm, nv


def adamw_layer(w, g, m, v, l, prev):
    nl, r, cols = w.shape
    tr = _row_tile(r, 256)

    def body(w_ref, g_ref, m_ref, v_ref, *rest):
        go_ref, d_ref, nm_ref, nv_ref = rest[-4:]
        gv = g_ref[...]
        d, nm, nv = _adamw_math(w_ref[...], gv, m_ref[...], v_ref[...])
        go_ref[...] = gv
        d_ref[...] = d
        nm_ref[...] = nm
        nv_ref[...] = nv

    lay = pl.BlockSpec((None, tr, cols), lambda i: (l, i, 0))
    o = _sds((nl, r, cols), F32)
    extra = [] if prev is None else list(prev)
    return _call(body, name="adamw_layer", out_shape=(o, o, o, o), grid=(r // tr,),
                 in_specs=[lay, pl.BlockSpec((tr, cols), lambda i: (i, 0)), lay, lay] + [ANY] * len(extra),
                 out_specs=(lay, lay, lay, lay), sem=("parallel",),
                 aliases={4 + j: j for j in range(len(extra))})(w, g, m, v, *extra)


def adamw(w, g, m, v):
    shape = w.shape
    cols = shape[-1]
    rows = w.size // cols
    tr = _row_tile(rows, 256)

    def body(w_ref, g_ref, m_ref, v_ref, d_ref, nm_ref, nv_ref):
        d_ref[...], nm_ref[...], nv_ref[...] = _adamw_math(w_ref[...], g_ref[...], m_ref[...], v_ref[...])

    blk = pl.BlockSpec((tr, cols), lambda i: (i, 0))
    o = _sds((rows, cols), F32)
    outs = _call(body, name="adamw", out_shape=(o, o, o), grid=(rows // tr,), in_specs=[blk] * 4, out_specs=(blk, blk, blk),
                 sem=("parallel",))(*[t.reshape(rows, cols) for t in (w, g, m, v)])
    return [t.reshape(shape) for t in outs]


SMALL = ("norm1_g", "gmlp_ln_g", "w_spatial", "b_spatial", "mix_norm_attn_g", "mix_norm_gmlp_g", "norm2_g", "final_g")
WEIGHTS = ("norm1_g", "w_in", "gmlp_ln_g", "w_spatial", "b_spatial", "mix_norm_attn_g", "mix_norm_gmlp_g", "w_out",
           "norm2_g", "w_gate", "w_up", "w_down", "final_g")
BIG = dict(win="w_in", wout="w_out", wg="w_gate", wu="w_up", wd="w_down")


def _pack(parts):
    flat = jnp.concatenate([parts[n].reshape(-1) for n in SMALL])
    rows = -(-flat.shape[0] // (LANES * 8 * N_DEV)) * 8 * N_DEV
    return jnp.pad(flat, (0, rows * LANES - flat.shape[0])).reshape(rows, LANES)


def _unpack(buf, like):
    flat = buf.reshape(-1)
    out, at = {}, 0
    for n in SMALL:
        out[n] = flat[at:at + like[n].size].reshape(like[n].shape)
        at += like[n].size
    return out


def kernel(x, norm1_g, w_in, gmlp_ln_g, w_spatial, b_spatial, mix_norm_attn_g, mix_norm_gmlp_g, w_out, norm2_g, w_gate, w_up, w_down, final_g, loss_target, m_norm1_g, m_w_in, m_gmlp_ln_g, m_w_spatial, m_b_spatial, m_mix_norm_attn_g, m_mix_norm_gmlp_g, m_w_out, m_norm2_g, m_w_gate, m_w_up, m_w_down, m_final_g, v_norm1_g, v_w_in, v_gmlp_ln_g, v_w_spatial, v_b_spatial, v_mix_norm_attn_g, v_mix_norm_gmlp_g, v_w_out, v_norm2_g, v_w_gate, v_w_up, v_w_down, v_final_g):
    w = dict(norm1_g=norm1_g, w_in=w_in, gmlp_ln_g=gmlp_ln_g, w_spatial=w_spatial, b_spatial=b_spatial,
             mix_norm_attn_g=mix_norm_attn_g, mix_norm_gmlp_g=mix_norm_gmlp_g, w_out=w_out, norm2_g=norm2_g,
             w_gate=w_gate, w_up=w_up, w_down=w_down, final_g=final_g)
    m = dict(norm1_g=m_norm1_g, w_in=m_w_in, gmlp_ln_g=m_gmlp_ln_g, w_spatial=m_w_spatial, b_spatial=m_b_spatial,
             mix_norm_attn_g=m_mix_norm_attn_g, mix_norm_gmlp_g=m_mix_norm_gmlp_g, w_out=m_w_out, norm2_g=m_norm2_g,
             w_gate=m_w_gate, w_up=m_w_up, w_down=m_w_down, final_g=m_final_g)
    v = dict(norm1_g=v_norm1_g, w_in=v_w_in, gmlp_ln_g=v_gmlp_ln_g, w_spatial=v_w_spatial, b_spatial=v_b_spatial,
             mix_norm_attn_g=v_mix_norm_attn_g, mix_norm_gmlp_g=v_mix_norm_gmlp_g, w_out=v_w_out, norm2_g=v_norm2_g,
             w_gate=v_w_gate, w_up=v_w_up, w_down=v_w_down, final_g=v_final_g)

    pos = place_vector()
    small = {n: w[n] for n in SMALL}
    tables = step_tables()

    def start_gathers(l, after):
        flights = []
        for kinds in GROUPS:
            flights.append(gather_start([cast_into(w[BIG[k]], l, pos, after) for k in kinds]))
            after = flights[-1][3]
        return flights, after

    def arrive(flight, kinds, after):
        fulls = pair_forward(gather_wait(flight[0], flight[1], flight[2], after))
        big = dict(zip(kinds, fulls))
        if "wout" in big:
            big["wout"] = big["wout"].reshape(D, D)
        if "wd" in big:
            big["wd"] = big["wd"].reshape(FF, D)
        return big

    act = x[0]
    flights, token = start_gathers(0, None)
    big, saved = [], []
    for l in range(NL):
        mine, after_mixer = flights, token
        if l + 1 < NL:
            flights, token = start_gathers(l + 1, act if l else token)
            after_mixer = token
        wl = arrive(mine[0], GROUPS[0], act if l else mine[0][3])
        x_mid, sv = mixer_fwd(act, small, l, wl, tables, after_mixer)
        wl.update(arrive(mine[1], GROUPS[1], x_mid))
        act, sv2 = ffn_fwd(x_mid, small, l, wl)
        big.append(wl)
        saved.append({**sv, **sv2})
    loss8, dx, dxb, dfinal = loss_head(act, _row(small["final_g"]), loss_target[0])
    loss = lax.psum(loss8[0, 0], ("x", "y", "c"))

    grads, delta, new_m, new_v = {}, {}, {}, {}
    stacked = {k: None for k in KINDS}
    gs = {k: [None] * NL for k in SMALL[:-1]}

    def send_off(gbig, kinds, l):
        partial_grads = [gbig[k] for k in kinds]
        pair = [add_halves(g, got, pos) for g, got in zip(partial_grads, rs_pair_send(partial_grads))]
        return chip_exchange_start(pair) + (kinds, l)

    def finish(flight, after):
        pair, lands = chip_exchange_wait(*flight[:4], after)
        halves = [add_slots(p, got, pos) for p, got in zip(pair, lands)]
        for k, full in zip(flight[5], rs_pair_share(halves)):
            n = BIG[k]
            stacked[k] = adamw_layer(w[n], full.reshape(w[n].shape[1:]), m[n], v[n], flight[6], stacked[k])

    pending, token = [], None
    for l in reversed(range(NL)):
        dx, dxb, gs1, gbig = ffn_bwd(dx, dxb, small, l, big[l], saved[l], token)
        for flight in pending:
            finish(flight, dx)
        ffn_flight = send_off(gbig, GROUPS[1], l)
        dx, dxb, gs2, gbig = mixer_bwd(dx, dxb, small, l, big[l], saved[l], tables, ffn_flight[4])
        mixer_flight = send_off(gbig, GROUPS[0], l)
        pending, token = [ffn_flight, mixer_flight], mixer_flight[4]
        for k, g in {**gs1, **gs2}.items():
            gs[k][l] = g
    for flight in pending:
        finish(flight, token)
    for k in KINDS:
        grads[BIG[k]], delta[BIG[k]], new_m[BIG[k]], new_v[BIG[k]] = stacked[k]
    part = {n: jnp.stack(gs[n]).reshape(w[n].shape) for n in SMALL[:-1]}
    part["final_g"] = dfinal.reshape(w["final_g"].shape)
    grads.update(_unpack(allreduce_small(_pack(part)), part))

    sm = adamw(_pack({n: w[n] for n in SMALL}), _pack({n: grads[n] for n in SMALL}), _pack({n: m[n] for n in SMALL}),
               _pack({n: v[n] for n in SMALL}))
    for res, packed in zip((delta, new_m, new_v), sm):
        res.update(_unpack(packed, part))

    return (loss, dx.reshape(x.shape), *[grads[n] for n in WEIGHTS], *[delta[n] for n in WEIGHTS],
            *[new_m[n] for n in WEIGHTS], *[new_v[n] for n in WEIGHTS])
```

```python
import functools

import jax
import jax.numpy as jnp
from jax import lax
from jax.experimental import pallas as pl
from jax.experimental.pallas import tpu as pltpu

F32 = jnp.float32
BF16 = jnp.bfloat16

D = 2048
T = 4096
NL = 4
HD = 128
DA = D // 2
DG = D - DA
NH = DA // HD
NG = DG // 128
CH = 128
DIN = 3 * DA + 2 * DG
FF = 5632
DILATIONS = (1, 4, 16)
NSIDE = 64
ROPE_THETA = 10000.0
EPS = 1e-6
NEG = -1e30
SCALE = HD ** -0.5
NCHIP = 4

LR, B1, B2, AEPS, WD, STEP = 0.001, 0.9, 0.999, 1e-08, 0.01, 10

LANES = 128
VMEM_LIMIT = 56 * 1024 * 1024
MESH = pl.DeviceIdType.MESH


def _call(body, *, name, out_shape, grid=(), in_specs=None, out_specs=None, scratch=(), sem=None, aliases=None,
          prefetch=0):
    params = dict(vmem_limit_bytes=VMEM_LIMIT)
    if sem is not None:
        params["dimension_semantics"] = sem
    if prefetch:
        spec = pltpu.PrefetchScalarGridSpec(num_scalar_prefetch=prefetch, grid=grid, in_specs=in_specs,
                                            out_specs=out_specs, scratch_shapes=list(scratch))
        return pl.pallas_call(body, name=name, out_shape=out_shape, grid_spec=spec,
                              input_output_aliases=aliases or {}, compiler_params=pltpu.CompilerParams(**params))
    kw = {}
    if grid:
        kw["grid"] = grid
    return pl.pallas_call(
        body, name=name, out_shape=out_shape, in_specs=in_specs, out_specs=out_specs,
        scratch_shapes=list(scratch), input_output_aliases=aliases or {},
        compiler_params=pltpu.CompilerParams(**params), **kw)


def _sds(shape, dtype):
    return jax.ShapeDtypeStruct(tuple(shape), dtype)


def _dot(a, b):
    return jnp.dot(a, b, preferred_element_type=F32)


def _dot_nt(a, b):
    return lax.dot_general(a, b, (((1,), (1,)), ((), ())), preferred_element_type=F32)


def _dot_tn(a, b):
    return lax.dot_general(a, b, (((0,), (0,)), ((), ())), preferred_element_type=F32)


def _sigmoid(x):
    return 0.5 * jnp.tanh(0.5 * x) + 0.5


def _behind(after):
    return ([], []) if after is None else ([after], [pl.BlockSpec(memory_space=pl.ANY)])


def rms_fwd(x, g, after=None):
    n, d = x.shape
    tm = 256
    extra, extra_specs = _behind(after)

    def body(x_ref, g_ref, *rest):
        h_ref = rest[-1]
        xv = x_ref[...]
        r = lax.rsqrt(jnp.mean(xv * xv, axis=-1, keepdims=True) + EPS)
        h_ref[...] = (xv * r * g_ref[...]).astype(BF16)

    return _call(body, name="rms_fwd", out_shape=_sds((n, d), BF16), grid=(n // tm,),
                 in_specs=[pl.BlockSpec((tm, d), lambda i: (i, 0)), pl.BlockSpec((1, d), lambda i: (0, 0))] + extra_specs,
                 out_specs=pl.BlockSpec((tm, d), lambda i: (i, 0)), sem=("parallel",))(x, g, *extra)


def rms_bwd(dh, x, g, dres):
    n, d = x.shape
    tm = 256

    def body(dh_ref, x_ref, g_ref, dres_ref, dx_ref, dxb_ref, dg_ref):
        @pl.when(pl.program_id(0) == 0)
        def _():
            dg_ref[...] = jnp.zeros_like(dg_ref)

        xv = x_ref[...]
        r = lax.rsqrt(jnp.mean(xv * xv, axis=-1, keepdims=True) + EPS)
        xhat = xv * r
        dhv = dh_ref[...].astype(F32)
        dg_ref[...] += jnp.sum(dhv * xhat, axis=0, keepdims=True)
        dxn = dhv * g_ref[...]
        dx = dres_ref[...] + r * (dxn - xhat * jnp.mean(dxn * xhat, axis=-1, keepdims=True))
        dx_ref[...] = dx
        dxb_ref[...] = dx.astype(BF16)

    row = pl.BlockSpec((tm, d), lambda i: (i, 0))
    vec = pl.BlockSpec((1, d), lambda i: (0, 0))
    return _call(body, name="rms_bwd", out_shape=(_sds((n, d), F32), _sds((n, d), BF16), _sds((1, d), F32)),
                 grid=(n // tm,), in_specs=[row, row, vec, row], out_specs=(row, row, vec), sem=("arbitrary",))(dh, x, g, dres)


def loss_head(x, g, target):
    n, d = x.shape
    tm = 256

    def body(x_ref, g_ref, t_ref, loss_ref, dx_ref, dxb_ref, dg_ref):
        @pl.when(pl.program_id(0) == 0)
        def _():
            dg_ref[...] = jnp.zeros_like(dg_ref)
            loss_ref[...] = jnp.zeros_like(loss_ref)

        xv = x_ref[...]
        r = lax.rsqrt(jnp.mean(xv * xv, axis=-1, keepdims=True) + EPS)
        xhat = xv * r
        gv = g_ref[...]
        err = xhat * gv - t_ref[...]
        loss_ref[...] += 0.5 * jnp.sum(jnp.mean(err * err, axis=-1, keepdims=True))
        dy = err * (1.0 / d)
        dg_ref[...] += jnp.sum(dy * xhat, axis=0, keepdims=True)
        dxn = dy * gv
        dx = r * (dxn - xhat * jnp.mean(dxn * xhat, axis=-1, keepdims=True))
        dx_ref[...] = dx
        dxb_ref[...] = dx.astype(BF16)

    row = pl.BlockSpec((tm, d), lambda i: (i, 0))
    vec = pl.BlockSpec((1, d), lambda i: (0, 0))
    return _call(body, name="loss_head",
                 out_shape=(_sds((8, LANES), F32), _sds((n, d), F32), _sds((n, d), BF16), _sds((1, d), F32)),
                 grid=(n // tm,), in_specs=[row, vec, row],
                 out_specs=(pl.BlockSpec((8, LANES), lambda i: (0, 0)), row, row, vec), sem=("arbitrary",))(x, g, target)


def mm_cols(a, wc, tm=512):
    n, k = a.shape
    s, _, nq = wc.shape

    def body(a_ref, w_ref, o_ref):
        o_ref[...] = _dot(a_ref[...], w_ref[...]).astype(BF16)

    return _call(body, name="mm_cols", out_shape=_sds((n, s * nq), BF16), grid=(s, n // tm),
                 in_specs=[pl.BlockSpec((tm, k), lambda j, i: (i, 0)), pl.BlockSpec((None, k, nq), lambda j, i: (j, 0, 0))],
                 out_specs=pl.BlockSpec((tm, nq), lambda j, i: (i, j)), sem=("parallel", "parallel"))(a, wc)


def mm_gateup(h, wg, wu, tm=512):
    n, k = h.shape
    s, _, nq = wg.shape

    def body(h_ref, wg_ref, wu_ref, gate_ref, up_ref, ff_ref):
        hv = h_ref[...]
        gate = _dot(hv, wg_ref[...])
        up = _dot(hv, wu_ref[...])
        gate_ref[...] = gate.astype(BF16)
        up_ref[...] = up.astype(BF16)
        ff_ref[...] = (gate * _sigmoid(gate) * up).astype(BF16)

    wspec = pl.BlockSpec((None, k, nq), lambda j, i: (j, 0, 0))
    ospec = pl.BlockSpec((tm, nq), lambda j, i: (i, j))
    o = _sds((n, s * nq), BF16)
    return _call(body, name="mm_gateup", out_shape=(o, o, o), grid=(s, n // tm),
                 in_specs=[pl.BlockSpec((tm, k), lambda j, i: (i, 0)), wspec, wspec],
                 out_specs=(ospec, ospec, ospec), sem=("parallel", "parallel"))(h, wg, wu)


def mm_rows_res(a, wr, res, tm=512, tn=1024):
    n, k = a.shape
    _, nn = wr.shape

    def body(a_ref, w_ref, r_ref, o_ref):
        o_ref[...] = r_ref[...] + _dot(a_ref[...], w_ref[...])

    return _call(body, name="mm_rows_res", out_shape=_sds((n, nn), F32), grid=(nn // tn, n // tm),
                 in_specs=[pl.BlockSpec((tm, k), lambda j, i: (i, 0)), pl.BlockSpec((k, tn), lambda j, i: (0, j)),
                           pl.BlockSpec((tm, tn), lambda j, i: (i, j))],
                 out_specs=pl.BlockSpec((tm, tn), lambda j, i: (i, j)), sem=("parallel", "parallel"))(a, wr, res)


def mm_nt_rows(dy, wr, tk, tm=512, after=None):
    n, nn = dy.shape
    k, _ = wr.shape
    extra, extra_specs = _behind(after)

    def body(dy_ref, w_ref, *rest):
        rest[-1][...] = _dot_nt(dy_ref[...], w_ref[...]).astype(BF16)

    return _call(body, name="mm_nt_rows", out_shape=_sds((n, k), BF16), grid=(k // tk, n // tm),
                 in_specs=[pl.BlockSpec((tm, nn), lambda j, i: (i, 0)), pl.BlockSpec((tk, nn), lambda j, i: (j, 0))]
                 + extra_specs,
                 out_specs=pl.BlockSpec((tm, tk), lambda j, i: (i, j)), sem=("parallel", "parallel"))(dy, wr, *extra)


def mm_dff(dy, wd, gate, up, tk, tm=512, after=None):
    n, nn = dy.shape
    k, _ = wd.shape
    extra, extra_specs = _behind(after)

    def body(dy_ref, w_ref, gate_ref, up_ref, *rest):
        dgate_ref, dup_ref = rest[-2:]
        dff = _dot_nt(dy_ref[...], w_ref[...])
        gate = gate_ref[...].astype(F32)
        up = up_ref[...].astype(F32)
        sig = _sigmoid(gate)
        dgate_ref[...] = (dff * up * (sig * (1.0 + gate * (1.0 - sig)))).astype(BF16)
        dup_ref[...] = (dff * (gate * sig)).astype(BF16)

    tile = pl.BlockSpec((tm, tk), lambda j, i: (i, j))
    o = _sds((n, k), BF16)
    return _call(body, name="mm_dff", out_shape=(o, o), grid=(k // tk, n // tm),
                 in_specs=[pl.BlockSpec((tm, nn), lambda j, i: (i, 0)), pl.BlockSpec((tk, nn), lambda j, i: (j, 0)), tile, tile]
                 + extra_specs,
                 out_specs=(tile, tile), sem=("parallel", "parallel"))(dy, wd, gate, up, *extra)


def mm_nt_cols(das, wcs, tm=512):
    n = das[0].shape[0]
    s, k, nq = wcs[0].shape
    npair = len(das)

    def body(*refs):
        da_refs, w_refs, o_ref, acc = refs[:npair], refs[npair:2 * npair], refs[2 * npair], refs[2 * npair + 1]
        ss = pl.program_id(1)

        @pl.when(ss == 0)
        def _():
            acc[...] = jnp.zeros_like(acc)

        for da_ref, w_ref in zip(da_refs, w_refs):
            acc[...] += _dot_nt(da_ref[...], w_ref[...])

        @pl.when(ss == s - 1)
        def _():
            o_ref[...] = acc[...].astype(BF16)

    return _call(body, name="mm_nt_cols%d" % npair, out_shape=_sds((n, k), BF16), grid=(n // tm, s),
                 in_specs=[pl.BlockSpec((tm, nq), lambda i, ss: (i, ss))] * npair
                 + [pl.BlockSpec((None, k, nq), lambda i, ss: (ss, 0, 0))] * npair,
                 out_specs=pl.BlockSpec((tm, k), lambda i, ss: (i, 0)),
                 scratch=[pltpu.VMEM((tm, k), F32)], sem=("parallel", "arbitrary"))(*das, *wcs)


def mm_tn(a, b, nq, shard_major, tka=512):
    n, ka = a.shape
    nb = b.shape[1] // nq

    def body(a_ref, b_ref, o_ref):
        o_ref[...] = _dot_tn(a_ref[...], b_ref[...]).astype(BF16)

    if shard_major:
        out_shape, out_spec = _sds((nb, ka, nq), BF16), pl.BlockSpec((None, tka, nq), lambda j, i: (j, i, 0))
    else:
        out_shape, out_spec = _sds((ka, nb * nq), BF16), pl.BlockSpec((tka, nq), lambda j, i: (i, j))
    return _call(body, name="mm_tn", out_shape=out_shape, grid=(nb, ka // tka),
                 in_specs=[pl.BlockSpec((n, tka), lambda j, i: (0, i)), pl.BlockSpec((n, nq), lambda j, i: (0, j))],
                 out_specs=out_spec, sem=("parallel", "parallel"))(a, b)


def rope_tables(n):
    pos = jnp.arange(n, dtype=F32)
    inv = ROPE_THETA ** (-jnp.arange(0, HD, 2, dtype=F32) / HD)
    ang = pos[:, None] * inv[None, :]
    cos, sin = jnp.cos(ang), jnp.sin(ang)
    return jnp.concatenate([cos, cos], axis=-1), jnp.concatenate([-sin, sin], axis=-1)


PERM_ROWS = 256


def perm_matrix(d):
    n = PERM_ROWS // d
    i = jnp.arange(PERM_ROWS)
    src = (i % n) * d + i // n
    return (src[:, None] == jnp.arange(PERM_ROWS)[None, :]).astype(BF16)


def _perm_f32(p, x):
    hi = x.astype(BF16)
    rest = x - hi.astype(F32)
    mid = rest.astype(BF16)
    lo = (rest - mid.astype(F32)).astype(BF16)
    return (_dot(p, hi) + _dot(p, mid)) + _dot(p, lo)


def _res_spec(d, cols, col_block=0):
    return pl.BlockSpec((d, PERM_ROWS // d, cols), lambda i: (0, i, col_block))


def _perm_specs():
    return [pl.BlockSpec((PERM_ROWS, PERM_ROWS), lambda i: (0, 0))] * (len(DILATIONS) - 1)


def qkv_split(proj, cos2, sin2, perms):
    n = proj.shape[0]
    tm = PERM_ROWS

    def body(p_ref, c_ref, s_ref, *rest):
        perm_refs, outs = rest[:len(DILATIONS) - 1], rest[len(DILATIONS) - 1:]
        c, s = c_ref[...], s_ref[...]
        for h in range(2 * NH):
            t = p_ref[:, h * HD:(h + 1) * HD].astype(F32)
            outs[0][0, :, h * HD:(h + 1) * HD] = (t * c + pltpu.roll(t, HD // 2, 1) * s).astype(BF16)
        outs[0][0, :, 2 * DA:] = p_ref[:, 2 * DA:]
        nat = outs[0][0]
        for o_ref, p_ref_d, d in zip(outs[1:], perm_refs, DILATIONS[1:]):
            o_ref[...] = _dot(p_ref_d[...], nat).astype(BF16).reshape(d, tm // d, 3 * DA)

    tab = pl.BlockSpec((tm, HD), lambda i: (i, 0))
    return _call(body, name="qkv_split", out_shape=tuple(_sds((d, n // d, 3 * DA), BF16) for d in DILATIONS),
                 grid=(n // tm,), in_specs=[pl.BlockSpec((tm, 3 * DA), lambda i: (i, 0)), tab, tab] + _perm_specs(),
                 out_specs=tuple(_res_spec(d, 3 * DA) for d in DILATIONS), sem=("parallel",))(proj, cos2, sin2, *perms)


def dproj_assemble(dqs, dks, dvs, duv, cos2, sin2, perms_t):
    n = duv.shape[0]
    tm = PERM_ROWS
    npat = len(DILATIONS)

    def body(*refs):
        parts = refs[0:npat], refs[npat:2 * npat], refs[2 * npat:3 * npat]
        duv_ref, c_ref, s_ref = refs[3 * npat:3 * npat + 3]
        pt_refs, o_ref = refs[3 * npat + 3:-1], refs[-1]
        c, s = c_ref[...], s_ref[...]
        for part, part_refs in enumerate(parts):
            t = part_refs[0][0].astype(F32)
            for ref, pt_ref in zip(part_refs[1:], pt_refs):
                t = t + _dot(pt_ref[...], ref[...].reshape(tm, DA))
            if part == 2:
                o_ref[:, 2 * DA:3 * DA] = t.astype(BF16)
                continue
            for h in range(NH):
                th = t[:, h * HD:(h + 1) * HD]
                o_ref[:, part * DA + h * HD:part * DA + (h + 1) * HD] = (th * c - pltpu.roll(th, HD // 2, 1) * s).astype(BF16)
        o_ref[:, 3 * DA:] = duv_ref[...]

    blks = [_res_spec(d, DA) for d in DILATIONS]
    tab = pl.BlockSpec((tm, HD), lambda i: (i, 0))
    return _call(body, name="dproj_assemble", out_shape=_sds((n, DIN), BF16), grid=(n // tm,),
                 in_specs=blks * 3 + [pl.BlockSpec((tm, 2 * DG), lambda i: (i, 0)), tab, tab] + _perm_specs(),
                 out_specs=pl.BlockSpec((tm, DIN), lambda i: (i, 0)), sem=("parallel",))(
        *dqs, *dks, *dvs, duv, cos2, sin2, *perms_t)


ATTN_BLOCK_BYTES = 2 << 20


def _attn_tiles(ls):
    tq = min(128, ls)
    w = min(tq + 2 * NSIDE, ls)
    return tq, w, ls // tq


def _attn_heads(ls):
    return NH if ls * DA * 2 <= ATTN_BLOCK_BYTES else 1


ATTN_UNROLL = 8


def _attn_unroll(nt):
    return min(nt, ATTN_UNROLL)


def _band(t, tq, w, ls):
    q0 = pl.multiple_of(t * tq, tq)
    ks = pl.multiple_of(jnp.clip(t * tq - NSIDE, 0, ls - w), NSIDE)
    qpos = q0 + lax.broadcasted_iota(jnp.int32, (tq, w), 0)
    kpos = ks + lax.broadcasted_iota(jnp.int32, (tq, w), 1)
    return q0, ks, jnp.abs(kpos - qpos) <= NSIDE


def _attn_specs(ls, hb):
    nhb = NH // hb
    q = pl.BlockSpec((None, ls, hb * HD), lambda g: (g // nhb, 0, g % nhb))
    k = pl.BlockSpec((None, ls, hb * HD), lambda g: (g // nhb, 0, nhb + g % nhb))
    v = pl.BlockSpec((None, ls, hb * HD), lambda g: (g // nhb, 0, 2 * nhb + g % nhb))
    return q, k, v, q


def attn_fwd(qkv):
    d, ls, _ = qkv.shape
    tq, w, nt = _attn_tiles(ls)
    hb = _attn_heads(ls)

    def body(q_ref, k_ref, v_ref, o_ref, l_ref):
        for h in range(hb):
            cols = slice(h * HD, (h + 1) * HD)

            def tile(t, carry, cols=cols):
                q0, ks, valid = _band(t, tq, w, ls)
                s = _dot_nt(q_ref[pl.ds(q0, tq), cols], k_ref[pl.ds(ks, w), cols]) * SCALE
                s = jnp.where(valid, s, NEG)
                m = jnp.max(s, axis=1, keepdims=True)
                p = jnp.exp(s - m)
                l = jnp.sum(p, axis=1, keepdims=True)
                o = _dot(p.astype(BF16), v_ref[pl.ds(ks, w), cols]) / l
                o_ref[pl.ds(q0, tq), cols] = o.astype(BF16)
                l_ref[pl.ds(q0, tq), cols] = jnp.broadcast_to(m + jnp.log(l), (tq, HD))
                return carry

            lax.fori_loop(0, nt, tile, 0, unroll=_attn_unroll(nt))

    q, k, v, o = _attn_specs(ls, hb)
    return _call(body, name="attn_fwd_d%d" % d, out_shape=(_sds((d, ls, DA), BF16), _sds((d, ls, DA), F32)),
                 grid=(d * NH // hb,), in_specs=[q, k, v], out_specs=(o, o), sem=("parallel",))(qkv, qkv, qkv)


def attn_merge(os_, ls_, ga, perms_t):
    n = os_[0].shape[1]
    tm = PERM_ROWS
    npat = len(DILATIONS)

    def body(*refs):
        o_refs, l_refs = refs[:npat], refs[npat:2 * npat]
        g_ref, pt_refs = refs[2 * npat], refs[2 * npat + 1:3 * npat]
        a_ref, lse_ref, mix_ref = refs[3 * npat:]
        ov = [o_refs[0][0].astype(F32)] + [_dot(pt[...], o[...].reshape(tm, DA)) for o, pt in zip(o_refs[1:], pt_refs)]
        lv = [l_refs[0][0]] + [_perm_f32(pt[...], l[...].reshape(tm, DA)) for l, pt in zip(l_refs[1:], pt_refs)]
        lmax = functools.reduce(jnp.maximum, lv)
        e = [jnp.exp(x - lmax) for x in lv]
        den = functools.reduce(lambda p, q: p + q, e)
        a = functools.reduce(lambda p, q: p + q, [ei * oi for ei, oi in zip(e, ov)]) / den
        a_ref[...] = a.astype(BF16)
        lse_ref[...] = lmax + jnp.log(den)
        r = lax.rsqrt(jnp.mean(a * a, axis=-1, keepdims=True) + EPS)
        mix_ref[...] = (a * r * g_ref[...]).astype(BF16)

    blk = pl.BlockSpec((tm, DA), lambda i: (i, 0))
    res = [_res_spec(d, DA) for d in DILATIONS]
    return _call(body, name="attn_merge", out_shape=(_sds((n, DA), BF16), _sds((n, DA), F32), _sds((n, D), BF16)),
                 grid=(n // tm,), in_specs=res * 2 + [pl.BlockSpec((1, DA), lambda i: (0, 0))] + _perm_specs(),
                 out_specs=(blk, blk, blk), sem=("parallel",))(*os_, *ls_, ga, *perms_t)


def attn_norm_bwd(dmix, a, lse, ga, perms):
    n = a.shape[0]
    tm = PERM_ROWS
    npat = len(DILATIONS)

    def body(dm_ref, a_ref, lse_ref, g_ref, *rest):
        p_refs = rest[:npat - 1]
        da_refs, a_refs, lse_refs = rest[npat - 1:2 * npat - 1], rest[2 * npat - 1:3 * npat - 2], rest[3 * npat - 2:4 * npat - 3]
        dg_ref = rest[-1]

        @pl.when(pl.program_id(0) == 0)
        def _():
            dg_ref[...] = jnp.zeros_like(dg_ref)

        ab = a_ref[...]
        av = ab.astype(F32)
        r = lax.rsqrt(jnp.mean(av * av, axis=-1, keepdims=True) + EPS)
        ahat = av * r
        dm = dm_ref[...].astype(F32)
        dg_ref[...] += jnp.sum(dm * ahat, axis=0, keepdims=True)
        dn = dm * g_ref[...]
        da = (r * (dn - ahat * jnp.mean(dn * ahat, axis=-1, keepdims=True))).astype(BF16)
        da_refs[0][0] = da
        lv = lse_ref[...]
        for i, d in enumerate(DILATIONS[1:]):
            p = p_refs[i][...]
            da_refs[i + 1][...] = _dot(p, da).astype(BF16).reshape(d, tm // d, DA)
            a_refs[i][...] = _dot(p, ab).astype(BF16).reshape(d, tm // d, DA)
            lse_refs[i][...] = _perm_f32(p, lv).reshape(d, tm // d, DA)

    blk = pl.BlockSpec((tm, DA), lambda i: (i, 0))
    vec = pl.BlockSpec((1, DA), lambda i: (0, 0))
    res = [_res_spec(d, DA) for d in DILATIONS]
    outs = _call(body, name="attn_norm_bwd",
                 out_shape=tuple([_sds((d, n // d, DA), BF16) for d in DILATIONS]
                                 + [_sds((d, n // d, DA), BF16) for d in DILATIONS[1:]]
                                 + [_sds((d, n // d, DA), F32) for d in DILATIONS[1:]] + [_sds((1, DA), F32)]),
                 grid=(n // tm,), in_specs=[blk, blk, blk, vec] + _perm_specs(),
                 out_specs=tuple(res + res[1:] + res[1:] + [vec]), sem=("arbitrary",))(dmix, a, lse, ga, *perms)
    return outs[:npat], outs[npat:2 * npat - 1], outs[2 * npat - 1:3 * npat - 2], outs[-1]


def attn_bwd(qkv, a, da, lse):
    d, ls, _ = qkv.shape
    tq, w, nt = _attn_tiles(ls)
    hb = _attn_heads(ls)

    def body(q_ref, k_ref, v_ref, a_ref, da_ref, lse_ref, dq_ref, dk_ref, dv_ref, dk_acc, dv_acc):
        dk_acc[...] = jnp.zeros_like(dk_acc)
        dv_acc[...] = jnp.zeros_like(dv_acc)
        for h in range(hb):
            cols = slice(h * HD, (h + 1) * HD)

            def tile(t, carry, cols=cols, lse_col=slice(h * HD, h * HD + 1)):
                q0, ks, valid = _band(t, tq, w, ls)
                q = q_ref[pl.ds(q0, tq), cols]
                k = k_ref[pl.ds(ks, w), cols]
                v = v_ref[pl.ds(ks, w), cols]
                do = da_ref[pl.ds(q0, tq), cols]
                s = jnp.where(valid, _dot_nt(q, k) * SCALE, NEG)
                p = jnp.exp(s - lse_ref[pl.ds(q0, tq), lse_col])
                drow = jnp.sum(do.astype(F32) * a_ref[pl.ds(q0, tq), cols].astype(F32), axis=1, keepdims=True)
                ds = (p * (_dot_nt(do, v) - drow) * SCALE).astype(BF16)
                dv_acc[pl.ds(ks, w), cols] += _dot_tn(p.astype(BF16), do)
                dk_acc[pl.ds(ks, w), cols] += _dot_tn(ds, q)
                dq_ref[pl.ds(q0, tq), cols] = _dot(ds, k).astype(BF16)
                return carry

            lax.fori_loop(0, nt, tile, 0, unroll=_attn_unroll(nt))
        dk_ref[...] = dk_acc[...].astype(BF16)
        dv_ref[...] = dv_acc[...].astype(BF16)

    q, k, v, o = _attn_specs(ls, hb)
    out = _sds((d, ls, DA), BF16)
    return _call(body, name="attn_bwd_d%d" % d, out_shape=(out, out, out), grid=(d * NH // hb,),
                 in_specs=[q, k, v, o, o, o], out_specs=(o, o, o),
                 scratch=[pltpu.VMEM((ls, hb * HD), F32), pltpu.VMEM((ls, hb * HD), F32)], sem=("parallel",))(
        qkv, qkv, qkv, a, da, lse)


GM_TM = 256
INV_SQRT2 = 0.7071067811865476
INV_SQRT2PI = 0.3989422804014327


def _gelu(x):
    return 0.5 * x * (1.0 + lax.erf(x * INV_SQRT2))


def _gelu_grad(x):
    return 0.5 * (1.0 + lax.erf(x * INV_SQRT2)) + x * (INV_SQRT2PI * jnp.exp(-0.5 * x * x))


def _gmlp_forward(up, vp, ln_g, ws_ref, bias):
    u = _gelu(up)
    v = _gelu(vp)
    vc = v - jnp.mean(v, axis=-1, keepdims=True)
    rs = lax.rsqrt(jnp.mean(vc * vc, axis=-1, keepdims=True) + EPS)
    vhat = vc * rs
    vln = (vhat * ln_g).astype(BF16)
    rows = []
    for c in range(GM_TM // CH):
        cols = [_dot(ws_ref[g], vln[c * CH:(c + 1) * CH, g * 128:(g + 1) * 128]) for g in range(NG)]
        rows.append(jnp.concatenate(cols, axis=1) + bias)
    return u, vhat, rs, vln, jnp.concatenate(rows, axis=0)


def _gmlp_specs():
    ublk = pl.BlockSpec((GM_TM, DG), lambda i: (i, 3 * DA // DG))
    vblk = pl.BlockSpec((GM_TM, DG), lambda i: (i, 3 * DA // DG + 1))
    vec = pl.BlockSpec((1, DG), lambda i: (0, 0))
    wsp = pl.BlockSpec((NG, CH, CH), lambda i: (0, 0, 0))
    bsp = pl.BlockSpec((CH, DG), lambda i: (0, 0))
    return ublk, vblk, vec, wsp, bsp


def gmlp_fwd(proj, ln_g, ws, bias, gg, mix):
    n = proj.shape[0]

    def body(up_ref, vp_ref, ln_ref, ws_ref, b_ref, gg_ref, mix_in, mix_ref):
        del mix_in
        u, _, _, _, mixed = _gmlp_forward(up_ref[...].astype(F32), vp_ref[...].astype(F32), ln_ref[...], ws_ref, b_ref[...])
        gout = u * mixed
        r = lax.rsqrt(jnp.mean(gout * gout, axis=-1, keepdims=True) + EPS)
        mix_ref[...] = (gout * r * gg_ref[...]).astype(BF16)

    ublk, vblk, vec, wsp, bsp = _gmlp_specs()
    return _call(body, name="gmlp_fwd", out_shape=_sds((n, D), BF16), grid=(n // GM_TM,),
                 in_specs=[ublk, vblk, vec, wsp, bsp, vec, pl.BlockSpec(memory_space=pl.ANY)],
                 out_specs=pl.BlockSpec((GM_TM, DG), lambda i: (i, DA // DG)), sem=("parallel",), aliases={6: 0})(
        proj, proj, ln_g, ws, bias, gg, mix)


def gmlp_bwd(proj, dmix, ln_g, ws, wst, bias, gg):
    n = proj.shape[0]

    def body(up_ref, vp_ref, dm_ref, ln_ref, ws_ref, wst_ref, b_ref, gg_ref, duv_ref, dln_ref, dws_ref, dbs_ref, dgg_ref,
             db_ref):
        @pl.when(pl.program_id(0) == 0)
        def _():
            dln_ref[...] = jnp.zeros_like(dln_ref)
            dws_ref[...] = jnp.zeros_like(dws_ref)
            db_ref[...] = jnp.zeros_like(db_ref)
            dgg_ref[...] = jnp.zeros_like(dgg_ref)

        up = up_ref[...].astype(F32)
        vp = vp_ref[...].astype(F32)
        ln_g = ln_ref[...]
        u, vhat, rs, vln, mixed = _gmlp_forward(up, vp, ln_g, ws_ref, b_ref[...])
        gout = u * mixed
        r = lax.rsqrt(jnp.mean(gout * gout, axis=-1, keepdims=True) + EPS)
        ghat = gout * r
        dm = dm_ref[...].astype(F32)
        dgg_ref[...] += jnp.sum(dm * ghat, axis=0, keepdims=True)
        dn = dm * gg_ref[...]
        dgout = r * (dn - ghat * jnp.mean(dn * ghat, axis=-1, keepdims=True))
        du = dgout * mixed
        dmixed = dgout * u
        dmb = dmixed.astype(BF16)
        rows = []
        for c in range(GM_TM // CH):
            rsl = slice(c * CH, (c + 1) * CH)
            db_ref[...] += dmixed[rsl, :]
            cols = []
            for g in range(NG):
                csl = slice(g * 128, (g + 1) * 128)
                dws_ref[g] += _dot_nt(dmb[rsl, csl], vln[rsl, csl])
                cols.append(_dot(wst_ref[g], dmb[rsl, csl]))
            rows.append(jnp.concatenate(cols, axis=1))
        dvln = jnp.concatenate(rows, axis=0)
        dln_ref[...] += jnp.sum(dvln * vhat, axis=0, keepdims=True)
        dvh = dvln * ln_g
        dv = rs * (dvh - jnp.mean(dvh, axis=-1, keepdims=True) - vhat * jnp.mean(dvh * vhat, axis=-1, keepdims=True))
        duv_ref[:, 0:DG] = (du * _gelu_grad(up)).astype(BF16)
        duv_ref[:, DG:] = (dv * _gelu_grad(vp)).astype(BF16)

        @pl.when(pl.program_id(0) == pl.num_programs(0) - 1)
        def _():
            for g in range(NG):
                dbs_ref[g:g + 1, :] = jnp.sum(jnp.transpose(db_ref[:, g * 128:(g + 1) * 128]), axis=0, keepdims=True)

    ublk, vblk, vec, wsp, bsp = _gmlp_specs()
    return _call(body, name="gmlp_bwd",
                 out_shape=(_sds((n, 2 * DG), BF16), _sds((1, DG), F32), _sds((NG, CH, CH), F32), _sds((NG, CH), F32),
                            _sds((1, DG), F32)),
                 grid=(n // GM_TM,),
                 in_specs=[ublk, vblk, pl.BlockSpec((GM_TM, DG), lambda i: (i, DA // DG)), vec, wsp, wsp, bsp, vec],
                 out_specs=(pl.BlockSpec((GM_TM, 2 * DG), lambda i: (i, 0)), vec, wsp,
                            pl.BlockSpec((NG, CH), lambda i: (0, 0)), vec),
                 scratch=[pltpu.VMEM((CH, DG), F32)], sem=("arbitrary",))(
        proj, proj, dmix, ln_g, ws, wst, bias, gg)


def _row(v):
    return v.reshape(1, -1)


def local_step(x, target, small, big):
    tables = step_tables()
    saved = []
    for l in range(NL):
        x_mid, sv = mixer_fwd(x, small, l, big[l], tables)
        x, sv2 = ffn_fwd(x_mid, small, l, big[l])
        saved.append({**sv, **sv2})
    loss8, dx, dxb, dfinal = loss_head(x, _row(small["final_g"]), target)
    gs = {k: [None] * NL for k in SMALL[:-1]}
    gbig = [None] * NL
    for l in reversed(range(NL)):
        dx, dxb, gs1, gb1 = ffn_bwd(dx, dxb, small, l, big[l], saved[l])
        dx, dxb, gs2, gb2 = mixer_bwd(dx, dxb, small, l, big[l], saved[l], tables)
        gbig[l] = {**gb1, **gb2}
        for k, g in {**gs1, **gs2}.items():
            gs[k][l] = g
    return loss8, dx, gs, dfinal, gbig


def step_tables():
    perms = [perm_matrix(d) for d in DILATIONS[1:]]
    return rope_tables(T) + (perms, [p.T for p in perms])


def mixer_fwd(x, small, l, w, tables, after=None):
    cos2, sin2, perms, perms_t = tables
    ws_b = small["w_spatial"][l].astype(BF16)
    bias = jnp.repeat(small["b_spatial"][l].T, 128, axis=1)
    h = rms_fwd(x, _row(small["norm1_g"][l]), after)
    proj = mm_cols(h, w["win"])
    qkvs = qkv_split(proj, cos2, sin2, perms)
    outs = [attn_fwd(qkv) for qkv in qkvs]
    a, lse, mix = attn_merge([o for o, _ in outs], [s for _, s in outs], _row(small["mix_norm_attn_g"][l]), perms_t)
    mix = gmlp_fwd(proj, _row(small["gmlp_ln_g"][l]), ws_b, bias, _row(small["mix_norm_gmlp_g"][l]), mix)
    x_mid = mm_rows_res(mix, w["wout"], x)
    return x_mid, dict(x=x, h=h, proj=proj, qkvs=qkvs, a=a, lse=lse, mix=mix, ws_b=ws_b, bias=bias)


def ffn_fwd(x_mid, small, l, w, after=None):
    h2 = rms_fwd(x_mid, _row(small["norm2_g"][l]), after)
    gate, up, ff = mm_gateup(h2, w["wg"], w["wu"])
    x_out = mm_rows_res(ff, w["wd"], x_mid)
    return x_out, dict(x_mid=x_mid, h2=h2, gate=gate, up=up, ff=ff)


def ffn_bwd(dx, dxb, small, l, w, sv, after=None):
    gs = {}
    dgate, dup = mm_dff(dxb, w["wd"], sv["gate"], sv["up"], tk=FF // NCHIP, after=after)
    g_wd = mm_tn(sv["ff"], dxb, D // 2, False)
    dh2 = mm_nt_cols([dgate, dup], [w["wg"], w["wu"]])
    g_wg = mm_tn(sv["h2"], dgate, FF // NCHIP, True)
    g_wu = mm_tn(sv["h2"], dup, FF // NCHIP, True)
    dx, dxb, gs["norm2_g"] = rms_bwd(dh2, sv["x_mid"], _row(small["norm2_g"][l]), dx)
    return dx, dxb, gs, dict(wg=g_wg, wu=g_wu, wd=g_wd.reshape(NCHIP, FF // NCHIP, D))


def mixer_bwd(dx, dxb, small, l, w, sv, tables, after=None):
    cos2, sin2, perms, perms_t = tables
    gs = {}
    dmix = mm_nt_rows(dxb, w["wout"], tk=D // 2, after=after)
    g_wout = mm_tn(sv["mix"], dxb, D // 2, False)
    das, a_res, lse_res, gs["mix_norm_attn_g"] = attn_norm_bwd(dmix, sv["a"], sv["lse"],
                                                                _row(small["mix_norm_attn_g"][l]), perms)
    a_all = [sv["a"].reshape((1,) + sv["a"].shape)] + list(a_res)
    lse_all = [sv["lse"].reshape((1,) + sv["lse"].shape)] + list(lse_res)
    parts = [attn_bwd(*operands) for operands in zip(sv["qkvs"], a_all, das, lse_all)]
    wst = jnp.swapaxes(small["w_spatial"][l], 1, 2).astype(BF16)
    duv, gs["gmlp_ln_g"], gs["w_spatial"], gs["b_spatial"], gs["mix_norm_gmlp_g"] = gmlp_bwd(
        sv["proj"], dmix, _row(small["gmlp_ln_g"][l]), sv["ws_b"], wst, sv["bias"], _row(small["mix_norm_gmlp_g"][l]))
    dproj = dproj_assemble([p[0] for p in parts], [p[1] for p in parts], [p[2] for p in parts], duv, cos2, sin2,
                           perms_t)
    dh = mm_nt_cols([dproj], [w["win"]])
    g_win = mm_tn(sv["h"], dproj, DIN // NCHIP, True)
    dx, dxb, gs["norm1_g"] = rms_bwd(dh, sv["x"], _row(small["norm1_g"][l]), dx)
    return dx, dxb, gs, dict(win=g_win, wout=g_wout.reshape(NCHIP, D // NCHIP, D))


KINDS = ("win", "wout", "wg", "wu", "wd")
GROUPS = (("win", "wout"), ("wg", "wu", "wd"))
ANY = pl.BlockSpec(memory_space=pl.ANY)


def _place():
    return lax.axis_index("x"), lax.axis_index("y"), lax.axis_index("c")


def _other_chips(x, y):
    return [(1 - x, y), (x, 1 - y), (1 - x, 1 - y)]


def _remote(src, dst, send_sem, recv_sem, to):
    return pltpu.make_async_remote_copy(src_ref=src, dst_ref=dst, send_sem=send_sem, recv_sem=recv_sem,
                                        device_id=to, device_id_type=MESH)


def _hbm_call(body, *, name, n_in, out_shape, scratch, in_place=False):
    return pl.pallas_call(body, name=name, out_shape=tuple(out_shape), in_specs=[ANY] * n_in,
                          out_specs=tuple(ANY for _ in out_shape), scratch_shapes=list(scratch),
                          input_output_aliases={k: k for k in range(n_in)} if in_place else {},
                          compiler_params=pltpu.CompilerParams(vmem_limit_bytes=VMEM_LIMIT))


def place_vector():
    x, y, c = _place()
    return jnp.stack([c, 2 * x + y] + [2 * cx + cy for cx, cy in _other_chips(x, y)]).astype(jnp.int32)


PUSH_ROWS = 512
PUSH_SLOTS = 3


def _push_tiles(src, dst, buf, load_sem, send_sem, recv_sem, to, src_at, dst_at, nt):
    tr = buf.shape[1]
    n = len(src_at) * nt

    def pick(vals, seg):
        out = vals[0]
        for q in range(1, len(vals)):
            out = jnp.where(seg == q, vals[q], out)
        return out

    def tile(ref, at, t):
        seg = t // nt
        row = pl.multiple_of(pick([a[1] for a in at], seg) + (t - seg * nt) * tr, 16)
        return ref.at[pick([a[0] for a in at], seg), pl.ds(row, tr), :]

    def load(t):
        slot = t % PUSH_SLOTS
        return pltpu.make_async_copy(tile(src, src_at, t), buf.at[slot], load_sem.at[slot])

    def send(t):
        slot = t % PUSH_SLOTS
        return _remote(buf.at[slot], tile(dst, dst_at, t), send_sem.at[slot], recv_sem, to)

    load(0).start()

    def step(t, carry):
        load(t).wait()
        send(t).start()

        @pl.when(t >= PUSH_SLOTS - 1)
        def _():
            send(t - (PUSH_SLOTS - 1)).wait_send()

        @pl.when(t + 1 < n)
        def _():
            load(t + 1).start()

        return carry

    lax.fori_loop(0, n, step, 0)
    for t in range(max(n - (PUSH_SLOTS - 1), 0), n):
        send(t).wait_send()


def _await_tiles(dst, send_sem, recv_sem, to, nseg, rows):
    whole = dst.at[pl.ds(0, nseg), pl.ds(0, rows), :]
    _remote(whole, whole, send_sem.at[0], recv_sem, to).wait_recv()


def _push_rows(seg_rows):
    return _row_tile(seg_rows, PUSH_ROWS)


def _push_scratch(arrs, seg_rows):
    return ([pltpu.VMEM((PUSH_SLOTS, _push_rows(r), a.shape[-1]), a.dtype) for a, r in zip(arrs, seg_rows)]
            + [pltpu.SemaphoreType.DMA((len(arrs), PUSH_SLOTS)), pltpu.SemaphoreType.DMA((len(arrs), PUSH_SLOTS)),
               pltpu.SemaphoreType.DMA((len(arrs),))])


def cast_into(w, l, pos, after=None):
    _, r, cols = w.shape
    tr = _row_tile(r)
    extra, extra_specs = _behind(after)

    def body(pos_ref, w_ref, *rest):
        del pos_ref
        rest[-1][...] = w_ref[...].astype(BF16)

    return _call(body, name="cast_into", out_shape=_sds((NCHIP, r, cols), BF16), grid=(r // tr,),
                 in_specs=[pl.BlockSpec((None, tr, cols), lambda i, pos: (l, i, 0))] + extra_specs,
                 out_specs=pl.BlockSpec((None, tr, cols), lambda i, pos: (pos[1], i, 0)),
                 sem=("parallel",), prefetch=1)(pos, w, *extra)


HBM = pl.BlockSpec(memory_space=pltpu.HBM)
SEM = pl.BlockSpec(memory_space=pltpu.SEMAPHORE)
EFFECT = pltpu.SideEffectType.DATAFLOW_SIDE_EFFECTING


def _in_hbm(a):
    return pltpu.with_memory_space_constraint(a, pltpu.HBM)


def _gather_copies(bufs, send, recv):
    x, y, c = _place()
    chips = _other_chips(x, y)

    def half(k, chip):
        hr = bufs[k].shape[1] // 2
        return bufs[k].at[chip, pl.ds(c * hr, hr), :]

    out, back = [], []
    for k in range(len(bufs)):
        for j, (cx, cy) in enumerate(chips):
            i = 3 * k + j
            out.append(_remote(half(k, 2 * x + y), half(k, 2 * x + y), send.at[i], recv.at[i], (cx, cy, c)))
            back.append(_remote(half(k, 2 * x + y), half(k, 2 * cx + cy), send.at[i], recv.at[i], (cx, cy, c)))
    return out, back


def gather_start(fulls):
    nk = len(fulls)

    def body(*refs):
        send, recv = refs[nk], refs[nk + 1]
        bufs, token = refs[nk + 2:2 * nk + 2], refs[2 * nk + 2]
        for cp in _gather_copies(bufs, send, recv)[0]:
            cp.start()
        token[...] = jnp.zeros_like(token)

    res = pl.pallas_call(
        body, name="gather_start",
        out_shape=(pltpu.SemaphoreType.DMA((3 * nk,)), pltpu.SemaphoreType.DMA((3 * nk,)),
                   *[pltpu.HBM(f.shape, f.dtype) for f in fulls], _sds((8, LANES), F32)),
        in_specs=[HBM] * nk, out_specs=(SEM, SEM, *[HBM] * nk, pl.BlockSpec(memory_space=pltpu.VMEM)),
        input_output_aliases={k: 2 + k for k in range(nk)},
        compiler_params=pltpu.CompilerParams(has_side_effects=EFFECT))(*[_in_hbm(f) for f in fulls])
    return res[0], res[1], list(res[2:2 + nk]), res[2 + nk]


def gather_wait(send, recv, fulls, after):
    nk = len(fulls)

    def body(*refs):
        bufs, send_ref, recv_ref = refs[:nk], refs[nk], refs[nk + 1]
        for cp in _gather_copies(bufs, send_ref, recv_ref)[1]:
            cp.wait_send()
            cp.wait_recv()

    return list(pl.pallas_call(
        body, name="gather_wait", out_shape=tuple(pltpu.HBM(f.shape, f.dtype) for f in fulls),
        in_specs=[HBM] * nk + [SEM, SEM, ANY], out_specs=tuple([HBM] * nk),
        input_output_aliases={k: k for k in range(nk)},
        compiler_params=pltpu.CompilerParams(has_side_effects=EFFECT))(*fulls, send, recv, after))


def pair_forward(fulls):
    nk = len(fulls)

    def body(*refs):
        bufs = refs[nk:2 * nk]
        stage = refs[2 * nk:3 * nk]
        load_sem, send_sem, recv_sem = refs[3 * nk:]
        x, y, c = _place()
        sib = (x, y, 1 - c)
        chips = [2 * cx + cy for cx, cy in _other_chips(x, y)]
        for k in range(nk):
            hr = bufs[k].shape[1] // 2
            at = [(chip, c * hr) for chip in chips]
            _push_tiles(bufs[k], bufs[k], stage[k], load_sem.at[k], send_sem.at[k], recv_sem.at[k], sib, at, at,
                        hr // _push_rows(hr))
        for k in range(nk):
            _await_tiles(bufs[k], send_sem.at[k], recv_sem.at[k], sib, 3, bufs[k].shape[1] // 2)

    return _hbm_call(body, name="pair_forward", n_in=nk, out_shape=[_sds(f.shape, f.dtype) for f in fulls], in_place=True,
                     scratch=_push_scratch(fulls, [f.shape[1] // 2 for f in fulls]))(*fulls)


def rs_pair_send(gs):
    nk = len(gs)

    def body(*refs):
        ins, got = refs[:nk], refs[nk:2 * nk]
        stage = refs[2 * nk:3 * nk]
        load_sem, send_sem, recv_sem = refs[3 * nk:]
        x, y, c = _place()
        sib = (x, y, 1 - c)
        for k in range(nk):
            hr = ins[k].shape[1] // 2
            _push_tiles(ins[k], got[k], stage[k], load_sem.at[k], send_sem.at[k], recv_sem.at[k], sib,
                        [(s, (1 - c) * hr) for s in range(NCHIP)], [(s, 0) for s in range(NCHIP)], hr // _push_rows(hr))
        for k in range(nk):
            _await_tiles(got[k], send_sem.at[k], recv_sem.at[k], sib, NCHIP, ins[k].shape[1] // 2)

    half = [_sds((g.shape[0], g.shape[1] // 2, g.shape[2]), g.dtype) for g in gs]
    return _hbm_call(body, name="rs_pair_send", n_in=nk, out_shape=half,
                     scratch=_push_scratch(gs, [g.shape[1] // 2 for g in gs]))(*gs)


def _chip_copies(ps, lands, send, recv):
    x, y, c = _place()
    return [_remote(ps[k].at[2 * cx + cy], lands[k].at[j], send.at[3 * k + j], recv.at[3 * k + j], (cx, cy, c))
            for k in range(len(ps)) for j, (cx, cy) in enumerate(_other_chips(x, y))]


def chip_exchange_start(ps):
    nk = len(ps)
    lands = [lax.empty((3,) + p.shape[1:], p.dtype) for p in ps]

    def body(*refs):
        send, recv = refs[2 * nk], refs[2 * nk + 1]
        srcs, dsts, token = refs[2 * nk + 2:3 * nk + 2], refs[3 * nk + 2:4 * nk + 2], refs[4 * nk + 2]
        for cp in _chip_copies(srcs, dsts, send, recv):
            cp.start()
        token[...] = jnp.zeros_like(token)

    res = pl.pallas_call(
        body, name="chip_exchange_start",
        out_shape=(pltpu.SemaphoreType.DMA((3 * nk,)), pltpu.SemaphoreType.DMA((3 * nk,)),
                   *[pltpu.HBM(a.shape, a.dtype) for a in ps + lands], _sds((8, LANES), F32)),
        in_specs=[HBM] * (2 * nk), out_specs=(SEM, SEM, *[HBM] * (2 * nk), pl.BlockSpec(memory_space=pltpu.VMEM)),
        input_output_aliases={k: 2 + k for k in range(2 * nk)},
        compiler_params=pltpu.CompilerParams(has_side_effects=EFFECT))(*[_in_hbm(a) for a in ps + lands])
    return res[0], res[1], list(res[2:2 + nk]), list(res[2 + nk:2 + 2 * nk]), res[2 + 2 * nk]


def chip_exchange_wait(send, recv, ps, lands, after):
    nk = len(ps)

    def body(*refs):
        srcs, dsts, send_ref, recv_ref = refs[:nk], refs[nk:2 * nk], refs[2 * nk], refs[2 * nk + 1]
        for cp in _chip_copies(srcs, dsts, send_ref, recv_ref):
            cp.wait_send()
            cp.wait_recv()

    res = pl.pallas_call(
        body, name="chip_exchange_wait", out_shape=tuple(pltpu.HBM(a.shape, a.dtype) for a in ps + lands),
        in_specs=[HBM] * (2 * nk) + [SEM, SEM, ANY], out_specs=tuple([HBM] * (2 * nk)),
        input_output_aliases={k: k for k in range(2 * nk)},
        compiler_params=pltpu.CompilerParams(has_side_effects=EFFECT))(*ps, *lands, send, recv, after)
    return list(res[:nk]), list(res[nk:])


def rs_pair_share(fulls):
    nk = len(fulls)

    def body(*refs):
        bufs = refs[nk:2 * nk]
        stage = refs[2 * nk:3 * nk]
        load_sem, send_sem, recv_sem = refs[3 * nk:]
        x, y, c = _place()
        sib = (x, y, 1 - c)
        for k in range(nk):
            hr = bufs[k].shape[1]
            _push_tiles(bufs[k], bufs[k], stage[k], load_sem.at[k], send_sem.at[k], recv_sem.at[k], sib,
                        [(c, 0)], [(c, 0)], hr // _push_rows(hr))
        for k in range(nk):
            _await_tiles(bufs[k], send_sem.at[k], recv_sem.at[k], sib, 1, bufs[k].shape[1])

    return _hbm_call(body, name="rs_pair_share", n_in=nk, out_shape=[_sds(f.shape, f.dtype) for f in fulls], in_place=True,
                     scratch=_push_scratch(fulls, [f.shape[1] for f in fulls]))(*fulls)


N_DEV = 8


def allreduce_small(buf):
    rows = buf.shape[0]
    rp = rows // N_DEV

    def body(in_ref, out_ref, land_ref, send, recv):
        x, y, c = _place()
        mine = pl.ds(pl.multiple_of((4 * x + 2 * y + c) * rp, 8), rp)
        peers = [(x ^ fx, y ^ fy, c ^ fc) for fx in (0, 1) for fy in (0, 1) for fc in (0, 1)][1:]
        block = [pl.ds(pl.multiple_of((4 * px + 2 * py + pc) * rp, 8), rp) for px, py, pc in peers]
        scatter = [_remote(in_ref.at[block[r], :], land_ref.at[r], send.at[r], recv.at[r], peers[r]) for r in range(7)]
        for cp in scatter:
            cp.start()
        for cp in scatter:
            cp.wait()
        acc = in_ref[mine, :]
        for r in range(7):
            acc = acc + land_ref[r]
        out_ref[mine, :] = acc
        spread = [_remote(out_ref.at[mine, :], out_ref.at[mine, :], send.at[7 + r], recv.at[7 + r], peers[r]) for r in range(7)]
        for cp in spread:
            cp.start()
        for r in range(7):
            _remote(out_ref.at[block[r], :], out_ref.at[block[r], :], send.at[7 + r], recv.at[7 + r], peers[r]).wait_recv()
        for cp in spread:
            cp.wait_send()

    vm = pl.BlockSpec(memory_space=pltpu.VMEM)
    return pl.pallas_call(body, name="allreduce_small", out_shape=_sds(buf.shape, F32), in_specs=[vm], out_specs=vm,
                          scratch_shapes=[pltpu.VMEM((7, rp, LANES), F32), pltpu.SemaphoreType.DMA((14,)),
                                          pltpu.SemaphoreType.DMA((14,))])(buf)


def _row_tile(rows, cap=512):
    return max(t for t in range(16, cap + 1, 16) if rows % t == 0)


def add_halves(g, got, pos):
    s, r, cols = g.shape
    hr = r // 2
    tr = _row_tile(hr)
    nt = hr // tr

    def body(pos_ref, a_ref, b_ref, o_ref):
        del pos_ref
        o_ref[...] = (a_ref[...].astype(F32) + b_ref[...].astype(F32)).astype(BF16)

    blk = pl.BlockSpec((None, tr, cols), lambda q, i, pos: (q, i, 0))
    return _call(body, name="add_halves", out_shape=_sds((s, hr, cols), BF16), grid=(s, nt),
                 in_specs=[pl.BlockSpec((None, tr, cols), lambda q, i, pos: (q, pos[0] * nt + i, 0)), blk],
                 out_specs=blk, sem=("parallel", "parallel"), prefetch=1)(pos, g, got)


def add_slots(p, got, pos):
    _, hr, cols = p.shape
    tr = _row_tile(hr)

    def body(pos_ref, o_ref, g0, g1, g2, out_ref):
        del pos_ref
        out_ref[...] = ((o_ref[...].astype(F32) + g0[...].astype(F32)) + g1[...].astype(F32)) + g2[...].astype(F32)

    slots = [pl.BlockSpec((None, tr, cols), functools.partial(lambda j, i, pos: (j, i, 0), j)) for j in range(3)]
    return _call(body, name="add_slots", out_shape=_sds((2, hr, cols), F32), grid=(hr // tr,),
                 in_specs=[pl.BlockSpec((None, tr, cols), lambda i, pos: (pos[1], i, 0))] + slots,
                 out_specs=pl.BlockSpec((None, tr, cols), lambda i, pos: (pos[0], i, 0)),
                 sem=("parallel",), prefetch=1)(pos, p, got, got, got)


def _adamw_math(w, g, m, v):
    nm = B1 * m + (1.0 - B1) * g
    nv = B2 * v + (1.0 - B2) * (g * g)
    m_hat = nm / (1.0 - B1 ** STEP)
    v_hat = nv / (1.0 - B2 ** STEP)
    return -LR * (m_hat / (jnp.sqrt(v_hat) + AEPS) + WD * w), nm, nv


def adamw_layer(w, g, m, v, l, prev):
    nl, r, cols = w.shape
    tr = _row_tile(r, 256)

    def body(w_ref, g_ref, m_ref, v_ref, *rest):
        go_ref, d_ref, nm_ref, nv_ref = rest[-4:]
        gv = g_ref[...]
        d, nm, nv = _adamw_math(w_ref[...], gv, m_ref[...], v_ref[...])
        go_ref[...] = gv
        d_ref[...] = d
        nm_ref[...] = nm
        nv_ref[...] = nv

    lay = pl.BlockSpec((None, tr, cols), lambda i: (l, i, 0))
    o = _sds((nl, r, cols), F32)
    extra = [] if prev is None else list(prev)
    return _call(body, name="adamw_layer", out_shape=(o, o, o, o), grid=(r // tr,),
                 in_specs=[lay, pl.BlockSpec((tr, cols), lambda i: (i, 0)), lay, lay] + [ANY] * len(extra),
                 out_specs=(lay, lay, lay, lay), sem=("parallel",),
                 aliases={4 + j: j for j in range(len(extra))})(w, g, m, v, *extra)


def adamw(w, g, m, v):
    shape = w.shape
    cols = shape[-1]
    rows = w.size // cols
    tr = _row_tile(rows, 256)

    def body(w_ref, g_ref, m_ref, v_ref, d_ref, nm_ref, nv_ref):
        d_ref[...], nm_ref[...], nv_ref[...] = _adamw_math(w_ref[...], g_ref[...], m_ref[...], v_ref[...])

    blk = pl.BlockSpec((tr, cols), lambda i: (i, 0))
    o = _sds((rows, cols), F32)
    outs = _call(body, name="adamw", out_shape=(o, o, o), grid=(rows // tr,), in_specs=[blk] * 4, out_specs=(blk, blk, blk),
                 sem=("parallel",))(*[t.reshape(rows, cols) for t in (w, g, m, v)])
    return [t.reshape(shape) for t in outs]


SMALL = ("norm1_g", "gmlp_ln_g", "w_spatial", "b_spatial", "mix_norm_attn_g", "mix_norm_gmlp_g", "norm2_g", "final_g")
WEIGHTS = ("norm1_g", "w_in", "gmlp_ln_g", "w_spatial", "b_spatial", "mix_norm_attn_g", "mix_norm_gmlp_g", "w_out",
           "norm2_g", "w_gate", "w_up", "w_down", "final_g")
BIG = dict(win="w_in", wout="w_out", wg="w_gate", wu="w_up", wd="w_down")


def _pack(parts):
    flat = jnp.concatenate([parts[n].reshape(-1) for n in SMALL])
    rows = -(-flat.shape[0] // (LANES * 8 * N_DEV)) * 8 * N_DEV
    return jnp.pad(flat, (0, rows * LANES - flat.shape[0])).reshape(rows, LANES)


def _unpack(buf, like):
    flat = buf.reshape(-1)
    out, at = {}, 0
    for n in SMALL:
        out[n] = flat[at:at + like[n].size].reshape(like[n].shape)
        at += like[n].size
    return out


def kernel(x, norm1_g, w_in, gmlp_ln_g, w_spatial, b_spatial, mix_norm_attn_g, mix_norm_gmlp_g, w_out, norm2_g, w_gate, w_up, w_down, final_g, loss_target, m_norm1_g, m_w_in, m_gmlp_ln_g, m_w_spatial, m_b_spatial, m_mix_norm_attn_g, m_mix_norm_gmlp_g, m_w_out, m_norm2_g, m_w_gate, m_w_up, m_w_down, m_final_g, v_norm1_g, v_w_in, v_gmlp_ln_g, v_w_spatial, v_b_spatial, v_mix_norm_attn_g, v_mix_norm_gmlp_g, v_w_out, v_norm2_g, v_w_gate, v_w_up, v_w_down, v_final_g):
    w = dict(norm1_g=norm1_g, w_in=w_in, gmlp_ln_g=gmlp_ln_g, w_spatial=w_spatial, b_spatial=b_spatial,
             mix_norm_attn_g=mix_norm_attn_g, mix_norm_gmlp_g=mix_norm_gmlp_g, w_out=w_out, norm2_g=norm2_g,
             w_gate=w_gate, w_up=w_up, w_down=w_down, final_g=final_g)
    m = dict(norm1_g=m_norm1_g, w_in=m_w_in, gmlp_ln_g=m_gmlp_ln_g, w_spatial=m_w_spatial, b_spatial=m_b_spatial,
             mix_norm_attn_g=m_mix_norm_attn_g, mix_norm_gmlp_g=m_mix_norm_gmlp_g, w_out=m_w_out, norm2_g=m_norm2_g,
             w_gate=m_w_gate, w_up=m_w_up, w_down=m_w_down, final_g=m_final_g)
    v = dict(norm1_g=v_norm1_g, w_in=v_w_in, gmlp_ln_g=v_gmlp_ln_g, w_spatial=v_w_spatial, b_spatial=v_b_spatial,
             mix_norm_attn_g=v_mix_norm_attn_g, mix_norm_gmlp_g=v_mix_norm_gmlp_g, w_out=v_w_out, norm2_g=v_norm2_g,
             w_gate=v_w_gate, w_up=v_w_up, w_down=v_w_down, final_g=v_final_g)

    pos = place_vector()
    small = {n: w[n] for n in SMALL}
    tables = step_tables()

    def start_gathers(l, after):
        flights = []
        for kinds in GROUPS:
            flights.append(gather_start([cast_into(w[BIG[k]], l, pos, after) for k in kinds]))
            after = flights[-1][3]
        return flights, after

    def arrive(flight, kinds, after):
        fulls = pair_forward(gather_wait(flight[0], flight[1], flight[2], after))
        big = dict(zip(kinds, fulls))
        if "wout" in big:
            big["wout"] = big["wout"].reshape(D, D)
        if "wd" in big:
            big["wd"] = big["wd"].reshape(FF, D)
        return big

    act = x[0]
    flights, token = start_gathers(0, None)
    big, saved = [], []
    for l in range(NL):
        mine, after_mixer = flights, token
        if l + 1 < NL:
            flights, token = start_gathers(l + 1, act if l else token)
            after_mixer = token
        wl = arrive(mine[0], GROUPS[0], act if l else mine[0][3])
        x_mid, sv = mixer_fwd(act, small, l, wl, tables, after_mixer)
        wl.update(arrive(mine[1], GROUPS[1], x_mid))
        act, sv2 = ffn_fwd(x_mid, small, l, wl)
        big.append(wl)
        saved.append({**sv, **sv2})
    loss8, dx, dxb, dfinal = loss_head(act, _row(small["final_g"]), loss_target[0])
    loss = lax.psum(loss8[0, 0], ("x", "y", "c"))

    grads, delta, new_m, new_v = {}, {}, {}, {}
    stacked = {k: None for k in KINDS}
    gs = {k: [None] * NL for k in SMALL[:-1]}

    def send_off(gbig, kinds, l):
        partial_grads = [gbig[k] for k in kinds]
        pair = [add_halves(g, got, pos) for g, got in zip(partial_grads, rs_pair_send(partial_grads))]
        return chip_exchange_start(pair) + (kinds, l)

    def finish(flight, after):
        pair, lands = chip_exchange_wait(*flight[:4], after)
        halves = [add_slots(p, got, pos) for p, got in zip(pair, lands)]
        for k, full in zip(flight[5], rs_pair_share(halves)):
            n = BIG[k]
            stacked[k] = adamw_layer(w[n], full.reshape(w[n].shape[1:]), m[n], v[n], flight[6], stacked[k])

    pending, token = [], None
    for l in reversed(range(NL)):
        dx, dxb, gs1, gbig = ffn_bwd(dx, dxb, small, l, big[l], saved[l], token)
        for flight in pending:
            finish(flight, dx)
        ffn_flight = send_off(gbig, GROUPS[1], l)
        dx, dxb, gs2, gbig = mixer_bwd(dx, dxb, small, l, big[l], saved[l], tables, ffn_flight[4])
        mixer_flight = send_off(gbig, GROUPS[0], l)
        pending, token = [ffn_flight, mixer_flight], mixer_flight[4]
        for k, g in {**gs1, **gs2}.items():
            gs[k][l] = g
    for flight in pending:
        finish(flight, token)
    for k in KINDS:
        grads[BIG[k]], delta[BIG[k]], new_m[BIG[k]], new_v[BIG[k]] = stacked[k]
    part = {n: jnp.stack(gs[n]).reshape(w[n].shape) for n in SMALL[:-1]}
    part["final_g"] = dfinal.reshape(w["final_g"].shape)
    grads.update(_unpack(allreduce_small(_pack(part)), part))

    sm = adamw(_pack({n: w[n] for n in SMALL}), _pack({n: grads[n] for n in SMALL}), _pack({n: m[n] for n in SMALL}),
               _pack({n: v[n] for n in SMALL}))
    for res, packed in zip((delta, new_m, new_v), sm):
        res.update(_unpack(packed, part))

    return (loss, dx.reshape(x.shape), *[grads[n] for n in WEIGHTS], *[delta[n] for n in WEIGHTS],
            *[new_m[n] for n in WEIGHTS], *[new_v[n] for n in WEIGHTS])
```

```python
import functools

import jax
import jax.numpy as jnp
from jax import lax
from jax.experimental import pallas as pl
from jax.experimental.pallas import tpu as pltpu

F32 = jnp.float32
BF16 = jnp.bfloat16

D = 2048
T = 4096
NL = 4
HD = 128
DA = D // 2
DG = D - DA
NH = DA // HD
NG = DG // 128
CH = 128
DIN = 3 * DA + 2 * DG
FF = 5632
DILATIONS = (1, 4, 16)
NSIDE = 64
ROPE_THETA = 10000.0
EPS = 1e-6
NEG = -1e30
SCALE = HD ** -0.5
NCHIP = 4

LR, B1, B2, AEPS, WD, STEP = 0.001, 0.9, 0.999, 1e-08, 0.01, 10

LANES = 128
MXU_COLS = 256
VMEM_LIMIT = 56 * 1024 * 1024
MESH = pl.DeviceIdType.MESH


def _call(body, *, name, out_shape, grid=(), in_specs=None, out_specs=None, scratch=(), sem=None, aliases=None,
          prefetch=0):
    params = dict(vmem_limit_bytes=VMEM_LIMIT)
    if sem is not None:
        params["dimension_semantics"] = sem
    if prefetch:
        spec = pltpu.PrefetchScalarGridSpec(num_scalar_prefetch=prefetch, grid=grid, in_specs=in_specs,
                                            out_specs=out_specs, scratch_shapes=list(scratch))
        return pl.pallas_call(body, name=name, out_shape=out_shape, grid_spec=spec,
                              input_output_aliases=aliases or {}, compiler_params=pltpu.CompilerParams(**params))
    kw = {}
    if grid:
        kw["grid"] = grid
    return pl.pallas_call(
        body, name=name, out_shape=out_shape, in_specs=in_specs, out_specs=out_specs,
        scratch_shapes=list(scratch), input_output_aliases=aliases or {},
        compiler_params=pltpu.CompilerParams(**params), **kw)


def _sds(shape, dtype):
    return jax.ShapeDtypeStruct(tuple(shape), dtype)


def _dot(a, b):
    return jnp.dot(a, b, preferred_element_type=F32)


def _dot_nt(a, b):
    return lax.dot_general(a, b, (((1,), (1,)), ((), ())), preferred_element_type=F32)


def _dot_tn(a, b):
    return lax.dot_general(a, b, (((0,), (0,)), ((), ())), preferred_element_type=F32)


def _sigmoid(x):
    return 0.5 * jnp.tanh(0.5 * x) + 0.5


def _behind(after):
    return ([], []) if after is None else ([after], [pl.BlockSpec(memory_space=pl.ANY)])


def rms_fwd(x, g, after=None):
    n, d = x.shape
    tm = 256
    extra, extra_specs = _behind(after)

    def body(x_ref, g_ref, *rest):
        h_ref = rest[-1]
        xv = x_ref[...]
        r = lax.rsqrt(jnp.mean(xv * xv, axis=-1, keepdims=True) + EPS)
        h_ref[...] = (xv * r * g_ref[...]).astype(BF16)

    return _call(body, name="rms_fwd", out_shape=_sds((n, d), BF16), grid=(n // tm,),
                 in_specs=[pl.BlockSpec((tm, d), lambda i: (i, 0)), pl.BlockSpec((1, d), lambda i: (0, 0))] + extra_specs,
                 out_specs=pl.BlockSpec((tm, d), lambda i: (i, 0)), sem=("parallel",))(x, g, *extra)


def rms_bwd(dh, x, g, dres):
    n, d = x.shape
    tm = 256

    def body(dh_ref, x_ref, g_ref, dres_ref, dx_ref, dxb_ref, dg_ref):
        @pl.when(pl.program_id(0) == 0)
        def _():
            dg_ref[...] = jnp.zeros_like(dg_ref)

        xv = x_ref[...]
        r = lax.rsqrt(jnp.mean(xv * xv, axis=-1, keepdims=True) + EPS)
        xhat = xv * r
        dhv = dh_ref[...].astype(F32)
        dg_ref[...] += jnp.sum(dhv * xhat, axis=0, keepdims=True)
        dxn = dhv * g_ref[...]
        dx = dres_ref[...] + r * (dxn - xhat * jnp.mean(dxn * xhat, axis=-1, keepdims=True))
        dx_ref[...] = dx
        dxb_ref[...] = dx.astype(BF16)

    row = pl.BlockSpec((tm, d), lambda i: (i, 0))
    vec = pl.BlockSpec((1, d), lambda i: (0, 0))
    return _call(body, name="rms_bwd", out_shape=(_sds((n, d), F32), _sds((n, d), BF16), _sds((1, d), F32)),
                 grid=(n // tm,), in_specs=[row, row, vec, row], out_specs=(row, row, vec), sem=("arbitrary",))(dh, x, g, dres)


def loss_head(x, g, target):
    n, d = x.shape
    tm = 256

    def body(x_ref, g_ref, t_ref, loss_ref, dx_ref, dxb_ref, dg_ref):
        @pl.when(pl.program_id(0) == 0)
        def _():
            dg_ref[...] = jnp.zeros_like(dg_ref)
            loss_ref[...] = jnp.zeros_like(loss_ref)

        xv = x_ref[...]
        r = lax.rsqrt(jnp.mean(xv * xv, axis=-1, keepdims=True) + EPS)
        xhat = xv * r
        gv = g_ref[...]
        err = xhat * gv - t_ref[...]
        loss_ref[...] += 0.5 * jnp.sum(jnp.mean(err * err, axis=-1, keepdims=True))
        dy = err * (1.0 / d)
        dg_ref[...] += jnp.sum(dy * xhat, axis=0, keepdims=True)
        dxn = dy * gv
        dx = r * (dxn - xhat * jnp.mean(dxn * xhat, axis=-1, keepdims=True))
        dx_ref[...] = dx
        dxb_ref[...] = dx.astype(BF16)

    row = pl.BlockSpec((tm, d), lambda i: (i, 0))
    vec = pl.BlockSpec((1, d), lambda i: (0, 0))
    return _call(body, name="loss_head",
                 out_shape=(_sds((8, LANES), F32), _sds((n, d), F32), _sds((n, d), BF16), _sds((1, d), F32)),
                 grid=(n // tm,), in_specs=[row, vec, row],
                 out_specs=(pl.BlockSpec((8, LANES), lambda i: (0, 0)), row, row, vec), sem=("arbitrary",))(x, g, target)


def mm_cols(a, wc, tm=512):
    n, k = a.shape
    s, _, nq = wc.shape

    def body(a_ref, w_ref, o_ref):
        o_ref[...] = _dot(a_ref[...], w_ref[...]).astype(BF16)

    return _call(body, name="mm_cols", out_shape=_sds((n, s * nq), BF16), grid=(s, n // tm),
                 in_specs=[pl.BlockSpec((tm, k), lambda j, i: (i, 0)), pl.BlockSpec((None, k, nq), lambda j, i: (j, 0, 0))],
                 out_specs=pl.BlockSpec((tm, nq), lambda j, i: (i, j)), sem=("parallel", "parallel"))(a, wc)


def mm_gateup(h, wg, wu, tm=512):
    n, k = h.shape
    s, _, nq = wg.shape

    def body(h_ref, wg_ref, wu_ref, gate_ref, up_ref, ff_ref):
        hv = h_ref[...]
        gate = _dot(hv, wg_ref[...])
        up = _dot(hv, wu_ref[...])
        gate_ref[...] = gate.astype(BF16)
        up_ref[...] = up.astype(BF16)
        ff_ref[...] = (gate * _sigmoid(gate) * up).astype(BF16)

    wspec = pl.BlockSpec((None, k, nq), lambda j, i: (j, 0, 0))
    ospec = pl.BlockSpec((tm, nq), lambda j, i: (i, j))
    o = _sds((n, s * nq), BF16)
    return _call(body, name="mm_gateup", out_shape=(o, o, o), grid=(s, n // tm),
                 in_specs=[pl.BlockSpec((tm, k), lambda j, i: (i, 0)), wspec, wspec],
                 out_specs=(ospec, ospec, ospec), sem=("parallel", "parallel"))(h, wg, wu)


def mm_rows_res(a, wr, res, tm=512, tn=1024):
    n, k = a.shape
    _, nn = wr.shape

    def body(a_ref, w_ref, r_ref, o_ref):
        o_ref[...] = r_ref[...] + _dot(a_ref[...], w_ref[...])

    return _call(body, name="mm_rows_res", out_shape=_sds((n, nn), F32), grid=(nn // tn, n // tm),
                 in_specs=[pl.BlockSpec((tm, k), lambda j, i: (i, 0)), pl.BlockSpec((k, tn), lambda j, i: (0, j)),
                           pl.BlockSpec((tm, tn), lambda j, i: (i, j))],
                 out_specs=pl.BlockSpec((tm, tn), lambda j, i: (i, j)), sem=("parallel", "parallel"))(a, wr, res)


def mm_nt_rows(dy, wr, tk, tm=512, after=None):
    n, nn = dy.shape
    k, _ = wr.shape
    extra, extra_specs = _behind(after)

    def body(dy_ref, w_ref, *rest):
        rest[-1][...] = _dot_nt(dy_ref[...], w_ref[...]).astype(BF16)

    return _call(body, name="mm_nt_rows", out_shape=_sds((n, k), BF16), grid=(k // tk, n // tm),
                 in_specs=[pl.BlockSpec((tm, nn), lambda j, i: (i, 0)), pl.BlockSpec((tk, nn), lambda j, i: (j, 0))]
                 + extra_specs,
                 out_specs=pl.BlockSpec((tm, tk), lambda j, i: (i, j)), sem=("parallel", "parallel"))(dy, wr, *extra)


def mm_dff(dy, wd, gate, up, tk, tm=512, after=None):
    n, nn = dy.shape
    k, _ = wd.shape
    extra, extra_specs = _behind(after)

    def body(dy_ref, w_ref, gate_ref, up_ref, *rest):
        dgate_ref, dup_ref = rest[-2:]
        dy_v = dy_ref[...]
        for c0 in range(0, tk, MXU_COLS):
            cols = slice(c0, min(c0 + MXU_COLS, tk))
            dff = _dot_nt(dy_v, w_ref[cols, :])
            gate = gate_ref[:, cols].astype(F32)
            up = up_ref[:, cols].astype(F32)
            sig = _sigmoid(gate)
            dgate_ref[:, cols] = (dff * up * (sig * (1.0 + gate * (1.0 - sig)))).astype(BF16)
            dup_ref[:, cols] = (dff * (gate * sig)).astype(BF16)

    tile = pl.BlockSpec((tm, tk), lambda j, i: (i, j))
    o = _sds((n, k), BF16)
    return _call(body, name="mm_dff", out_shape=(o, o), grid=(k // tk, n // tm),
                 in_specs=[pl.BlockSpec((tm, nn), lambda j, i: (i, 0)), pl.BlockSpec((tk, nn), lambda j, i: (j, 0)), tile, tile]
                 + extra_specs,
                 out_specs=(tile, tile), sem=("parallel", "parallel"))(dy, wd, gate, up, *extra)


def mm_nt_cols(das, wcs, tm=512):
    n = das[0].shape[0]
    s, k, nq = wcs[0].shape
    npair = len(das)

    def body(*refs):
        da_refs, w_refs, o_ref, acc = refs[:npair], refs[npair:2 * npair], refs[2 * npair], refs[2 * npair + 1]
        ss = pl.program_id(1)

        @pl.when(ss == 0)
        def _():
            acc[...] = jnp.zeros_like(acc)

        for da_ref, w_ref in zip(da_refs, w_refs):
            acc[...] += _dot_nt(da_ref[...], w_ref[...])

        @pl.when(ss == s - 1)
        def _():
            o_ref[...] = acc[...].astype(BF16)

    return _call(body, name="mm_nt_cols%d" % npair, out_shape=_sds((n, k), BF16), grid=(n // tm, s),
                 in_specs=[pl.BlockSpec((tm, nq), lambda i, ss: (i, ss))] * npair
                 + [pl.BlockSpec((None, k, nq), lambda i, ss: (ss, 0, 0))] * npair,
                 out_specs=pl.BlockSpec((tm, k), lambda i, ss: (i, 0)),
                 scratch=[pltpu.VMEM((tm, k), F32)], sem=("parallel", "arbitrary"))(*das, *wcs)


def mm_tn(a, b, nq, shard_major, tka=512):
    n, ka = a.shape
    nb = b.shape[1] // nq

    def body(a_ref, b_ref, o_ref):
        o_ref[...] = _dot_tn(a_ref[...], b_ref[...]).astype(BF16)

    if shard_major:
        out_shape, out_spec = _sds((nb, ka, nq), BF16), pl.BlockSpec((None, tka, nq), lambda j, i: (j, i, 0))
    else:
        out_shape, out_spec = _sds((ka, nb * nq), BF16), pl.BlockSpec((tka, nq), lambda j, i: (i, j))
    return _call(body, name="mm_tn", out_shape=out_shape, grid=(nb, ka // tka),
                 in_specs=[pl.BlockSpec((n, tka), lambda j, i: (0, i)), pl.BlockSpec((n, nq), lambda j, i: (0, j))],
                 out_specs=out_spec, sem=("parallel", "parallel"))(a, b)


def rope_tables(n):
    pos = jnp.arange(n, dtype=F32)
    inv = ROPE_THETA ** (-jnp.arange(0, HD, 2, dtype=F32) / HD)
    ang = pos[:, None] * inv[None, :]
    cos, sin = jnp.cos(ang), jnp.sin(ang)
    return jnp.concatenate([cos, cos], axis=-1), jnp.concatenate([-sin, sin], axis=-1)


PERM_ROWS = 256


def perm_matrix(d):
    n = PERM_ROWS // d
    i = jnp.arange(PERM_ROWS)
    src = (i % n) * d + i // n
    return (src[:, None] == jnp.arange(PERM_ROWS)[None, :]).astype(BF16)


def _perm_f32(p, x):
    hi = x.astype(BF16)
    rest = x - hi.astype(F32)
    mid = rest.astype(BF16)
    lo = (rest - mid.astype(F32)).astype(BF16)
    return (_dot(p, hi) + _dot(p, mid)) + _dot(p, lo)


def _res_spec(d, cols, col_block=0):
    return pl.BlockSpec((d, PERM_ROWS // d, cols), lambda i: (0, i, col_block))


def _perm_specs():
    return [pl.BlockSpec((PERM_ROWS, PERM_ROWS), lambda i: (0, 0))] * (len(DILATIONS) - 1)


def qkv_split(proj, cos2, sin2, perms):
    n = proj.shape[0]
    tm = PERM_ROWS

    def body(p_ref, c_ref, s_ref, *rest):
        perm_refs, outs = rest[:len(DILATIONS) - 1], rest[len(DILATIONS) - 1:]
        c, s = c_ref[...], s_ref[...]
        for h in range(2 * NH):
            t = p_ref[:, h * HD:(h + 1) * HD].astype(F32)
            outs[0][0, :, h * HD:(h + 1) * HD] = (t * c + pltpu.roll(t, HD // 2, 1) * s).astype(BF16)
        outs[0][0, :, 2 * DA:] = p_ref[:, 2 * DA:]
        nat = outs[0][0]
        for o_ref, p_ref_d, d in zip(outs[1:], perm_refs, DILATIONS[1:]):
            o_ref[...] = _dot(p_ref_d[...], nat).astype(BF16).reshape(d, tm // d, 3 * DA)

    tab = pl.BlockSpec((tm, HD), lambda i: (i, 0))
    return _call(body, name="qkv_split", out_shape=tuple(_sds((d, n // d, 3 * DA), BF16) for d in DILATIONS),
                 grid=(n // tm,), in_specs=[pl.BlockSpec((tm, 3 * DA), lambda i: (i, 0)), tab, tab] + _perm_specs(),
                 out_specs=tuple(_res_spec(d, 3 * DA) for d in DILATIONS), sem=("parallel",))(proj, cos2, sin2, *perms)


def dproj_assemble(dqs, dks, dvs, duv, cos2, sin2, perms_t):
    n = duv.shape[0]
    tm = PERM_ROWS
    npat = len(DILATIONS)

    def body(*refs):
        parts = refs[0:npat], refs[npat:2 * npat], refs[2 * npat:3 * npat]
        duv_ref, c_ref, s_ref = refs[3 * npat:3 * npat + 3]
        pt_refs, o_ref = refs[3 * npat + 3:-1], refs[-1]
        c, s = c_ref[...], s_ref[...]
        for part, part_refs in enumerate(parts):
            t = part_refs[0][0].astype(F32)
            for ref, pt_ref in zip(part_refs[1:], pt_refs):
                t = t + _dot(pt_ref[...], ref[...].reshape(tm, DA))
            if part == 2:
                o_ref[:, 2 * DA:3 * DA] = t.astype(BF16)
                continue
            for h in range(NH):
                th = t[:, h * HD:(h + 1) * HD]
                o_ref[:, part * DA + h * HD:part * DA + (h + 1) * HD] = (th * c - pltpu.roll(th, HD // 2, 1) * s).astype(BF16)
        o_ref[:, 3 * DA:] = duv_ref[...]

    blks = [_res_spec(d, DA) for d in DILATIONS]
    tab = pl.BlockSpec((tm, HD), lambda i: (i, 0))
    return _call(body, name="dproj_assemble", out_shape=_sds((n, DIN), BF16), grid=(n // tm,),
                 in_specs=blks * 3 + [pl.BlockSpec((tm, 2 * DG), lambda i: (i, 0)), tab, tab] + _perm_specs(),
                 out_specs=pl.BlockSpec((tm, DIN), lambda i: (i, 0)), sem=("parallel",))(
        *dqs, *dks, *dvs, duv, cos2, sin2, *perms_t)


ATTN_BLOCK_BYTES = 2 << 20


def _attn_tiles(ls):
    tq = min(128, ls)
    w = min(tq + 2 * NSIDE, ls)
    return tq, w, ls // tq


def _attn_heads(ls):
    return NH if ls * DA * 2 <= ATTN_BLOCK_BYTES else 1


ATTN_UNROLL = 8


def _attn_unroll(nt, hb):
    return min(nt, max(1, ATTN_UNROLL // hb))


def _band(t, tq, w, ls):
    q0 = pl.multiple_of(t * tq, tq)
    ks = pl.multiple_of(jnp.clip(t * tq - NSIDE, 0, ls - w), NSIDE)
    qpos = q0 + lax.broadcasted_iota(jnp.int32, (tq, w), 0)
    kpos = ks + lax.broadcasted_iota(jnp.int32, (tq, w), 1)
    return q0, ks, jnp.abs(kpos - qpos) <= NSIDE


def _attn_specs(ls, hb):
    nhb = NH // hb
    q = pl.BlockSpec((None, ls, hb * HD), lambda g: (g // nhb, 0, g % nhb))
    k = pl.BlockSpec((None, ls, hb * HD), lambda g: (g // nhb, 0, nhb + g % nhb))
    v = pl.BlockSpec((None, ls, hb * HD), lambda g: (g // nhb, 0, 2 * nhb + g % nhb))
    return q, k, v, q


def _lse_spec(ls, hb):
    nhb = NH // hb
    return pl.BlockSpec((None, ls, LANES), lambda g: (g // nhb, 0, 0))


def attn_fwd(qkv):
    d, ls, _ = qkv.shape
    tq, w, nt = _attn_tiles(ls)
    hb = _attn_heads(ls)
    nhb = NH // hb

    def body(q_ref, k_ref, v_ref, o_ref, l_ref):
        first = (pl.program_id(0) % nhb) * hb
        lane = lax.broadcasted_iota(jnp.int32, (tq, LANES), 1)

        @pl.when(first == 0)
        def _():
            l_ref[...] = jnp.zeros_like(l_ref)

        def tile(t, carry):
            q0, ks, valid = _band(t, tq, w, ls)
            lt = l_ref[pl.ds(q0, tq), :]
            for h in range(hb):
                cols = slice(h * HD, (h + 1) * HD)
                s = _dot_nt(q_ref[pl.ds(q0, tq), cols], k_ref[pl.ds(ks, w), cols]) * SCALE
                s = jnp.where(valid, s, NEG)
                m = jnp.max(s, axis=1, keepdims=True)
                p = jnp.exp(s - m)
                l = jnp.sum(p, axis=1, keepdims=True)
                o = _dot(p.astype(BF16), v_ref[pl.ds(ks, w), cols]) / l
                o_ref[pl.ds(q0, tq), cols] = o.astype(BF16)
                lt = jnp.where(lane == first + h, m + jnp.log(l), lt)
            l_ref[pl.ds(q0, tq), :] = lt
            return carry

        lax.fori_loop(0, nt, tile, 0, unroll=_attn_unroll(nt, hb))

    q, k, v, o = _attn_specs(ls, hb)
    return _call(body, name="attn_fwd_d%d" % d, out_shape=(_sds((d, ls, DA), BF16), _sds((d, ls, LANES), F32)),
                 grid=(d * NH // hb,), in_specs=[q, k, v], out_specs=(o, _lse_spec(ls, hb)), sem=("arbitrary",))(
        qkv, qkv, qkv)


def attn_merge(os_, ls_, ga, perms_t):
    n = os_[0].shape[1]
    tm = PERM_ROWS
    npat = len(DILATIONS)

    def body(*refs):
        o_refs, l_refs = refs[:npat], refs[npat:2 * npat]
        g_ref, pt_refs = refs[2 * npat], refs[2 * npat + 1:3 * npat]
        a_ref, lse_ref, mix_ref = refs[3 * npat:]
        ov = [o_refs[0][0].astype(F32)] + [_dot(pt[...], o[...].reshape(tm, DA)) for o, pt in zip(o_refs[1:], pt_refs)]
        lv = [l_refs[0][0]] + [_perm_f32(pt[...], l[...].reshape(tm, LANES)) for l, pt in zip(l_refs[1:], pt_refs)]
        lmax = functools.reduce(jnp.maximum, lv)
        e = [jnp.exp(x - lmax) for x in lv]
        den = functools.reduce(lambda p, q: p + q, e)
        wts = [ei / den for ei in e]
        lse_ref[...] = lmax + jnp.log(den)
        heads = []
        for h in range(NH):
            cols = slice(h * HD, (h + 1) * HD)
            heads.append(functools.reduce(lambda p, q: p + q, [wt[:, h:h + 1] * oi[:, cols] for wt, oi in zip(wts, ov)]))
        a = jnp.concatenate(heads, axis=1)
        a_ref[...] = a.astype(BF16)
        r = lax.rsqrt(jnp.mean(a * a, axis=-1, keepdims=True) + EPS)
        mix_ref[...] = (a * r * g_ref[...]).astype(BF16)

    blk = pl.BlockSpec((tm, DA), lambda i: (i, 0))
    return _call(body, name="attn_merge", out_shape=(_sds((n, DA), BF16), _sds((n, LANES), F32), _sds((n, D), BF16)),
                 grid=(n // tm,),
                 in_specs=[_res_spec(d, DA) for d in DILATIONS] + [_res_spec(d, LANES) for d in DILATIONS]
                 + [pl.BlockSpec((1, DA), lambda i: (0, 0))] + _perm_specs(),
                 out_specs=(blk, pl.BlockSpec((tm, LANES), lambda i: (i, 0)), blk), sem=("parallel",))(
        *os_, *ls_, ga, *perms_t)


def attn_norm_bwd(dmix, a, lse, ga, perms):
    n = a.shape[0]
    tm = PERM_ROWS
    npat = len(DILATIONS)

    def body(dm_ref, a_ref, lse_ref, g_ref, *rest):
        p_refs = rest[:npat - 1]
        da_refs, a_refs, lse_refs = rest[npat - 1:2 * npat - 1], rest[2 * npat - 1:3 * npat - 2], rest[3 * npat - 2:4 * npat - 3]
        dg_ref = rest[-1]

        @pl.when(pl.program_id(0) == 0)
        def _():
            dg_ref[...] = jnp.zeros_like(dg_ref)

        ab = a_ref[...]
        av = ab.astype(F32)
        r = lax.rsqrt(jnp.mean(av * av, axis=-1, keepdims=True) + EPS)
        ahat = av * r
        dm = dm_ref[...].astype(F32)
        dg_ref[...] += jnp.sum(dm * ahat, axis=0, keepdims=True)
        dn = dm * g_ref[...]
        da = (r * (dn - ahat * jnp.mean(dn * ahat, axis=-1, keepdims=True))).astype(BF16)
        da_refs[0][0] = da
        lv = lse_ref[...]
        for i, d in enumerate(DILATIONS[1:]):
            p = p_refs[i][...]
            da_refs[i + 1][...] = _dot(p, da).astype(BF16).reshape(d, tm // d, DA)
            a_refs[i][...] = _dot(p, ab).astype(BF16).reshape(d, tm // d, DA)
            lse_refs[i][...] = _perm_f32(p, lv).reshape(d, tm // d, LANES)

    blk = pl.BlockSpec((tm, DA), lambda i: (i, 0))
    vec = pl.BlockSpec((1, DA), lambda i: (0, 0))
    res = [_res_spec(d, DA) for d in DILATIONS]
    outs = _call(body, name="attn_norm_bwd",
                 out_shape=tuple([_sds((d, n // d, DA), BF16) for d in DILATIONS]
                                 + [_sds((d, n // d, DA), BF16) for d in DILATIONS[1:]]
                                 + [_sds((d, n // d, LANES), F32) for d in DILATIONS[1:]] + [_sds((1, DA), F32)]),
                 grid=(n // tm,),
                 in_specs=[blk, blk, pl.BlockSpec((tm, LANES), lambda i: (i, 0)), vec] + _perm_specs(),
                 out_specs=tuple(res + res[1:] + [_res_spec(d, LANES) for d in DILATIONS[1:]] + [vec]),
                 sem=("arbitrary",))(dmix, a, lse, ga, *perms)
    return outs[:npat], outs[npat:2 * npat - 1], outs[2 * npat - 1:3 * npat - 2], outs[-1]


def attn_bwd(qkv, a, da, lse):
    d, ls, _ = qkv.shape
    tq, w, nt = _attn_tiles(ls)
    hb = _attn_heads(ls)

    nhb = NH // hb

    def body(q_ref, k_ref, v_ref, a_ref, da_ref, lse_ref, dq_ref, dk_ref, dv_ref, dk_acc, dv_acc):
        first = (pl.program_id(0) % nhb) * hb
        lane = lax.broadcasted_iota(jnp.int32, (tq, LANES), 1)
        dk_acc[...] = jnp.zeros_like(dk_acc)
        dv_acc[...] = jnp.zeros_like(dv_acc)

        def tile(t, carry):
            q0, ks, valid = _band(t, tq, w, ls)
            lt = lse_ref[pl.ds(q0, tq), :]
            for h in range(hb):
                cols = slice(h * HD, (h + 1) * HD)
                q = q_ref[pl.ds(q0, tq), cols]
                k = k_ref[pl.ds(ks, w), cols]
                v = v_ref[pl.ds(ks, w), cols]
                do = da_ref[pl.ds(q0, tq), cols]
                lse_h = lt[:, h:h + 1] if nhb == 1 else jnp.sum(jnp.where(lane == first + h, lt, 0.0), axis=1, keepdims=True)
                s = jnp.where(valid, _dot_nt(q, k) * SCALE, NEG)
                p = jnp.exp(s - lse_h)
                drow = jnp.sum(do.astype(F32) * a_ref[pl.ds(q0, tq), cols].astype(F32), axis=1, keepdims=True)
                ds = (p * (_dot_nt(do, v) - drow) * SCALE).astype(BF16)
                dv_acc[pl.ds(ks, w), cols] += _dot_tn(p.astype(BF16), do)
                dk_acc[pl.ds(ks, w), cols] += _dot_tn(ds, q)
                dq_ref[pl.ds(q0, tq), cols] = _dot(ds, k).astype(BF16)
            return carry

        lax.fori_loop(0, nt, tile, 0, unroll=_attn_unroll(nt, hb))
        dk_ref[...] = dk_acc[...].astype(BF16)
        dv_ref[...] = dv_acc[...].astype(BF16)

    q, k, v, o = _attn_specs(ls, hb)
    out = _sds((d, ls, DA), BF16)
    return _call(body, name="attn_bwd_d%d" % d, out_shape=(out, out, out), grid=(d * NH // hb,),
                 in_specs=[q, k, v, o, o, _lse_spec(ls, hb)], out_specs=(o, o, o),
                 scratch=[pltpu.VMEM((ls, hb * HD), F32), pltpu.VMEM((ls, hb * HD), F32)], sem=("parallel",))(
        qkv, qkv, qkv, a, da, lse)


GM_TM = 256
INV_SQRT2 = 0.7071067811865476
INV_SQRT2PI = 0.3989422804014327


def _gelu(x):
    return 0.5 * x * (1.0 + lax.erf(x * INV_SQRT2))


def _gelu_grad(x):
    return 0.5 * (1.0 + lax.erf(x * INV_SQRT2)) + x * (INV_SQRT2PI * jnp.exp(-0.5 * x * x))


def _gmlp_forward(up, vp, ln_g, ws_ref, bias):
    u = _gelu(up)
    v = _gelu(vp)
    vc = v - jnp.mean(v, axis=-1, keepdims=True)
    rs = lax.rsqrt(jnp.mean(vc * vc, axis=-1, keepdims=True) + EPS)
    vhat = vc * rs
    vln = (vhat * ln_g).astype(BF16)
    rows = []
    for c in range(GM_TM // CH):
        cols = [_dot(ws_ref[g], vln[c * CH:(c + 1) * CH, g * 128:(g + 1) * 128]) for g in range(NG)]
        rows.append(jnp.concatenate(cols, axis=1) + bias)
    return u, vhat, rs, vln, jnp.concatenate(rows, axis=0)


def _gmlp_specs():
    ublk = pl.BlockSpec((GM_TM, DG), lambda i: (i, 3 * DA // DG))
    vblk = pl.BlockSpec((GM_TM, DG), lambda i: (i, 3 * DA // DG + 1))
    vec = pl.BlockSpec((1, DG), lambda i: (0, 0))
    wsp = pl.BlockSpec((NG, CH, CH), lambda i: (0, 0, 0))
    bsp = pl.BlockSpec((CH, DG), lambda i: (0, 0))
    return ublk, vblk, vec, wsp, bsp


def gmlp_fwd(proj, ln_g, ws, bias, gg, mix):
    n = proj.shape[0]

    def body(up_ref, vp_ref, ln_ref, ws_ref, b_ref, gg_ref, mix_in, mix_ref):
        del mix_in
        u, _, _, _, mixed = _gmlp_forward(up_ref[...].astype(F32), vp_ref[...].astype(F32), ln_ref[...], ws_ref, b_ref[...])
        gout = u * mixed
        r = lax.rsqrt(jnp.mean(gout * gout, axis=-1, keepdims=True) + EPS)
        mix_ref[...] = (gout * r * gg_ref[...]).astype(BF16)

    ublk, vblk, vec, wsp, bsp = _gmlp_specs()
    return _call(body, name="gmlp_fwd", out_shape=_sds((n, D), BF16), grid=(n // GM_TM,),
                 in_specs=[ublk, vblk, vec, wsp, bsp, vec, pl.BlockSpec(memory_space=pl.ANY)],
                 out_specs=pl.BlockSpec((GM_TM, DG), lambda i: (i, DA // DG)), sem=("parallel",), aliases={6: 0})(
        proj, proj, ln_g, ws, bias, gg, mix)


def gmlp_bwd(proj, dmix, ln_g, ws, wst, bias, gg):
    n = proj.shape[0]

    def body(up_ref, vp_ref, dm_ref, ln_ref, ws_ref, wst_ref, b_ref, gg_ref, duv_ref, dln_ref, dws_ref, dbs_ref, dgg_ref,
             db_ref):
        @pl.when(pl.program_id(0) == 0)
        def _():
            dln_ref[...] = jnp.zeros_like(dln_ref)
            dws_ref[...] = jnp.zeros_like(dws_ref)
            db_ref[...] = jnp.zeros_like(db_ref)
            dgg_ref[...] = jnp.zeros_like(dgg_ref)

        up = up_ref[...].astype(F32)
        vp = vp_ref[...].astype(F32)
        ln_g = ln_ref[...]
        u, vhat, rs, vln, mixed = _gmlp_forward(up, vp, ln_g, ws_ref, b_ref[...])
        gout = u * mixed
        r = lax.rsqrt(jnp.mean(gout * gout, axis=-1, keepdims=True) + EPS)
        ghat = gout * r
        dm = dm_ref[...].astype(F32)
        dgg_ref[...] += jnp.sum(dm * ghat, axis=0, keepdims=True)
        dn = dm * gg_ref[...]
        dgout = r * (dn - ghat * jnp.mean(dn * ghat, axis=-1, keepdims=True))
        du = dgout * mixed
        dmixed = dgout * u
        dmb = dmixed.astype(BF16)
        rows = []
        for c in range(GM_TM // CH):
            rsl = slice(c * CH, (c + 1) * CH)
            db_ref[...] += dmixed[rsl, :]
            cols = []
            for g in range(NG):
                csl = slice(g * 128, (g + 1) * 128)
                dws_ref[g] += _dot_nt(dmb[rsl, csl], vln[rsl, csl])
                cols.append(_dot(wst_ref[g], dmb[rsl, csl]))
            rows.append(jnp.concatenate(cols, axis=1))
        dvln = jnp.concatenate(rows, axis=0)
        dln_ref[...] += jnp.sum(dvln * vhat, axis=0, keepdims=True)
        dvh = dvln * ln_g
        dv = rs * (dvh - jnp.mean(dvh, axis=-1, keepdims=True) - vhat * jnp.mean(dvh * vhat, axis=-1, keepdims=True))
        duv_ref[:, 0:DG] = (du * _gelu_grad(up)).astype(BF16)
        duv_ref[:, DG:] = (dv * _gelu_grad(vp)).astype(BF16)

        @pl.when(pl.program_id(0) == pl.num_programs(0) - 1)
        def _():
            for g in range(NG):
                dbs_ref[g:g + 1, :] = jnp.sum(jnp.transpose(db_ref[:, g * 128:(g + 1) * 128]), axis=0, keepdims=True)

    ublk, vblk, vec, wsp, bsp = _gmlp_specs()
    return _call(body, name="gmlp_bwd",
                 out_shape=(_sds((n, 2 * DG), BF16), _sds((1, DG), F32), _sds((NG, CH, CH), F32), _sds((NG, CH), F32),
                            _sds((1, DG), F32)),
                 grid=(n // GM_TM,),
                 in_specs=[ublk, vblk, pl.BlockSpec((GM_TM, DG), lambda i: (i, DA // DG)), vec, wsp, wsp, bsp, vec],
                 out_specs=(pl.BlockSpec((GM_TM, 2 * DG), lambda i: (i, 0)), vec, wsp,
                            pl.BlockSpec((NG, CH), lambda i: (0, 0)), vec),
                 scratch=[pltpu.VMEM((CH, DG), F32)], sem=("arbitrary",))(
        proj, proj, dmix, ln_g, ws, wst, bias, gg)


def _row(v):
    return v.reshape(1, -1)


def local_step(x, target, small, big):
    tables = step_tables()
    saved = []
    for l in range(NL):
        x_mid, sv = mixer_fwd(x, small, l, big[l], tables)
        x, sv2 = ffn_fwd(x_mid, small, l, big[l])
        saved.append({**sv, **sv2})
    loss8, dx, dxb, dfinal = loss_head(x, _row(small["final_g"]), target)
    gs = {k: [None] * NL for k in SMALL[:-1]}
    gbig = [None] * NL
    for l in reversed(range(NL)):
        dx, dxb, gs1, gb1 = ffn_bwd(dx, dxb, small, l, big[l], saved[l])
        dx, dxb, gs2, gb2 = mixer_bwd(dx, dxb, small, l, big[l], saved[l], tables)
        gbig[l] = {**gb1, **gb2}
        for k, g in {**gs1, **gs2}.items():
            gs[k][l] = g
    return loss8, dx, gs, dfinal, gbig


def step_tables():
    perms = [perm_matrix(d) for d in DILATIONS[1:]]
    return rope_tables(T) + (perms, [p.T for p in perms])


def mixer_fwd(x, small, l, w, tables, after=None):
    cos2, sin2, perms, perms_t = tables
    ws_b = small["w_spatial"][l].astype(BF16)
    bias = jnp.repeat(small["b_spatial"][l].T, 128, axis=1)
    h = rms_fwd(x, _row(small["norm1_g"][l]), after)
    proj = mm_cols(h, w["win"])
    qkvs = qkv_split(proj, cos2, sin2, perms)
    outs = [attn_fwd(qkv) for qkv in qkvs]
    a, lse, mix = attn_merge([o for o, _ in outs], [s for _, s in outs], _row(small["mix_norm_attn_g"][l]), perms_t)
    mix = gmlp_fwd(proj, _row(small["gmlp_ln_g"][l]), ws_b, bias, _row(small["mix_norm_gmlp_g"][l]), mix)
    x_mid = mm_rows_res(mix, w["wout"], x)
    return x_mid, dict(x=x, h=h, proj=proj, qkvs=qkvs, a=a, lse=lse, mix=mix, ws_b=ws_b, bias=bias)


def ffn_fwd(x_mid, small, l, w, after=None):
    h2 = rms_fwd(x_mid, _row(small["norm2_g"][l]), after)
    gate, up, ff = mm_gateup(h2, w["wg"], w["wu"])
    x_out = mm_rows_res(ff, w["wd"], x_mid)
    return x_out, dict(x_mid=x_mid, h2=h2, gate=gate, up=up, ff=ff)


def ffn_bwd(dx, dxb, small, l, w, sv, after=None):
    gs = {}
    dgate, dup = mm_dff(dxb, w["wd"], sv["gate"], sv["up"], tk=FF // NCHIP, after=after)
    g_wd = mm_tn(sv["ff"], dxb, D // 2, False)
    dh2 = mm_nt_cols([dgate, dup], [w["wg"], w["wu"]])
    g_wg = mm_tn(sv["h2"], dgate, FF // NCHIP, True)
    g_wu = mm_tn(sv["h2"], dup, FF // NCHIP, True)
    dx, dxb, gs["norm2_g"] = rms_bwd(dh2, sv["x_mid"], _row(small["norm2_g"][l]), dx)
    return dx, dxb, gs, dict(wg=g_wg, wu=g_wu, wd=g_wd.reshape(NCHIP, FF // NCHIP, D))


def mixer_bwd(dx, dxb, small, l, w, sv, tables, after=None):
    cos2, sin2, perms, perms_t = tables
    gs = {}
    dmix = mm_nt_rows(dxb, w["wout"], tk=D // 2, after=after)
    g_wout = mm_tn(sv["mix"], dxb, D // 2, False)
    das, a_res, lse_res, gs["mix_norm_attn_g"] = attn_norm_bwd(dmix, sv["a"], sv["lse"],
                                                                _row(small["mix_norm_attn_g"][l]), perms)
    a_all = [sv["a"].reshape((1,) + sv["a"].shape)] + list(a_res)
    lse_all = [sv["lse"].reshape((1,) + sv["lse"].shape)] + list(lse_res)
    parts = [attn_bwd(*operands) for operands in zip(sv["qkvs"], a_all, das, lse_all)]
    wst = jnp.swapaxes(small["w_spatial"][l], 1, 2).astype(BF16)
    duv, gs["gmlp_ln_g"], gs["w_spatial"], gs["b_spatial"], gs["mix_norm_gmlp_g"] = gmlp_bwd(
        sv["proj"], dmix, _row(small["gmlp_ln_g"][l]), sv["ws_b"], wst, sv["bias"], _row(small["mix_norm_gmlp_g"][l]))
    dproj = dproj_assemble([p[0] for p in parts], [p[1] for p in parts], [p[2] for p in parts], duv, cos2, sin2,
                           perms_t)
    dh = mm_nt_cols([dproj], [w["win"]])
    g_win = mm_tn(sv["h"], dproj, DIN // NCHIP, True)
    dx, dxb, gs["norm1_g"] = rms_bwd(dh, sv["x"], _row(small["norm1_g"][l]), dx)
    return dx, dxb, gs, dict(win=g_win, wout=g_wout.reshape(NCHIP, D // NCHIP, D))


KINDS = ("win", "wout", "wg", "wu", "wd")
GROUPS = (("win", "wout"), ("wg", "wu", "wd"))
ANY = pl.BlockSpec(memory_space=pl.ANY)


def _place():
    return lax.axis_index("x"), lax.axis_index("y"), lax.axis_index("c")


def _other_chips(x, y):
    return [(1 - x, y), (x, 1 - y), (1 - x, 1 - y)]


def _remote(src, dst, send_sem, recv_sem, to):
    return pltpu.make_async_remote_copy(src_ref=src, dst_ref=dst, send_sem=send_sem, recv_sem=recv_sem,
                                        device_id=to, device_id_type=MESH)


def _hbm_call(body, *, name, n_in, out_shape, scratch, in_place=False):
    return pl.pallas_call(body, name=name, out_shape=tuple(out_shape), in_specs=[ANY] * n_in,
                          out_specs=tuple(ANY for _ in out_shape), scratch_shapes=list(scratch),
                          input_output_aliases={k: k for k in range(n_in)} if in_place else {},
                          compiler_params=pltpu.CompilerParams(vmem_limit_bytes=VMEM_LIMIT))


def place_vector():
    x, y, c = _place()
    return jnp.stack([c, 2 * x + y] + [2 * cx + cy for cx, cy in _other_chips(x, y)]).astype(jnp.int32)


PUSH_ROWS = 512
PUSH_SLOTS = 3


def _push_tiles(src, dst, buf, load_sem, send_sem, recv_sem, to, src_at, dst_at, nt):
    tr = buf.shape[1]
    n = len(src_at) * nt

    def pick(vals, seg):
        out = vals[0]
        for q in range(1, len(vals)):
            out = jnp.where(seg == q, vals[q], out)
        return out

    def tile(ref, at, t):
        seg = t // nt
        row = pl.multiple_of(pick([a[1] for a in at], seg) + (t - seg * nt) * tr, 16)
        return ref.at[pick([a[0] for a in at], seg), pl.ds(row, tr), :]

    def load(t):
        slot = t % PUSH_SLOTS
        return pltpu.make_async_copy(tile(src, src_at, t), buf.at[slot], load_sem.at[slot])

    def send(t):
        slot = t % PUSH_SLOTS
        return _remote(buf.at[slot], tile(dst, dst_at, t), send_sem.at[slot], recv_sem, to)

    load(0).start()

    def step(t, carry):
        load(t).wait()
        send(t).start()

        @pl.when(t >= PUSH_SLOTS - 1)
        def _():
            send(t - (PUSH_SLOTS - 1)).wait_send()

        @pl.when(t + 1 < n)
        def _():
            load(t + 1).start()

        return carry

    lax.fori_loop(0, n, step, 0)
    for t in range(max(n - (PUSH_SLOTS - 1), 0), n):
        send(t).wait_send()


def _await_tiles(dst, send_sem, recv_sem, to, nseg, rows):
    whole = dst.at[pl.ds(0, nseg), pl.ds(0, rows), :]
    _remote(whole, whole, send_sem.at[0], recv_sem, to).wait_recv()


def _push_rows(seg_rows):
    return _row_tile(seg_rows, PUSH_ROWS)


def _push_scratch(arrs, seg_rows):
    return ([pltpu.VMEM((PUSH_SLOTS, _push_rows(r), a.shape[-1]), a.dtype) for a, r in zip(arrs, seg_rows)]
            + [pltpu.SemaphoreType.DMA((len(arrs), PUSH_SLOTS)), pltpu.SemaphoreType.DMA((len(arrs), PUSH_SLOTS)),
               pltpu.SemaphoreType.DMA((len(arrs),))])


def cast_into(w, l, pos, after=None):
    _, r, cols = w.shape
    tr = _row_tile(r)
    extra, extra_specs = _behind(after)

    def body(pos_ref, w_ref, *rest):
        del pos_ref
        rest[-1][...] = w_ref[...].astype(BF16)

    return _call(body, name="cast_into", out_shape=_sds((NCHIP, r, cols), BF16), grid=(r // tr,),
                 in_specs=[pl.BlockSpec((None, tr, cols), lambda i, pos: (l, i, 0))] + extra_specs,
                 out_specs=pl.BlockSpec((None, tr, cols), lambda i, pos: (pos[1], i, 0)),
                 sem=("parallel",), prefetch=1)(pos, w, *extra)


HBM = pl.BlockSpec(memory_space=pltpu.HBM)
SEM = pl.BlockSpec(memory_space=pltpu.SEMAPHORE)
EFFECT = pltpu.SideEffectType.DATAFLOW_SIDE_EFFECTING


def _in_hbm(a):
    return pltpu.with_memory_space_constraint(a, pltpu.HBM)


def _gather_copies(bufs, send, recv):
    x, y, c = _place()
    chips = _other_chips(x, y)

    def half(k, chip):
        hr = bufs[k].shape[1] // 2
        return bufs[k].at[chip, pl.ds(c * hr, hr), :]

    out, back = [], []
    for k in range(len(bufs)):
        for j, (cx, cy) in enumerate(chips):
            i = 3 * k + j
            out.append(_remote(half(k, 2 * x + y), half(k, 2 * x + y), send.at[i], recv.at[i], (cx, cy, c)))
            back.append(_remote(half(k, 2 * x + y), half(k, 2 * cx + cy), send.at[i], recv.at[i], (cx, cy, c)))
    return out, back


def gather_start(fulls):
    nk = len(fulls)

    def body(*refs):
        send, recv = refs[nk], refs[nk + 1]
        bufs, token = refs[nk + 2:2 * nk + 2], refs[2 * nk + 2]
        for cp in _gather_copies(bufs, send, recv)[0]:
            cp.start()
        token[...] = jnp.zeros_like(token)

    res = pl.pallas_call(
        body, name="gather_start",
        out_shape=(pltpu.SemaphoreType.DMA((3 * nk,)), pltpu.SemaphoreType.DMA((3 * nk,)),
                   *[pltpu.HBM(f.shape, f.dtype) for f in fulls], _sds((8, LANES), F32)),
        in_specs=[HBM] * nk, out_specs=(SEM, SEM, *[HBM] * nk, pl.BlockSpec(memory_space=pltpu.VMEM)),
        input_output_aliases={k: 2 + k for k in range(nk)},
        compiler_params=pltpu.CompilerParams(has_side_effects=EFFECT))(*[_in_hbm(f) for f in fulls])
    return res[0], res[1], list(res[2:2 + nk]), res[2 + nk]


def gather_wait(send, recv, fulls, after):
    nk = len(fulls)

    def body(*refs):
        bufs, send_ref, recv_ref = refs[:nk], refs[nk], refs[nk + 1]
        for cp in _gather_copies(bufs, send_ref, recv_ref)[1]:
            cp.wait_send()
            cp.wait_recv()

    return list(pl.pallas_call(
        body, name="gather_wait", out_shape=tuple(pltpu.HBM(f.shape, f.dtype) for f in fulls),
        in_specs=[HBM] * nk + [SEM, SEM, ANY], out_specs=tuple([HBM] * nk),
        input_output_aliases={k: k for k in range(nk)},
        compiler_params=pltpu.CompilerParams(has_side_effects=EFFECT))(*fulls, send, recv, after))


def pair_forward(fulls):
    nk = len(fulls)

    def body(*refs):
        bufs = refs[nk:2 * nk]
        stage = refs[2 * nk:3 * nk]
        load_sem, send_sem, recv_sem = refs[3 * nk:]
        x, y, c = _place()
        sib = (x, y, 1 - c)
        chips = [2 * cx + cy for cx, cy in _other_chips(x, y)]
        for k in range(nk):
            hr = bufs[k].shape[1] // 2
            at = [(chip, c * hr) for chip in chips]
            _push_tiles(bufs[k], bufs[k], stage[k], load_sem.at[k], send_sem.at[k], recv_sem.at[k], sib, at, at,
                        hr // _push_rows(hr))
        for k in range(nk):
            _await_tiles(bufs[k], send_sem.at[k], recv_sem.at[k], sib, 3, bufs[k].shape[1] // 2)

    return _hbm_call(body, name="pair_forward", n_in=nk, out_shape=[_sds(f.shape, f.dtype) for f in fulls], in_place=True,
                     scratch=_push_scratch(fulls, [f.shape[1] // 2 for f in fulls]))(*fulls)


def rs_pair_send(gs):
    nk = len(gs)

    def body(*refs):
        ins, got = refs[:nk], refs[nk:2 * nk]
        stage = refs[2 * nk:3 * nk]
        load_sem, send_sem, recv_sem = refs[3 * nk:]
        x, y, c = _place()
        sib = (x, y, 1 - c)
        for k in range(nk):
            hr = ins[k].shape[1] // 2
            _push_tiles(ins[k], got[k], stage[k], load_sem.at[k], send_sem.at[k], recv_sem.at[k], sib,
                        [(s, (1 - c) * hr) for s in range(NCHIP)], [(s, 0) for s in range(NCHIP)], hr // _push_rows(hr))
        for k in range(nk):
            _await_tiles(got[k], send_sem.at[k], recv_sem.at[k], sib, NCHIP, ins[k].shape[1] // 2)

    half = [_sds((g.shape[0], g.shape[1] // 2, g.shape[2]), g.dtype) for g in gs]
    return _hbm_call(body, name="rs_pair_send", n_in=nk, out_shape=half,
                     scratch=_push_scratch(gs, [g.shape[1] // 2 for g in gs]))(*gs)


def _chip_copies(ps, lands, send, recv):
    x, y, c = _place()
    return [_remote(ps[k].at[2 * cx + cy], lands[k].at[j], send.at[3 * k + j], recv.at[3 * k + j], (cx, cy, c))
            for k in range(len(ps)) for j, (cx, cy) in enumerate(_other_chips(x, y))]


def chip_exchange_start(ps):
    nk = len(ps)
    lands = [lax.empty((3,) + p.shape[1:], p.dtype) for p in ps]

    def body(*refs):
        send, recv = refs[2 * nk], refs[2 * nk + 1]
        srcs, dsts, token = refs[2 * nk + 2:3 * nk + 2], refs[3 * nk + 2:4 * nk + 2], refs[4 * nk + 2]
        for cp in _chip_copies(srcs, dsts, send, recv):
            cp.start()
        token[...] = jnp.zeros_like(token)

    res = pl.pallas_call(
        body, name="chip_exchange_start",
        out_shape=(pltpu.SemaphoreType.DMA((3 * nk,)), pltpu.SemaphoreType.DMA((3 * nk,)),
                   *[pltpu.HBM(a.shape, a.dtype) for a in ps + lands], _sds((8, LANES), F32)),
        in_specs=[HBM] * (2 * nk), out_specs=(SEM, SEM, *[HBM] * (2 * nk), pl.BlockSpec(memory_space=pltpu.VMEM)),
        input_output_aliases={k: 2 + k for k in range(2 * nk)},
        compiler_params=pltpu.CompilerParams(has_side_effects=EFFECT))(*[_in_hbm(a) for a in ps + lands])
    return res[0], res[1], list(res[2:2 + nk]), list(res[2 + nk:2 + 2 * nk]), res[2 + 2 * nk]


def chip_exchange_wait(send, recv, ps, lands, after):
    nk = len(ps)

    def body(*refs):
        srcs, dsts, send_ref, recv_ref = refs[:nk], refs[nk:2 * nk], refs[2 * nk], refs[2 * nk + 1]
        for cp in _chip_copies(srcs, dsts, send_ref, recv_ref):
            cp.wait_send()
            cp.wait_recv()

    res = pl.pallas_call(
        body, name="chip_exchange_wait", out_shape=tuple(pltpu.HBM(a.shape, a.dtype) for a in ps + lands),
        in_specs=[HBM] * (2 * nk) + [SEM, SEM, ANY], out_specs=tuple([HBM] * (2 * nk)),
        input_output_aliases={k: k for k in range(2 * nk)},
        compiler_params=pltpu.CompilerParams(has_side_effects=EFFECT))(*ps, *lands, send, recv, after)
    return list(res[:nk]), list(res[nk:])


def rs_pair_share(fulls):
    nk = len(fulls)

    def body(*refs):
        bufs = refs[nk:2 * nk]
        stage = refs[2 * nk:3 * nk]
        load_sem, send_sem, recv_sem = refs[3 * nk:]
        x, y, c = _place()
        sib = (x, y, 1 - c)
        for k in range(nk):
            hr = bufs[k].shape[1]
            _push_tiles(bufs[k], bufs[k], stage[k], load_sem.at[k], send_sem.at[k], recv_sem.at[k], sib,
                        [(c, 0)], [(c, 0)], hr // _push_rows(hr))
        for k in range(nk):
            _await_tiles(bufs[k], send_sem.at[k], recv_sem.at[k], sib, 1, bufs[k].shape[1])

    return _hbm_call(body, name="rs_pair_share", n_in=nk, out_shape=[_sds(f.shape, f.dtype) for f in fulls], in_place=True,
                     scratch=_push_scratch(fulls, [f.shape[1] for f in fulls]))(*fulls)


N_DEV = 8


def allreduce_small(buf):
    rows = buf.shape[0]
    rp = rows // N_DEV

    def body(in_ref, out_ref, land_ref, send, recv):
        x, y, c = _place()
        mine = pl.ds(pl.multiple_of((4 * x + 2 * y + c) * rp, 8), rp)
        peers = [(x ^ fx, y ^ fy, c ^ fc) for fx in (0, 1) for fy in (0, 1) for fc in (0, 1)][1:]
        block = [pl.ds(pl.multiple_of((4 * px + 2 * py + pc) * rp, 8), rp) for px, py, pc in peers]
        scatter = [_remote(in_ref.at[block[r], :], land_ref.at[r], send.at[r], recv.at[r], peers[r]) for r in range(7)]
        for cp in scatter:
            cp.start()
        for cp in scatter:
            cp.wait()
        acc = in_ref[mine, :]
        for r in range(7):
            acc = acc + land_ref[r]
        out_ref[mine, :] = acc
        spread = [_remote(out_ref.at[mine, :], out_ref.at[mine, :], send.at[7 + r], recv.at[7 + r], peers[r]) for r in range(7)]
        for cp in spread:
            cp.start()
        for r in range(7):
            _remote(out_ref.at[block[r], :], out_ref.at[block[r], :], send.at[7 + r], recv.at[7 + r], peers[r]).wait_recv()
        for cp in spread:
            cp.wait_send()

    vm = pl.BlockSpec(memory_space=pltpu.VMEM)
    return pl.pallas_call(body, name="allreduce_small", out_shape=_sds(buf.shape, F32), in_specs=[vm], out_specs=vm,
                          scratch_shapes=[pltpu.VMEM((7, rp, LANES), F32), pltpu.SemaphoreType.DMA((14,)),
                                          pltpu.SemaphoreType.DMA((14,))])(buf)


def _row_tile(rows, cap=512):
    return max(t for t in range(16, cap + 1, 16) if rows % t == 0)


def add_halves(g, got, pos):
    s, r, cols = g.shape
    hr = r // 2
    tr = _row_tile(hr)
    nt = hr // tr

    def body(pos_ref, a_ref, b_ref, o_ref):
        del pos_ref
        o_ref[...] = (a_ref[...].astype(F32) + b_ref[...].astype(F32)).astype(BF16)

    blk = pl.BlockSpec((None, tr, cols), lambda q, i, pos: (q, i, 0))
    return _call(body, name="add_halves", out_shape=_sds((s, hr, cols), BF16), grid=(s, nt),
                 in_specs=[pl.BlockSpec((None, tr, cols), lambda q, i, pos: (q, pos[0] * nt + i, 0)), blk],
                 out_specs=blk, sem=("parallel", "parallel"), prefetch=1)(pos, g, got)


def add_slots(p, got, pos):
    _, hr, cols = p.shape
    tr = _row_tile(hr)

    def body(pos_ref, o_ref, g0, g1, g2, out_ref):
        del pos_ref
        out_ref[...] = ((o_ref[...].astype(F32) + g0[...].astype(F32)) + g1[...].astype(F32)) + g2[...].astype(F32)

    slots = [pl.BlockSpec((None, tr, cols), functools.partial(lambda j, i, pos: (j, i, 0), j)) for j in range(3)]
    return _call(body, name="add_slots", out_shape=_sds((2, hr, cols), F32), grid=(hr // tr,),
                 in_specs=[pl.BlockSpec((None, tr, cols), lambda i, pos: (pos[1], i, 0))] + slots,
                 out_specs=pl.BlockSpec((None, tr, cols), lambda i, pos: (pos[0], i, 0)),
                 sem=("parallel",), prefetch=1)(pos, p, got, got, got)


def _adamw_math(w, g, m, v):
    nm = B1 * m + (1.0 - B1) * g
    nv = B2 * v + (1.0 - B2) * (g * g)
    m_hat = nm / (1.0 - B1 ** STEP)
    v_hat = nv / (1.0 - B2 ** STEP)
    return -LR * (m_hat / (jnp.sqrt(v_hat) + AEPS) + WD * w), nm, nv


def adamw_layer(w, g, m, v, l, prev):
    nl, r, cols = w.shape
    tr = _row_tile(r, 256)

    def body(w_ref, g_ref, m_ref, v_ref, *rest):
        go_ref, d_ref, nm_ref, nv_ref = rest[-4:]
        gv = g_ref[...]
        d, nm, nv = _adamw_math(w_ref[...], gv, m_ref[...], v_ref[...])
        go_ref[...] = gv
        d_ref[...] = d
        nm_ref[...] = nm
        nv_ref[...] = nv

    lay = pl.BlockSpec((None, tr, cols), lambda i: (l, i, 0))
    o = _sds((nl, r, cols), F32)
    extra = [] if prev is None else list(prev)
    return _call(body, name="adamw_layer", out_shape=(o, o, o, o), grid=(r // tr,),
                 in_specs=[lay, pl.BlockSpec((tr, cols), lambda i: (i, 0)), lay, lay] + [ANY] * len(extra),
                 out_specs=(lay, lay, lay, lay), sem=("parallel",),
                 aliases={4 + j: j for j in range(len(extra))})(w, g, m, v, *extra)


def adamw(w, g, m, v):
    shape = w.shape
    cols = shape[-1]
    rows = w.size // cols
    tr = _row_tile(rows, 256)

    def body(w_ref, g_ref, m_ref, v_ref, d_ref, nm_ref, nv_ref):
        d_ref[...], nm_ref[...], nv_ref[...] = _adamw_math(w_ref[...], g_ref[...], m_ref[...], v_ref[...])

    blk = pl.BlockSpec((tr, cols), lambda i: (i, 0))
    o = _sds((rows, cols), F32)
    outs = _call(body, name="adamw", out_shape=(o, o, o), grid=(rows // tr,), in_specs=[blk] * 4, out_specs=(blk, blk, blk),
                 sem=("parallel",))(*[t.reshape(rows, cols) for t in (w, g, m, v)])
    return [t.reshape(shape) for t in outs]


SMALL = ("norm1_g", "gmlp_ln_g", "w_spatial", "b_spatial", "mix_norm_attn_g", "mix_norm_gmlp_g", "norm2_g", "final_g")
WEIGHTS = ("norm1_g", "w_in", "gmlp_ln_g", "w_spatial", "b_spatial", "mix_norm_attn_g", "mix_norm_gmlp_g", "w_out",
           "norm2_g", "w_gate", "w_up", "w_down", "final_g")
BIG = dict(win="w_in", wout="w_out", wg="w_gate", wu="w_up", wd="w_down")


def _pack(parts):
    flat = jnp.concatenate([parts[n].reshape(-1) for n in SMALL])
    rows = -(-flat.shape[0] // (LANES * 8 * N_DEV)) * 8 * N_DEV
    return jnp.pad(flat, (0, rows * LANES - flat.shape[0])).reshape(rows, LANES)


def _unpack(buf, like):
    flat = buf.reshape(-1)
    out, at = {}, 0
    for n in SMALL:
        out[n] = flat[at:at + like[n].size].reshape(like[n].shape)
        at += like[n].size
    return out


def kernel(x, norm1_g, w_in, gmlp_ln_g, w_spatial, b_spatial, mix_norm_attn_g, mix_norm_gmlp_g, w_out, norm2_g, w_gate, w_up, w_down, final_g, loss_target, m_norm1_g, m_w_in, m_gmlp_ln_g, m_w_spatial, m_b_spatial, m_mix_norm_attn_g, m_mix_norm_gmlp_g, m_w_out, m_norm2_g, m_w_gate, m_w_up, m_w_down, m_final_g, v_norm1_g, v_w_in, v_gmlp_ln_g, v_w_spatial, v_b_spatial, v_mix_norm_attn_g, v_mix_norm_gmlp_g, v_w_out, v_norm2_g, v_w_gate, v_w_up, v_w_down, v_final_g):
    w = dict(norm1_g=norm1_g, w_in=w_in, gmlp_ln_g=gmlp_ln_g, w_spatial=w_spatial, b_spatial=b_spatial,
             mix_norm_attn_g=mix_norm_attn_g, mix_norm_gmlp_g=mix_norm_gmlp_g, w_out=w_out, norm2_g=norm2_g,
             w_gate=w_gate, w_up=w_up, w_down=w_down, final_g=final_g)
    m = dict(norm1_g=m_norm1_g, w_in=m_w_in, gmlp_ln_g=m_gmlp_ln_g, w_spatial=m_w_spatial, b_spatial=m_b_spatial,
             mix_norm_attn_g=m_mix_norm_attn_g, mix_norm_gmlp_g=m_mix_norm_gmlp_g, w_out=m_w_out, norm2_g=m_norm2_g,
             w_gate=m_w_gate, w_up=m_w_up, w_down=m_w_down, final_g=m_final_g)
    v = dict(norm1_g=v_norm1_g, w_in=v_w_in, gmlp_ln_g=v_gmlp_ln_g, w_spatial=v_w_spatial, b_spatial=v_b_spatial,
             mix_norm_attn_g=v_mix_norm_attn_g, mix_norm_gmlp_g=v_mix_norm_gmlp_g, w_out=v_w_out, norm2_g=v_norm2_g,
             w_gate=v_w_gate, w_up=v_w_up, w_down=v_w_down, final_g=v_final_g)

    pos = place_vector()
    small = {n: w[n] for n in SMALL}
    tables = step_tables()

    def start_gathers(l, after):
        flights = []
        for kinds in GROUPS:
            flights.append(gather_start([cast_into(w[BIG[k]], l, pos, after) for k in kinds]))
            after = flights[-1][3]
        return flights, after

    def arrive(flight, kinds, after):
        fulls = pair_forward(gather_wait(flight[0], flight[1], flight[2], after))
        big = dict(zip(kinds, fulls))
        if "wout" in big:
            big["wout"] = big["wout"].reshape(D, D)
        if "wd" in big:
            big["wd"] = big["wd"].reshape(FF, D)
        return big

    act = x[0]
    flights, token = start_gathers(0, None)
    big, saved = [], []
    for l in range(NL):
        mine, after_mixer = flights, token
        if l + 1 < NL:
            flights, token = start_gathers(l + 1, act if l else token)
            after_mixer = token
        wl = arrive(mine[0], GROUPS[0], act if l else mine[0][3])
        x_mid, sv = mixer_fwd(act, small, l, wl, tables, after_mixer)
        wl.update(arrive(mine[1], GROUPS[1], x_mid))
        act, sv2 = ffn_fwd(x_mid, small, l, wl)
        big.append(wl)
        saved.append({**sv, **sv2})
    loss8, dx, dxb, dfinal = loss_head(act, _row(small["final_g"]), loss_target[0])
    loss = lax.psum(loss8[0, 0], ("x", "y", "c"))

    grads, delta, new_m, new_v = {}, {}, {}, {}
    stacked = {k: None for k in KINDS}
    gs = {k: [None] * NL for k in SMALL[:-1]}

    def send_off(gbig, kinds, l):
        partial_grads = [gbig[k] for k in kinds]
        pair = [add_halves(g, got, pos) for g, got in zip(partial_grads, rs_pair_send(partial_grads))]
        return chip_exchange_start(pair) + (kinds, l)

    def finish(flight, after):
        pair, lands = chip_exchange_wait(*flight[:4], after)
        halves = [add_slots(p, got, pos) for p, got in zip(pair, lands)]
        for k, full in zip(flight[5], rs_pair_share(halves)):
            n = BIG[k]
            stacked[k] = adamw_layer(w[n], full.reshape(w[n].shape[1:]), m[n], v[n], flight[6], stacked[k])

    pending, token = [], None
    for l in reversed(range(NL)):
        dx, dxb, gs1, gbig = ffn_bwd(dx, dxb, small, l, big[l], saved[l], token)
        for flight in pending:
            finish(flight, dx)
        ffn_flight = send_off(gbig, GROUPS[1], l)
        dx, dxb, gs2, gbig = mixer_bwd(dx, dxb, small, l, big[l], saved[l], tables, ffn_flight[4])
        mixer_flight = send_off(gbig, GROUPS[0], l)
        pending, token = [ffn_flight, mixer_flight], mixer_flight[4]
        for k, g in {**gs1, **gs2}.items():
            gs[k][l] = g
    for flight in pending:
        finish(flight, token)
    for k in KINDS:
        grads[BIG[k]], delta[BIG[k]], new_m[BIG[k]], new_v[BIG[k]] = stacked[k]
    part = {n: jnp.stack(gs[n]).reshape(w[n].shape) for n in SMALL[:-1]}
    part["final_g"] = dfinal.reshape(w["final_g"].shape)
    grads.update(_unpack(allreduce_small(_pack(part)), part))

    sm = adamw(_pack({n: w[n] for n in SMALL}), _pack({n: grads[n] for n in SMALL}), _pack({n: m[n] for n in SMALL}),
               _pack({n: v[n] for n in SMALL}))
    for res, packed in zip((delta, new_m, new_v), sm):
        res.update(_unpack(packed, part))

    return (loss, dx.reshape(x.shape), *[grads[n] for n in WEIGHTS], *[delta[n] for n in WEIGHTS],
            *[new_m[n] for n in WEIGHTS], *[new_v[n] for n in WEIGHTS])
```

```python
import functools

import jax
import jax.numpy as jnp
from jax import lax
from jax.experimental import pallas as pl
from jax.experimental.pallas import tpu as pltpu

F32 = jnp.float32
BF16 = jnp.bfloat16

D = 2048
T = 4096
NL = 4
HD = 128
DA = D // 2
DG = D - DA
NH = DA // HD
NG = DG // 128
CH = 128
DIN = 3 * DA + 2 * DG
FF = 5632
DILATIONS = (1, 4, 16)
NSIDE = 64
ROPE_THETA = 10000.0
EPS = 1e-6
NEG = -1e30
SCALE = HD ** -0.5
NCHIP = 4

LR, B1, B2, AEPS, WD, STEP = 0.001, 0.9, 0.999, 1e-08, 0.01, 10

LANES = 128
VMEM_LIMIT = 56 * 1024 * 1024
MESH = pl.DeviceIdType.MESH


def _call(body, *, name, out_shape, grid=(), in_specs=None, out_specs=None, scratch=(), sem=None, aliases=None,
          prefetch=0):
    params = dict(vmem_limit_bytes=VMEM_LIMIT)
    if sem is not None:
        params["dimension_semantics"] = sem
    if prefetch:
        spec = pltpu.PrefetchScalarGridSpec(num_scalar_prefetch=prefetch, grid=grid, in_specs=in_specs,
                                            out_specs=out_specs, scratch_shapes=list(scratch))
        return pl.pallas_call(body, name=name, out_shape=out_shape, grid_spec=spec,
                              input_output_aliases=aliases or {}, compiler_params=pltpu.CompilerParams(**params))
    kw = {}
    if grid:
        kw["grid"] = grid
    return pl.pallas_call(
        body, name=name, out_shape=out_shape, in_specs=in_specs, out_specs=out_specs,
        scratch_shapes=list(scratch), input_output_aliases=aliases or {},
        compiler_params=pltpu.CompilerParams(**params), **kw)


def _sds(shape, dtype):
    return jax.ShapeDtypeStruct(tuple(shape), dtype)


def _dot(a, b):
    return jnp.dot(a, b, preferred_element_type=F32)


def _dot_nt(a, b):
    return lax.dot_general(a, b, (((1,), (1,)), ((), ())), preferred_element_type=F32)


def _dot_tn(a, b):
    return lax.dot_general(a, b, (((0,), (0,)), ((), ())), preferred_element_type=F32)


def _sigmoid(x):
    return 0.5 * jnp.tanh(0.5 * x) + 0.5


def _behind(after):
    return ([], []) if after is None else ([after], [pl.BlockSpec(memory_space=pl.ANY)])


def rms_fwd(x, g, after=None):
    n, d = x.shape
    tm = 256
    extra, extra_specs = _behind(after)

    def body(x_ref, g_ref, *rest):
        h_ref = rest[-1]
        xv = x_ref[...]
        r = lax.rsqrt(jnp.mean(xv * xv, axis=-1, keepdims=True) + EPS)
        h_ref[...] = (xv * r * g_ref[...]).astype(BF16)

    return _call(body, name="rms_fwd", out_shape=_sds((n, d), BF16), grid=(n // tm,),
                 in_specs=[pl.BlockSpec((tm, d), lambda i: (i, 0)), pl.BlockSpec((1, d), lambda i: (0, 0))] + extra_specs,
                 out_specs=pl.BlockSpec((tm, d), lambda i: (i, 0)), sem=("parallel",))(x, g, *extra)


def rms_bwd(dh, x, g, dres):
    n, d = x.shape
    tm = 256

    def body(dh_ref, x_ref, g_ref, dres_ref, dx_ref, dxb_ref, dg_ref):
        @pl.when(pl.program_id(0) == 0)
        def _():
            dg_ref[...] = jnp.zeros_like(dg_ref)

        xv = x_ref[...]
        r = lax.rsqrt(jnp.mean(xv * xv, axis=-1, keepdims=True) + EPS)
        xhat = xv * r
        dhv = dh_ref[...].astype(F32)
        dg_ref[...] += jnp.sum(dhv * xhat, axis=0, keepdims=True)
        dxn = dhv * g_ref[...]
        dx = dres_ref[...] + r * (dxn - xhat * jnp.mean(dxn * xhat, axis=-1, keepdims=True))
        dx_ref[...] = dx
        dxb_ref[...] = dx.astype(BF16)

    row = pl.BlockSpec((tm, d), lambda i: (i, 0))
    vec = pl.BlockSpec((1, d), lambda i: (0, 0))
    return _call(body, name="rms_bwd", out_shape=(_sds((n, d), F32), _sds((n, d), BF16), _sds((1, d), F32)),
                 grid=(n // tm,), in_specs=[row, row, vec, row], out_specs=(row, row, vec), sem=("arbitrary",))(dh, x, g, dres)


def loss_head(x, g, target):
    n, d = x.shape
    tm = 256

    def body(x_ref, g_ref, t_ref, loss_ref, dx_ref, dxb_ref, dg_ref):
        @pl.when(pl.program_id(0) == 0)
        def _():
            dg_ref[...] = jnp.zeros_like(dg_ref)
            loss_ref[...] = jnp.zeros_like(loss_ref)

        xv = x_ref[...]
        r = lax.rsqrt(jnp.mean(xv * xv, axis=-1, keepdims=True) + EPS)
        xhat = xv * r
        gv = g_ref[...]
        err = xhat * gv - t_ref[...]
        loss_ref[...] += 0.5 * jnp.sum(jnp.mean(err * err, axis=-1, keepdims=True))
        dy = err * (1.0 / d)
        dg_ref[...] += jnp.sum(dy * xhat, axis=0, keepdims=True)
        dxn = dy * gv
        dx = r * (dxn - xhat * jnp.mean(dxn * xhat, axis=-1, keepdims=True))
        dx_ref[...] = dx
        dxb_ref[...] = dx.astype(BF16)

    row = pl.BlockSpec((tm, d), lambda i: (i, 0))
    vec = pl.BlockSpec((1, d), lambda i: (0, 0))
    return _call(body, name="loss_head",
                 out_shape=(_sds((8, LANES), F32), _sds((n, d), F32), _sds((n, d), BF16), _sds((1, d), F32)),
                 grid=(n // tm,), in_specs=[row, vec, row],
                 out_specs=(pl.BlockSpec((8, LANES), lambda i: (0, 0)), row, row, vec), sem=("arbitrary",))(x, g, target)


def mm_cols(a, wc, tm=512):
    n, k = a.shape
    s, _, nq = wc.shape

    def body(a_ref, w_ref, o_ref):
        o_ref[...] = _dot(a_ref[...], w_ref[...]).astype(BF16)

    return _call(body, name="mm_cols", out_shape=_sds((n, s * nq), BF16), grid=(s, n // tm),
                 in_specs=[pl.BlockSpec((tm, k), lambda j, i: (i, 0)), pl.BlockSpec((None, k, nq), lambda j, i: (j, 0, 0))],
                 out_specs=pl.BlockSpec((tm, nq), lambda j, i: (i, j)), sem=("parallel", "parallel"))(a, wc)


def mm_gateup(h, wg, wu, tm=512):
    n, k = h.shape
    s, _, nq = wg.shape

    def body(h_ref, wg_ref, wu_ref, gate_ref, up_ref, ff_ref):
        hv = h_ref[...]
        gate = _dot(hv, wg_ref[...])
        up = _dot(hv, wu_ref[...])
        gate_ref[...] = gate.astype(BF16)
        up_ref[...] = up.astype(BF16)
        ff_ref[...] = (gate * _sigmoid(gate) * up).astype(BF16)

    wspec = pl.BlockSpec((None, k, nq), lambda j, i: (j, 0, 0))
    ospec = pl.BlockSpec((tm, nq), lambda j, i: (i, j))
    o = _sds((n, s * nq), BF16)
    return _call(body, name="mm_gateup", out_shape=(o, o, o), grid=(s, n // tm),
                 in_specs=[pl.BlockSpec((tm, k), lambda j, i: (i, 0)), wspec, wspec],
                 out_specs=(ospec, ospec, ospec), sem=("parallel", "parallel"))(h, wg, wu)


def mm_rows_res(a, wr, res, tm=512, tn=1024):
    n, k = a.shape
    _, nn = wr.shape

    def body(a_ref, w_ref, r_ref, o_ref):
        o_ref[...] = r_ref[...] + _dot(a_ref[...], w_ref[...])

    return _call(body, name="mm_rows_res", out_shape=_sds((n, nn), F32), grid=(nn // tn, n // tm),
                 in_specs=[pl.BlockSpec((tm, k), lambda j, i: (i, 0)), pl.BlockSpec((k, tn), lambda j, i: (0, j)),
                           pl.BlockSpec((tm, tn), lambda j, i: (i, j))],
                 out_specs=pl.BlockSpec((tm, tn), lambda j, i: (i, j)), sem=("parallel", "parallel"))(a, wr, res)


def mm_nt_rows(dy, wr, tk, tm=512, after=None):
    n, nn = dy.shape
    k, _ = wr.shape
    extra, extra_specs = _behind(after)

    def body(dy_ref, w_ref, *rest):
        rest[-1][...] = _dot_nt(dy_ref[...], w_ref[...]).astype(BF16)

    return _call(body, name="mm_nt_rows", out_shape=_sds((n, k), BF16), grid=(k // tk, n // tm),
                 in_specs=[pl.BlockSpec((tm, nn), lambda j, i: (i, 0)), pl.BlockSpec((tk, nn), lambda j, i: (j, 0))]
                 + extra_specs,
                 out_specs=pl.BlockSpec((tm, tk), lambda j, i: (i, j)), sem=("parallel", "parallel"))(dy, wr, *extra)


def mm_dff(dy, wd, gate, up, tk, tm=512, after=None):
    n, nn = dy.shape
    k, _ = wd.shape
    extra, extra_specs = _behind(after)

    def body(dy_ref, w_ref, gate_ref, up_ref, *rest):
        dgate_ref, dup_ref = rest[-2:]
        dff = _dot_nt(dy_ref[...], w_ref[...])
        gate = gate_ref[...].astype(F32)
        up = up_ref[...].astype(F32)
        sig = _sigmoid(gate)
        dgate_ref[...] = (dff * up * (sig * (1.0 + gate * (1.0 - sig)))).astype(BF16)
        dup_ref[...] = (dff * (gate * sig)).astype(BF16)

    tile = pl.BlockSpec((tm, tk), lambda j, i: (i, j))
    o = _sds((n, k), BF16)
    return _call(body, name="mm_dff", out_shape=(o, o), grid=(k // tk, n // tm),
                 in_specs=[pl.BlockSpec((tm, nn), lambda j, i: (i, 0)), pl.BlockSpec((tk, nn), lambda j, i: (j, 0)), tile, tile]
                 + extra_specs,
                 out_specs=(tile, tile), sem=("parallel", "parallel"))(dy, wd, gate, up, *extra)


def mm_nt_cols(das, wcs, tm=512):
    n = das[0].shape[0]
    s, k, nq = wcs[0].shape
    npair = len(das)

    def body(*refs):
        da_refs, w_refs, o_ref, acc = refs[:npair], refs[npair:2 * npair], refs[2 * npair], refs[2 * npair + 1]
        ss = pl.program_id(1)

        @pl.when(ss == 0)
        def _():
            acc[...] = jnp.zeros_like(acc)

        for da_ref, w_ref in zip(da_refs, w_refs):
            acc[...] += _dot_nt(da_ref[...], w_ref[...])

        @pl.when(ss == s - 1)
        def _():
            o_ref[...] = acc[...].astype(BF16)

    return _call(body, name="mm_nt_cols%d" % npair, out_shape=_sds((n, k), BF16), grid=(n // tm, s),
                 in_specs=[pl.BlockSpec((tm, nq), lambda i, ss: (i, ss))] * npair
                 + [pl.BlockSpec((None, k, nq), lambda i, ss: (ss, 0, 0))] * npair,
                 out_specs=pl.BlockSpec((tm, k), lambda i, ss: (i, 0)),
                 scratch=[pltpu.VMEM((tm, k), F32)], sem=("parallel", "arbitrary"))(*das, *wcs)


def mm_tn(a, b, nq, shard_major, tka=512):
    n, ka = a.shape
    nb = b.shape[1] // nq

    def body(a_ref, b_ref, o_ref):
        o_ref[...] = _dot_tn(a_ref[...], b_ref[...]).astype(BF16)

    if shard_major:
        out_shape, out_spec = _sds((nb, ka, nq), BF16), pl.BlockSpec((None, tka, nq), lambda j, i: (j, i, 0))
    else:
        out_shape, out_spec = _sds((ka, nb * nq), BF16), pl.BlockSpec((tka, nq), lambda j, i: (i, j))
    return _call(body, name="mm_tn", out_shape=out_shape, grid=(nb, ka // tka),
                 in_specs=[pl.BlockSpec((n, tka), lambda j, i: (0, i)), pl.BlockSpec((n, nq), lambda j, i: (0, j))],
                 out_specs=out_spec, sem=("parallel", "parallel"))(a, b)


def rope_tables(n):
    pos = jnp.arange(n, dtype=F32)
    inv = ROPE_THETA ** (-jnp.arange(0, HD, 2, dtype=F32) / HD)
    ang = pos[:, None] * inv[None, :]
    cos, sin = jnp.cos(ang), jnp.sin(ang)
    return jnp.concatenate([cos, cos], axis=-1), jnp.concatenate([-sin, sin], axis=-1)


PERM_ROWS = 256


def perm_matrix(d):
    n = PERM_ROWS // d
    i = jnp.arange(PERM_ROWS)
    src = (i % n) * d + i // n
    return (src[:, None] == jnp.arange(PERM_ROWS)[None, :]).astype(BF16)


def _perm_f32(p, x):
    hi = x.astype(BF16)
    rest = x - hi.astype(F32)
    mid = rest.astype(BF16)
    lo = (rest - mid.astype(F32)).astype(BF16)
    return (_dot(p, hi) + _dot(p, mid)) + _dot(p, lo)


def _res_spec(d, cols, col_block=0):
    return pl.BlockSpec((d, PERM_ROWS // d, cols), lambda i: (0, i, col_block))


def _perm_specs():
    return [pl.BlockSpec((PERM_ROWS, PERM_ROWS), lambda i: (0, 0))] * (len(DILATIONS) - 1)


def qkv_split(proj, cos2, sin2, perms):
    n = proj.shape[0]
    tm = PERM_ROWS

    def body(p_ref, c_ref, s_ref, *rest):
        perm_refs, outs = rest[:len(DILATIONS) - 1], rest[len(DILATIONS) - 1:]
        c, s = c_ref[...], s_ref[...]
        for h in range(2 * NH):
            t = p_ref[:, h * HD:(h + 1) * HD].astype(F32)
            outs[0][0, :, h * HD:(h + 1) * HD] = (t * c + pltpu.roll(t, HD // 2, 1) * s).astype(BF16)
        outs[0][0, :, 2 * DA:] = p_ref[:, 2 * DA:]
        nat = outs[0][0]
        for o_ref, p_ref_d, d in zip(outs[1:], perm_refs, DILATIONS[1:]):
            o_ref[...] = _dot(p_ref_d[...], nat).astype(BF16).reshape(d, tm // d, 3 * DA)

    tab = pl.BlockSpec((tm, HD), lambda i: (i, 0))
    return _call(body, name="qkv_split", out_shape=tuple(_sds((d, n // d, 3 * DA), BF16) for d in DILATIONS),
                 grid=(n // tm,), in_specs=[pl.BlockSpec((tm, 3 * DA), lambda i: (i, 0)), tab, tab] + _perm_specs(),
                 out_specs=tuple(_res_spec(d, 3 * DA) for d in DILATIONS), sem=("parallel",))(proj, cos2, sin2, *perms)


def dproj_assemble(dqs, dks, dvs, duv, cos2, sin2, perms_t):
    n = duv.shape[0]
    tm = PERM_ROWS
    npat = len(DILATIONS)

    def body(*refs):
        parts = refs[0:npat], refs[npat:2 * npat], refs[2 * npat:3 * npat]
        duv_ref, c_ref, s_ref = refs[3 * npat:3 * npat + 3]
        pt_refs, o_ref = refs[3 * npat + 3:-1], refs[-1]
        c, s = c_ref[...], s_ref[...]
        for part, part_refs in enumerate(parts):
            t = part_refs[0][0].astype(F32)
            for ref, pt_ref in zip(part_refs[1:], pt_refs):
                t = t + _dot(pt_ref[...], ref[...].reshape(tm, DA))
            if part == 2:
                o_ref[:, 2 * DA:3 * DA] = t.astype(BF16)
                continue
            for h in range(NH):
                th = t[:, h * HD:(h + 1) * HD]
                o_ref[:, part * DA + h * HD:part * DA + (h + 1) * HD] = (th * c - pltpu.roll(th, HD // 2, 1) * s).astype(BF16)
        o_ref[:, 3 * DA:] = duv_ref[...]

    blks = [_res_spec(d, DA) for d in DILATIONS]
    tab = pl.BlockSpec((tm, HD), lambda i: (i, 0))
    return _call(body, name="dproj_assemble", out_shape=_sds((n, DIN), BF16), grid=(n // tm,),
                 in_specs=blks * 3 + [pl.BlockSpec((tm, 2 * DG), lambda i: (i, 0)), tab, tab] + _perm_specs(),
                 out_specs=pl.BlockSpec((tm, DIN), lambda i: (i, 0)), sem=("parallel",))(
        *dqs, *dks, *dvs, duv, cos2, sin2, *perms_t)


ATTN_BLOCK_BYTES = 2 << 20


def _attn_tiles(ls):
    tq = min(128, ls)
    w = min(tq + 2 * NSIDE, ls)
    return tq, w, ls // tq


def _attn_heads(ls):
    return NH if ls * DA * 2 <= ATTN_BLOCK_BYTES else 1


ATTN_UNROLL = 8


def _attn_loop(nt, hb, tile):
    for h in range(hb):
        def trip(t, carry, h=h):
            tile(t, h)
            return carry

        lax.fori_loop(0, nt, trip, 0, unroll=min(nt, ATTN_UNROLL))


def _band(t, tq, w, ls):
    q0 = pl.multiple_of(t * tq, tq)
    ks = pl.multiple_of(jnp.clip(t * tq - NSIDE, 0, ls - w), NSIDE)
    qpos = q0 + lax.broadcasted_iota(jnp.int32, (tq, w), 0)
    kpos = ks + lax.broadcasted_iota(jnp.int32, (tq, w), 1)
    return q0, ks, jnp.abs(kpos - qpos) <= NSIDE


def _attn_specs(ls, hb):
    nhb = NH // hb
    q = pl.BlockSpec((None, ls, hb * HD), lambda g: (g // nhb, 0, g % nhb))
    k = pl.BlockSpec((None, ls, hb * HD), lambda g: (g // nhb, 0, nhb + g % nhb))
    v = pl.BlockSpec((None, ls, hb * HD), lambda g: (g // nhb, 0, 2 * nhb + g % nhb))
    return q, k, v, q


def _lse_spec(ls, hb):
    nhb = NH // hb
    return pl.BlockSpec((None, ls, LANES), lambda g: (g // nhb, 0, 0))


def attn_fwd(qkv):
    d, ls, _ = qkv.shape
    tq, w, nt = _attn_tiles(ls)
    hb = _attn_heads(ls)
    nhb = NH // hb

    def body(q_ref, k_ref, v_ref, o_ref, l_ref):
        first = (pl.program_id(0) % nhb) * hb
        lane = lax.broadcasted_iota(jnp.int32, (tq, LANES), 1)

        @pl.when(first == 0)
        def _():
            l_ref[...] = jnp.zeros_like(l_ref)

        def tile(t, h):
            q0, ks, valid = _band(t, tq, w, ls)
            cols = slice(h * HD, (h + 1) * HD)
            s = _dot_nt(q_ref[pl.ds(q0, tq), cols], k_ref[pl.ds(ks, w), cols]) * SCALE
            s = jnp.where(valid, s, NEG)
            m = jnp.max(s, axis=1, keepdims=True)
            p = jnp.exp(s - m)
            l = jnp.sum(p, axis=1, keepdims=True)
            o = _dot(p.astype(BF16), v_ref[pl.ds(ks, w), cols]) / l
            o_ref[pl.ds(q0, tq), cols] = o.astype(BF16)
            l_ref[pl.ds(q0, tq), :] = jnp.where(lane == first + h, m + jnp.log(l), l_ref[pl.ds(q0, tq), :])

        _attn_loop(nt, hb, tile)

    q, k, v, o = _attn_specs(ls, hb)
    return _call(body, name="attn_fwd_d%d" % d, out_shape=(_sds((d, ls, DA), BF16), _sds((d, ls, LANES), F32)),
                 grid=(d * NH // hb,), in_specs=[q, k, v], out_specs=(o, _lse_spec(ls, hb)), sem=("arbitrary",))(
        qkv, qkv, qkv)


def attn_merge(os_, ls_, ga, perms_t):
    n = os_[0].shape[1]
    tm = PERM_ROWS
    npat = len(DILATIONS)

    def body(*refs):
        o_refs, l_refs = refs[:npat], refs[npat:2 * npat]
        g_ref, pt_refs = refs[2 * npat], refs[2 * npat + 1:3 * npat]
        a_ref, lse_ref, mix_ref = refs[3 * npat:]
        ov = [o_refs[0][0].astype(F32)] + [_dot(pt[...], o[...].reshape(tm, DA)) for o, pt in zip(o_refs[1:], pt_refs)]
        lv = [l_refs[0][0]] + [_perm_f32(pt[...], l[...].reshape(tm, LANES)) for l, pt in zip(l_refs[1:], pt_refs)]
        lmax = functools.reduce(jnp.maximum, lv)
        e = [jnp.exp(x - lmax) for x in lv]
        den = functools.reduce(lambda p, q: p + q, e)
        wts = [ei / den for ei in e]
        lse_ref[...] = lmax + jnp.log(den)
        heads = []
        for h in range(NH):
            cols = slice(h * HD, (h + 1) * HD)
            heads.append(functools.reduce(lambda p, q: p + q, [wt[:, h:h + 1] * oi[:, cols] for wt, oi in zip(wts, ov)]))
        a = jnp.concatenate(heads, axis=1)
        a_ref[...] = a.astype(BF16)
        r = lax.rsqrt(jnp.mean(a * a, axis=-1, keepdims=True) + EPS)
        mix_ref[...] = (a * r * g_ref[...]).astype(BF16)

    blk = pl.BlockSpec((tm, DA), lambda i: (i, 0))
    return _call(body, name="attn_merge", out_shape=(_sds((n, DA), BF16), _sds((n, LANES), F32), _sds((n, D), BF16)),
                 grid=(n // tm,),
                 in_specs=[_res_spec(d, DA) for d in DILATIONS] + [_res_spec(d, LANES) for d in DILATIONS]
                 + [pl.BlockSpec((1, DA), lambda i: (0, 0))] + _perm_specs(),
                 out_specs=(blk, pl.BlockSpec((tm, LANES), lambda i: (i, 0)), blk), sem=("parallel",))(
        *os_, *ls_, ga, *perms_t)


def attn_norm_bwd(dmix, a, lse, ga, perms):
    n = a.shape[0]
    tm = PERM_ROWS
    npat = len(DILATIONS)

    def body(dm_ref, a_ref, lse_ref, g_ref, *rest):
        p_refs = rest[:npat - 1]
        da_refs, a_refs, lse_refs = rest[npat - 1:2 * npat - 1], rest[2 * npat - 1:3 * npat - 2], rest[3 * npat - 2:4 * npat - 3]
        dg_ref = rest[-1]

        @pl.when(pl.program_id(0) == 0)
        def _():
            dg_ref[...] = jnp.zeros_like(dg_ref)

        ab = a_ref[...]
        av = ab.astype(F32)
        r = lax.rsqrt(jnp.mean(av * av, axis=-1, keepdims=True) + EPS)
        ahat = av * r
        dm = dm_ref[...].astype(F32)
        dg_ref[...] += jnp.sum(dm * ahat, axis=0, keepdims=True)
        dn = dm * g_ref[...]
        da = (r * (dn - ahat * jnp.mean(dn * ahat, axis=-1, keepdims=True))).astype(BF16)
        da_refs[0][0] = da
        lv = lse_ref[...]
        for i, d in enumerate(DILATIONS[1:]):
            p = p_refs[i][...]
            da_refs[i + 1][...] = _dot(p, da).astype(BF16).reshape(d, tm // d, DA)
            a_refs[i][...] = _dot(p, ab).astype(BF16).reshape(d, tm // d, DA)
            lse_refs[i][...] = _perm_f32(p, lv).reshape(d, tm // d, LANES)

    blk = pl.BlockSpec((tm, DA), lambda i: (i, 0))
    vec = pl.BlockSpec((1, DA), lambda i: (0, 0))
    res = [_res_spec(d, DA) for d in DILATIONS]
    outs = _call(body, name="attn_norm_bwd",
                 out_shape=tuple([_sds((d, n // d, DA), BF16) for d in DILATIONS]
                                 + [_sds((d, n // d, DA), BF16) for d in DILATIONS[1:]]
                                 + [_sds((d, n // d, LANES), F32) for d in DILATIONS[1:]] + [_sds((1, DA), F32)]),
                 grid=(n // tm,),
                 in_specs=[blk, blk, pl.BlockSpec((tm, LANES), lambda i: (i, 0)), vec] + _perm_specs(),
                 out_specs=tuple(res + res[1:] + [_res_spec(d, LANES) for d in DILATIONS[1:]] + [vec]),
                 sem=("arbitrary",))(dmix, a, lse, ga, *perms)
    return outs[:npat], outs[npat:2 * npat - 1], outs[2 * npat - 1:3 * npat - 2], outs[-1]


def attn_bwd(qkv, a, da, lse):
    d, ls, _ = qkv.shape
    tq, w, nt = _attn_tiles(ls)
    hb = _attn_heads(ls)

    nhb = NH // hb

    def body(q_ref, k_ref, v_ref, a_ref, da_ref, lse_ref, dq_ref, dk_ref, dv_ref, dk_acc, dv_acc):
        first = (pl.program_id(0) % nhb) * hb
        lane = lax.broadcasted_iota(jnp.int32, (tq, LANES), 1)
        dk_acc[...] = jnp.zeros_like(dk_acc)
        dv_acc[...] = jnp.zeros_like(dv_acc)

        def tile(t, h):
            q0, ks, valid = _band(t, tq, w, ls)
            cols = slice(h * HD, (h + 1) * HD)
            q = q_ref[pl.ds(q0, tq), cols]
            k = k_ref[pl.ds(ks, w), cols]
            v = v_ref[pl.ds(ks, w), cols]
            do = da_ref[pl.ds(q0, tq), cols]
            if nhb == 1:
                lse_h = lse_ref[pl.ds(q0, tq), h:h + 1]
            else:
                lse_h = jnp.sum(jnp.where(lane == first + h, lse_ref[pl.ds(q0, tq), :], 0.0), axis=1, keepdims=True)
            s = jnp.where(valid, _dot_nt(q, k) * SCALE, NEG)
            p = jnp.exp(s - lse_h)
            drow = jnp.sum(do.astype(F32) * a_ref[pl.ds(q0, tq), cols].astype(F32), axis=1, keepdims=True)
            ds = (p * (_dot_nt(do, v) - drow) * SCALE).astype(BF16)
            dv_acc[pl.ds(ks, w), cols] += _dot_tn(p.astype(BF16), do)
            dk_acc[pl.ds(ks, w), cols] += _dot_tn(ds, q)
            dq_ref[pl.ds(q0, tq), cols] = _dot(ds, k).astype(BF16)

        _attn_loop(nt, hb, tile)
        dk_ref[...] = dk_acc[...].astype(BF16)
        dv_ref[...] = dv_acc[...].astype(BF16)

    q, k, v, o = _attn_specs(ls, hb)
    out = _sds((d, ls, DA), BF16)
    return _call(body, name="attn_bwd_d%d" % d, out_shape=(out, out, out), grid=(d * NH // hb,),
                 in_specs=[q, k, v, o, o, _lse_spec(ls, hb)], out_specs=(o, o, o),
                 scratch=[pltpu.VMEM((ls, hb * HD), F32), pltpu.VMEM((ls, hb * HD), F32)], sem=("parallel",))(
        qkv, qkv, qkv, a, da, lse)


GM_TM = 256
INV_SQRT2 = 0.7071067811865476
INV_SQRT2PI = 0.3989422804014327


def _gelu(x):
    return 0.5 * x * (1.0 + lax.erf(x * INV_SQRT2))


def _gelu_grad(x):
    return 0.5 * (1.0 + lax.erf(x * INV_SQRT2)) + x * (INV_SQRT2PI * jnp.exp(-0.5 * x * x))


def _gmlp_forward(up, vp, ln_g, ws_ref, bias):
    u = _gelu(up)
    v = _gelu(vp)
    vc = v - jnp.mean(v, axis=-1, keepdims=True)
    rs = lax.rsqrt(jnp.mean(vc * vc, axis=-1, keepdims=True) + EPS)
    vhat = vc * rs
    vln = (vhat * ln_g).astype(BF16)
    rows = []
    for c in range(GM_TM // CH):
        cols = [_dot(ws_ref[g], vln[c * CH:(c + 1) * CH, g * 128:(g + 1) * 128]) for g in range(NG)]
        rows.append(jnp.concatenate(cols, axis=1) + bias)
    return u, vhat, rs, vln, jnp.concatenate(rows, axis=0)


def _gmlp_specs():
    ublk = pl.BlockSpec((GM_TM, DG), lambda i: (i, 3 * DA // DG))
    vblk = pl.BlockSpec((GM_TM, DG), lambda i: (i, 3 * DA // DG + 1))
    vec = pl.BlockSpec((1, DG), lambda i: (0, 0))
    wsp = pl.BlockSpec((NG, CH, CH), lambda i: (0, 0, 0))
    bsp = pl.BlockSpec((CH, DG), lambda i: (0, 0))
    return ublk, vblk, vec, wsp, bsp


def gmlp_fwd(proj, ln_g, ws, bias, gg, mix):
    n = proj.shape[0]

    def body(up_ref, vp_ref, ln_ref, ws_ref, b_ref, gg_ref, mix_in, mix_ref):
        del mix_in
        u, _, _, _, mixed = _gmlp_forward(up_ref[...].astype(F32), vp_ref[...].astype(F32), ln_ref[...], ws_ref, b_ref[...])
        gout = u * mixed
        r = lax.rsqrt(jnp.mean(gout * gout, axis=-1, keepdims=True) + EPS)
        mix_ref[...] = (gout * r * gg_ref[...]).astype(BF16)

    ublk, vblk, vec, wsp, bsp = _gmlp_specs()
    return _call(body, name="gmlp_fwd", out_shape=_sds((n, D), BF16), grid=(n // GM_TM,),
                 in_specs=[ublk, vblk, vec, wsp, bsp, vec, pl.BlockSpec(memory_space=pl.ANY)],
                 out_specs=pl.BlockSpec((GM_TM, DG), lambda i: (i, DA // DG)), sem=("parallel",), aliases={6: 0})(
        proj, proj, ln_g, ws, bias, gg, mix)


def gmlp_bwd(proj, dmix, ln_g, ws, wst, bias, gg):
    n = proj.shape[0]

    def body(up_ref, vp_ref, dm_ref, ln_ref, ws_ref, wst_ref, b_ref, gg_ref, duv_ref, dln_ref, dws_ref, dbs_ref, dgg_ref,
             db_ref):
        @pl.when(pl.program_id(0) == 0)
        def _():
            dln_ref[...] = jnp.zeros_like(dln_ref)
            dws_ref[...] = jnp.zeros_like(dws_ref)
            db_ref[...] = jnp.zeros_like(db_ref)
            dgg_ref[...] = jnp.zeros_like(dgg_ref)

        up = up_ref[...].astype(F32)
        vp = vp_ref[...].astype(F32)
        ln_g = ln_ref[...]
        u, vhat, rs, vln, mixed = _gmlp_forward(up, vp, ln_g, ws_ref, b_ref[...])
        gout = u * mixed
        r = lax.rsqrt(jnp.mean(gout * gout, axis=-1, keepdims=True) + EPS)
        ghat = gout * r
        dm = dm_ref[...].astype(F32)
        dgg_ref[...] += jnp.sum(dm * ghat, axis=0, keepdims=True)
        dn = dm * gg_ref[...]
        dgout = r * (dn - ghat * jnp.mean(dn * ghat, axis=-1, keepdims=True))
        du = dgout * mixed
        dmixed = dgout * u
        dmb = dmixed.astype(BF16)
        rows = []
        for c in range(GM_TM // CH):
            rsl = slice(c * CH, (c + 1) * CH)
            db_ref[...] += dmixed[rsl, :]
            cols = []
            for g in range(NG):
                csl = slice(g * 128, (g + 1) * 128)
                dws_ref[g] += _dot_nt(dmb[rsl, csl], vln[rsl, csl])
                cols.append(_dot(wst_ref[g], dmb[rsl, csl]))
            rows.append(jnp.concatenate(cols, axis=1))
        dvln = jnp.concatenate(rows, axis=0)
        dln_ref[...] += jnp.sum(dvln * vhat, axis=0, keepdims=True)
        dvh = dvln * ln_g
        dv = rs * (dvh - jnp.mean(dvh, axis=-1, keepdims=True) - vhat * jnp.mean(dvh * vhat, axis=-1, keepdims=True))
        duv_ref[:, 0:DG] = (du * _gelu_grad(up)).astype(BF16)
        duv_ref[:, DG:] = (dv * _gelu_grad(vp)).astype(BF16)

        @pl.when(pl.program_id(0) == pl.num_programs(0) - 1)
        def _():
            for g in range(NG):
                dbs_ref[g:g + 1, :] = jnp.sum(jnp.transpose(db_ref[:, g * 128:(g + 1) * 128]), axis=0, keepdims=True)

    ublk, vblk, vec, wsp, bsp = _gmlp_specs()
    return _call(body, name="gmlp_bwd",
                 out_shape=(_sds((n, 2 * DG), BF16), _sds((1, DG), F32), _sds((NG, CH, CH), F32), _sds((NG, CH), F32),
                            _sds((1, DG), F32)),
                 grid=(n // GM_TM,),
                 in_specs=[ublk, vblk, pl.BlockSpec((GM_TM, DG), lambda i: (i, DA // DG)), vec, wsp, wsp, bsp, vec],
                 out_specs=(pl.BlockSpec((GM_TM, 2 * DG), lambda i: (i, 0)), vec, wsp,
                            pl.BlockSpec((NG, CH), lambda i: (0, 0)), vec),
                 scratch=[pltpu.VMEM((CH, DG), F32)], sem=("arbitrary",))(
        proj, proj, dmix, ln_g, ws, wst, bias, gg)


def _row(v):
    return v.reshape(1, -1)


def local_step(x, target, small, big):
    tables = step_tables()
    saved = []
    for l in range(NL):
        x_mid, sv = mixer_fwd(x, small, l, big[l], tables)
        x, sv2 = ffn_fwd(x_mid, small, l, big[l])
        saved.append({**sv, **sv2})
    loss8, dx, dxb, dfinal = loss_head(x, _row(small["final_g"]), target)
    gs = {k: [None] * NL for k in SMALL[:-1]}
    gbig = [None] * NL
    for l in reversed(range(NL)):
        dx, dxb, gs1, gb1 = ffn_bwd(dx, dxb, small, l, big[l], saved[l])
        dx, dxb, gs2, gb2 = mixer_bwd(dx, dxb, small, l, big[l], saved[l], tables)
        gbig[l] = {**gb1, **gb2}
        for k, g in {**gs1, **gs2}.items():
            gs[k][l] = g
    return loss8, dx, gs, dfinal, gbig


def step_tables():
    perms = [perm_matrix(d) for d in DILATIONS[1:]]
    return rope_tables(T) + (perms, [p.T for p in perms])


def mixer_fwd(x, small, l, w, tables, after=None):
    cos2, sin2, perms, perms_t = tables
    ws_b = small["w_spatial"][l].astype(BF16)
    bias = jnp.repeat(small["b_spatial"][l].T, 128, axis=1)
    h = rms_fwd(x, _row(small["norm1_g"][l]), after)
    proj = mm_cols(h, w["win"])
    qkvs = qkv_split(proj, cos2, sin2, perms)
    outs = [attn_fwd(qkv) for qkv in qkvs]
    a, lse, mix = attn_merge([o for o, _ in outs], [s for _, s in outs], _row(small["mix_norm_attn_g"][l]), perms_t)
    mix = gmlp_fwd(proj, _row(small["gmlp_ln_g"][l]), ws_b, bias, _row(small["mix_norm_gmlp_g"][l]), mix)
    x_mid = mm_rows_res(mix, w["wout"], x)
    return x_mid, dict(x=x, h=h, proj=proj, qkvs=qkvs, a=a, lse=lse, mix=mix, ws_b=ws_b, bias=bias)


def ffn_fwd(x_mid, small, l, w, after=None):
    h2 = rms_fwd(x_mid, _row(small["norm2_g"][l]), after)
    gate, up, ff = mm_gateup(h2, w["wg"], w["wu"])
    x_out = mm_rows_res(ff, w["wd"], x_mid)
    return x_out, dict(x_mid=x_mid, h2=h2, gate=gate, up=up, ff=ff)


def ffn_bwd(dx, dxb, small, l, w, sv, after=None):
    gs = {}
    dgate, dup = mm_dff(dxb, w["wd"], sv["gate"], sv["up"], tk=FF // NCHIP, after=after)
    g_wd = mm_tn(sv["ff"], dxb, D // 2, False)
    dh2 = mm_nt_cols([dgate, dup], [w["wg"], w["wu"]])
    g_wg = mm_tn(sv["h2"], dgate, FF // NCHIP, True)
    g_wu = mm_tn(sv["h2"], dup, FF // NCHIP, True)
    dx, dxb, gs["norm2_g"] = rms_bwd(dh2, sv["x_mid"], _row(small["norm2_g"][l]), dx)
    return dx, dxb, gs, dict(wg=g_wg, wu=g_wu, wd=g_wd.reshape(NCHIP, FF // NCHIP, D))


def mixer_bwd(dx, dxb, small, l, w, sv, tables, after=None):
    cos2, sin2, perms, perms_t = tables
    gs = {}
    dmix = mm_nt_rows(dxb, w["wout"], tk=D // 2, after=after)
    g_wout = mm_tn(sv["mix"], dxb, D // 2, False)
    das, a_res, lse_res, gs["mix_norm_attn_g"] = attn_norm_bwd(dmix, sv["a"], sv["lse"],
                                                                _row(small["mix_norm_attn_g"][l]), perms)
    a_all = [sv["a"].reshape((1,) + sv["a"].shape)] + list(a_res)
    lse_all = [sv["lse"].reshape((1,) + sv["lse"].shape)] + list(lse_res)
    parts = [attn_bwd(*operands) for operands in zip(sv["qkvs"], a_all, das, lse_all)]
    wst = jnp.swapaxes(small["w_spatial"][l], 1, 2).astype(BF16)
    duv, gs["gmlp_ln_g"], gs["w_spatial"], gs["b_spatial"], gs["mix_norm_gmlp_g"] = gmlp_bwd(
        sv["proj"], dmix, _row(small["gmlp_ln_g"][l]), sv["ws_b"], wst, sv["bias"], _row(small["mix_norm_gmlp_g"][l]))
    dproj = dproj_assemble([p[0] for p in parts], [p[1] for p in parts], [p[2] for p in parts], duv, cos2, sin2,
                           perms_t)
    dh = mm_nt_cols([dproj], [w["win"]])
    g_win = mm_tn(sv["h"], dproj, DIN // NCHIP, True)
    dx, dxb, gs["norm1_g"] = rms_bwd(dh, sv["x"], _row(small["norm1_g"][l]), dx)
    return dx, dxb, gs, dict(win=g_win, wout=g_wout.reshape(NCHIP, D // NCHIP, D))


KINDS = ("win", "wout", "wg", "wu", "wd")
GROUPS = (("win", "wout"), ("wg", "wu", "wd"))
ANY = pl.BlockSpec(memory_space=pl.ANY)


def _place():
    return lax.axis_index("x"), lax.axis_index("y"), lax.axis_index("c")


def _other_chips(x, y):
    return [(1 - x, y), (x, 1 - y), (1 - x, 1 - y)]


def _remote(src, dst, send_sem, recv_sem, to):
    return pltpu.make_async_remote_copy(src_ref=src, dst_ref=dst, send_sem=send_sem, recv_sem=recv_sem,
                                        device_id=to, device_id_type=MESH)


def _hbm_call(body, *, name, n_in, out_shape, scratch, in_place=False):
    return pl.pallas_call(body, name=name, out_shape=tuple(out_shape), in_specs=[ANY] * n_in,
                          out_specs=tuple(ANY for _ in out_shape), scratch_shapes=list(scratch),
                          input_output_aliases={k: k for k in range(n_in)} if in_place else {},
                          compiler_params=pltpu.CompilerParams(vmem_limit_bytes=VMEM_LIMIT))


def place_vector():
    x, y, c = _place()
    return jnp.stack([c, 2 * x + y] + [2 * cx + cy for cx, cy in _other_chips(x, y)]).astype(jnp.int32)


PUSH_ROWS = 512
PUSH_SLOTS = 3


def _push_tiles(src, dst, buf, load_sem, send_sem, recv_sem, to, src_at, dst_at, nt):
    tr = buf.shape[1]
    n = len(src_at) * nt

    def pick(vals, seg):
        out = vals[0]
        for q in range(1, len(vals)):
            out = jnp.where(seg == q, vals[q], out)
        return out

    def tile(ref, at, t):
        seg = t // nt
        row = pl.multiple_of(pick([a[1] for a in at], seg) + (t - seg * nt) * tr, 16)
        return ref.at[pick([a[0] for a in at], seg), pl.ds(row, tr), :]

    def load(t):
        slot = t % PUSH_SLOTS
        return pltpu.make_async_copy(tile(src, src_at, t), buf.at[slot], load_sem.at[slot])

    def send(t):
        slot = t % PUSH_SLOTS
        return _remote(buf.at[slot], tile(dst, dst_at, t), send_sem.at[slot], recv_sem, to)

    load(0).start()

    def step(t, carry):
        load(t).wait()
        send(t).start()

        @pl.when(t >= PUSH_SLOTS - 1)
        def _():
            send(t - (PUSH_SLOTS - 1)).wait_send()

        @pl.when(t + 1 < n)
        def _():
            load(t + 1).start()

        return carry

    lax.fori_loop(0, n, step, 0)
    for t in range(max(n - (PUSH_SLOTS - 1), 0), n):
        send(t).wait_send()


def _await_tiles(dst, send_sem, recv_sem, to, nseg, rows):
    whole = dst.at[pl.ds(0, nseg), pl.ds(0, rows), :]
    _remote(whole, whole, send_sem.at[0], recv_sem, to).wait_recv()


def _push_rows(seg_rows):
    return _row_tile(seg_rows, PUSH_ROWS)


def _push_scratch(arrs, seg_rows):
    return ([pltpu.VMEM((PUSH_SLOTS, _push_rows(r), a.shape[-1]), a.dtype) for a, r in zip(arrs, seg_rows)]
            + [pltpu.SemaphoreType.DMA((len(arrs), PUSH_SLOTS)), pltpu.SemaphoreType.DMA((len(arrs), PUSH_SLOTS)),
               pltpu.SemaphoreType.DMA((len(arrs),))])


def cast_into(w, l, pos, after=None):
    _, r, cols = w.shape
    tr = _row_tile(r)
    extra, extra_specs = _behind(after)

    def body(pos_ref, w_ref, *rest):
        del pos_ref
        rest[-1][...] = w_ref[...].astype(BF16)

    return _call(body, name="cast_into", out_shape=_sds((NCHIP, r, cols), BF16), grid=(r // tr,),
                 in_specs=[pl.BlockSpec((None, tr, cols), lambda i, pos: (l, i, 0))] + extra_specs,
                 out_specs=pl.BlockSpec((None, tr, cols), lambda i, pos: (pos[1], i, 0)),
                 sem=("parallel",), prefetch=1)(pos, w, *extra)


HBM = pl.BlockSpec(memory_space=pltpu.HBM)
SEM = pl.BlockSpec(memory_space=pltpu.SEMAPHORE)
EFFECT = pltpu.SideEffectType.DATAFLOW_SIDE_EFFECTING


def _in_hbm(a):
    return pltpu.with_memory_space_constraint(a, pltpu.HBM)


def _gather_copies(bufs, send, recv):
    x, y, c = _place()
    chips = _other_chips(x, y)

    def half(k, chip):
        hr = bufs[k].shape[1] // 2
        return bufs[k].at[chip, pl.ds(c * hr, hr), :]

    out, back = [], []
    for k in range(len(bufs)):
        for j, (cx, cy) in enumerate(chips):
            i = 3 * k + j
            out.append(_remote(half(k, 2 * x + y), half(k, 2 * x + y), send.at[i], recv.at[i], (cx, cy, c)))
            back.append(_remote(half(k, 2 * x + y), half(k, 2 * cx + cy), send.at[i], recv.at[i], (cx, cy, c)))
    return out, back


def gather_start(fulls):
    nk = len(fulls)

    def body(*refs):
        send, recv = refs[nk], refs[nk + 1]
        bufs, token = refs[nk + 2:2 * nk + 2], refs[2 * nk + 2]
        for cp in _gather_copies(bufs, send, recv)[0]:
            cp.start()
        token[...] = jnp.zeros_like(token)

    res = pl.pallas_call(
        body, name="gather_start",
        out_shape=(pltpu.SemaphoreType.DMA((3 * nk,)), pltpu.SemaphoreType.DMA((3 * nk,)),
                   *[pltpu.HBM(f.shape, f.dtype) for f in fulls], _sds((8, LANES), F32)),
        in_specs=[HBM] * nk, out_specs=(SEM, SEM, *[HBM] * nk, pl.BlockSpec(memory_space=pltpu.VMEM)),
        input_output_aliases={k: 2 + k for k in range(nk)},
        compiler_params=pltpu.CompilerParams(has_side_effects=EFFECT))(*[_in_hbm(f) for f in fulls])
    return res[0], res[1], list(res[2:2 + nk]), res[2 + nk]


def gather_wait(send, recv, fulls, after):
    nk = len(fulls)

    def body(*refs):
        bufs, send_ref, recv_ref = refs[:nk], refs[nk], refs[nk + 1]
        for cp in _gather_copies(bufs, send_ref, recv_ref)[1]:
            cp.wait_send()
            cp.wait_recv()

    return list(pl.pallas_call(
        body, name="gather_wait", out_shape=tuple(pltpu.HBM(f.shape, f.dtype) for f in fulls),
        in_specs=[HBM] * nk + [SEM, SEM, ANY], out_specs=tuple([HBM] * nk),
        input_output_aliases={k: k for k in range(nk)},
        compiler_params=pltpu.CompilerParams(has_side_effects=EFFECT))(*fulls, send, recv, after))


def pair_forward(fulls):
    nk = len(fulls)

    def body(*refs):
        bufs = refs[nk:2 * nk]
        stage = refs[2 * nk:3 * nk]
        load_sem, send_sem, recv_sem = refs[3 * nk:]
        x, y, c = _place()
        sib = (x, y, 1 - c)
        chips = [2 * cx + cy for cx, cy in _other_chips(x, y)]
        for k in range(nk):
            hr = bufs[k].shape[1] // 2
            at = [(chip, c * hr) for chip in chips]
            _push_tiles(bufs[k], bufs[k], stage[k], load_sem.at[k], send_sem.at[k], recv_sem.at[k], sib, at, at,
                        hr // _push_rows(hr))
        for k in range(nk):
            _await_tiles(bufs[k], send_sem.at[k], recv_sem.at[k], sib, 3, bufs[k].shape[1] // 2)

    return _hbm_call(body, name="pair_forward", n_in=nk, out_shape=[_sds(f.shape, f.dtype) for f in fulls], in_place=True,
                     scratch=_push_scratch(fulls, [f.shape[1] // 2 for f in fulls]))(*fulls)


def rs_pair_send(gs):
    nk = len(gs)

    def body(*refs):
        ins, got = refs[:nk], refs[nk:2 * nk]
        stage = refs[2 * nk:3 * nk]
        load_sem, send_sem, recv_sem = refs[3 * nk:]
        x, y, c = _place()
        sib = (x, y, 1 - c)
        for k in range(nk):
            hr = ins[k].shape[1] // 2
            _push_tiles(ins[k], got[k], stage[k], load_sem.at[k], send_sem.at[k], recv_sem.at[k], sib,
                        [(s, (1 - c) * hr) for s in range(NCHIP)], [(s, 0) for s in range(NCHIP)], hr // _push_rows(hr))
        for k in range(nk):
            _await_tiles(got[k], send_sem.at[k], recv_sem.at[k], sib, NCHIP, ins[k].shape[1] // 2)

    half = [_sds((g.shape[0], g.shape[1] // 2, g.shape[2]), g.dtype) for g in gs]
    return _hbm_call(body, name="rs_pair_send", n_in=nk, out_shape=half,
                     scratch=_push_scratch(gs, [g.shape[1] // 2 for g in gs]))(*gs)


def _chip_copies(ps, lands, send, recv):
    x, y, c = _place()
    return [_remote(ps[k].at[2 * cx + cy], lands[k].at[j], send.at[3 * k + j], recv.at[3 * k + j], (cx, cy, c))
            for k in range(len(ps)) for j, (cx, cy) in enumerate(_other_chips(x, y))]


def chip_exchange_start(ps):
    nk = len(ps)
    lands = [lax.empty((3,) + p.shape[1:], p.dtype) for p in ps]

    def body(*refs):
        send, recv = refs[2 * nk], refs[2 * nk + 1]
        srcs, dsts, token = refs[2 * nk + 2:3 * nk + 2], refs[3 * nk + 2:4 * nk + 2], refs[4 * nk + 2]
        for cp in _chip_copies(srcs, dsts, send, recv):
            cp.start()
        token[...] = jnp.zeros_like(token)

    res = pl.pallas_call(
        body, name="chip_exchange_start",
        out_shape=(pltpu.SemaphoreType.DMA((3 * nk,)), pltpu.SemaphoreType.DMA((3 * nk,)),
                   *[pltpu.HBM(a.shape, a.dtype) for a in ps + lands], _sds((8, LANES), F32)),
        in_specs=[HBM] * (2 * nk), out_specs=(SEM, SEM, *[HBM] * (2 * nk), pl.BlockSpec(memory_space=pltpu.VMEM)),
        input_output_aliases={k: 2 + k for k in range(2 * nk)},
        compiler_params=pltpu.CompilerParams(has_side_effects=EFFECT))(*[_in_hbm(a) for a in ps + lands])
    return res[0], res[1], list(res[2:2 + nk]), list(res[2 + nk:2 + 2 * nk]), res[2 + 2 * nk]


def chip_exchange_wait(send, recv, ps, lands, after):
    nk = len(ps)

    def body(*refs):
        srcs, dsts, send_ref, recv_ref = refs[:nk], refs[nk:2 * nk], refs[2 * nk], refs[2 * nk + 1]
        for cp in _chip_copies(srcs, dsts, send_ref, recv_ref):
            cp.wait_send()
            cp.wait_recv()

    res = pl.pallas_call(
        body, name="chip_exchange_wait", out_shape=tuple(pltpu.HBM(a.shape, a.dtype) for a in ps + lands),
        in_specs=[HBM] * (2 * nk) + [SEM, SEM, ANY], out_specs=tuple([HBM] * (2 * nk)),
        input_output_aliases={k: k for k in range(2 * nk)},
        compiler_params=pltpu.CompilerParams(has_side_effects=EFFECT))(*ps, *lands, send, recv, after)
    return list(res[:nk]), list(res[nk:])


def rs_pair_share(fulls):
    nk = len(fulls)

    def body(*refs):
        bufs = refs[nk:2 * nk]
        stage = refs[2 * nk:3 * nk]
        load_sem, send_sem, recv_sem = refs[3 * nk:]
        x, y, c = _place()
        sib = (x, y, 1 - c)
        for k in range(nk):
            hr = bufs[k].shape[1]
            _push_tiles(bufs[k], bufs[k], stage[k], load_sem.at[k], send_sem.at[k], recv_sem.at[k], sib,
                        [(c, 0)], [(c, 0)], hr // _push_rows(hr))
        for k in range(nk):
            _await_tiles(bufs[k], send_sem.at[k], recv_sem.at[k], sib, 1, bufs[k].shape[1])

    return _hbm_call(body, name="rs_pair_share", n_in=nk, out_shape=[_sds(f.shape, f.dtype) for f in fulls], in_place=True,
                     scratch=_push_scratch(fulls, [f.shape[1] for f in fulls]))(*fulls)


N_DEV = 8


def allreduce_small(buf, after=None):
    rows = buf.shape[0]
    rp = rows // N_DEV
    extra, extra_specs = _behind(after)

    def body(in_ref, *rest):
        out_ref, land_ref, send, recv = rest[-4:]
        x, y, c = _place()
        mine = pl.ds(pl.multiple_of((4 * x + 2 * y + c) * rp, 8), rp)
        peers = [(x ^ fx, y ^ fy, c ^ fc) for fx in (0, 1) for fy in (0, 1) for fc in (0, 1)][1:]
        block = [pl.ds(pl.multiple_of((4 * px + 2 * py + pc) * rp, 8), rp) for px, py, pc in peers]
        scatter = [_remote(in_ref.at[block[r], :], land_ref.at[r], send.at[r], recv.at[r], peers[r]) for r in range(7)]
        for cp in scatter:
            cp.start()
        for cp in scatter:
            cp.wait()
        acc = in_ref[mine, :]
        for r in range(7):
            acc = acc + land_ref[r]
        out_ref[mine, :] = acc
        spread = [_remote(out_ref.at[mine, :], out_ref.at[mine, :], send.at[7 + r], recv.at[7 + r], peers[r]) for r in range(7)]
        for cp in spread:
            cp.start()
        for r in range(7):
            _remote(out_ref.at[block[r], :], out_ref.at[block[r], :], send.at[7 + r], recv.at[7 + r], peers[r]).wait_recv()
        for cp in spread:
            cp.wait_send()

    vm = pl.BlockSpec(memory_space=pltpu.VMEM)
    return pl.pallas_call(body, name="allreduce_small", out_shape=_sds(buf.shape, F32), in_specs=[vm] + extra_specs,
                          out_specs=vm,
                          scratch_shapes=[pltpu.VMEM((7, rp, LANES), F32), pltpu.SemaphoreType.DMA((14,)),
                                          pltpu.SemaphoreType.DMA((14,))])(buf, *extra)


def _row_tile(rows, cap=512):
    return max(t for t in range(16, cap + 1, 16) if rows % t == 0)


def add_halves(g, got, pos):
    s, r, cols = g.shape
    hr = r // 2
    tr = _row_tile(hr)
    nt = hr // tr

    def body(pos_ref, a_ref, b_ref, o_ref):
        del pos_ref
        o_ref[...] = (a_ref[...].astype(F32) + b_ref[...].astype(F32)).astype(BF16)

    blk = pl.BlockSpec((None, tr, cols), lambda q, i, pos: (q, i, 0))
    return _call(body, name="add_halves", out_shape=_sds((s, hr, cols), BF16), grid=(s, nt),
                 in_specs=[pl.BlockSpec((None, tr, cols), lambda q, i, pos: (q, pos[0] * nt + i, 0)), blk],
                 out_specs=blk, sem=("parallel", "parallel"), prefetch=1)(pos, g, got)


def add_slots(p, got, pos):
    _, hr, cols = p.shape
    tr = _row_tile(hr)

    def body(pos_ref, o_ref, g0, g1, g2, out_ref):
        del pos_ref
        out_ref[...] = ((o_ref[...].astype(F32) + g0[...].astype(F32)) + g1[...].astype(F32)) + g2[...].astype(F32)

    slots = [pl.BlockSpec((None, tr, cols), functools.partial(lambda j, i, pos: (j, i, 0), j)) for j in range(3)]
    return _call(body, name="add_slots", out_shape=_sds((2, hr, cols), F32), grid=(hr // tr,),
                 in_specs=[pl.BlockSpec((None, tr, cols), lambda i, pos: (pos[1], i, 0))] + slots,
                 out_specs=pl.BlockSpec((None, tr, cols), lambda i, pos: (pos[0], i, 0)),
                 sem=("parallel",), prefetch=1)(pos, p, got, got, got)


def _adamw_math(w, g, m, v):
    nm = B1 * m + (1.0 - B1) * g
    nv = B2 * v + (1.0 - B2) * (g * g)
    m_hat = nm / (1.0 - B1 ** STEP)
    v_hat = nv / (1.0 - B2 ** STEP)
    return -LR * (m_hat / (jnp.sqrt(v_hat) + AEPS) + WD * w), nm, nv


def adamw_layer(w, g, m, v, l, prev):
    nl, r, cols = w.shape
    tr = _row_tile(r, 256)

    def body(w_ref, g_ref, m_ref, v_ref, *rest):
        go_ref, d_ref, nm_ref, nv_ref = rest[-4:]
        gv = g_ref[...]
        d, nm, nv = _adamw_math(w_ref[...], gv, m_ref[...], v_ref[...])
        go_ref[...] = gv
        d_ref[...] = d
        nm_ref[...] = nm
        nv_ref[...] = nv

    lay = pl.BlockSpec((None, tr, cols), lambda i: (l, i, 0))
    o = _sds((nl, r, cols), F32)
    extra = [] if prev is None else list(prev)
    return _call(body, name="adamw_layer", out_shape=(o, o, o, o), grid=(r // tr,),
                 in_specs=[lay, pl.BlockSpec((tr, cols), lambda i: (i, 0)), lay, lay] + [ANY] * len(extra),
                 out_specs=(lay, lay, lay, lay), sem=("parallel",),
                 aliases={4 + j: j for j in range(len(extra))})(w, g, m, v, *extra)


def adamw(w, g, m, v):
    shape = w.shape
    cols = shape[-1]
    rows = w.size // cols
    tr = _row_tile(rows, 256)

    def body(w_ref, g_ref, m_ref, v_ref, d_ref, nm_ref, nv_ref):
        d_ref[...], nm_ref[...], nv_ref[...] = _adamw_math(w_ref[...], g_ref[...], m_ref[...], v_ref[...])

    blk = pl.BlockSpec((tr, cols), lambda i: (i, 0))
    o = _sds((rows, cols), F32)
    outs = _call(body, name="adamw", out_shape=(o, o, o), grid=(rows // tr,), in_specs=[blk] * 4, out_specs=(blk, blk, blk),
                 sem=("parallel",))(*[t.reshape(rows, cols) for t in (w, g, m, v)])
    return [t.reshape(shape) for t in outs]


SMALL = ("norm1_g", "gmlp_ln_g", "w_spatial", "b_spatial", "mix_norm_attn_g", "mix_norm_gmlp_g", "norm2_g", "final_g")
WEIGHTS = ("norm1_g", "w_in", "gmlp_ln_g", "w_spatial", "b_spatial", "mix_norm_attn_g", "mix_norm_gmlp_g", "w_out",
           "norm2_g", "w_gate", "w_up", "w_down", "final_g")
BIG = dict(win="w_in", wout="w_out", wg="w_gate", wu="w_up", wd="w_down")


def _pack(parts):
    flat = jnp.concatenate([parts[n].reshape(-1) for n in SMALL])
    rows = -(-flat.shape[0] // (LANES * 8 * N_DEV)) * 8 * N_DEV
    return jnp.pad(flat, (0, rows * LANES - flat.shape[0])).reshape(rows, LANES)


def _unpack(buf, like):
    flat = buf.reshape(-1)
    out, at = {}, 0
    for n in SMALL:
        out[n] = flat[at:at + like[n].size].reshape(like[n].shape)
        at += like[n].size
    return out


def kernel(x, norm1_g, w_in, gmlp_ln_g, w_spatial, b_spatial, mix_norm_attn_g, mix_norm_gmlp_g, w_out, norm2_g, w_gate, w_up, w_down, final_g, loss_target, m_norm1_g, m_w_in, m_gmlp_ln_g, m_w_spatial, m_b_spatial, m_mix_norm_attn_g, m_mix_norm_gmlp_g, m_w_out, m_norm2_g, m_w_gate, m_w_up, m_w_down, m_final_g, v_norm1_g, v_w_in, v_gmlp_ln_g, v_w_spatial, v_b_spatial, v_mix_norm_attn_g, v_mix_norm_gmlp_g, v_w_out, v_norm2_g, v_w_gate, v_w_up, v_w_down, v_final_g):
    w = dict(norm1_g=norm1_g, w_in=w_in, gmlp_ln_g=gmlp_ln_g, w_spatial=w_spatial, b_spatial=b_spatial,
             mix_norm_attn_g=mix_norm_attn_g, mix_norm_gmlp_g=mix_norm_gmlp_g, w_out=w_out, norm2_g=norm2_g,
             w_gate=w_gate, w_up=w_up, w_down=w_down, final_g=final_g)
    m = dict(norm1_g=m_norm1_g, w_in=m_w_in, gmlp_ln_g=m_gmlp_ln_g, w_spatial=m_w_spatial, b_spatial=m_b_spatial,
             mix_norm_attn_g=m_mix_norm_attn_g, mix_norm_gmlp_g=m_mix_norm_gmlp_g, w_out=m_w_out, norm2_g=m_norm2_g,
             w_gate=m_w_gate, w_up=m_w_up, w_down=m_w_down, final_g=m_final_g)
    v = dict(norm1_g=v_norm1_g, w_in=v_w_in, gmlp_ln_g=v_gmlp_ln_g, w_spatial=v_w_spatial, b_spatial=v_b_spatial,
             mix_norm_attn_g=v_mix_norm_attn_g, mix_norm_gmlp_g=v_mix_norm_gmlp_g, w_out=v_w_out, norm2_g=v_norm2_g,
             w_gate=v_w_gate, w_up=v_w_up, w_down=v_w_down, final_g=v_final_g)

    pos = place_vector()
    small = {n: w[n] for n in SMALL}
    tables = step_tables()

    def start_gathers(l, after):
        flights = []
        for kinds in GROUPS:
            flights.append(gather_start([cast_into(w[BIG[k]], l, pos, after) for k in kinds]))
            after = flights[-1][3]
        return flights, after

    def arrive(flight, kinds, after):
        fulls = pair_forward(gather_wait(flight[0], flight[1], flight[2], after))
        big = dict(zip(kinds, fulls))
        if "wout" in big:
            big["wout"] = big["wout"].reshape(D, D)
        if "wd" in big:
            big["wd"] = big["wd"].reshape(FF, D)
        return big

    act = x[0]
    flights, token = start_gathers(0, None)
    big, saved = [], []
    for l in range(NL):
        mine, after_mixer = flights, token
        if l + 1 < NL:
            flights, token = start_gathers(l + 1, act if l else token)
            after_mixer = token
        wl = arrive(mine[0], GROUPS[0], act if l else mine[0][3])
        x_mid, sv = mixer_fwd(act, small, l, wl, tables, after_mixer)
        wl.update(arrive(mine[1], GROUPS[1], x_mid))
        act, sv2 = ffn_fwd(x_mid, small, l, wl)
        big.append(wl)
        saved.append({**sv, **sv2})
    loss8, dx, dxb, dfinal = loss_head(act, _row(small["final_g"]), loss_target[0])
    loss = lax.psum(loss8[0, 0], ("x", "y", "c"))

    grads, delta, new_m, new_v = {}, {}, {}, {}
    stacked = {k: None for k in KINDS}
    gs = {k: [None] * NL for k in SMALL[:-1]}

    def send_off(gbig, kinds, l):
        partial_grads = [gbig[k] for k in kinds]
        pair = [add_halves(g, got, pos) for g, got in zip(partial_grads, rs_pair_send(partial_grads))]
        return chip_exchange_start(pair) + (kinds, l)

    def finish(flight, after):
        pair, lands = chip_exchange_wait(*flight[:4], after)
        halves = [add_slots(p, got, pos) for p, got in zip(pair, lands)]
        for k, full in zip(flight[5], rs_pair_share(halves)):
            n = BIG[k]
            stacked[k] = adamw_layer(w[n], full.reshape(w[n].shape[1:]), m[n], v[n], flight[6], stacked[k])
        return stacked[flight[5][-1]][1]

    pending, token = [], None
    for l in reversed(range(NL)):
        dx, dxb, gs1, gbig = ffn_bwd(dx, dxb, small, l, big[l], saved[l], token)
        for flight in pending:
            finish(flight, dx)
        ffn_flight = send_off(gbig, GROUPS[1], l)
        dx, dxb, gs2, gbig = mixer_bwd(dx, dxb, small, l, big[l], saved[l], tables, ffn_flight[4])
        mixer_flight = send_off(gbig, GROUPS[0], l)
        pending, token = [ffn_flight, mixer_flight], mixer_flight[4]
        for k, g in {**gs1, **gs2}.items():
            gs[k][l] = g
    part = {n: jnp.stack(gs[n]).reshape(w[n].shape) for n in SMALL[:-1]}
    part["final_g"] = dfinal.reshape(w["final_g"].shape)
    token = small_sum = allreduce_small(_pack(part), token)
    for flight in pending:
        token = finish(flight, token)
    for k in KINDS:
        grads[BIG[k]], delta[BIG[k]], new_m[BIG[k]], new_v[BIG[k]] = stacked[k]
    grads.update(_unpack(small_sum, part))

    sm = adamw(_pack({n: w[n] for n in SMALL}), _pack({n: grads[n] for n in SMALL}), _pack({n: m[n] for n in SMALL}),
               _pack({n: v[n] for n in SMALL}))
    for res, packed in zip((delta, new_m, new_v), sm):
        res.update(_unpack(packed, part))

    return (loss, dx.reshape(x.shape), *[grads[n] for n in WEIGHTS], *[delta[n] for n in WEIGHTS],
            *[new_m[n] for n in WEIGHTS], *[new_v[n] for n in WEIGHTS])
```

```python
import functools

import jax
import jax.numpy as jnp
from jax import lax
from jax.experimental import pallas as pl
from jax.experimental.pallas import tpu as pltpu

F32 = jnp.float32
BF16 = jnp.bfloat16

D = 2048
T = 4096
NL = 4
HD = 128
DA = D // 2
DG = D - DA
NH = DA // HD
NG = DG // 128
CH = 128
DIN = 3 * DA + 2 * DG
FF = 5632
DILATIONS = (1, 4, 16)
NSIDE = 64
ROPE_THETA = 10000.0
EPS = 1e-6
NEG = -1e30
SCALE = HD ** -0.5
NCHIP = 4

LR, B1, B2, AEPS, WD, STEP = 0.001, 0.9, 0.999, 1e-08, 0.01, 10

LANES = 128
VMEM_LIMIT = 56 * 1024 * 1024
MESH = pl.DeviceIdType.MESH


def _call(body, *, name, out_shape, grid=(), in_specs=None, out_specs=None, scratch=(), sem=None, aliases=None,
          prefetch=0):
    params = dict(vmem_limit_bytes=VMEM_LIMIT)
    if sem is not None:
        params["dimension_semantics"] = sem
    if prefetch:
        spec = pltpu.PrefetchScalarGridSpec(num_scalar_prefetch=prefetch, grid=grid, in_specs=in_specs,
                                            out_specs=out_specs, scratch_shapes=list(scratch))
        return pl.pallas_call(body, name=name, out_shape=out_shape, grid_spec=spec,
                              input_output_aliases=aliases or {}, compiler_params=pltpu.CompilerParams(**params))
    kw = {}
    if grid:
        kw["grid"] = grid
    return pl.pallas_call(
        body, name=name, out_shape=out_shape, in_specs=in_specs, out_specs=out_specs,
        scratch_shapes=list(scratch), input_output_aliases=aliases or {},
        compiler_params=pltpu.CompilerParams(**params), **kw)


def _sds(shape, dtype):
    return jax.ShapeDtypeStruct(tuple(shape), dtype)


def _dot(a, b):
    return jnp.dot(a, b, preferred_element_type=F32)


def _dot_nt(a, b):
    return lax.dot_general(a, b, (((1,), (1,)), ((), ())), preferred_element_type=F32)


def _dot_tn(a, b):
    return lax.dot_general(a, b, (((0,), (0,)), ((), ())), preferred_element_type=F32)


def _sigmoid(x):
    return 0.5 * jnp.tanh(0.5 * x) + 0.5


def _behind(after):
    return ([], []) if after is None else ([after], [pl.BlockSpec(memory_space=pl.ANY)])


def rms_fwd(x, g, after=None):
    n, d = x.shape
    tm = 256
    extra, extra_specs = _behind(after)

    def body(x_ref, g_ref, *rest):
        h_ref = rest[-1]
        xv = x_ref[...]
        r = lax.rsqrt(jnp.mean(xv * xv, axis=-1, keepdims=True) + EPS)
        h_ref[...] = (xv * r * g_ref[...]).astype(BF16)

    return _call(body, name="rms_fwd", out_shape=_sds((n, d), BF16), grid=(n // tm,),
                 in_specs=[pl.BlockSpec((tm, d), lambda i: (i, 0)), pl.BlockSpec((1, d), lambda i: (0, 0))] + extra_specs,
                 out_specs=pl.BlockSpec((tm, d), lambda i: (i, 0)), sem=("parallel",))(x, g, *extra)


def rms_bwd(dh, x, g, dres):
    n, d = x.shape
    tm = 256

    def body(dh_ref, x_ref, g_ref, dres_ref, dx_ref, dxb_ref, dg_ref):
        @pl.when(pl.program_id(0) == 0)
        def _():
            dg_ref[...] = jnp.zeros_like(dg_ref)

        xv = x_ref[...]
        r = lax.rsqrt(jnp.mean(xv * xv, axis=-1, keepdims=True) + EPS)
        xhat = xv * r
        dhv = dh_ref[...].astype(F32)
        dg_ref[...] += jnp.sum(dhv * xhat, axis=0, keepdims=True)
        dxn = dhv * g_ref[...]
        dx = dres_ref[...] + r * (dxn - xhat * jnp.mean(dxn * xhat, axis=-1, keepdims=True))
        dx_ref[...] = dx
        dxb_ref[...] = dx.astype(BF16)

    row = pl.BlockSpec((tm, d), lambda i: (i, 0))
    vec = pl.BlockSpec((1, d), lambda i: (0, 0))
    return _call(body, name="rms_bwd", out_shape=(_sds((n, d), F32), _sds((n, d), BF16), _sds((1, d), F32)),
                 grid=(n // tm,), in_specs=[row, row, vec, row], out_specs=(row, row, vec), sem=("arbitrary",))(dh, x, g, dres)


def loss_head(x, g, target):
    n, d = x.shape
    tm = 256

    def body(x_ref, g_ref, t_ref, loss_ref, dx_ref, dxb_ref, dg_ref):
        @pl.when(pl.program_id(0) == 0)
        def _():
            dg_ref[...] = jnp.zeros_like(dg_ref)
            loss_ref[...] = jnp.zeros_like(loss_ref)

        xv = x_ref[...]
        r = lax.rsqrt(jnp.mean(xv * xv, axis=-1, keepdims=True) + EPS)
        xhat = xv * r
        gv = g_ref[...]
        err = xhat * gv - t_ref[...]
        loss_ref[...] += 0.5 * jnp.sum(jnp.mean(err * err, axis=-1, keepdims=True))
        dy = err * (1.0 / d)
        dg_ref[...] += jnp.sum(dy * xhat, axis=0, keepdims=True)
        dxn = dy * gv
        dx = r * (dxn - xhat * jnp.mean(dxn * xhat, axis=-1, keepdims=True))
        dx_ref[...] = dx
        dxb_ref[...] = dx.astype(BF16)

    row = pl.BlockSpec((tm, d), lambda i: (i, 0))
    vec = pl.BlockSpec((1, d), lambda i: (0, 0))
    return _call(body, name="loss_head",
                 out_shape=(_sds((8, LANES), F32), _sds((n, d), F32), _sds((n, d), BF16), _sds((1, d), F32)),
                 grid=(n // tm,), in_specs=[row, vec, row],
                 out_specs=(pl.BlockSpec((8, LANES), lambda i: (0, 0)), row, row, vec), sem=("arbitrary",))(x, g, target)


def mm_cols(a, wc, tm=512):
    n, k = a.shape
    s, _, nq = wc.shape

    def body(a_ref, w_ref, o_ref):
        o_ref[...] = _dot(a_ref[...], w_ref[...]).astype(BF16)

    return _call(body, name="mm_cols", out_shape=_sds((n, s * nq), BF16), grid=(s, n // tm),
                 in_specs=[pl.BlockSpec((tm, k), lambda j, i: (i, 0)), pl.BlockSpec((None, k, nq), lambda j, i: (j, 0, 0))],
                 out_specs=pl.BlockSpec((tm, nq), lambda j, i: (i, j)), sem=("parallel", "parallel"))(a, wc)


def mm_gateup(h, wg, wu, tm=512):
    n, k = h.shape
    s, _, nq = wg.shape

    def body(h_ref, wg_ref, wu_ref, gate_ref, up_ref, ff_ref):
        hv = h_ref[...]
        gate = _dot(hv, wg_ref[...])
        up = _dot(hv, wu_ref[...])
        gate_ref[...] = gate.astype(BF16)
        up_ref[...] = up.astype(BF16)
        ff_ref[...] = (gate * _sigmoid(gate) * up).astype(BF16)

    wspec = pl.BlockSpec((None, k, nq), lambda j, i: (j, 0, 0))
    ospec = pl.BlockSpec((tm, nq), lambda j, i: (i, j))
    o = _sds((n, s * nq), BF16)
    return _call(body, name="mm_gateup", out_shape=(o, o, o), grid=(s, n // tm),
                 in_specs=[pl.BlockSpec((tm, k), lambda j, i: (i, 0)), wspec, wspec],
                 out_specs=(ospec, ospec, ospec), sem=("parallel", "parallel"))(h, wg, wu)


def mm_rows_res(a, wr, res, tm=512, tn=1024):
    n, k = a.shape
    _, nn = wr.shape

    def body(a_ref, w_ref, r_ref, o_ref):
        o_ref[...] = r_ref[...] + _dot(a_ref[...], w_ref[...])

    return _call(body, name="mm_rows_res", out_shape=_sds((n, nn), F32), grid=(nn // tn, n // tm),
                 in_specs=[pl.BlockSpec((tm, k), lambda j, i: (i, 0)), pl.BlockSpec((k, tn), lambda j, i: (0, j)),
                           pl.BlockSpec((tm, tn), lambda j, i: (i, j))],
                 out_specs=pl.BlockSpec((tm, tn), lambda j, i: (i, j)), sem=("parallel", "parallel"))(a, wr, res)


def mm_nt_rows(dy, wr, tk, tm=512, after=None):
    n, nn = dy.shape
    k, _ = wr.shape
    extra, extra_specs = _behind(after)

    def body(dy_ref, w_ref, *rest):
        rest[-1][...] = _dot_nt(dy_ref[...], w_ref[...]).astype(BF16)

    return _call(body, name="mm_nt_rows", out_shape=_sds((n, k), BF16), grid=(k // tk, n // tm),
                 in_specs=[pl.BlockSpec((tm, nn), lambda j, i: (i, 0)), pl.BlockSpec((tk, nn), lambda j, i: (j, 0))]
                 + extra_specs,
                 out_specs=pl.BlockSpec((tm, tk), lambda j, i: (i, j)), sem=("parallel", "parallel"))(dy, wr, *extra)


def mm_dff(dy, wd, gate, up, tk, tm=512, after=None):
    n, nn = dy.shape
    k, _ = wd.shape
    extra, extra_specs = _behind(after)

    def body(dy_ref, w_ref, gate_ref, up_ref, *rest):
        dgate_ref, dup_ref = rest[-2:]
        dff = _dot_nt(dy_ref[...], w_ref[...])
        gate = gate_ref[...].astype(F32)
        up = up_ref[...].astype(F32)
        sig = _sigmoid(gate)
        dgate_ref[...] = (dff * up * (sig * (1.0 + gate * (1.0 - sig)))).astype(BF16)
        dup_ref[...] = (dff * (gate * sig)).astype(BF16)

    tile = pl.BlockSpec((tm, tk), lambda j, i: (i, j))
    o = _sds((n, k), BF16)
    return _call(body, name="mm_dff", out_shape=(o, o), grid=(k // tk, n // tm),
                 in_specs=[pl.BlockSpec((tm, nn), lambda j, i: (i, 0)), pl.BlockSpec((tk, nn), lambda j, i: (j, 0)), tile, tile]
                 + extra_specs,
                 out_specs=(tile, tile), sem=("parallel", "parallel"))(dy, wd, gate, up, *extra)


def mm_nt_cols(das, wcs, tm=512):
    n = das[0].shape[0]
    s, k, nq = wcs[0].shape
    npair = len(das)

    def body(*refs):
        da_refs, w_refs, o_ref, acc = refs[:npair], refs[npair:2 * npair], refs[2 * npair], refs[2 * npair + 1]
        ss = pl.program_id(1)

        @pl.when(ss == 0)
        def _():
            acc[...] = jnp.zeros_like(acc)

        for da_ref, w_ref in zip(da_refs, w_refs):
            acc[...] += _dot_nt(da_ref[...], w_ref[...])

        @pl.when(ss == s - 1)
        def _():
            o_ref[...] = acc[...].astype(BF16)

    return _call(body, name="mm_nt_cols%d" % npair, out_shape=_sds((n, k), BF16), grid=(n // tm, s),
                 in_specs=[pl.BlockSpec((tm, nq), lambda i, ss: (i, ss))] * npair
                 + [pl.BlockSpec((None, k, nq), lambda i, ss: (ss, 0, 0))] * npair,
                 out_specs=pl.BlockSpec((tm, k), lambda i, ss: (i, 0)),
                 scratch=[pltpu.VMEM((tm, k), F32)], sem=("parallel", "arbitrary"))(*das, *wcs)


def mm_tn(a, b, nq, shard_major, tka=512):
    n, ka = a.shape
    nb = b.shape[1] // nq

    def body(a_ref, b_ref, o_ref):
        o_ref[...] = _dot_tn(a_ref[...], b_ref[...]).astype(BF16)

    if shard_major:
        out_shape, out_spec = _sds((nb, ka, nq), BF16), pl.BlockSpec((None, tka, nq), lambda j, i: (j, i, 0))
    else:
        out_shape, out_spec = _sds((ka, nb * nq), BF16), pl.BlockSpec((tka, nq), lambda j, i: (i, j))
    return _call(body, name="mm_tn", out_shape=out_shape, grid=(nb, ka // tka),
                 in_specs=[pl.BlockSpec((n, tka), lambda j, i: (0, i)), pl.BlockSpec((n, nq), lambda j, i: (0, j))],
                 out_specs=out_spec, sem=("parallel", "parallel"))(a, b)


def rope_tables(n):
    pos = jnp.arange(n, dtype=F32)
    inv = ROPE_THETA ** (-jnp.arange(0, HD, 2, dtype=F32) / HD)
    ang = pos[:, None] * inv[None, :]
    cos, sin = jnp.cos(ang), jnp.sin(ang)
    return jnp.concatenate([cos, cos], axis=-1), jnp.concatenate([-sin, sin], axis=-1)


PERM_ROWS = 256


def perm_matrix(d):
    n = PERM_ROWS // d
    i = jnp.arange(PERM_ROWS)
    src = (i % n) * d + i // n
    return (src[:, None] == jnp.arange(PERM_ROWS)[None, :]).astype(BF16)


def _perm_f32(p, x):
    hi = x.astype(BF16)
    rest = x - hi.astype(F32)
    mid = rest.astype(BF16)
    lo = (rest - mid.astype(F32)).astype(BF16)
    return (_dot(p, hi) + _dot(p, mid)) + _dot(p, lo)


def _res_spec(d, cols, col_block=0):
    return pl.BlockSpec((d, PERM_ROWS // d, cols), lambda i: (0, i, col_block))


def _perm_specs():
    return [pl.BlockSpec((PERM_ROWS, PERM_ROWS), lambda i: (0, 0))] * (len(DILATIONS) - 1)


def qkv_split(proj, cos2, sin2, perms):
    n = proj.shape[0]
    tm = PERM_ROWS

    def body(p_ref, c_ref, s_ref, *rest):
        perm_refs, outs = rest[:len(DILATIONS) - 1], rest[len(DILATIONS) - 1:]
        c, s = c_ref[...], s_ref[...]
        for h in range(2 * NH):
            t = p_ref[:, h * HD:(h + 1) * HD].astype(F32)
            outs[0][0, :, h * HD:(h + 1) * HD] = (t * c + pltpu.roll(t, HD // 2, 1) * s).astype(BF16)
        outs[0][0, :, 2 * DA:] = p_ref[:, 2 * DA:]
        nat = outs[0][0]
        for o_ref, p_ref_d, d in zip(outs[1:], perm_refs, DILATIONS[1:]):
            o_ref[...] = _dot(p_ref_d[...], nat).astype(BF16).reshape(d, tm // d, 3 * DA)

    tab = pl.BlockSpec((tm, HD), lambda i: (i, 0))
    return _call(body, name="qkv_split", out_shape=tuple(_sds((d, n // d, 3 * DA), BF16) for d in DILATIONS),
                 grid=(n // tm,), in_specs=[pl.BlockSpec((tm, 3 * DA), lambda i: (i, 0)), tab, tab] + _perm_specs(),
                 out_specs=tuple(_res_spec(d, 3 * DA) for d in DILATIONS), sem=("parallel",))(proj, cos2, sin2, *perms)


def dproj_assemble(dqs, dks, dvs, duv, cos2, sin2, perms_t):
    n = duv.shape[0]
    tm = PERM_ROWS
    npat = len(DILATIONS)

    def body(*refs):
        parts = refs[0:npat], refs[npat:2 * npat], refs[2 * npat:3 * npat]
        duv_ref, c_ref, s_ref = refs[3 * npat:3 * npat + 3]
        pt_refs, o_ref = refs[3 * npat + 3:-1], refs[-1]
        c, s = c_ref[...], s_ref[...]
        for part, part_refs in enumerate(parts):
            t = part_refs[0][0].astype(F32)
            for ref, pt_ref in zip(part_refs[1:], pt_refs):
                t = t + _dot(pt_ref[...], ref[...].reshape(tm, DA))
            if part == 2:
                o_ref[:, 2 * DA:3 * DA] = t.astype(BF16)
                continue
            for h in range(NH):
                th = t[:, h * HD:(h + 1) * HD]
                o_ref[:, part * DA + h * HD:part * DA + (h + 1) * HD] = (th * c - pltpu.roll(th, HD // 2, 1) * s).astype(BF16)
        o_ref[:, 3 * DA:] = duv_ref[...]

    blks = [_res_spec(d, DA) for d in DILATIONS]
    tab = pl.BlockSpec((tm, HD), lambda i: (i, 0))
    return _call(body, name="dproj_assemble", out_shape=_sds((n, DIN), BF16), grid=(n // tm,),
                 in_specs=blks * 3 + [pl.BlockSpec((tm, 2 * DG), lambda i: (i, 0)), tab, tab] + _perm_specs(),
                 out_specs=pl.BlockSpec((tm, DIN), lambda i: (i, 0)), sem=("parallel",))(
        *dqs, *dks, *dvs, duv, cos2, sin2, *perms_t)


ATTN_BLOCK_BYTES = 2 << 20


def _attn_tiles(ls):
    tq = min(128, ls)
    w = min(tq + 2 * NSIDE, ls)
    return tq, w, ls // tq


def _attn_heads(ls):
    return NH if ls * DA * 2 <= ATTN_BLOCK_BYTES else 1


ATTN_UNROLL = 8


def _attn_loop(nt, hb, tile):
    for h in range(hb):
        def trip(t, carry, h=h):
            tile(t, h)
            return carry

        lax.fori_loop(0, nt, trip, 0, unroll=min(nt, ATTN_UNROLL))


def _band(t, tq, w, ls):
    q0 = pl.multiple_of(t * tq, tq)
    ks = pl.multiple_of(jnp.clip(t * tq - NSIDE, 0, ls - w), NSIDE)
    qpos = q0 + lax.broadcasted_iota(jnp.int32, (tq, w), 0)
    kpos = ks + lax.broadcasted_iota(jnp.int32, (tq, w), 1)
    return q0, ks, jnp.abs(kpos - qpos) <= NSIDE


def _attn_specs(ls, hb):
    nhb = NH // hb
    q = pl.BlockSpec((None, ls, hb * HD), lambda g: (g // nhb, 0, g % nhb))
    k = pl.BlockSpec((None, ls, hb * HD), lambda g: (g // nhb, 0, nhb + g % nhb))
    v = pl.BlockSpec((None, ls, hb * HD), lambda g: (g // nhb, 0, 2 * nhb + g % nhb))
    return q, k, v, q


def _lse_spec(ls, hb):
    nhb = NH // hb
    return pl.BlockSpec((None, ls, LANES), lambda g: (g // nhb, 0, 0))


def attn_fwd(qkv):
    d, ls, _ = qkv.shape
    tq, w, nt = _attn_tiles(ls)
    hb = _attn_heads(ls)
    nhb = NH // hb

    def body(q_ref, k_ref, v_ref, o_ref, l_ref):
        first = (pl.program_id(0) % nhb) * hb
        lane = lax.broadcasted_iota(jnp.int32, (tq, LANES), 1)

        @pl.when(first == 0)
        def _():
            l_ref[...] = jnp.zeros_like(l_ref)

        def tile(t, h):
            q0, ks, valid = _band(t, tq, w, ls)
            cols = slice(h * HD, (h + 1) * HD)
            s = _dot_nt(q_ref[pl.ds(q0, tq), cols], k_ref[pl.ds(ks, w), cols]) * SCALE
            s = jnp.where(valid, s, NEG)
            m = jnp.max(s, axis=1, keepdims=True)
            p = jnp.exp(s - m)
            l = jnp.sum(p, axis=1, keepdims=True)
            o = _dot(p.astype(BF16), v_ref[pl.ds(ks, w), cols]) / l
            o_ref[pl.ds(q0, tq), cols] = o.astype(BF16)
            l_ref[pl.ds(q0, tq), :] = jnp.where(lane == first + h, m + jnp.log(l), l_ref[pl.ds(q0, tq), :])

        _attn_loop(nt, hb, tile)

    q, k, v, o = _attn_specs(ls, hb)
    return _call(body, name="attn_fwd_d%d" % d, out_shape=(_sds((d, ls, DA), BF16), _sds((d, ls, LANES), F32)),
                 grid=(d * NH // hb,), in_specs=[q, k, v], out_specs=(o, _lse_spec(ls, hb)), sem=("arbitrary",))(
        qkv, qkv, qkv)


def attn_merge(os_, ls_, ga, perms_t):
    n = os_[0].shape[1]
    tm = PERM_ROWS
    npat = len(DILATIONS)

    def body(*refs):
        o_refs, l_refs = refs[:npat], refs[npat:2 * npat]
        g_ref, pt_refs = refs[2 * npat], refs[2 * npat + 1:3 * npat]
        a_ref, lse_ref, mix_ref = refs[3 * npat:]
        ov = [o_refs[0][0].astype(F32)] + [_dot(pt[...], o[...].reshape(tm, DA)) for o, pt in zip(o_refs[1:], pt_refs)]
        lv = [l_refs[0][0]] + [_perm_f32(pt[...], l[...].reshape(tm, LANES)) for l, pt in zip(l_refs[1:], pt_refs)]
        lmax = functools.reduce(jnp.maximum, lv)
        e = [jnp.exp(x - lmax) for x in lv]
        den = functools.reduce(lambda p, q: p + q, e)
        wts = [ei / den for ei in e]
        lse_ref[...] = lmax + jnp.log(den)
        heads = []
        for h in range(NH):
            cols = slice(h * HD, (h + 1) * HD)
            heads.append(functools.reduce(lambda p, q: p + q, [wt[:, h:h + 1] * oi[:, cols] for wt, oi in zip(wts, ov)]))
        a = jnp.concatenate(heads, axis=1)
        a_ref[...] = a.astype(BF16)
        r = lax.rsqrt(jnp.mean(a * a, axis=-1, keepdims=True) + EPS)
        mix_ref[...] = (a * r * g_ref[...]).astype(BF16)

    blk = pl.BlockSpec((tm, DA), lambda i: (i, 0))
    return _call(body, name="attn_merge", out_shape=(_sds((n, DA), BF16), _sds((n, LANES), F32), _sds((n, D), BF16)),
                 grid=(n // tm,),
                 in_specs=[_res_spec(d, DA) for d in DILATIONS] + [_res_spec(d, LANES) for d in DILATIONS]
                 + [pl.BlockSpec((1, DA), lambda i: (0, 0))] + _perm_specs(),
                 out_specs=(blk, pl.BlockSpec((tm, LANES), lambda i: (i, 0)), blk), sem=("parallel",))(
        *os_, *ls_, ga, *perms_t)


def attn_norm_bwd(dmix, a, lse, ga, perms):
    n = a.shape[0]
    tm = PERM_ROWS
    npat = len(DILATIONS)

    def body(dm_ref, a_ref, lse_ref, g_ref, *rest):
        p_refs = rest[:npat - 1]
        da_refs, a_refs, lse_refs = rest[npat - 1:2 * npat - 1], rest[2 * npat - 1:3 * npat - 2], rest[3 * npat - 2:4 * npat - 3]
        dg_ref = rest[-1]

        @pl.when(pl.program_id(0) == 0)
        def _():
            dg_ref[...] = jnp.zeros_like(dg_ref)

        ab = a_ref[...]
        av = ab.astype(F32)
        r = lax.rsqrt(jnp.mean(av * av, axis=-1, keepdims=True) + EPS)
        ahat = av * r
        dm = dm_ref[...].astype(F32)
        dg_ref[...] += jnp.sum(dm * ahat, axis=0, keepdims=True)
        dn = dm * g_ref[...]
        da = (r * (dn - ahat * jnp.mean(dn * ahat, axis=-1, keepdims=True))).astype(BF16)
        da_refs[0][0] = da
        lv = lse_ref[...]
        for i, d in enumerate(DILATIONS[1:]):
            p = p_refs[i][...]
            da_refs[i + 1][...] = _dot(p, da).astype(BF16).reshape(d, tm // d, DA)
            a_refs[i][...] = _dot(p, ab).astype(BF16).reshape(d, tm // d, DA)
            lse_refs[i][...] = _perm_f32(p, lv).reshape(d, tm // d, LANES)

    blk = pl.BlockSpec((tm, DA), lambda i: (i, 0))
    vec = pl.BlockSpec((1, DA), lambda i: (0, 0))
    res = [_res_spec(d, DA) for d in DILATIONS]
    outs = _call(body, name="attn_norm_bwd",
                 out_shape=tuple([_sds((d, n // d, DA), BF16) for d in DILATIONS]
                                 + [_sds((d, n // d, DA), BF16) for d in DILATIONS[1:]]
                                 + [_sds((d, n // d, LANES), F32) for d in DILATIONS[1:]] + [_sds((1, DA), F32)]),
                 grid=(n // tm,),
                 in_specs=[blk, blk, pl.BlockSpec((tm, LANES), lambda i: (i, 0)), vec] + _perm_specs(),
                 out_specs=tuple(res + res[1:] + [_res_spec(d, LANES) for d in DILATIONS[1:]] + [vec]),
                 sem=("arbitrary",))(dmix, a, lse, ga, *perms)
    return outs[:npat], outs[npat:2 * npat - 1], outs[2 * npat - 1:3 * npat - 2], outs[-1]


def attn_bwd(qkv, a, da, lse):
    d, ls, _ = qkv.shape
    tq, w, nt = _attn_tiles(ls)
    hb = _attn_heads(ls)

    nhb = NH // hb

    def body(q_ref, k_ref, v_ref, a_ref, da_ref, lse_ref, dq_ref, dk_ref, dv_ref, dk_acc, dv_acc):
        first = (pl.program_id(0) % nhb) * hb
        lane = lax.broadcasted_iota(jnp.int32, (tq, LANES), 1)
        dk_acc[...] = jnp.zeros_like(dk_acc)
        dv_acc[...] = jnp.zeros_like(dv_acc)

        def tile(t, h):
            q0, ks, valid = _band(t, tq, w, ls)
            cols = slice(h * HD, (h + 1) * HD)
            q = q_ref[pl.ds(q0, tq), cols]
            k = k_ref[pl.ds(ks, w), cols]
            v = v_ref[pl.ds(ks, w), cols]
            do = da_ref[pl.ds(q0, tq), cols]
            if nhb == 1:
                lse_h = lse_ref[pl.ds(q0, tq), h:h + 1]
            else:
                lse_h = jnp.sum(jnp.where(lane == first + h, lse_ref[pl.ds(q0, tq), :], 0.0), axis=1, keepdims=True)
            s = jnp.where(valid, _dot_nt(q, k) * SCALE, NEG)
            p = jnp.exp(s - lse_h)
            drow = jnp.sum(do.astype(F32) * a_ref[pl.ds(q0, tq), cols].astype(F32), axis=1, keepdims=True)
            ds = (p * (_dot_nt(do, v) - drow) * SCALE).astype(BF16)
            dv_acc[pl.ds(ks, w), cols] += _dot_tn(p.astype(BF16), do)
            dk_acc[pl.ds(ks, w), cols] += _dot_tn(ds, q)
            dq_ref[pl.ds(q0, tq), cols] = _dot(ds, k).astype(BF16)

        _attn_loop(nt, hb, tile)
        dk_ref[...] = dk_acc[...].astype(BF16)
        dv_ref[...] = dv_acc[...].astype(BF16)

    q, k, v, o = _attn_specs(ls, hb)
    out = _sds((d, ls, DA), BF16)
    return _call(body, name="attn_bwd_d%d" % d, out_shape=(out, out, out), grid=(d * NH // hb,),
                 in_specs=[q, k, v, o, o, _lse_spec(ls, hb)], out_specs=(o, o, o),
                 scratch=[pltpu.VMEM((ls, hb * HD), F32), pltpu.VMEM((ls, hb * HD), F32)], sem=("parallel",))(
        qkv, qkv, qkv, a, da, lse)


GM_TM = 256
INV_SQRT2 = 0.7071067811865476
INV_SQRT2PI = 0.3989422804014327


def _gelu(x):
    return 0.5 * x * (1.0 + lax.erf(x * INV_SQRT2))


def _gelu_grad(x):
    return 0.5 * (1.0 + lax.erf(x * INV_SQRT2)) + x * (INV_SQRT2PI * jnp.exp(-0.5 * x * x))


def _gmlp_forward(up, vp, ln_g, ws_ref, bias):
    u = _gelu(up)
    v = _gelu(vp)
    vc = v - jnp.mean(v, axis=-1, keepdims=True)
    rs = lax.rsqrt(jnp.mean(vc * vc, axis=-1, keepdims=True) + EPS)
    vhat = vc * rs
    vln = (vhat * ln_g).astype(BF16)
    rows = []
    for c in range(GM_TM // CH):
        cols = [_dot(ws_ref[g], vln[c * CH:(c + 1) * CH, g * 128:(g + 1) * 128]) for g in range(NG)]
        rows.append(jnp.concatenate(cols, axis=1) + bias)
    return u, vhat, rs, vln, jnp.concatenate(rows, axis=0)


def _gmlp_specs():
    ublk = pl.BlockSpec((GM_TM, DG), lambda i: (i, 3 * DA // DG))
    vblk = pl.BlockSpec((GM_TM, DG), lambda i: (i, 3 * DA // DG + 1))
    vec = pl.BlockSpec((1, DG), lambda i: (0, 0))
    wsp = pl.BlockSpec((NG, CH, CH), lambda i: (0, 0, 0))
    bsp = pl.BlockSpec((CH, DG), lambda i: (0, 0))
    return ublk, vblk, vec, wsp, bsp


def gmlp_fwd(proj, ln_g, ws, bias, gg, mix):
    n = proj.shape[0]

    def body(up_ref, vp_ref, ln_ref, ws_ref, b_ref, gg_ref, mix_in, mix_ref):
        del mix_in
        u, _, _, _, mixed = _gmlp_forward(up_ref[...].astype(F32), vp_ref[...].astype(F32), ln_ref[...], ws_ref, b_ref[...])
        gout = u * mixed
        r = lax.rsqrt(jnp.mean(gout * gout, axis=-1, keepdims=True) + EPS)
        mix_ref[...] = (gout * r * gg_ref[...]).astype(BF16)

    ublk, vblk, vec, wsp, bsp = _gmlp_specs()
    return _call(body, name="gmlp_fwd", out_shape=_sds((n, D), BF16), grid=(n // GM_TM,),
                 in_specs=[ublk, vblk, vec, wsp, bsp, vec, pl.BlockSpec(memory_space=pl.ANY)],
                 out_specs=pl.BlockSpec((GM_TM, DG), lambda i: (i, DA // DG)), sem=("parallel",), aliases={6: 0})(
        proj, proj, ln_g, ws, bias, gg, mix)


def gmlp_bwd(proj, dmix, ln_g, ws, wst, bias, gg):
    n = proj.shape[0]

    def body(up_ref, vp_ref, dm_ref, ln_ref, ws_ref, wst_ref, b_ref, gg_ref, duv_ref, dln_ref, dws_ref, dbs_ref, dgg_ref,
             db_ref):
        @pl.when(pl.program_id(0) == 0)
        def _():
            dln_ref[...] = jnp.zeros_like(dln_ref)
            dws_ref[...] = jnp.zeros_like(dws_ref)
            db_ref[...] = jnp.zeros_like(db_ref)
            dgg_ref[...] = jnp.zeros_like(dgg_ref)

        up = up_ref[...].astype(F32)
        vp = vp_ref[...].astype(F32)
        ln_g = ln_ref[...]
        u, vhat, rs, vln, mixed = _gmlp_forward(up, vp, ln_g, ws_ref, b_ref[...])
        gout = u * mixed
        r = lax.rsqrt(jnp.mean(gout * gout, axis=-1, keepdims=True) + EPS)
        ghat = gout * r
        dm = dm_ref[...].astype(F32)
        dgg_ref[...] += jnp.sum(dm * ghat, axis=0, keepdims=True)
        dn = dm * gg_ref[...]
        dgout = r * (dn - ghat * jnp.mean(dn * ghat, axis=-1, keepdims=True))
        du = dgout * mixed
        dmixed = dgout * u
        dmb = dmixed.astype(BF16)
        rows = []
        for c in range(GM_TM // CH):
            rsl = slice(c * CH, (c + 1) * CH)
            db_ref[...] += dmixed[rsl, :]
            cols = []
            for g in range(NG):
                csl = slice(g * 128, (g + 1) * 128)
                dws_ref[g] += _dot_nt(dmb[rsl, csl], vln[rsl, csl])
                cols.append(_dot(wst_ref[g], dmb[rsl, csl]))
            rows.append(jnp.concatenate(cols, axis=1))
        dvln = jnp.concatenate(rows, axis=0)
        dln_ref[...] += jnp.sum(dvln * vhat, axis=0, keepdims=True)
        dvh = dvln * ln_g
        dv = rs * (dvh - jnp.mean(dvh, axis=-1, keepdims=True) - vhat * jnp.mean(dvh * vhat, axis=-1, keepdims=True))
        duv_ref[:, 0:DG] = (du * _gelu_grad(up)).astype(BF16)
        duv_ref[:, DG:] = (dv * _gelu_grad(vp)).astype(BF16)

        @pl.when(pl.program_id(0) == pl.num_programs(0) - 1)
        def _():
            for g in range(NG):
                dbs_ref[g:g + 1, :] = jnp.sum(jnp.transpose(db_ref[:, g * 128:(g + 1) * 128]), axis=0, keepdims=True)

    ublk, vblk, vec, wsp, bsp = _gmlp_specs()
    return _call(body, name="gmlp_bwd",
                 out_shape=(_sds((n, 2 * DG), BF16), _sds((1, DG), F32), _sds((NG, CH, CH), F32), _sds((NG, CH), F32),
                            _sds((1, DG), F32)),
                 grid=(n // GM_TM,),
                 in_specs=[ublk, vblk, pl.BlockSpec((GM_TM, DG), lambda i: (i, DA // DG)), vec, wsp, wsp, bsp, vec],
                 out_specs=(pl.BlockSpec((GM_TM, 2 * DG), lambda i: (i, 0)), vec, wsp,
                            pl.BlockSpec((NG, CH), lambda i: (0, 0)), vec),
                 scratch=[pltpu.VMEM((CH, DG), F32)], sem=("arbitrary",))(
        proj, proj, dmix, ln_g, ws, wst, bias, gg)


def _row(v):
    return v.reshape(1, -1)


def local_step(x, target, small, big):
    tables = step_tables()
    saved = []
    for l in range(NL):
        x_mid, sv = mixer_fwd(x, small, l, big[l], tables)
        x, sv2 = ffn_fwd(x_mid, small, l, big[l])
        saved.append({**sv, **sv2})
    loss8, dx, dxb, dfinal = loss_head(x, _row(small["final_g"]), target)
    gs = {k: [None] * NL for k in SMALL[:-1]}
    gbig = [None] * NL
    for l in reversed(range(NL)):
        dx, dxb, gs1, gb1 = ffn_bwd(dx, dxb, small, l, big[l], saved[l])
        dx, dxb, gs2, gb2 = mixer_bwd(dx, dxb, small, l, big[l], saved[l], tables)
        gbig[l] = {**gb1, **gb2}
        for k, g in {**gs1, **gs2}.items():
            gs[k][l] = g
    return loss8, dx, gs, dfinal, gbig


def step_tables():
    perms = [perm_matrix(d) for d in DILATIONS[1:]]
    return rope_tables(T) + (perms, [p.T for p in perms])


def mixer_fwd(x, small, l, w, tables, after=None):
    cos2, sin2, perms, perms_t = tables
    ws_b = small["w_spatial"][l].astype(BF16)
    bias = jnp.repeat(small["b_spatial"][l].T, 128, axis=1)
    h = rms_fwd(x, _row(small["norm1_g"][l]), after)
    proj = mm_cols(h, w["win"])
    qkvs = qkv_split(proj, cos2, sin2, perms)
    outs = [attn_fwd(qkv) for qkv in qkvs]
    a, lse, mix = attn_merge([o for o, _ in outs], [s for _, s in outs], _row(small["mix_norm_attn_g"][l]), perms_t)
    mix = gmlp_fwd(proj, _row(small["gmlp_ln_g"][l]), ws_b, bias, _row(small["mix_norm_gmlp_g"][l]), mix)
    x_mid = mm_rows_res(mix, w["wout"], x)
    return x_mid, dict(x=x, h=h, proj=proj, qkvs=qkvs, a=a, lse=lse, mix=mix, ws_b=ws_b, bias=bias)


def ffn_fwd(x_mid, small, l, w, after=None):
    h2 = rms_fwd(x_mid, _row(small["norm2_g"][l]), after)
    gate, up, ff = mm_gateup(h2, w["wg"], w["wu"])
    x_out = mm_rows_res(ff, w["wd"], x_mid)
    return x_out, dict(x_mid=x_mid, h2=h2, gate=gate, up=up, ff=ff)


def ffn_bwd(dx, dxb, small, l, w, sv, after=None):
    gs = {}
    dgate, dup = mm_dff(dxb, w["wd"], sv["gate"], sv["up"], tk=FF // NCHIP, after=after)
    g_wd = mm_tn(sv["ff"], dxb, D // 2, False)
    dh2 = mm_nt_cols([dgate, dup], [w["wg"], w["wu"]])
    g_wg = mm_tn(sv["h2"], dgate, FF // NCHIP, True)
    g_wu = mm_tn(sv["h2"], dup, FF // NCHIP, True)
    dx, dxb, gs["norm2_g"] = rms_bwd(dh2, sv["x_mid"], _row(small["norm2_g"][l]), dx)
    return dx, dxb, gs, dict(wg=g_wg, wu=g_wu, wd=g_wd.reshape(NCHIP, FF // NCHIP, D))


def mixer_bwd(dx, dxb, small, l, w, sv, tables, after=None):
    cos2, sin2, perms, perms_t = tables
    gs = {}
    dmix = mm_nt_rows(dxb, w["wout"], tk=D // 2, after=after)
    g_wout = mm_tn(sv["mix"], dxb, D // 2, False)
    das, a_res, lse_res, gs["mix_norm_attn_g"] = attn_norm_bwd(dmix, sv["a"], sv["lse"],
                                                                _row(small["mix_norm_attn_g"][l]), perms)
    a_all = [sv["a"].reshape((1,) + sv["a"].shape)] + list(a_res)
    lse_all = [sv["lse"].reshape((1,) + sv["lse"].shape)] + list(lse_res)
    parts = [attn_bwd(*operands) for operands in zip(sv["qkvs"], a_all, das, lse_all)]
    wst = jnp.swapaxes(small["w_spatial"][l], 1, 2).astype(BF16)
    duv, gs["gmlp_ln_g"], gs["w_spatial"], gs["b_spatial"], gs["mix_norm_gmlp_g"] = gmlp_bwd(
        sv["proj"], dmix, _row(small["gmlp_ln_g"][l]), sv["ws_b"], wst, sv["bias"], _row(small["mix_norm_gmlp_g"][l]))
    dproj = dproj_assemble([p[0] for p in parts], [p[1] for p in parts], [p[2] for p in parts], duv, cos2, sin2,
                           perms_t)
    dh = mm_nt_cols([dproj], [w["win"]])
    g_win = mm_tn(sv["h"], dproj, DIN // NCHIP, True)
    dx, dxb, gs["norm1_g"] = rms_bwd(dh, sv["x"], _row(small["norm1_g"][l]), dx)
    return dx, dxb, gs, dict(win=g_win, wout=g_wout.reshape(NCHIP, D // NCHIP, D))


KINDS = ("win", "wout", "wg", "wu", "wd")
GROUPS = (("win", "wout"), ("wg", "wu", "wd"))
ANY = pl.BlockSpec(memory_space=pl.ANY)


def _place():
    return lax.axis_index("x"), lax.axis_index("y"), lax.axis_index("c")


def _other_chips(x, y):
    return [(1 - x, y), (x, 1 - y), (1 - x, 1 - y)]


def _remote(src, dst, send_sem, recv_sem, to):
    return pltpu.make_async_remote_copy(src_ref=src, dst_ref=dst, send_sem=send_sem, recv_sem=recv_sem,
                                        device_id=to, device_id_type=MESH)


def _hbm_call(body, *, name, n_in, out_shape, scratch, in_place=False):
    return pl.pallas_call(body, name=name, out_shape=tuple(out_shape), in_specs=[ANY] * n_in,
                          out_specs=tuple(ANY for _ in out_shape), scratch_shapes=list(scratch),
                          input_output_aliases={k: k for k in range(n_in)} if in_place else {},
                          compiler_params=pltpu.CompilerParams(vmem_limit_bytes=VMEM_LIMIT))


def place_vector():
    x, y, c = _place()
    return jnp.stack([c, 2 * x + y] + [2 * cx + cy for cx, cy in _other_chips(x, y)]).astype(jnp.int32)


PUSH_ROWS = 512
PUSH_SLOTS = 3


def _push_tiles(src, dst, buf, load_sem, send_sem, recv_sem, to, src_at, dst_at, nt):
    tr = buf.shape[1]
    n = len(src_at) * nt

    def pick(vals, seg):
        out = vals[0]
        for q in range(1, len(vals)):
            out = jnp.where(seg == q, vals[q], out)
        return out

    def tile(ref, at, t):
        seg = t // nt
        row = pl.multiple_of(pick([a[1] for a in at], seg) + (t - seg * nt) * tr, 16)
        return ref.at[pick([a[0] for a in at], seg), pl.ds(row, tr), :]

    def load(t):
        slot = t % PUSH_SLOTS
        return pltpu.make_async_copy(tile(src, src_at, t), buf.at[slot], load_sem.at[slot])

    def send(t):
        slot = t % PUSH_SLOTS
        return _remote(buf.at[slot], tile(dst, dst_at, t), send_sem.at[slot], recv_sem, to)

    load(0).start()

    def step(t, carry):
        load(t).wait()
        send(t).start()

        @pl.when(t >= PUSH_SLOTS - 1)
        def _():
            send(t - (PUSH_SLOTS - 1)).wait_send()

        @pl.when(t + 1 < n)
        def _():
            load(t + 1).start()

        return carry

    lax.fori_loop(0, n, step, 0)
    for t in range(max(n - (PUSH_SLOTS - 1), 0), n):
        send(t).wait_send()


def _await_tiles(dst, send_sem, recv_sem, to, nseg, rows):
    whole = dst.at[pl.ds(0, nseg), pl.ds(0, rows), :]
    _remote(whole, whole, send_sem.at[0], recv_sem, to).wait_recv()


def _push_rows(seg_rows):
    return _row_tile(seg_rows, PUSH_ROWS)


def _push_scratch(arrs, seg_rows):
    return ([pltpu.VMEM((PUSH_SLOTS, _push_rows(r), a.shape[-1]), a.dtype) for a, r in zip(arrs, seg_rows)]
            + [pltpu.SemaphoreType.DMA((len(arrs), PUSH_SLOTS)), pltpu.SemaphoreType.DMA((len(arrs), PUSH_SLOTS)),
               pltpu.SemaphoreType.DMA((len(arrs),))])


def cast_into(w, l, pos, after=None):
    _, r, cols = w.shape
    tr = _row_tile(r)
    extra, extra_specs = _behind(after)

    def body(pos_ref, w_ref, *rest):
        del pos_ref
        rest[-1][...] = w_ref[...].astype(BF16)

    return _call(body, name="cast_into", out_shape=_sds((NCHIP, r, cols), BF16), grid=(r // tr,),
                 in_specs=[pl.BlockSpec((None, tr, cols), lambda i, pos: (l, i, 0))] + extra_specs,
                 out_specs=pl.BlockSpec((None, tr, cols), lambda i, pos: (pos[1], i, 0)),
                 sem=("parallel",), prefetch=1)(pos, w, *extra)


HBM = pl.BlockSpec(memory_space=pltpu.HBM)
SEM = pl.BlockSpec(memory_space=pltpu.SEMAPHORE)
EFFECT = pltpu.SideEffectType.DATAFLOW_SIDE_EFFECTING


def _in_hbm(a):
    return pltpu.with_memory_space_constraint(a, pltpu.HBM)


def _gather_copies(bufs, send, recv):
    x, y, c = _place()
    chips = _other_chips(x, y)

    def half(k, chip):
        hr = bufs[k].shape[1] // 2
        return bufs[k].at[chip, pl.ds(c * hr, hr), :]

    out, back = [], []
    for k in range(len(bufs)):
        for j, (cx, cy) in enumerate(chips):
            i = 3 * k + j
            out.append(_remote(half(k, 2 * x + y), half(k, 2 * x + y), send.at[i], recv.at[i], (cx, cy, c)))
            back.append(_remote(half(k, 2 * x + y), half(k, 2 * cx + cy), send.at[i], recv.at[i], (cx, cy, c)))
    return out, back


def gather_start(fulls):
    nk = len(fulls)

    def body(*refs):
        send, recv = refs[nk], refs[nk + 1]
        bufs, token = refs[nk + 2:2 * nk + 2], refs[2 * nk + 2]
        for cp in _gather_copies(bufs, send, recv)[0]:
            cp.start()
        token[...] = jnp.zeros_like(token)

    res = pl.pallas_call(
        body, name="gather_start",
        out_shape=(pltpu.SemaphoreType.DMA((3 * nk,)), pltpu.SemaphoreType.DMA((3 * nk,)),
                   *[pltpu.HBM(f.shape, f.dtype) for f in fulls], _sds((8, LANES), F32)),
        in_specs=[HBM] * nk, out_specs=(SEM, SEM, *[HBM] * nk, pl.BlockSpec(memory_space=pltpu.VMEM)),
        input_output_aliases={k: 2 + k for k in range(nk)},
        compiler_params=pltpu.CompilerParams(has_side_effects=EFFECT))(*[_in_hbm(f) for f in fulls])
    return res[0], res[1], list(res[2:2 + nk]), res[2 + nk]


def gather_wait(send, recv, fulls, after):
    nk = len(fulls)

    def body(*refs):
        bufs, send_ref, recv_ref = refs[:nk], refs[nk], refs[nk + 1]
        for cp in _gather_copies(bufs, send_ref, recv_ref)[1]:
            cp.wait_send()
            cp.wait_recv()

    return list(pl.pallas_call(
        body, name="gather_wait", out_shape=tuple(pltpu.HBM(f.shape, f.dtype) for f in fulls),
        in_specs=[HBM] * nk + [SEM, SEM, ANY], out_specs=tuple([HBM] * nk),
        input_output_aliases={k: k for k in range(nk)},
        compiler_params=pltpu.CompilerParams(has_side_effects=EFFECT))(*fulls, send, recv, after))


def pair_forward(fulls):
    nk = len(fulls)

    def body(*refs):
        bufs = refs[nk:2 * nk]
        stage = refs[2 * nk:3 * nk]
        load_sem, send_sem, recv_sem = refs[3 * nk:]
        x, y, c = _place()
        sib = (x, y, 1 - c)
        chips = [2 * cx + cy for cx, cy in _other_chips(x, y)]
        for k in range(nk):
            hr = bufs[k].shape[1] // 2
            at = [(chip, c * hr) for chip in chips]
            _push_tiles(bufs[k], bufs[k], stage[k], load_sem.at[k], send_sem.at[k], recv_sem.at[k], sib, at, at,
                        hr // _push_rows(hr))
        for k in range(nk):
            _await_tiles(bufs[k], send_sem.at[k], recv_sem.at[k], sib, 3, bufs[k].shape[1] // 2)

    return _hbm_call(body, name="pair_forward", n_in=nk, out_shape=[_sds(f.shape, f.dtype) for f in fulls], in_place=True,
                     scratch=_push_scratch(fulls, [f.shape[1] // 2 for f in fulls]))(*fulls)


N_PEERS = 7
PEER_FLIPS = [(fx, fy, fc) for fx in (0, 1) for fy in (0, 1) for fc in (0, 1)][1:]


def _chip_copies(ps, lands, send, recv):
    x, y, c = _place()
    cps = []
    for k in range(len(ps)):
        hr = ps[k].shape[1] // 2
        for r, (fx, fy, fc) in enumerate(PEER_FLIPS):
            px, py, pc = x ^ fx, y ^ fy, c ^ fc
            cps.append(_remote(ps[k].at[2 * px + py, pl.ds(pc * hr, hr), :], lands[k].at[r],
                               send.at[N_PEERS * k + r], recv.at[N_PEERS * k + r], (px, py, pc)))
    return cps


def chip_exchange_start(ps, after=None):
    nk = len(ps)
    lands = [lax.empty((N_PEERS, p.shape[1] // 2, p.shape[2]), p.dtype) for p in ps]
    extra, extra_specs = _behind(after)

    def body(*refs):
        refs = refs[2 * nk + len(extra):]
        send, recv = refs[0], refs[1]
        srcs, dsts, token = refs[2:nk + 2], refs[nk + 2:2 * nk + 2], refs[2 * nk + 2]
        for cp in _chip_copies(srcs, dsts, send, recv):
            cp.start()
        token[...] = jnp.zeros_like(token)

    res = pl.pallas_call(
        body, name="chip_exchange_start",
        out_shape=(pltpu.SemaphoreType.DMA((N_PEERS * nk,)), pltpu.SemaphoreType.DMA((N_PEERS * nk,)),
                   *[pltpu.HBM(a.shape, a.dtype) for a in ps + lands], _sds((8, LANES), F32)),
        in_specs=[HBM] * (2 * nk) + extra_specs,
        out_specs=(SEM, SEM, *[HBM] * (2 * nk), pl.BlockSpec(memory_space=pltpu.VMEM)),
        input_output_aliases={k: 2 + k for k in range(2 * nk)},
        compiler_params=pltpu.CompilerParams(has_side_effects=EFFECT))(*[_in_hbm(a) for a in ps + lands], *extra)
    return res[0], res[1], list(res[2:2 + nk]), list(res[2 + nk:2 + 2 * nk]), res[2 + 2 * nk]


def chip_exchange_wait(send, recv, ps, lands, after):
    nk = len(ps)

    def body(*refs):
        srcs, dsts, send_ref, recv_ref = refs[:nk], refs[nk:2 * nk], refs[2 * nk], refs[2 * nk + 1]
        for cp in _chip_copies(srcs, dsts, send_ref, recv_ref):
            cp.wait_send()
            cp.wait_recv()

    res = pl.pallas_call(
        body, name="chip_exchange_wait", out_shape=tuple(pltpu.HBM(a.shape, a.dtype) for a in ps + lands),
        in_specs=[HBM] * (2 * nk) + [SEM, SEM, ANY], out_specs=tuple([HBM] * (2 * nk)),
        input_output_aliases={k: k for k in range(2 * nk)},
        compiler_params=pltpu.CompilerParams(has_side_effects=EFFECT))(*ps, *lands, send, recv, after)
    return list(res[:nk]), list(res[nk:])


def rs_pair_share(fulls):
    nk = len(fulls)

    def body(*refs):
        bufs = refs[nk:2 * nk]
        stage = refs[2 * nk:3 * nk]
        load_sem, send_sem, recv_sem = refs[3 * nk:]
        x, y, c = _place()
        sib = (x, y, 1 - c)
        for k in range(nk):
            hr = bufs[k].shape[1]
            _push_tiles(bufs[k], bufs[k], stage[k], load_sem.at[k], send_sem.at[k], recv_sem.at[k], sib,
                        [(c, 0)], [(c, 0)], hr // _push_rows(hr))
        for k in range(nk):
            _await_tiles(bufs[k], send_sem.at[k], recv_sem.at[k], sib, 1, bufs[k].shape[1])

    return _hbm_call(body, name="rs_pair_share", n_in=nk, out_shape=[_sds(f.shape, f.dtype) for f in fulls], in_place=True,
                     scratch=_push_scratch(fulls, [f.shape[1] for f in fulls]))(*fulls)


N_DEV = 8


def allreduce_small(buf, after=None):
    rows = buf.shape[0]
    rp = rows // N_DEV
    extra, extra_specs = _behind(after)

    def body(in_ref, *rest):
        out_ref, land_ref, send, recv = rest[-4:]
        x, y, c = _place()
        mine = pl.ds(pl.multiple_of((4 * x + 2 * y + c) * rp, 8), rp)
        peers = [(x ^ fx, y ^ fy, c ^ fc) for fx in (0, 1) for fy in (0, 1) for fc in (0, 1)][1:]
        block = [pl.ds(pl.multiple_of((4 * px + 2 * py + pc) * rp, 8), rp) for px, py, pc in peers]
        scatter = [_remote(in_ref.at[block[r], :], land_ref.at[r], send.at[r], recv.at[r], peers[r]) for r in range(7)]
        for cp in scatter:
            cp.start()
        for cp in scatter:
            cp.wait()
        acc = in_ref[mine, :]
        for r in range(7):
            acc = acc + land_ref[r]
        out_ref[mine, :] = acc
        spread = [_remote(out_ref.at[mine, :], out_ref.at[mine, :], send.at[7 + r], recv.at[7 + r], peers[r]) for r in range(7)]
        for cp in spread:
            cp.start()
        for r in range(7):
            _remote(out_ref.at[block[r], :], out_ref.at[block[r], :], send.at[7 + r], recv.at[7 + r], peers[r]).wait_recv()
        for cp in spread:
            cp.wait_send()

    vm = pl.BlockSpec(memory_space=pltpu.VMEM)
    return pl.pallas_call(body, name="allreduce_small", out_shape=_sds(buf.shape, F32), in_specs=[vm] + extra_specs,
                          out_specs=vm,
                          scratch_shapes=[pltpu.VMEM((7, rp, LANES), F32), pltpu.SemaphoreType.DMA((14,)),
                                          pltpu.SemaphoreType.DMA((14,))])(buf, *extra)


def _row_tile(rows, cap=512):
    return max(t for t in range(16, cap + 1, 16) if rows % t == 0)


def add_slots(g, got, pos):
    _, r, cols = g.shape
    hr = r // 2
    tr = _row_tile(hr, 256)
    nt = hr // tr

    def body(pos_ref, o_ref, *rest):
        del pos_ref
        acc = o_ref[...].astype(F32)
        for ref in rest[:N_PEERS]:
            acc = acc + ref[...].astype(F32)
        rest[N_PEERS][...] = acc

    slots = [pl.BlockSpec((None, tr, cols), functools.partial(lambda j, i, pos: (j, i, 0), j)) for j in range(N_PEERS)]
    return _call(body, name="add_slots", out_shape=_sds((2, hr, cols), F32), grid=(nt,),
                 in_specs=[pl.BlockSpec((None, tr, cols), lambda i, pos: (pos[1], pos[0] * nt + i, 0))] + slots,
                 out_specs=pl.BlockSpec((None, tr, cols), lambda i, pos: (pos[0], i, 0)),
                 sem=("parallel",), prefetch=1)(pos, g, *[got] * N_PEERS)


def _adamw_math(w, g, m, v):
    nm = B1 * m + (1.0 - B1) * g
    nv = B2 * v + (1.0 - B2) * (g * g)
    m_hat = nm / (1.0 - B1 ** STEP)
    v_hat = nv / (1.0 - B2 ** STEP)
    return -LR * (m_hat / (jnp.sqrt(v_hat) + AEPS) + WD * w), nm, nv


def adamw_layer(w, g, m, v, l, prev):
    nl, r, cols = w.shape
    tr = _row_tile(r, 256)

    def body(w_ref, g_ref, m_ref, v_ref, *rest):
        go_ref, d_ref, nm_ref, nv_ref = rest[-4:]
        gv = g_ref[...]
        d, nm, nv = _adamw_math(w_ref[...], gv, m_ref[...], v_ref[...])
        go_ref[...] = gv
        d_ref[...] = d
        nm_ref[...] = nm
        nv_ref[...] = nv

    lay = pl.BlockSpec((None, tr, cols), lambda i: (l, i, 0))
    o = _sds((nl, r, cols), F32)
    extra = [] if prev is None else list(prev)
    return _call(body, name="adamw_layer", out_shape=(o, o, o, o), grid=(r // tr,),
                 in_specs=[lay, pl.BlockSpec((tr, cols), lambda i: (i, 0)), lay, lay] + [ANY] * len(extra),
                 out_specs=(lay, lay, lay, lay), sem=("parallel",),
                 aliases={4 + j: j for j in range(len(extra))})(w, g, m, v, *extra)


def adamw(w, g, m, v):
    shape = w.shape
    cols = shape[-1]
    rows = w.size // cols
    tr = _row_tile(rows, 256)

    def body(w_ref, g_ref, m_ref, v_ref, d_ref, nm_ref, nv_ref):
        d_ref[...], nm_ref[...], nv_ref[...] = _adamw_math(w_ref[...], g_ref[...], m_ref[...], v_ref[...])

    blk = pl.BlockSpec((tr, cols), lambda i: (i, 0))
    o = _sds((rows, cols), F32)
    outs = _call(body, name="adamw", out_shape=(o, o, o), grid=(rows // tr,), in_specs=[blk] * 4, out_specs=(blk, blk, blk),
                 sem=("parallel",))(*[t.reshape(rows, cols) for t in (w, g, m, v)])
    return [t.reshape(shape) for t in outs]


SMALL = ("norm1_g", "gmlp_ln_g", "w_spatial", "b_spatial", "mix_norm_attn_g", "mix_norm_gmlp_g", "norm2_g", "final_g")
WEIGHTS = ("norm1_g", "w_in", "gmlp_ln_g", "w_spatial", "b_spatial", "mix_norm_attn_g", "mix_norm_gmlp_g", "w_out",
           "norm2_g", "w_gate", "w_up", "w_down", "final_g")
BIG = dict(win="w_in", wout="w_out", wg="w_gate", wu="w_up", wd="w_down")


def _pack(parts):
    flat = jnp.concatenate([parts[n].reshape(-1) for n in SMALL])
    rows = -(-flat.shape[0] // (LANES * 8 * N_DEV)) * 8 * N_DEV
    return jnp.pad(flat, (0, rows * LANES - flat.shape[0])).reshape(rows, LANES)


def _unpack(buf, like):
    flat = buf.reshape(-1)
    out, at = {}, 0
    for n in SMALL:
        out[n] = flat[at:at + like[n].size].reshape(like[n].shape)
        at += like[n].size
    return out


def kernel(x, norm1_g, w_in, gmlp_ln_g, w_spatial, b_spatial, mix_norm_attn_g, mix_norm_gmlp_g, w_out, norm2_g, w_gate, w_up, w_down, final_g, loss_target, m_norm1_g, m_w_in, m_gmlp_ln_g, m_w_spatial, m_b_spatial, m_mix_norm_attn_g, m_mix_norm_gmlp_g, m_w_out, m_norm2_g, m_w_gate, m_w_up, m_w_down, m_final_g, v_norm1_g, v_w_in, v_gmlp_ln_g, v_w_spatial, v_b_spatial, v_mix_norm_attn_g, v_mix_norm_gmlp_g, v_w_out, v_norm2_g, v_w_gate, v_w_up, v_w_down, v_final_g):
    w = dict(norm1_g=norm1_g, w_in=w_in, gmlp_ln_g=gmlp_ln_g, w_spatial=w_spatial, b_spatial=b_spatial,
             mix_norm_attn_g=mix_norm_attn_g, mix_norm_gmlp_g=mix_norm_gmlp_g, w_out=w_out, norm2_g=norm2_g,
             w_gate=w_gate, w_up=w_up, w_down=w_down, final_g=final_g)
    m = dict(norm1_g=m_norm1_g, w_in=m_w_in, gmlp_ln_g=m_gmlp_ln_g, w_spatial=m_w_spatial, b_spatial=m_b_spatial,
             mix_norm_attn_g=m_mix_norm_attn_g, mix_norm_gmlp_g=m_mix_norm_gmlp_g, w_out=m_w_out, norm2_g=m_norm2_g,
             w_gate=m_w_gate, w_up=m_w_up, w_down=m_w_down, final_g=m_final_g)
    v = dict(norm1_g=v_norm1_g, w_in=v_w_in, gmlp_ln_g=v_gmlp_ln_g, w_spatial=v_w_spatial, b_spatial=v_b_spatial,
             mix_norm_attn_g=v_mix_norm_attn_g, mix_norm_gmlp_g=v_mix_norm_gmlp_g, w_out=v_w_out, norm2_g=v_norm2_g,
             w_gate=v_w_gate, w_up=v_w_up, w_down=v_w_down, final_g=v_final_g)

    pos = place_vector()
    small = {n: w[n] for n in SMALL}
    tables = step_tables()

    def start_gathers(l, after):
        flights = []
        for kinds in GROUPS:
            flights.append(gather_start([cast_into(w[BIG[k]], l, pos, after) for k in kinds]))
            after = flights[-1][3]
        return flights, after

    def arrive(flight, kinds, after):
        fulls = pair_forward(gather_wait(flight[0], flight[1], flight[2], after))
        big = dict(zip(kinds, fulls))
        if "wout" in big:
            big["wout"] = big["wout"].reshape(D, D)
        if "wd" in big:
            big["wd"] = big["wd"].reshape(FF, D)
        return big

    act = x[0]
    flights, token = start_gathers(0, None)
    big, saved = [], []
    for l in range(NL):
        mine, after_mixer = flights, token
        if l + 1 < NL:
            flights, token = start_gathers(l + 1, act if l else token)
            after_mixer = token
        wl = arrive(mine[0], GROUPS[0], act if l else mine[0][3])
        x_mid, sv = mixer_fwd(act, small, l, wl, tables, after_mixer)
        wl.update(arrive(mine[1], GROUPS[1], x_mid))
        act, sv2 = ffn_fwd(x_mid, small, l, wl)
        big.append(wl)
        saved.append({**sv, **sv2})
    loss8, dx, dxb, dfinal = loss_head(act, _row(small["final_g"]), loss_target[0])
    loss = lax.psum(loss8[0, 0], ("x", "y", "c"))

    grads, delta, new_m, new_v = {}, {}, {}, {}
    stacked = {k: None for k in KINDS}
    gs = {k: [None] * NL for k in SMALL[:-1]}

    def send_off(gbig, kinds, l, after=None):
        return chip_exchange_start([gbig[k] for k in kinds], after) + (kinds, l)

    def finish(flight, after):
        partial_grads, lands = chip_exchange_wait(*flight[:4], after)
        halves = [add_slots(g, got, pos) for g, got in zip(partial_grads, lands)]
        for k, full in zip(flight[5], rs_pair_share(halves)):
            n = BIG[k]
            stacked[k] = adamw_layer(w[n], full.reshape(w[n].shape[1:]), m[n], v[n], flight[6], stacked[k])
        return stacked[flight[5][-1]][1]

    pending, token = [], None
    for l in reversed(range(NL)):
        dx, dxb, gs1, gbig = ffn_bwd(dx, dxb, small, l, big[l], saved[l], token)
        for flight in pending:
            finish(flight, dx)
        ffn_flight = send_off(gbig, GROUPS[1], l)
        dx, dxb, gs2, gbig = mixer_bwd(dx, dxb, small, l, big[l], saved[l], tables, ffn_flight[4])
        for k, g in {**gs1, **gs2}.items():
            gs[k][l] = g
        if l:
            mixer_flight = send_off(gbig, GROUPS[0], l)
            pending, token = [ffn_flight, mixer_flight], mixer_flight[4]
    part = {n: jnp.stack(gs[n]).reshape(w[n].shape) for n in SMALL[:-1]}
    part["final_g"] = dfinal.reshape(w["final_g"].shape)
    small_sum = allreduce_small(_pack(part))
    mixer_flight = send_off(gbig, GROUPS[0], 0, small_sum)
    token = mixer_flight[4]
    for flight in (ffn_flight, mixer_flight):
        token = finish(flight, token)
    for k in KINDS:
        grads[BIG[k]], delta[BIG[k]], new_m[BIG[k]], new_v[BIG[k]] = stacked[k]
    grads.update(_unpack(small_sum, part))

    sm = adamw(_pack({n: w[n] for n in SMALL}), _pack({n: grads[n] for n in SMALL}), _pack({n: m[n] for n in SMALL}),
               _pack({n: v[n] for n in SMALL}))
    for res, packed in zip((delta, new_m, new_v), sm):
        res.update(_unpack(packed, part))

    return (loss, dx.reshape(x.shape), *[grads[n] for n in WEIGHTS], *[delta[n] for n in WEIGHTS],
            *[new_m[n] for n in WEIGHTS], *[new_v[n] for n in WEIGHTS])
```

```python
import functools

import jax
import jax.numpy as jnp
from jax import lax
from jax.experimental import pallas as pl
from jax.experimental.pallas import tpu as pltpu

F32 = jnp.float32
BF16 = jnp.bfloat16

D = 2048
T = 4096
NL = 4
HD = 128
DA = D // 2
DG = D - DA
NH = DA // HD
NG = DG // 128
CH = 128
DIN = 3 * DA + 2 * DG
FF = 5632
DILATIONS = (1, 4, 16)
NSIDE = 64
ROPE_THETA = 10000.0
EPS = 1e-6
NEG = -1e30
SCALE = HD ** -0.5
NCHIP = 4

LR, B1, B2, AEPS, WD, STEP = 0.001, 0.9, 0.999, 1e-08, 0.01, 10

LANES = 128
VMEM_LIMIT = 56 * 1024 * 1024
MESH = pl.DeviceIdType.MESH


def _call(body, *, name, out_shape, grid=(), in_specs=None, out_specs=None, scratch=(), sem=None, aliases=None,
          prefetch=0):
    params = dict(vmem_limit_bytes=VMEM_LIMIT)
    if sem is not None:
        params["dimension_semantics"] = sem
    if prefetch:
        spec = pltpu.PrefetchScalarGridSpec(num_scalar_prefetch=prefetch, grid=grid, in_specs=in_specs,
                                            out_specs=out_specs, scratch_shapes=list(scratch))
        return pl.pallas_call(body, name=name, out_shape=out_shape, grid_spec=spec,
                              input_output_aliases=aliases or {}, compiler_params=pltpu.CompilerParams(**params))
    kw = {}
    if grid:
        kw["grid"] = grid
    return pl.pallas_call(
        body, name=name, out_shape=out_shape, in_specs=in_specs, out_specs=out_specs,
        scratch_shapes=list(scratch), input_output_aliases=aliases or {},
        compiler_params=pltpu.CompilerParams(**params), **kw)


def _sds(shape, dtype):
    return jax.ShapeDtypeStruct(tuple(shape), dtype)


def _dot(a, b):
    return jnp.dot(a, b, preferred_element_type=F32)


def _dot_nt(a, b):
    return lax.dot_general(a, b, (((1,), (1,)), ((), ())), preferred_element_type=F32)


def _dot_tn(a, b):
    return lax.dot_general(a, b, (((0,), (0,)), ((), ())), preferred_element_type=F32)


def _sigmoid(x):
    return 0.5 * jnp.tanh(0.5 * x) + 0.5


def _behind(after):
    return ([], []) if after is None else ([after], [pl.BlockSpec(memory_space=pl.ANY)])


def rms_fwd(x, g, after=None):
    n, d = x.shape
    tm = 256
    extra, extra_specs = _behind(after)

    def body(x_ref, g_ref, *rest):
        h_ref = rest[-1]
        xv = x_ref[...]
        r = lax.rsqrt(jnp.mean(xv * xv, axis=-1, keepdims=True) + EPS)
        h_ref[...] = (xv * r * g_ref[...]).astype(BF16)

    return _call(body, name="rms_fwd", out_shape=_sds((n, d), BF16), grid=(n // tm,),
                 in_specs=[pl.BlockSpec((tm, d), lambda i: (i, 0)), pl.BlockSpec((1, d), lambda i: (0, 0))] + extra_specs,
                 out_specs=pl.BlockSpec((tm, d), lambda i: (i, 0)), sem=("parallel",))(x, g, *extra)


def rms_bwd(dh, x, g, dres):
    n, d = x.shape
    tm = 256

    def body(dh_ref, x_ref, g_ref, dres_ref, dx_ref, dxb_ref, dg_ref):
        @pl.when(pl.program_id(0) == 0)
        def _():
            dg_ref[...] = jnp.zeros_like(dg_ref)

        xv = x_ref[...]
        r = lax.rsqrt(jnp.mean(xv * xv, axis=-1, keepdims=True) + EPS)
        xhat = xv * r
        dhv = dh_ref[...].astype(F32)
        dg_ref[...] += jnp.sum(dhv * xhat, axis=0, keepdims=True)
        dxn = dhv * g_ref[...]
        dx = dres_ref[...] + r * (dxn - xhat * jnp.mean(dxn * xhat, axis=-1, keepdims=True))
        dx_ref[...] = dx
        dxb_ref[...] = dx.astype(BF16)

    row = pl.BlockSpec((tm, d), lambda i: (i, 0))
    vec = pl.BlockSpec((1, d), lambda i: (0, 0))
    return _call(body, name="rms_bwd", out_shape=(_sds((n, d), F32), _sds((n, d), BF16), _sds((1, d), F32)),
                 grid=(n // tm,), in_specs=[row, row, vec, row], out_specs=(row, row, vec), sem=("arbitrary",))(dh, x, g, dres)


def loss_head(x, g, target):
    n, d = x.shape
    tm = 256

    def body(x_ref, g_ref, t_ref, loss_ref, dx_ref, dxb_ref, dg_ref):
        @pl.when(pl.program_id(0) == 0)
        def _():
            dg_ref[...] = jnp.zeros_like(dg_ref)
            loss_ref[...] = jnp.zeros_like(loss_ref)

        xv = x_ref[...]
        r = lax.rsqrt(jnp.mean(xv * xv, axis=-1, keepdims=True) + EPS)
        xhat = xv * r
        gv = g_ref[...]
        err = xhat * gv - t_ref[...]
        loss_ref[...] += 0.5 * jnp.sum(jnp.mean(err * err, axis=-1, keepdims=True))
        dy = err * (1.0 / d)
        dg_ref[...] += jnp.sum(dy * xhat, axis=0, keepdims=True)
        dxn = dy * gv
        dx = r * (dxn - xhat * jnp.mean(dxn * xhat, axis=-1, keepdims=True))
        dx_ref[...] = dx
        dxb_ref[...] = dx.astype(BF16)

    row = pl.BlockSpec((tm, d), lambda i: (i, 0))
    vec = pl.BlockSpec((1, d), lambda i: (0, 0))
    return _call(body, name="loss_head",
                 out_shape=(_sds((8, LANES), F32), _sds((n, d), F32), _sds((n, d), BF16), _sds((1, d), F32)),
                 grid=(n // tm,), in_specs=[row, vec, row],
                 out_specs=(pl.BlockSpec((8, LANES), lambda i: (0, 0)), row, row, vec), sem=("arbitrary",))(x, g, target)


def mm_cols(a, wc, tm=512):
    n, k = a.shape
    s, _, nq = wc.shape

    def body(a_ref, w_ref, o_ref):
        o_ref[...] = _dot(a_ref[...], w_ref[...]).astype(BF16)

    return _call(body, name="mm_cols", out_shape=_sds((n, s * nq), BF16), grid=(s, n // tm),
                 in_specs=[pl.BlockSpec((tm, k), lambda j, i: (i, 0)), pl.BlockSpec((None, k, nq), lambda j, i: (j, 0, 0))],
                 out_specs=pl.BlockSpec((tm, nq), lambda j, i: (i, j)), sem=("parallel", "parallel"))(a, wc)


def mm_gateup(h, wg, wu, tm=512):
    n, k = h.shape
    s, _, nq = wg.shape

    def body(h_ref, wg_ref, wu_ref, gate_ref, up_ref, ff_ref):
        hv = h_ref[...]
        gate = _dot(hv, wg_ref[...])
        up = _dot(hv, wu_ref[...])
        gate_ref[...] = gate.astype(BF16)
        up_ref[...] = up.astype(BF16)
        ff_ref[...] = (gate * _sigmoid(gate) * up).astype(BF16)

    wspec = pl.BlockSpec((None, k, nq), lambda j, i: (j, 0, 0))
    ospec = pl.BlockSpec((tm, nq), lambda j, i: (i, j))
    o = _sds((n, s * nq), BF16)
    return _call(body, name="mm_gateup", out_shape=(o, o, o), grid=(s, n // tm),
                 in_specs=[pl.BlockSpec((tm, k), lambda j, i: (i, 0)), wspec, wspec],
                 out_specs=(ospec, ospec, ospec), sem=("parallel", "parallel"))(h, wg, wu)


def mm_rows_res(a, wr, res, tm=512, tn=1024):
    n, k = a.shape
    _, nn = wr.shape

    def body(a_ref, w_ref, r_ref, o_ref):
        o_ref[...] = r_ref[...] + _dot(a_ref[...], w_ref[...])

    return _call(body, name="mm_rows_res", out_shape=_sds((n, nn), F32), grid=(nn // tn, n // tm),
                 in_specs=[pl.BlockSpec((tm, k), lambda j, i: (i, 0)), pl.BlockSpec((k, tn), lambda j, i: (0, j)),
                           pl.BlockSpec((tm, tn), lambda j, i: (i, j))],
                 out_specs=pl.BlockSpec((tm, tn), lambda j, i: (i, j)), sem=("parallel", "parallel"))(a, wr, res)


def mm_nt_rows(dy, wr, tk, tm=512, after=None):
    n, nn = dy.shape
    k, _ = wr.shape
    extra, extra_specs = _behind(after)

    def body(dy_ref, w_ref, *rest):
        rest[-1][...] = _dot_nt(dy_ref[...], w_ref[...]).astype(BF16)

    return _call(body, name="mm_nt_rows", out_shape=_sds((n, k), BF16), grid=(k // tk, n // tm),
                 in_specs=[pl.BlockSpec((tm, nn), lambda j, i: (i, 0)), pl.BlockSpec((tk, nn), lambda j, i: (j, 0))]
                 + extra_specs,
                 out_specs=pl.BlockSpec((tm, tk), lambda j, i: (i, j)), sem=("parallel", "parallel"))(dy, wr, *extra)


def mm_dff(dy, wd, gate, up, tk, tm=512, after=None):
    n, nn = dy.shape
    k, _ = wd.shape
    extra, extra_specs = _behind(after)

    def body(dy_ref, w_ref, gate_ref, up_ref, *rest):
        dgate_ref, dup_ref = rest[-2:]
        dff = _dot_nt(dy_ref[...], w_ref[...])
        gate = gate_ref[...].astype(F32)
        up = up_ref[...].astype(F32)
        sig = _sigmoid(gate)
        dgate_ref[...] = (dff * up * (sig * (1.0 + gate * (1.0 - sig)))).astype(BF16)
        dup_ref[...] = (dff * (gate * sig)).astype(BF16)

    tile = pl.BlockSpec((tm, tk), lambda j, i: (i, j))
    o = _sds((n, k), BF16)
    return _call(body, name="mm_dff", out_shape=(o, o), grid=(k // tk, n // tm),
                 in_specs=[pl.BlockSpec((tm, nn), lambda j, i: (i, 0)), pl.BlockSpec((tk, nn), lambda j, i: (j, 0)), tile, tile]
                 + extra_specs,
                 out_specs=(tile, tile), sem=("parallel", "parallel"))(dy, wd, gate, up, *extra)


def mm_nt_cols(das, wcs, tm=512):
    n = das[0].shape[0]
    s, k, nq = wcs[0].shape
    npair = len(das)

    def body(*refs):
        da_refs, w_refs, o_ref, acc = refs[:npair], refs[npair:2 * npair], refs[2 * npair], refs[2 * npair + 1]
        ss = pl.program_id(1)

        @pl.when(ss == 0)
        def _():
            acc[...] = jnp.zeros_like(acc)

        for da_ref, w_ref in zip(da_refs, w_refs):
            acc[...] += _dot_nt(da_ref[...], w_ref[...])

        @pl.when(ss == s - 1)
        def _():
            o_ref[...] = acc[...].astype(BF16)

    return _call(body, name="mm_nt_cols%d" % npair, out_shape=_sds((n, k), BF16), grid=(n // tm, s),
                 in_specs=[pl.BlockSpec((tm, nq), lambda i, ss: (i, ss))] * npair
                 + [pl.BlockSpec((None, k, nq), lambda i, ss: (ss, 0, 0))] * npair,
                 out_specs=pl.BlockSpec((tm, k), lambda i, ss: (i, 0)),
                 scratch=[pltpu.VMEM((tm, k), F32)], sem=("parallel", "arbitrary"))(*das, *wcs)


def mm_tn(a, b, nq, shard_major, tka=512):
    n, ka = a.shape
    nb = b.shape[1] // nq

    def body(a_ref, b_ref, o_ref):
        o_ref[...] = _dot_tn(a_ref[...], b_ref[...]).astype(BF16)

    if shard_major:
        out_shape, out_spec = _sds((nb, ka, nq), BF16), pl.BlockSpec((None, tka, nq), lambda j, i: (j, i, 0))
    else:
        out_shape, out_spec = _sds((ka, nb * nq), BF16), pl.BlockSpec((tka, nq), lambda j, i: (i, j))
    return _call(body, name="mm_tn", out_shape=out_shape, grid=(nb, ka // tka),
                 in_specs=[pl.BlockSpec((n, tka), lambda j, i: (0, i)), pl.BlockSpec((n, nq), lambda j, i: (0, j))],
                 out_specs=out_spec, sem=("parallel", "parallel"))(a, b)


def rope_tables(n):
    pos = jnp.arange(n, dtype=F32)
    inv = ROPE_THETA ** (-jnp.arange(0, HD, 2, dtype=F32) / HD)
    ang = pos[:, None] * inv[None, :]
    cos, sin = jnp.cos(ang), jnp.sin(ang)
    return jnp.concatenate([cos, cos], axis=-1), jnp.concatenate([-sin, sin], axis=-1)


PERM_ROWS = 256


def perm_matrix(d):
    n = PERM_ROWS // d
    i = jnp.arange(PERM_ROWS)
    src = (i % n) * d + i // n
    return (src[:, None] == jnp.arange(PERM_ROWS)[None, :]).astype(BF16)


def _perm_f32(p, x):
    hi = x.astype(BF16)
    rest = x - hi.astype(F32)
    mid = rest.astype(BF16)
    lo = (rest - mid.astype(F32)).astype(BF16)
    return (_dot(p, hi) + _dot(p, mid)) + _dot(p, lo)


def _res_spec(d, cols, col_block=0):
    return pl.BlockSpec((d, PERM_ROWS // d, cols), lambda i: (0, i, col_block))


def _perm_specs():
    return [pl.BlockSpec((PERM_ROWS, PERM_ROWS), lambda i: (0, 0))] * (len(DILATIONS) - 1)


def qkv_split(proj, cos2, sin2, perms):
    n = proj.shape[0]
    tm = PERM_ROWS

    def body(p_ref, c_ref, s_ref, *rest):
        perm_refs, outs = rest[:len(DILATIONS) - 1], rest[len(DILATIONS) - 1:]
        c, s = c_ref[...], s_ref[...]
        for h in range(2 * NH):
            t = p_ref[:, h * HD:(h + 1) * HD].astype(F32)
            outs[0][0, :, h * HD:(h + 1) * HD] = (t * c + pltpu.roll(t, HD // 2, 1) * s).astype(BF16)
        outs[0][0, :, 2 * DA:] = p_ref[:, 2 * DA:]
        nat = outs[0][0]
        for o_ref, p_ref_d, d in zip(outs[1:], perm_refs, DILATIONS[1:]):
            o_ref[...] = _dot(p_ref_d[...], nat).astype(BF16).reshape(d, tm // d, 3 * DA)

    tab = pl.BlockSpec((tm, HD), lambda i: (i, 0))
    return _call(body, name="qkv_split", out_shape=tuple(_sds((d, n // d, 3 * DA), BF16) for d in DILATIONS),
                 grid=(n // tm,), in_specs=[pl.BlockSpec((tm, 3 * DA), lambda i: (i, 0)), tab, tab] + _perm_specs(),
                 out_specs=tuple(_res_spec(d, 3 * DA) for d in DILATIONS), sem=("parallel",))(proj, cos2, sin2, *perms)


def dproj_assemble(dqs, dks, dvs, duv, cos2, sin2, perms_t):
    n = duv.shape[0]
    tm = PERM_ROWS
    npat = len(DILATIONS)

    def body(*refs):
        parts = refs[0:npat], refs[npat:2 * npat], refs[2 * npat:3 * npat]
        duv_ref, c_ref, s_ref = refs[3 * npat:3 * npat + 3]
        pt_refs, o_ref = refs[3 * npat + 3:-1], refs[-1]
        c, s = c_ref[...], s_ref[...]
        for part, part_refs in enumerate(parts):
            t = part_refs[0][0].astype(F32)
            for ref, pt_ref in zip(part_refs[1:], pt_refs):
                t = t + _dot(pt_ref[...], ref[...].reshape(tm, DA))
            if part == 2:
                o_ref[:, 2 * DA:3 * DA] = t.astype(BF16)
                continue
            for h in range(NH):
                th = t[:, h * HD:(h + 1) * HD]
                o_ref[:, part * DA + h * HD:part * DA + (h + 1) * HD] = (th * c - pltpu.roll(th, HD // 2, 1) * s).astype(BF16)
        o_ref[:, 3 * DA:] = duv_ref[...]

    blks = [_res_spec(d, DA) for d in DILATIONS]
    tab = pl.BlockSpec((tm, HD), lambda i: (i, 0))
    return _call(body, name="dproj_assemble", out_shape=_sds((n, DIN), BF16), grid=(n // tm,),
                 in_specs=blks * 3 + [pl.BlockSpec((tm, 2 * DG), lambda i: (i, 0)), tab, tab] + _perm_specs(),
                 out_specs=pl.BlockSpec((tm, DIN), lambda i: (i, 0)), sem=("parallel",))(
        *dqs, *dks, *dvs, duv, cos2, sin2, *perms_t)


ATTN_BLOCK_BYTES = 2 << 20


def _attn_tiles(ls):
    tq = min(128, ls)
    w = min(tq + 2 * NSIDE, ls)
    return tq, w, ls // tq


def _attn_heads(ls):
    return NH if ls * DA * 2 <= ATTN_BLOCK_BYTES else 1


ATTN_UNROLL = 8


def _attn_loop(nt, hb, tile):
    for h in range(hb):
        def trip(t, carry, h=h):
            tile(t, h)
            return carry

        lax.fori_loop(0, nt, trip, 0, unroll=min(nt, ATTN_UNROLL))


def _band(t, tq, w, ls):
    q0 = pl.multiple_of(t * tq, tq)
    ks = pl.multiple_of(jnp.clip(t * tq - NSIDE, 0, ls - w), NSIDE)
    qpos = q0 + lax.broadcasted_iota(jnp.int32, (tq, w), 0)
    kpos = ks + lax.broadcasted_iota(jnp.int32, (tq, w), 1)
    return q0, ks, jnp.abs(kpos - qpos) <= NSIDE


def _attn_specs(ls, hb):
    nhb = NH // hb
    q = pl.BlockSpec((None, ls, hb * HD), lambda g: (g // nhb, 0, g % nhb))
    k = pl.BlockSpec((None, ls, hb * HD), lambda g: (g // nhb, 0, nhb + g % nhb))
    v = pl.BlockSpec((None, ls, hb * HD), lambda g: (g // nhb, 0, 2 * nhb + g % nhb))
    return q, k, v, q


def _lse_spec(ls, hb):
    nhb = NH // hb
    return pl.BlockSpec((None, ls, LANES), lambda g: (g // nhb, 0, 0))


def attn_fwd(qkv):
    d, ls, _ = qkv.shape
    tq, w, nt = _attn_tiles(ls)
    hb = _attn_heads(ls)
    nhb = NH // hb

    def body(q_ref, k_ref, v_ref, o_ref, l_ref):
        first = (pl.program_id(0) % nhb) * hb
        lane = lax.broadcasted_iota(jnp.int32, (tq, LANES), 1)

        @pl.when(first == 0)
        def _():
            l_ref[...] = jnp.zeros_like(l_ref)

        def tile(t, h):
            q0, ks, valid = _band(t, tq, w, ls)
            cols = slice(h * HD, (h + 1) * HD)
            s = _dot_nt(q_ref[pl.ds(q0, tq), cols], k_ref[pl.ds(ks, w), cols]) * SCALE
            s = jnp.where(valid, s, NEG)
            m = jnp.max(s, axis=1, keepdims=True)
            p = jnp.exp(s - m)
            l = jnp.sum(p, axis=1, keepdims=True)
            o = _dot(p.astype(BF16), v_ref[pl.ds(ks, w), cols]) / l
            o_ref[pl.ds(q0, tq), cols] = o.astype(BF16)
            l_ref[pl.ds(q0, tq), :] = jnp.where(lane == first + h, m + jnp.log(l), l_ref[pl.ds(q0, tq), :])

        _attn_loop(nt, hb, tile)

    q, k, v, o = _attn_specs(ls, hb)
    return _call(body, name="attn_fwd_d%d" % d, out_shape=(_sds((d, ls, DA), BF16), _sds((d, ls, LANES), F32)),
                 grid=(d * NH // hb,), in_specs=[q, k, v], out_specs=(o, _lse_spec(ls, hb)), sem=("arbitrary",))(
        qkv, qkv, qkv)


def attn_merge(os_, ls_, ga, perms_t):
    n = os_[0].shape[1]
    tm = PERM_ROWS
    npat = len(DILATIONS)

    def body(*refs):
        o_refs, l_refs = refs[:npat], refs[npat:2 * npat]
        g_ref, pt_refs = refs[2 * npat], refs[2 * npat + 1:3 * npat]
        a_ref, lse_ref, mix_ref = refs[3 * npat:]
        ov = [o_refs[0][0].astype(F32)] + [_dot(pt[...], o[...].reshape(tm, DA)) for o, pt in zip(o_refs[1:], pt_refs)]
        lv = [l_refs[0][0]] + [_perm_f32(pt[...], l[...].reshape(tm, LANES)) for l, pt in zip(l_refs[1:], pt_refs)]
        lmax = functools.reduce(jnp.maximum, lv)
        e = [jnp.exp(x - lmax) for x in lv]
        den = functools.reduce(lambda p, q: p + q, e)
        wts = [ei / den for ei in e]
        lse_ref[...] = lmax + jnp.log(den)
        heads = []
        for h in range(NH):
            cols = slice(h * HD, (h + 1) * HD)
            heads.append(functools.reduce(lambda p, q: p + q, [wt[:, h:h + 1] * oi[:, cols] for wt, oi in zip(wts, ov)]))
        a = jnp.concatenate(heads, axis=1)
        a_ref[...] = a.astype(BF16)
        r = lax.rsqrt(jnp.mean(a * a, axis=-1, keepdims=True) + EPS)
        mix_ref[...] = (a * r * g_ref[...]).astype(BF16)

    blk = pl.BlockSpec((tm, DA), lambda i: (i, 0))
    return _call(body, name="attn_merge", out_shape=(_sds((n, DA), BF16), _sds((n, LANES), F32), _sds((n, D), BF16)),
                 grid=(n // tm,),
                 in_specs=[_res_spec(d, DA) for d in DILATIONS] + [_res_spec(d, LANES) for d in DILATIONS]
                 + [pl.BlockSpec((1, DA), lambda i: (0, 0))] + _perm_specs(),
                 out_specs=(blk, pl.BlockSpec((tm, LANES), lambda i: (i, 0)), blk), sem=("parallel",))(
        *os_, *ls_, ga, *perms_t)


def attn_norm_bwd(dmix, a, lse, ga, perms):
    n = a.shape[0]
    tm = PERM_ROWS
    npat = len(DILATIONS)

    def body(dm_ref, a_ref, lse_ref, g_ref, *rest):
        p_refs = rest[:npat - 1]
        da_refs, a_refs, lse_refs = rest[npat - 1:2 * npat - 1], rest[2 * npat - 1:3 * npat - 2], rest[3 * npat - 2:4 * npat - 3]
        dg_ref = rest[-1]

        @pl.when(pl.program_id(0) == 0)
        def _():
            dg_ref[...] = jnp.zeros_like(dg_ref)

        ab = a_ref[...]
        av = ab.astype(F32)
        r = lax.rsqrt(jnp.mean(av * av, axis=-1, keepdims=True) + EPS)
        ahat = av * r
        dm = dm_ref[...].astype(F32)
        dg_ref[...] += jnp.sum(dm * ahat, axis=0, keepdims=True)
        dn = dm * g_ref[...]
        da = (r * (dn - ahat * jnp.mean(dn * ahat, axis=-1, keepdims=True))).astype(BF16)
        da_refs[0][0] = da
        lv = lse_ref[...]
        for i, d in enumerate(DILATIONS[1:]):
            p = p_refs[i][...]
            da_refs[i + 1][...] = _dot(p, da).astype(BF16).reshape(d, tm // d, DA)
            a_refs[i][...] = _dot(p, ab).astype(BF16).reshape(d, tm // d, DA)
            lse_refs[i][...] = _perm_f32(p, lv).reshape(d, tm // d, LANES)

    blk = pl.BlockSpec((tm, DA), lambda i: (i, 0))
    vec = pl.BlockSpec((1, DA), lambda i: (0, 0))
    res = [_res_spec(d, DA) for d in DILATIONS]
    outs = _call(body, name="attn_norm_bwd",
                 out_shape=tuple([_sds((d, n // d, DA), BF16) for d in DILATIONS]
                                 + [_sds((d, n // d, DA), BF16) for d in DILATIONS[1:]]
                                 + [_sds((d, n // d, LANES), F32) for d in DILATIONS[1:]] + [_sds((1, DA), F32)]),
                 grid=(n // tm,),
                 in_specs=[blk, blk, pl.BlockSpec((tm, LANES), lambda i: (i, 0)), vec] + _perm_specs(),
                 out_specs=tuple(res + res[1:] + [_res_spec(d, LANES) for d in DILATIONS[1:]] + [vec]),
                 sem=("arbitrary",))(dmix, a, lse, ga, *perms)
    return outs[:npat], outs[npat:2 * npat - 1], outs[2 * npat - 1:3 * npat - 2], outs[-1]


def attn_bwd(qkv, a, da, lse):
    d, ls, _ = qkv.shape
    tq, w, nt = _attn_tiles(ls)
    hb = _attn_heads(ls)

    nhb = NH // hb

    def body(q_ref, k_ref, v_ref, a_ref, da_ref, lse_ref, dq_ref, dk_ref, dv_ref, dk_acc, dv_acc):
        first = (pl.program_id(0) % nhb) * hb
        lane = lax.broadcasted_iota(jnp.int32, (tq, LANES), 1)
        dk_acc[...] = jnp.zeros_like(dk_acc)
        dv_acc[...] = jnp.zeros_like(dv_acc)

        def tile(t, h):
            q0, ks, valid = _band(t, tq, w, ls)
            cols = slice(h * HD, (h + 1) * HD)
            q = q_ref[pl.ds(q0, tq), cols]
            k = k_ref[pl.ds(ks, w), cols]
            v = v_ref[pl.ds(ks, w), cols]
            do = da_ref[pl.ds(q0, tq), cols]
            if nhb == 1:
                lse_h = lse_ref[pl.ds(q0, tq), h:h + 1]
            else:
                lse_h = jnp.sum(jnp.where(lane == first + h, lse_ref[pl.ds(q0, tq), :], 0.0), axis=1, keepdims=True)
            s = jnp.where(valid, _dot_nt(q, k) * SCALE, NEG)
            p = jnp.exp(s - lse_h)
            drow = jnp.sum(do.astype(F32) * a_ref[pl.ds(q0, tq), cols].astype(F32), axis=1, keepdims=True)
            ds = (p * (_dot_nt(do, v) - drow) * SCALE).astype(BF16)
            dv_acc[pl.ds(ks, w), cols] += _dot_tn(p.astype(BF16), do)
            dk_acc[pl.ds(ks, w), cols] += _dot_tn(ds, q)
            dq_ref[pl.ds(q0, tq), cols] = _dot(ds, k).astype(BF16)

        _attn_loop(nt, hb, tile)
        dk_ref[...] = dk_acc[...].astype(BF16)
        dv_ref[...] = dv_acc[...].astype(BF16)

    q, k, v, o = _attn_specs(ls, hb)
    out = _sds((d, ls, DA), BF16)
    return _call(body, name="attn_bwd_d%d" % d, out_shape=(out, out, out), grid=(d * NH // hb,),
                 in_specs=[q, k, v, o, o, _lse_spec(ls, hb)], out_specs=(o, o, o),
                 scratch=[pltpu.VMEM((ls, hb * HD), F32), pltpu.VMEM((ls, hb * HD), F32)], sem=("parallel",))(
        qkv, qkv, qkv, a, da, lse)


GM_TM = 256
INV_SQRT2 = 0.7071067811865476
INV_SQRT2PI = 0.3989422804014327


def _gelu(x):
    return 0.5 * x * (1.0 + lax.erf(x * INV_SQRT2))


def _gelu_grad(x):
    return 0.5 * (1.0 + lax.erf(x * INV_SQRT2)) + x * (INV_SQRT2PI * jnp.exp(-0.5 * x * x))


def _gmlp_forward(up, vp, ln_g, ws_ref, bias):
    u = _gelu(up)
    v = _gelu(vp)
    vc = v - jnp.mean(v, axis=-1, keepdims=True)
    rs = lax.rsqrt(jnp.mean(vc * vc, axis=-1, keepdims=True) + EPS)
    vhat = vc * rs
    vln = (vhat * ln_g).astype(BF16)
    rows = []
    for c in range(GM_TM // CH):
        cols = [_dot(ws_ref[g], vln[c * CH:(c + 1) * CH, g * 128:(g + 1) * 128]) for g in range(NG)]
        rows.append(jnp.concatenate(cols, axis=1) + bias)
    return u, vhat, rs, vln, jnp.concatenate(rows, axis=0)


def _gmlp_specs():
    ublk = pl.BlockSpec((GM_TM, DG), lambda i: (i, 3 * DA // DG))
    vblk = pl.BlockSpec((GM_TM, DG), lambda i: (i, 3 * DA // DG + 1))
    vec = pl.BlockSpec((1, DG), lambda i: (0, 0))
    wsp = pl.BlockSpec((NG, CH, CH), lambda i: (0, 0, 0))
    bsp = pl.BlockSpec((CH, DG), lambda i: (0, 0))
    return ublk, vblk, vec, wsp, bsp


def gmlp_fwd(proj, ln_g, ws, bias, gg, mix):
    n = proj.shape[0]

    def body(up_ref, vp_ref, ln_ref, ws_ref, b_ref, gg_ref, mix_in, mix_ref):
        del mix_in
        u, _, _, _, mixed = _gmlp_forward(up_ref[...].astype(F32), vp_ref[...].astype(F32), ln_ref[...], ws_ref, b_ref[...])
        gout = u * mixed
        r = lax.rsqrt(jnp.mean(gout * gout, axis=-1, keepdims=True) + EPS)
        mix_ref[...] = (gout * r * gg_ref[...]).astype(BF16)

    ublk, vblk, vec, wsp, bsp = _gmlp_specs()
    return _call(body, name="gmlp_fwd", out_shape=_sds((n, D), BF16), grid=(n // GM_TM,),
                 in_specs=[ublk, vblk, vec, wsp, bsp, vec, pl.BlockSpec(memory_space=pl.ANY)],
                 out_specs=pl.BlockSpec((GM_TM, DG), lambda i: (i, DA // DG)), sem=("parallel",), aliases={6: 0})(
        proj, proj, ln_g, ws, bias, gg, mix)


def gmlp_bwd(proj, dmix, ln_g, ws, wst, bias, gg):
    n = proj.shape[0]

    def body(up_ref, vp_ref, dm_ref, ln_ref, ws_ref, wst_ref, b_ref, gg_ref, duv_ref, dln_ref, dws_ref, dbs_ref, dgg_ref,
             db_ref):
        @pl.when(pl.program_id(0) == 0)
        def _():
            dln_ref[...] = jnp.zeros_like(dln_ref)
            dws_ref[...] = jnp.zeros_like(dws_ref)
            db_ref[...] = jnp.zeros_like(db_ref)
            dgg_ref[...] = jnp.zeros_like(dgg_ref)

        up = up_ref[...].astype(F32)
        vp = vp_ref[...].astype(F32)
        ln_g = ln_ref[...]
        u, vhat, rs, vln, mixed = _gmlp_forward(up, vp, ln_g, ws_ref, b_ref[...])
        gout = u * mixed
        r = lax.rsqrt(jnp.mean(gout * gout, axis=-1, keepdims=True) + EPS)
        ghat = gout * r
        dm = dm_ref[...].astype(F32)
        dgg_ref[...] += jnp.sum(dm * ghat, axis=0, keepdims=True)
        dn = dm * gg_ref[...]
        dgout = r * (dn - ghat * jnp.mean(dn * ghat, axis=-1, keepdims=True))
        du = dgout * mixed
        dmixed = dgout * u
        dmb = dmixed.astype(BF16)
        rows = []
        for c in range(GM_TM // CH):
            rsl = slice(c * CH, (c + 1) * CH)
            db_ref[...] += dmixed[rsl, :]
            cols = []
            for g in range(NG):
                csl = slice(g * 128, (g + 1) * 128)
                dws_ref[g] += _dot_nt(dmb[rsl, csl], vln[rsl, csl])
                cols.append(_dot(wst_ref[g], dmb[rsl, csl]))
            rows.append(jnp.concatenate(cols, axis=1))
        dvln = jnp.concatenate(rows, axis=0)
        dln_ref[...] += jnp.sum(dvln * vhat, axis=0, keepdims=True)
        dvh = dvln * ln_g
        dv = rs * (dvh - jnp.mean(dvh, axis=-1, keepdims=True) - vhat * jnp.mean(dvh * vhat, axis=-1, keepdims=True))
        duv_ref[:, 0:DG] = (du * _gelu_grad(up)).astype(BF16)
        duv_ref[:, DG:] = (dv * _gelu_grad(vp)).astype(BF16)

        @pl.when(pl.program_id(0) == pl.num_programs(0) - 1)
        def _():
            for g in range(NG):
                dbs_ref[g:g + 1, :] = jnp.sum(jnp.transpose(db_ref[:, g * 128:(g + 1) * 128]), axis=0, keepdims=True)

    ublk, vblk, vec, wsp, bsp = _gmlp_specs()
    return _call(body, name="gmlp_bwd",
                 out_shape=(_sds((n, 2 * DG), BF16), _sds((1, DG), F32), _sds((NG, CH, CH), F32), _sds((NG, CH), F32),
                            _sds((1, DG), F32)),
                 grid=(n // GM_TM,),
                 in_specs=[ublk, vblk, pl.BlockSpec((GM_TM, DG), lambda i: (i, DA // DG)), vec, wsp, wsp, bsp, vec],
                 out_specs=(pl.BlockSpec((GM_TM, 2 * DG), lambda i: (i, 0)), vec, wsp,
                            pl.BlockSpec((NG, CH), lambda i: (0, 0)), vec),
                 scratch=[pltpu.VMEM((CH, DG), F32)], sem=("arbitrary",))(
        proj, proj, dmix, ln_g, ws, wst, bias, gg)


def _row(v):
    return v.reshape(1, -1)


def local_step(x, target, small, big):
    tables = step_tables()
    saved = []
    for l in range(NL):
        x_mid, sv = mixer_fwd(x, small, l, big[l], tables)
        x, sv2 = ffn_fwd(x_mid, small, l, big[l])
        saved.append({**sv, **sv2})
    loss8, dx, dxb, dfinal = loss_head(x, _row(small["final_g"]), target)
    gs = {k: [None] * NL for k in SMALL[:-1]}
    gbig = [None] * NL
    for l in reversed(range(NL)):
        dx, dxb, gs1, gb1 = ffn_bwd(dx, dxb, small, l, big[l], saved[l])
        dx, dxb, gs2, gb2 = mixer_bwd(dx, dxb, small, l, big[l], saved[l], tables)
        gbig[l] = {**gb1, **gb2}
        for k, g in {**gs1, **gs2}.items():
            gs[k][l] = g
    return loss8, dx, gs, dfinal, gbig


def step_tables():
    perms = [perm_matrix(d) for d in DILATIONS[1:]]
    return rope_tables(T) + (perms, [p.T for p in perms])


def mixer_fwd(x, small, l, w, tables, after=None):
    cos2, sin2, perms, perms_t = tables
    ws_b = small["w_spatial"][l].astype(BF16)
    bias = jnp.repeat(small["b_spatial"][l].T, 128, axis=1)
    h = rms_fwd(x, _row(small["norm1_g"][l]), after)
    proj = mm_cols(h, w["win"])
    qkvs = qkv_split(proj, cos2, sin2, perms)
    outs = [attn_fwd(qkv) for qkv in qkvs]
    a, lse, mix = attn_merge([o for o, _ in outs], [s for _, s in outs], _row(small["mix_norm_attn_g"][l]), perms_t)
    mix = gmlp_fwd(proj, _row(small["gmlp_ln_g"][l]), ws_b, bias, _row(small["mix_norm_gmlp_g"][l]), mix)
    x_mid = mm_rows_res(mix, w["wout"], x)
    return x_mid, dict(x=x, h=h, proj=proj, qkvs=qkvs, a=a, lse=lse, mix=mix, ws_b=ws_b, bias=bias)


def ffn_fwd(x_mid, small, l, w, after=None):
    h2 = rms_fwd(x_mid, _row(small["norm2_g"][l]), after)
    gate, up, ff = mm_gateup(h2, w["wg"], w["wu"])
    x_out = mm_rows_res(ff, w["wd"], x_mid)
    return x_out, dict(x_mid=x_mid, h2=h2, gate=gate, up=up, ff=ff)


def ffn_bwd(dx, dxb, small, l, w, sv, after=None):
    gs = {}
    dgate, dup = mm_dff(dxb, w["wd"], sv["gate"], sv["up"], tk=FF // NCHIP, after=after)
    g_wd = mm_tn(sv["ff"], dxb, D // 2, False)
    dh2 = mm_nt_cols([dgate, dup], [w["wg"], w["wu"]])
    g_wg = mm_tn(sv["h2"], dgate, FF // NCHIP, True)
    g_wu = mm_tn(sv["h2"], dup, FF // NCHIP, True)
    dx, dxb, gs["norm2_g"] = rms_bwd(dh2, sv["x_mid"], _row(small["norm2_g"][l]), dx)
    return dx, dxb, gs, dict(wg=g_wg, wu=g_wu, wd=g_wd.reshape(NCHIP, FF // NCHIP, D))


def mixer_bwd(dx, dxb, small, l, w, sv, tables, after=None):
    cos2, sin2, perms, perms_t = tables
    gs = {}
    dmix = mm_nt_rows(dxb, w["wout"], tk=D // 2, after=after)
    g_wout = mm_tn(sv["mix"], dxb, D // 2, False)
    das, a_res, lse_res, gs["mix_norm_attn_g"] = attn_norm_bwd(dmix, sv["a"], sv["lse"],
                                                                _row(small["mix_norm_attn_g"][l]), perms)
    a_all = [sv["a"].reshape((1,) + sv["a"].shape)] + list(a_res)
    lse_all = [sv["lse"].reshape((1,) + sv["lse"].shape)] + list(lse_res)
    parts = [attn_bwd(*operands) for operands in zip(sv["qkvs"], a_all, das, lse_all)]
    wst = jnp.swapaxes(small["w_spatial"][l], 1, 2).astype(BF16)
    duv, gs["gmlp_ln_g"], gs["w_spatial"], gs["b_spatial"], gs["mix_norm_gmlp_g"] = gmlp_bwd(
        sv["proj"], dmix, _row(small["gmlp_ln_g"][l]), sv["ws_b"], wst, sv["bias"], _row(small["mix_norm_gmlp_g"][l]))
    dproj = dproj_assemble([p[0] for p in parts], [p[1] for p in parts], [p[2] for p in parts], duv, cos2, sin2,
                           perms_t)
    dh = mm_nt_cols([dproj], [w["win"]])
    g_win = mm_tn(sv["h"], dproj, DIN // NCHIP, True)
    dx, dxb, gs["norm1_g"] = rms_bwd(dh, sv["x"], _row(small["norm1_g"][l]), dx)
    return dx, dxb, gs, dict(win=g_win, wout=g_wout.reshape(NCHIP, D // NCHIP, D))


KINDS = ("win", "wout", "wg", "wu", "wd")
GROUPS = (("win", "wout"), ("wg", "wu", "wd"))
ANY = pl.BlockSpec(memory_space=pl.ANY)


def _place():
    return lax.axis_index("x"), lax.axis_index("y"), lax.axis_index("c")


def _other_chips(x, y):
    return [(1 - x, y), (x, 1 - y), (1 - x, 1 - y)]


def _remote(src, dst, send_sem, recv_sem, to):
    return pltpu.make_async_remote_copy(src_ref=src, dst_ref=dst, send_sem=send_sem, recv_sem=recv_sem,
                                        device_id=to, device_id_type=MESH)


def _hbm_call(body, *, name, n_in, out_shape, scratch, in_place=False):
    return pl.pallas_call(body, name=name, out_shape=tuple(out_shape), in_specs=[ANY] * n_in,
                          out_specs=tuple(ANY for _ in out_shape), scratch_shapes=list(scratch),
                          input_output_aliases={k: k for k in range(n_in)} if in_place else {},
                          compiler_params=pltpu.CompilerParams(vmem_limit_bytes=VMEM_LIMIT))


def place_vector():
    x, y, c = _place()
    return jnp.stack([c, 2 * x + y] + [2 * cx + cy for cx, cy in _other_chips(x, y)]).astype(jnp.int32)


PUSH_ROWS = 512
PUSH_SLOTS = 3


def _push_tiles(src, dst, buf, load_sem, send_sem, recv_sem, to, src_at, dst_at, nt):
    tr = buf.shape[1]
    n = len(src_at) * nt

    def pick(vals, seg):
        out = vals[0]
        for q in range(1, len(vals)):
            out = jnp.where(seg == q, vals[q], out)
        return out

    def tile(ref, at, t):
        seg = t // nt
        row = pl.multiple_of(pick([a[1] for a in at], seg) + (t - seg * nt) * tr, 16)
        return ref.at[pick([a[0] for a in at], seg), pl.ds(row, tr), :]

    def load(t):
        slot = t % PUSH_SLOTS
        return pltpu.make_async_copy(tile(src, src_at, t), buf.at[slot], load_sem.at[slot])

    def send(t):
        slot = t % PUSH_SLOTS
        return _remote(buf.at[slot], tile(dst, dst_at, t), send_sem.at[slot], recv_sem, to)

    load(0).start()

    def step(t, carry):
        load(t).wait()
        send(t).start()

        @pl.when(t >= PUSH_SLOTS - 1)
        def _():
            send(t - (PUSH_SLOTS - 1)).wait_send()

        @pl.when(t + 1 < n)
        def _():
            load(t + 1).start()

        return carry

    lax.fori_loop(0, n, step, 0)
    for t in range(max(n - (PUSH_SLOTS - 1), 0), n):
        send(t).wait_send()


def _await_tiles(dst, send_sem, recv_sem, to, nseg, rows):
    whole = dst.at[pl.ds(0, nseg), pl.ds(0, rows), :]
    _remote(whole, whole, send_sem.at[0], recv_sem, to).wait_recv()


def _push_rows(seg_rows):
    return _row_tile(seg_rows, PUSH_ROWS)


def _push_scratch(arrs, seg_rows):
    return ([pltpu.VMEM((PUSH_SLOTS, _push_rows(r), a.shape[-1]), a.dtype) for a, r in zip(arrs, seg_rows)]
            + [pltpu.SemaphoreType.DMA((len(arrs), PUSH_SLOTS)), pltpu.SemaphoreType.DMA((len(arrs), PUSH_SLOTS)),
               pltpu.SemaphoreType.DMA((len(arrs),))])


def cast_into(w, l, pos, after=None):
    _, r, cols = w.shape
    tr = _row_tile(r)
    extra, extra_specs = _behind(after)

    def body(pos_ref, w_ref, *rest):
        del pos_ref
        rest[-1][...] = w_ref[...].astype(BF16)

    return _call(body, name="cast_into", out_shape=_sds((NCHIP, r, cols), BF16), grid=(r // tr,),
                 in_specs=[pl.BlockSpec((None, tr, cols), lambda i, pos: (l, i, 0))] + extra_specs,
                 out_specs=pl.BlockSpec((None, tr, cols), lambda i, pos: (pos[1], i, 0)),
                 sem=("parallel",), prefetch=1)(pos, w, *extra)


HBM = pl.BlockSpec(memory_space=pltpu.HBM)
SEM = pl.BlockSpec(memory_space=pltpu.SEMAPHORE)
EFFECT = pltpu.SideEffectType.DATAFLOW_SIDE_EFFECTING


def _in_hbm(a):
    return pltpu.with_memory_space_constraint(a, pltpu.HBM)


def _gather_copies(bufs, send, recv):
    x, y, c = _place()
    chips = _other_chips(x, y)

    def half(k, chip):
        hr = bufs[k].shape[1] // 2
        return bufs[k].at[chip, pl.ds(c * hr, hr), :]

    out, back = [], []
    for k in range(len(bufs)):
        for j, (cx, cy) in enumerate(chips):
            i = 3 * k + j
            out.append(_remote(half(k, 2 * x + y), half(k, 2 * x + y), send.at[i], recv.at[i], (cx, cy, c)))
            back.append(_remote(half(k, 2 * x + y), half(k, 2 * cx + cy), send.at[i], recv.at[i], (cx, cy, c)))
    return out, back


def gather_start(fulls):
    nk = len(fulls)

    def body(*refs):
        send, recv = refs[nk], refs[nk + 1]
        bufs, token = refs[nk + 2:2 * nk + 2], refs[2 * nk + 2]
        for cp in _gather_copies(bufs, send, recv)[0]:
            cp.start()
        token[...] = jnp.zeros_like(token)

    res = pl.pallas_call(
        body, name="gather_start",
        out_shape=(pltpu.SemaphoreType.DMA((3 * nk,)), pltpu.SemaphoreType.DMA((3 * nk,)),
                   *[pltpu.HBM(f.shape, f.dtype) for f in fulls], _sds((8, LANES), F32)),
        in_specs=[HBM] * nk, out_specs=(SEM, SEM, *[HBM] * nk, pl.BlockSpec(memory_space=pltpu.VMEM)),
        input_output_aliases={k: 2 + k for k in range(nk)},
        compiler_params=pltpu.CompilerParams(has_side_effects=EFFECT))(*[_in_hbm(f) for f in fulls])
    return res[0], res[1], list(res[2:2 + nk]), res[2 + nk]


def gather_wait(send, recv, fulls, after):
    nk = len(fulls)

    def body(*refs):
        bufs, send_ref, recv_ref = refs[:nk], refs[nk], refs[nk + 1]
        for cp in _gather_copies(bufs, send_ref, recv_ref)[1]:
            cp.wait_send()
            cp.wait_recv()

    return list(pl.pallas_call(
        body, name="gather_wait", out_shape=tuple(pltpu.HBM(f.shape, f.dtype) for f in fulls),
        in_specs=[HBM] * nk + [SEM, SEM, ANY], out_specs=tuple([HBM] * nk),
        input_output_aliases={k: k for k in range(nk)},
        compiler_params=pltpu.CompilerParams(has_side_effects=EFFECT))(*fulls, send, recv, after))


def pair_forward(fulls):
    nk = len(fulls)

    def body(*refs):
        bufs = refs[nk:2 * nk]
        stage = refs[2 * nk:3 * nk]
        load_sem, send_sem, recv_sem = refs[3 * nk:]
        x, y, c = _place()
        sib = (x, y, 1 - c)
        chips = [2 * cx + cy for cx, cy in _other_chips(x, y)]
        for k in range(nk):
            hr = bufs[k].shape[1] // 2
            at = [(chip, c * hr) for chip in chips]
            _push_tiles(bufs[k], bufs[k], stage[k], load_sem.at[k], send_sem.at[k], recv_sem.at[k], sib, at, at,
                        hr // _push_rows(hr))
        for k in range(nk):
            _await_tiles(bufs[k], send_sem.at[k], recv_sem.at[k], sib, 3, bufs[k].shape[1] // 2)

    return _hbm_call(body, name="pair_forward", n_in=nk, out_shape=[_sds(f.shape, f.dtype) for f in fulls], in_place=True,
                     scratch=_push_scratch(fulls, [f.shape[1] // 2 for f in fulls]))(*fulls)


N_PEERS = 7
PEER_FLIPS = [(fx, fy, fc) for fx in (0, 1) for fy in (0, 1) for fc in (0, 1)][1:]


def _chip_copies(ps, lands, send, recv):
    x, y, c = _place()
    cps = []
    for k in range(len(ps)):
        hr = ps[k].shape[1] // 2
        for r, (fx, fy, fc) in enumerate(PEER_FLIPS):
            px, py, pc = x ^ fx, y ^ fy, c ^ fc
            cps.append(_remote(ps[k].at[2 * px + py, pl.ds(pc * hr, hr), :], lands[k].at[r],
                               send.at[N_PEERS * k + r], recv.at[N_PEERS * k + r], (px, py, pc)))
    return cps


def chip_exchange_start(ps, after=None):
    nk = len(ps)
    lands = [lax.empty((N_PEERS, p.shape[1] // 2, p.shape[2]), p.dtype) for p in ps]
    extra, extra_specs = _behind(after)

    def body(*refs):
        refs = refs[2 * nk + len(extra):]
        send, recv = refs[0], refs[1]
        srcs, dsts, token = refs[2:nk + 2], refs[nk + 2:2 * nk + 2], refs[2 * nk + 2]
        for cp in _chip_copies(srcs, dsts, send, recv):
            cp.start()
        token[...] = jnp.zeros_like(token)

    res = pl.pallas_call(
        body, name="chip_exchange_start",
        out_shape=(pltpu.SemaphoreType.DMA((N_PEERS * nk,)), pltpu.SemaphoreType.DMA((N_PEERS * nk,)),
                   *[pltpu.HBM(a.shape, a.dtype) for a in ps + lands], _sds((8, LANES), F32)),
        in_specs=[HBM] * (2 * nk) + extra_specs,
        out_specs=(SEM, SEM, *[HBM] * (2 * nk), pl.BlockSpec(memory_space=pltpu.VMEM)),
        input_output_aliases={k: 2 + k for k in range(2 * nk)},
        compiler_params=pltpu.CompilerParams(has_side_effects=EFFECT))(*[_in_hbm(a) for a in ps + lands], *extra)
    return res[0], res[1], list(res[2:2 + nk]), list(res[2 + nk:2 + 2 * nk]), res[2 + 2 * nk]


def chip_exchange_wait(send, recv, ps, lands, after):
    nk = len(ps)

    def body(*refs):
        srcs, dsts, send_ref, recv_ref = refs[:nk], refs[nk:2 * nk], refs[2 * nk], refs[2 * nk + 1]
        for cp in _chip_copies(srcs, dsts, send_ref, recv_ref):
            cp.wait_send()
            cp.wait_recv()

    res = pl.pallas_call(
        body, name="chip_exchange_wait", out_shape=tuple(pltpu.HBM(a.shape, a.dtype) for a in ps + lands),
        in_specs=[HBM] * (2 * nk) + [SEM, SEM, ANY], out_specs=tuple([HBM] * (2 * nk)),
        input_output_aliases={k: k for k in range(2 * nk)},
        compiler_params=pltpu.CompilerParams(has_side_effects=EFFECT))(*ps, *lands, send, recv, after)
    return list(res[:nk]), list(res[nk:])


def rs_pair_share(fulls):
    nk = len(fulls)

    def body(*refs):
        bufs = refs[nk:2 * nk]
        stage = refs[2 * nk:3 * nk]
        load_sem, send_sem, recv_sem = refs[3 * nk:]
        x, y, c = _place()
        sib = (x, y, 1 - c)
        for k in range(nk):
            hr = bufs[k].shape[1]
            _push_tiles(bufs[k], bufs[k], stage[k], load_sem.at[k], send_sem.at[k], recv_sem.at[k], sib,
                        [(c, 0)], [(c, 0)], hr // _push_rows(hr))
        for k in range(nk):
            _await_tiles(bufs[k], send_sem.at[k], recv_sem.at[k], sib, 1, bufs[k].shape[1])

    return _hbm_call(body, name="rs_pair_share", n_in=nk, out_shape=[_sds(f.shape, f.dtype) for f in fulls], in_place=True,
                     scratch=_push_scratch(fulls, [f.shape[1] for f in fulls]))(*fulls)


N_DEV = 8


def allreduce_small(buf, after=None):
    rows = buf.shape[0]
    rp = rows // N_DEV
    extra, extra_specs = _behind(after)

    def body(in_ref, *rest):
        out_ref, land_ref, send, recv = rest[-4:]
        x, y, c = _place()
        mine = pl.ds(pl.multiple_of((4 * x + 2 * y + c) * rp, 8), rp)
        peers = [(x ^ fx, y ^ fy, c ^ fc) for fx in (0, 1) for fy in (0, 1) for fc in (0, 1)][1:]
        block = [pl.ds(pl.multiple_of((4 * px + 2 * py + pc) * rp, 8), rp) for px, py, pc in peers]
        scatter = [_remote(in_ref.at[block[r], :], land_ref.at[r], send.at[r], recv.at[r], peers[r]) for r in range(7)]
        for cp in scatter:
            cp.start()
        for cp in scatter:
            cp.wait()
        acc = in_ref[mine, :]
        for r in range(7):
            acc = acc + land_ref[r]
        out_ref[mine, :] = acc
        spread = [_remote(out_ref.at[mine, :], out_ref.at[mine, :], send.at[7 + r], recv.at[7 + r], peers[r]) for r in range(7)]
        for cp in spread:
            cp.start()
        for r in range(7):
            _remote(out_ref.at[block[r], :], out_ref.at[block[r], :], send.at[7 + r], recv.at[7 + r], peers[r]).wait_recv()
        for cp in spread:
            cp.wait_send()

    vm = pl.BlockSpec(memory_space=pltpu.VMEM)
    return pl.pallas_call(body, name="allreduce_small", out_shape=_sds(buf.shape, F32), in_specs=[vm] + extra_specs,
                          out_specs=vm,
                          scratch_shapes=[pltpu.VMEM((7, rp, LANES), F32), pltpu.SemaphoreType.DMA((14,)),
                                          pltpu.SemaphoreType.DMA((14,))])(buf, *extra)


def _row_tile(rows, cap=512):
    return max(t for t in range(16, cap + 1, 16) if rows % t == 0)


def add_slots(g, got, pos):
    _, r, cols = g.shape
    hr = r // 2
    tr = _row_tile(hr, 256)
    nt = hr // tr

    def body(pos_ref, o_ref, *rest):
        del pos_ref
        acc = o_ref[...].astype(F32)
        for ref in rest[:N_PEERS]:
            acc = acc + ref[...].astype(F32)
        rest[N_PEERS][...] = acc.astype(BF16)

    slots = [pl.BlockSpec((None, tr, cols), functools.partial(lambda j, i, pos: (j, i, 0), j)) for j in range(N_PEERS)]
    return _call(body, name="add_slots", out_shape=_sds((2, hr, cols), BF16), grid=(nt,),
                 in_specs=[pl.BlockSpec((None, tr, cols), lambda i, pos: (pos[1], pos[0] * nt + i, 0))] + slots,
                 out_specs=pl.BlockSpec((None, tr, cols), lambda i, pos: (pos[0], i, 0)),
                 sem=("parallel",), prefetch=1)(pos, g, *[got] * N_PEERS)


def _adamw_math(w, g, m, v):
    nm = B1 * m + (1.0 - B1) * g
    nv = B2 * v + (1.0 - B2) * (g * g)
    m_hat = nm / (1.0 - B1 ** STEP)
    v_hat = nv / (1.0 - B2 ** STEP)
    return -LR * (m_hat / (jnp.sqrt(v_hat) + AEPS) + WD * w), nm, nv


def adamw_layer(w, g, m, v, l, prev):
    nl, r, cols = w.shape
    tr = _row_tile(r, 256)

    def body(w_ref, g_ref, m_ref, v_ref, *rest):
        go_ref, d_ref, nm_ref, nv_ref = rest[-4:]
        gv = g_ref[...].astype(F32)
        d, nm, nv = _adamw_math(w_ref[...], gv, m_ref[...], v_ref[...])
        go_ref[...] = gv
        d_ref[...] = d
        nm_ref[...] = nm
        nv_ref[...] = nv

    lay = pl.BlockSpec((None, tr, cols), lambda i: (l, i, 0))
    o = _sds((nl, r, cols), F32)
    extra = [] if prev is None else list(prev)
    return _call(body, name="adamw_layer", out_shape=(o, o, o, o), grid=(r // tr,),
                 in_specs=[lay, pl.BlockSpec((tr, cols), lambda i: (i, 0)), lay, lay] + [ANY] * len(extra),
                 out_specs=(lay, lay, lay, lay), sem=("parallel",),
                 aliases={4 + j: j for j in range(len(extra))})(w, g, m, v, *extra)


def adamw(w, g, m, v):
    shape = w.shape
    cols = shape[-1]
    rows = w.size // cols
    tr = _row_tile(rows, 256)

    def body(w_ref, g_ref, m_ref, v_ref, d_ref, nm_ref, nv_ref):
        d_ref[...], nm_ref[...], nv_ref[...] = _adamw_math(w_ref[...], g_ref[...], m_ref[...], v_ref[...])

    blk = pl.BlockSpec((tr, cols), lambda i: (i, 0))
    o = _sds((rows, cols), F32)
    outs = _call(body, name="adamw", out_shape=(o, o, o), grid=(rows // tr,), in_specs=[blk] * 4, out_specs=(blk, blk, blk),
                 sem=("parallel",))(*[t.reshape(rows, cols) for t in (w, g, m, v)])
    return [t.reshape(shape) for t in outs]


SMALL = ("norm1_g", "gmlp_ln_g", "w_spatial", "b_spatial", "mix_norm_attn_g", "mix_norm_gmlp_g", "norm2_g", "final_g")
WEIGHTS = ("norm1_g", "w_in", "gmlp_ln_g", "w_spatial", "b_spatial", "mix_norm_attn_g", "mix_norm_gmlp_g", "w_out",
           "norm2_g", "w_gate", "w_up", "w_down", "final_g")
BIG = dict(win="w_in", wout="w_out", wg="w_gate", wu="w_up", wd="w_down")


def _pack(parts):
    flat = jnp.concatenate([parts[n].reshape(-1) for n in SMALL])
    rows = -(-flat.shape[0] // (LANES * 8 * N_DEV)) * 8 * N_DEV
    return jnp.pad(flat, (0, rows * LANES - flat.shape[0])).reshape(rows, LANES)


def _unpack(buf, like):
    flat = buf.reshape(-1)
    out, at = {}, 0
    for n in SMALL:
        out[n] = flat[at:at + like[n].size].reshape(like[n].shape)
        at += like[n].size
    return out


def kernel(x, norm1_g, w_in, gmlp_ln_g, w_spatial, b_spatial, mix_norm_attn_g, mix_norm_gmlp_g, w_out, norm2_g, w_gate, w_up, w_down, final_g, loss_target, m_norm1_g, m_w_in, m_gmlp_ln_g, m_w_spatial, m_b_spatial, m_mix_norm_attn_g, m_mix_norm_gmlp_g, m_w_out, m_norm2_g, m_w_gate, m_w_up, m_w_down, m_final_g, v_norm1_g, v_w_in, v_gmlp_ln_g, v_w_spatial, v_b_spatial, v_mix_norm_attn_g, v_mix_norm_gmlp_g, v_w_out, v_norm2_g, v_w_gate, v_w_up, v_w_down, v_final_g):
    w = dict(norm1_g=norm1_g, w_in=w_in, gmlp_ln_g=gmlp_ln_g, w_spatial=w_spatial, b_spatial=b_spatial,
             mix_norm_attn_g=mix_norm_attn_g, mix_norm_gmlp_g=mix_norm_gmlp_g, w_out=w_out, norm2_g=norm2_g,
             w_gate=w_gate, w_up=w_up, w_down=w_down, final_g=final_g)
    m = dict(norm1_g=m_norm1_g, w_in=m_w_in, gmlp_ln_g=m_gmlp_ln_g, w_spatial=m_w_spatial, b_spatial=m_b_spatial,
             mix_norm_attn_g=m_mix_norm_attn_g, mix_norm_gmlp_g=m_mix_norm_gmlp_g, w_out=m_w_out, norm2_g=m_norm2_g,
             w_gate=m_w_gate, w_up=m_w_up, w_down=m_w_down, final_g=m_final_g)
    v = dict(norm1_g=v_norm1_g, w_in=v_w_in, gmlp_ln_g=v_gmlp_ln_g, w_spatial=v_w_spatial, b_spatial=v_b_spatial,
             mix_norm_attn_g=v_mix_norm_attn_g, mix_norm_gmlp_g=v_mix_norm_gmlp_g, w_out=v_w_out, norm2_g=v_norm2_g,
             w_gate=v_w_gate, w_up=v_w_up, w_down=v_w_down, final_g=v_final_g)

    pos = place_vector()
    small = {n: w[n] for n in SMALL}
    tables = step_tables()

    def start_gathers(l, after):
        flights = []
        for kinds in GROUPS:
            flights.append(gather_start([cast_into(w[BIG[k]], l, pos, after) for k in kinds]))
            after = flights[-1][3]
        return flights, after

    def arrive(flight, kinds, after):
        fulls = pair_forward(gather_wait(flight[0], flight[1], flight[2], after))
        big = dict(zip(kinds, fulls))
        if "wout" in big:
            big["wout"] = big["wout"].reshape(D, D)
        if "wd" in big:
            big["wd"] = big["wd"].reshape(FF, D)
        return big

    act = x[0]
    flights, token = start_gathers(0, None)
    big, saved = [], []
    for l in range(NL):
        mine, after_mixer = flights, token
        if l + 1 < NL:
            flights, token = start_gathers(l + 1, act if l else token)
            after_mixer = token
        wl = arrive(mine[0], GROUPS[0], act if l else mine[0][3])
        x_mid, sv = mixer_fwd(act, small, l, wl, tables, after_mixer)
        wl.update(arrive(mine[1], GROUPS[1], x_mid))
        act, sv2 = ffn_fwd(x_mid, small, l, wl)
        big.append(wl)
        saved.append({**sv, **sv2})
    loss8, dx, dxb, dfinal = loss_head(act, _row(small["final_g"]), loss_target[0])
    loss = lax.psum(loss8[0, 0], ("x", "y", "c"))

    grads, delta, new_m, new_v = {}, {}, {}, {}
    stacked = {k: None for k in KINDS}
    gs = {k: [None] * NL for k in SMALL[:-1]}

    def send_off(gbig, kinds, l, after=None):
        return chip_exchange_start([gbig[k] for k in kinds], after) + (kinds, l)

    def finish(flight, after):
        partial_grads, lands = chip_exchange_wait(*flight[:4], after)
        halves = [add_slots(g, got, pos) for g, got in zip(partial_grads, lands)]
        for k, full in zip(flight[5], rs_pair_share(halves)):
            n = BIG[k]
            stacked[k] = adamw_layer(w[n], full.reshape(w[n].shape[1:]), m[n], v[n], flight[6], stacked[k])
        return stacked[flight[5][-1]][1]

    pending, token = [], None
    for l in reversed(range(NL)):
        dx, dxb, gs1, gbig = ffn_bwd(dx, dxb, small, l, big[l], saved[l], token)
        ffn_flight = send_off(gbig, GROUPS[1], l)
        for flight in pending:
            finish(flight, ffn_flight[4])
        dx, dxb, gs2, gbig = mixer_bwd(dx, dxb, small, l, big[l], saved[l], tables, ffn_flight[4])
        for k, g in {**gs1, **gs2}.items():
            gs[k][l] = g
        if l:
            mixer_flight = send_off(gbig, GROUPS[0], l)
            pending, token = [ffn_flight, mixer_flight], mixer_flight[4]
    part = {n: jnp.stack(gs[n]).reshape(w[n].shape) for n in SMALL[:-1]}
    part["final_g"] = dfinal.reshape(w["final_g"].shape)
    small_sum = allreduce_small(_pack(part))
    mixer_flight = send_off(gbig, GROUPS[0], 0, small_sum)
    token = mixer_flight[4]
    for flight in (ffn_flight, mixer_flight):
        token = finish(flight, token)
    for k in KINDS:
        grads[BIG[k]], delta[BIG[k]], new_m[BIG[k]], new_v[BIG[k]] = stacked[k]
    grads.update(_unpack(small_sum, part))

    sm = adamw(_pack({n: w[n] for n in SMALL}), _pack({n: grads[n] for n in SMALL}), _pack({n: m[n] for n in SMALL}),
               _pack({n: v[n] for n in SMALL}))
    for res, packed in zip((delta, new_m, new_v), sm):
        res.update(_unpack(packed, part))

    return (loss, dx.reshape(x.shape), *[grads[n] for n in WEIGHTS], *[delta[n] for n in WEIGHTS],
            *[new_m[n] for n in WEIGHTS], *[new_v[n] for n in WEIGHTS])
```

```python
import functools

import jax
import jax.numpy as jnp
from jax import lax
from jax.experimental import pallas as pl
from jax.experimental.pallas import tpu as pltpu

F32 = jnp.float32
BF16 = jnp.bfloat16

D = 2048
T = 4096
NL = 4
HD = 128
DA = D // 2
DG = D - DA
NH = DA // HD
NG = DG // 128
CH = 128
DIN = 3 * DA + 2 * DG
FF = 5632
DILATIONS = (1, 4, 16)
NSIDE = 64
ROPE_THETA = 10000.0
EPS = 1e-6
NEG = -1e30
SCALE = HD ** -0.5
NCHIP = 4

LR, B1, B2, AEPS, WD, STEP = 0.001, 0.9, 0.999, 1e-08, 0.01, 10

LANES = 128
VMEM_LIMIT = 56 * 1024 * 1024
MESH = pl.DeviceIdType.MESH


def _call(body, *, name, out_shape, grid=(), in_specs=None, out_specs=None, scratch=(), sem=None, aliases=None,
          prefetch=0):
    params = dict(vmem_limit_bytes=VMEM_LIMIT)
    if sem is not None:
        params["dimension_semantics"] = sem
    if prefetch:
        spec = pltpu.PrefetchScalarGridSpec(num_scalar_prefetch=prefetch, grid=grid, in_specs=in_specs,
                                            out_specs=out_specs, scratch_shapes=list(scratch))
        return pl.pallas_call(body, name=name, out_shape=out_shape, grid_spec=spec,
                              input_output_aliases=aliases or {}, compiler_params=pltpu.CompilerParams(**params))
    kw = {}
    if grid:
        kw["grid"] = grid
    return pl.pallas_call(
        body, name=name, out_shape=out_shape, in_specs=in_specs, out_specs=out_specs,
        scratch_shapes=list(scratch), input_output_aliases=aliases or {},
        compiler_params=pltpu.CompilerParams(**params), **kw)


def _sds(shape, dtype):
    return jax.ShapeDtypeStruct(tuple(shape), dtype)


def _dot(a, b):
    return jnp.dot(a, b, preferred_element_type=F32)


def _dot_nt(a, b):
    return lax.dot_general(a, b, (((1,), (1,)), ((), ())), preferred_element_type=F32)


def _dot_tn(a, b):
    return lax.dot_general(a, b, (((0,), (0,)), ((), ())), preferred_element_type=F32)


def _sigmoid(x):
    return 0.5 * jnp.tanh(0.5 * x) + 0.5


def _behind(after):
    arrays = [] if after is None else list(after) if isinstance(after, (list, tuple)) else [after]
    return arrays, [pl.BlockSpec(memory_space=pl.ANY)] * len(arrays)


def rms_fwd(x, g, after=None):
    n, d = x.shape
    tm = 256
    extra, extra_specs = _behind(after)

    def body(x_ref, g_ref, *rest):
        h_ref = rest[-1]
        xv = x_ref[...]
        r = lax.rsqrt(jnp.mean(xv * xv, axis=-1, keepdims=True) + EPS)
        h_ref[...] = (xv * r * g_ref[...]).astype(BF16)

    return _call(body, name="rms_fwd", out_shape=_sds((n, d), BF16), grid=(n // tm,),
                 in_specs=[pl.BlockSpec((tm, d), lambda i: (i, 0)), pl.BlockSpec((1, d), lambda i: (0, 0))] + extra_specs,
                 out_specs=pl.BlockSpec((tm, d), lambda i: (i, 0)), sem=("parallel",))(x, g, *extra)


def rms_bwd(dh, x, g, dres):
    n, d = x.shape
    tm = 256

    def body(dh_ref, x_ref, g_ref, dres_ref, dx_ref, dxb_ref, dg_ref):
        @pl.when(pl.program_id(0) == 0)
        def _():
            dg_ref[...] = jnp.zeros_like(dg_ref)

        xv = x_ref[...]
        r = lax.rsqrt(jnp.mean(xv * xv, axis=-1, keepdims=True) + EPS)
        xhat = xv * r
        dhv = dh_ref[...].astype(F32)
        dg_ref[...] += jnp.sum(dhv * xhat, axis=0, keepdims=True)
        dxn = dhv * g_ref[...]
        dx = dres_ref[...] + r * (dxn - xhat * jnp.mean(dxn * xhat, axis=-1, keepdims=True))
        dx_ref[...] = dx
        dxb_ref[...] = dx.astype(BF16)

    row = pl.BlockSpec((tm, d), lambda i: (i, 0))
    vec = pl.BlockSpec((1, d), lambda i: (0, 0))
    return _call(body, name="rms_bwd", out_shape=(_sds((n, d), F32), _sds((n, d), BF16), _sds((1, d), F32)),
                 grid=(n // tm,), in_specs=[row, row, vec, row], out_specs=(row, row, vec), sem=("arbitrary",))(dh, x, g, dres)


def loss_head(x, g, target):
    n, d = x.shape
    tm = 256

    def body(x_ref, g_ref, t_ref, loss_ref, dx_ref, dxb_ref, dg_ref):
        @pl.when(pl.program_id(0) == 0)
        def _():
            dg_ref[...] = jnp.zeros_like(dg_ref)
            loss_ref[...] = jnp.zeros_like(loss_ref)

        xv = x_ref[...]
        r = lax.rsqrt(jnp.mean(xv * xv, axis=-1, keepdims=True) + EPS)
        xhat = xv * r
        gv = g_ref[...]
        err = xhat * gv - t_ref[...]
        loss_ref[...] += 0.5 * jnp.sum(jnp.mean(err * err, axis=-1, keepdims=True))
        dy = err * (1.0 / d)
        dg_ref[...] += jnp.sum(dy * xhat, axis=0, keepdims=True)
        dxn = dy * gv
        dx = r * (dxn - xhat * jnp.mean(dxn * xhat, axis=-1, keepdims=True))
        dx_ref[...] = dx
        dxb_ref[...] = dx.astype(BF16)

    row = pl.BlockSpec((tm, d), lambda i: (i, 0))
    vec = pl.BlockSpec((1, d), lambda i: (0, 0))
    return _call(body, name="loss_head",
                 out_shape=(_sds((8, LANES), F32), _sds((n, d), F32), _sds((n, d), BF16), _sds((1, d), F32)),
                 grid=(n // tm,), in_specs=[row, vec, row],
                 out_specs=(pl.BlockSpec((8, LANES), lambda i: (0, 0)), row, row, vec), sem=("arbitrary",))(x, g, target)


def mm_cols(a, wc, tm=512):
    n, k = a.shape
    s, _, nq = wc.shape

    def body(a_ref, w_ref, o_ref):
        o_ref[...] = _dot(a_ref[...], w_ref[...]).astype(BF16)

    return _call(body, name="mm_cols", out_shape=_sds((n, s * nq), BF16), grid=(s, n // tm),
                 in_specs=[pl.BlockSpec((tm, k), lambda j, i: (i, 0)), pl.BlockSpec((None, k, nq), lambda j, i: (j, 0, 0))],
                 out_specs=pl.BlockSpec((tm, nq), lambda j, i: (i, j)), sem=("parallel", "parallel"))(a, wc)


def mm_gateup(h, wg, wu, tm=512):
    n, k = h.shape
    s, _, nq = wg.shape

    def body(h_ref, wg_ref, wu_ref, gate_ref, up_ref, ff_ref):
        hv = h_ref[...]
        gate = _dot(hv, wg_ref[...])
        up = _dot(hv, wu_ref[...])
        gate_ref[...] = gate.astype(BF16)
        up_ref[...] = up.astype(BF16)
        ff_ref[...] = (gate * _sigmoid(gate) * up).astype(BF16)

    wspec = pl.BlockSpec((None, k, nq), lambda j, i: (j, 0, 0))
    ospec = pl.BlockSpec((tm, nq), lambda j, i: (i, j))
    o = _sds((n, s * nq), BF16)
    return _call(body, name="mm_gateup", out_shape=(o, o, o), grid=(s, n // tm),
                 in_specs=[pl.BlockSpec((tm, k), lambda j, i: (i, 0)), wspec, wspec],
                 out_specs=(ospec, ospec, ospec), sem=("parallel", "parallel"))(h, wg, wu)


def mm_rows_res(a, wr, res, tm=512, tn=1024):
    n, k = a.shape
    _, nn = wr.shape

    def body(a_ref, w_ref, r_ref, o_ref):
        o_ref[...] = r_ref[...] + _dot(a_ref[...], w_ref[...])

    return _call(body, name="mm_rows_res", out_shape=_sds((n, nn), F32), grid=(nn // tn, n // tm),
                 in_specs=[pl.BlockSpec((tm, k), lambda j, i: (i, 0)), pl.BlockSpec((k, tn), lambda j, i: (0, j)),
                           pl.BlockSpec((tm, tn), lambda j, i: (i, j))],
                 out_specs=pl.BlockSpec((tm, tn), lambda j, i: (i, j)), sem=("parallel", "parallel"))(a, wr, res)


def mm_nt_rows(dy, wr, tk, tm=512, after=None):
    n, nn = dy.shape
    k, _ = wr.shape
    extra, extra_specs = _behind(after)

    def body(dy_ref, w_ref, *rest):
        rest[-1][...] = _dot_nt(dy_ref[...], w_ref[...]).astype(BF16)

    return _call(body, name="mm_nt_rows", out_shape=_sds((n, k), BF16), grid=(k // tk, n // tm),
                 in_specs=[pl.BlockSpec((tm, nn), lambda j, i: (i, 0)), pl.BlockSpec((tk, nn), lambda j, i: (j, 0))]
                 + extra_specs,
                 out_specs=pl.BlockSpec((tm, tk), lambda j, i: (i, j)), sem=("parallel", "parallel"))(dy, wr, *extra)


def mm_dff(dy, wd, gate, up, tk, tm=512, after=None):
    n, nn = dy.shape
    k, _ = wd.shape
    extra, extra_specs = _behind(after)

    def body(dy_ref, w_ref, gate_ref, up_ref, *rest):
        dgate_ref, dup_ref = rest[-2:]
        dff = _dot_nt(dy_ref[...], w_ref[...])
        gate = gate_ref[...].astype(F32)
        up = up_ref[...].astype(F32)
        sig = _sigmoid(gate)
        dgate_ref[...] = (dff * up * (sig * (1.0 + gate * (1.0 - sig)))).astype(BF16)
        dup_ref[...] = (dff * (gate * sig)).astype(BF16)

    tile = pl.BlockSpec((tm, tk), lambda j, i: (i, j))
    o = _sds((n, k), BF16)
    return _call(body, name="mm_dff", out_shape=(o, o), grid=(k // tk, n // tm),
                 in_specs=[pl.BlockSpec((tm, nn), lambda j, i: (i, 0)), pl.BlockSpec((tk, nn), lambda j, i: (j, 0)), tile, tile]
                 + extra_specs,
                 out_specs=(tile, tile), sem=("parallel", "parallel"))(dy, wd, gate, up, *extra)


def mm_nt_cols(das, wcs, tm=512):
    n = das[0].shape[0]
    s, k, nq = wcs[0].shape
    npair = len(das)

    def body(*refs):
        da_refs, w_refs, o_ref, acc = refs[:npair], refs[npair:2 * npair], refs[2 * npair], refs[2 * npair + 1]
        ss = pl.program_id(1)

        @pl.when(ss == 0)
        def _():
            acc[...] = jnp.zeros_like(acc)

        for da_ref, w_ref in zip(da_refs, w_refs):
            acc[...] += _dot_nt(da_ref[...], w_ref[...])

        @pl.when(ss == s - 1)
        def _():
            o_ref[...] = acc[...].astype(BF16)

    return _call(body, name="mm_nt_cols%d" % npair, out_shape=_sds((n, k), BF16), grid=(n // tm, s),
                 in_specs=[pl.BlockSpec((tm, nq), lambda i, ss: (i, ss))] * npair
                 + [pl.BlockSpec((None, k, nq), lambda i, ss: (ss, 0, 0))] * npair,
                 out_specs=pl.BlockSpec((tm, k), lambda i, ss: (i, 0)),
                 scratch=[pltpu.VMEM((tm, k), F32)], sem=("parallel", "arbitrary"))(*das, *wcs)


def mm_tn(a, b, nq, shard_major, tka=512):
    n, ka = a.shape
    nb = b.shape[1] // nq

    def body(a_ref, b_ref, o_ref):
        o_ref[...] = _dot_tn(a_ref[...], b_ref[...]).astype(BF16)

    if shard_major:
        out_shape, out_spec = _sds((nb, ka, nq), BF16), pl.BlockSpec((None, tka, nq), lambda j, i: (j, i, 0))
    else:
        out_shape, out_spec = _sds((ka, nb * nq), BF16), pl.BlockSpec((tka, nq), lambda j, i: (i, j))
    return _call(body, name="mm_tn", out_shape=out_shape, grid=(nb, ka // tka),
                 in_specs=[pl.BlockSpec((n, tka), lambda j, i: (0, i)), pl.BlockSpec((n, nq), lambda j, i: (0, j))],
                 out_specs=out_spec, sem=("parallel", "parallel"))(a, b)


def rope_tables(n):
    pos = jnp.arange(n, dtype=F32)
    inv = ROPE_THETA ** (-jnp.arange(0, HD, 2, dtype=F32) / HD)
    ang = pos[:, None] * inv[None, :]
    cos, sin = jnp.cos(ang), jnp.sin(ang)
    return jnp.concatenate([cos, cos], axis=-1), jnp.concatenate([-sin, sin], axis=-1)


PERM_ROWS = 256


def perm_matrix(d):
    n = PERM_ROWS // d
    i = jnp.arange(PERM_ROWS)
    src = (i % n) * d + i // n
    return (src[:, None] == jnp.arange(PERM_ROWS)[None, :]).astype(BF16)


def _perm_f32(p, x):
    hi = x.astype(BF16)
    rest = x - hi.astype(F32)
    mid = rest.astype(BF16)
    lo = (rest - mid.astype(F32)).astype(BF16)
    return (_dot(p, hi) + _dot(p, mid)) + _dot(p, lo)


def _res_spec(d, cols, col_block=0):
    return pl.BlockSpec((d, PERM_ROWS // d, cols), lambda i: (0, i, col_block))


def _perm_specs():
    return [pl.BlockSpec((PERM_ROWS, PERM_ROWS), lambda i: (0, 0))] * (len(DILATIONS) - 1)


def qkv_split(proj, cos2, sin2, perms):
    n = proj.shape[0]
    tm = PERM_ROWS

    def body(p_ref, c_ref, s_ref, *rest):
        perm_refs, outs = rest[:len(DILATIONS) - 1], rest[len(DILATIONS) - 1:]
        c, s = c_ref[...], s_ref[...]
        for h in range(2 * NH):
            t = p_ref[:, h * HD:(h + 1) * HD].astype(F32)
            outs[0][0, :, h * HD:(h + 1) * HD] = (t * c + pltpu.roll(t, HD // 2, 1) * s).astype(BF16)
        outs[0][0, :, 2 * DA:] = p_ref[:, 2 * DA:]
        nat = outs[0][0]
        for o_ref, p_ref_d, d in zip(outs[1:], perm_refs, DILATIONS[1:]):
            o_ref[...] = _dot(p_ref_d[...], nat).astype(BF16).reshape(d, tm // d, 3 * DA)

    tab = pl.BlockSpec((tm, HD), lambda i: (i, 0))
    return _call(body, name="qkv_split", out_shape=tuple(_sds((d, n // d, 3 * DA), BF16) for d in DILATIONS),
                 grid=(n // tm,), in_specs=[pl.BlockSpec((tm, 3 * DA), lambda i: (i, 0)), tab, tab] + _perm_specs(),
                 out_specs=tuple(_res_spec(d, 3 * DA) for d in DILATIONS), sem=("parallel",))(proj, cos2, sin2, *perms)


def dproj_assemble(dqs, dks, dvs, duv, cos2, sin2, perms_t):
    n = duv.shape[0]
    tm = PERM_ROWS
    npat = len(DILATIONS)

    def body(*refs):
        parts = refs[0:npat], refs[npat:2 * npat], refs[2 * npat:3 * npat]
        duv_ref, c_ref, s_ref = refs[3 * npat:3 * npat + 3]
        pt_refs, o_ref = refs[3 * npat + 3:-1], refs[-1]
        c, s = c_ref[...], s_ref[...]
        for part, part_refs in enumerate(parts):
            t = part_refs[0][0].astype(F32)
            for ref, pt_ref in zip(part_refs[1:], pt_refs):
                t = t + _dot(pt_ref[...], ref[...].reshape(tm, DA))
            if part == 2:
                o_ref[:, 2 * DA:3 * DA] = t.astype(BF16)
                continue
            for h in range(NH):
                th = t[:, h * HD:(h + 1) * HD]
                o_ref[:, part * DA + h * HD:part * DA + (h + 1) * HD] = (th * c - pltpu.roll(th, HD // 2, 1) * s).astype(BF16)
        o_ref[:, 3 * DA:] = duv_ref[...]

    blks = [_res_spec(d, DA) for d in DILATIONS]
    tab = pl.BlockSpec((tm, HD), lambda i: (i, 0))
    return _call(body, name="dproj_assemble", out_shape=_sds((n, DIN), BF16), grid=(n // tm,),
                 in_specs=blks * 3 + [pl.BlockSpec((tm, 2 * DG), lambda i: (i, 0)), tab, tab] + _perm_specs(),
                 out_specs=pl.BlockSpec((tm, DIN), lambda i: (i, 0)), sem=("parallel",))(
        *dqs, *dks, *dvs, duv, cos2, sin2, *perms_t)


ATTN_BLOCK_BYTES = 2 << 20


def _attn_tiles(ls):
    tq = min(128, ls)
    w = min(tq + 2 * NSIDE, ls)
    return tq, w, ls // tq


def _attn_heads(ls):
    return NH if ls * DA * 2 <= ATTN_BLOCK_BYTES else 1


ATTN_UNROLL = 8


def _attn_loop(nt, hb, tile):
    for h in range(hb):
        def trip(t, carry, h=h):
            tile(t, h)
            return carry

        lax.fori_loop(0, nt, trip, 0, unroll=min(nt, ATTN_UNROLL))


def _band(t, tq, w, ls):
    q0 = pl.multiple_of(t * tq, tq)
    ks = pl.multiple_of(jnp.clip(t * tq - NSIDE, 0, ls - w), NSIDE)
    qpos = q0 + lax.broadcasted_iota(jnp.int32, (tq, w), 0)
    kpos = ks + lax.broadcasted_iota(jnp.int32, (tq, w), 1)
    return q0, ks, jnp.abs(kpos - qpos) <= NSIDE


def _attn_specs(ls, hb):
    nhb = NH // hb
    q = pl.BlockSpec((None, ls, hb * HD), lambda g: (g // nhb, 0, g % nhb))
    k = pl.BlockSpec((None, ls, hb * HD), lambda g: (g // nhb, 0, nhb + g % nhb))
    v = pl.BlockSpec((None, ls, hb * HD), lambda g: (g // nhb, 0, 2 * nhb + g % nhb))
    return q, k, v, q


def _lse_spec(ls, hb):
    nhb = NH // hb
    return pl.BlockSpec((None, ls, LANES), lambda g: (g // nhb, 0, 0))


def attn_fwd(qkv):
    d, ls, _ = qkv.shape
    tq, w, nt = _attn_tiles(ls)
    hb = _attn_heads(ls)
    nhb = NH // hb

    def body(q_ref, k_ref, v_ref, o_ref, l_ref):
        first = (pl.program_id(0) % nhb) * hb
        lane = lax.broadcasted_iota(jnp.int32, (tq, LANES), 1)

        @pl.when(first == 0)
        def _():
            l_ref[...] = jnp.zeros_like(l_ref)

        def tile(t, h):
            q0, ks, valid = _band(t, tq, w, ls)
            cols = slice(h * HD, (h + 1) * HD)
            s = _dot_nt(q_ref[pl.ds(q0, tq), cols], k_ref[pl.ds(ks, w), cols]) * SCALE
            s = jnp.where(valid, s, NEG)
            m = jnp.max(s, axis=1, keepdims=True)
            p = jnp.exp(s - m)
            l = jnp.sum(p, axis=1, keepdims=True)
            o = _dot(p.astype(BF16), v_ref[pl.ds(ks, w), cols]) / l
            o_ref[pl.ds(q0, tq), cols] = o.astype(BF16)
            l_ref[pl.ds(q0, tq), :] = jnp.where(lane == first + h, m + jnp.log(l), l_ref[pl.ds(q0, tq), :])

        _attn_loop(nt, hb, tile)

    q, k, v, o = _attn_specs(ls, hb)
    return _call(body, name="attn_fwd_d%d" % d, out_shape=(_sds((d, ls, DA), BF16), _sds((d, ls, LANES), F32)),
                 grid=(d * NH // hb,), in_specs=[q, k, v], out_specs=(o, _lse_spec(ls, hb)), sem=("arbitrary",))(
        qkv, qkv, qkv)


def attn_merge(os_, ls_, ga, perms_t):
    n = os_[0].shape[1]
    tm = PERM_ROWS
    npat = len(DILATIONS)

    def body(*refs):
        o_refs, l_refs = refs[:npat], refs[npat:2 * npat]
        g_ref, pt_refs = refs[2 * npat], refs[2 * npat + 1:3 * npat]
        a_ref, lse_ref, mix_ref = refs[3 * npat:]
        ov = [o_refs[0][0].astype(F32)] + [_dot(pt[...], o[...].reshape(tm, DA)) for o, pt in zip(o_refs[1:], pt_refs)]
        lv = [l_refs[0][0]] + [_perm_f32(pt[...], l[...].reshape(tm, LANES)) for l, pt in zip(l_refs[1:], pt_refs)]
        lmax = functools.reduce(jnp.maximum, lv)
        e = [jnp.exp(x - lmax) for x in lv]
        den = functools.reduce(lambda p, q: p + q, e)
        wts = [ei / den for ei in e]
        lse_ref[...] = lmax + jnp.log(den)
        heads = []
        for h in range(NH):
            cols = slice(h * HD, (h + 1) * HD)
            heads.append(functools.reduce(lambda p, q: p + q, [wt[:, h:h + 1] * oi[:, cols] for wt, oi in zip(wts, ov)]))
        a = jnp.concatenate(heads, axis=1)
        a_ref[...] = a.astype(BF16)
        r = lax.rsqrt(jnp.mean(a * a, axis=-1, keepdims=True) + EPS)
        mix_ref[...] = (a * r * g_ref[...]).astype(BF16)

    blk = pl.BlockSpec((tm, DA), lambda i: (i, 0))
    return _call(body, name="attn_merge", out_shape=(_sds((n, DA), BF16), _sds((n, LANES), F32), _sds((n, D), BF16)),
                 grid=(n // tm,),
                 in_specs=[_res_spec(d, DA) for d in DILATIONS] + [_res_spec(d, LANES) for d in DILATIONS]
                 + [pl.BlockSpec((1, DA), lambda i: (0, 0))] + _perm_specs(),
                 out_specs=(blk, pl.BlockSpec((tm, LANES), lambda i: (i, 0)), blk), sem=("parallel",))(
        *os_, *ls_, ga, *perms_t)


def attn_norm_bwd(dmix, a, lse, ga, perms):
    n = a.shape[0]
    tm = PERM_ROWS
    npat = len(DILATIONS)

    def body(dm_ref, a_ref, lse_ref, g_ref, *rest):
        p_refs = rest[:npat - 1]
        da_refs, a_refs, lse_refs = rest[npat - 1:2 * npat - 1], rest[2 * npat - 1:3 * npat - 2], rest[3 * npat - 2:4 * npat - 3]
        dg_ref = rest[-1]

        @pl.when(pl.program_id(0) == 0)
        def _():
            dg_ref[...] = jnp.zeros_like(dg_ref)

        ab = a_ref[...]
        av = ab.astype(F32)
        r = lax.rsqrt(jnp.mean(av * av, axis=-1, keepdims=True) + EPS)
        ahat = av * r
        dm = dm_ref[...].astype(F32)
        dg_ref[...] += jnp.sum(dm * ahat, axis=0, keepdims=True)
        dn = dm * g_ref[...]
        da = (r * (dn - ahat * jnp.mean(dn * ahat, axis=-1, keepdims=True))).astype(BF16)
        da_refs[0][0] = da
        lv = lse_ref[...]
        for i, d in enumerate(DILATIONS[1:]):
            p = p_refs[i][...]
            da_refs[i + 1][...] = _dot(p, da).astype(BF16).reshape(d, tm // d, DA)
            a_refs[i][...] = _dot(p, ab).astype(BF16).reshape(d, tm // d, DA)
            lse_refs[i][...] = _perm_f32(p, lv).reshape(d, tm // d, LANES)

    blk = pl.BlockSpec((tm, DA), lambda i: (i, 0))
    vec = pl.BlockSpec((1, DA), lambda i: (0, 0))
    res = [_res_spec(d, DA) for d in DILATIONS]
    outs = _call(body, name="attn_norm_bwd",
                 out_shape=tuple([_sds((d, n // d, DA), BF16) for d in DILATIONS]
                                 + [_sds((d, n // d, DA), BF16) for d in DILATIONS[1:]]
                                 + [_sds((d, n // d, LANES), F32) for d in DILATIONS[1:]] + [_sds((1, DA), F32)]),
                 grid=(n // tm,),
                 in_specs=[blk, blk, pl.BlockSpec((tm, LANES), lambda i: (i, 0)), vec] + _perm_specs(),
                 out_specs=tuple(res + res[1:] + [_res_spec(d, LANES) for d in DILATIONS[1:]] + [vec]),
                 sem=("arbitrary",))(dmix, a, lse, ga, *perms)
    return outs[:npat], outs[npat:2 * npat - 1], outs[2 * npat - 1:3 * npat - 2], outs[-1]


def attn_bwd(qkv, a, da, lse):
    d, ls, _ = qkv.shape
    tq, w, nt = _attn_tiles(ls)
    hb = _attn_heads(ls)

    nhb = NH // hb

    def body(q_ref, k_ref, v_ref, a_ref, da_ref, lse_ref, dq_ref, dk_ref, dv_ref, dk_acc, dv_acc):
        first = (pl.program_id(0) % nhb) * hb
        lane = lax.broadcasted_iota(jnp.int32, (tq, LANES), 1)
        dk_acc[...] = jnp.zeros_like(dk_acc)
        dv_acc[...] = jnp.zeros_like(dv_acc)

        def tile(t, h):
            q0, ks, valid = _band(t, tq, w, ls)
            cols = slice(h * HD, (h + 1) * HD)
            q = q_ref[pl.ds(q0, tq), cols]
            k = k_ref[pl.ds(ks, w), cols]
            v = v_ref[pl.ds(ks, w), cols]
            do = da_ref[pl.ds(q0, tq), cols]
            if nhb == 1:
                lse_h = lse_ref[pl.ds(q0, tq), h:h + 1]
            else:
                lse_h = jnp.sum(jnp.where(lane == first + h, lse_ref[pl.ds(q0, tq), :], 0.0), axis=1, keepdims=True)
            s = jnp.where(valid, _dot_nt(q, k) * SCALE, NEG)
            p = jnp.exp(s - lse_h)
            drow = jnp.sum(do.astype(F32) * a_ref[pl.ds(q0, tq), cols].astype(F32), axis=1, keepdims=True)
            ds = (p * (_dot_nt(do, v) - drow) * SCALE).astype(BF16)
            dv_acc[pl.ds(ks, w), cols] += _dot_tn(p.astype(BF16), do)
            dk_acc[pl.ds(ks, w), cols] += _dot_tn(ds, q)
            dq_ref[pl.ds(q0, tq), cols] = _dot(ds, k).astype(BF16)

        _attn_loop(nt, hb, tile)
        dk_ref[...] = dk_acc[...].astype(BF16)
        dv_ref[...] = dv_acc[...].astype(BF16)

    q, k, v, o = _attn_specs(ls, hb)
    out = _sds((d, ls, DA), BF16)
    return _call(body, name="attn_bwd_d%d" % d, out_shape=(out, out, out), grid=(d * NH // hb,),
                 in_specs=[q, k, v, o, o, _lse_spec(ls, hb)], out_specs=(o, o, o),
                 scratch=[pltpu.VMEM((ls, hb * HD), F32), pltpu.VMEM((ls, hb * HD), F32)], sem=("parallel",))(
        qkv, qkv, qkv, a, da, lse)


GM_TM = 256
INV_SQRT2 = 0.7071067811865476
INV_SQRT2PI = 0.3989422804014327


def _gelu(x):
    return 0.5 * x * (1.0 + lax.erf(x * INV_SQRT2))


def _gelu_grad(x):
    return 0.5 * (1.0 + lax.erf(x * INV_SQRT2)) + x * (INV_SQRT2PI * jnp.exp(-0.5 * x * x))


def _gmlp_forward(up, vp, ln_g, ws_ref, bias):
    u = _gelu(up)
    v = _gelu(vp)
    vc = v - jnp.mean(v, axis=-1, keepdims=True)
    rs = lax.rsqrt(jnp.mean(vc * vc, axis=-1, keepdims=True) + EPS)
    vhat = vc * rs
    vln = (vhat * ln_g).astype(BF16)
    rows = []
    for c in range(GM_TM // CH):
        cols = [_dot(ws_ref[g], vln[c * CH:(c + 1) * CH, g * 128:(g + 1) * 128]) for g in range(NG)]
        rows.append(jnp.concatenate(cols, axis=1) + bias)
    return u, vhat, rs, vln, jnp.concatenate(rows, axis=0)


def _gmlp_specs():
    ublk = pl.BlockSpec((GM_TM, DG), lambda i: (i, 3 * DA // DG))
    vblk = pl.BlockSpec((GM_TM, DG), lambda i: (i, 3 * DA // DG + 1))
    vec = pl.BlockSpec((1, DG), lambda i: (0, 0))
    wsp = pl.BlockSpec((NG, CH, CH), lambda i: (0, 0, 0))
    bsp = pl.BlockSpec((CH, DG), lambda i: (0, 0))
    return ublk, vblk, vec, wsp, bsp


def gmlp_fwd(proj, ln_g, ws, bias, gg, mix):
    n = proj.shape[0]

    def body(up_ref, vp_ref, ln_ref, ws_ref, b_ref, gg_ref, mix_in, mix_ref):
        del mix_in
        u, _, _, _, mixed = _gmlp_forward(up_ref[...].astype(F32), vp_ref[...].astype(F32), ln_ref[...], ws_ref, b_ref[...])
        gout = u * mixed
        r = lax.rsqrt(jnp.mean(gout * gout, axis=-1, keepdims=True) + EPS)
        mix_ref[...] = (gout * r * gg_ref[...]).astype(BF16)

    ublk, vblk, vec, wsp, bsp = _gmlp_specs()
    return _call(body, name="gmlp_fwd", out_shape=_sds((n, D), BF16), grid=(n // GM_TM,),
                 in_specs=[ublk, vblk, vec, wsp, bsp, vec, pl.BlockSpec(memory_space=pl.ANY)],
                 out_specs=pl.BlockSpec((GM_TM, DG), lambda i: (i, DA // DG)), sem=("parallel",), aliases={6: 0})(
        proj, proj, ln_g, ws, bias, gg, mix)


def gmlp_bwd(proj, dmix, ln_g, ws, wst, bias, gg):
    n = proj.shape[0]

    def body(up_ref, vp_ref, dm_ref, ln_ref, ws_ref, wst_ref, b_ref, gg_ref, duv_ref, dln_ref, dws_ref, dbs_ref, dgg_ref,
             db_ref):
        @pl.when(pl.program_id(0) == 0)
        def _():
            dln_ref[...] = jnp.zeros_like(dln_ref)
            dws_ref[...] = jnp.zeros_like(dws_ref)
            db_ref[...] = jnp.zeros_like(db_ref)
            dgg_ref[...] = jnp.zeros_like(dgg_ref)

        up = up_ref[...].astype(F32)
        vp = vp_ref[...].astype(F32)
        ln_g = ln_ref[...]
        u, vhat, rs, vln, mixed = _gmlp_forward(up, vp, ln_g, ws_ref, b_ref[...])
        gout = u * mixed
        r = lax.rsqrt(jnp.mean(gout * gout, axis=-1, keepdims=True) + EPS)
        ghat = gout * r
        dm = dm_ref[...].astype(F32)
        dgg_ref[...] += jnp.sum(dm * ghat, axis=0, keepdims=True)
        dn = dm * gg_ref[...]
        dgout = r * (dn - ghat * jnp.mean(dn * ghat, axis=-1, keepdims=True))
        du = dgout * mixed
        dmixed = dgout * u
        dmb = dmixed.astype(BF16)
        rows = []
        for c in range(GM_TM // CH):
            rsl = slice(c * CH, (c + 1) * CH)
            db_ref[...] += dmixed[rsl, :]
            cols = []
            for g in range(NG):
                csl = slice(g * 128, (g + 1) * 128)
                dws_ref[g] += _dot_nt(dmb[rsl, csl], vln[rsl, csl])
                cols.append(_dot(wst_ref[g], dmb[rsl, csl]))
            rows.append(jnp.concatenate(cols, axis=1))
        dvln = jnp.concatenate(rows, axis=0)
        dln_ref[...] += jnp.sum(dvln * vhat, axis=0, keepdims=True)
        dvh = dvln * ln_g
        dv = rs * (dvh - jnp.mean(dvh, axis=-1, keepdims=True) - vhat * jnp.mean(dvh * vhat, axis=-1, keepdims=True))
        duv_ref[:, 0:DG] = (du * _gelu_grad(up)).astype(BF16)
        duv_ref[:, DG:] = (dv * _gelu_grad(vp)).astype(BF16)

        @pl.when(pl.program_id(0) == pl.num_programs(0) - 1)
        def _():
            for g in range(NG):
                dbs_ref[g:g + 1, :] = jnp.sum(jnp.transpose(db_ref[:, g * 128:(g + 1) * 128]), axis=0, keepdims=True)

    ublk, vblk, vec, wsp, bsp = _gmlp_specs()
    return _call(body, name="gmlp_bwd",
                 out_shape=(_sds((n, 2 * DG), BF16), _sds((1, DG), F32), _sds((NG, CH, CH), F32), _sds((NG, CH), F32),
                            _sds((1, DG), F32)),
                 grid=(n // GM_TM,),
                 in_specs=[ublk, vblk, pl.BlockSpec((GM_TM, DG), lambda i: (i, DA // DG)), vec, wsp, wsp, bsp, vec],
                 out_specs=(pl.BlockSpec((GM_TM, 2 * DG), lambda i: (i, 0)), vec, wsp,
                            pl.BlockSpec((NG, CH), lambda i: (0, 0)), vec),
                 scratch=[pltpu.VMEM((CH, DG), F32)], sem=("arbitrary",))(
        proj, proj, dmix, ln_g, ws, wst, bias, gg)


def _row(v):
    return v.reshape(1, -1)


def local_step(x, target, small, big):
    tables = step_tables()
    saved = []
    for l in range(NL):
        x_mid, sv = mixer_fwd(x, small, l, big[l], tables)
        x, sv2 = ffn_fwd(x_mid, small, l, big[l])
        saved.append({**sv, **sv2})
    loss8, dx, dxb, dfinal = loss_head(x, _row(small["final_g"]), target)
    gs = {k: [None] * NL for k in SMALL[:-1]}
    gbig = [None] * NL
    for l in reversed(range(NL)):
        dx, dxb, gs1, gb1 = ffn_bwd(dx, dxb, small, l, big[l], saved[l])
        dx, dxb, gs2, gb2 = mixer_bwd(dx, dxb, small, l, big[l], saved[l], tables)
        gbig[l] = {**gb1, **gb2}
        for k, g in {**gs1, **gs2}.items():
            gs[k][l] = g
    return loss8, dx, gs, dfinal, gbig


def step_tables():
    perms = [perm_matrix(d) for d in DILATIONS[1:]]
    return rope_tables(T) + (perms, [p.T for p in perms])


def mixer_fwd(x, small, l, w, tables, after=None):
    cos2, sin2, perms, perms_t = tables
    ws_b = small["w_spatial"][l].astype(BF16)
    bias = jnp.repeat(small["b_spatial"][l].T, 128, axis=1)
    h = rms_fwd(x, _row(small["norm1_g"][l]), after)
    proj = mm_cols(h, w["win"])
    qkvs = qkv_split(proj, cos2, sin2, perms)
    outs = [attn_fwd(qkv) for qkv in qkvs]
    a, lse, mix = attn_merge([o for o, _ in outs], [s for _, s in outs], _row(small["mix_norm_attn_g"][l]), perms_t)
    mix = gmlp_fwd(proj, _row(small["gmlp_ln_g"][l]), ws_b, bias, _row(small["mix_norm_gmlp_g"][l]), mix)
    x_mid = mm_rows_res(mix, w["wout"], x)
    return x_mid, dict(x=x, h=h, proj=proj, qkvs=qkvs, a=a, lse=lse, mix=mix, ws_b=ws_b, bias=bias)


def ffn_fwd(x_mid, small, l, w, after=None):
    h2 = rms_fwd(x_mid, _row(small["norm2_g"][l]), after)
    gate, up, ff = mm_gateup(h2, w["wg"], w["wu"])
    x_out = mm_rows_res(ff, w["wd"], x_mid)
    return x_out, dict(x_mid=x_mid, h2=h2, gate=gate, up=up, ff=ff)


def ffn_bwd(dx, dxb, small, l, w, sv, after=None):
    gs = {}
    dgate, dup = mm_dff(dxb, w["wd"], sv["gate"], sv["up"], tk=FF // NCHIP, after=after)
    g_wd = mm_tn(sv["ff"], dxb, D // 2, False)
    dh2 = mm_nt_cols([dgate, dup], [w["wg"], w["wu"]])
    g_wg = mm_tn(sv["h2"], dgate, FF // NCHIP, True)
    g_wu = mm_tn(sv["h2"], dup, FF // NCHIP, True)
    dx, dxb, gs["norm2_g"] = rms_bwd(dh2, sv["x_mid"], _row(small["norm2_g"][l]), dx)
    return dx, dxb, gs, dict(wg=g_wg, wu=g_wu, wd=g_wd.reshape(NCHIP, FF // NCHIP, D))


def mixer_bwd(dx, dxb, small, l, w, sv, tables, after=None):
    cos2, sin2, perms, perms_t = tables
    gs = {}
    dmix = mm_nt_rows(dxb, w["wout"], tk=D // 2, after=after)
    g_wout = mm_tn(sv["mix"], dxb, D // 2, False)
    das, a_res, lse_res, gs["mix_norm_attn_g"] = attn_norm_bwd(dmix, sv["a"], sv["lse"],
                                                                _row(small["mix_norm_attn_g"][l]), perms)
    a_all = [sv["a"].reshape((1,) + sv["a"].shape)] + list(a_res)
    lse_all = [sv["lse"].reshape((1,) + sv["lse"].shape)] + list(lse_res)
    parts = [attn_bwd(*operands) for operands in zip(sv["qkvs"], a_all, das, lse_all)]
    wst = jnp.swapaxes(small["w_spatial"][l], 1, 2).astype(BF16)
    duv, gs["gmlp_ln_g"], gs["w_spatial"], gs["b_spatial"], gs["mix_norm_gmlp_g"] = gmlp_bwd(
        sv["proj"], dmix, _row(small["gmlp_ln_g"][l]), sv["ws_b"], wst, sv["bias"], _row(small["mix_norm_gmlp_g"][l]))
    dproj = dproj_assemble([p[0] for p in parts], [p[1] for p in parts], [p[2] for p in parts], duv, cos2, sin2,
                           perms_t)
    dh = mm_nt_cols([dproj], [w["win"]])
    g_win = mm_tn(sv["h"], dproj, DIN // NCHIP, True)
    dx, dxb, gs["norm1_g"] = rms_bwd(dh, sv["x"], _row(small["norm1_g"][l]), dx)
    return dx, dxb, gs, dict(win=g_win, wout=g_wout.reshape(NCHIP, D // NCHIP, D))


KINDS = ("win", "wout", "wg", "wu", "wd")
GROUPS = (("win", "wout"), ("wg", "wu", "wd"))
ANY = pl.BlockSpec(memory_space=pl.ANY)


def _place():
    return lax.axis_index("x"), lax.axis_index("y"), lax.axis_index("c")


def _other_chips(x, y):
    return [(1 - x, y), (x, 1 - y), (1 - x, 1 - y)]


def _remote(src, dst, send_sem, recv_sem, to):
    return pltpu.make_async_remote_copy(src_ref=src, dst_ref=dst, send_sem=send_sem, recv_sem=recv_sem,
                                        device_id=to, device_id_type=MESH)


def _hbm_call(body, *, name, n_in, out_shape, scratch, in_place=False):
    return pl.pallas_call(body, name=name, out_shape=tuple(out_shape), in_specs=[ANY] * n_in,
                          out_specs=tuple(ANY for _ in out_shape), scratch_shapes=list(scratch),
                          input_output_aliases={k: k for k in range(n_in)} if in_place else {},
                          compiler_params=pltpu.CompilerParams(vmem_limit_bytes=VMEM_LIMIT))


def place_vector():
    x, y, c = _place()
    return jnp.stack([c, 2 * x + y] + [2 * cx + cy for cx, cy in _other_chips(x, y)]).astype(jnp.int32)


PUSH_ROWS = 512
PUSH_SLOTS = 3


def _push_tiles(src, dst, buf, load_sem, send_sem, recv_sem, to, src_at, dst_at, nt):
    tr = buf.shape[1]
    n = len(src_at) * nt

    def pick(vals, seg):
        out = vals[0]
        for q in range(1, len(vals)):
            out = jnp.where(seg == q, vals[q], out)
        return out

    def tile(ref, at, t):
        seg = t // nt
        row = pl.multiple_of(pick([a[1] for a in at], seg) + (t - seg * nt) * tr, 16)
        return ref.at[pick([a[0] for a in at], seg), pl.ds(row, tr), :]

    def load(t):
        slot = t % PUSH_SLOTS
        return pltpu.make_async_copy(tile(src, src_at, t), buf.at[slot], load_sem.at[slot])

    def send(t):
        slot = t % PUSH_SLOTS
        return _remote(buf.at[slot], tile(dst, dst_at, t), send_sem.at[slot], recv_sem, to)

    load(0).start()

    def step(t, carry):
        load(t).wait()
        send(t).start()

        @pl.when(t >= PUSH_SLOTS - 1)
        def _():
            send(t - (PUSH_SLOTS - 1)).wait_send()

        @pl.when(t + 1 < n)
        def _():
            load(t + 1).start()

        return carry

    lax.fori_loop(0, n, step, 0)
    for t in range(max(n - (PUSH_SLOTS - 1), 0), n):
        send(t).wait_send()


def _await_tiles(dst, send_sem, recv_sem, to, nseg, rows):
    whole = dst.at[pl.ds(0, nseg), pl.ds(0, rows), :]
    _remote(whole, whole, send_sem.at[0], recv_sem, to).wait_recv()


def _push_rows(seg_rows):
    return _row_tile(seg_rows, PUSH_ROWS)


def _push_scratch(arrs, seg_rows):
    return ([pltpu.VMEM((PUSH_SLOTS, _push_rows(r), a.shape[-1]), a.dtype) for a, r in zip(arrs, seg_rows)]
            + [pltpu.SemaphoreType.DMA((len(arrs), PUSH_SLOTS)), pltpu.SemaphoreType.DMA((len(arrs), PUSH_SLOTS)),
               pltpu.SemaphoreType.DMA((len(arrs),))])


def cast_into(w, l, pos, after=None):
    _, r, cols = w.shape
    tr = _row_tile(r)
    extra, extra_specs = _behind(after)

    def body(pos_ref, w_ref, *rest):
        del pos_ref
        rest[-1][...] = w_ref[...].astype(BF16)

    return _call(body, name="cast_into", out_shape=_sds((NCHIP, r, cols), BF16), grid=(r // tr,),
                 in_specs=[pl.BlockSpec((None, tr, cols), lambda i, pos: (l, i, 0))] + extra_specs,
                 out_specs=pl.BlockSpec((None, tr, cols), lambda i, pos: (pos[1], i, 0)),
                 sem=("parallel",), prefetch=1)(pos, w, *extra)


HBM = pl.BlockSpec(memory_space=pltpu.HBM)
SEM = pl.BlockSpec(memory_space=pltpu.SEMAPHORE)
EFFECT = pltpu.SideEffectType.DATAFLOW_SIDE_EFFECTING


def _in_hbm(a):
    return pltpu.with_memory_space_constraint(a, pltpu.HBM)


def _gather_copies(bufs, send, recv):
    x, y, c = _place()
    chips = _other_chips(x, y)

    def half(k, chip):
        hr = bufs[k].shape[1] // 2
        return bufs[k].at[chip, pl.ds(c * hr, hr), :]

    out, back = [], []
    for k in range(len(bufs)):
        for j, (cx, cy) in enumerate(chips):
            i = 3 * k + j
            out.append(_remote(half(k, 2 * x + y), half(k, 2 * x + y), send.at[i], recv.at[i], (cx, cy, c)))
            back.append(_remote(half(k, 2 * x + y), half(k, 2 * cx + cy), send.at[i], recv.at[i], (cx, cy, c)))
    return out, back


def gather_start(fulls):
    nk = len(fulls)

    def body(*refs):
        send, recv = refs[nk], refs[nk + 1]
        bufs, token = refs[nk + 2:2 * nk + 2], refs[2 * nk + 2]
        for cp in _gather_copies(bufs, send, recv)[0]:
            cp.start()
        token[...] = jnp.zeros_like(token)

    res = pl.pallas_call(
        body, name="gather_start",
        out_shape=(pltpu.SemaphoreType.DMA((3 * nk,)), pltpu.SemaphoreType.DMA((3 * nk,)),
                   *[pltpu.HBM(f.shape, f.dtype) for f in fulls], _sds((8, LANES), F32)),
        in_specs=[HBM] * nk, out_specs=(SEM, SEM, *[HBM] * nk, pl.BlockSpec(memory_space=pltpu.VMEM)),
        input_output_aliases={k: 2 + k for k in range(nk)},
        compiler_params=pltpu.CompilerParams(has_side_effects=EFFECT))(*[_in_hbm(f) for f in fulls])
    return res[0], res[1], list(res[2:2 + nk]), res[2 + nk]


def gather_wait(send, recv, fulls, after):
    nk = len(fulls)

    def body(*refs):
        bufs, send_ref, recv_ref = refs[:nk], refs[nk], refs[nk + 1]
        for cp in _gather_copies(bufs, send_ref, recv_ref)[1]:
            cp.wait_send()
            cp.wait_recv()

    return list(pl.pallas_call(
        body, name="gather_wait", out_shape=tuple(pltpu.HBM(f.shape, f.dtype) for f in fulls),
        in_specs=[HBM] * nk + [SEM, SEM, ANY], out_specs=tuple([HBM] * nk),
        input_output_aliases={k: k for k in range(nk)},
        compiler_params=pltpu.CompilerParams(has_side_effects=EFFECT))(*fulls, send, recv, after))


def pair_forward(fulls):
    nk = len(fulls)

    def body(*refs):
        bufs = refs[nk:2 * nk]
        stage = refs[2 * nk:3 * nk]
        load_sem, send_sem, recv_sem = refs[3 * nk:]
        x, y, c = _place()
        sib = (x, y, 1 - c)
        chips = [2 * cx + cy for cx, cy in _other_chips(x, y)]
        for k in range(nk):
            hr = bufs[k].shape[1] // 2
            at = [(chip, c * hr) for chip in chips]
            _push_tiles(bufs[k], bufs[k], stage[k], load_sem.at[k], send_sem.at[k], recv_sem.at[k], sib, at, at,
                        hr // _push_rows(hr))
        for k in range(nk):
            _await_tiles(bufs[k], send_sem.at[k], recv_sem.at[k], sib, 3, bufs[k].shape[1] // 2)

    return _hbm_call(body, name="pair_forward", n_in=nk, out_shape=[_sds(f.shape, f.dtype) for f in fulls], in_place=True,
                     scratch=_push_scratch(fulls, [f.shape[1] // 2 for f in fulls]))(*fulls)


N_PEERS = 7
PEER_FLIPS = [(fx, fy, fc) for fx in (0, 1) for fy in (0, 1) for fc in (0, 1)][1:]


def _chip_copies(ps, lands, send, recv):
    x, y, c = _place()
    cps = []
    for k in range(len(ps)):
        hr = ps[k].shape[1] // 2
        for r, (fx, fy, fc) in enumerate(PEER_FLIPS):
            px, py, pc = x ^ fx, y ^ fy, c ^ fc
            cps.append(_remote(ps[k].at[2 * px + py, pl.ds(pc * hr, hr), :], lands[k].at[r],
                               send.at[N_PEERS * k + r], recv.at[N_PEERS * k + r], (px, py, pc)))
    return cps


def chip_exchange_start(ps, after=None):
    nk = len(ps)
    lands = [lax.empty((N_PEERS, p.shape[1] // 2, p.shape[2]), p.dtype) for p in ps]
    extra, extra_specs = _behind(after)

    def body(*refs):
        refs = refs[2 * nk + len(extra):]
        send, recv = refs[0], refs[1]
        srcs, dsts, token = refs[2:nk + 2], refs[nk + 2:2 * nk + 2], refs[2 * nk + 2]
        for cp in _chip_copies(srcs, dsts, send, recv):
            cp.start()
        token[...] = jnp.zeros_like(token)

    res = pl.pallas_call(
        body, name="chip_exchange_start",
        out_shape=(pltpu.SemaphoreType.DMA((N_PEERS * nk,)), pltpu.SemaphoreType.DMA((N_PEERS * nk,)),
                   *[pltpu.HBM(a.shape, a.dtype) for a in ps + lands], _sds((8, LANES), F32)),
        in_specs=[HBM] * (2 * nk) + extra_specs,
        out_specs=(SEM, SEM, *[HBM] * (2 * nk), pl.BlockSpec(memory_space=pltpu.VMEM)),
        input_output_aliases={k: 2 + k for k in range(2 * nk)},
        compiler_params=pltpu.CompilerParams(has_side_effects=EFFECT))(*[_in_hbm(a) for a in ps + lands], *extra)
    return res[0], res[1], list(res[2:2 + nk]), list(res[2 + nk:2 + 2 * nk]), res[2 + 2 * nk]


def chip_exchange_wait(send, recv, ps, lands, after):
    nk = len(ps)

    def body(*refs):
        srcs, dsts, send_ref, recv_ref = refs[:nk], refs[nk:2 * nk], refs[2 * nk], refs[2 * nk + 1]
        for cp in _chip_copies(srcs, dsts, send_ref, recv_ref):
            cp.wait_send()
            cp.wait_recv()

    res = pl.pallas_call(
        body, name="chip_exchange_wait", out_shape=tuple(pltpu.HBM(a.shape, a.dtype) for a in ps + lands),
        in_specs=[HBM] * (2 * nk) + [SEM, SEM, ANY], out_specs=tuple([HBM] * (2 * nk)),
        input_output_aliases={k: k for k in range(2 * nk)},
        compiler_params=pltpu.CompilerParams(has_side_effects=EFFECT))(*ps, *lands, send, recv, after)
    return list(res[:nk]), list(res[nk:])


def rs_pair_share(fulls):
    nk = len(fulls)

    def body(*refs):
        bufs = refs[nk:2 * nk]
        stage = refs[2 * nk:3 * nk]
        load_sem, send_sem, recv_sem = refs[3 * nk:]
        x, y, c = _place()
        sib = (x, y, 1 - c)
        for k in range(nk):
            hr = bufs[k].shape[1]
            _push_tiles(bufs[k], bufs[k], stage[k], load_sem.at[k], send_sem.at[k], recv_sem.at[k], sib,
                        [(c, 0)], [(c, 0)], hr // _push_rows(hr))
        for k in range(nk):
            _await_tiles(bufs[k], send_sem.at[k], recv_sem.at[k], sib, 1, bufs[k].shape[1])

    return _hbm_call(body, name="rs_pair_share", n_in=nk, out_shape=[_sds(f.shape, f.dtype) for f in fulls], in_place=True,
                     scratch=_push_scratch(fulls, [f.shape[1] for f in fulls]))(*fulls)


N_DEV = 8


def allreduce_small(buf, after=None):
    rows = buf.shape[0]
    rp = rows // N_DEV
    extra, extra_specs = _behind(after)

    def body(in_ref, *rest):
        out_ref, land_ref, send, recv = rest[-4:]
        x, y, c = _place()
        mine = pl.ds(pl.multiple_of((4 * x + 2 * y + c) * rp, 8), rp)
        peers = [(x ^ fx, y ^ fy, c ^ fc) for fx in (0, 1) for fy in (0, 1) for fc in (0, 1)][1:]
        block = [pl.ds(pl.multiple_of((4 * px + 2 * py + pc) * rp, 8), rp) for px, py, pc in peers]
        scatter = [_remote(in_ref.at[block[r], :], land_ref.at[r], send.at[r], recv.at[r], peers[r]) for r in range(7)]
        for cp in scatter:
            cp.start()
        for cp in scatter:
            cp.wait()
        acc = in_ref[mine, :]
        for r in range(7):
            acc = acc + land_ref[r]
        out_ref[mine, :] = acc
        spread = [_remote(out_ref.at[mine, :], out_ref.at[mine, :], send.at[7 + r], recv.at[7 + r], peers[r]) for r in range(7)]
        for cp in spread:
            cp.start()
        for r in range(7):
            _remote(out_ref.at[block[r], :], out_ref.at[block[r], :], send.at[7 + r], recv.at[7 + r], peers[r]).wait_recv()
        for cp in spread:
            cp.wait_send()

    vm = pl.BlockSpec(memory_space=pltpu.VMEM)
    return pl.pallas_call(body, name="allreduce_small", out_shape=_sds(buf.shape, F32), in_specs=[vm] + extra_specs,
                          out_specs=vm,
                          scratch_shapes=[pltpu.VMEM((7, rp, LANES), F32), pltpu.SemaphoreType.DMA((14,)),
                                          pltpu.SemaphoreType.DMA((14,))])(buf, *extra)


def _row_tile(rows, cap=512):
    return max(t for t in range(16, cap + 1, 16) if rows % t == 0)


def add_slots(g, got, pos):
    _, r, cols = g.shape
    hr = r // 2
    tr = _row_tile(hr, 256)
    nt = hr // tr

    def body(pos_ref, o_ref, *rest):
        del pos_ref
        acc = o_ref[...].astype(F32)
        for ref in rest[:N_PEERS]:
            acc = acc + ref[...].astype(F32)
        rest[N_PEERS][...] = acc.astype(BF16)

    slots = [pl.BlockSpec((None, tr, cols), functools.partial(lambda j, i, pos: (j, i, 0), j)) for j in range(N_PEERS)]
    return _call(body, name="add_slots", out_shape=_sds((2, hr, cols), BF16), grid=(nt,),
                 in_specs=[pl.BlockSpec((None, tr, cols), lambda i, pos: (pos[1], pos[0] * nt + i, 0))] + slots,
                 out_specs=pl.BlockSpec((None, tr, cols), lambda i, pos: (pos[0], i, 0)),
                 sem=("parallel",), prefetch=1)(pos, g, *[got] * N_PEERS)


def _adamw_math(w, g, m, v):
    nm = B1 * m + (1.0 - B1) * g
    nv = B2 * v + (1.0 - B2) * (g * g)
    m_hat = nm / (1.0 - B1 ** STEP)
    v_hat = nv / (1.0 - B2 ** STEP)
    return -LR * (m_hat / (jnp.sqrt(v_hat) + AEPS) + WD * w), nm, nv


def adamw_layer(w, g, m, v, l, prev):
    nl, r, cols = w.shape
    tr = _row_tile(r, 256)

    def body(w_ref, g_ref, m_ref, v_ref, *rest):
        go_ref, d_ref, nm_ref, nv_ref = rest[-4:]
        gv = g_ref[...].astype(F32)
        d, nm, nv = _adamw_math(w_ref[...], gv, m_ref[...], v_ref[...])
        go_ref[...] = gv
        d_ref[...] = d
        nm_ref[...] = nm
        nv_ref[...] = nv

    lay = pl.BlockSpec((None, tr, cols), lambda i: (l, i, 0))
    o = _sds((nl, r, cols), F32)
    extra = [] if prev is None else list(prev)
    return _call(body, name="adamw_layer", out_shape=(o, o, o, o), grid=(r // tr,),
                 in_specs=[lay, pl.BlockSpec((tr, cols), lambda i: (i, 0)), lay, lay] + [ANY] * len(extra),
                 out_specs=(lay, lay, lay, lay), sem=("parallel",),
                 aliases={4 + j: j for j in range(len(extra))})(w, g, m, v, *extra)


def adamw(w, g, m, v):
    shape = w.shape
    cols = shape[-1]
    rows = w.size // cols
    tr = _row_tile(rows, 256)

    def body(w_ref, g_ref, m_ref, v_ref, d_ref, nm_ref, nv_ref):
        d_ref[...], nm_ref[...], nv_ref[...] = _adamw_math(w_ref[...], g_ref[...], m_ref[...], v_ref[...])

    blk = pl.BlockSpec((tr, cols), lambda i: (i, 0))
    o = _sds((rows, cols), F32)
    outs = _call(body, name="adamw", out_shape=(o, o, o), grid=(rows // tr,), in_specs=[blk] * 4, out_specs=(blk, blk, blk),
                 sem=("parallel",))(*[t.reshape(rows, cols) for t in (w, g, m, v)])
    return [t.reshape(shape) for t in outs]


SMALL = ("norm1_g", "gmlp_ln_g", "w_spatial", "b_spatial", "mix_norm_attn_g", "mix_norm_gmlp_g", "norm2_g", "final_g")
WEIGHTS = ("norm1_g", "w_in", "gmlp_ln_g", "w_spatial", "b_spatial", "mix_norm_attn_g", "mix_norm_gmlp_g", "w_out",
           "norm2_g", "w_gate", "w_up", "w_down", "final_g")
BIG = dict(win="w_in", wout="w_out", wg="w_gate", wu="w_up", wd="w_down")


def _pack(parts):
    flat = jnp.concatenate([parts[n].reshape(-1) for n in SMALL])
    rows = -(-flat.shape[0] // (LANES * 8 * N_DEV)) * 8 * N_DEV
    return jnp.pad(flat, (0, rows * LANES - flat.shape[0])).reshape(rows, LANES)


def _unpack(buf, like):
    flat = buf.reshape(-1)
    out, at = {}, 0
    for n in SMALL:
        out[n] = flat[at:at + like[n].size].reshape(like[n].shape)
        at += like[n].size
    return out


def kernel(x, norm1_g, w_in, gmlp_ln_g, w_spatial, b_spatial, mix_norm_attn_g, mix_norm_gmlp_g, w_out, norm2_g, w_gate, w_up, w_down, final_g, loss_target, m_norm1_g, m_w_in, m_gmlp_ln_g, m_w_spatial, m_b_spatial, m_mix_norm_attn_g, m_mix_norm_gmlp_g, m_w_out, m_norm2_g, m_w_gate, m_w_up, m_w_down, m_final_g, v_norm1_g, v_w_in, v_gmlp_ln_g, v_w_spatial, v_b_spatial, v_mix_norm_attn_g, v_mix_norm_gmlp_g, v_w_out, v_norm2_g, v_w_gate, v_w_up, v_w_down, v_final_g):
    w = dict(norm1_g=norm1_g, w_in=w_in, gmlp_ln_g=gmlp_ln_g, w_spatial=w_spatial, b_spatial=b_spatial,
             mix_norm_attn_g=mix_norm_attn_g, mix_norm_gmlp_g=mix_norm_gmlp_g, w_out=w_out, norm2_g=norm2_g,
             w_gate=w_gate, w_up=w_up, w_down=w_down, final_g=final_g)
    m = dict(norm1_g=m_norm1_g, w_in=m_w_in, gmlp_ln_g=m_gmlp_ln_g, w_spatial=m_w_spatial, b_spatial=m_b_spatial,
             mix_norm_attn_g=m_mix_norm_attn_g, mix_norm_gmlp_g=m_mix_norm_gmlp_g, w_out=m_w_out, norm2_g=m_norm2_g,
             w_gate=m_w_gate, w_up=m_w_up, w_down=m_w_down, final_g=m_final_g)
    v = dict(norm1_g=v_norm1_g, w_in=v_w_in, gmlp_ln_g=v_gmlp_ln_g, w_spatial=v_w_spatial, b_spatial=v_b_spatial,
             mix_norm_attn_g=v_mix_norm_attn_g, mix_norm_gmlp_g=v_mix_norm_gmlp_g, w_out=v_w_out, norm2_g=v_norm2_g,
             w_gate=v_w_gate, w_up=v_w_up, w_down=v_w_down, final_g=v_final_g)

    pos = place_vector()
    small = {n: w[n] for n in SMALL}
    tables = step_tables()

    def start_gathers(l, after):
        flights = []
        for kinds in GROUPS:
            flights.append(gather_start([cast_into(w[BIG[k]], l, pos, after) for k in kinds]))
            after = flights[-1][3]
        return flights, after

    def arrive(flight, kinds, after):
        fulls = pair_forward(gather_wait(flight[0], flight[1], flight[2], after))
        big = dict(zip(kinds, fulls))
        if "wout" in big:
            big["wout"] = big["wout"].reshape(D, D)
        if "wd" in big:
            big["wd"] = big["wd"].reshape(FF, D)
        return big

    act = x[0]
    flights, token = start_gathers(0, None)
    big, saved = [], []
    for l in range(NL):
        mine, after_mixer = flights, token
        if l + 1 < NL:
            flights, token = start_gathers(l + 1, act if l else token)
            after_mixer = token
        wl = arrive(mine[0], GROUPS[0], act if l else mine[0][3])
        x_mid, sv = mixer_fwd(act, small, l, wl, tables, after_mixer)
        wl.update(arrive(mine[1], GROUPS[1], x_mid))
        act, sv2 = ffn_fwd(x_mid, small, l, wl)
        big.append(wl)
        saved.append({**sv, **sv2})
    loss8, dx, dxb, dfinal = loss_head(act, _row(small["final_g"]), loss_target[0])
    loss = lax.psum(loss8[0, 0], ("x", "y", "c"))

    grads, delta, new_m, new_v = {}, {}, {}, {}
    stacked = {k: None for k in KINDS}
    gs = {k: [None] * NL for k in SMALL[:-1]}

    def send_off(gbig, kinds, l, after=None):
        return chip_exchange_start([gbig[k] for k in kinds], after) + (kinds, l)

    def finish(flight, after):
        partial_grads, lands = chip_exchange_wait(*flight[:4], after)
        halves = [add_slots(g, got, pos) for g, got in zip(partial_grads, lands)]
        for k, full in zip(flight[5], rs_pair_share(halves)):
            n = BIG[k]
            stacked[k] = adamw_layer(w[n], full.reshape(w[n].shape[1:]), m[n], v[n], flight[6], stacked[k])
        return stacked[flight[5][-1]][1]

    pending, token = [], None
    for l in reversed(range(NL)):
        dx, dxb, gs1, gbig = ffn_bwd(dx, dxb, small, l, big[l], saved[l], token)
        ffn_flight = send_off(gbig, GROUPS[1], l)
        for flight in pending:
            finish(flight, ffn_flight[4])
        dx, dxb, gs2, gbig = mixer_bwd(dx, dxb, small, l, big[l], saved[l], tables, ffn_flight[4])
        for k, g in {**gs1, **gs2}.items():
            gs[k][l] = g
        if l:
            mixer_flight = send_off(gbig, GROUPS[0], l)
            pending, token = [ffn_flight, mixer_flight], mixer_flight[4]
    part = {n: jnp.stack(gs[n]).reshape(w[n].shape) for n in SMALL[:-1]}
    part["final_g"] = dfinal.reshape(w["final_g"].shape)
    small_sum = allreduce_small(_pack(part), [gbig[k] for k in GROUPS[0]])
    mixer_flight = send_off(gbig, GROUPS[0], 0, small_sum)
    token = mixer_flight[4]
    for flight in (ffn_flight, mixer_flight):
        token = finish(flight, token)
    for k in KINDS:
        grads[BIG[k]], delta[BIG[k]], new_m[BIG[k]], new_v[BIG[k]] = stacked[k]
    grads.update(_unpack(small_sum, part))

    sm = adamw(_pack({n: w[n] for n in SMALL}), _pack({n: grads[n] for n in SMALL}), _pack({n: m[n] for n in SMALL}),
               _pack({n: v[n] for n in SMALL}))
    for res, packed in zip((delta, new_m, new_v), sm):
        res.update(_unpack(packed, part))

    return (loss, dx.reshape(x.shape), *[grads[n] for n in WEIGHTS], *[delta[n] for n in WEIGHTS],
            *[new_m[n] for n in WEIGHTS], *[new_v[n] for n in WEIGHTS])
```

```python
import functools

import jax
import jax.numpy as jnp
from jax import lax
from jax.experimental import pallas as pl
from jax.experimental.pallas import tpu as pltpu

F32 = jnp.float32
BF16 = jnp.bfloat16

D = 2048
T = 4096
NL = 4
HD = 128
DA = D // 2
DG = D - DA
NH = DA // HD
NG = DG // 128
CH = 128
DIN = 3 * DA + 2 * DG
FF = 5632
DILATIONS = (1, 4, 16)
NSIDE = 64
ROPE_THETA = 10000.0
EPS = 1e-6
NEG = -1e30
SCALE = HD ** -0.5
NCHIP = 4

LR, B1, B2, AEPS, WD, STEP = 0.001, 0.9, 0.999, 1e-08, 0.01, 10

LANES = 128
VMEM_LIMIT = 56 * 1024 * 1024
MESH = pl.DeviceIdType.MESH


def _call(body, *, name, out_shape, grid=(), in_specs=None, out_specs=None, scratch=(), sem=None, aliases=None,
          prefetch=0):
    params = dict(vmem_limit_bytes=VMEM_LIMIT)
    if sem is not None:
        params["dimension_semantics"] = sem
    if prefetch:
        spec = pltpu.PrefetchScalarGridSpec(num_scalar_prefetch=prefetch, grid=grid, in_specs=in_specs,
                                            out_specs=out_specs, scratch_shapes=list(scratch))
        return pl.pallas_call(body, name=name, out_shape=out_shape, grid_spec=spec,
                              input_output_aliases=aliases or {}, compiler_params=pltpu.CompilerParams(**params))
    kw = {}
    if grid:
        kw["grid"] = grid
    return pl.pallas_call(
        body, name=name, out_shape=out_shape, in_specs=in_specs, out_specs=out_specs,
        scratch_shapes=list(scratch), input_output_aliases=aliases or {},
        compiler_params=pltpu.CompilerParams(**params), **kw)


def _sds(shape, dtype):
    return jax.ShapeDtypeStruct(tuple(shape), dtype)


def _dot(a, b):
    return jnp.dot(a, b, preferred_element_type=F32)


def _dot_nt(a, b):
    return lax.dot_general(a, b, (((1,), (1,)), ((), ())), preferred_element_type=F32)


def _dot_tn(a, b):
    return lax.dot_general(a, b, (((0,), (0,)), ((), ())), preferred_element_type=F32)


def _sigmoid(x):
    return 0.5 * jnp.tanh(0.5 * x) + 0.5


def _behind(after):
    arrays = [] if after is None else list(after) if isinstance(after, (list, tuple)) else [after]
    return arrays, [pl.BlockSpec(memory_space=pl.ANY)] * len(arrays)


def rms_fwd(x, g, after=None):
    n, d = x.shape
    tm = 256
    extra, extra_specs = _behind(after)

    def body(x_ref, g_ref, *rest):
        h_ref = rest[-1]
        xv = x_ref[...]
        r = lax.rsqrt(jnp.mean(xv * xv, axis=-1, keepdims=True) + EPS)
        h_ref[...] = (xv * r * g_ref[...]).astype(BF16)

    return _call(body, name="rms_fwd", out_shape=_sds((n, d), BF16), grid=(n // tm,),
                 in_specs=[pl.BlockSpec((tm, d), lambda i: (i, 0)), pl.BlockSpec((1, d), lambda i: (0, 0))] + extra_specs,
                 out_specs=pl.BlockSpec((tm, d), lambda i: (i, 0)), sem=("parallel",))(x, g, *extra)


def rms_bwd(dh, x, g, dres):
    n, d = x.shape
    tm = 256

    def body(dh_ref, x_ref, g_ref, dres_ref, dx_ref, dxb_ref, dg_ref):
        @pl.when(pl.program_id(0) == 0)
        def _():
            dg_ref[...] = jnp.zeros_like(dg_ref)

        xv = x_ref[...]
        r = lax.rsqrt(jnp.mean(xv * xv, axis=-1, keepdims=True) + EPS)
        xhat = xv * r
        dhv = dh_ref[...].astype(F32)
        dg_ref[...] += jnp.sum(dhv * xhat, axis=0, keepdims=True)
        dxn = dhv * g_ref[...]
        dx = dres_ref[...] + r * (dxn - xhat * jnp.mean(dxn * xhat, axis=-1, keepdims=True))
        dx_ref[...] = dx
        dxb_ref[...] = dx.astype(BF16)

    row = pl.BlockSpec((tm, d), lambda i: (i, 0))
    vec = pl.BlockSpec((1, d), lambda i: (0, 0))
    return _call(body, name="rms_bwd", out_shape=(_sds((n, d), F32), _sds((n, d), BF16), _sds((1, d), F32)),
                 grid=(n // tm,), in_specs=[row, row, vec, row], out_specs=(row, row, vec), sem=("arbitrary",))(dh, x, g, dres)


def loss_head(x, g, target):
    n, d = x.shape
    tm = 256

    def body(x_ref, g_ref, t_ref, loss_ref, dx_ref, dxb_ref, dg_ref):
        @pl.when(pl.program_id(0) == 0)
        def _():
            dg_ref[...] = jnp.zeros_like(dg_ref)
            loss_ref[...] = jnp.zeros_like(loss_ref)

        xv = x_ref[...]
        r = lax.rsqrt(jnp.mean(xv * xv, axis=-1, keepdims=True) + EPS)
        xhat = xv * r
        gv = g_ref[...]
        err = xhat * gv - t_ref[...]
        loss_ref[...] += 0.5 * jnp.sum(jnp.mean(err * err, axis=-1, keepdims=True))
        dy = err * (1.0 / d)
        dg_ref[...] += jnp.sum(dy * xhat, axis=0, keepdims=True)
        dxn = dy * gv
        dx = r * (dxn - xhat * jnp.mean(dxn * xhat, axis=-1, keepdims=True))
        dx_ref[...] = dx
        dxb_ref[...] = dx.astype(BF16)

    row = pl.BlockSpec((tm, d), lambda i: (i, 0))
    vec = pl.BlockSpec((1, d), lambda i: (0, 0))
    return _call(body, name="loss_head",
                 out_shape=(_sds((8, LANES), F32), _sds((n, d), F32), _sds((n, d), BF16), _sds((1, d), F32)),
                 grid=(n // tm,), in_specs=[row, vec, row],
                 out_specs=(pl.BlockSpec((8, LANES), lambda i: (0, 0)), row, row, vec), sem=("arbitrary",))(x, g, target)


def mm_cols(a, wc, tm=512):
    n, k = a.shape
    s, _, nq = wc.shape

    def body(a_ref, w_ref, o_ref):
        o_ref[...] = _dot(a_ref[...], w_ref[...]).astype(BF16)

    return _call(body, name="mm_cols", out_shape=_sds((n, s * nq), BF16), grid=(s, n // tm),
                 in_specs=[pl.BlockSpec((tm, k), lambda j, i: (i, 0)), pl.BlockSpec((None, k, nq), lambda j, i: (j, 0, 0))],
                 out_specs=pl.BlockSpec((tm, nq), lambda j, i: (i, j)), sem=("parallel", "parallel"))(a, wc)


def mm_gateup(h, wg, wu, tm=512):
    n, k = h.shape
    s, _, nq = wg.shape

    def body(h_ref, wg_ref, wu_ref, gate_ref, up_ref, ff_ref):
        hv = h_ref[...]
        gate = _dot(hv, wg_ref[...])
        up = _dot(hv, wu_ref[...])
        gate_ref[...] = gate.astype(BF16)
        up_ref[...] = up.astype(BF16)
        ff_ref[...] = (gate * _sigmoid(gate) * up).astype(BF16)

    wspec = pl.BlockSpec((None, k, nq), lambda j, i: (j, 0, 0))
    ospec = pl.BlockSpec((tm, nq), lambda j, i: (i, j))
    o = _sds((n, s * nq), BF16)
    return _call(body, name="mm_gateup", out_shape=(o, o, o), grid=(s, n // tm),
                 in_specs=[pl.BlockSpec((tm, k), lambda j, i: (i, 0)), wspec, wspec],
                 out_specs=(ospec, ospec, ospec), sem=("parallel", "parallel"))(h, wg, wu)


def mm_rows_res(a, wr, res, tm=512, tn=1024):
    n, k = a.shape
    _, nn = wr.shape

    def body(a_ref, w_ref, r_ref, o_ref):
        o_ref[...] = r_ref[...] + _dot(a_ref[...], w_ref[...])

    return _call(body, name="mm_rows_res", out_shape=_sds((n, nn), F32), grid=(nn // tn, n // tm),
                 in_specs=[pl.BlockSpec((tm, k), lambda j, i: (i, 0)), pl.BlockSpec((k, tn), lambda j, i: (0, j)),
                           pl.BlockSpec((tm, tn), lambda j, i: (i, j))],
                 out_specs=pl.BlockSpec((tm, tn), lambda j, i: (i, j)), sem=("parallel", "parallel"))(a, wr, res)


def mm_nt_rows(dy, wr, tk, tm=512, after=None):
    n, nn = dy.shape
    k, _ = wr.shape
    extra, extra_specs = _behind(after)

    def body(dy_ref, w_ref, *rest):
        rest[-1][...] = _dot_nt(dy_ref[...], w_ref[...]).astype(BF16)

    return _call(body, name="mm_nt_rows", out_shape=_sds((n, k), BF16), grid=(k // tk, n // tm),
                 in_specs=[pl.BlockSpec((tm, nn), lambda j, i: (i, 0)), pl.BlockSpec((tk, nn), lambda j, i: (j, 0))]
                 + extra_specs,
                 out_specs=pl.BlockSpec((tm, tk), lambda j, i: (i, j)), sem=("parallel", "parallel"))(dy, wr, *extra)


def mm_dff(dy, wd, gate, up, tk, tm=512, after=None):
    n, nn = dy.shape
    k, _ = wd.shape
    extra, extra_specs = _behind(after)

    def body(dy_ref, w_ref, gate_ref, up_ref, *rest):
        dgate_ref, dup_ref = rest[-2:]
        dff = _dot_nt(dy_ref[...], w_ref[...])
        gate = gate_ref[...].astype(F32)
        up = up_ref[...].astype(F32)
        sig = _sigmoid(gate)
        dgate_ref[...] = (dff * up * (sig * (1.0 + gate * (1.0 - sig)))).astype(BF16)
        dup_ref[...] = (dff * (gate * sig)).astype(BF16)

    tile = pl.BlockSpec((tm, tk), lambda j, i: (i, j))
    o = _sds((n, k), BF16)
    return _call(body, name="mm_dff", out_shape=(o, o), grid=(k // tk, n // tm),
                 in_specs=[pl.BlockSpec((tm, nn), lambda j, i: (i, 0)), pl.BlockSpec((tk, nn), lambda j, i: (j, 0)), tile, tile]
                 + extra_specs,
                 out_specs=(tile, tile), sem=("parallel", "parallel"))(dy, wd, gate, up, *extra)


def mm_nt_cols(das, wcs, tm=512):
    n = das[0].shape[0]
    s, k, nq = wcs[0].shape
    npair = len(das)

    def body(*refs):
        da_refs, w_refs, o_ref, acc = refs[:npair], refs[npair:2 * npair], refs[2 * npair], refs[2 * npair + 1]
        ss = pl.program_id(1)

        @pl.when(ss == 0)
        def _():
            acc[...] = jnp.zeros_like(acc)

        for da_ref, w_ref in zip(da_refs, w_refs):
            acc[...] += _dot_nt(da_ref[...], w_ref[...])

        @pl.when(ss == s - 1)
        def _():
            o_ref[...] = acc[...].astype(BF16)

    return _call(body, name="mm_nt_cols%d" % npair, out_shape=_sds((n, k), BF16), grid=(n // tm, s),
                 in_specs=[pl.BlockSpec((tm, nq), lambda i, ss: (i, ss))] * npair
                 + [pl.BlockSpec((None, k, nq), lambda i, ss: (ss, 0, 0))] * npair,
                 out_specs=pl.BlockSpec((tm, k), lambda i, ss: (i, 0)),
                 scratch=[pltpu.VMEM((tm, k), F32)], sem=("parallel", "arbitrary"))(*das, *wcs)


def mm_tn(a, b, nq, shard_major, tka=512):
    n, ka = a.shape
    nb = b.shape[1] // nq

    def body(a_ref, b_ref, o_ref):
        o_ref[...] = _dot_tn(a_ref[...], b_ref[...]).astype(BF16)

    if shard_major:
        out_shape, out_spec = _sds((nb, ka, nq), BF16), pl.BlockSpec((None, tka, nq), lambda j, i: (j, i, 0))
    else:
        out_shape, out_spec = _sds((ka, nb * nq), BF16), pl.BlockSpec((tka, nq), lambda j, i: (i, j))
    return _call(body, name="mm_tn", out_shape=out_shape, grid=(nb, ka // tka),
                 in_specs=[pl.BlockSpec((n, tka), lambda j, i: (0, i)), pl.BlockSpec((n, nq), lambda j, i: (0, j))],
                 out_specs=out_spec, sem=("parallel", "parallel"))(a, b)


def rope_tables(n):
    pos = jnp.arange(n, dtype=F32)
    inv = ROPE_THETA ** (-jnp.arange(0, HD, 2, dtype=F32) / HD)
    ang = pos[:, None] * inv[None, :]
    cos, sin = jnp.cos(ang), jnp.sin(ang)
    return jnp.concatenate([cos, cos], axis=-1), jnp.concatenate([-sin, sin], axis=-1)


PERM_ROWS = 256


def perm_matrix(d):
    n = PERM_ROWS // d
    i = jnp.arange(PERM_ROWS)
    src = (i % n) * d + i // n
    return (src[:, None] == jnp.arange(PERM_ROWS)[None, :]).astype(BF16)


def _perm_f32(p, x):
    hi = x.astype(BF16)
    rest = x - hi.astype(F32)
    mid = rest.astype(BF16)
    lo = (rest - mid.astype(F32)).astype(BF16)
    return (_dot(p, hi) + _dot(p, mid)) + _dot(p, lo)


def _res_spec(d, cols, col_block=0):
    return pl.BlockSpec((d, PERM_ROWS // d, cols), lambda i: (0, i, col_block))


def _perm_specs():
    return [pl.BlockSpec((PERM_ROWS, PERM_ROWS), lambda i: (0, 0))] * (len(DILATIONS) - 1)


def qkv_split(proj, cos2, sin2, perms):
    n = proj.shape[0]
    tm = PERM_ROWS

    def body(p_ref, c_ref, s_ref, *rest):
        perm_refs, outs = rest[:len(DILATIONS) - 1], rest[len(DILATIONS) - 1:]
        c, s = c_ref[...], s_ref[...]
        for h in range(2 * NH):
            t = p_ref[:, h * HD:(h + 1) * HD].astype(F32)
            outs[0][0, :, h * HD:(h + 1) * HD] = (t * c + pltpu.roll(t, HD // 2, 1) * s).astype(BF16)
        outs[0][0, :, 2 * DA:] = p_ref[:, 2 * DA:]
        nat = outs[0][0]
        for o_ref, p_ref_d, d in zip(outs[1:], perm_refs, DILATIONS[1:]):
            o_ref[...] = _dot(p_ref_d[...], nat).astype(BF16).reshape(d, tm // d, 3 * DA)

    tab = pl.BlockSpec((tm, HD), lambda i: (i, 0))
    return _call(body, name="qkv_split", out_shape=tuple(_sds((d, n // d, 3 * DA), BF16) for d in DILATIONS),
                 grid=(n // tm,), in_specs=[pl.BlockSpec((tm, 3 * DA), lambda i: (i, 0)), tab, tab] + _perm_specs(),
                 out_specs=tuple(_res_spec(d, 3 * DA) for d in DILATIONS), sem=("parallel",))(proj, cos2, sin2, *perms)


def dproj_assemble(dqs, dks, dvs, duv, cos2, sin2, perms_t):
    n = duv.shape[0]
    tm = PERM_ROWS
    npat = len(DILATIONS)

    def body(*refs):
        parts = refs[0:npat], refs[npat:2 * npat], refs[2 * npat:3 * npat]
        duv_ref, c_ref, s_ref = refs[3 * npat:3 * npat + 3]
        pt_refs, o_ref = refs[3 * npat + 3:-1], refs[-1]
        c, s = c_ref[...], s_ref[...]
        for part, part_refs in enumerate(parts):
            t = part_refs[0][0].astype(F32)
            for ref, pt_ref in zip(part_refs[1:], pt_refs):
                t = t + _dot(pt_ref[...], ref[...].reshape(tm, DA))
            if part == 2:
                o_ref[:, 2 * DA:3 * DA] = t.astype(BF16)
                continue
            for h in range(NH):
                th = t[:, h * HD:(h + 1) * HD]
                o_ref[:, part * DA + h * HD:part * DA + (h + 1) * HD] = (th * c - pltpu.roll(th, HD // 2, 1) * s).astype(BF16)
        o_ref[:, 3 * DA:] = duv_ref[...]

    blks = [_res_spec(d, DA) for d in DILATIONS]
    tab = pl.BlockSpec((tm, HD), lambda i: (i, 0))
    return _call(body, name="dproj_assemble", out_shape=_sds((n, DIN), BF16), grid=(n // tm,),
                 in_specs=blks * 3 + [pl.BlockSpec((tm, 2 * DG), lambda i: (i, 0)), tab, tab] + _perm_specs(),
                 out_specs=pl.BlockSpec((tm, DIN), lambda i: (i, 0)), sem=("parallel",))(
        *dqs, *dks, *dvs, duv, cos2, sin2, *perms_t)


ATTN_BLOCK_BYTES = 2 << 20


def _attn_tiles(ls):
    tq = min(128, ls)
    w = min(tq + 2 * NSIDE, ls)
    return tq, w, ls // tq


def _attn_heads(ls):
    return NH if ls * DA * 2 <= ATTN_BLOCK_BYTES else 1


ATTN_UNROLL = 8


def _attn_loop(nt, hb, tile):
    for h in range(hb):
        def trip(t, carry, h=h):
            tile(t, h)
            return carry

        lax.fori_loop(0, nt, trip, 0, unroll=min(nt, ATTN_UNROLL))


def _band(t, tq, w, ls):
    q0 = pl.multiple_of(t * tq, tq)
    ks = pl.multiple_of(jnp.clip(t * tq - NSIDE, 0, ls - w), NSIDE)
    qpos = q0 + lax.broadcasted_iota(jnp.int32, (tq, w), 0)
    kpos = ks + lax.broadcasted_iota(jnp.int32, (tq, w), 1)
    return q0, ks, jnp.abs(kpos - qpos) <= NSIDE


def _attn_specs(ls, hb):
    nhb = NH // hb
    q = pl.BlockSpec((None, ls, hb * HD), lambda g: (g // nhb, 0, g % nhb))
    k = pl.BlockSpec((None, ls, hb * HD), lambda g: (g // nhb, 0, nhb + g % nhb))
    v = pl.BlockSpec((None, ls, hb * HD), lambda g: (g // nhb, 0, 2 * nhb + g % nhb))
    return q, k, v, q


def _lse_spec(ls, hb):
    nhb = NH // hb
    return pl.BlockSpec((None, ls, LANES), lambda g: (g // nhb, 0, 0))


def attn_fwd(qkv):
    d, ls, _ = qkv.shape
    tq, w, nt = _attn_tiles(ls)
    hb = _attn_heads(ls)
    nhb = NH // hb

    def body(q_ref, k_ref, v_ref, o_ref, l_ref):
        first = (pl.program_id(0) % nhb) * hb
        lane = lax.broadcasted_iota(jnp.int32, (tq, LANES), 1)

        @pl.when(first == 0)
        def _():
            l_ref[...] = jnp.zeros_like(l_ref)

        def tile(t, h):
            q0, ks, valid = _band(t, tq, w, ls)
            cols = slice(h * HD, (h + 1) * HD)
            s = _dot_nt(q_ref[pl.ds(q0, tq), cols], k_ref[pl.ds(ks, w), cols]) * SCALE
            s = jnp.where(valid, s, NEG)
            m = jnp.max(s, axis=1, keepdims=True)
            p = jnp.exp(s - m)
            l = jnp.sum(p, axis=1, keepdims=True)
            o = _dot(p.astype(BF16), v_ref[pl.ds(ks, w), cols]) / l
            o_ref[pl.ds(q0, tq), cols] = o.astype(BF16)
            l_ref[pl.ds(q0, tq), :] = jnp.where(lane == first + h, m + jnp.log(l), l_ref[pl.ds(q0, tq), :])

        _attn_loop(nt, hb, tile)

    q, k, v, o = _attn_specs(ls, hb)
    return _call(body, name="attn_fwd_d%d" % d, out_shape=(_sds((d, ls, DA), BF16), _sds((d, ls, LANES), F32)),
                 grid=(d * NH // hb,), in_specs=[q, k, v], out_specs=(o, _lse_spec(ls, hb)), sem=("arbitrary",))(
        qkv, qkv, qkv)


def attn_merge(os_, ls_, ga, perms_t):
    n = os_[0].shape[1]
    tm = PERM_ROWS
    npat = len(DILATIONS)

    def body(*refs):
        o_refs, l_refs = refs[:npat], refs[npat:2 * npat]
        g_ref, pt_refs = refs[2 * npat], refs[2 * npat + 1:3 * npat]
        a_ref, lse_ref, mix_ref = refs[3 * npat:]
        ov = [o_refs[0][0].astype(F32)] + [_dot(pt[...], o[...].reshape(tm, DA)) for o, pt in zip(o_refs[1:], pt_refs)]
        lv = [l_refs[0][0]] + [_perm_f32(pt[...], l[...].reshape(tm, LANES)) for l, pt in zip(l_refs[1:], pt_refs)]
        lmax = functools.reduce(jnp.maximum, lv)
        e = [jnp.exp(x - lmax) for x in lv]
        den = functools.reduce(lambda p, q: p + q, e)
        wts = [ei / den for ei in e]
        lse_ref[...] = lmax + jnp.log(den)
        heads = []
        for h in range(NH):
            cols = slice(h * HD, (h + 1) * HD)
            heads.append(functools.reduce(lambda p, q: p + q, [wt[:, h:h + 1] * oi[:, cols] for wt, oi in zip(wts, ov)]))
        a = jnp.concatenate(heads, axis=1)
        a_ref[...] = a.astype(BF16)
        r = lax.rsqrt(jnp.mean(a * a, axis=-1, keepdims=True) + EPS)
        mix_ref[...] = (a * r * g_ref[...]).astype(BF16)

    blk = pl.BlockSpec((tm, DA), lambda i: (i, 0))
    return _call(body, name="attn_merge", out_shape=(_sds((n, DA), BF16), _sds((n, LANES), F32), _sds((n, D), BF16)),
                 grid=(n // tm,),
                 in_specs=[_res_spec(d, DA) for d in DILATIONS] + [_res_spec(d, LANES) for d in DILATIONS]
                 + [pl.BlockSpec((1, DA), lambda i: (0, 0))] + _perm_specs(),
                 out_specs=(blk, pl.BlockSpec((tm, LANES), lambda i: (i, 0)), blk), sem=("parallel",))(
        *os_, *ls_, ga, *perms_t)


def attn_norm_bwd(dmix, a, lse, ga, perms):
    n = a.shape[0]
    tm = PERM_ROWS
    npat = len(DILATIONS)

    def body(dm_ref, a_ref, lse_ref, g_ref, *rest):
        p_refs = rest[:npat - 1]
        da_refs, a_refs, lse_refs = rest[npat - 1:2 * npat - 1], rest[2 * npat - 1:3 * npat - 2], rest[3 * npat - 2:4 * npat - 3]
        dg_ref = rest[-1]

        @pl.when(pl.program_id(0) == 0)
        def _():
            dg_ref[...] = jnp.zeros_like(dg_ref)

        ab = a_ref[...]
        av = ab.astype(F32)
        r = lax.rsqrt(jnp.mean(av * av, axis=-1, keepdims=True) + EPS)
        ahat = av * r
        dm = dm_ref[...].astype(F32)
        dg_ref[...] += jnp.sum(dm * ahat, axis=0, keepdims=True)
        dn = dm * g_ref[...]
        da = (r * (dn - ahat * jnp.mean(dn * ahat, axis=-1, keepdims=True))).astype(BF16)
        da_refs[0][0] = da
        lv = lse_ref[...]
        for i, d in enumerate(DILATIONS[1:]):
            p = p_refs[i][...]
            da_refs[i + 1][...] = _dot(p, da).astype(BF16).reshape(d, tm // d, DA)
            a_refs[i][...] = _dot(p, ab).astype(BF16).reshape(d, tm // d, DA)
            lse_refs[i][...] = _perm_f32(p, lv).reshape(d, tm // d, LANES)

    blk = pl.BlockSpec((tm, DA), lambda i: (i, 0))
    vec = pl.BlockSpec((1, DA), lambda i: (0, 0))
    res = [_res_spec(d, DA) for d in DILATIONS]
    outs = _call(body, name="attn_norm_bwd",
                 out_shape=tuple([_sds((d, n // d, DA), BF16) for d in DILATIONS]
                                 + [_sds((d, n // d, DA), BF16) for d in DILATIONS[1:]]
                                 + [_sds((d, n // d, LANES), F32) for d in DILATIONS[1:]] + [_sds((1, DA), F32)]),
                 grid=(n // tm,),
                 in_specs=[blk, blk, pl.BlockSpec((tm, LANES), lambda i: (i, 0)), vec] + _perm_specs(),
                 out_specs=tuple(res + res[1:] + [_res_spec(d, LANES) for d in DILATIONS[1:]] + [vec]),
                 sem=("arbitrary",))(dmix, a, lse, ga, *perms)
    return outs[:npat], outs[npat:2 * npat - 1], outs[2 * npat - 1:3 * npat - 2], outs[-1]


def attn_bwd(qkv, a, da, lse):
    d, ls, _ = qkv.shape
    tq, w, nt = _attn_tiles(ls)
    hb = _attn_heads(ls)

    nhb = NH // hb

    def body(q_ref, k_ref, v_ref, a_ref, da_ref, lse_ref, dq_ref, dk_ref, dv_ref, dk_acc, dv_acc):
        first = (pl.program_id(0) % nhb) * hb
        lane = lax.broadcasted_iota(jnp.int32, (tq, LANES), 1)
        dk_acc[...] = jnp.zeros_like(dk_acc)
        dv_acc[...] = jnp.zeros_like(dv_acc)

        def tile(t, h):
            q0, ks, valid = _band(t, tq, w, ls)
            cols = slice(h * HD, (h + 1) * HD)
            q = q_ref[pl.ds(q0, tq), cols]
            k = k_ref[pl.ds(ks, w), cols]
            v = v_ref[pl.ds(ks, w), cols]
            do = da_ref[pl.ds(q0, tq), cols]
            if nhb == 1:
                lse_h = lse_ref[pl.ds(q0, tq), h:h + 1]
            else:
                lse_h = jnp.sum(jnp.where(lane == first + h, lse_ref[pl.ds(q0, tq), :], 0.0), axis=1, keepdims=True)
            s = jnp.where(valid, _dot_nt(q, k) * SCALE, NEG)
            p = jnp.exp(s - lse_h)
            drow = jnp.sum(do.astype(F32) * a_ref[pl.ds(q0, tq), cols].astype(F32), axis=1, keepdims=True)
            ds = (p * (_dot_nt(do, v) - drow) * SCALE).astype(BF16)
            dv_acc[pl.ds(ks, w), cols] += _dot_tn(p.astype(BF16), do)
            dk_acc[pl.ds(ks, w), cols] += _dot_tn(ds, q)
            dq_ref[pl.ds(q0, tq), cols] = _dot(ds, k).astype(BF16)

        _attn_loop(nt, hb, tile)
        dk_ref[...] = dk_acc[...].astype(BF16)
        dv_ref[...] = dv_acc[...].astype(BF16)

    q, k, v, o = _attn_specs(ls, hb)
    out = _sds((d, ls, DA), BF16)
    return _call(body, name="attn_bwd_d%d" % d, out_shape=(out, out, out), grid=(d * NH // hb,),
                 in_specs=[q, k, v, o, o, _lse_spec(ls, hb)], out_specs=(o, o, o),
                 scratch=[pltpu.VMEM((ls, hb * HD), F32), pltpu.VMEM((ls, hb * HD), F32)], sem=("parallel",))(
        qkv, qkv, qkv, a, da, lse)


GM_TM = 256
INV_SQRT2 = 0.7071067811865476
INV_SQRT2PI = 0.3989422804014327


def _gelu(x):
    return 0.5 * x * (1.0 + lax.erf(x * INV_SQRT2))


def _gelu_grad(x):
    return 0.5 * (1.0 + lax.erf(x * INV_SQRT2)) + x * (INV_SQRT2PI * jnp.exp(-0.5 * x * x))


def _gmlp_forward(up, vp, ln_g, ws_ref, bias):
    u = _gelu(up)
    v = _gelu(vp)
    vc = v - jnp.mean(v, axis=-1, keepdims=True)
    rs = lax.rsqrt(jnp.mean(vc * vc, axis=-1, keepdims=True) + EPS)
    vhat = vc * rs
    vln = (vhat * ln_g).astype(BF16)
    rows = []
    for c in range(GM_TM // CH):
        cols = [_dot(ws_ref[g], vln[c * CH:(c + 1) * CH, g * 128:(g + 1) * 128]) for g in range(NG)]
        rows.append(jnp.concatenate(cols, axis=1) + bias)
    return u, vhat, rs, vln, jnp.concatenate(rows, axis=0)


def _gmlp_specs():
    ublk = pl.BlockSpec((GM_TM, DG), lambda i: (i, 3 * DA // DG))
    vblk = pl.BlockSpec((GM_TM, DG), lambda i: (i, 3 * DA // DG + 1))
    vec = pl.BlockSpec((1, DG), lambda i: (0, 0))
    wsp = pl.BlockSpec((NG, CH, CH), lambda i: (0, 0, 0))
    bsp = pl.BlockSpec((CH, DG), lambda i: (0, 0))
    return ublk, vblk, vec, wsp, bsp


def gmlp_fwd(proj, ln_g, ws, bias, gg, mix):
    n = proj.shape[0]

    def body(up_ref, vp_ref, ln_ref, ws_ref, b_ref, gg_ref, mix_in, mix_ref):
        del mix_in
        u, _, _, _, mixed = _gmlp_forward(up_ref[...].astype(F32), vp_ref[...].astype(F32), ln_ref[...], ws_ref, b_ref[...])
        gout = u * mixed
        r = lax.rsqrt(jnp.mean(gout * gout, axis=-1, keepdims=True) + EPS)
        mix_ref[...] = (gout * r * gg_ref[...]).astype(BF16)

    ublk, vblk, vec, wsp, bsp = _gmlp_specs()
    return _call(body, name="gmlp_fwd", out_shape=_sds((n, D), BF16), grid=(n // GM_TM,),
                 in_specs=[ublk, vblk, vec, wsp, bsp, vec, pl.BlockSpec(memory_space=pl.ANY)],
                 out_specs=pl.BlockSpec((GM_TM, DG), lambda i: (i, DA // DG)), sem=("parallel",), aliases={6: 0})(
        proj, proj, ln_g, ws, bias, gg, mix)


def gmlp_bwd(proj, dmix, ln_g, ws, wst, bias, gg):
    n = proj.shape[0]

    def body(up_ref, vp_ref, dm_ref, ln_ref, ws_ref, wst_ref, b_ref, gg_ref, duv_ref, dln_ref, dws_ref, dbs_ref, dgg_ref,
             db_ref):
        @pl.when(pl.program_id(0) == 0)
        def _():
            dln_ref[...] = jnp.zeros_like(dln_ref)
            dws_ref[...] = jnp.zeros_like(dws_ref)
            db_ref[...] = jnp.zeros_like(db_ref)
            dgg_ref[...] = jnp.zeros_like(dgg_ref)

        up = up_ref[...].astype(F32)
        vp = vp_ref[...].astype(F32)
        ln_g = ln_ref[...]
        u, vhat, rs, vln, mixed = _gmlp_forward(up, vp, ln_g, ws_ref, b_ref[...])
        gout = u * mixed
        r = lax.rsqrt(jnp.mean(gout * gout, axis=-1, keepdims=True) + EPS)
        ghat = gout * r
        dm = dm_ref[...].astype(F32)
        dgg_ref[...] += jnp.sum(dm * ghat, axis=0, keepdims=True)
        dn = dm * gg_ref[...]
        dgout = r * (dn - ghat * jnp.mean(dn * ghat, axis=-1, keepdims=True))
        du = dgout * mixed
        dmixed = dgout * u
        dmb = dmixed.astype(BF16)
        rows = []
        for c in range(GM_TM // CH):
            rsl = slice(c * CH, (c + 1) * CH)
            db_ref[...] += dmixed[rsl, :]
            cols = []
            for g in range(NG):
                csl = slice(g * 128, (g + 1) * 128)
                dws_ref[g] += _dot_nt(dmb[rsl, csl], vln[rsl, csl])
                cols.append(_dot(wst_ref[g], dmb[rsl, csl]))
            rows.append(jnp.concatenate(cols, axis=1))
        dvln = jnp.concatenate(rows, axis=0)
        dln_ref[...] += jnp.sum(dvln * vhat, axis=0, keepdims=True)
        dvh = dvln * ln_g
        dv = rs * (dvh - jnp.mean(dvh, axis=-1, keepdims=True) - vhat * jnp.mean(dvh * vhat, axis=-1, keepdims=True))
        duv_ref[:, 0:DG] = (du * _gelu_grad(up)).astype(BF16)
        duv_ref[:, DG:] = (dv * _gelu_grad(vp)).astype(BF16)

        @pl.when(pl.program_id(0) == pl.num_programs(0) - 1)
        def _():
            for g in range(NG):
                dbs_ref[g:g + 1, :] = jnp.sum(jnp.transpose(db_ref[:, g * 128:(g + 1) * 128]), axis=0, keepdims=True)

    ublk, vblk, vec, wsp, bsp = _gmlp_specs()
    return _call(body, name="gmlp_bwd",
                 out_shape=(_sds((n, 2 * DG), BF16), _sds((1, DG), F32), _sds((NG, CH, CH), F32), _sds((NG, CH), F32),
                            _sds((1, DG), F32)),
                 grid=(n // GM_TM,),
                 in_specs=[ublk, vblk, pl.BlockSpec((GM_TM, DG), lambda i: (i, DA // DG)), vec, wsp, wsp, bsp, vec],
                 out_specs=(pl.BlockSpec((GM_TM, 2 * DG), lambda i: (i, 0)), vec, wsp,
                            pl.BlockSpec((NG, CH), lambda i: (0, 0)), vec),
                 scratch=[pltpu.VMEM((CH, DG), F32)], sem=("arbitrary",))(
        proj, proj, dmix, ln_g, ws, wst, bias, gg)


def _row(v):
    return v.reshape(1, -1)


def local_step(x, target, small, big):
    tables = step_tables()
    saved = []
    for l in range(NL):
        x_mid, sv = mixer_fwd(x, small, l, big[l], tables)
        x, sv2 = ffn_fwd(x_mid, small, l, big[l])
        saved.append({**sv, **sv2})
    loss8, dx, dxb, dfinal = loss_head(x, _row(small["final_g"]), target)
    gs = {k: [None] * NL for k in SMALL[:-1]}
    gbig = [None] * NL
    for l in reversed(range(NL)):
        dx, dxb, gs1, gb1 = ffn_bwd(dx, dxb, small, l, big[l], saved[l])
        dx, dxb, gs2, gb2 = mixer_bwd(dx, dxb, small, l, big[l], saved[l], tables)
        gbig[l] = {**gb1, **gb2}
        for k, g in {**gs1, **gs2}.items():
            gs[k][l] = g
    return loss8, dx, gs, dfinal, gbig


def step_tables():
    perms = [perm_matrix(d) for d in DILATIONS[1:]]
    return rope_tables(T) + (perms, [p.T for p in perms])


def mixer_fwd(x, small, l, w, tables, after=None):
    cos2, sin2, perms, perms_t = tables
    ws_b = small["w_spatial"][l].astype(BF16)
    bias = jnp.repeat(small["b_spatial"][l].T, 128, axis=1)
    h = rms_fwd(x, _row(small["norm1_g"][l]), after)
    proj = mm_cols(h, w["win"])
    qkvs = qkv_split(proj, cos2, sin2, perms)
    outs = [attn_fwd(qkv) for qkv in qkvs]
    a, lse, mix = attn_merge([o for o, _ in outs], [s for _, s in outs], _row(small["mix_norm_attn_g"][l]), perms_t)
    mix = gmlp_fwd(proj, _row(small["gmlp_ln_g"][l]), ws_b, bias, _row(small["mix_norm_gmlp_g"][l]), mix)
    x_mid = mm_rows_res(mix, w["wout"], x)
    return x_mid, dict(x=x, h=h, proj=proj, qkvs=qkvs, a=a, lse=lse, mix=mix, ws_b=ws_b, bias=bias)


def ffn_fwd(x_mid, small, l, w, after=None):
    h2 = rms_fwd(x_mid, _row(small["norm2_g"][l]), after)
    gate, up, ff = mm_gateup(h2, w["wg"], w["wu"])
    x_out = mm_rows_res(ff, w["wd"], x_mid)
    return x_out, dict(x_mid=x_mid, h2=h2, gate=gate, up=up, ff=ff)


def ffn_bwd(dx, dxb, small, l, w, sv, after=None):
    gs = {}
    dgate, dup = mm_dff(dxb, w["wd"], sv["gate"], sv["up"], tk=FF // NCHIP, after=after)
    g_wd = mm_tn(sv["ff"], dxb, D // 2, False)
    dh2 = mm_nt_cols([dgate, dup], [w["wg"], w["wu"]])
    g_wg = mm_tn(sv["h2"], dgate, FF // NCHIP, True)
    g_wu = mm_tn(sv["h2"], dup, FF // NCHIP, True)
    dx, dxb, gs["norm2_g"] = rms_bwd(dh2, sv["x_mid"], _row(small["norm2_g"][l]), dx)
    return dx, dxb, gs, dict(wg=g_wg, wu=g_wu, wd=g_wd.reshape(NCHIP, FF // NCHIP, D))


def mixer_bwd(dx, dxb, small, l, w, sv, tables, after=None):
    cos2, sin2, perms, perms_t = tables
    gs = {}
    dmix = mm_nt_rows(dxb, w["wout"], tk=D // 2, after=after)
    g_wout = mm_tn(sv["mix"], dxb, D // 2, False)
    das, a_res, lse_res, gs["mix_norm_attn_g"] = attn_norm_bwd(dmix, sv["a"], sv["lse"],
                                                                _row(small["mix_norm_attn_g"][l]), perms)
    a_all = [sv["a"].reshape((1,) + sv["a"].shape)] + list(a_res)
    lse_all = [sv["lse"].reshape((1,) + sv["lse"].shape)] + list(lse_res)
    parts = [attn_bwd(*operands) for operands in zip(sv["qkvs"], a_all, das, lse_all)]
    wst = jnp.swapaxes(small["w_spatial"][l], 1, 2).astype(BF16)
    duv, gs["gmlp_ln_g"], gs["w_spatial"], gs["b_spatial"], gs["mix_norm_gmlp_g"] = gmlp_bwd(
        sv["proj"], dmix, _row(small["gmlp_ln_g"][l]), sv["ws_b"], wst, sv["bias"], _row(small["mix_norm_gmlp_g"][l]))
    dproj = dproj_assemble([p[0] for p in parts], [p[1] for p in parts], [p[2] for p in parts], duv, cos2, sin2,
                           perms_t)
    dh = mm_nt_cols([dproj], [w["win"]])
    g_win = mm_tn(sv["h"], dproj, DIN // NCHIP, True)
    dx, dxb, gs["norm1_g"] = rms_bwd(dh, sv["x"], _row(small["norm1_g"][l]), dx)
    return dx, dxb, gs, dict(win=g_win, wout=g_wout.reshape(NCHIP, D // NCHIP, D))


KINDS = ("win", "wout", "wg", "wu", "wd")
GROUPS = (("win", "wout"), ("wg", "wu", "wd"))
ANY = pl.BlockSpec(memory_space=pl.ANY)


def _place():
    return lax.axis_index("x"), lax.axis_index("y"), lax.axis_index("c")


def _other_chips(x, y):
    return [(1 - x, y), (x, 1 - y), (1 - x, 1 - y)]


def _remote(src, dst, send_sem, recv_sem, to):
    return pltpu.make_async_remote_copy(src_ref=src, dst_ref=dst, send_sem=send_sem, recv_sem=recv_sem,
                                        device_id=to, device_id_type=MESH)


def _hbm_call(body, *, name, n_in, out_shape, scratch, in_place=False):
    return pl.pallas_call(body, name=name, out_shape=tuple(out_shape), in_specs=[ANY] * n_in,
                          out_specs=tuple(ANY for _ in out_shape), scratch_shapes=list(scratch),
                          input_output_aliases={k: k for k in range(n_in)} if in_place else {},
                          compiler_params=pltpu.CompilerParams(vmem_limit_bytes=VMEM_LIMIT))


def place_vector():
    x, y, c = _place()
    return jnp.stack([c, 2 * x + y] + [2 * cx + cy for cx, cy in _other_chips(x, y)]).astype(jnp.int32)


PUSH_ROWS = 512
PUSH_SLOTS = 4


def _pusher(src, dst, buf, load_sem, send_sem, recv_sem, to, src_at, dst_at, nt):
    tr = buf.shape[1]
    n = len(src_at) * nt

    def pick(vals, seg):
        out = vals[0]
        for q in range(1, len(vals)):
            out = jnp.where(seg == q, vals[q], out)
        return out

    def tile(ref, at, t):
        seg = t // nt
        row = pl.multiple_of(pick([a[1] for a in at], seg) + (t - seg * nt) * tr, 16)
        return ref.at[pick([a[0] for a in at], seg), pl.ds(row, tr), :]

    def load(t):
        slot = t % PUSH_SLOTS
        return pltpu.make_async_copy(tile(src, src_at, t), buf.at[slot], load_sem.at[slot])

    def send(t):
        slot = t % PUSH_SLOTS
        return _remote(buf.at[slot], tile(dst, dst_at, t), send_sem.at[slot], recv_sem, to)

    def step(t, carry):
        load(t).wait()
        send(t).start()

        @pl.when(t >= PUSH_SLOTS - 1)
        def _():
            send(t - (PUSH_SLOTS - 1)).wait_send()

        @pl.when(t + 1 < n)
        def _():
            load(t + 1).start()

        return carry

    def prime():
        load(0).start()

    def run():
        lax.fori_loop(0, n, step, 0)

    def drain():
        for t in range(max(n - (PUSH_SLOTS - 1), 0), n):
            send(t).wait_send()

    return prime, run, drain


def _push_all(pushers):
    pushers[0][0]()
    for i, (_, run, drain) in enumerate(pushers):
        run()
        if i + 1 < len(pushers):
            pushers[i + 1][0]()
        drain()


def _await_tiles(dst, send_sem, recv_sem, to, nseg, rows):
    whole = dst.at[pl.ds(0, nseg), pl.ds(0, rows), :]
    _remote(whole, whole, send_sem.at[0], recv_sem, to).wait_recv()


def _push_rows(seg_rows):
    return _row_tile(seg_rows, PUSH_ROWS)


def _push_scratch(arrs, seg_rows):
    return ([pltpu.VMEM((PUSH_SLOTS, _push_rows(r), a.shape[-1]), a.dtype) for a, r in zip(arrs, seg_rows)]
            + [pltpu.SemaphoreType.DMA((len(arrs), PUSH_SLOTS)), pltpu.SemaphoreType.DMA((len(arrs), PUSH_SLOTS)),
               pltpu.SemaphoreType.DMA((len(arrs),))])


def cast_into(w, l, pos, after=None):
    _, r, cols = w.shape
    tr = _row_tile(r)
    extra, extra_specs = _behind(after)

    def body(pos_ref, w_ref, *rest):
        del pos_ref
        rest[-1][...] = w_ref[...].astype(BF16)

    return _call(body, name="cast_into", out_shape=_sds((NCHIP, r, cols), BF16), grid=(r // tr,),
                 in_specs=[pl.BlockSpec((None, tr, cols), lambda i, pos: (l, i, 0))] + extra_specs,
                 out_specs=pl.BlockSpec((None, tr, cols), lambda i, pos: (pos[1], i, 0)),
                 sem=("parallel",), prefetch=1)(pos, w, *extra)


HBM = pl.BlockSpec(memory_space=pltpu.HBM)
SEM = pl.BlockSpec(memory_space=pltpu.SEMAPHORE)
EFFECT = pltpu.SideEffectType.DATAFLOW_SIDE_EFFECTING


def _in_hbm(a):
    return pltpu.with_memory_space_constraint(a, pltpu.HBM)


def _gather_copies(bufs, send, recv):
    x, y, c = _place()
    chips = _other_chips(x, y)

    def half(k, chip):
        hr = bufs[k].shape[1] // 2
        return bufs[k].at[chip, pl.ds(c * hr, hr), :]

    out, back = [], []
    for k in range(len(bufs)):
        for j, (cx, cy) in enumerate(chips):
            i = 3 * k + j
            out.append(_remote(half(k, 2 * x + y), half(k, 2 * x + y), send.at[i], recv.at[i], (cx, cy, c)))
            back.append(_remote(half(k, 2 * x + y), half(k, 2 * cx + cy), send.at[i], recv.at[i], (cx, cy, c)))
    return out, back


def gather_start(fulls):
    nk = len(fulls)

    def body(*refs):
        send, recv = refs[nk], refs[nk + 1]
        bufs, token = refs[nk + 2:2 * nk + 2], refs[2 * nk + 2]
        for cp in _gather_copies(bufs, send, recv)[0]:
            cp.start()
        token[...] = jnp.zeros_like(token)

    res = pl.pallas_call(
        body, name="gather_start",
        out_shape=(pltpu.SemaphoreType.DMA((3 * nk,)), pltpu.SemaphoreType.DMA((3 * nk,)),
                   *[pltpu.HBM(f.shape, f.dtype) for f in fulls], _sds((8, LANES), F32)),
        in_specs=[HBM] * nk, out_specs=(SEM, SEM, *[HBM] * nk, pl.BlockSpec(memory_space=pltpu.VMEM)),
        input_output_aliases={k: 2 + k for k in range(nk)},
        compiler_params=pltpu.CompilerParams(has_side_effects=EFFECT))(*[_in_hbm(f) for f in fulls])
    return res[0], res[1], list(res[2:2 + nk]), res[2 + nk]


def gather_wait(send, recv, fulls, after):
    nk = len(fulls)

    def body(*refs):
        bufs, send_ref, recv_ref = refs[:nk], refs[nk], refs[nk + 1]
        for cp in _gather_copies(bufs, send_ref, recv_ref)[1]:
            cp.wait_send()
            cp.wait_recv()

    return list(pl.pallas_call(
        body, name="gather_wait", out_shape=tuple(pltpu.HBM(f.shape, f.dtype) for f in fulls),
        in_specs=[HBM] * nk + [SEM, SEM, ANY], out_specs=tuple([HBM] * nk),
        input_output_aliases={k: k for k in range(nk)},
        compiler_params=pltpu.CompilerParams(has_side_effects=EFFECT))(*fulls, send, recv, after))


def pair_forward(fulls):
    nk = len(fulls)

    def body(*refs):
        bufs = refs[nk:2 * nk]
        stage = refs[2 * nk:3 * nk]
        load_sem, send_sem, recv_sem = refs[3 * nk:]
        x, y, c = _place()
        sib = (x, y, 1 - c)
        chips = [2 * cx + cy for cx, cy in _other_chips(x, y)]
        pushers = []
        for k in range(nk):
            hr = bufs[k].shape[1] // 2
            at = [(chip, c * hr) for chip in chips]
            pushers.append(_pusher(bufs[k], bufs[k], stage[k], load_sem.at[k], send_sem.at[k], recv_sem.at[k], sib,
                                   at, at, hr // _push_rows(hr)))
        _push_all(pushers)
        for k in range(nk):
            _await_tiles(bufs[k], send_sem.at[k], recv_sem.at[k], sib, 3, bufs[k].shape[1] // 2)

    return _hbm_call(body, name="pair_forward", n_in=nk, out_shape=[_sds(f.shape, f.dtype) for f in fulls], in_place=True,
                     scratch=_push_scratch(fulls, [f.shape[1] // 2 for f in fulls]))(*fulls)


N_PEERS = 7
PEER_FLIPS = [(fx, fy, fc) for fx in (0, 1) for fy in (0, 1) for fc in (0, 1)][1:]


def _chip_copies(ps, lands, send, recv):
    x, y, c = _place()
    cps = []
    for k in range(len(ps)):
        hr = ps[k].shape[1] // 2
        for r, (fx, fy, fc) in enumerate(PEER_FLIPS):
            px, py, pc = x ^ fx, y ^ fy, c ^ fc
            cps.append(_remote(ps[k].at[2 * px + py, pl.ds(pc * hr, hr), :], lands[k].at[r],
                               send.at[N_PEERS * k + r], recv.at[N_PEERS * k + r], (px, py, pc)))
    return cps


def chip_exchange_start(ps, after=None):
    nk = len(ps)
    lands = [lax.empty((N_PEERS, p.shape[1] // 2, p.shape[2]), p.dtype) for p in ps]
    extra, extra_specs = _behind(after)

    def body(*refs):
        refs = refs[2 * nk + len(extra):]
        send, recv = refs[0], refs[1]
        srcs, dsts, token = refs[2:nk + 2], refs[nk + 2:2 * nk + 2], refs[2 * nk + 2]
        for cp in _chip_copies(srcs, dsts, send, recv):
            cp.start()
        token[...] = jnp.zeros_like(token)

    res = pl.pallas_call(
        body, name="chip_exchange_start",
        out_shape=(pltpu.SemaphoreType.DMA((N_PEERS * nk,)), pltpu.SemaphoreType.DMA((N_PEERS * nk,)),
                   *[pltpu.HBM(a.shape, a.dtype) for a in ps + lands], _sds((8, LANES), F32)),
        in_specs=[HBM] * (2 * nk) + extra_specs,
        out_specs=(SEM, SEM, *[HBM] * (2 * nk), pl.BlockSpec(memory_space=pltpu.VMEM)),
        input_output_aliases={k: 2 + k for k in range(2 * nk)},
        compiler_params=pltpu.CompilerParams(has_side_effects=EFFECT))(*[_in_hbm(a) for a in ps + lands], *extra)
    return res[0], res[1], list(res[2:2 + nk]), list(res[2 + nk:2 + 2 * nk]), res[2 + 2 * nk]


def chip_exchange_wait(send, recv, ps, lands, after):
    nk = len(ps)

    def body(*refs):
        srcs, dsts, send_ref, recv_ref = refs[:nk], refs[nk:2 * nk], refs[2 * nk], refs[2 * nk + 1]
        for cp in _chip_copies(srcs, dsts, send_ref, recv_ref):
            cp.wait_send()
            cp.wait_recv()

    res = pl.pallas_call(
        body, name="chip_exchange_wait", out_shape=tuple(pltpu.HBM(a.shape, a.dtype) for a in ps + lands),
        in_specs=[HBM] * (2 * nk) + [SEM, SEM, ANY], out_specs=tuple([HBM] * (2 * nk)),
        input_output_aliases={k: k for k in range(2 * nk)},
        compiler_params=pltpu.CompilerParams(has_side_effects=EFFECT))(*ps, *lands, send, recv, after)
    return list(res[:nk]), list(res[nk:])


def rs_pair_share(fulls):
    nk = len(fulls)

    def body(*refs):
        bufs = refs[nk:2 * nk]
        stage = refs[2 * nk:3 * nk]
        load_sem, send_sem, recv_sem = refs[3 * nk:]
        x, y, c = _place()
        sib = (x, y, 1 - c)
        pushers = []
        for k in range(nk):
            hr = bufs[k].shape[1]
            pushers.append(_pusher(bufs[k], bufs[k], stage[k], load_sem.at[k], send_sem.at[k], recv_sem.at[k], sib,
                                   [(c, 0)], [(c, 0)], hr // _push_rows(hr)))
        _push_all(pushers)
        for k in range(nk):
            _await_tiles(bufs[k], send_sem.at[k], recv_sem.at[k], sib, 1, bufs[k].shape[1])

    return _hbm_call(body, name="rs_pair_share", n_in=nk, out_shape=[_sds(f.shape, f.dtype) for f in fulls], in_place=True,
                     scratch=_push_scratch(fulls, [f.shape[1] for f in fulls]))(*fulls)


N_DEV = 8


def allreduce_small(buf, after=None):
    rows = buf.shape[0]
    rp = rows // N_DEV
    extra, extra_specs = _behind(after)

    def body(in_ref, *rest):
        out_ref, land_ref, send, recv = rest[-4:]
        x, y, c = _place()
        mine = pl.ds(pl.multiple_of((4 * x + 2 * y + c) * rp, 8), rp)
        peers = [(x ^ fx, y ^ fy, c ^ fc) for fx in (0, 1) for fy in (0, 1) for fc in (0, 1)][1:]
        block = [pl.ds(pl.multiple_of((4 * px + 2 * py + pc) * rp, 8), rp) for px, py, pc in peers]
        scatter = [_remote(in_ref.at[block[r], :], land_ref.at[r], send.at[r], recv.at[r], peers[r]) for r in range(7)]
        for cp in scatter:
            cp.start()
        for cp in scatter:
            cp.wait()
        acc = in_ref[mine, :]
        for r in range(7):
            acc = acc + land_ref[r]
        out_ref[mine, :] = acc
        spread = [_remote(out_ref.at[mine, :], out_ref.at[mine, :], send.at[7 + r], recv.at[7 + r], peers[r]) for r in range(7)]
        for cp in spread:
            cp.start()
        for r in range(7):
            _remote(out_ref.at[block[r], :], out_ref.at[block[r], :], send.at[7 + r], recv.at[7 + r], peers[r]).wait_recv()
        for cp in spread:
            cp.wait_send()

    vm = pl.BlockSpec(memory_space=pltpu.VMEM)
    return pl.pallas_call(body, name="allreduce_small", out_shape=_sds(buf.shape, F32), in_specs=[vm] + extra_specs,
                          out_specs=vm,
                          scratch_shapes=[pltpu.VMEM((7, rp, LANES), F32), pltpu.SemaphoreType.DMA((14,)),
                                          pltpu.SemaphoreType.DMA((14,))])(buf, *extra)


def _row_tile(rows, cap=512):
    return max(t for t in range(16, cap + 1, 16) if rows % t == 0)


def add_slots(g, got, pos):
    _, r, cols = g.shape
    hr = r // 2
    tr = _row_tile(hr, 256)
    nt = hr // tr

    def body(pos_ref, o_ref, *rest):
        del pos_ref
        acc = o_ref[...].astype(F32)
        for ref in rest[:N_PEERS]:
            acc = acc + ref[...].astype(F32)
        rest[N_PEERS][...] = acc.astype(BF16)

    slots = [pl.BlockSpec((None, tr, cols), functools.partial(lambda j, i, pos: (j, i, 0), j)) for j in range(N_PEERS)]
    return _call(body, name="add_slots", out_shape=_sds((2, hr, cols), BF16), grid=(nt,),
                 in_specs=[pl.BlockSpec((None, tr, cols), lambda i, pos: (pos[1], pos[0] * nt + i, 0))] + slots,
                 out_specs=pl.BlockSpec((None, tr, cols), lambda i, pos: (pos[0], i, 0)),
                 sem=("parallel",), prefetch=1)(pos, g, *[got] * N_PEERS)


def _adamw_math(w, g, m, v):
    nm = B1 * m + (1.0 - B1) * g
    nv = B2 * v + (1.0 - B2) * (g * g)
    m_hat = nm / (1.0 - B1 ** STEP)
    v_hat = nv / (1.0 - B2 ** STEP)
    return -LR * (m_hat / (jnp.sqrt(v_hat) + AEPS) + WD * w), nm, nv


def adamw_layer(w, g, m, v, l, prev):
    nl, r, cols = w.shape
    tr = _row_tile(r, 256)

    def body(w_ref, g_ref, m_ref, v_ref, *rest):
        go_ref, d_ref, nm_ref, nv_ref = rest[-4:]
        gv = g_ref[...].astype(F32)
        d, nm, nv = _adamw_math(w_ref[...], gv, m_ref[...], v_ref[...])
        go_ref[...] = gv
        d_ref[...] = d
        nm_ref[...] = nm
        nv_ref[...] = nv

    lay = pl.BlockSpec((None, tr, cols), lambda i: (l, i, 0))
    o = _sds((nl, r, cols), F32)
    extra = [] if prev is None else list(prev)
    return _call(body, name="adamw_layer", out_shape=(o, o, o, o), grid=(r // tr,),
                 in_specs=[lay, pl.BlockSpec((tr, cols), lambda i: (i, 0)), lay, lay] + [ANY] * len(extra),
                 out_specs=(lay, lay, lay, lay), sem=("parallel",),
                 aliases={4 + j: j for j in range(len(extra))})(w, g, m, v, *extra)


def adamw(w, g, m, v):
    shape = w.shape
    cols = shape[-1]
    rows = w.size // cols
    tr = _row_tile(rows, 256)

    def body(w_ref, g_ref, m_ref, v_ref, d_ref, nm_ref, nv_ref):
        d_ref[...], nm_ref[...], nv_ref[...] = _adamw_math(w_ref[...], g_ref[...], m_ref[...], v_ref[...])

    blk = pl.BlockSpec((tr, cols), lambda i: (i, 0))
    o = _sds((rows, cols), F32)
    outs = _call(body, name="adamw", out_shape=(o, o, o), grid=(rows // tr,), in_specs=[blk] * 4, out_specs=(blk, blk, blk),
                 sem=("parallel",))(*[t.reshape(rows, cols) for t in (w, g, m, v)])
    return [t.reshape(shape) for t in outs]


SMALL = ("norm1_g", "gmlp_ln_g", "w_spatial", "b_spatial", "mix_norm_attn_g", "mix_norm_gmlp_g", "norm2_g", "final_g")
WEIGHTS = ("norm1_g", "w_in", "gmlp_ln_g", "w_spatial", "b_spatial", "mix_norm_attn_g", "mix_norm_gmlp_g", "w_out",
           "norm2_g", "w_gate", "w_up", "w_down", "final_g")
BIG = dict(win="w_in", wout="w_out", wg="w_gate", wu="w_up", wd="w_down")


def _pack(parts):
    flat = jnp.concatenate([parts[n].reshape(-1) for n in SMALL])
    rows = -(-flat.shape[0] // (LANES * 8 * N_DEV)) * 8 * N_DEV
    return jnp.pad(flat, (0, rows * LANES - flat.shape[0])).reshape(rows, LANES)


def _unpack(buf, like):
    flat = buf.reshape(-1)
    out, at = {}, 0
    for n in SMALL:
        out[n] = flat[at:at + like[n].size].reshape(like[n].shape)
        at += like[n].size
    return out


def kernel(x, norm1_g, w_in, gmlp_ln_g, w_spatial, b_spatial, mix_norm_attn_g, mix_norm_gmlp_g, w_out, norm2_g, w_gate, w_up, w_down, final_g, loss_target, m_norm1_g, m_w_in, m_gmlp_ln_g, m_w_spatial, m_b_spatial, m_mix_norm_attn_g, m_mix_norm_gmlp_g, m_w_out, m_norm2_g, m_w_gate, m_w_up, m_w_down, m_final_g, v_norm1_g, v_w_in, v_gmlp_ln_g, v_w_spatial, v_b_spatial, v_mix_norm_attn_g, v_mix_norm_gmlp_g, v_w_out, v_norm2_g, v_w_gate, v_w_up, v_w_down, v_final_g):
    w = dict(norm1_g=norm1_g, w_in=w_in, gmlp_ln_g=gmlp_ln_g, w_spatial=w_spatial, b_spatial=b_spatial,
             mix_norm_attn_g=mix_norm_attn_g, mix_norm_gmlp_g=mix_norm_gmlp_g, w_out=w_out, norm2_g=norm2_g,
             w_gate=w_gate, w_up=w_up, w_down=w_down, final_g=final_g)
    m = dict(norm1_g=m_norm1_g, w_in=m_w_in, gmlp_ln_g=m_gmlp_ln_g, w_spatial=m_w_spatial, b_spatial=m_b_spatial,
             mix_norm_attn_g=m_mix_norm_attn_g, mix_norm_gmlp_g=m_mix_norm_gmlp_g, w_out=m_w_out, norm2_g=m_norm2_g,
             w_gate=m_w_gate, w_up=m_w_up, w_down=m_w_down, final_g=m_final_g)
    v = dict(norm1_g=v_norm1_g, w_in=v_w_in, gmlp_ln_g=v_gmlp_ln_g, w_spatial=v_w_spatial, b_spatial=v_b_spatial,
             mix_norm_attn_g=v_mix_norm_attn_g, mix_norm_gmlp_g=v_mix_norm_gmlp_g, w_out=v_w_out, norm2_g=v_norm2_g,
             w_gate=v_w_gate, w_up=v_w_up, w_down=v_w_down, final_g=v_final_g)

    pos = place_vector()
    small = {n: w[n] for n in SMALL}
    tables = step_tables()

    def start_gathers(l, after):
        flights = []
        for kinds in GROUPS:
            flights.append(gather_start([cast_into(w[BIG[k]], l, pos, after) for k in kinds]))
            after = flights[-1][3]
        return flights, after

    def arrive(flight, kinds, after):
        fulls = pair_forward(gather_wait(flight[0], flight[1], flight[2], after))
        big = dict(zip(kinds, fulls))
        if "wout" in big:
            big["wout"] = big["wout"].reshape(D, D)
        if "wd" in big:
            big["wd"] = big["wd"].reshape(FF, D)
        return big

    act = x[0]
    flights, token = start_gathers(0, None)
    big, saved = [], []
    for l in range(NL):
        mine, after_mixer = flights, token
        if l + 1 < NL:
            flights, token = start_gathers(l + 1, act if l else token)
            after_mixer = token
        wl = arrive(mine[0], GROUPS[0], act if l else mine[0][3])
        x_mid, sv = mixer_fwd(act, small, l, wl, tables, after_mixer)
        wl.update(arrive(mine[1], GROUPS[1], x_mid))
        act, sv2 = ffn_fwd(x_mid, small, l, wl)
        big.append(wl)
        saved.append({**sv, **sv2})
    loss8, dx, dxb, dfinal = loss_head(act, _row(small["final_g"]), loss_target[0])
    loss = lax.psum(loss8[0, 0], ("x", "y", "c"))

    grads, delta, new_m, new_v = {}, {}, {}, {}
    stacked = {k: None for k in KINDS}
    gs = {k: [None] * NL for k in SMALL[:-1]}

    def send_off(gbig, kinds, l, after=None):
        return chip_exchange_start([gbig[k] for k in kinds], after) + (kinds, l)

    def finish(flight, after):
        partial_grads, lands = chip_exchange_wait(*flight[:4], after)
        halves = [add_slots(g, got, pos) for g, got in zip(partial_grads, lands)]
        for k, full in zip(flight[5], rs_pair_share(halves)):
            n = BIG[k]
            stacked[k] = adamw_layer(w[n], full.reshape(w[n].shape[1:]), m[n], v[n], flight[6], stacked[k])
        return stacked[flight[5][-1]][1]

    pending, token = [], None
    for l in reversed(range(NL)):
        dx, dxb, gs1, gbig = ffn_bwd(dx, dxb, small, l, big[l], saved[l], token)
        ffn_flight = send_off(gbig, GROUPS[1], l)
        for flight in pending:
            finish(flight, ffn_flight[4])
        dx, dxb, gs2, gbig = mixer_bwd(dx, dxb, small, l, big[l], saved[l], tables, ffn_flight[4])
        for k, g in {**gs1, **gs2}.items():
            gs[k][l] = g
        if l:
            mixer_flight = send_off(gbig, GROUPS[0], l)
            pending, token = [ffn_flight, mixer_flight], mixer_flight[4]
    part = {n: jnp.stack(gs[n]).reshape(w[n].shape) for n in SMALL[:-1]}
    part["final_g"] = dfinal.reshape(w["final_g"].shape)
    small_sum = allreduce_small(_pack(part), [gbig[k] for k in GROUPS[0]])
    mixer_flight = send_off(gbig, GROUPS[0], 0, small_sum)
    token = mixer_flight[4]
    for flight in (ffn_flight, mixer_flight):
        token = finish(flight, token)
    for k in KINDS:
        grads[BIG[k]], delta[BIG[k]], new_m[BIG[k]], new_v[BIG[k]] = stacked[k]
    grads.update(_unpack(small_sum, part))

    sm = adamw(_pack({n: w[n] for n in SMALL}), _pack({n: grads[n] for n in SMALL}), _pack({n: m[n] for n in SMALL}),
               _pack({n: v[n] for n in SMALL}))
    for res, packed in zip((delta, new_m, new_v), sm):
        res.update(_unpack(packed, part))

    return (loss, dx.reshape(x.shape), *[grads[n] for n in WEIGHTS], *[delta[n] for n in WEIGHTS],
            *[new_m[n] for n in WEIGHTS], *[new_v[n] for n in WEIGHTS])
```

```python
import functools

import jax
import jax.numpy as jnp
from jax import lax
from jax.experimental import pallas as pl
from jax.experimental.pallas import tpu as pltpu

F32 = jnp.float32
BF16 = jnp.bfloat16

D = 2048
T = 4096
NL = 4
HD = 128
DA = D // 2
DG = D - DA
NH = DA // HD
NG = DG // 128
CH = 128
DIN = 3 * DA + 2 * DG
FF = 5632
DILATIONS = (1, 4, 16)
NSIDE = 64
ROPE_THETA = 10000.0
EPS = 1e-6
NEG = -1e30
SCALE = HD ** -0.5
NCHIP = 4

LR, B1, B2, AEPS, WD, STEP = 0.001, 0.9, 0.999, 1e-08, 0.01, 10

LANES = 128
VMEM_LIMIT = 56 * 1024 * 1024
MESH = pl.DeviceIdType.MESH


def _call(body, *, name, out_shape, grid=(), in_specs=None, out_specs=None, scratch=(), sem=None, aliases=None,
          prefetch=0):
    params = dict(vmem_limit_bytes=VMEM_LIMIT)
    if sem is not None:
        params["dimension_semantics"] = sem
    if prefetch:
        spec = pltpu.PrefetchScalarGridSpec(num_scalar_prefetch=prefetch, grid=grid, in_specs=in_specs,
                                            out_specs=out_specs, scratch_shapes=list(scratch))
        return pl.pallas_call(body, name=name, out_shape=out_shape, grid_spec=spec,
                              input_output_aliases=aliases or {}, compiler_params=pltpu.CompilerParams(**params))
    kw = {}
    if grid:
        kw["grid"] = grid
    return pl.pallas_call(
        body, name=name, out_shape=out_shape, in_specs=in_specs, out_specs=out_specs,
        scratch_shapes=list(scratch), input_output_aliases=aliases or {},
        compiler_params=pltpu.CompilerParams(**params), **kw)


def _sds(shape, dtype):
    return jax.ShapeDtypeStruct(tuple(shape), dtype)


def _dot(a, b):
    return jnp.dot(a, b, preferred_element_type=F32)


def _dot_nt(a, b):
    return lax.dot_general(a, b, (((1,), (1,)), ((), ())), preferred_element_type=F32)


def _dot_tn(a, b):
    return lax.dot_general(a, b, (((0,), (0,)), ((), ())), preferred_element_type=F32)


def _sigmoid(x):
    return 0.5 * jnp.tanh(0.5 * x) + 0.5


def _behind(after):
    arrays = [] if after is None else list(after) if isinstance(after, (list, tuple)) else [after]
    return arrays, [pl.BlockSpec(memory_space=pl.ANY)] * len(arrays)


def rms_fwd(x, g, after=None):
    n, d = x.shape
    tm = 256
    extra, extra_specs = _behind(after)

    def body(x_ref, g_ref, *rest):
        h_ref = rest[-1]
        xv = x_ref[...]
        r = lax.rsqrt(jnp.mean(xv * xv, axis=-1, keepdims=True) + EPS)
        h_ref[...] = (xv * r * g_ref[...]).astype(BF16)

    return _call(body, name="rms_fwd", out_shape=_sds((n, d), BF16), grid=(n // tm,),
                 in_specs=[pl.BlockSpec((tm, d), lambda i: (i, 0)), pl.BlockSpec((1, d), lambda i: (0, 0))] + extra_specs,
                 out_specs=pl.BlockSpec((tm, d), lambda i: (i, 0)), sem=("parallel",))(x, g, *extra)


def rms_bwd(dh, x, g, dres):
    n, d = x.shape
    tm = 256

    def body(dh_ref, x_ref, g_ref, dres_ref, dx_ref, dxb_ref, dg_ref):
        @pl.when(pl.program_id(0) == 0)
        def _():
            dg_ref[...] = jnp.zeros_like(dg_ref)

        xv = x_ref[...]
        r = lax.rsqrt(jnp.mean(xv * xv, axis=-1, keepdims=True) + EPS)
        xhat = xv * r
        dhv = dh_ref[...].astype(F32)
        dg_ref[...] += jnp.sum(dhv * xhat, axis=0, keepdims=True)
        dxn = dhv * g_ref[...]
        dx = dres_ref[...] + r * (dxn - xhat * jnp.mean(dxn * xhat, axis=-1, keepdims=True))
        dx_ref[...] = dx
        dxb_ref[...] = dx.astype(BF16)

    row = pl.BlockSpec((tm, d), lambda i: (i, 0))
    vec = pl.BlockSpec((1, d), lambda i: (0, 0))
    return _call(body, name="rms_bwd", out_shape=(_sds((n, d), F32), _sds((n, d), BF16), _sds((1, d), F32)),
                 grid=(n // tm,), in_specs=[row, row, vec, row], out_specs=(row, row, vec), sem=("arbitrary",))(dh, x, g, dres)


def loss_head(x, g, target):
    n, d = x.shape
    tm = 256

    def body(x_ref, g_ref, t_ref, loss_ref, dx_ref, dxb_ref, dg_ref):
        @pl.when(pl.program_id(0) == 0)
        def _():
            dg_ref[...] = jnp.zeros_like(dg_ref)
            loss_ref[...] = jnp.zeros_like(loss_ref)

        xv = x_ref[...]
        r = lax.rsqrt(jnp.mean(xv * xv, axis=-1, keepdims=True) + EPS)
        xhat = xv * r
        gv = g_ref[...]
        err = xhat * gv - t_ref[...]
        loss_ref[...] += 0.5 * jnp.sum(jnp.mean(err * err, axis=-1, keepdims=True))
        dy = err * (1.0 / d)
        dg_ref[...] += jnp.sum(dy * xhat, axis=0, keepdims=True)
        dxn = dy * gv
        dx = r * (dxn - xhat * jnp.mean(dxn * xhat, axis=-1, keepdims=True))
        dx_ref[...] = dx
        dxb_ref[...] = dx.astype(BF16)

    row = pl.BlockSpec((tm, d), lambda i: (i, 0))
    vec = pl.BlockSpec((1, d), lambda i: (0, 0))
    return _call(body, name="loss_head",
                 out_shape=(_sds((8, LANES), F32), _sds((n, d), F32), _sds((n, d), BF16), _sds((1, d), F32)),
                 grid=(n // tm,), in_specs=[row, vec, row],
                 out_specs=(pl.BlockSpec((8, LANES), lambda i: (0, 0)), row, row, vec), sem=("arbitrary",))(x, g, target)


def mm_cols(a, wc, tm=512):
    n, k = a.shape
    s, _, nq = wc.shape

    def body(a_ref, w_ref, o_ref):
        o_ref[...] = _dot(a_ref[...], w_ref[...]).astype(BF16)

    return _call(body, name="mm_cols", out_shape=_sds((n, s * nq), BF16), grid=(s, n // tm),
                 in_specs=[pl.BlockSpec((tm, k), lambda j, i: (i, 0)), pl.BlockSpec((None, k, nq), lambda j, i: (j, 0, 0))],
                 out_specs=pl.BlockSpec((tm, nq), lambda j, i: (i, j)), sem=("parallel", "parallel"))(a, wc)


def mm_gateup(h, wg, wu, tm=512):
    n, k = h.shape
    s, _, nq = wg.shape

    def body(h_ref, wg_ref, wu_ref, gate_ref, up_ref, ff_ref):
        hv = h_ref[...]
        gate = _dot(hv, wg_ref[...])
        up = _dot(hv, wu_ref[...])
        gate_ref[...] = gate.astype(BF16)
        up_ref[...] = up.astype(BF16)
        ff_ref[...] = (gate * _sigmoid(gate) * up).astype(BF16)

    wspec = pl.BlockSpec((None, k, nq), lambda j, i: (j, 0, 0))
    ospec = pl.BlockSpec((tm, nq), lambda j, i: (i, j))
    o = _sds((n, s * nq), BF16)
    return _call(body, name="mm_gateup", out_shape=(o, o, o), grid=(s, n // tm),
                 in_specs=[pl.BlockSpec((tm, k), lambda j, i: (i, 0)), wspec, wspec],
                 out_specs=(ospec, ospec, ospec), sem=("parallel", "parallel"))(h, wg, wu)


def mm_rows_res(a, wr, res, tm=512, tn=1024):
    n, k = a.shape
    _, nn = wr.shape

    def body(a_ref, w_ref, r_ref, o_ref):
        o_ref[...] = r_ref[...] + _dot(a_ref[...], w_ref[...])

    return _call(body, name="mm_rows_res", out_shape=_sds((n, nn), F32), grid=(nn // tn, n // tm),
                 in_specs=[pl.BlockSpec((tm, k), lambda j, i: (i, 0)), pl.BlockSpec((k, tn), lambda j, i: (0, j)),
                           pl.BlockSpec((tm, tn), lambda j, i: (i, j))],
                 out_specs=pl.BlockSpec((tm, tn), lambda j, i: (i, j)), sem=("parallel", "parallel"))(a, wr, res)


def mm_nt_rows(dy, wr, tk, tm=512, after=None):
    n, nn = dy.shape
    k, _ = wr.shape
    extra, extra_specs = _behind(after)

    def body(dy_ref, w_ref, *rest):
        rest[-1][...] = _dot_nt(dy_ref[...], w_ref[...]).astype(BF16)

    return _call(body, name="mm_nt_rows", out_shape=_sds((n, k), BF16), grid=(k // tk, n // tm),
                 in_specs=[pl.BlockSpec((tm, nn), lambda j, i: (i, 0)), pl.BlockSpec((tk, nn), lambda j, i: (j, 0))]
                 + extra_specs,
                 out_specs=pl.BlockSpec((tm, tk), lambda j, i: (i, j)), sem=("parallel", "parallel"))(dy, wr, *extra)


def mm_dff(dy, wd, gate, up, tk, tm=512, after=None):
    n, nn = dy.shape
    k, _ = wd.shape
    extra, extra_specs = _behind(after)

    def body(dy_ref, w_ref, gate_ref, up_ref, *rest):
        dgate_ref, dup_ref = rest[-2:]
        dff = _dot_nt(dy_ref[...], w_ref[...])
        gate = gate_ref[...].astype(F32)
        up = up_ref[...].astype(F32)
        sig = _sigmoid(gate)
        dgate_ref[...] = (dff * up * (sig * (1.0 + gate * (1.0 - sig)))).astype(BF16)
        dup_ref[...] = (dff * (gate * sig)).astype(BF16)

    tile = pl.BlockSpec((tm, tk), lambda j, i: (i, j))
    o = _sds((n, k), BF16)
    return _call(body, name="mm_dff", out_shape=(o, o), grid=(k // tk, n // tm),
                 in_specs=[pl.BlockSpec((tm, nn), lambda j, i: (i, 0)), pl.BlockSpec((tk, nn), lambda j, i: (j, 0)), tile, tile]
                 + extra_specs,
                 out_specs=(tile, tile), sem=("parallel", "parallel"))(dy, wd, gate, up, *extra)


def mm_nt_cols(das, wcs, tm=512):
    n = das[0].shape[0]
    s, k, nq = wcs[0].shape
    npair = len(das)

    def body(*refs):
        da_refs, w_refs, o_ref, acc = refs[:npair], refs[npair:2 * npair], refs[2 * npair], refs[2 * npair + 1]
        ss = pl.program_id(1)

        @pl.when(ss == 0)
        def _():
            acc[...] = jnp.zeros_like(acc)

        for da_ref, w_ref in zip(da_refs, w_refs):
            acc[...] += _dot_nt(da_ref[...], w_ref[...])

        @pl.when(ss == s - 1)
        def _():
            o_ref[...] = acc[...].astype(BF16)

    return _call(body, name="mm_nt_cols%d" % npair, out_shape=_sds((n, k), BF16), grid=(n // tm, s),
                 in_specs=[pl.BlockSpec((tm, nq), lambda i, ss: (i, ss))] * npair
                 + [pl.BlockSpec((None, k, nq), lambda i, ss: (ss, 0, 0))] * npair,
                 out_specs=pl.BlockSpec((tm, k), lambda i, ss: (i, 0)),
                 scratch=[pltpu.VMEM((tm, k), F32)], sem=("parallel", "arbitrary"))(*das, *wcs)


def mm_tn(a, b, nq, shard_major, tka=512):
    n, ka = a.shape
    nb = b.shape[1] // nq

    def body(a_ref, b_ref, o_ref):
        o_ref[...] = _dot_tn(a_ref[...], b_ref[...]).astype(BF16)

    if shard_major:
        out_shape, out_spec = _sds((nb, ka, nq), BF16), pl.BlockSpec((None, tka, nq), lambda j, i: (j, i, 0))
    else:
        out_shape, out_spec = _sds((ka, nb * nq), BF16), pl.BlockSpec((tka, nq), lambda j, i: (i, j))
    return _call(body, name="mm_tn", out_shape=out_shape, grid=(nb, ka // tka),
                 in_specs=[pl.BlockSpec((n, tka), lambda j, i: (0, i)), pl.BlockSpec((n, nq), lambda j, i: (0, j))],
                 out_specs=out_spec, sem=("parallel", "parallel"))(a, b)


def rope_tables(n):
    pos = jnp.arange(n, dtype=F32)
    inv = ROPE_THETA ** (-jnp.arange(0, HD, 2, dtype=F32) / HD)
    ang = pos[:, None] * inv[None, :]
    cos, sin = jnp.cos(ang), jnp.sin(ang)
    return jnp.concatenate([cos, cos], axis=-1), jnp.concatenate([-sin, sin], axis=-1)


PERM_ROWS = 256


def perm_matrix(d):
    n = PERM_ROWS // d
    i = jnp.arange(PERM_ROWS)
    src = (i % n) * d + i // n
    return (src[:, None] == jnp.arange(PERM_ROWS)[None, :]).astype(BF16)


def _perm_f32(p, x):
    hi = x.astype(BF16)
    rest = x - hi.astype(F32)
    mid = rest.astype(BF16)
    lo = (rest - mid.astype(F32)).astype(BF16)
    return (_dot(p, hi) + _dot(p, mid)) + _dot(p, lo)


def _res_spec(d, cols, col_block=0):
    return pl.BlockSpec((d, PERM_ROWS // d, cols), lambda i: (0, i, col_block))


def _perm_specs():
    return [pl.BlockSpec((PERM_ROWS, PERM_ROWS), lambda i: (0, 0))] * (len(DILATIONS) - 1)


def qkv_split(proj, cos2, sin2, perms):
    n = proj.shape[0]
    tm = PERM_ROWS

    def body(p_ref, c_ref, s_ref, *rest):
        perm_refs, outs = rest[:len(DILATIONS) - 1], rest[len(DILATIONS) - 1:]
        c, s = c_ref[...], s_ref[...]
        for h in range(2 * NH):
            t = p_ref[:, h * HD:(h + 1) * HD].astype(F32)
            outs[0][0, :, h * HD:(h + 1) * HD] = (t * c + pltpu.roll(t, HD // 2, 1) * s).astype(BF16)
        outs[0][0, :, 2 * DA:] = p_ref[:, 2 * DA:]
        nat = outs[0][0]
        for o_ref, p_ref_d, d in zip(outs[1:], perm_refs, DILATIONS[1:]):
            o_ref[...] = _dot(p_ref_d[...], nat).astype(BF16).reshape(d, tm // d, 3 * DA)

    tab = pl.BlockSpec((tm, HD), lambda i: (i, 0))
    return _call(body, name="qkv_split", out_shape=tuple(_sds((d, n // d, 3 * DA), BF16) for d in DILATIONS),
                 grid=(n // tm,), in_specs=[pl.BlockSpec((tm, 3 * DA), lambda i: (i, 0)), tab, tab] + _perm_specs(),
                 out_specs=tuple(_res_spec(d, 3 * DA) for d in DILATIONS), sem=("parallel",))(proj, cos2, sin2, *perms)


def dproj_assemble(dqs, dks, dvs, duv, cos2, sin2, perms_t):
    n = duv.shape[0]
    tm = PERM_ROWS
    npat = len(DILATIONS)

    def body(*refs):
        parts = refs[0:npat], refs[npat:2 * npat], refs[2 * npat:3 * npat]
        duv_ref, c_ref, s_ref = refs[3 * npat:3 * npat + 3]
        pt_refs, o_ref = refs[3 * npat + 3:-1], refs[-1]
        c, s = c_ref[...], s_ref[...]
        for part, part_refs in enumerate(parts):
            t = part_refs[0][0].astype(F32)
            for ref, pt_ref in zip(part_refs[1:], pt_refs):
                t = t + _dot(pt_ref[...], ref[...].reshape(tm, DA))
            if part == 2:
                o_ref[:, 2 * DA:3 * DA] = t.astype(BF16)
                continue
            for h in range(NH):
                th = t[:, h * HD:(h + 1) * HD]
                o_ref[:, part * DA + h * HD:part * DA + (h + 1) * HD] = (th * c - pltpu.roll(th, HD // 2, 1) * s).astype(BF16)
        o_ref[:, 3 * DA:] = duv_ref[...]

    blks = [_res_spec(d, DA) for d in DILATIONS]
    tab = pl.BlockSpec((tm, HD), lambda i: (i, 0))
    return _call(body, name="dproj_assemble", out_shape=_sds((n, DIN), BF16), grid=(n // tm,),
                 in_specs=blks * 3 + [pl.BlockSpec((tm, 2 * DG), lambda i: (i, 0)), tab, tab] + _perm_specs(),
                 out_specs=pl.BlockSpec((tm, DIN), lambda i: (i, 0)), sem=("parallel",))(
        *dqs, *dks, *dvs, duv, cos2, sin2, *perms_t)


ATTN_BLOCK_BYTES = 2 << 20


def _attn_tiles(ls):
    tq = min(128, ls)
    w = min(tq + 2 * NSIDE, ls)
    return tq, w, ls // tq


def _attn_heads(ls):
    return NH if ls * DA * 2 <= ATTN_BLOCK_BYTES else 1


ATTN_UNROLL = 8


def _attn_loop(nt, hb, tile):
    for h in range(hb):
        def trip(t, carry, h=h):
            tile(t, h)
            return carry

        lax.fori_loop(0, nt, trip, 0, unroll=min(nt, ATTN_UNROLL))


def _band(t, tq, w, ls):
    q0 = pl.multiple_of(t * tq, tq)
    ks = pl.multiple_of(jnp.clip(t * tq - NSIDE, 0, ls - w), NSIDE)
    qpos = q0 + lax.broadcasted_iota(jnp.int32, (tq, w), 0)
    kpos = ks + lax.broadcasted_iota(jnp.int32, (tq, w), 1)
    return q0, ks, jnp.abs(kpos - qpos) <= NSIDE


def _attn_specs(ls, hb):
    nhb = NH // hb
    q = pl.BlockSpec((None, ls, hb * HD), lambda g: (g // nhb, 0, g % nhb))
    k = pl.BlockSpec((None, ls, hb * HD), lambda g: (g // nhb, 0, nhb + g % nhb))
    v = pl.BlockSpec((None, ls, hb * HD), lambda g: (g // nhb, 0, 2 * nhb + g % nhb))
    return q, k, v, q


def _lse_spec(ls, hb):
    nhb = NH // hb
    return pl.BlockSpec((None, ls, LANES), lambda g: (g // nhb, 0, 0))


def attn_fwd(qkv):
    d, ls, _ = qkv.shape
    tq, w, nt = _attn_tiles(ls)
    hb = _attn_heads(ls)
    nhb = NH // hb

    def body(q_ref, k_ref, v_ref, o_ref, l_ref):
        first = (pl.program_id(0) % nhb) * hb
        lane = lax.broadcasted_iota(jnp.int32, (tq, LANES), 1)

        @pl.when(first == 0)
        def _():
            l_ref[...] = jnp.zeros_like(l_ref)

        def tile(t, h):
            q0, ks, valid = _band(t, tq, w, ls)
            cols = slice(h * HD, (h + 1) * HD)
            s = _dot_nt(q_ref[pl.ds(q0, tq), cols], k_ref[pl.ds(ks, w), cols]) * SCALE
            s = jnp.where(valid, s, NEG)
            m = jnp.max(s, axis=1, keepdims=True)
            p = jnp.exp(s - m)
            l = jnp.sum(p, axis=1, keepdims=True)
            o = _dot(p.astype(BF16), v_ref[pl.ds(ks, w), cols]) / l
            o_ref[pl.ds(q0, tq), cols] = o.astype(BF16)
            l_ref[pl.ds(q0, tq), :] = jnp.where(lane == first + h, m + jnp.log(l), l_ref[pl.ds(q0, tq), :])

        _attn_loop(nt, hb, tile)

    q, k, v, o = _attn_specs(ls, hb)
    return _call(body, name="attn_fwd_d%d" % d, out_shape=(_sds((d, ls, DA), BF16), _sds((d, ls, LANES), F32)),
                 grid=(d * NH // hb,), in_specs=[q, k, v], out_specs=(o, _lse_spec(ls, hb)), sem=("arbitrary",))(
        qkv, qkv, qkv)


def attn_merge(os_, ls_, ga, perms_t):
    n = os_[0].shape[1]
    tm = PERM_ROWS
    npat = len(DILATIONS)

    def body(*refs):
        o_refs, l_refs = refs[:npat], refs[npat:2 * npat]
        g_ref, pt_refs = refs[2 * npat], refs[2 * npat + 1:3 * npat]
        a_ref, lse_ref, mix_ref = refs[3 * npat:]
        ov = [o_refs[0][0].astype(F32)] + [_dot(pt[...], o[...].reshape(tm, DA)) for o, pt in zip(o_refs[1:], pt_refs)]
        lv = [l_refs[0][0]] + [_perm_f32(pt[...], l[...].reshape(tm, LANES)) for l, pt in zip(l_refs[1:], pt_refs)]
        lmax = functools.reduce(jnp.maximum, lv)
        e = [jnp.exp(x - lmax) for x in lv]
        den = functools.reduce(lambda p, q: p + q, e)
        wts = [ei / den for ei in e]
        lse_ref[...] = lmax + jnp.log(den)
        heads = []
        for h in range(NH):
            cols = slice(h * HD, (h + 1) * HD)
            heads.append(functools.reduce(lambda p, q: p + q, [wt[:, h:h + 1] * oi[:, cols] for wt, oi in zip(wts, ov)]))
        a = jnp.concatenate(heads, axis=1)
        a_ref[...] = a.astype(BF16)
        r = lax.rsqrt(jnp.mean(a * a, axis=-1, keepdims=True) + EPS)
        mix_ref[...] = (a * r * g_ref[...]).astype(BF16)

    blk = pl.BlockSpec((tm, DA), lambda i: (i, 0))
    return _call(body, name="attn_merge", out_shape=(_sds((n, DA), BF16), _sds((n, LANES), F32), _sds((n, D), BF16)),
                 grid=(n // tm,),
                 in_specs=[_res_spec(d, DA) for d in DILATIONS] + [_res_spec(d, LANES) for d in DILATIONS]
                 + [pl.BlockSpec((1, DA), lambda i: (0, 0))] + _perm_specs(),
                 out_specs=(blk, pl.BlockSpec((tm, LANES), lambda i: (i, 0)), blk), sem=("parallel",))(
        *os_, *ls_, ga, *perms_t)


def attn_norm_bwd(dmix, a, lse, ga, perms):
    n = a.shape[0]
    tm = PERM_ROWS
    npat = len(DILATIONS)

    def body(dm_ref, a_ref, lse_ref, g_ref, *rest):
        p_refs = rest[:npat - 1]
        da_refs, a_refs, lse_refs = rest[npat - 1:2 * npat - 1], rest[2 * npat - 1:3 * npat - 2], rest[3 * npat - 2:4 * npat - 3]
        dg_ref = rest[-1]

        @pl.when(pl.program_id(0) == 0)
        def _():
            dg_ref[...] = jnp.zeros_like(dg_ref)

        ab = a_ref[...]
        av = ab.astype(F32)
        r = lax.rsqrt(jnp.mean(av * av, axis=-1, keepdims=True) + EPS)
        ahat = av * r
        dm = dm_ref[...].astype(F32)
        dg_ref[...] += jnp.sum(dm * ahat, axis=0, keepdims=True)
        dn = dm * g_ref[...]
        da = (r * (dn - ahat * jnp.mean(dn * ahat, axis=-1, keepdims=True))).astype(BF16)
        da_refs[0][0] = da
        lv = lse_ref[...]
        for i, d in enumerate(DILATIONS[1:]):
            p = p_refs[i][...]
            da_refs[i + 1][...] = _dot(p, da).astype(BF16).reshape(d, tm // d, DA)
            a_refs[i][...] = _dot(p, ab).astype(BF16).reshape(d, tm // d, DA)
            lse_refs[i][...] = _perm_f32(p, lv).reshape(d, tm // d, LANES)

    blk = pl.BlockSpec((tm, DA), lambda i: (i, 0))
    vec = pl.BlockSpec((1, DA), lambda i: (0, 0))
    res = [_res_spec(d, DA) for d in DILATIONS]
    outs = _call(body, name="attn_norm_bwd",
                 out_shape=tuple([_sds((d, n // d, DA), BF16) for d in DILATIONS]
                                 + [_sds((d, n // d, DA), BF16) for d in DILATIONS[1:]]
                                 + [_sds((d, n // d, LANES), F32) for d in DILATIONS[1:]] + [_sds((1, DA), F32)]),
                 grid=(n // tm,),
                 in_specs=[blk, blk, pl.BlockSpec((tm, LANES), lambda i: (i, 0)), vec] + _perm_specs(),
                 out_specs=tuple(res + res[1:] + [_res_spec(d, LANES) for d in DILATIONS[1:]] + [vec]),
                 sem=("arbitrary",))(dmix, a, lse, ga, *perms)
    return outs[:npat], outs[npat:2 * npat - 1], outs[2 * npat - 1:3 * npat - 2], outs[-1]


def attn_bwd(qkv, a, da, lse):
    d, ls, _ = qkv.shape
    tq, w, nt = _attn_tiles(ls)
    hb = _attn_heads(ls)

    nhb = NH // hb

    def body(q_ref, k_ref, v_ref, a_ref, da_ref, lse_ref, dq_ref, dk_ref, dv_ref, dk_acc, dv_acc):
        first = (pl.program_id(0) % nhb) * hb
        lane = lax.broadcasted_iota(jnp.int32, (tq, LANES), 1)
        dk_acc[...] = jnp.zeros_like(dk_acc)
        dv_acc[...] = jnp.zeros_like(dv_acc)

        def tile(t, h):
            q0, ks, valid = _band(t, tq, w, ls)
            cols = slice(h * HD, (h + 1) * HD)
            q = q_ref[pl.ds(q0, tq), cols]
            k = k_ref[pl.ds(ks, w), cols]
            v = v_ref[pl.ds(ks, w), cols]
            do = da_ref[pl.ds(q0, tq), cols]
            if nhb == 1:
                lse_h = lse_ref[pl.ds(q0, tq), h:h + 1]
            else:
                lse_h = jnp.sum(jnp.where(lane == first + h, lse_ref[pl.ds(q0, tq), :], 0.0), axis=1, keepdims=True)
            s = jnp.where(valid, _dot_nt(q, k) * SCALE, NEG)
            p = jnp.exp(s - lse_h)
            drow = jnp.sum(do.astype(F32) * a_ref[pl.ds(q0, tq), cols].astype(F32), axis=1, keepdims=True)
            ds = (p * (_dot_nt(do, v) - drow) * SCALE).astype(BF16)
            dv_acc[pl.ds(ks, w), cols] += _dot_tn(p.astype(BF16), do)
            dk_acc[pl.ds(ks, w), cols] += _dot_tn(ds, q)
            dq_ref[pl.ds(q0, tq), cols] = _dot(ds, k).astype(BF16)

        _attn_loop(nt, hb, tile)
        dk_ref[...] = dk_acc[...].astype(BF16)
        dv_ref[...] = dv_acc[...].astype(BF16)

    q, k, v, o = _attn_specs(ls, hb)
    out = _sds((d, ls, DA), BF16)
    return _call(body, name="attn_bwd_d%d" % d, out_shape=(out, out, out), grid=(d * NH // hb,),
                 in_specs=[q, k, v, o, o, _lse_spec(ls, hb)], out_specs=(o, o, o),
                 scratch=[pltpu.VMEM((ls, hb * HD), F32), pltpu.VMEM((ls, hb * HD), F32)], sem=("parallel",))(
        qkv, qkv, qkv, a, da, lse)


GM_TM = 256
INV_SQRT2 = 0.7071067811865476
INV_SQRT2PI = 0.3989422804014327


def _gelu(x):
    return 0.5 * x * (1.0 + lax.erf(x * INV_SQRT2))


def _gelu_grad(x):
    return 0.5 * (1.0 + lax.erf(x * INV_SQRT2)) + x * (INV_SQRT2PI * jnp.exp(-0.5 * x * x))


def _gmlp_forward(up, vp, ln_g, ws_ref, bias):
    u = _gelu(up)
    v = _gelu(vp)
    vc = v - jnp.mean(v, axis=-1, keepdims=True)
    rs = lax.rsqrt(jnp.mean(vc * vc, axis=-1, keepdims=True) + EPS)
    vhat = vc * rs
    vln = (vhat * ln_g).astype(BF16)
    rows = []
    for c in range(GM_TM // CH):
        cols = [_dot(ws_ref[g], vln[c * CH:(c + 1) * CH, g * 128:(g + 1) * 128]) for g in range(NG)]
        rows.append(jnp.concatenate(cols, axis=1) + bias)
    return u, vhat, rs, vln, jnp.concatenate(rows, axis=0)


def _gmlp_specs():
    ublk = pl.BlockSpec((GM_TM, DG), lambda i: (i, 3 * DA // DG))
    vblk = pl.BlockSpec((GM_TM, DG), lambda i: (i, 3 * DA // DG + 1))
    vec = pl.BlockSpec((1, DG), lambda i: (0, 0))
    wsp = pl.BlockSpec((NG, CH, CH), lambda i: (0, 0, 0))
    bsp = pl.BlockSpec((CH, DG), lambda i: (0, 0))
    return ublk, vblk, vec, wsp, bsp


def gmlp_fwd(proj, ln_g, ws, bias, gg, mix):
    n = proj.shape[0]

    def body(up_ref, vp_ref, ln_ref, ws_ref, b_ref, gg_ref, mix_in, mix_ref):
        del mix_in
        u, _, _, _, mixed = _gmlp_forward(up_ref[...].astype(F32), vp_ref[...].astype(F32), ln_ref[...], ws_ref, b_ref[...])
        gout = u * mixed
        r = lax.rsqrt(jnp.mean(gout * gout, axis=-1, keepdims=True) + EPS)
        mix_ref[...] = (gout * r * gg_ref[...]).astype(BF16)

    ublk, vblk, vec, wsp, bsp = _gmlp_specs()
    return _call(body, name="gmlp_fwd", out_shape=_sds((n, D), BF16), grid=(n // GM_TM,),
                 in_specs=[ublk, vblk, vec, wsp, bsp, vec, pl.BlockSpec(memory_space=pl.ANY)],
                 out_specs=pl.BlockSpec((GM_TM, DG), lambda i: (i, DA // DG)), sem=("parallel",), aliases={6: 0})(
        proj, proj, ln_g, ws, bias, gg, mix)


def gmlp_bwd(proj, dmix, ln_g, ws, wst, bias, gg):
    n = proj.shape[0]

    def body(up_ref, vp_ref, dm_ref, ln_ref, ws_ref, wst_ref, b_ref, gg_ref, duv_ref, dln_ref, dws_ref, dbs_ref, dgg_ref,
             db_ref):
        @pl.when(pl.program_id(0) == 0)
        def _():
            dln_ref[...] = jnp.zeros_like(dln_ref)
            dws_ref[...] = jnp.zeros_like(dws_ref)
            db_ref[...] = jnp.zeros_like(db_ref)
            dgg_ref[...] = jnp.zeros_like(dgg_ref)

        up = up_ref[...].astype(F32)
        vp = vp_ref[...].astype(F32)
        ln_g = ln_ref[...]
        u, vhat, rs, vln, mixed = _gmlp_forward(up, vp, ln_g, ws_ref, b_ref[...])
        gout = u * mixed
        r = lax.rsqrt(jnp.mean(gout * gout, axis=-1, keepdims=True) + EPS)
        ghat = gout * r
        dm = dm_ref[...].astype(F32)
        dgg_ref[...] += jnp.sum(dm * ghat, axis=0, keepdims=True)
        dn = dm * gg_ref[...]
        dgout = r * (dn - ghat * jnp.mean(dn * ghat, axis=-1, keepdims=True))
        du = dgout * mixed
        dmixed = dgout * u
        dmb = dmixed.astype(BF16)
        rows = []
        for c in range(GM_TM // CH):
            rsl = slice(c * CH, (c + 1) * CH)
            db_ref[...] += dmixed[rsl, :]
            cols = []
            for g in range(NG):
                csl = slice(g * 128, (g + 1) * 128)
                dws_ref[g] += _dot_nt(dmb[rsl, csl], vln[rsl, csl])
                cols.append(_dot(wst_ref[g], dmb[rsl, csl]))
            rows.append(jnp.concatenate(cols, axis=1))
        dvln = jnp.concatenate(rows, axis=0)
        dln_ref[...] += jnp.sum(dvln * vhat, axis=0, keepdims=True)
        dvh = dvln * ln_g
        dv = rs * (dvh - jnp.mean(dvh, axis=-1, keepdims=True) - vhat * jnp.mean(dvh * vhat, axis=-1, keepdims=True))
        duv_ref[:, 0:DG] = (du * _gelu_grad(up)).astype(BF16)
        duv_ref[:, DG:] = (dv * _gelu_grad(vp)).astype(BF16)

        @pl.when(pl.program_id(0) == pl.num_programs(0) - 1)
        def _():
            for g in range(NG):
                dbs_ref[g:g + 1, :] = jnp.sum(jnp.transpose(db_ref[:, g * 128:(g + 1) * 128]), axis=0, keepdims=True)

    ublk, vblk, vec, wsp, bsp = _gmlp_specs()
    return _call(body, name="gmlp_bwd",
                 out_shape=(_sds((n, 2 * DG), BF16), _sds((1, DG), F32), _sds((NG, CH, CH), F32), _sds((NG, CH), F32),
                            _sds((1, DG), F32)),
                 grid=(n // GM_TM,),
                 in_specs=[ublk, vblk, pl.BlockSpec((GM_TM, DG), lambda i: (i, DA // DG)), vec, wsp, wsp, bsp, vec],
                 out_specs=(pl.BlockSpec((GM_TM, 2 * DG), lambda i: (i, 0)), vec, wsp,
                            pl.BlockSpec((NG, CH), lambda i: (0, 0)), vec),
                 scratch=[pltpu.VMEM((CH, DG), F32)], sem=("arbitrary",))(
        proj, proj, dmix, ln_g, ws, wst, bias, gg)


def _row(v):
    return v.reshape(1, -1)


def local_step(x, target, small, big):
    tables = step_tables()
    saved = []
    for l in range(NL):
        x_mid, sv = mixer_fwd(x, small, l, big[l], tables)
        x, sv2 = ffn_fwd(x_mid, small, l, big[l])
        saved.append({**sv, **sv2})
    loss8, dx, dxb, dfinal = loss_head(x, _row(small["final_g"]), target)
    gs = {k: [None] * NL for k in SMALL[:-1]}
    gbig = [None] * NL
    for l in reversed(range(NL)):
        dx, dxb, gs1, gb1 = ffn_bwd(dx, dxb, small, l, big[l], saved[l])
        dx, dxb, gs2, gb2 = mixer_bwd(dx, dxb, small, l, big[l], saved[l], tables)
        gbig[l] = {**gb1, **gb2}
        for k, g in {**gs1, **gs2}.items():
            gs[k][l] = g
    return loss8, dx, gs, dfinal, gbig


def step_tables():
    perms = [perm_matrix(d) for d in DILATIONS[1:]]
    return rope_tables(T) + (perms, [p.T for p in perms])


def mixer_fwd(x, small, l, w, tables, after=None):
    cos2, sin2, perms, perms_t = tables
    ws_b = small["w_spatial"][l].astype(BF16)
    bias = jnp.repeat(small["b_spatial"][l].T, 128, axis=1)
    h = rms_fwd(x, _row(small["norm1_g"][l]), after)
    proj = mm_cols(h, w["win"])
    qkvs = qkv_split(proj, cos2, sin2, perms)
    outs = [attn_fwd(qkv) for qkv in qkvs]
    a, lse, mix = attn_merge([o for o, _ in outs], [s for _, s in outs], _row(small["mix_norm_attn_g"][l]), perms_t)
    mix = gmlp_fwd(proj, _row(small["gmlp_ln_g"][l]), ws_b, bias, _row(small["mix_norm_gmlp_g"][l]), mix)
    x_mid = mm_rows_res(mix, w["wout"], x)
    return x_mid, dict(x=x, h=h, proj=proj, qkvs=qkvs, a=a, lse=lse, mix=mix, ws_b=ws_b, bias=bias)


def ffn_fwd(x_mid, small, l, w, after=None):
    h2 = rms_fwd(x_mid, _row(small["norm2_g"][l]), after)
    gate, up, ff = mm_gateup(h2, w["wg"], w["wu"])
    x_out = mm_rows_res(ff, w["wd"], x_mid)
    return x_out, dict(x_mid=x_mid, h2=h2, gate=gate, up=up, ff=ff)


def ffn_bwd(dx, dxb, small, l, w, sv, after=None):
    gs = {}
    dgate, dup = mm_dff(dxb, w["wd"], sv["gate"], sv["up"], tk=FF // NCHIP, after=after)
    g_wd = mm_tn(sv["ff"], dxb, D // 2, False)
    dh2 = mm_nt_cols([dgate, dup], [w["wg"], w["wu"]])
    g_wg = mm_tn(sv["h2"], dgate, FF // NCHIP, True)
    g_wu = mm_tn(sv["h2"], dup, FF // NCHIP, True)
    dx, dxb, gs["norm2_g"] = rms_bwd(dh2, sv["x_mid"], _row(small["norm2_g"][l]), dx)
    return dx, dxb, gs, dict(wg=g_wg, wu=g_wu, wd=g_wd.reshape(NCHIP, FF // NCHIP, D))


def mixer_bwd(dx, dxb, small, l, w, sv, tables, after=None):
    cos2, sin2, perms, perms_t = tables
    gs = {}
    dmix = mm_nt_rows(dxb, w["wout"], tk=D // 2, after=after)
    g_wout = mm_tn(sv["mix"], dxb, D // 2, False)
    das, a_res, lse_res, gs["mix_norm_attn_g"] = attn_norm_bwd(dmix, sv["a"], sv["lse"],
                                                                _row(small["mix_norm_attn_g"][l]), perms)
    a_all = [sv["a"].reshape((1,) + sv["a"].shape)] + list(a_res)
    lse_all = [sv["lse"].reshape((1,) + sv["lse"].shape)] + list(lse_res)
    parts = [attn_bwd(*operands) for operands in zip(sv["qkvs"], a_all, das, lse_all)]
    wst = jnp.swapaxes(small["w_spatial"][l], 1, 2).astype(BF16)
    duv, gs["gmlp_ln_g"], gs["w_spatial"], gs["b_spatial"], gs["mix_norm_gmlp_g"] = gmlp_bwd(
        sv["proj"], dmix, _row(small["gmlp_ln_g"][l]), sv["ws_b"], wst, sv["bias"], _row(small["mix_norm_gmlp_g"][l]))
    dproj = dproj_assemble([p[0] for p in parts], [p[1] for p in parts], [p[2] for p in parts], duv, cos2, sin2,
                           perms_t)
    dh = mm_nt_cols([dproj], [w["win"]])
    g_win = mm_tn(sv["h"], dproj, DIN // NCHIP, True)
    dx, dxb, gs["norm1_g"] = rms_bwd(dh, sv["x"], _row(small["norm1_g"][l]), dx)
    return dx, dxb, gs, dict(win=g_win, wout=g_wout.reshape(NCHIP, D // NCHIP, D))


KINDS = ("win", "wout", "wg", "wu", "wd")
GROUPS = (("win", "wout"), ("wg", "wu", "wd"))
ANY = pl.BlockSpec(memory_space=pl.ANY)


def _place():
    return lax.axis_index("x"), lax.axis_index("y"), lax.axis_index("c")


def _other_chips(x, y):
    return [(1 - x, y), (x, 1 - y), (1 - x, 1 - y)]


def _remote(src, dst, send_sem, recv_sem, to):
    return pltpu.make_async_remote_copy(src_ref=src, dst_ref=dst, send_sem=send_sem, recv_sem=recv_sem,
                                        device_id=to, device_id_type=MESH)


def _hbm_call(body, *, name, n_in, out_shape, scratch, in_place=False):
    return pl.pallas_call(body, name=name, out_shape=tuple(out_shape), in_specs=[ANY] * n_in,
                          out_specs=tuple(ANY for _ in out_shape), scratch_shapes=list(scratch),
                          input_output_aliases={k: k for k in range(n_in)} if in_place else {},
                          compiler_params=pltpu.CompilerParams(vmem_limit_bytes=VMEM_LIMIT))


def place_vector():
    x, y, c = _place()
    return jnp.stack([c, 2 * x + y] + [2 * cx + cy for cx, cy in _other_chips(x, y)]).astype(jnp.int32)


PUSH_ROWS = 512
PUSH_SLOTS = 3


def _push_tiles(src, dst, buf, load_sem, send_sem, recv_sem, to, src_at, dst_at, nt):
    tr = buf.shape[1]
    n = len(src_at) * nt

    def pick(vals, seg):
        out = vals[0]
        for q in range(1, len(vals)):
            out = jnp.where(seg == q, vals[q], out)
        return out

    def tile(ref, at, t):
        seg = t // nt
        row = pl.multiple_of(pick([a[1] for a in at], seg) + (t - seg * nt) * tr, 16)
        return ref.at[pick([a[0] for a in at], seg), pl.ds(row, tr), :]

    def load(t):
        slot = t % PUSH_SLOTS
        return pltpu.make_async_copy(tile(src, src_at, t), buf.at[slot], load_sem.at[slot])

    def send(t):
        slot = t % PUSH_SLOTS
        return _remote(buf.at[slot], tile(dst, dst_at, t), send_sem.at[slot], recv_sem, to)

    load(0).start()

    def step(t, carry):
        load(t).wait()
        send(t).start()

        @pl.when(t >= PUSH_SLOTS - 1)
        def _():
            send(t - (PUSH_SLOTS - 1)).wait_send()

        @pl.when(t + 1 < n)
        def _():
            load(t + 1).start()

        return carry

    lax.fori_loop(0, n, step, 0)
    for t in range(max(n - (PUSH_SLOTS - 1), 0), n):
        send(t).wait_send()


def _await_tiles(dst, send_sem, recv_sem, to, nseg, rows):
    whole = dst.at[pl.ds(0, nseg), pl.ds(0, rows), :]
    _remote(whole, whole, send_sem.at[0], recv_sem, to).wait_recv()


def _push_rows(seg_rows):
    return _row_tile(seg_rows, PUSH_ROWS)


def _push_scratch(arrs, seg_rows):
    return ([pltpu.VMEM((PUSH_SLOTS, _push_rows(r), a.shape[-1]), a.dtype) for a, r in zip(arrs, seg_rows)]
            + [pltpu.SemaphoreType.DMA((len(arrs), PUSH_SLOTS)), pltpu.SemaphoreType.DMA((len(arrs), PUSH_SLOTS)),
               pltpu.SemaphoreType.DMA((len(arrs),))])


def cast_into(w, l, pos, after=None):
    _, r, cols = w.shape
    tr = _row_tile(r)
    extra, extra_specs = _behind(after)

    def body(pos_ref, w_ref, *rest):
        del pos_ref
        rest[-1][...] = w_ref[...].astype(BF16)

    return _call(body, name="cast_into", out_shape=_sds((NCHIP, r, cols), BF16), grid=(r // tr,),
                 in_specs=[pl.BlockSpec((None, tr, cols), lambda i, pos: (l, i, 0))] + extra_specs,
                 out_specs=pl.BlockSpec((None, tr, cols), lambda i, pos: (pos[1], i, 0)),
                 sem=("parallel",), prefetch=1)(pos, w, *extra)


HBM = pl.BlockSpec(memory_space=pltpu.HBM)
SEM = pl.BlockSpec(memory_space=pltpu.SEMAPHORE)
EFFECT = pltpu.SideEffectType.DATAFLOW_SIDE_EFFECTING


def _in_hbm(a):
    return pltpu.with_memory_space_constraint(a, pltpu.HBM)


def _gather_copies(bufs, send, recv, direct):
    x, y, c = _place()
    chips = _other_chips(x, y)

    def half(k, chip, which):
        hr = bufs[k].shape[1] // 2
        return bufs[k].at[chip, pl.ds(which * hr, hr), :]

    mine = lambda k: half(k, 2 * x + y, c)
    out, back = [], []
    for k in range(len(bufs)):
        for j, (cx, cy) in enumerate(chips):
            if direct:
                for s in (0, 1):
                    out.append(_remote(mine(k), mine(k), send.at[6 * k + 2 * j + s], recv.at[6 * k + 2 * j + c], (cx, cy, s)))
                    back.append(_remote(mine(k), half(k, 2 * cx + cy, s), send.at[6 * k + 2 * j + s],
                                        recv.at[6 * k + 2 * j + s], (cx, cy, s)))
            else:
                i = 3 * k + j
                out.append(_remote(mine(k), mine(k), send.at[i], recv.at[i], (cx, cy, c)))
                back.append(_remote(mine(k), half(k, 2 * cx + cy, c), send.at[i], recv.at[i], (cx, cy, c)))
    return out, back


def gather_start(fulls, direct):
    nk = len(fulls)
    nsem = (6 if direct else 3) * nk

    def body(*refs):
        send, recv = refs[nk], refs[nk + 1]
        bufs, token = refs[nk + 2:2 * nk + 2], refs[2 * nk + 2]
        for cp in _gather_copies(bufs, send, recv, direct)[0]:
            cp.start()
        token[...] = jnp.zeros_like(token)

    res = pl.pallas_call(
        body, name="gather_start",
        out_shape=(pltpu.SemaphoreType.DMA((nsem,)), pltpu.SemaphoreType.DMA((nsem,)),
                   *[pltpu.HBM(f.shape, f.dtype) for f in fulls], _sds((8, LANES), F32)),
        in_specs=[HBM] * nk, out_specs=(SEM, SEM, *[HBM] * nk, pl.BlockSpec(memory_space=pltpu.VMEM)),
        input_output_aliases={k: 2 + k for k in range(nk)},
        compiler_params=pltpu.CompilerParams(has_side_effects=EFFECT))(*[_in_hbm(f) for f in fulls])
    return res[0], res[1], list(res[2:2 + nk]), res[2 + nk]


def gather_wait(send, recv, fulls, after, direct):
    nk = len(fulls)

    def body(*refs):
        bufs, send_ref, recv_ref = refs[:nk], refs[nk], refs[nk + 1]
        for cp in _gather_copies(bufs, send_ref, recv_ref, direct)[1]:
            cp.wait_send()
            cp.wait_recv()

    return list(pl.pallas_call(
        body, name="gather_wait", out_shape=tuple(pltpu.HBM(f.shape, f.dtype) for f in fulls),
        in_specs=[HBM] * nk + [SEM, SEM, ANY], out_specs=tuple([HBM] * nk),
        input_output_aliases={k: k for k in range(nk)},
        compiler_params=pltpu.CompilerParams(has_side_effects=EFFECT))(*fulls, send, recv, after))


def pair_forward(fulls):
    nk = len(fulls)

    def body(*refs):
        bufs = refs[nk:2 * nk]
        stage = refs[2 * nk:3 * nk]
        load_sem, send_sem, recv_sem = refs[3 * nk:]
        x, y, c = _place()
        sib = (x, y, 1 - c)
        chips = [2 * cx + cy for cx, cy in _other_chips(x, y)]
        for k in range(nk):
            hr = bufs[k].shape[1] // 2
            at = [(chip, c * hr) for chip in chips]
            _push_tiles(bufs[k], bufs[k], stage[k], load_sem.at[k], send_sem.at[k], recv_sem.at[k], sib, at, at,
                        hr // _push_rows(hr))
        for k in range(nk):
            _await_tiles(bufs[k], send_sem.at[k], recv_sem.at[k], sib, 3, bufs[k].shape[1] // 2)

    return _hbm_call(body, name="pair_forward", n_in=nk, out_shape=[_sds(f.shape, f.dtype) for f in fulls], in_place=True,
                     scratch=_push_scratch(fulls, [f.shape[1] // 2 for f in fulls]))(*fulls)


N_PEERS = 7
PEER_FLIPS = [(fx, fy, fc) for fx in (0, 1) for fy in (0, 1) for fc in (0, 1)][1:]


def _chip_copies(ps, lands, send, recv):
    x, y, c = _place()
    cps = []
    for k in range(len(ps)):
        hr = ps[k].shape[1] // 2
        for r, (fx, fy, fc) in enumerate(PEER_FLIPS):
            px, py, pc = x ^ fx, y ^ fy, c ^ fc
            cps.append(_remote(ps[k].at[2 * px + py, pl.ds(pc * hr, hr), :], lands[k].at[r],
                               send.at[N_PEERS * k + r], recv.at[N_PEERS * k + r], (px, py, pc)))
    return cps


def chip_exchange_start(ps, after=None):
    nk = len(ps)
    lands = [lax.empty((N_PEERS, p.shape[1] // 2, p.shape[2]), p.dtype) for p in ps]
    extra, extra_specs = _behind(after)

    def body(*refs):
        refs = refs[2 * nk + len(extra):]
        send, recv = refs[0], refs[1]
        srcs, dsts, token = refs[2:nk + 2], refs[nk + 2:2 * nk + 2], refs[2 * nk + 2]
        for cp in _chip_copies(srcs, dsts, send, recv):
            cp.start()
        token[...] = jnp.zeros_like(token)

    res = pl.pallas_call(
        body, name="chip_exchange_start",
        out_shape=(pltpu.SemaphoreType.DMA((N_PEERS * nk,)), pltpu.SemaphoreType.DMA((N_PEERS * nk,)),
                   *[pltpu.HBM(a.shape, a.dtype) for a in ps + lands], _sds((8, LANES), F32)),
        in_specs=[HBM] * (2 * nk) + extra_specs,
        out_specs=(SEM, SEM, *[HBM] * (2 * nk), pl.BlockSpec(memory_space=pltpu.VMEM)),
        input_output_aliases={k: 2 + k for k in range(2 * nk)},
        compiler_params=pltpu.CompilerParams(has_side_effects=EFFECT))(*[_in_hbm(a) for a in ps + lands], *extra)
    return res[0], res[1], list(res[2:2 + nk]), list(res[2 + nk:2 + 2 * nk]), res[2 + 2 * nk]


def chip_exchange_wait(send, recv, ps, lands, after):
    nk = len(ps)

    def body(*refs):
        srcs, dsts, send_ref, recv_ref = refs[:nk], refs[nk:2 * nk], refs[2 * nk], refs[2 * nk + 1]
        for cp in _chip_copies(srcs, dsts, send_ref, recv_ref):
            cp.wait_send()
            cp.wait_recv()

    res = pl.pallas_call(
        body, name="chip_exchange_wait", out_shape=tuple(pltpu.HBM(a.shape, a.dtype) for a in ps + lands),
        in_specs=[HBM] * (2 * nk) + [SEM, SEM, ANY], out_specs=tuple([HBM] * (2 * nk)),
        input_output_aliases={k: k for k in range(2 * nk)},
        compiler_params=pltpu.CompilerParams(has_side_effects=EFFECT))(*ps, *lands, send, recv, after)
    return list(res[:nk]), list(res[nk:])


def rs_pair_share(fulls):
    nk = len(fulls)

    def body(*refs):
        bufs = refs[nk:2 * nk]
        stage = refs[2 * nk:3 * nk]
        load_sem, send_sem, recv_sem = refs[3 * nk:]
        x, y, c = _place()
        sib = (x, y, 1 - c)
        for k in range(nk):
            hr = bufs[k].shape[1]
            _push_tiles(bufs[k], bufs[k], stage[k], load_sem.at[k], send_sem.at[k], recv_sem.at[k], sib,
                        [(c, 0)], [(c, 0)], hr // _push_rows(hr))
        for k in range(nk):
            _await_tiles(bufs[k], send_sem.at[k], recv_sem.at[k], sib, 1, bufs[k].shape[1])

    return _hbm_call(body, name="rs_pair_share", n_in=nk, out_shape=[_sds(f.shape, f.dtype) for f in fulls], in_place=True,
                     scratch=_push_scratch(fulls, [f.shape[1] for f in fulls]))(*fulls)


N_DEV = 8


def allreduce_small(buf, after=None):
    rows = buf.shape[0]
    rp = rows // N_DEV
    extra, extra_specs = _behind(after)

    def body(in_ref, *rest):
        out_ref, land_ref, send, recv = rest[-4:]
        x, y, c = _place()
        mine = pl.ds(pl.multiple_of((4 * x + 2 * y + c) * rp, 8), rp)
        peers = [(x ^ fx, y ^ fy, c ^ fc) for fx in (0, 1) for fy in (0, 1) for fc in (0, 1)][1:]
        block = [pl.ds(pl.multiple_of((4 * px + 2 * py + pc) * rp, 8), rp) for px, py, pc in peers]
        scatter = [_remote(in_ref.at[block[r], :], land_ref.at[r], send.at[r], recv.at[r], peers[r]) for r in range(7)]
        for cp in scatter:
            cp.start()
        for cp in scatter:
            cp.wait()
        acc = in_ref[mine, :]
        for r in range(7):
            acc = acc + land_ref[r]
        out_ref[mine, :] = acc
        spread = [_remote(out_ref.at[mine, :], out_ref.at[mine, :], send.at[7 + r], recv.at[7 + r], peers[r]) for r in range(7)]
        for cp in spread:
            cp.start()
        for r in range(7):
            _remote(out_ref.at[block[r], :], out_ref.at[block[r], :], send.at[7 + r], recv.at[7 + r], peers[r]).wait_recv()
        for cp in spread:
            cp.wait_send()

    vm = pl.BlockSpec(memory_space=pltpu.VMEM)
    return pl.pallas_call(body, name="allreduce_small", out_shape=_sds(buf.shape, F32), in_specs=[vm] + extra_specs,
                          out_specs=vm,
                          scratch_shapes=[pltpu.VMEM((7, rp, LANES), F32), pltpu.SemaphoreType.DMA((14,)),
                                          pltpu.SemaphoreType.DMA((14,))])(buf, *extra)


def _row_tile(rows, cap=512):
    return max(t for t in range(16, cap + 1, 16) if rows % t == 0)


def add_slots(g, got, pos):
    _, r, cols = g.shape
    hr = r // 2
    tr = _row_tile(hr, 256)
    nt = hr // tr

    def body(pos_ref, o_ref, *rest):
        del pos_ref
        acc = o_ref[...].astype(F32)
        for ref in rest[:N_PEERS]:
            acc = acc + ref[...].astype(F32)
        rest[N_PEERS][...] = acc.astype(BF16)

    slots = [pl.BlockSpec((None, tr, cols), functools.partial(lambda j, i, pos: (j, i, 0), j)) for j in range(N_PEERS)]
    return _call(body, name="add_slots", out_shape=_sds((2, hr, cols), BF16), grid=(nt,),
                 in_specs=[pl.BlockSpec((None, tr, cols), lambda i, pos: (pos[1], pos[0] * nt + i, 0))] + slots,
                 out_specs=pl.BlockSpec((None, tr, cols), lambda i, pos: (pos[0], i, 0)),
                 sem=("parallel",), prefetch=1)(pos, g, *[got] * N_PEERS)


def _adamw_math(w, g, m, v):
    nm = B1 * m + (1.0 - B1) * g
    nv = B2 * v + (1.0 - B2) * (g * g)
    m_hat = nm / (1.0 - B1 ** STEP)
    v_hat = nv / (1.0 - B2 ** STEP)
    return -LR * (m_hat / (jnp.sqrt(v_hat) + AEPS) + WD * w), nm, nv


def adamw_layer(w, g, m, v, l, prev):
    nl, r, cols = w.shape
    tr = _row_tile(r, 256)

    def body(w_ref, g_ref, m_ref, v_ref, *rest):
        go_ref, d_ref, nm_ref, nv_ref = rest[-4:]
        gv = g_ref[...].astype(F32)
        d, nm, nv = _adamw_math(w_ref[...], gv, m_ref[...], v_ref[...])
        go_ref[...] = gv
        d_ref[...] = d
        nm_ref[...] = nm
        nv_ref[...] = nv

    lay = pl.BlockSpec((None, tr, cols), lambda i: (l, i, 0))
    o = _sds((nl, r, cols), F32)
    extra = [] if prev is None else list(prev)
    return _call(body, name="adamw_layer", out_shape=(o, o, o, o), grid=(r // tr,),
                 in_specs=[lay, pl.BlockSpec((tr, cols), lambda i: (i, 0)), lay, lay] + [ANY] * len(extra),
                 out_specs=(lay, lay, lay, lay), sem=("parallel",),
                 aliases={4 + j: j for j in range(len(extra))})(w, g, m, v, *extra)


def adamw(w, g, m, v):
    shape = w.shape
    cols = shape[-1]
    rows = w.size // cols
    tr = _row_tile(rows, 256)

    def body(w_ref, g_ref, m_ref, v_ref, d_ref, nm_ref, nv_ref):
        d_ref[...], nm_ref[...], nv_ref[...] = _adamw_math(w_ref[...], g_ref[...], m_ref[...], v_ref[...])

    blk = pl.BlockSpec((tr, cols), lambda i: (i, 0))
    o = _sds((rows, cols), F32)
    outs = _call(body, name="adamw", out_shape=(o, o, o), grid=(rows // tr,), in_specs=[blk] * 4, out_specs=(blk, blk, blk),
                 sem=("parallel",))(*[t.reshape(rows, cols) for t in (w, g, m, v)])
    return [t.reshape(shape) for t in outs]


SMALL = ("norm1_g", "gmlp_ln_g", "w_spatial", "b_spatial", "mix_norm_attn_g", "mix_norm_gmlp_g", "norm2_g", "final_g")
WEIGHTS = ("norm1_g", "w_in", "gmlp_ln_g", "w_spatial", "b_spatial", "mix_norm_attn_g", "mix_norm_gmlp_g", "w_out",
           "norm2_g", "w_gate", "w_up", "w_down", "final_g")
BIG = dict(win="w_in", wout="w_out", wg="w_gate", wu="w_up", wd="w_down")


def _pack(parts):
    flat = jnp.concatenate([parts[n].reshape(-1) for n in SMALL])
    rows = -(-flat.shape[0] // (LANES * 8 * N_DEV)) * 8 * N_DEV
    return jnp.pad(flat, (0, rows * LANES - flat.shape[0])).reshape(rows, LANES)


def _unpack(buf, like):
    flat = buf.reshape(-1)
    out, at = {}, 0
    for n in SMALL:
        out[n] = flat[at:at + like[n].size].reshape(like[n].shape)
        at += like[n].size
    return out


def kernel(x, norm1_g, w_in, gmlp_ln_g, w_spatial, b_spatial, mix_norm_attn_g, mix_norm_gmlp_g, w_out, norm2_g, w_gate, w_up, w_down, final_g, loss_target, m_norm1_g, m_w_in, m_gmlp_ln_g, m_w_spatial, m_b_spatial, m_mix_norm_attn_g, m_mix_norm_gmlp_g, m_w_out, m_norm2_g, m_w_gate, m_w_up, m_w_down, m_final_g, v_norm1_g, v_w_in, v_gmlp_ln_g, v_w_spatial, v_b_spatial, v_mix_norm_attn_g, v_mix_norm_gmlp_g, v_w_out, v_norm2_g, v_w_gate, v_w_up, v_w_down, v_final_g):
    w = dict(norm1_g=norm1_g, w_in=w_in, gmlp_ln_g=gmlp_ln_g, w_spatial=w_spatial, b_spatial=b_spatial,
             mix_norm_attn_g=mix_norm_attn_g, mix_norm_gmlp_g=mix_norm_gmlp_g, w_out=w_out, norm2_g=norm2_g,
             w_gate=w_gate, w_up=w_up, w_down=w_down, final_g=final_g)
    m = dict(norm1_g=m_norm1_g, w_in=m_w_in, gmlp_ln_g=m_gmlp_ln_g, w_spatial=m_w_spatial, b_spatial=m_b_spatial,
             mix_norm_attn_g=m_mix_norm_attn_g, mix_norm_gmlp_g=m_mix_norm_gmlp_g, w_out=m_w_out, norm2_g=m_norm2_g,
             w_gate=m_w_gate, w_up=m_w_up, w_down=m_w_down, final_g=m_final_g)
    v = dict(norm1_g=v_norm1_g, w_in=v_w_in, gmlp_ln_g=v_gmlp_ln_g, w_spatial=v_w_spatial, b_spatial=v_b_spatial,
             mix_norm_attn_g=v_mix_norm_attn_g, mix_norm_gmlp_g=v_mix_norm_gmlp_g, w_out=v_w_out, norm2_g=v_norm2_g,
             w_gate=v_w_gate, w_up=v_w_up, w_down=v_w_down, final_g=v_final_g)

    pos = place_vector()
    small = {n: w[n] for n in SMALL}
    tables = step_tables()

    def start_gathers(l, after):
        flights = []
        for kinds in GROUPS:
            direct = bool(l) and kinds is GROUPS[0]
            flights.append(gather_start([cast_into(w[BIG[k]], l, pos, after) for k in kinds], direct) + (direct,))
            after = flights[-1][3]
        return flights, after

    def arrive(flight, kinds, after):
        fulls = gather_wait(flight[0], flight[1], flight[2], after, flight[4])
        if not flight[4]:
            fulls = pair_forward(fulls)
        big = dict(zip(kinds, fulls))
        if "wout" in big:
            big["wout"] = big["wout"].reshape(D, D)
        if "wd" in big:
            big["wd"] = big["wd"].reshape(FF, D)
        return big

    act = x[0]
    flights, token = start_gathers(0, None)
    big, saved = [], []
    for l in range(NL):
        mine, after_mixer = flights, token
        if l + 1 < NL:
            flights, token = start_gathers(l + 1, act if l else token)
            after_mixer = token
        wl = arrive(mine[0], GROUPS[0], act if l else mine[0][3])
        x_mid, sv = mixer_fwd(act, small, l, wl, tables, after_mixer)
        wl.update(arrive(mine[1], GROUPS[1], x_mid))
        act, sv2 = ffn_fwd(x_mid, small, l, wl)
        big.append(wl)
        saved.append({**sv, **sv2})
    loss8, dx, dxb, dfinal = loss_head(act, _row(small["final_g"]), loss_target[0])
    loss = lax.psum(loss8[0, 0], ("x", "y", "c"))

    grads, delta, new_m, new_v = {}, {}, {}, {}
    stacked = {k: None for k in KINDS}
    gs = {k: [None] * NL for k in SMALL[:-1]}

    def send_off(gbig, kinds, l, after=None):
        return chip_exchange_start([gbig[k] for k in kinds], after) + (kinds, l)

    def finish(flight, after):
        partial_grads, lands = chip_exchange_wait(*flight[:4], after)
        halves = [add_slots(g, got, pos) for g, got in zip(partial_grads, lands)]
        for k, full in zip(flight[5], rs_pair_share(halves)):
            n = BIG[k]
            stacked[k] = adamw_layer(w[n], full.reshape(w[n].shape[1:]), m[n], v[n], flight[6], stacked[k])
        return stacked[flight[5][-1]][1]

    pending, token = [], None
    for l in reversed(range(NL)):
        dx, dxb, gs1, gbig = ffn_bwd(dx, dxb, small, l, big[l], saved[l], token)
        ffn_flight = send_off(gbig, GROUPS[1], l)
        for flight in pending:
            finish(flight, ffn_flight[4])
        dx, dxb, gs2, gbig = mixer_bwd(dx, dxb, small, l, big[l], saved[l], tables, ffn_flight[4])
        for k, g in {**gs1, **gs2}.items():
            gs[k][l] = g
        if l:
            mixer_flight = send_off(gbig, GROUPS[0], l)
            pending, token = [ffn_flight, mixer_flight], mixer_flight[4]
    part = {n: jnp.stack(gs[n]).reshape(w[n].shape) for n in SMALL[:-1]}
    part["final_g"] = dfinal.reshape(w["final_g"].shape)
    small_sum = allreduce_small(_pack(part), [gbig[k] for k in GROUPS[0]])
    mixer_flight = send_off(gbig, GROUPS[0], 0, small_sum)
    token = mixer_flight[4]
    for flight in (ffn_flight, mixer_flight):
        token = finish(flight, token)
    for k in KINDS:
        grads[BIG[k]], delta[BIG[k]], new_m[BIG[k]], new_v[BIG[k]] = stacked[k]
    grads.update(_unpack(small_sum, part))

    sm = adamw(_pack({n: w[n] for n in SMALL}), _pack({n: grads[n] for n in SMALL}), _pack({n: m[n] for n in SMALL}),
               _pack({n: v[n] for n in SMALL}))
    for res, packed in zip((delta, new_m, new_v), sm):
        res.update(_unpack(packed, part))

    return (loss, dx.reshape(x.shape), *[grads[n] for n in WEIGHTS], *[delta[n] for n in WEIGHTS],
            *[new_m[n] for n in WEIGHTS], *[new_v[n] for n in WEIGHTS])
```

```python
import functools

import jax
import jax.numpy as jnp
from jax import lax
from jax.experimental import pallas as pl
from jax.experimental.pallas import tpu as pltpu

F32 = jnp.float32
BF16 = jnp.bfloat16

D = 2048
T = 4096
NL = 4
HD = 128
DA = D // 2
DG = D - DA
NH = DA // HD
NG = DG // 128
CH = 128
DIN = 3 * DA + 2 * DG
FF = 5632
DILATIONS = (1, 4, 16)
NSIDE = 64
ROPE_THETA = 10000.0
EPS = 1e-6
NEG = -1e30
SCALE = HD ** -0.5
NCHIP = 4

LR, B1, B2, AEPS, WD, STEP = 0.001, 0.9, 0.999, 1e-08, 0.01, 10

LANES = 128
VMEM_LIMIT = 56 * 1024 * 1024
MESH = pl.DeviceIdType.MESH


def _call(body, *, name, out_shape, grid=(), in_specs=None, out_specs=None, scratch=(), sem=None, aliases=None,
          prefetch=0):
    params = dict(vmem_limit_bytes=VMEM_LIMIT)
    if sem is not None:
        params["dimension_semantics"] = sem
    if prefetch:
        spec = pltpu.PrefetchScalarGridSpec(num_scalar_prefetch=prefetch, grid=grid, in_specs=in_specs,
                                            out_specs=out_specs, scratch_shapes=list(scratch))
        return pl.pallas_call(body, name=name, out_shape=out_shape, grid_spec=spec,
                              input_output_aliases=aliases or {}, compiler_params=pltpu.CompilerParams(**params))
    kw = {}
    if grid:
        kw["grid"] = grid
    return pl.pallas_call(
        body, name=name, out_shape=out_shape, in_specs=in_specs, out_specs=out_specs,
        scratch_shapes=list(scratch), input_output_aliases=aliases or {},
        compiler_params=pltpu.CompilerParams(**params), **kw)


def _sds(shape, dtype):
    return jax.ShapeDtypeStruct(tuple(shape), dtype)


def _dot(a, b):
    return jnp.dot(a, b, preferred_element_type=F32)


def _dot_nt(a, b):
    return lax.dot_general(a, b, (((1,), (1,)), ((), ())), preferred_element_type=F32)


def _dot_tn(a, b):
    return lax.dot_general(a, b, (((0,), (0,)), ((), ())), preferred_element_type=F32)


def _sigmoid(x):
    return 0.5 * jnp.tanh(0.5 * x) + 0.5


def _behind(after):
    arrays = [] if after is None else list(after) if isinstance(after, (list, tuple)) else [after]
    return arrays, [pl.BlockSpec(memory_space=pl.ANY)] * len(arrays)


def rms_fwd(x, g, after=None):
    n, d = x.shape
    tm = 256
    extra, extra_specs = _behind(after)

    def body(x_ref, g_ref, *rest):
        h_ref = rest[-1]
        xv = x_ref[...]
        r = lax.rsqrt(jnp.mean(xv * xv, axis=-1, keepdims=True) + EPS)
        h_ref[...] = (xv * r * g_ref[...]).astype(BF16)

    return _call(body, name="rms_fwd", out_shape=_sds((n, d), BF16), grid=(n // tm,),
                 in_specs=[pl.BlockSpec((tm, d), lambda i: (i, 0)), pl.BlockSpec((1, d), lambda i: (0, 0))] + extra_specs,
                 out_specs=pl.BlockSpec((tm, d), lambda i: (i, 0)), sem=("parallel",))(x, g, *extra)


def rms_bwd(dh, x, g, dres):
    n, d = x.shape
    tm = 256

    def body(dh_ref, x_ref, g_ref, dres_ref, dx_ref, dxb_ref, dg_ref):
        @pl.when(pl.program_id(0) == 0)
        def _():
            dg_ref[...] = jnp.zeros_like(dg_ref)

        xv = x_ref[...]
        r = lax.rsqrt(jnp.mean(xv * xv, axis=-1, keepdims=True) + EPS)
        xhat = xv * r
        dhv = dh_ref[...].astype(F32)
        dg_ref[...] += jnp.sum(dhv * xhat, axis=0, keepdims=True)
        dxn = dhv * g_ref[...]
        dx = dres_ref[...] + r * (dxn - xhat * jnp.mean(dxn * xhat, axis=-1, keepdims=True))
        dx_ref[...] = dx
        dxb_ref[...] = dx.astype(BF16)

    row = pl.BlockSpec((tm, d), lambda i: (i, 0))
    vec = pl.BlockSpec((1, d), lambda i: (0, 0))
    return _call(body, name="rms_bwd", out_shape=(_sds((n, d), F32), _sds((n, d), BF16), _sds((1, d), F32)),
                 grid=(n // tm,), in_specs=[row, row, vec, row], out_specs=(row, row, vec), sem=("arbitrary",))(dh, x, g, dres)


def loss_head(x, g, target):
    n, d = x.shape
    tm = 256

    def body(x_ref, g_ref, t_ref, loss_ref, dx_ref, dxb_ref, dg_ref):
        @pl.when(pl.program_id(0) == 0)
        def _():
            dg_ref[...] = jnp.zeros_like(dg_ref)
            loss_ref[...] = jnp.zeros_like(loss_ref)

        xv = x_ref[...]
        r = lax.rsqrt(jnp.mean(xv * xv, axis=-1, keepdims=True) + EPS)
        xhat = xv * r
        gv = g_ref[...]
        err = xhat * gv - t_ref[...]
        loss_ref[...] += 0.5 * jnp.sum(jnp.mean(err * err, axis=-1, keepdims=True))
        dy = err * (1.0 / d)
        dg_ref[...] += jnp.sum(dy * xhat, axis=0, keepdims=True)
        dxn = dy * gv
        dx = r * (dxn - xhat * jnp.mean(dxn * xhat, axis=-1, keepdims=True))
        dx_ref[...] = dx
        dxb_ref[...] = dx.astype(BF16)

    row = pl.BlockSpec((tm, d), lambda i: (i, 0))
    vec = pl.BlockSpec((1, d), lambda i: (0, 0))
    return _call(body, name="loss_head",
                 out_shape=(_sds((8, LANES), F32), _sds((n, d), F32), _sds((n, d), BF16), _sds((1, d), F32)),
                 grid=(n // tm,), in_specs=[row, vec, row],
                 out_specs=(pl.BlockSpec((8, LANES), lambda i: (0, 0)), row, row, vec), sem=("arbitrary",))(x, g, target)


def mm_cols(a, wc, tm=512):
    n, k = a.shape
    s, _, nq = wc.shape

    def body(a_ref, w_ref, o_ref):
        o_ref[...] = _dot(a_ref[...], w_ref[...]).astype(BF16)

    return _call(body, name="mm_cols", out_shape=_sds((n, s * nq), BF16), grid=(s, n // tm),
                 in_specs=[pl.BlockSpec((tm, k), lambda j, i: (i, 0)), pl.BlockSpec((None, k, nq), lambda j, i: (j, 0, 0))],
                 out_specs=pl.BlockSpec((tm, nq), lambda j, i: (i, j)), sem=("parallel", "parallel"))(a, wc)


def mm_gateup(h, wg, wu, tm=512):
    n, k = h.shape
    s, _, nq = wg.shape

    def body(h_ref, wg_ref, wu_ref, gate_ref, up_ref, ff_ref):
        hv = h_ref[...]
        gate = _dot(hv, wg_ref[...])
        up = _dot(hv, wu_ref[...])
        gate_ref[...] = gate.astype(BF16)
        up_ref[...] = up.astype(BF16)
        ff_ref[...] = (gate * _sigmoid(gate) * up).astype(BF16)

    wspec = pl.BlockSpec((None, k, nq), lambda j, i: (j, 0, 0))
    ospec = pl.BlockSpec((tm, nq), lambda j, i: (i, j))
    o = _sds((n, s * nq), BF16)
    return _call(body, name="mm_gateup", out_shape=(o, o, o), grid=(s, n // tm),
                 in_specs=[pl.BlockSpec((tm, k), lambda j, i: (i, 0)), wspec, wspec],
                 out_specs=(ospec, ospec, ospec), sem=("parallel", "parallel"))(h, wg, wu)


def mm_rows_res(a, wr, res, tm=512, tn=1024):
    n, k = a.shape
    _, nn = wr.shape

    def body(a_ref, w_ref, r_ref, o_ref):
        o_ref[...] = r_ref[...] + _dot(a_ref[...], w_ref[...])

    return _call(body, name="mm_rows_res", out_shape=_sds((n, nn), F32), grid=(nn // tn, n // tm),
                 in_specs=[pl.BlockSpec((tm, k), lambda j, i: (i, 0)), pl.BlockSpec((k, tn), lambda j, i: (0, j)),
                           pl.BlockSpec((tm, tn), lambda j, i: (i, j))],
                 out_specs=pl.BlockSpec((tm, tn), lambda j, i: (i, j)), sem=("parallel", "parallel"))(a, wr, res)


def mm_nt_rows(dy, wr, tk, tm=512, after=None):
    n, nn = dy.shape
    k, _ = wr.shape
    extra, extra_specs = _behind(after)

    def body(dy_ref, w_ref, *rest):
        rest[-1][...] = _dot_nt(dy_ref[...], w_ref[...]).astype(BF16)

    return _call(body, name="mm_nt_rows", out_shape=_sds((n, k), BF16), grid=(k // tk, n // tm),
                 in_specs=[pl.BlockSpec((tm, nn), lambda j, i: (i, 0)), pl.BlockSpec((tk, nn), lambda j, i: (j, 0))]
                 + extra_specs,
                 out_specs=pl.BlockSpec((tm, tk), lambda j, i: (i, j)), sem=("parallel", "parallel"))(dy, wr, *extra)


def mm_dff(dy, wd, gate, up, tk, tm=512, after=None):
    n, nn = dy.shape
    k, _ = wd.shape
    extra, extra_specs = _behind(after)

    rows = 32

    def body(dy_ref, w_ref, gate_ref, up_ref, *rest):
        dgate_ref, dup_ref, dff_ref = rest[-3:]
        dff_ref[...] = _dot_nt(dy_ref[...], w_ref[...])

        def tail(r, carry):
            rs = pl.ds(pl.multiple_of(r * rows, rows), rows)
            dff = dff_ref[rs, :]
            gate = gate_ref[rs, :].astype(F32)
            up = up_ref[rs, :].astype(F32)
            sig = _sigmoid(gate)
            dgate_ref[rs, :] = (dff * up * (sig * (1.0 + gate * (1.0 - sig)))).astype(BF16)
            dup_ref[rs, :] = (dff * (gate * sig)).astype(BF16)
            return carry

        lax.fori_loop(0, tm // rows, tail, 0)

    tile = pl.BlockSpec((tm, tk), lambda j, i: (i, j))
    o = _sds((n, k), BF16)
    return _call(body, name="mm_dff", out_shape=(o, o), grid=(k // tk, n // tm),
                 in_specs=[pl.BlockSpec((tm, nn), lambda j, i: (i, 0)), pl.BlockSpec((tk, nn), lambda j, i: (j, 0)), tile, tile]
                 + extra_specs,
                 out_specs=(tile, tile), scratch=[pltpu.VMEM((tm, tk), F32)],
                 sem=("parallel", "parallel"))(dy, wd, gate, up, *extra)


def mm_nt_cols(das, wcs, tm=512):
    n = das[0].shape[0]
    s, k, nq = wcs[0].shape
    npair = len(das)

    def body(*refs):
        da_refs, w_refs, o_ref, acc = refs[:npair], refs[npair:2 * npair], refs[2 * npair], refs[2 * npair + 1]
        ss = pl.program_id(1)

        @pl.when(ss == 0)
        def _():
            acc[...] = jnp.zeros_like(acc)

        for da_ref, w_ref in zip(da_refs, w_refs):
            acc[...] += _dot_nt(da_ref[...], w_ref[...])

        @pl.when(ss == s - 1)
        def _():
            o_ref[...] = acc[...].astype(BF16)

    return _call(body, name="mm_nt_cols%d" % npair, out_shape=_sds((n, k), BF16), grid=(n // tm, s),
                 in_specs=[pl.BlockSpec((tm, nq), lambda i, ss: (i, ss))] * npair
                 + [pl.BlockSpec((None, k, nq), lambda i, ss: (ss, 0, 0))] * npair,
                 out_specs=pl.BlockSpec((tm, k), lambda i, ss: (i, 0)),
                 scratch=[pltpu.VMEM((tm, k), F32)], sem=("parallel", "arbitrary"))(*das, *wcs)


def mm_tn(a, b, nq, shard_major, tka=512):
    n, ka = a.shape
    nb = b.shape[1] // nq

    def body(a_ref, b_ref, o_ref):
        o_ref[...] = _dot_tn(a_ref[...], b_ref[...]).astype(BF16)

    if shard_major:
        out_shape, out_spec = _sds((nb, ka, nq), BF16), pl.BlockSpec((None, tka, nq), lambda j, i: (j, i, 0))
    else:
        out_shape, out_spec = _sds((ka, nb * nq), BF16), pl.BlockSpec((tka, nq), lambda j, i: (i, j))
    return _call(body, name="mm_tn", out_shape=out_shape, grid=(nb, ka // tka),
                 in_specs=[pl.BlockSpec((n, tka), lambda j, i: (0, i)), pl.BlockSpec((n, nq), lambda j, i: (0, j))],
                 out_specs=out_spec, sem=("parallel", "parallel"))(a, b)


def rope_tables(n):
    pos = jnp.arange(n, dtype=F32)
    inv = ROPE_THETA ** (-jnp.arange(0, HD, 2, dtype=F32) / HD)
    ang = pos[:, None] * inv[None, :]
    cos, sin = jnp.cos(ang), jnp.sin(ang)
    return jnp.concatenate([cos, cos], axis=-1), jnp.concatenate([-sin, sin], axis=-1)


PERM_ROWS = 256


def perm_matrix(d):
    n = PERM_ROWS // d
    i = jnp.arange(PERM_ROWS)
    src = (i % n) * d + i // n
    return (src[:, None] == jnp.arange(PERM_ROWS)[None, :]).astype(BF16)


def _perm_f32(p, x):
    hi = x.astype(BF16)
    rest = x - hi.astype(F32)
    mid = rest.astype(BF16)
    lo = (rest - mid.astype(F32)).astype(BF16)
    return (_dot(p, hi) + _dot(p, mid)) + _dot(p, lo)


def _res_spec(d, cols, col_block=0):
    return pl.BlockSpec((d, PERM_ROWS // d, cols), lambda i: (0, i, col_block))


def _perm_specs():
    return [pl.BlockSpec((PERM_ROWS, PERM_ROWS), lambda i: (0, 0))] * (len(DILATIONS) - 1)


def qkv_split(proj, cos2, sin2, perms):
    n = proj.shape[0]
    tm = PERM_ROWS

    def body(p_ref, c_ref, s_ref, *rest):
        perm_refs, outs = rest[:len(DILATIONS) - 1], rest[len(DILATIONS) - 1:]
        c, s = c_ref[...], s_ref[...]
        for h in range(2 * NH):
            t = p_ref[:, h * HD:(h + 1) * HD].astype(F32)
            outs[0][0, :, h * HD:(h + 1) * HD] = (t * c + pltpu.roll(t, HD // 2, 1) * s).astype(BF16)
        outs[0][0, :, 2 * DA:] = p_ref[:, 2 * DA:]
        nat = outs[0][0]
        for o_ref, p_ref_d, d in zip(outs[1:], perm_refs, DILATIONS[1:]):
            o_ref[...] = _dot(p_ref_d[...], nat).astype(BF16).reshape(d, tm // d, 3 * DA)

    tab = pl.BlockSpec((tm, HD), lambda i: (i, 0))
    return _call(body, name="qkv_split", out_shape=tuple(_sds((d, n // d, 3 * DA), BF16) for d in DILATIONS),
                 grid=(n // tm,), in_specs=[pl.BlockSpec((tm, 3 * DA), lambda i: (i, 0)), tab, tab] + _perm_specs(),
                 out_specs=tuple(_res_spec(d, 3 * DA) for d in DILATIONS), sem=("parallel",))(proj, cos2, sin2, *perms)


def dproj_assemble(dqs, dks, dvs, duv, cos2, sin2, perms_t):
    n = duv.shape[0]
    tm = PERM_ROWS
    npat = len(DILATIONS)

    def body(*refs):
        parts = refs[0:npat], refs[npat:2 * npat], refs[2 * npat:3 * npat]
        duv_ref, c_ref, s_ref = refs[3 * npat:3 * npat + 3]
        pt_refs, o_ref = refs[3 * npat + 3:-1], refs[-1]
        c, s = c_ref[...], s_ref[...]
        for part, part_refs in enumerate(parts):
            t = part_refs[0][0].astype(F32)
            for ref, pt_ref in zip(part_refs[1:], pt_refs):
                t = t + _dot(pt_ref[...], ref[...].reshape(tm, DA))
            if part == 2:
                o_ref[:, 2 * DA:3 * DA] = t.astype(BF16)
                continue
            for h in range(NH):
                th = t[:, h * HD:(h + 1) * HD]
                o_ref[:, part * DA + h * HD:part * DA + (h + 1) * HD] = (th * c - pltpu.roll(th, HD // 2, 1) * s).astype(BF16)
        o_ref[:, 3 * DA:] = duv_ref[...]

    blks = [_res_spec(d, DA) for d in DILATIONS]
    tab = pl.BlockSpec((tm, HD), lambda i: (i, 0))
    return _call(body, name="dproj_assemble", out_shape=_sds((n, DIN), BF16), grid=(n // tm,),
                 in_specs=blks * 3 + [pl.BlockSpec((tm, 2 * DG), lambda i: (i, 0)), tab, tab] + _perm_specs(),
                 out_specs=pl.BlockSpec((tm, DIN), lambda i: (i, 0)), sem=("parallel",))(
        *dqs, *dks, *dvs, duv, cos2, sin2, *perms_t)


ATTN_BLOCK_BYTES = 2 << 20


def _attn_tiles(ls):
    tq = min(128, ls)
    w = min(tq + 2 * NSIDE, ls)
    return tq, w, ls // tq


def _attn_heads(ls):
    return NH if ls * DA * 2 <= ATTN_BLOCK_BYTES else 1


ATTN_UNROLL = 8


def _attn_loop(nt, hb, tile):
    for h in range(hb):
        def trip(t, carry, h=h):
            tile(t, h)
            return carry

        lax.fori_loop(0, nt, trip, 0, unroll=min(nt, ATTN_UNROLL))


def _band(t, tq, w, ls):
    q0 = pl.multiple_of(t * tq, tq)
    ks = pl.multiple_of(jnp.clip(t * tq - NSIDE, 0, ls - w), NSIDE)
    qpos = q0 + lax.broadcasted_iota(jnp.int32, (tq, w), 0)
    kpos = ks + lax.broadcasted_iota(jnp.int32, (tq, w), 1)
    return q0, ks, jnp.abs(kpos - qpos) <= NSIDE


def _attn_specs(ls, hb):
    nhb = NH // hb
    q = pl.BlockSpec((None, ls, hb * HD), lambda g: (g // nhb, 0, g % nhb))
    k = pl.BlockSpec((None, ls, hb * HD), lambda g: (g // nhb, 0, nhb + g % nhb))
    v = pl.BlockSpec((None, ls, hb * HD), lambda g: (g // nhb, 0, 2 * nhb + g % nhb))
    return q, k, v, q


def _lse_spec(ls, hb):
    nhb = NH // hb
    return pl.BlockSpec((None, ls, LANES), lambda g: (g // nhb, 0, 0))


def attn_fwd(qkv):
    d, ls, _ = qkv.shape
    tq, w, nt = _attn_tiles(ls)
    hb = _attn_heads(ls)
    nhb = NH // hb

    def body(q_ref, k_ref, v_ref, o_ref, l_ref):
        first = (pl.program_id(0) % nhb) * hb
        lane = lax.broadcasted_iota(jnp.int32, (tq, LANES), 1)

        @pl.when(first == 0)
        def _():
            l_ref[...] = jnp.zeros_like(l_ref)

        def tile(t, h):
            q0, ks, valid = _band(t, tq, w, ls)
            cols = slice(h * HD, (h + 1) * HD)
            s = _dot_nt(q_ref[pl.ds(q0, tq), cols], k_ref[pl.ds(ks, w), cols]) * SCALE
            s = jnp.where(valid, s, NEG)
            m = jnp.max(s, axis=1, keepdims=True)
            p = jnp.exp(s - m)
            l = jnp.sum(p, axis=1, keepdims=True)
            o = _dot(p.astype(BF16), v_ref[pl.ds(ks, w), cols]) / l
            o_ref[pl.ds(q0, tq), cols] = o.astype(BF16)
            l_ref[pl.ds(q0, tq), :] = jnp.where(lane == first + h, m + jnp.log(l), l_ref[pl.ds(q0, tq), :])

        _attn_loop(nt, hb, tile)

    q, k, v, o = _attn_specs(ls, hb)
    return _call(body, name="attn_fwd_d%d" % d, out_shape=(_sds((d, ls, DA), BF16), _sds((d, ls, LANES), F32)),
                 grid=(d * NH // hb,), in_specs=[q, k, v], out_specs=(o, _lse_spec(ls, hb)), sem=("arbitrary",))(
        qkv, qkv, qkv)


def attn_merge(os_, ls_, ga, perms_t):
    n = os_[0].shape[1]
    tm = PERM_ROWS
    npat = len(DILATIONS)

    def body(*refs):
        o_refs, l_refs = refs[:npat], refs[npat:2 * npat]
        g_ref, pt_refs = refs[2 * npat], refs[2 * npat + 1:3 * npat]
        a_ref, lse_ref, mix_ref = refs[3 * npat:]
        ov = [o_refs[0][0].astype(F32)] + [_dot(pt[...], o[...].reshape(tm, DA)) for o, pt in zip(o_refs[1:], pt_refs)]
        lv = [l_refs[0][0]] + [_perm_f32(pt[...], l[...].reshape(tm, LANES)) for l, pt in zip(l_refs[1:], pt_refs)]
        lmax = functools.reduce(jnp.maximum, lv)
        e = [jnp.exp(x - lmax) for x in lv]
        den = functools.reduce(lambda p, q: p + q, e)
        wts = [ei / den for ei in e]
        lse_ref[...] = lmax + jnp.log(den)
        heads = []
        for h in range(NH):
            cols = slice(h * HD, (h + 1) * HD)
            heads.append(functools.reduce(lambda p, q: p + q, [wt[:, h:h + 1] * oi[:, cols] for wt, oi in zip(wts, ov)]))
        a = jnp.concatenate(heads, axis=1)
        a_ref[...] = a.astype(BF16)
        r = lax.rsqrt(jnp.mean(a * a, axis=-1, keepdims=True) + EPS)
        mix_ref[...] = (a * r * g_ref[...]).astype(BF16)

    blk = pl.BlockSpec((tm, DA), lambda i: (i, 0))
    return _call(body, name="attn_merge", out_shape=(_sds((n, DA), BF16), _sds((n, LANES), F32), _sds((n, D), BF16)),
                 grid=(n // tm,),
                 in_specs=[_res_spec(d, DA) for d in DILATIONS] + [_res_spec(d, LANES) for d in DILATIONS]
                 + [pl.BlockSpec((1, DA), lambda i: (0, 0))] + _perm_specs(),
                 out_specs=(blk, pl.BlockSpec((tm, LANES), lambda i: (i, 0)), blk), sem=("parallel",))(
        *os_, *ls_, ga, *perms_t)


def attn_norm_bwd(dmix, a, lse, ga, perms):
    n = a.shape[0]
    tm = PERM_ROWS
    npat = len(DILATIONS)

    def body(dm_ref, a_ref, lse_ref, g_ref, *rest):
        p_refs = rest[:npat - 1]
        da_refs, a_refs, lse_refs = rest[npat - 1:2 * npat - 1], rest[2 * npat - 1:3 * npat - 2], rest[3 * npat - 2:4 * npat - 3]
        dg_ref = rest[-1]

        @pl.when(pl.program_id(0) == 0)
        def _():
            dg_ref[...] = jnp.zeros_like(dg_ref)

        ab = a_ref[...]
        av = ab.astype(F32)
        r = lax.rsqrt(jnp.mean(av * av, axis=-1, keepdims=True) + EPS)
        ahat = av * r
        dm = dm_ref[...].astype(F32)
        dg_ref[...] += jnp.sum(dm * ahat, axis=0, keepdims=True)
        dn = dm * g_ref[...]
        da = (r * (dn - ahat * jnp.mean(dn * ahat, axis=-1, keepdims=True))).astype(BF16)
        da_refs[0][0] = da
        lv = lse_ref[...]
        for i, d in enumerate(DILATIONS[1:]):
            p = p_refs[i][...]
            da_refs[i + 1][...] = _dot(p, da).astype(BF16).reshape(d, tm // d, DA)
            a_refs[i][...] = _dot(p, ab).astype(BF16).reshape(d, tm // d, DA)
            lse_refs[i][...] = _perm_f32(p, lv).reshape(d, tm // d, LANES)

    blk = pl.BlockSpec((tm, DA), lambda i: (i, 0))
    vec = pl.BlockSpec((1, DA), lambda i: (0, 0))
    res = [_res_spec(d, DA) for d in DILATIONS]
    outs = _call(body, name="attn_norm_bwd",
                 out_shape=tuple([_sds((d, n // d, DA), BF16) for d in DILATIONS]
                                 + [_sds((d, n // d, DA), BF16) for d in DILATIONS[1:]]
                                 + [_sds((d, n // d, LANES), F32) for d in DILATIONS[1:]] + [_sds((1, DA), F32)]),
                 grid=(n // tm,),
                 in_specs=[blk, blk, pl.BlockSpec((tm, LANES), lambda i: (i, 0)), vec] + _perm_specs(),
                 out_specs=tuple(res + res[1:] + [_res_spec(d, LANES) for d in DILATIONS[1:]] + [vec]),
                 sem=("arbitrary",))(dmix, a, lse, ga, *perms)
    return outs[:npat], outs[npat:2 * npat - 1], outs[2 * npat - 1:3 * npat - 2], outs[-1]


def attn_bwd(qkv, a, da, lse):
    d, ls, _ = qkv.shape
    tq, w, nt = _attn_tiles(ls)
    hb = _attn_heads(ls)

    nhb = NH // hb

    def body(q_ref, k_ref, v_ref, a_ref, da_ref, lse_ref, dq_ref, dk_ref, dv_ref, dk_acc, dv_acc):
        first = (pl.program_id(0) % nhb) * hb
        lane = lax.broadcasted_iota(jnp.int32, (tq, LANES), 1)
        dk_acc[...] = jnp.zeros_like(dk_acc)
        dv_acc[...] = jnp.zeros_like(dv_acc)

        def tile(t, h):
            q0, ks, valid = _band(t, tq, w, ls)
            cols = slice(h * HD, (h + 1) * HD)
            q = q_ref[pl.ds(q0, tq), cols]
            k = k_ref[pl.ds(ks, w), cols]
            v = v_ref[pl.ds(ks, w), cols]
            do = da_ref[pl.ds(q0, tq), cols]
            if nhb == 1:
                lse_h = lse_ref[pl.ds(q0, tq), h:h + 1]
            else:
                lse_h = jnp.sum(jnp.where(lane == first + h, lse_ref[pl.ds(q0, tq), :], 0.0), axis=1, keepdims=True)
            s = jnp.where(valid, _dot_nt(q, k) * SCALE, NEG)
            p = jnp.exp(s - lse_h)
            drow = jnp.sum(do.astype(F32) * a_ref[pl.ds(q0, tq), cols].astype(F32), axis=1, keepdims=True)
            ds = (p * (_dot_nt(do, v) - drow) * SCALE).astype(BF16)
            dv_acc[pl.ds(ks, w), cols] += _dot_tn(p.astype(BF16), do)
            dk_acc[pl.ds(ks, w), cols] += _dot_tn(ds, q)
            dq_ref[pl.ds(q0, tq), cols] = _dot(ds, k).astype(BF16)

        _attn_loop(nt, hb, tile)
        dk_ref[...] = dk_acc[...].astype(BF16)
        dv_ref[...] = dv_acc[...].astype(BF16)

    q, k, v, o = _attn_specs(ls, hb)
    out = _sds((d, ls, DA), BF16)
    return _call(body, name="attn_bwd_d%d" % d, out_shape=(out, out, out), grid=(d * NH // hb,),
                 in_specs=[q, k, v, o, o, _lse_spec(ls, hb)], out_specs=(o, o, o),
                 scratch=[pltpu.VMEM((ls, hb * HD), F32), pltpu.VMEM((ls, hb * HD), F32)], sem=("parallel",))(
        qkv, qkv, qkv, a, da, lse)


GM_TM = 256
INV_SQRT2 = 0.7071067811865476
INV_SQRT2PI = 0.3989422804014327


def _gelu(x):
    return 0.5 * x * (1.0 + lax.erf(x * INV_SQRT2))


def _gelu_grad(x):
    return 0.5 * (1.0 + lax.erf(x * INV_SQRT2)) + x * (INV_SQRT2PI * jnp.exp(-0.5 * x * x))


def _gmlp_forward(up, vp, ln_g, ws_ref, bias):
    u = _gelu(up)
    v = _gelu(vp)
    vc = v - jnp.mean(v, axis=-1, keepdims=True)
    rs = lax.rsqrt(jnp.mean(vc * vc, axis=-1, keepdims=True) + EPS)
    vhat = vc * rs
    vln = (vhat * ln_g).astype(BF16)
    rows = []
    for c in range(GM_TM // CH):
        cols = [_dot(ws_ref[g], vln[c * CH:(c + 1) * CH, g * 128:(g + 1) * 128]) for g in range(NG)]
        rows.append(jnp.concatenate(cols, axis=1) + bias)
    return u, vhat, rs, vln, jnp.concatenate(rows, axis=0)


def _gmlp_specs():
    ublk = pl.BlockSpec((GM_TM, DG), lambda i: (i, 3 * DA // DG))
    vblk = pl.BlockSpec((GM_TM, DG), lambda i: (i, 3 * DA // DG + 1))
    vec = pl.BlockSpec((1, DG), lambda i: (0, 0))
    wsp = pl.BlockSpec((NG, CH, CH), lambda i: (0, 0, 0))
    bsp = pl.BlockSpec((CH, DG), lambda i: (0, 0))
    return ublk, vblk, vec, wsp, bsp


def gmlp_fwd(proj, ln_g, ws, bias, gg, mix):
    n = proj.shape[0]

    def body(up_ref, vp_ref, ln_ref, ws_ref, b_ref, gg_ref, mix_in, mix_ref):
        del mix_in
        u, _, _, _, mixed = _gmlp_forward(up_ref[...].astype(F32), vp_ref[...].astype(F32), ln_ref[...], ws_ref, b_ref[...])
        gout = u * mixed
        r = lax.rsqrt(jnp.mean(gout * gout, axis=-1, keepdims=True) + EPS)
        mix_ref[...] = (gout * r * gg_ref[...]).astype(BF16)

    ublk, vblk, vec, wsp, bsp = _gmlp_specs()
    return _call(body, name="gmlp_fwd", out_shape=_sds((n, D), BF16), grid=(n // GM_TM,),
                 in_specs=[ublk, vblk, vec, wsp, bsp, vec, pl.BlockSpec(memory_space=pl.ANY)],
                 out_specs=pl.BlockSpec((GM_TM, DG), lambda i: (i, DA // DG)), sem=("parallel",), aliases={6: 0})(
        proj, proj, ln_g, ws, bias, gg, mix)


def gmlp_bwd(proj, dmix, ln_g, ws, wst, bias, gg):
    n = proj.shape[0]

    def body(up_ref, vp_ref, dm_ref, ln_ref, ws_ref, wst_ref, b_ref, gg_ref, duv_ref, dln_ref, dws_ref, dbs_ref, dgg_ref,
             db_ref):
        @pl.when(pl.program_id(0) == 0)
        def _():
            dln_ref[...] = jnp.zeros_like(dln_ref)
            dws_ref[...] = jnp.zeros_like(dws_ref)
            db_ref[...] = jnp.zeros_like(db_ref)
            dgg_ref[...] = jnp.zeros_like(dgg_ref)

        up = up_ref[...].astype(F32)
        vp = vp_ref[...].astype(F32)
        ln_g = ln_ref[...]
        u, vhat, rs, vln, mixed = _gmlp_forward(up, vp, ln_g, ws_ref, b_ref[...])
        gout = u * mixed
        r = lax.rsqrt(jnp.mean(gout * gout, axis=-1, keepdims=True) + EPS)
        ghat = gout * r
        dm = dm_ref[...].astype(F32)
        dgg_ref[...] += jnp.sum(dm * ghat, axis=0, keepdims=True)
        dn = dm * gg_ref[...]
        dgout = r * (dn - ghat * jnp.mean(dn * ghat, axis=-1, keepdims=True))
        du = dgout * mixed
        dmixed = dgout * u
        dmb = dmixed.astype(BF16)
        rows = []
        for c in range(GM_TM // CH):
            rsl = slice(c * CH, (c + 1) * CH)
            db_ref[...] += dmixed[rsl, :]
            cols = []
            for g in range(NG):
                csl = slice(g * 128, (g + 1) * 128)
                dws_ref[g] += _dot_nt(dmb[rsl, csl], vln[rsl, csl])
                cols.append(_dot(wst_ref[g], dmb[rsl, csl]))
            rows.append(jnp.concatenate(cols, axis=1))
        dvln = jnp.concatenate(rows, axis=0)
        dln_ref[...] += jnp.sum(dvln * vhat, axis=0, keepdims=True)
        dvh = dvln * ln_g
        dv = rs * (dvh - jnp.mean(dvh, axis=-1, keepdims=True) - vhat * jnp.mean(dvh * vhat, axis=-1, keepdims=True))
        duv_ref[:, 0:DG] = (du * _gelu_grad(up)).astype(BF16)
        duv_ref[:, DG:] = (dv * _gelu_grad(vp)).astype(BF16)

        @pl.when(pl.program_id(0) == pl.num_programs(0) - 1)
        def _():
            for g in range(NG):
                dbs_ref[g:g + 1, :] = jnp.sum(jnp.transpose(db_ref[:, g * 128:(g + 1) * 128]), axis=0, keepdims=True)

    ublk, vblk, vec, wsp, bsp = _gmlp_specs()
    return _call(body, name="gmlp_bwd",
                 out_shape=(_sds((n, 2 * DG), BF16), _sds((1, DG), F32), _sds((NG, CH, CH), F32), _sds((NG, CH), F32),
                            _sds((1, DG), F32)),
                 grid=(n // GM_TM,),
                 in_specs=[ublk, vblk, pl.BlockSpec((GM_TM, DG), lambda i: (i, DA // DG)), vec, wsp, wsp, bsp, vec],
                 out_specs=(pl.BlockSpec((GM_TM, 2 * DG), lambda i: (i, 0)), vec, wsp,
                            pl.BlockSpec((NG, CH), lambda i: (0, 0)), vec),
                 scratch=[pltpu.VMEM((CH, DG), F32)], sem=("arbitrary",))(
        proj, proj, dmix, ln_g, ws, wst, bias, gg)


def _row(v):
    return v.reshape(1, -1)


def local_step(x, target, small, big):
    tables = step_tables()
    saved = []
    for l in range(NL):
        x_mid, sv = mixer_fwd(x, small, l, big[l], tables)
        x, sv2 = ffn_fwd(x_mid, small, l, big[l])
        saved.append({**sv, **sv2})
    loss8, dx, dxb, dfinal = loss_head(x, _row(small["final_g"]), target)
    gs = {k: [None] * NL for k in SMALL[:-1]}
    gbig = [None] * NL
    for l in reversed(range(NL)):
        dx, dxb, gs1, gb1 = ffn_bwd(dx, dxb, small, l, big[l], saved[l])
        dx, dxb, gs2, gb2 = mixer_bwd(dx, dxb, small, l, big[l], saved[l], tables)
        gbig[l] = {**gb1, **gb2}
        for k, g in {**gs1, **gs2}.items():
            gs[k][l] = g
    return loss8, dx, gs, dfinal, gbig


def step_tables():
    perms = [perm_matrix(d) for d in DILATIONS[1:]]
    return rope_tables(T) + (perms, [p.T for p in perms])


def mixer_fwd(x, small, l, w, tables, after=None):
    cos2, sin2, perms, perms_t = tables
    ws_b = small["w_spatial"][l].astype(BF16)
    bias = jnp.repeat(small["b_spatial"][l].T, 128, axis=1)
    h = rms_fwd(x, _row(small["norm1_g"][l]), after)
    proj = mm_cols(h, w["win"])
    qkvs = qkv_split(proj, cos2, sin2, perms)
    outs = [attn_fwd(qkv) for qkv in qkvs]
    a, lse, mix = attn_merge([o for o, _ in outs], [s for _, s in outs], _row(small["mix_norm_attn_g"][l]), perms_t)
    mix = gmlp_fwd(proj, _row(small["gmlp_ln_g"][l]), ws_b, bias, _row(small["mix_norm_gmlp_g"][l]), mix)
    x_mid = mm_rows_res(mix, w["wout"], x)
    return x_mid, dict(x=x, h=h, proj=proj, qkvs=qkvs, a=a, lse=lse, mix=mix, ws_b=ws_b, bias=bias)


def ffn_fwd(x_mid, small, l, w, after=None):
    h2 = rms_fwd(x_mid, _row(small["norm2_g"][l]), after)
    gate, up, ff = mm_gateup(h2, w["wg"], w["wu"])
    x_out = mm_rows_res(ff, w["wd"], x_mid)
    return x_out, dict(x_mid=x_mid, h2=h2, gate=gate, up=up, ff=ff)


def ffn_bwd(dx, dxb, small, l, w, sv, after=None):
    gs = {}
    dgate, dup = mm_dff(dxb, w["wd"], sv["gate"], sv["up"], tk=FF // NCHIP, after=after)
    g_wd = mm_tn(sv["ff"], dxb, D // 2, False)
    dh2 = mm_nt_cols([dgate, dup], [w["wg"], w["wu"]])
    g_wg = mm_tn(sv["h2"], dgate, FF // NCHIP, True)
    g_wu = mm_tn(sv["h2"], dup, FF // NCHIP, True)
    dx, dxb, gs["norm2_g"] = rms_bwd(dh2, sv["x_mid"], _row(small["norm2_g"][l]), dx)
    return dx, dxb, gs, dict(wg=g_wg, wu=g_wu, wd=g_wd.reshape(NCHIP, FF // NCHIP, D))


def mixer_bwd(dx, dxb, small, l, w, sv, tables, after=None):
    cos2, sin2, perms, perms_t = tables
    gs = {}
    dmix = mm_nt_rows(dxb, w["wout"], tk=D // 2, after=after)
    g_wout = mm_tn(sv["mix"], dxb, D // 2, False)
    das, a_res, lse_res, gs["mix_norm_attn_g"] = attn_norm_bwd(dmix, sv["a"], sv["lse"],
                                                                _row(small["mix_norm_attn_g"][l]), perms)
    a_all = [sv["a"].reshape((1,) + sv["a"].shape)] + list(a_res)
    lse_all = [sv["lse"].reshape((1,) + sv["lse"].shape)] + list(lse_res)
    parts = [attn_bwd(*operands) for operands in zip(sv["qkvs"], a_all, das, lse_all)]
    wst = jnp.swapaxes(small["w_spatial"][l], 1, 2).astype(BF16)
    duv, gs["gmlp_ln_g"], gs["w_spatial"], gs["b_spatial"], gs["mix_norm_gmlp_g"] = gmlp_bwd(
        sv["proj"], dmix, _row(small["gmlp_ln_g"][l]), sv["ws_b"], wst, sv["bias"], _row(small["mix_norm_gmlp_g"][l]))
    dproj = dproj_assemble([p[0] for p in parts], [p[1] for p in parts], [p[2] for p in parts], duv, cos2, sin2,
                           perms_t)
    dh = mm_nt_cols([dproj], [w["win"]])
    g_win = mm_tn(sv["h"], dproj, DIN // NCHIP, True)
    dx, dxb, gs["norm1_g"] = rms_bwd(dh, sv["x"], _row(small["norm1_g"][l]), dx)
    return dx, dxb, gs, dict(win=g_win, wout=g_wout.reshape(NCHIP, D // NCHIP, D))


KINDS = ("win", "wout", "wg", "wu", "wd")
GROUPS = (("win", "wout"), ("wg", "wu", "wd"))
ANY = pl.BlockSpec(memory_space=pl.ANY)


def _place():
    return lax.axis_index("x"), lax.axis_index("y"), lax.axis_index("c")


def _other_chips(x, y):
    return [(1 - x, y), (x, 1 - y), (1 - x, 1 - y)]


def _remote(src, dst, send_sem, recv_sem, to):
    return pltpu.make_async_remote_copy(src_ref=src, dst_ref=dst, send_sem=send_sem, recv_sem=recv_sem,
                                        device_id=to, device_id_type=MESH)


def _hbm_call(body, *, name, n_in, out_shape, scratch, in_place=False):
    return pl.pallas_call(body, name=name, out_shape=tuple(out_shape), in_specs=[ANY] * n_in,
                          out_specs=tuple(ANY for _ in out_shape), scratch_shapes=list(scratch),
                          input_output_aliases={k: k for k in range(n_in)} if in_place else {},
                          compiler_params=pltpu.CompilerParams(vmem_limit_bytes=VMEM_LIMIT))


def place_vector():
    x, y, c = _place()
    return jnp.stack([c, 2 * x + y] + [2 * cx + cy for cx, cy in _other_chips(x, y)]).astype(jnp.int32)


PUSH_ROWS = 512
PUSH_SLOTS = 3


def _push_tiles(src, dst, buf, load_sem, send_sem, recv_sem, to, src_at, dst_at, nt):
    tr = buf.shape[1]
    n = len(src_at) * nt

    def pick(vals, seg):
        out = vals[0]
        for q in range(1, len(vals)):
            out = jnp.where(seg == q, vals[q], out)
        return out

    def tile(ref, at, t):
        seg = t // nt
        row = pl.multiple_of(pick([a[1] for a in at], seg) + (t - seg * nt) * tr, 16)
        return ref.at[pick([a[0] for a in at], seg), pl.ds(row, tr), :]

    def load(t):
        slot = t % PUSH_SLOTS
        return pltpu.make_async_copy(tile(src, src_at, t), buf.at[slot], load_sem.at[slot])

    def send(t):
        slot = t % PUSH_SLOTS
        return _remote(buf.at[slot], tile(dst, dst_at, t), send_sem.at[slot], recv_sem, to)

    load(0).start()

    def step(t, carry):
        load(t).wait()
        send(t).start()

        @pl.when(t >= PUSH_SLOTS - 1)
        def _():
            send(t - (PUSH_SLOTS - 1)).wait_send()

        @pl.when(t + 1 < n)
        def _():
            load(t + 1).start()

        return carry

    lax.fori_loop(0, n, step, 0)
    for t in range(max(n - (PUSH_SLOTS - 1), 0), n):
        send(t).wait_send()


def _await_tiles(dst, send_sem, recv_sem, to, nseg, rows):
    whole = dst.at[pl.ds(0, nseg), pl.ds(0, rows), :]
    _remote(whole, whole, send_sem.at[0], recv_sem, to).wait_recv()


def _push_rows(seg_rows):
    return _row_tile(seg_rows, PUSH_ROWS)


def _push_scratch(arrs, seg_rows):
    return ([pltpu.VMEM((PUSH_SLOTS, _push_rows(r), a.shape[-1]), a.dtype) for a, r in zip(arrs, seg_rows)]
            + [pltpu.SemaphoreType.DMA((len(arrs), PUSH_SLOTS)), pltpu.SemaphoreType.DMA((len(arrs), PUSH_SLOTS)),
               pltpu.SemaphoreType.DMA((len(arrs),))])


def cast_into(w, l, pos, after=None):
    _, r, cols = w.shape
    tr = _row_tile(r)
    extra, extra_specs = _behind(after)

    def body(pos_ref, w_ref, *rest):
        del pos_ref
        rest[-1][...] = w_ref[...].astype(BF16)

    return _call(body, name="cast_into", out_shape=_sds((NCHIP, r, cols), BF16), grid=(r // tr,),
                 in_specs=[pl.BlockSpec((None, tr, cols), lambda i, pos: (l, i, 0))] + extra_specs,
                 out_specs=pl.BlockSpec((None, tr, cols), lambda i, pos: (pos[1], i, 0)),
                 sem=("parallel",), prefetch=1)(pos, w, *extra)


HBM = pl.BlockSpec(memory_space=pltpu.HBM)
SEM = pl.BlockSpec(memory_space=pltpu.SEMAPHORE)
EFFECT = pltpu.SideEffectType.DATAFLOW_SIDE_EFFECTING


def _in_hbm(a):
    return pltpu.with_memory_space_constraint(a, pltpu.HBM)


def _gather_copies(bufs, send, recv):
    x, y, c = _place()
    chips = _other_chips(x, y)

    def half(k, chip):
        hr = bufs[k].shape[1] // 2
        return bufs[k].at[chip, pl.ds(c * hr, hr), :]

    out, back = [], []
    for k in range(len(bufs)):
        for j, (cx, cy) in enumerate(chips):
            i = 3 * k + j
            out.append(_remote(half(k, 2 * x + y), half(k, 2 * x + y), send.at[i], recv.at[i], (cx, cy, c)))
            back.append(_remote(half(k, 2 * x + y), half(k, 2 * cx + cy), send.at[i], recv.at[i], (cx, cy, c)))
    return out, back


def gather_start(fulls):
    nk = len(fulls)

    def body(*refs):
        send, recv = refs[nk], refs[nk + 1]
        bufs, token = refs[nk + 2:2 * nk + 2], refs[2 * nk + 2]
        for cp in _gather_copies(bufs, send, recv)[0]:
            cp.start()
        token[...] = jnp.zeros_like(token)

    res = pl.pallas_call(
        body, name="gather_start",
        out_shape=(pltpu.SemaphoreType.DMA((3 * nk,)), pltpu.SemaphoreType.DMA((3 * nk,)),
                   *[pltpu.HBM(f.shape, f.dtype) for f in fulls], _sds((8, LANES), F32)),
        in_specs=[HBM] * nk, out_specs=(SEM, SEM, *[HBM] * nk, pl.BlockSpec(memory_space=pltpu.VMEM)),
        input_output_aliases={k: 2 + k for k in range(nk)},
        compiler_params=pltpu.CompilerParams(has_side_effects=EFFECT))(*[_in_hbm(f) for f in fulls])
    return res[0], res[1], list(res[2:2 + nk]), res[2 + nk]


def gather_wait(send, recv, fulls, after):
    nk = len(fulls)

    def body(*refs):
        bufs, send_ref, recv_ref = refs[:nk], refs[nk], refs[nk + 1]
        for cp in _gather_copies(bufs, send_ref, recv_ref)[1]:
            cp.wait_send()
            cp.wait_recv()

    return list(pl.pallas_call(
        body, name="gather_wait", out_shape=tuple(pltpu.HBM(f.shape, f.dtype) for f in fulls),
        in_specs=[HBM] * nk + [SEM, SEM, ANY], out_specs=tuple([HBM] * nk),
        input_output_aliases={k: k for k in range(nk)},
        compiler_params=pltpu.CompilerParams(has_side_effects=EFFECT))(*fulls, send, recv, after))


def pair_forward(fulls):
    nk = len(fulls)

    def body(*refs):
        bufs = refs[nk:2 * nk]
        stage = refs[2 * nk:3 * nk]
        load_sem, send_sem, recv_sem = refs[3 * nk:]
        x, y, c = _place()
        sib = (x, y, 1 - c)
        chips = [2 * cx + cy for cx, cy in _other_chips(x, y)]
        for k in range(nk):
            hr = bufs[k].shape[1] // 2
            at = [(chip, c * hr) for chip in chips]
            _push_tiles(bufs[k], bufs[k], stage[k], load_sem.at[k], send_sem.at[k], recv_sem.at[k], sib, at, at,
                        hr // _push_rows(hr))
        for k in range(nk):
            _await_tiles(bufs[k], send_sem.at[k], recv_sem.at[k], sib, 3, bufs[k].shape[1] // 2)

    return _hbm_call(body, name="pair_forward", n_in=nk, out_shape=[_sds(f.shape, f.dtype) for f in fulls], in_place=True,
                     scratch=_push_scratch(fulls, [f.shape[1] // 2 for f in fulls]))(*fulls)


N_PEERS = 7
PEER_FLIPS = [(fx, fy, fc) for fx in (0, 1) for fy in (0, 1) for fc in (0, 1)][1:]


def _chip_copies(ps, lands, send, recv):
    x, y, c = _place()
    cps = []
    for k in range(len(ps)):
        hr = ps[k].shape[1] // 2
        for r, (fx, fy, fc) in enumerate(PEER_FLIPS):
            px, py, pc = x ^ fx, y ^ fy, c ^ fc
            cps.append(_remote(ps[k].at[2 * px + py, pl.ds(pc * hr, hr), :], lands[k].at[r],
                               send.at[N_PEERS * k + r], recv.at[N_PEERS * k + r], (px, py, pc)))
    return cps


def chip_exchange_start(ps, after=None):
    nk = len(ps)
    lands = [lax.empty((N_PEERS, p.shape[1] // 2, p.shape[2]), p.dtype) for p in ps]
    extra, extra_specs = _behind(after)

    def body(*refs):
        refs = refs[2 * nk + len(extra):]
        send, recv = refs[0], refs[1]
        srcs, dsts, token = refs[2:nk + 2], refs[nk + 2:2 * nk + 2], refs[2 * nk + 2]
        for cp in _chip_copies(srcs, dsts, send, recv):
            cp.start()
        token[...] = jnp.zeros_like(token)

    res = pl.pallas_call(
        body, name="chip_exchange_start",
        out_shape=(pltpu.SemaphoreType.DMA((N_PEERS * nk,)), pltpu.SemaphoreType.DMA((N_PEERS * nk,)),
                   *[pltpu.HBM(a.shape, a.dtype) for a in ps + lands], _sds((8, LANES), F32)),
        in_specs=[HBM] * (2 * nk) + extra_specs,
        out_specs=(SEM, SEM, *[HBM] * (2 * nk), pl.BlockSpec(memory_space=pltpu.VMEM)),
        input_output_aliases={k: 2 + k for k in range(2 * nk)},
        compiler_params=pltpu.CompilerParams(has_side_effects=EFFECT))(*[_in_hbm(a) for a in ps + lands], *extra)
    return res[0], res[1], list(res[2:2 + nk]), list(res[2 + nk:2 + 2 * nk]), res[2 + 2 * nk]


def chip_exchange_wait(send, recv, ps, lands, after):
    nk = len(ps)

    def body(*refs):
        srcs, dsts, send_ref, recv_ref = refs[:nk], refs[nk:2 * nk], refs[2 * nk], refs[2 * nk + 1]
        for cp in _chip_copies(srcs, dsts, send_ref, recv_ref):
            cp.wait_send()
            cp.wait_recv()

    res = pl.pallas_call(
        body, name="chip_exchange_wait", out_shape=tuple(pltpu.HBM(a.shape, a.dtype) for a in ps + lands),
        in_specs=[HBM] * (2 * nk) + [SEM, SEM, ANY], out_specs=tuple([HBM] * (2 * nk)),
        input_output_aliases={k: k for k in range(2 * nk)},
        compiler_params=pltpu.CompilerParams(has_side_effects=EFFECT))(*ps, *lands, send, recv, after)
    return list(res[:nk]), list(res[nk:])


def rs_pair_share(fulls):
    nk = len(fulls)

    def body(*refs):
        bufs = refs[nk:2 * nk]
        stage = refs[2 * nk:3 * nk]
        load_sem, send_sem, recv_sem = refs[3 * nk:]
        x, y, c = _place()
        sib = (x, y, 1 - c)
        for k in range(nk):
            hr = bufs[k].shape[1]
            _push_tiles(bufs[k], bufs[k], stage[k], load_sem.at[k], send_sem.at[k], recv_sem.at[k], sib,
                        [(c, 0)], [(c, 0)], hr // _push_rows(hr))
        for k in range(nk):
            _await_tiles(bufs[k], send_sem.at[k], recv_sem.at[k], sib, 1, bufs[k].shape[1])

    return _hbm_call(body, name="rs_pair_share", n_in=nk, out_shape=[_sds(f.shape, f.dtype) for f in fulls], in_place=True,
                     scratch=_push_scratch(fulls, [f.shape[1] for f in fulls]))(*fulls)


N_DEV = 8


def allreduce_small(buf, after=None):
    rows = buf.shape[0]
    rp = rows // N_DEV
    extra, extra_specs = _behind(after)

    def body(in_ref, *rest):
        out_ref, land_ref, send, recv = rest[-4:]
        x, y, c = _place()
        mine = pl.ds(pl.multiple_of((4 * x + 2 * y + c) * rp, 8), rp)
        peers = [(x ^ fx, y ^ fy, c ^ fc) for fx in (0, 1) for fy in (0, 1) for fc in (0, 1)][1:]
        block = [pl.ds(pl.multiple_of((4 * px + 2 * py + pc) * rp, 8), rp) for px, py, pc in peers]
        scatter = [_remote(in_ref.at[block[r], :], land_ref.at[r], send.at[r], recv.at[r], peers[r]) for r in range(7)]
        for cp in scatter:
            cp.start()
        for cp in scatter:
            cp.wait()
        acc = in_ref[mine, :]
        for r in range(7):
            acc = acc + land_ref[r]
        out_ref[mine, :] = acc
        spread = [_remote(out_ref.at[mine, :], out_ref.at[mine, :], send.at[7 + r], recv.at[7 + r], peers[r]) for r in range(7)]
        for cp in spread:
            cp.start()
        for r in range(7):
            _remote(out_ref.at[block[r], :], out_ref.at[block[r], :], send.at[7 + r], recv.at[7 + r], peers[r]).wait_recv()
        for cp in spread:
            cp.wait_send()

    vm = pl.BlockSpec(memory_space=pltpu.VMEM)
    return pl.pallas_call(body, name="allreduce_small", out_shape=_sds(buf.shape, F32), in_specs=[vm] + extra_specs,
                          out_specs=vm,
                          scratch_shapes=[pltpu.VMEM((7, rp, LANES), F32), pltpu.SemaphoreType.DMA((14,)),
                                          pltpu.SemaphoreType.DMA((14,))])(buf, *extra)


def _row_tile(rows, cap=512):
    return max(t for t in range(16, cap + 1, 16) if rows % t == 0)


def add_slots(g, got, pos):
    _, r, cols = g.shape
    hr = r // 2
    tr = _row_tile(hr, 256)
    nt = hr // tr

    def body(pos_ref, o_ref, *rest):
        del pos_ref
        acc = o_ref[...].astype(F32)
        for ref in rest[:N_PEERS]:
            acc = acc + ref[...].astype(F32)
        rest[N_PEERS][...] = acc.astype(BF16)

    slots = [pl.BlockSpec((None, tr, cols), functools.partial(lambda j, i, pos: (j, i, 0), j)) for j in range(N_PEERS)]
    return _call(body, name="add_slots", out_shape=_sds((2, hr, cols), BF16), grid=(nt,),
                 in_specs=[pl.BlockSpec((None, tr, cols), lambda i, pos: (pos[1], pos[0] * nt + i, 0))] + slots,
                 out_specs=pl.BlockSpec((None, tr, cols), lambda i, pos: (pos[0], i, 0)),
                 sem=("parallel",), prefetch=1)(pos, g, *[got] * N_PEERS)


def _adamw_math(w, g, m, v):
    nm = B1 * m + (1.0 - B1) * g
    nv = B2 * v + (1.0 - B2) * (g * g)
    m_hat = nm / (1.0 - B1 ** STEP)
    v_hat = nv / (1.0 - B2 ** STEP)
    return -LR * (m_hat / (jnp.sqrt(v_hat) + AEPS) + WD * w), nm, nv


def adamw_layer(w, g, m, v, l, prev):
    nl, r, cols = w.shape
    tr = _row_tile(r, 256)

    def body(w_ref, g_ref, m_ref, v_ref, *rest):
        go_ref, d_ref, nm_ref, nv_ref = rest[-4:]
        gv = g_ref[...].astype(F32)
        d, nm, nv = _adamw_math(w_ref[...], gv, m_ref[...], v_ref[...])
        go_ref[...] = gv
        d_ref[...] = d
        nm_ref[...] = nm
        nv_ref[...] = nv

    lay = pl.BlockSpec((None, tr, cols), lambda i: (l, i, 0))
    o = _sds((nl, r, cols), F32)
    extra = [] if prev is None else list(prev)
    return _call(body, name="adamw_layer", out_shape=(o, o, o, o), grid=(r // tr,),
                 in_specs=[lay, pl.BlockSpec((tr, cols), lambda i: (i, 0)), lay, lay] + [ANY] * len(extra),
                 out_specs=(lay, lay, lay, lay), sem=("parallel",),
                 aliases={4 + j: j for j in range(len(extra))})(w, g, m, v, *extra)


def adamw(w, g, m, v):
    shape = w.shape
    cols = shape[-1]
    rows = w.size // cols
    tr = _row_tile(rows, 256)

    def body(w_ref, g_ref, m_ref, v_ref, d_ref, nm_ref, nv_ref):
        d_ref[...], nm_ref[...], nv_ref[...] = _adamw_math(w_ref[...], g_ref[...], m_ref[...], v_ref[...])

    blk = pl.BlockSpec((tr, cols), lambda i: (i, 0))
    o = _sds((rows, cols), F32)
    outs = _call(body, name="adamw", out_shape=(o, o, o), grid=(rows // tr,), in_specs=[blk] * 4, out_specs=(blk, blk, blk),
                 sem=("parallel",))(*[t.reshape(rows, cols) for t in (w, g, m, v)])
    return [t.reshape(shape) for t in outs]


SMALL = ("norm1_g", "gmlp_ln_g", "w_spatial", "b_spatial", "mix_norm_attn_g", "mix_norm_gmlp_g", "norm2_g", "final_g")
WEIGHTS = ("norm1_g", "w_in", "gmlp_ln_g", "w_spatial", "b_spatial", "mix_norm_attn_g", "mix_norm_gmlp_g", "w_out",
           "norm2_g", "w_gate", "w_up", "w_down", "final_g")
BIG = dict(win="w_in", wout="w_out", wg="w_gate", wu="w_up", wd="w_down")


def _pack(parts):
    flat = jnp.concatenate([parts[n].reshape(-1) for n in SMALL])
    rows = -(-flat.shape[0] // (LANES * 8 * N_DEV)) * 8 * N_DEV
    return jnp.pad(flat, (0, rows * LANES - flat.shape[0])).reshape(rows, LANES)


def _unpack(buf, like):
    flat = buf.reshape(-1)
    out, at = {}, 0
    for n in SMALL:
        out[n] = flat[at:at + like[n].size].reshape(like[n].shape)
        at += like[n].size
    return out


def kernel(x, norm1_g, w_in, gmlp_ln_g, w_spatial, b_spatial, mix_norm_attn_g, mix_norm_gmlp_g, w_out, norm2_g, w_gate, w_up, w_down, final_g, loss_target, m_norm1_g, m_w_in, m_gmlp_ln_g, m_w_spatial, m_b_spatial, m_mix_norm_attn_g, m_mix_norm_gmlp_g, m_w_out, m_norm2_g, m_w_gate, m_w_up, m_w_down, m_final_g, v_norm1_g, v_w_in, v_gmlp_ln_g, v_w_spatial, v_b_spatial, v_mix_norm_attn_g, v_mix_norm_gmlp_g, v_w_out, v_norm2_g, v_w_gate, v_w_up, v_w_down, v_final_g):
    w = dict(norm1_g=norm1_g, w_in=w_in, gmlp_ln_g=gmlp_ln_g, w_spatial=w_spatial, b_spatial=b_spatial,
             mix_norm_attn_g=mix_norm_attn_g, mix_norm_gmlp_g=mix_norm_gmlp_g, w_out=w_out, norm2_g=norm2_g,
             w_gate=w_gate, w_up=w_up, w_down=w_down, final_g=final_g)
    m = dict(norm1_g=m_norm1_g, w_in=m_w_in, gmlp_ln_g=m_gmlp_ln_g, w_spatial=m_w_spatial, b_spatial=m_b_spatial,
             mix_norm_attn_g=m_mix_norm_attn_g, mix_norm_gmlp_g=m_mix_norm_gmlp_g, w_out=m_w_out, norm2_g=m_norm2_g,
             w_gate=m_w_gate, w_up=m_w_up, w_down=m_w_down, final_g=m_final_g)
    v = dict(norm1_g=v_norm1_g, w_in=v_w_in, gmlp_ln_g=v_gmlp_ln_g, w_spatial=v_w_spatial, b_spatial=v_b_spatial,
             mix_norm_attn_g=v_mix_norm_attn_g, mix_norm_gmlp_g=v_mix_norm_gmlp_g, w_out=v_w_out, norm2_g=v_norm2_g,
             w_gate=v_w_gate, w_up=v_w_up, w_down=v_w_down, final_g=v_final_g)

    pos = place_vector()
    small = {n: w[n] for n in SMALL}
    tables = step_tables()

    def start_gathers(l, after):
        flights = []
        for kinds in GROUPS:
            flights.append(gather_start([cast_into(w[BIG[k]], l, pos, after) for k in kinds]))
            after = flights[-1][3]
        return flights, after

    def arrive(flight, kinds, after):
        fulls = pair_forward(gather_wait(flight[0], flight[1], flight[2], after))
        big = dict(zip(kinds, fulls))
        if "wout" in big:
            big["wout"] = big["wout"].reshape(D, D)
        if "wd" in big:
            big["wd"] = big["wd"].reshape(FF, D)
        return big

    act = x[0]
    flights, token = start_gathers(0, None)
    big, saved = [], []
    for l in range(NL):
        mine, after_mixer = flights, token
        if l + 1 < NL:
            flights, token = start_gathers(l + 1, act if l else token)
            after_mixer = token
        wl = arrive(mine[0], GROUPS[0], act if l else mine[0][3])
        x_mid, sv = mixer_fwd(act, small, l, wl, tables, after_mixer)
        wl.update(arrive(mine[1], GROUPS[1], x_mid))
        act, sv2 = ffn_fwd(x_mid, small, l, wl)
        big.append(wl)
        saved.append({**sv, **sv2})
    loss8, dx, dxb, dfinal = loss_head(act, _row(small["final_g"]), loss_target[0])
    loss = lax.psum(loss8[0, 0], ("x", "y", "c"))

    grads, delta, new_m, new_v = {}, {}, {}, {}
    stacked = {k: None for k in KINDS}
    gs = {k: [None] * NL for k in SMALL[:-1]}

    def send_off(gbig, kinds, l, after=None):
        return chip_exchange_start([gbig[k] for k in kinds], after) + (kinds, l)

    def finish(flight, after):
        partial_grads, lands = chip_exchange_wait(*flight[:4], after)
        halves = [add_slots(g, got, pos) for g, got in zip(partial_grads, lands)]
        for k, full in zip(flight[5], rs_pair_share(halves)):
            n = BIG[k]
            stacked[k] = adamw_layer(w[n], full.reshape(w[n].shape[1:]), m[n], v[n], flight[6], stacked[k])
        return stacked[flight[5][-1]][1]

    pending, token = [], None
    for l in reversed(range(NL)):
        dx, dxb, gs1, gbig = ffn_bwd(dx, dxb, small, l, big[l], saved[l], token)
        ffn_flight = send_off(gbig, GROUPS[1], l)
        for flight in pending:
            finish(flight, ffn_flight[4])
        dx, dxb, gs2, gbig = mixer_bwd(dx, dxb, small, l, big[l], saved[l], tables, ffn_flight[4])
        for k, g in {**gs1, **gs2}.items():
            gs[k][l] = g
        if l:
            mixer_flight = send_off(gbig, GROUPS[0], l)
            pending, token = [ffn_flight, mixer_flight], mixer_flight[4]
    part = {n: jnp.stack(gs[n]).reshape(w[n].shape) for n in SMALL[:-1]}
    part["final_g"] = dfinal.reshape(w["final_g"].shape)
    small_sum = allreduce_small(_pack(part), [gbig[k] for k in GROUPS[0]])
    mixer_flight = send_off(gbig, GROUPS[0], 0, small_sum)
    token = mixer_flight[4]
    for flight in (ffn_flight, mixer_flight):
        token = finish(flight, token)
    for k in KINDS:
        grads[BIG[k]], delta[BIG[k]], new_m[BIG[k]], new_v[BIG[k]] = stacked[k]
    grads.update(_unpack(small_sum, part))

    sm = adamw(_pack({n: w[n] for n in SMALL}), _pack({n: grads[n] for n in SMALL}), _pack({n: m[n] for n in SMALL}),
               _pack({n: v[n] for n in SMALL}))
    for res, packed in zip((delta, new_m, new_v), sm):
        res.update(_unpack(packed, part))

    return (loss, dx.reshape(x.shape), *[grads[n] for n in WEIGHTS], *[delta[n] for n in WEIGHTS],
            *[new_m[n] for n in WEIGHTS], *[new_v[n] for n in WEIGHTS])
```
